```python
import math
import jax, jax.numpy as jnp
from jax import lax
import numpy as np

D_MODEL = 1024
BATCH = 8
SEQ = 2048
DEPTH = 2
DEC_BATCH = 128
DEC_SEQ = 8
PAST_LEN = 16384
PAGE_SIZE = 128

GLA_H = 4
GLA_DK = 32
GLA_DV = 64
GLA_WIDTH = GLA_H * GLA_DV
GLA_GATE_RANK = 16
GLA_GATE_TEMP = 16.0
GLA_CHUNK = 16
RET_H = 4
RET_DK = 64
RET_DV = 64
RET_WIDTH = RET_H * RET_DV
RET_CHUNK = 64
ROPE_BASE = 10000.0
SSD_H = 8
SSD_P = 64
SSD_WIDTH = SSD_H * SSD_P
SSD_G = 2
SSD_N = 64
SSD_CONV_W = 4
SSD_CONV_DIM = SSD_WIDTH + 2 * SSD_G * SSD_N
SSD_CHUNK = 64
D_MIX = GLA_WIDTH + RET_WIDTH + SSD_WIDTH
MOE_GROUPS = 4
MOE_PER_GROUP = 4
MOE_EXPERTS = MOE_GROUPS * MOE_PER_GROUP
MOE_TOPK = 2
MOE_FF = 256
ALPHA = (2 * DEPTH) ** 0.25
BETA = (8 * DEPTH) ** -0.25
EPS = 1e-5
IN_WIDTHS = (GLA_H * GLA_DK, GLA_H * GLA_DK, GLA_WIDTH, GLA_GATE_RANK, GLA_WIDTH,
             RET_H * RET_DK, RET_H * RET_DK, RET_WIDTH, RET_WIDTH,
             SSD_WIDTH, SSD_CONV_DIM, SSD_H)
N_IN = sum(IN_WIDTHS)

kernel_name = 'hymba_gla_ret_ssd_hmoe_step'

F32 = jnp.float32


def layer_norm(x, g, b):
    xf = x.astype(F32)
    mu = jnp.mean(xf, -1, keepdims=True)
    var = jnp.mean(jnp.square(xf - mu), -1, keepdims=True)
    return (xf - mu) * lax.rsqrt(var + EPS) * g.astype(F32) + b.astype(F32)


def rms_norm(x, g):
    xf = x.astype(F32)
    return xf * lax.rsqrt(jnp.mean(jnp.square(xf), -1, keepdims=True) + EPS) * g.astype(F32)


def head_group_norm(x, g):
    xf = x.astype(F32)
    mu = jnp.mean(xf, -1, keepdims=True)
    var = jnp.mean(jnp.square(xf - mu), -1, keepdims=True)
    y = (xf - mu) * lax.rsqrt(var + EPS)
    return y.reshape(x.shape[0], x.shape[1], -1) * g.astype(F32)


def rope(x, pos):
    half = x.shape[-1] // 2
    inv = ROPE_BASE ** (-jnp.arange(half, dtype=F32) / half)
    ang = pos.astype(F32)[:, None] * inv[None, :]
    cos = jnp.cos(ang)[None, :, None, :]
    sin = jnp.sin(ang)[None, :, None, :]
    x1 = x[..., :half].astype(F32)
    x2 = x[..., half:].astype(F32)
    return jnp.concatenate([x1 * cos - x2 * sin, x1 * sin + x2 * cos], -1)


def chunked_linear_recurrence(q, k, v, log_a, s0, chunk):
    B, T, H, K = q.shape
    V = v.shape[-1]
    c = math.gcd(chunk, T)
    n = T // c
    q = q.astype(F32).reshape(B, n, c, H, K)
    k = k.astype(F32).reshape(B, n, c, H, K)
    v = v.astype(F32).reshape(B, n, c, H, V)
    g = jnp.cumsum(log_a.astype(F32).reshape(B, n, c, H, -1), axis=2)
    causal = jnp.tril(jnp.ones((c, c), dtype=bool))[None, None, :, :, None, None]
    diff = g[:, :, :, None] - g[:, :, None, :]
    decay = jnp.exp(jnp.where(causal, diff, -jnp.inf))
    if g.shape[-1] == 1:
        scores = jnp.einsum('bnihk,bnjhk->bnijh', q, k) * decay[..., 0]
    else:
        scores = jnp.einsum('bnihk,bnjhk,bnijhk->bnijh', q, k, decay)
    o_intra = jnp.einsum('bnijh,bnjhv->bnihv', scores, v)
    g_last = g[:, :, -1]
    u = jnp.einsum('bnjhk,bnjhv->bnhkv', k * jnp.exp(g_last[:, :, None] - g), v)
    a_chunk = jnp.exp(g_last)

    def step(s, inp):
        a_c, u_c = inp
        return a_c[..., None] * s + u_c, s

    s_final, s_starts = lax.scan(step, s0.astype(F32),
                                 (jnp.moveaxis(a_chunk, 1, 0), jnp.moveaxis(u, 1, 0)))
    s_starts = jnp.moveaxis(s_starts, 0, 1)
    o_inter = jnp.einsum('bnihk,bnhkv->bnihv', q * jnp.exp(g), s_starts)
    return (o_intra + o_inter).reshape(B, T, H, V), s_final


def causal_dwconv(x, prev, w, b):
    T = x.shape[1]
    xx = jnp.concatenate([prev.astype(x.dtype), x], axis=1)
    out = b + sum(w[i] * xx[:, i:i + T] for i in range(SSD_CONV_W))
    return out, xx[:, -(SSD_CONV_W - 1):]


def mixer(h, pos0, s_gla, s_ret, s_ssd, s_conv, l, p):
    B, T, _ = h.shape
    splits = [int(s) for s in np.cumsum(IN_WIDTHS)[:-1]]
    proj = h @ p['w_in'][l]
    gq, gk, gv, ga, gr, rq, rk, rv, rg, sz, sxbc, sdt = jnp.split(proj, splits, axis=-1)

    q = gq.reshape(B, T, GLA_H, GLA_DK) * GLA_DK ** -0.5
    k = gk.reshape(B, T, GLA_H, GLA_DK)
    v = gv.reshape(B, T, GLA_H, GLA_DV)
    gate = (ga @ p['gla_w_gate'][l] + p['gla_b_gate'][l]).astype(F32)
    log_a = (jax.nn.log_sigmoid(gate) / GLA_GATE_TEMP).reshape(B, T, GLA_H, GLA_DK)
    o, s_gla_new = chunked_linear_recurrence(q, k, v, log_a, s_gla, GLA_CHUNK)
    o_gla = rms_norm(o, p['gla_norm'][l].reshape(GLA_H, GLA_DV)).reshape(B, T, GLA_WIDTH)
    o_gla = o_gla * jax.nn.silu(gr.astype(F32))

    pos = pos0 + jnp.arange(T, dtype=jnp.int32)
    q = rope(rq.reshape(B, T, RET_H, RET_DK), pos)
    k = rope(rk.reshape(B, T, RET_H, RET_DK), pos) * RET_DK ** -0.5
    v = rv.reshape(B, T, RET_H, RET_DV)
    log_gamma = jnp.log(1.0 - 2.0 ** (-5.0 - jnp.arange(RET_H, dtype=F32)))
    log_a = jnp.broadcast_to(log_gamma[None, None, :, None], (B, T, RET_H, 1))
    o, s_ret_new = chunked_linear_recurrence(q, k, v, log_a, s_ret, RET_CHUNK)
    o_ret = head_group_norm(o, p['ret_norm'][l]) * jax.nn.silu(rg.astype(F32))

    xbc, s_conv_new = causal_dwconv(sxbc, s_conv, p['ssd_conv_w'][l], p['ssd_conv_b'][l])
    xbc = jax.nn.silu(xbc.astype(F32))
    xs, bm, cm = jnp.split(xbc, [SSD_WIDTH, SSD_WIDTH + SSD_G * SSD_N], axis=-1)
    xs = xs.reshape(B, T, SSD_H, SSD_P)
    bm = jnp.repeat(bm.reshape(B, T, SSD_G, SSD_N), SSD_H // SSD_G, axis=2)
    cm = jnp.repeat(cm.reshape(B, T, SSD_G, SSD_N), SSD_H // SSD_G, axis=2)
    dt = jax.nn.softplus(sdt.astype(F32) + p['ssd_dt_bias'][l].astype(F32))
    a = -jnp.exp(p['ssd_a_log'][l].astype(F32))
    o, s_ssd_new = chunked_linear_recurrence(cm, bm, xs * dt[..., None],
                                             (dt * a)[..., None], s_ssd, SSD_CHUNK)
    y = o + p['ssd_d'][l].astype(F32)[:, None] * xs
    o_ssd = rms_norm(y.reshape(B, T, SSD_WIDTH) * jax.nn.silu(sz.astype(F32)), p['ssd_norm'][l])

    merged = jnp.concatenate([o_gla, o_ret, o_ssd], axis=-1).astype(h.dtype)
    out = merged @ p['w_out'][l]
    return out, (s_gla_new, s_ret_new, s_ssd_new, s_conv_new)


def hier_moe(h, l, p):
    B, T, D = h.shape
    t = h.reshape(B * T, D)
    p_group = jax.nn.softmax((t @ p['moe_w_group'][l] + p['moe_b_group'][l]).astype(F32), -1)
    g_sel = jnp.argmax(p_group, -1)
    g_gate = jnp.take_along_axis(p_group, g_sel[:, None], -1)
    e_logits = (t @ p['moe_w_expert'][l] + p['moe_b_expert'][l]).astype(F32)
    e_logits = e_logits.reshape(-1, MOE_GROUPS, MOE_PER_GROUP)
    e_in = jnp.take_along_axis(e_logits, g_sel[:, None, None], 1)[:, 0]
    top_v, top_i = lax.top_k(e_in, MOE_TOPK)
    w_top = jax.nn.softmax(top_v, -1) * g_gate
    expert_idx = g_sel[:, None] * MOE_PER_GROUP + top_i
    combine = jnp.sum(jax.nn.one_hot(expert_idx, MOE_EXPERTS, dtype=F32) * w_top[..., None], 1)
    hid = jax.nn.silu(jnp.einsum('nd,edf->nef', t, p['moe_w1'][l])) * \
        jnp.einsum('nd,edf->nef', t, p['moe_w3'][l])
    y = jnp.einsum('nef,efd->nd', hid * combine[..., None].astype(hid.dtype), p['moe_w2'][l])
    return y.reshape(B, T, D)


def trunk(x, c, pos0, s_gla, s_ret, s_ssd, s_conv, p):
    new_gla, new_ret, new_ssd, new_conv = [], [], [], []
    for l in range(DEPTH):
        mod = jax.nn.silu(c) @ p['w_ada'][l] + p['b_ada'][l]
        sh1, sc1, g1, sh2, sc2, g2 = jnp.split(mod[:, None, :], 6, axis=-1)
        h = x * (1.0 + sc1) + sh1
        mix, st = mixer(h, pos0, s_gla[l], s_ret[l], s_ssd[l], s_conv[l], l, p)
        x = layer_norm(ALPHA * x + g1 * mix, p['ln1_g'][l], p['ln1_b'][l]).astype(x.dtype)
        h = x * (1.0 + sc2) + sh2
        x = layer_norm(ALPHA * x + g2 * hier_moe(h, l, p), p['ln2_g'][l], p['ln2_b'][l]).astype(x.dtype)
        new_gla.append(st[0])
        new_ret.append(st[1])
        new_ssd.append(st[2])
        new_conv.append(st[3])
    return x, jnp.stack(new_gla), jnp.stack(new_ret), jnp.stack(new_ssd), jnp.stack(new_conv)


def setup_inputs(seed: int = 0) -> dict:
    key = jax.random.key(seed)
    keys = iter(jax.random.split(key, 64))
    nrm = lambda shape, s=1.0: jax.random.normal(next(keys), shape, F32) * s
    D = D_MODEL
    value_cols = (2, 7, 10)
    col_scale = jnp.concatenate([
        jnp.full((w,), BETA if i in value_cols else 1.0, F32) for i, w in enumerate(IN_WIDTHS)])
    off = sum(IN_WIDTHS[:10])
    col_scale = col_scale.at[off + SSD_WIDTH:off + SSD_CONV_DIM].set(1.0)
    u = jax.random.uniform(next(keys), (DEPTH, SSD_H), F32)
    dt0 = jnp.exp(u * (math.log(0.1) - math.log(0.001)) + math.log(0.001))
    inp = {
        'x_prompt': nrm((BATCH, SEQ, D)),
        'x_sample': nrm((DEC_BATCH, DEC_SEQ, D)),
        'c_prompt': nrm((BATCH, D)),
        'c_sample': nrm((DEC_BATCH, D)),
        'state_gla': nrm((DEPTH, DEC_BATCH, GLA_H, GLA_DK, GLA_DV), 0.5),
        'state_ret': nrm((DEPTH, DEC_BATCH, RET_H, RET_DK, RET_DV), 0.5),
        'state_ssd': nrm((DEPTH, DEC_BATCH, SSD_H, SSD_N, SSD_P), 0.5),
        'state_conv': nrm((DEPTH, DEC_BATCH, SSD_CONV_W - 1, SSD_CONV_DIM)),
        'w_ada': nrm((DEPTH, D, 6 * D), D ** -0.5),
        'b_ada': nrm((DEPTH, 6 * D), 0.02),
        'w_in': nrm((DEPTH, D, N_IN), D ** -0.5) * col_scale,
        'gla_w_gate': nrm((DEPTH, GLA_GATE_RANK, GLA_H * GLA_DK), GLA_GATE_RANK ** -0.5),
        'gla_b_gate': nrm((DEPTH, GLA_H * GLA_DK), 0.5),
        'gla_norm': 1.0 + nrm((DEPTH, GLA_WIDTH), 0.02),
        'ret_norm': 1.0 + nrm((DEPTH, RET_WIDTH), 0.02),
        'ssd_conv_w': nrm((DEPTH, SSD_CONV_W, SSD_CONV_DIM), SSD_CONV_W ** -0.5),
        'ssd_conv_b': nrm((DEPTH, SSD_CONV_DIM), 0.01),
        'ssd_dt_bias': dt0 + jnp.log(-jnp.expm1(-dt0)),
        'ssd_a_log': jnp.log(jax.random.uniform(next(keys), (DEPTH, SSD_H), F32, 1.0, 16.0)),
        'ssd_d': 1.0 + nrm((DEPTH, SSD_H), 0.1),
        'ssd_norm': 1.0 + nrm((DEPTH, SSD_WIDTH), 0.02),
        'w_out': nrm((DEPTH, D_MIX, D), D_MIX ** -0.5 * BETA),
        'ln1_g': 1.0 + nrm((DEPTH, D), 0.02),
        'ln1_b': nrm((DEPTH, D), 0.02),
        'moe_w_group': nrm((DEPTH, D, MOE_GROUPS), D ** -0.5),
        'moe_b_group': nrm((DEPTH, MOE_GROUPS), 0.01),
        'moe_w_expert': nrm((DEPTH, D, MOE_EXPERTS), D ** -0.5),
        'moe_b_expert': nrm((DEPTH, MOE_EXPERTS), 0.01),
        'moe_w1': nrm((DEPTH, MOE_EXPERTS, D, MOE_FF), D ** -0.5 * BETA),
        'moe_w3': nrm((DEPTH, MOE_EXPERTS, D, MOE_FF), D ** -0.5 * BETA),
        'moe_w2': nrm((DEPTH, MOE_EXPERTS, MOE_FF, D), MOE_FF ** -0.5 * BETA),
        'ln2_g': 1.0 + nrm((DEPTH, D), 0.02),
        'ln2_b': nrm((DEPTH, D), 0.02),
    }
    return inp


def reference(x_prompt, x_sample, c_prompt, c_sample, state_gla, state_ret, state_ssd, state_conv,
              w_ada, b_ada, w_in, gla_w_gate, gla_b_gate, gla_norm, ret_norm, ssd_conv_w, ssd_conv_b,
              ssd_dt_bias, ssd_a_log, ssd_d, ssd_norm, w_out, ln1_g, ln1_b, moe_w_group, moe_b_group,
              moe_w_expert, moe_b_expert, moe_w1, moe_w3, moe_w2, ln2_g, ln2_b):
    p = {'w_ada': w_ada, 'b_ada': b_ada, 'w_in': w_in, 'gla_w_gate': gla_w_gate,
         'gla_b_gate': gla_b_gate, 'gla_norm': gla_norm, 'ret_norm': ret_norm,
         'ssd_conv_w': ssd_conv_w, 'ssd_conv_b': ssd_conv_b, 'ssd_dt_bias': ssd_dt_bias,
         'ssd_a_log': ssd_a_log, 'ssd_d': ssd_d, 'ssd_norm': ssd_norm, 'w_out': w_out,
         'ln1_g': ln1_g, 'ln1_b': ln1_b, 'moe_w_group': moe_w_group, 'moe_b_group': moe_b_group,
         'moe_w_expert': moe_w_expert, 'moe_b_expert': moe_b_expert, 'moe_w1': moe_w1,
         'moe_w3': moe_w3, 'moe_w2': moe_w2, 'ln2_g': ln2_g, 'ln2_b': ln2_b}
    B = x_prompt.shape[0]
    dt = x_prompt.dtype
    z_gla = jnp.zeros((DEPTH, B, GLA_H, GLA_DK, GLA_DV), dt)
    z_ret = jnp.zeros((DEPTH, B, RET_H, RET_DK, RET_DV), dt)
    z_ssd = jnp.zeros((DEPTH, B, SSD_H, SSD_N, SSD_P), dt)
    z_conv = jnp.zeros((DEPTH, B, SSD_CONV_W - 1, SSD_CONV_DIM), dt)
    y_prompt, gla_p, ret_p, ssd_p, conv_p = trunk(x_prompt, c_prompt, 0, z_gla, z_ret, z_ssd, z_conv, p)
    y_sample, gla_s, ret_s, ssd_s, conv_s = trunk(x_sample, c_sample, PAST_LEN, state_gla, state_ret,
                                                  state_ssd, state_conv, p)
    return (y_prompt, y_sample, gla_p, ret_p, ssd_p, conv_p, gla_s, ret_s, ssd_s, conv_s)
```

```python
import functools
import math

import numpy as np
import jax
import jax.numpy as jnp
from jax import lax
from jax.experimental import pallas as pl
from jax.experimental.pallas import tpu as pltpu

F32 = jnp.float32
BF = jnp.bfloat16

D_MODEL = 1024
DEPTH = 2
PAST_LEN = 16384
GLA_H, GLA_DK, GLA_DV = 4, 32, 64
GLA_WIDTH = GLA_H * GLA_DV
GLA_GATE_RANK = 16
GLA_GATE_TEMP = 16.0
GLA_CHUNK = 16
RET_H, RET_DK, RET_DV = 4, 64, 64
RET_WIDTH = RET_H * RET_DV
ROPE_BASE = 10000.0
SSD_H, SSD_P, SSD_G, SSD_N = 8, 64, 2, 64
SSD_WIDTH = SSD_H * SSD_P
SSD_CONV_W = 4
SSD_CONV_DIM = SSD_WIDTH + 2 * SSD_G * SSD_N
MOE_GROUPS, MOE_PER_GROUP = 4, 4
MOE_EXPERTS = MOE_GROUPS * MOE_PER_GROUP
MOE_FF = 256
ALPHA = (2 * DEPTH) ** 0.25
EPS = 1e-5

LANE = 128
GLA_IN_W = 128 + 128 + 256 + LANE + 256
RET_IN_W = 4 * 256
SSD_IN_W = 512 + SSD_CONV_DIM + LANE
IN_W = GLA_IN_W + RET_IN_W + SSD_IN_W
VMEM_LIMIT = 56 * 1024 * 1024


def _cp(n_axes, vmem=VMEM_LIMIT):
    return pltpu.CompilerParams(dimension_semantics=("arbitrary",) * n_axes, vmem_limit_bytes=vmem)


def _dot(a, b):
    return jnp.dot(a, b, preferred_element_type=F32)


def _dot_nt(a, b):
    return lax.dot_general(a, b, (((1,), (1,)), ((), ())), preferred_element_type=F32)


def _dot_tn(a, b):
    return lax.dot_general(a, b, (((0,), (0,)), ((), ())), preferred_element_type=F32)


def _split3(x):
    hi = x.astype(BF)
    r = x - hi.astype(F32)
    mid = r.astype(BF)
    lo = (r - mid.astype(F32)).astype(BF)
    return hi, mid, lo


def _dot_x3(x, e):
    hi, mid, lo = _split3(x)
    return _dot(hi, e) + (_dot(mid, e) + _dot(lo, e))


def _dot_x2(x, e):
    hi = x.astype(BF)
    lo = (x - hi.astype(F32)).astype(BF)
    return _dot(hi, e) + _dot(lo, e)


def _dot_3x(e, x):
    hi, mid, lo = _split3(x)
    return _dot(e, hi) + (_dot(e, mid) + _dot(e, lo))


def _sigmoid(x):
    return 1.0 / (1.0 + jnp.exp(-x))


def _silu(x):
    return x * _sigmoid(x)


def _log_sigmoid(x):
    return jnp.minimum(x, 0.0) - jnp.log(1.0 + jnp.exp(-jnp.abs(x)))


def _softplus(x):
    return jnp.maximum(x, 0.0) + jnp.log(1.0 + jnp.exp(-jnp.abs(x)))


def _layer_norm(x, g, b):
    mu = jnp.mean(x, axis=-1, keepdims=True)
    d = x - mu
    var = jnp.mean(d * d, axis=-1, keepdims=True)
    return d * lax.rsqrt(var + EPS) * g + b


def _mod_kernel(c_ref, w_ref, b_ref, o_ref):
    s = _silu(c_ref[...]).astype(BF)
    o_ref[0] = _dot(s, w_ref[0].astype(BF)) + b_ref[0]


def _mod_call(c_all, w_ada, b_ada):
    R = c_all.shape[0]
    tn = 1536
    return pl.pallas_call(
        _mod_kernel,
        grid=(DEPTH, 6 * D_MODEL // tn),
        in_specs=[pl.BlockSpec((R, D_MODEL), lambda l, j: (0, 0)),
                  pl.BlockSpec((1, D_MODEL, tn), lambda l, j: (l, 0, j)),
                  pl.BlockSpec((1, 1, tn), lambda l, j: (l, 0, j))],
        out_specs=pl.BlockSpec((1, R, tn), lambda l, j: (l, 0, j)),
        out_shape=jax.ShapeDtypeStruct((DEPTH, R, 6 * D_MODEL), F32),
        compiler_params=_cp(2),
        name="ada_mod",
    )(c_all, w_ada, b_ada.reshape(DEPTH, 1, 6 * D_MODEL))


def _inproj_kernel(x_ref, sc_ref, sh_ref, w_ref, og_ref, or_ref, os_ref):
    bB, bT, D = x_ref.shape
    h = x_ref[...] * (1.0 + sc_ref[...]) + sh_ref[...]
    hb = h.reshape(bB * bT, D).astype(BF)
    og_ref[...] = _dot(hb, w_ref[:, 0:GLA_IN_W])
    or_ref[...] = _dot(hb, w_ref[:, GLA_IN_W:GLA_IN_W + RET_IN_W])
    os_ref[...] = _dot(hb, w_ref[:, GLA_IN_W + RET_IN_W:IN_W])


def _tok_tiles(B, T):
    if T >= 512:
        return 1, 512
    return 512 // T, T


def _inproj_call(x3, sc, sh, w_p):
    B, T, D = x3.shape
    bB, bT = _tok_tiles(B, T)
    nT = T // bT
    R = bB * bT
    N = B * T
    xmap = lambda i, j: (i, j, 0)
    mmap = lambda i, j: (i, 0, 0)
    omap = lambda i, j: (i * nT + j, 0)
    return pl.pallas_call(
        _inproj_kernel,
        grid=(B // bB, nT),
        in_specs=[pl.BlockSpec((bB, bT, D), xmap),
                  pl.BlockSpec((bB, 1, D), mmap),
                  pl.BlockSpec((bB, 1, D), mmap),
                  pl.BlockSpec((D, IN_W), lambda i, j: (0, 0))],
        out_specs=[pl.BlockSpec((R, GLA_IN_W), omap),
                   pl.BlockSpec((R, RET_IN_W), omap),
                   pl.BlockSpec((R, SSD_IN_W), omap)],
        out_shape=[jax.ShapeDtypeStruct((N, GLA_IN_W), F32),
                   jax.ShapeDtypeStruct((N, RET_IN_W), F32),
                   jax.ShapeDtypeStruct((N, SSD_IN_W), F32)],
        compiler_params=_cp(2),
        name="in_proj",
    )(x3, sc, sh, w_p)


def _head_block_mask(rows_per, cols_per, n):
    r = np.arange(rows_per * n)[:, None] // rows_per
    c = np.arange(cols_per * n)[None, :] // cols_per
    return (r == c).astype(np.float32)


def _block_tril(n, c):
    i = np.arange(n)[:, None]
    j = np.arange(n)[None, :]
    return ((i // c == j // c) & (j <= i)).astype(np.float32)


def _gla_front(x_ref, wg_ref, bg_ref, L_ref):
    q = x_ref[:, 0:128] * (GLA_DK ** -0.5)
    k = x_ref[:, 128:256]
    v = x_ref[:, 256:512]
    ga = x_ref[:, 512:640]
    r = x_ref[:, 640:896]
    gate = _dot(ga.astype(BF), wg_ref[...]) + bg_ref[...]
    la = _log_sigmoid(gate) * (1.0 / GLA_GATE_TEMP)
    g = _dot_3x(L_ref[...], la)
    return q, k, v, r, g


def _gla_intra(q, g, kp_ref, gp_ref, vp_ref, E_ref, c):
    TT = q.shape[0]
    PAD = kp_ref.shape[0] - TT
    pos = lax.broadcasted_iota(jnp.int32, (TT, 1), 0) & (c - 1)
    o = jnp.zeros((TT, 2 * LANE), F32)
    for s in range(c):
        ks = kp_ref[pl.ds(PAD - s, TT), :]
        gs = gp_ref[pl.ds(PAD - s, TT), :]
        vs = vp_ref[pl.ds(PAD - s, TT), :]
        w = jnp.where(pos >= s, q * ks * jnp.exp(jnp.minimum(g - gs, 0.0)), 0.0)
        o = o + _dot(w.astype(BF), E_ref[...]) * vs
    return o


def _gla_norm_gate(o, r, nw_ref, EA_ref):
    ms = _dot_x3(o * o, EA_ref[...])
    return o * lax.rsqrt(ms + EPS) * nw_ref[...] * _silu(r)


def _gla_prompt_kernel(x_ref, wg_ref, bg_ref, nw_ref, L_ref, E_ref, EA_ref, M_ref,
                       o_ref, sfin_ref, st_ref, kp_ref, gp_ref, vp_ref, oi_ref, *, c):
    TT = x_ref.shape[0]
    nc = TT // c
    PAD = kp_ref.shape[0] - TT
    t = pl.program_id(1)

    @pl.when(t == 0)
    def _():
        st_ref[...] = jnp.zeros_like(st_ref)

    q, k, v, r, g = _gla_front(x_ref, wg_ref, bg_ref, L_ref)
    kp_ref[0:PAD, :] = jnp.zeros((PAD, LANE), F32)
    gp_ref[0:PAD, :] = jnp.zeros((PAD, LANE), F32)
    vp_ref[0:PAD, :] = jnp.zeros((PAD, 2 * LANE), F32)
    kp_ref[PAD:PAD + TT, :] = k
    gp_ref[PAD:PAD + TT, :] = g
    vp_ref[PAD:PAD + TT, :] = v
    o = _gla_intra(q, g, kp_ref, gp_ref, vp_ref, E_ref, c)

    qe = (q * jnp.exp(g)).astype(BF)
    S = st_ref[...]
    M = M_ref[...]
    for n in range(nc):
        lo = n * c
        gl = g[lo + c - 1:lo + c, :]
        ke = (k[lo:lo + c, :] * jnp.exp(gl - g[lo:lo + c, :])).astype(BF)
        oi_ref[lo:lo + c, :] = _dot_nt(qe[lo:lo + c, :], S.astype(BF))
        uT = _dot_tn(v[lo:lo + c, :].astype(BF), ke)
        S = S * jnp.exp(gl) + uT * M
    st_ref[...] = S
    o = o + oi_ref[...]
    o_ref[...] = _gla_norm_gate(o, r, nw_ref, EA_ref).astype(o_ref.dtype)

    @pl.when(t == pl.num_programs(1) - 1)
    def _():
        Sf = S.T
        for h in range(GLA_H):
            sfin_ref[0, h] = Sf[h * GLA_DK:(h + 1) * GLA_DK, h * GLA_DV:(h + 1) * GLA_DV]


def _gla_tables(TT, c):
    L = jnp.asarray(_block_tril(TT, c), BF)
    E = jnp.asarray(_head_block_mask(GLA_DK, GLA_DV, GLA_H), BF)
    EA = jnp.asarray(_head_block_mask(GLA_DV, GLA_DV, GLA_H) / GLA_DV, BF)
    M = jnp.asarray(_head_block_mask(GLA_DV, GLA_DK, GLA_H), F32)
    return L, E, EA, M


def _gla_params(w_gate, b_gate, norm_w):
    wg = jnp.zeros((LANE, GLA_H * GLA_DK), F32).at[:GLA_GATE_RANK].set(w_gate).astype(BF)
    return wg, b_gate.reshape(1, -1), norm_w.reshape(1, -1)


def _const(shape):
    return pl.BlockSpec(shape, lambda *_: (0,) * len(shape))


def _gla_prompt_call(gin, B, T, w_gate, b_gate, norm_w):
    TT, c = 256, GLA_CHUNK
    nT = T // TT
    L, E, EA, M = _gla_tables(TT, c)
    wg, bg, nw = _gla_params(w_gate, b_gate, norm_w)
    PAD = 16
    return pl.pallas_call(
        functools.partial(_gla_prompt_kernel, c=c),
        grid=(B, nT),
        in_specs=[pl.BlockSpec((TT, GLA_IN_W), lambda b, t: (b * nT + t, 0)),
                  _const(wg.shape), _const(bg.shape), _const(nw.shape),
                  _const(L.shape), _const(E.shape), _const(EA.shape), _const(M.shape)],
        out_specs=[pl.BlockSpec((TT, GLA_WIDTH), lambda b, t: (b * nT + t, 0)),
                   pl.BlockSpec((1, GLA_H, GLA_DK, GLA_DV), lambda b, t: (b, 0, 0, 0))],
        out_shape=[jax.ShapeDtypeStruct((B * T, GLA_WIDTH), BF),
                   jax.ShapeDtypeStruct((B, GLA_H, GLA_DK, GLA_DV), F32)],
        scratch_shapes=[pltpu.VMEM((GLA_H * GLA_DV, GLA_H * GLA_DK), F32),
                        pltpu.VMEM((TT + PAD, LANE), F32),
                        pltpu.VMEM((TT + PAD, LANE), F32),
                        pltpu.VMEM((TT + PAD, 2 * LANE), F32),
                        pltpu.VMEM((TT, 2 * LANE), F32)],
        compiler_params=_cp(2),
        name="gla_prompt",
    )(gin, wg, bg, nw, L, E, EA, M)


def _rope(x, cos, sin_signed):
    lane = lax.broadcasted_iota(jnp.int32, (1, LANE), 1)
    first_half = (lane & (RET_DK - 1)) < RET_DK // 2
    out = []
    for p in range(2):
        xs = x[:, p * LANE:(p + 1) * LANE]
        up = pltpu.roll(xs, LANE - RET_DK // 2, 1)
        dn = pltpu.roll(xs, RET_DK // 2, 1)
        out.append(xs * cos + jnp.where(first_half, up, dn) * sin_signed)
    return jnp.concatenate(out, axis=1)


def _ret_front(x_ref, cos_ref, sin_ref):
    q = _rope(x_ref[:, 0:256], cos_ref[...], sin_ref[...])
    k = _rope(x_ref[:, 256:512], cos_ref[...], sin_ref[...]) * (RET_DK ** -0.5)
    v = x_ref[:, 512:768]
    rg = x_ref[:, 768:1024]
    return q, k, v, rg


def _ret_intra(q, k, v, D_ref):
    lane = lax.broadcasted_iota(jnp.int32, (1, RET_WIDTH), 1)
    kb = k.astype(BF)
    o = jnp.zeros(q.shape, F32)
    for h in range(RET_H):
        hm = (lane // RET_DK) == h
        s = _dot_nt(jnp.where(hm, q, 0.0).astype(BF), kb)
        p = (s * D_ref[h]).astype(BF)
        o = o + _dot(p, jnp.where(hm, v, 0.0).astype(BF))
    return o


def _ret_norm_gate(o, rg, nw_ref, EA_ref):
    mu = _dot_x3(o, EA_ref[...])
    d = o - mu
    var = _dot_x3(d * d, EA_ref[...])
    return d * lax.rsqrt(var + EPS) * nw_ref[...] * _silu(rg)


def _ret_prompt_kernel(x_ref, cos_ref, sin_ref, D_ref, rd_ref, kd_ref, G_ref, M_ref, EA_ref, nw_ref,
                       o_ref, sfin_ref, st_ref):
    t = pl.program_id(1)

    @pl.when(t == 0)
    def _():
        st_ref[...] = jnp.zeros_like(st_ref)

    q, k, v, rg = _ret_front(x_ref, cos_ref, sin_ref)
    o = _ret_intra(q, k, v, D_ref)
    S = st_ref[...]
    o = o + _dot((q * rd_ref[...]).astype(BF), S.astype(BF))
    u = _dot_tn((k * kd_ref[...]).astype(BF), v.astype(BF))
    S = S * G_ref[...] + u * M_ref[...]
    st_ref[...] = S
    o_ref[...] = _ret_norm_gate(o, rg, nw_ref, EA_ref).astype(o_ref.dtype)

    @pl.when(t == pl.num_programs(1) - 1)
    def _():
        for h in range(RET_H):
            sfin_ref[0, h] = S[h * RET_DK:(h + 1) * RET_DK, h * RET_DV:(h + 1) * RET_DV]


def _rope_tables(pos):
    half = RET_DK // 2
    inv = ROPE_BASE ** (-jnp.arange(half, dtype=F32) / half)
    ang = pos.astype(F32)[:, None] * inv[None, :]
    cos, sin = jnp.cos(ang), jnp.sin(ang)
    return jnp.tile(jnp.concatenate([cos, cos], 1), (1, 2)), jnp.tile(jnp.concatenate([-sin, sin], 1), (1, 2))


def _ret_log_gamma():
    return np.log(1.0 - 2.0 ** (-5.0 - np.arange(RET_H, dtype=np.float64)))


def _ret_prompt_call(rin, B, T, norm_w):
    TT = 256
    nT = T // TT
    cos, sin = _rope_tables(jnp.arange(T, dtype=jnp.int32))
    lg = _ret_log_gamma()
    i = np.arange(TT)
    dec = np.exp(lg[:, None, None] * (i[:, None] - i[None, :])[None]) * (i[:, None] >= i[None, :])[None]
    Dm = jnp.asarray(dec, F32)
    rd = jnp.asarray(np.repeat(np.exp(lg[None, :] * (i[:, None] + 1)), RET_DK, 1), F32)
    kd = jnp.asarray(np.repeat(np.exp(lg[None, :] * (TT - 1 - i[:, None])), RET_DK, 1), F32)
    M = _head_block_mask(RET_DK, RET_DV, RET_H)
    G = jnp.asarray(M * np.repeat(np.exp(lg * TT), RET_DK)[:, None], F32)
    M = jnp.asarray(M, F32)
    EA = jnp.asarray(_head_block_mask(RET_DV, RET_DV, RET_H) / RET_DV, BF)
    nw = norm_w.reshape(1, -1)
    return pl.pallas_call(
        _ret_prompt_kernel,
        grid=(B, nT),
        in_specs=[pl.BlockSpec((TT, RET_IN_W), lambda b, t: (b * nT + t, 0)),
                  pl.BlockSpec((TT, LANE), lambda b, t: (t, 0)),
                  pl.BlockSpec((TT, LANE), lambda b, t: (t, 0)),
                  _const(Dm.shape), _const(rd.shape), _const(kd.shape), _const(G.shape), _const(M.shape),
                  _const(EA.shape), _const(nw.shape)],
        out_specs=[pl.BlockSpec((TT, RET_WIDTH), lambda b, t: (b * nT + t, 0)),
                   pl.BlockSpec((1, RET_H, RET_DK, RET_DV), lambda b, t: (b, 0, 0, 0))],
        out_shape=[jax.ShapeDtypeStruct((B * T, RET_WIDTH), BF),
                   jax.ShapeDtypeStruct((B, RET_H, RET_DK, RET_DV), F32)],
        scratch_shapes=[pltpu.VMEM((RET_H * RET_DK, RET_H * RET_DV), F32)],
        compiler_params=_cp(2),
        name="ret_prompt",
    )(rin, cos, sin, Dm, rd, kd, G, M, EA, nw)


def _ssd_conv(xp_ref, cw_ref, cb_ref, TT):
    acc = cb_ref[...] + cw_ref[SSD_CONV_W - 1:SSD_CONV_W, :] * xp_ref[pl.ds(8, TT), :]
    for i in range(SSD_CONV_W - 1):
        acc = acc + cw_ref[i:i + 1, :] * xp_ref[pl.ds(8 - (SSD_CONV_W - 1) + i, TT), :]
    return acc


def _ssd_intra(xs, bm, cm, g, dt, Mk_ref):
    TT = xs.shape[0]
    gT = g.T
    dtT = dt.T
    lane = lax.broadcasted_iota(jnp.int32, (1, LANE), 1)
    lane2 = lax.broadcasted_iota(jnp.int32, (1, 2 * LANE), 1)
    causal = Mk_ref[...] > 0.0
    bmb = bm.astype(BF)
    o_parts = []
    for grp in range(SSD_G):
        cb = _dot_nt(jnp.where((lane // SSD_N) == grp, cm, 0.0).astype(BF), bmb)
        xg = xs[:, grp * 2 * LANE:(grp + 1) * 2 * LANE]
        og = jnp.zeros((TT, 2 * LANE), F32)
        for h4 in range(SSD_H // SSD_G):
            h = grp * (SSD_H // SSD_G) + h4
            dec = jnp.where(causal, jnp.exp(jnp.minimum(g[:, h:h + 1] - gT[h:h + 1, :], 0.0)), 0.0)
            p = (cb * dec * dtT[h:h + 1, :]).astype(BF)
            og = og + _dot(p, jnp.where((lane2 // SSD_P) == h4, xg, 0.0).astype(BF))
        o_parts.append(og)
    return jnp.concatenate(o_parts, axis=1)


def _ssd_prompt_kernel(x_ref, cw_ref, cb_ref, dtb_ref, alog_ref, dexp_ref, nw_ref, L_ref, Mk_ref, Eexp_ref, M2_ref,
                       o_ref, sfin_ref, cfin_ref, st_ref, xp_ref):
    TT = x_ref.shape[0]
    t = pl.program_id(1)

    @pl.when(t == 0)
    def _():
        st_ref[...] = jnp.zeros_like(st_ref)
        xp_ref[0:8, :] = jnp.zeros((8, SSD_CONV_DIM), F32)

    z = x_ref[:, 0:SSD_WIDTH]
    xp_ref[8:8 + TT, :] = x_ref[:, SSD_WIDTH:SSD_WIDTH + SSD_CONV_DIM]
    sdt = x_ref[:, SSD_WIDTH + SSD_CONV_DIM:SSD_IN_W]
    xbc = _silu(_ssd_conv(xp_ref, cw_ref, cb_ref, TT))
    tail = xp_ref[TT:TT + 8, :]
    xp_ref[0:8, :] = tail
    xs = xbc[:, 0:SSD_WIDTH]
    bm = xbc[:, SSD_WIDTH:SSD_WIDTH + LANE]
    cm = xbc[:, SSD_WIDTH + LANE:SSD_CONV_DIM]

    dt = _softplus(sdt + dtb_ref[...])
    la = dt * (-jnp.exp(alog_ref[...]))
    g = _dot_3x(L_ref[...], la)
    gl = g[TT - 1:TT, :]
    Eexp = Eexp_ref[...]
    eg_x = _dot_x2(jnp.exp(g), Eexp)
    cw_x = _dot_x2(dt * jnp.exp(gl - g), Eexp)
    egl_x = _dot_x2(jnp.exp(gl), Eexp)

    o = _ssd_intra(xs, bm, cm, g, dt, Mk_ref)

    S = st_ref[...]
    o = o + eg_x * _dot(cm.astype(BF), S.astype(BF))
    u = _dot_tn(bm.astype(BF), (xs * cw_x).astype(BF))
    S = S * egl_x + u * M2_ref[...]
    st_ref[...] = S

    y = (o + dexp_ref[...] * xs) * _silu(z)
    ms = jnp.mean(y * y, axis=-1, keepdims=True)
    o_ref[...] = (y * lax.rsqrt(ms + EPS) * nw_ref[...]).astype(o_ref.dtype)

    @pl.when(t == pl.num_programs(1) - 1)
    def _():
        for h in range(SSD_H):
            gi = h // (SSD_H // SSD_G)
            sfin_ref[0, h] = S[gi * SSD_N:(gi + 1) * SSD_N, h * SSD_P:(h + 1) * SSD_P]
        cfin_ref[0] = tail[8 - (SSD_CONV_W - 1):8, :]


def _pad_lanes(v, n=LANE):
    v = v.reshape(1, -1)
    return jnp.zeros((1, n), F32).at[:, :v.shape[1]].set(v)


def _ssd_tables(TT, c):
    L = jnp.asarray(_block_tril(TT, c), BF)
    Mk = jnp.asarray(_block_tril(TT, c), F32)
    e = np.zeros((LANE, SSD_WIDTH), np.float32)
    for h in range(SSD_H):
        e[h, h * SSD_P:(h + 1) * SSD_P] = 1.0
    M2 = np.zeros((SSD_G * SSD_N, SSD_WIDTH), np.float32)
    for h in range(SSD_H):
        gi = h // (SSD_H // SSD_G)
        M2[gi * SSD_N:(gi + 1) * SSD_N, h * SSD_P:(h + 1) * SSD_P] = 1.0
    return L, Mk, jnp.asarray(e, BF), jnp.asarray(M2, F32)


def _ssd_params(conv_w, conv_b, dt_bias, a_log, d, norm_w):
    return (conv_w, conv_b.reshape(1, -1), _pad_lanes(dt_bias), _pad_lanes(a_log),
            jnp.repeat(d, SSD_P).reshape(1, -1), norm_w.reshape(1, -1))


def _ssd_prompt_call(sin_, B, T, conv_w, conv_b, dt_bias, a_log, d, norm_w):
    TT = 256
    nT = T // TT
    L, Mk, Eexp, M2 = _ssd_tables(TT, TT)
    prm = _ssd_params(conv_w, conv_b, dt_bias, a_log, d, norm_w)
    return pl.pallas_call(
        _ssd_prompt_kernel,
        grid=(B, nT),
        in_specs=[pl.BlockSpec((TT, SSD_IN_W), lambda b, t: (b * nT + t, 0))]
                 + [_const(p.shape) for p in prm]
                 + [_const(L.shape), _const(Mk.shape), _const(Eexp.shape), _const(M2.shape)],
        out_specs=[pl.BlockSpec((TT, SSD_WIDTH), lambda b, t: (b * nT + t, 0)),
                   pl.BlockSpec((1, SSD_H, SSD_N, SSD_P), lambda b, t: (b, 0, 0, 0)),
                   pl.BlockSpec((1, SSD_CONV_W - 1, SSD_CONV_DIM), lambda b, t: (b, 0, 0))],
        out_shape=[jax.ShapeDtypeStruct((B * T, SSD_WIDTH), BF),
                   jax.ShapeDtypeStruct((B, SSD_H, SSD_N, SSD_P), F32),
                   jax.ShapeDtypeStruct((B, SSD_CONV_W - 1, SSD_CONV_DIM), F32)],
        scratch_shapes=[pltpu.VMEM((SSD_G * SSD_N, SSD_WIDTH), F32),
                        pltpu.VMEM((TT + 8, SSD_CONV_DIM), F32)],
        compiler_params=_cp(2),
        name="ssd_prompt",
    )(sin_, *prm, L, Mk, Eexp, M2)


SEQ_TILE = 8


def _tile_lanes(n_rep, width):
    return np.tile(np.eye(width, dtype=np.float32), (1, n_rep))


def _fold_head_blocks(ubd):
    a = ubd[:, 0:LANE] + ubd[:, LANE:2 * LANE]
    return (a + pltpu.roll(a, LANE // 2, 1))[:, 0:LANE // 2]


def _col_bcast(row8, ones_ref):
    first = lax.broadcasted_iota(jnp.int32, (8, 1), 0) == 0
    hi, mid, lo = _split3(jnp.where(first, row8, 0.0))
    ones = ones_ref[...]
    return _dot_tn(hi, ones) + (_dot_tn(mid, ones) + _dot_tn(lo, ones))


def _gla_sample_kernel(x_ref, s0_ref, wg_ref, bg_ref, nw_ref, L_ref, E_ref, EA_ref, M_ref, T4_ref, ones_ref,
                       o_ref, sn_ref, kp_ref, gp_ref, vp_ref, oi_ref, *, c):
    TT = x_ref.shape[0]
    PAD = kp_ref.shape[0] - TT
    q, k, v, r, g = _gla_front(x_ref, wg_ref, bg_ref, L_ref)
    kp_ref[0:PAD, :] = jnp.zeros((PAD, LANE), F32)
    gp_ref[0:PAD, :] = jnp.zeros((PAD, LANE), F32)
    vp_ref[0:PAD, :] = jnp.zeros((PAD, 2 * LANE), F32)
    kp_ref[PAD:PAD + TT, :] = k
    gp_ref[PAD:PAD + TT, :] = g
    vp_ref[PAD:PAD + TT, :] = v
    o = _gla_intra(q, g, kp_ref, gp_ref, vp_ref, E_ref, c)
    qe = (q * jnp.exp(g)).astype(BF)
    M = M_ref[...]
    for s in range(TT // c):
        lo = s * c
        S0 = s0_ref[s].reshape(GLA_H * GLA_DK, GLA_DV)
        Sbd = (_dot(S0.astype(BF), T4_ref[...]) * M).astype(BF)
        oi_ref[lo:lo + c, :] = _dot(qe[lo:lo + c, :], Sbd)
        gl = g[lo + c - 1:lo + c, :]
        ke = (k[lo:lo + c, :] * jnp.exp(gl - g[lo:lo + c, :])).astype(BF)
        u = _fold_head_blocks(_dot_tn(ke, v[lo:lo + c, :].astype(BF)) * M)
        acol = _col_bcast(jnp.broadcast_to(jnp.exp(gl), (8, LANE)), ones_ref)
        sn_ref[s] = (acol * S0 + u).reshape(GLA_H, GLA_DK, GLA_DV)
    o = o + oi_ref[...]
    o_ref[...] = _gla_norm_gate(o, r, nw_ref, EA_ref).astype(o_ref.dtype)


def _gla_sample_call(gin, s0, B, T, w_gate, b_gate, norm_w):
    TT = SEQ_TILE * T
    L, E, EA, _ = _gla_tables(TT, T)
    M = jnp.asarray(_head_block_mask(GLA_DK, GLA_DV, GLA_H), F32)
    T4 = jnp.asarray(_tile_lanes(GLA_H, GLA_DV), BF)
    ones = jnp.ones((8, GLA_DV), BF)
    wg, bg, nw = _gla_params(w_gate, b_gate, norm_w)
    PAD = 8
    sspec = pl.BlockSpec((SEQ_TILE, GLA_H, GLA_DK, GLA_DV), lambda i: (i, 0, 0, 0))
    return pl.pallas_call(
        functools.partial(_gla_sample_kernel, c=T),
        grid=(B // SEQ_TILE,),
        in_specs=[pl.BlockSpec((TT, GLA_IN_W), lambda i: (i, 0)), sspec,
                  _const(wg.shape), _const(bg.shape), _const(nw.shape),
                  _const(L.shape), _const(E.shape), _const(EA.shape), _const(M.shape), _const(T4.shape),
                  _const(ones.shape)],
        out_specs=[pl.BlockSpec((TT, GLA_WIDTH), lambda i: (i, 0)), sspec],
        out_shape=[jax.ShapeDtypeStruct((B * T, GLA_WIDTH), BF),
                   jax.ShapeDtypeStruct((B, GLA_H, GLA_DK, GLA_DV), F32)],
        scratch_shapes=[pltpu.VMEM((TT + PAD, LANE), F32),
                        pltpu.VMEM((TT + PAD, LANE), F32),
                        pltpu.VMEM((TT + PAD, 2 * LANE), F32),
                        pltpu.VMEM((TT, 2 * LANE), F32)],
        compiler_params=_cp(1),
        name="gla_sample",
    )(gin, s0, wg, bg, nw, L, E, EA, M, T4, ones)


def _ret_sample_kernel(x_ref, s0_ref, cos_ref, sin_ref, D_ref, rd_ref, kd_ref, G_ref, M_ref, EA_ref, nw_ref, T4_ref,
                       o_ref, sn_ref, oi_ref, *, c):
    TT = x_ref.shape[0]
    q, k, v, rg = _ret_front(x_ref, cos_ref, sin_ref)
    o = _ret_intra(q, k, v, D_ref)
    qd = (q * rd_ref[...]).astype(BF)
    kd = (k * kd_ref[...]).astype(BF)
    vb = v.astype(BF)
    M = M_ref[...]
    for s in range(TT // c):
        lo = s * c
        S0 = s0_ref[s].reshape(RET_H * RET_DK, RET_DV)
        Sbd = (_dot(S0.astype(BF), T4_ref[...]) * M).astype(BF)
        oi_ref[lo:lo + c, :] = _dot(qd[lo:lo + c, :], Sbd)
        u = _fold_head_blocks(_dot_tn(kd[lo:lo + c, :], vb[lo:lo + c, :]) * M)
        sn_ref[s] = (G_ref[...] * S0 + u).reshape(RET_H, RET_DK, RET_DV)
    o = o + oi_ref[...]
    o_ref[...] = _ret_norm_gate(o, rg, nw_ref, EA_ref).astype(o_ref.dtype)


def _ret_sample_call(rin, s0, B, T, norm_w):
    TT = SEQ_TILE * T
    cos, sin = _rope_tables(PAST_LEN + jnp.arange(T, dtype=jnp.int32))
    cos, sin = jnp.tile(cos, (SEQ_TILE, 1)), jnp.tile(sin, (SEQ_TILE, 1))
    lg = _ret_log_gamma()
    i = np.arange(TT)
    same = (i[:, None] // T == i[None, :] // T) & (i[:, None] >= i[None, :])
    Dm = jnp.asarray(np.exp(lg[:, None, None] * (i[:, None] - i[None, :])[None]) * same[None], F32)
    tt = i % T
    rd = jnp.asarray(np.repeat(np.exp(lg[None, :] * (tt[:, None] + 1)), RET_DK, 1), F32)
    kd = jnp.asarray(np.repeat(np.exp(lg[None, :] * (T - 1 - tt[:, None])), RET_DK, 1), F32)
    G = jnp.asarray(np.repeat(np.repeat(np.exp(lg * T), RET_DK)[:, None], RET_DV, 1), F32)
    M = jnp.asarray(_head_block_mask(RET_DK, RET_DV, RET_H), F32)
    EA = jnp.asarray(_head_block_mask(RET_DV, RET_DV, RET_H) / RET_DV, BF)
    T4 = jnp.asarray(_tile_lanes(RET_H, RET_DV), BF)
    nw = norm_w.reshape(1, -1)
    sspec = pl.BlockSpec((SEQ_TILE, RET_H, RET_DK, RET_DV), lambda i: (i, 0, 0, 0))
    consts = (cos, sin, Dm, rd, kd, G, M, EA, nw, T4)
    return pl.pallas_call(
        functools.partial(_ret_sample_kernel, c=T),
        grid=(B // SEQ_TILE,),
        in_specs=[pl.BlockSpec((TT, RET_IN_W), lambda i: (i, 0)), sspec] + [_const(a.shape) for a in consts],
        out_specs=[pl.BlockSpec((TT, RET_WIDTH), lambda i: (i, 0)), sspec],
        out_shape=[jax.ShapeDtypeStruct((B * T, RET_WIDTH), BF),
                   jax.ShapeDtypeStruct((B, RET_H, RET_DK, RET_DV), F32)],
        scratch_shapes=[pltpu.VMEM((TT, RET_WIDTH), F32)],
        compiler_params=_cp(1),
        name="ret_sample",
    )(rin, s0, *consts)


def _ssd_sample_kernel(x_ref, c0_ref, s0_ref, cw_ref, cb_ref, dtb_ref, alog_ref, dexp_ref, nw_ref,
                       L_ref, Mk_ref, Eexp_ref, Bl_ref, R2_ref, T8_ref, T8T_ref, M8_ref, ones_ref,
                       o_ref, sn_ref, cn_ref, xp_ref, oi_ref, *, c):
    TT = x_ref.shape[0]
    ns = TT // c
    RP = 2 * c
    xp_ref[...] = jnp.zeros_like(xp_ref)
    z = x_ref[:, 0:SSD_WIDTH]
    sdt = x_ref[:, SSD_WIDTH + SSD_CONV_DIM:SSD_IN_W]
    for s in range(ns):
        base = 8 + s * RP
        xp_ref[base + c - (SSD_CONV_W - 1):base + c, :] = c0_ref[s]
        xp_ref[base + c:base + RP, :] = x_ref[s * c:(s + 1) * c, SSD_WIDTH:SSD_WIDTH + SSD_CONV_DIM]
    conv = _ssd_conv(xp_ref, cw_ref, cb_ref, ns * RP)
    xbc = _silu(conv.reshape(ns, RP, SSD_CONV_DIM)[:, c:RP, :].reshape(TT, SSD_CONV_DIM))
    for s in range(ns):
        base = 8 + s * RP
        cn_ref[s] = xp_ref[base + RP - (SSD_CONV_W - 1):base + RP, :]
    xs = xbc[:, 0:SSD_WIDTH]
    bm = xbc[:, SSD_WIDTH:SSD_WIDTH + LANE]
    cm = xbc[:, SSD_WIDTH + LANE:SSD_CONV_DIM]

    dt = _softplus(sdt + dtb_ref[...])
    la = dt * (-jnp.exp(alog_ref[...]))
    g = _dot_3x(L_ref[...], la)
    gl = _dot_3x(Bl_ref[...], g)
    Eexp = Eexp_ref[...]
    eg_x = _dot_x2(jnp.exp(g), Eexp)
    cw_x = _dot_x2(dt * jnp.exp(gl - g), Eexp)
    egl_x = _dot_x2(jnp.exp(gl), Eexp)
    o = _ssd_intra(xs, bm, cm, g, dt, Mk_ref)

    Cx = _dot(cm.astype(BF), R2_ref[...])
    Bx = _dot(bm.astype(BF), R2_ref[...])
    Xw = xs * cw_x
    M8 = M8_ref[...]
    nh = SSD_H

    def rows_by_head(a):
        return jnp.concatenate([a] * nh, axis=0) * M8

    for s in range(ns):
        lo = s * c
        S0 = s0_ref[s].reshape(SSD_H * SSD_N, SSD_P)
        oi = _dot(rows_by_head(Cx[lo:lo + c, :]).astype(BF), S0.astype(BF))
        oix = _dot_x2(oi, T8_ref[...]) * M8
        acc = oix[0:c, :]
        for h in range(1, nh):
            acc = acc + oix[h * c:(h + 1) * c, :]
        oi_ref[lo:lo + c, :] = acc
        Xst = _dot(rows_by_head(Xw[lo:lo + c, :]).astype(BF), T8T_ref[...])
        u = _dot_tn(rows_by_head(Bx[lo:lo + c, :]).astype(BF), Xst.astype(BF))
        acol = _col_bcast(egl_x[lo:lo + c, :], ones_ref)
        sn_ref[s] = (acol * S0 + u).reshape(SSD_H, SSD_N, SSD_P)

    o = o + eg_x * oi_ref[...]
    y = (o + dexp_ref[...] * xs) * _silu(z)
    ms = jnp.mean(y * y, axis=-1, keepdims=True)
    o_ref[...] = (y * lax.rsqrt(ms + EPS) * nw_ref[...]).astype(o_ref.dtype)


def _ssd_sample_call(sin_, c0, s0, B, T, conv_w, conv_b, dt_bias, a_log, d, norm_w):
    TT = SEQ_TILE * T
    L, Mk, Eexp, _ = _ssd_tables(TT, T)
    i = np.arange(TT)
    Bl = jnp.asarray((i[None, :] == (i[:, None] // T) * T + T - 1).astype(np.float32), BF)
    hpg = SSD_H // SSD_G
    R2 = np.zeros((LANE, SSD_H * SSD_N), np.float32)
    for h in range(SSD_H):
        R2[(h // hpg) * SSD_N:(h // hpg + 1) * SSD_N, h * SSD_N:(h + 1) * SSD_N] = np.eye(SSD_N)
    T8 = _tile_lanes(SSD_H, SSD_P)
    M8 = _head_block_mask(T, SSD_P, SSD_H)
    tabs = (L, Mk, Eexp, Bl, jnp.asarray(R2, BF), jnp.asarray(T8, BF), jnp.asarray(T8.T, BF), jnp.asarray(M8, F32),
            jnp.ones((8, SSD_P), BF))
    prm = _ssd_params(conv_w, conv_b, dt_bias, a_log, d, norm_w)
    sspec = pl.BlockSpec((SEQ_TILE, SSD_H, SSD_N, SSD_P), lambda i: (i, 0, 0, 0))
    cspec = pl.BlockSpec((SEQ_TILE, SSD_CONV_W - 1, SSD_CONV_DIM), lambda i: (i, 0, 0))
    return pl.pallas_call(
        functools.partial(_ssd_sample_kernel, c=T),
        grid=(B // SEQ_TILE,),
        in_specs=[pl.BlockSpec((TT, SSD_IN_W), lambda i: (i, 0)), cspec, sspec]
                 + [_const(p.shape) for p in prm] + [_const(a.shape) for a in tabs],
        out_specs=[pl.BlockSpec((TT, SSD_WIDTH), lambda i: (i, 0)), sspec, cspec],
        out_shape=[jax.ShapeDtypeStruct((B * T, SSD_WIDTH), BF),
                   jax.ShapeDtypeStruct((B, SSD_H, SSD_N, SSD_P), F32),
                   jax.ShapeDtypeStruct((B, SSD_CONV_W - 1, SSD_CONV_DIM), F32)],
        scratch_shapes=[pltpu.VMEM((8 + SEQ_TILE * 2 * T, SSD_CONV_DIM), F32),
                        pltpu.VMEM((TT, SSD_WIDTH), F32)],
        compiler_params=_cp(1),
        name="ssd_sample",
    )(sin_, c0, s0, *prm, *tabs)


def _outproj_kernel(x_ref, g_ref, og_ref, or_ref, os_ref, w_ref, lg_ref, lb_ref, o_ref):
    bB, bT, D = x_ref.shape
    mix = (_dot(og_ref[...], w_ref[0:GLA_WIDTH, :])
           + _dot(or_ref[...], w_ref[GLA_WIDTH:GLA_WIDTH + RET_WIDTH, :])
           + _dot(os_ref[...], w_ref[GLA_WIDTH + RET_WIDTH:D, :]))
    y = ALPHA * x_ref[...] + g_ref[...] * mix.reshape(bB, bT, D)
    o_ref[...] = _layer_norm(y, lg_ref[...], lb_ref[...])


def _outproj_call(x3, g1, og, orr, os_, w_out, ln_g, ln_b):
    B, T, D = x3.shape
    bB, bT = _tok_tiles(B, T)
    nT = T // bT
    R = bB * bT
    xmap = lambda i, j: (i, j, 0)
    mmap = lambda i, j: (i, 0, 0)
    rmap = lambda i, j: (i * nT + j, 0)
    return pl.pallas_call(
        _outproj_kernel,
        grid=(B // bB, nT),
        in_specs=[pl.BlockSpec((bB, bT, D), xmap),
                  pl.BlockSpec((bB, 1, D), mmap),
                  pl.BlockSpec((R, GLA_WIDTH), rmap),
                  pl.BlockSpec((R, RET_WIDTH), rmap),
                  pl.BlockSpec((R, SSD_WIDTH), rmap),
                  _const((D, D)), _const((1, 1, D)), _const((1, 1, D))],
        out_specs=pl.BlockSpec((bB, bT, D), xmap),
        out_shape=jax.ShapeDtypeStruct((B, T, D), F32),
        compiler_params=_cp(2),
        name="out_proj_ln",
    )(x3, g1, og, orr, os_, w_out, ln_g.reshape(1, 1, D), ln_b.reshape(1, 1, D))


ROUTE_OFF = MOE_GROUPS


def _moe_route(logits):
    lane = lax.broadcasted_iota(jnp.int32, (1, LANE), 1)
    big = jnp.int32(LANE)
    neg = jnp.float32(-jnp.inf)
    lg = jnp.where(lane < MOE_GROUPS, logits, neg)
    mg = jnp.max(lg, axis=-1, keepdims=True)
    gsel = jnp.min(jnp.where(lg == mg, lane, big), axis=-1, keepdims=True)
    g_gate = 1.0 / jnp.sum(jnp.exp(lg - mg), axis=-1, keepdims=True)
    in_grp = (lane >= ROUTE_OFF) & (lane < ROUTE_OFF + MOE_EXPERTS) & (((lane - ROUTE_OFF) // MOE_PER_GROUP) == gsel)
    le = jnp.where(in_grp, logits, neg)
    m1 = jnp.max(le, axis=-1, keepdims=True)
    i1 = jnp.min(jnp.where(le == m1, lane, big), axis=-1, keepdims=True)
    le2 = jnp.where(lane == i1, neg, le)
    m2 = jnp.max(le2, axis=-1, keepdims=True)
    i2 = jnp.min(jnp.where(le2 == m2, lane, big), axis=-1, keepdims=True)
    e2 = jnp.exp(m2 - m1)
    w1 = g_gate / (1.0 + e2)
    w2 = g_gate * e2 / (1.0 + e2)
    return jnp.where(lane == i1, w1, jnp.where(lane == i2, w2, 0.0))


def _moe_kernel(x_ref, sc_ref, sh_ref, g_ref, wr_ref, br_ref, w1_ref, w3_ref, w2_ref, lg_ref, lb_ref, o_ref):
    bB, bT, D = x_ref.shape
    R = bB * bT
    x = x_ref[...]
    h = (x * (1.0 + sc_ref[...]) + sh_ref[...]).reshape(R, D)
    hb = h.astype(BF)
    comb = _moe_route(_dot(hb, wr_ref[...]) + br_ref[...])
    y = jnp.zeros((R, D), F32)
    for e in range(MOE_EXPERTS):
        hid = _silu(_dot(hb, w1_ref[e])) * _dot(hb, w3_ref[e])
        hid = hid * comb[:, ROUTE_OFF + e:ROUTE_OFF + e + 1]
        y = y + _dot(hid.astype(BF), w2_ref[e])
    z = ALPHA * x + g_ref[...] * y.reshape(bB, bT, D)
    o_ref[...] = _layer_norm(z, lg_ref[...], lb_ref[...])


def _resident(shape):
    return pl.BlockSpec(shape, lambda *_: (0,) * len(shape), pipeline_mode=pl.Buffered(1))


def _moe_call(x3, sc, sh, g2, wr, br, w1, w3, w2, ln_g, ln_b):
    B, T, D = x3.shape
    bB, bT = _tok_tiles(B, T)
    xmap = lambda i, j: (i, j, 0)
    mmap = lambda i, j: (i, 0, 0)
    return pl.pallas_call(
        _moe_kernel,
        grid=(B // bB, T // bT),
        in_specs=[pl.BlockSpec((bB, bT, D), xmap),
                  pl.BlockSpec((bB, 1, D), mmap), pl.BlockSpec((bB, 1, D), mmap), pl.BlockSpec((bB, 1, D), mmap),
                  _const(wr.shape), _const(br.shape),
                  _resident(w1.shape), _resident(w3.shape), _resident(w2.shape),
                  _const((1, 1, D)), _const((1, 1, D))],
        out_specs=pl.BlockSpec((bB, bT, D), xmap),
        out_shape=jax.ShapeDtypeStruct((B, T, D), F32),
        compiler_params=_cp(2),
        name="moe_ln",
    )(x3, sc, sh, g2, wr, br, w1, w3, w2, ln_g.reshape(1, 1, D), ln_b.reshape(1, 1, D))


def _router_params(w_group, b_group, w_expert, b_expert):
    wr = jnp.zeros((D_MODEL, LANE), F32).at[:, :MOE_GROUPS].set(w_group)
    wr = wr.at[:, ROUTE_OFF:ROUTE_OFF + MOE_EXPERTS].set(w_expert)
    br = jnp.zeros((1, LANE), F32).at[0, :MOE_GROUPS].set(b_group).at[0, ROUTE_OFF:ROUTE_OFF + MOE_EXPERTS].set(b_expert)
    return wr.astype(BF), br


def kernel(x_prompt, x_sample, c_prompt, c_sample, state_gla, state_ret, state_ssd, state_conv, w_ada, b_ada, w_in, gla_w_gate, gla_b_gate, gla_norm, ret_norm, ssd_conv_w, ssd_conv_b, ssd_dt_bias, ssd_a_log, ssd_d, ssd_norm, w_out, ln1_g, ln1_b, moe_w_group, moe_b_group, moe_w_expert, moe_b_expert, moe_w1, moe_w3, moe_w2, ln2_g, ln2_b):
    Bp, Tp, D = x_prompt.shape
    Bs, Ts, _ = x_sample.shape
    n_ga = 128 + 128 + 256 + GLA_GATE_RANK
    w_in_p = jnp.concatenate(
        [w_in[:, :, :n_ga], jnp.zeros((DEPTH, D, LANE - GLA_GATE_RANK), F32), w_in[:, :, n_ga:],
         jnp.zeros((DEPTH, D, LANE - SSD_H), F32)], axis=-1).astype(BF)
    w_out_b = w_out.astype(BF)
    w1_b, w3_b, w2_b = moe_w1.astype(BF), moe_w3.astype(BF), moe_w2.astype(BF)

    mod = _mod_call(jnp.concatenate([c_prompt, c_sample], axis=0), w_ada, b_ada)

    def trunk(x, mod_rows, B, T, states):
        new = [[], [], [], []]
        for l in range(DEPTH):
            sh1, sc1, g1, sh2, sc2, g2 = (mod_rows[l][:, None, i * D:(i + 1) * D] for i in range(6))
            gin, rin, sin_ = _inproj_call(x, sc1, sh1, w_in_p[l])
            if states is None:
                og, s_gla = _gla_prompt_call(gin, B, T, gla_w_gate[l], gla_b_gate[l], gla_norm[l])
                orr, s_ret = _ret_prompt_call(rin, B, T, ret_norm[l])
                os_, s_ssd, s_conv = _ssd_prompt_call(sin_, B, T, ssd_conv_w[l], ssd_conv_b[l], ssd_dt_bias[l],
                                                      ssd_a_log[l], ssd_d[l], ssd_norm[l])
            else:
                og, s_gla = _gla_sample_call(gin, states[0][l], B, T, gla_w_gate[l], gla_b_gate[l], gla_norm[l])
                orr, s_ret = _ret_sample_call(rin, states[1][l], B, T, ret_norm[l])
                os_, s_ssd, s_conv = _ssd_sample_call(sin_, states[3][l], states[2][l], B, T, ssd_conv_w[l],
                                                      ssd_conv_b[l], ssd_dt_bias[l], ssd_a_log[l], ssd_d[l],
                                                      ssd_norm[l])
            x = _outproj_call(x, g1, og, orr, os_, w_out_b[l], ln1_g[l], ln1_b[l])
            wr, br = _router_params(moe_w_group[l], moe_b_group[l], moe_w_expert[l], moe_b_expert[l])
            x = _moe_call(x, sc2, sh2, g2, wr, br, w1_b[l], w3_b[l], w2_b[l], ln2_g[l], ln2_b[l])
            for acc, s in zip(new, (s_gla, s_ret, s_ssd, s_conv)):
                acc.append(s)
        return (x,) + tuple(jnp.stack(a) for a in new)

    y_p, gla_p, ret_p, ssd_p, conv_p = trunk(x_prompt, mod[:, :Bp], Bp, Tp, None)
    y_s, gla_s, ret_s, ssd_s, conv_s = trunk(x_sample, mod[:, Bp:], Bs, Ts,
                                             (state_gla, state_ret, state_ssd, state_conv))
    return (y_p, y_s, gla_p, ret_p, ssd_p, conv_p, gla_s, ret_s, ssd_s, conv_s)
```

```python
import functools
import math

import numpy as np
import jax
import jax.numpy as jnp
from jax import lax
from jax.experimental import pallas as pl
from jax.experimental.pallas import tpu as pltpu

F32 = jnp.float32
BF = jnp.bfloat16

D_MODEL = 1024
DEPTH = 2
PAST_LEN = 16384
GLA_H, GLA_DK, GLA_DV = 4, 32, 64
GLA_WIDTH = GLA_H * GLA_DV
GLA_GATE_RANK = 16
GLA_GATE_TEMP = 16.0
GLA_CHUNK = 16
RET_H, RET_DK, RET_DV = 4, 64, 64
RET_WIDTH = RET_H * RET_DV
ROPE_BASE = 10000.0
SSD_H, SSD_P, SSD_G, SSD_N = 8, 64, 2, 64
SSD_WIDTH = SSD_H * SSD_P
SSD_CONV_W = 4
SSD_CONV_DIM = SSD_WIDTH + 2 * SSD_G * SSD_N
MOE_GROUPS, MOE_PER_GROUP = 4, 4
MOE_EXPERTS = MOE_GROUPS * MOE_PER_GROUP
MOE_FF = 256
ALPHA = (2 * DEPTH) ** 0.25
EPS = 1e-5

LANE = 128
GLA_IN_W = 128 + 128 + 256 + LANE + 256
RET_IN_W = 4 * 256
SSD_IN_W = 512 + SSD_CONV_DIM + LANE
IN_W = GLA_IN_W + RET_IN_W + SSD_IN_W
VMEM_LIMIT = 56 * 1024 * 1024


def _cp(n_axes, vmem=VMEM_LIMIT):
    return pltpu.CompilerParams(dimension_semantics=("arbitrary",) * n_axes, vmem_limit_bytes=vmem)


def _dot(a, b):
    return jnp.dot(a, b, preferred_element_type=F32)


def _dot_nt(a, b):
    return lax.dot_general(a, b, (((1,), (1,)), ((), ())), preferred_element_type=F32)


def _dot_tn(a, b):
    return lax.dot_general(a, b, (((0,), (0,)), ((), ())), preferred_element_type=F32)


def _split3(x):
    hi = x.astype(BF)
    r = x - hi.astype(F32)
    mid = r.astype(BF)
    lo = (r - mid.astype(F32)).astype(BF)
    return hi, mid, lo


def _dot_x3(x, e):
    hi, mid, lo = _split3(x)
    return _dot(hi, e) + (_dot(mid, e) + _dot(lo, e))


def _dot_x2(x, e):
    hi = x.astype(BF)
    lo = (x - hi.astype(F32)).astype(BF)
    return _dot(hi, e) + _dot(lo, e)


def _dot_3x(e, x):
    hi, mid, lo = _split3(x)
    return _dot(e, hi) + (_dot(e, mid) + _dot(e, lo))


def _sigmoid(x):
    return 1.0 / (1.0 + jnp.exp(-x))


def _silu(x):
    return x * _sigmoid(x)


def _log_sigmoid(x):
    return jnp.minimum(x, 0.0) - jnp.log(1.0 + jnp.exp(-jnp.abs(x)))


def _softplus(x):
    return jnp.maximum(x, 0.0) + jnp.log(1.0 + jnp.exp(-jnp.abs(x)))


def _layer_norm(x, g, b):
    mu = jnp.mean(x, axis=-1, keepdims=True)
    d = x - mu
    var = jnp.mean(d * d, axis=-1, keepdims=True)
    return d * lax.rsqrt(var + EPS) * g + b


def _mod_kernel(c_ref, w_ref, b_ref, o_ref):
    s = _silu(c_ref[...]).astype(BF)
    o_ref[0] = _dot(s, w_ref[0].astype(BF)) + b_ref[0]


def _mod_call(c_all, w_ada, b_ada):
    R = c_all.shape[0]
    tn = 1536
    return pl.pallas_call(
        _mod_kernel,
        grid=(DEPTH, 6 * D_MODEL // tn),
        in_specs=[pl.BlockSpec((R, D_MODEL), lambda l, j: (0, 0)),
                  pl.BlockSpec((1, D_MODEL, tn), lambda l, j: (l, 0, j)),
                  pl.BlockSpec((1, 1, tn), lambda l, j: (l, 0, j))],
        out_specs=pl.BlockSpec((1, R, tn), lambda l, j: (l, 0, j)),
        out_shape=jax.ShapeDtypeStruct((DEPTH, R, 6 * D_MODEL), F32),
        compiler_params=_cp(2),
        name="ada_mod",
    )(c_all, w_ada, b_ada.reshape(DEPTH, 1, 6 * D_MODEL))


def _inproj_kernel(x_ref, sc_ref, sh_ref, w_ref, og_ref, or_ref, os_ref):
    bB, bT, D = x_ref.shape
    h = x_ref[...] * (1.0 + sc_ref[...]) + sh_ref[...]
    hb = h.reshape(bB * bT, D).astype(BF)
    og_ref[...] = _dot(hb, w_ref[:, 0:GLA_IN_W])
    or_ref[...] = _dot(hb, w_ref[:, GLA_IN_W:GLA_IN_W + RET_IN_W])
    os_ref[...] = _dot(hb, w_ref[:, GLA_IN_W + RET_IN_W:IN_W])


def _tok_tiles(B, T):
    if T >= 512:
        return 1, 512
    return 512 // T, T


def _inproj_call(x3, sc, sh, w_p):
    B, T, D = x3.shape
    bB, bT = _tok_tiles(B, T)
    nT = T // bT
    R = bB * bT
    N = B * T
    xmap = lambda i, j: (i, j, 0)
    mmap = lambda i, j: (i, 0, 0)
    omap = lambda i, j: (i * nT + j, 0)
    return pl.pallas_call(
        _inproj_kernel,
        grid=(B // bB, nT),
        in_specs=[pl.BlockSpec((bB, bT, D), xmap),
                  pl.BlockSpec((bB, 1, D), mmap),
                  pl.BlockSpec((bB, 1, D), mmap),
                  pl.BlockSpec((D, IN_W), lambda i, j: (0, 0))],
        out_specs=[pl.BlockSpec((R, GLA_IN_W), omap),
                   pl.BlockSpec((R, RET_IN_W), omap),
                   pl.BlockSpec((R, SSD_IN_W), omap)],
        out_shape=[jax.ShapeDtypeStruct((N, GLA_IN_W), F32),
                   jax.ShapeDtypeStruct((N, RET_IN_W), F32),
                   jax.ShapeDtypeStruct((N, SSD_IN_W), F32)],
        compiler_params=_cp(2),
        name="in_proj",
    )(x3, sc, sh, w_p)


def _head_block_mask(rows_per, cols_per, n):
    r = np.arange(rows_per * n)[:, None] // rows_per
    c = np.arange(cols_per * n)[None, :] // cols_per
    return (r == c).astype(np.float32)


def _block_tril(n, c):
    i = np.arange(n)[:, None]
    j = np.arange(n)[None, :]
    return ((i // c == j // c) & (j <= i)).astype(np.float32)


def _gla_front(x_ref, wg_ref, bg_ref, L_ref):
    q = x_ref[:, 0:128] * (GLA_DK ** -0.5)
    k = x_ref[:, 128:256]
    v = x_ref[:, 256:512]
    ga = x_ref[:, 512:640]
    r = x_ref[:, 640:896]
    gate = _dot(ga.astype(BF), wg_ref[...]) + bg_ref[...]
    la = _log_sigmoid(gate) * (1.0 / GLA_GATE_TEMP)
    g = _dot_3x(L_ref[...], la)
    return q, k, v, r, g


def _gla_intra(q, g, kp_ref, gp_ref, vp_ref, E_ref, c):
    TT = q.shape[0]
    PAD = kp_ref.shape[0] - TT
    pos = lax.broadcasted_iota(jnp.int32, (TT, 1), 0) & (c - 1)
    o = jnp.zeros((TT, 2 * LANE), F32)
    for s in range(c):
        ks = kp_ref[pl.ds(PAD - s, TT), :]
        gs = gp_ref[pl.ds(PAD - s, TT), :]
        vs = vp_ref[pl.ds(PAD - s, TT), :]
        w = jnp.where(pos >= s, q * ks * jnp.exp(jnp.minimum(g - gs, 0.0)), 0.0)
        o = o + _dot(w.astype(BF), E_ref[...]) * vs
    return o


def _gla_norm_gate(o, r, nw_ref, EA_ref):
    ms = _dot_x3(o * o, EA_ref[...])
    return o * lax.rsqrt(ms + EPS) * nw_ref[...] * _silu(r)


def _gla_prompt_kernel(x_ref, wg_ref, bg_ref, nw_ref, L_ref, E_ref, EA_ref, M_ref,
                       o_ref, sfin_ref, st_ref, kp_ref, gp_ref, vp_ref, oi_ref, u_ref, sb_ref, *, c):
    TT = x_ref.shape[0]
    nc = TT // c
    PAD = kp_ref.shape[0] - TT
    t = pl.program_id(1)

    @pl.when(t == 0)
    def _():
        st_ref[...] = jnp.zeros_like(st_ref)

    q, k, v, r, g = _gla_front(x_ref, wg_ref, bg_ref, L_ref)
    kp_ref[0:PAD, :] = jnp.zeros((PAD, LANE), F32)
    gp_ref[0:PAD, :] = jnp.zeros((PAD, LANE), F32)
    vp_ref[0:PAD, :] = jnp.zeros((PAD, 2 * LANE), F32)
    kp_ref[PAD:PAD + TT, :] = k
    gp_ref[PAD:PAD + TT, :] = g
    vp_ref[PAD:PAD + TT, :] = v
    o = _gla_intra(q, g, kp_ref, gp_ref, vp_ref, E_ref, c)

    M = M_ref[...]
    gl_all = gp_ref[pl.ds(PAD + c - 1, nc, stride=c), :]
    for n in range(nc):
        lo = n * c
        ke = (k[lo:lo + c, :] * jnp.exp(gl_all[n:n + 1, :] - g[lo:lo + c, :])).astype(BF)
        u_ref[n] = _dot_tn(ke, v[lo:lo + c, :].astype(BF)) * M
    a_cols = jnp.concatenate([jnp.exp(gl_all), jnp.zeros((LANE - nc, LANE), F32)], axis=0).T
    S = st_ref[...]
    for n in range(nc):
        sb_ref[n] = S.astype(BF)
        S = a_cols[:, n:n + 1] * S + u_ref[n]
    st_ref[...] = S
    qe = (q * jnp.exp(g)).astype(BF)
    for n in range(nc):
        lo = n * c
        oi_ref[lo:lo + c, :] = _dot(qe[lo:lo + c, :], sb_ref[n])
    o = o + oi_ref[...]
    o_ref[...] = _gla_norm_gate(o, r, nw_ref, EA_ref).astype(o_ref.dtype)

    @pl.when(t == pl.num_programs(1) - 1)
    def _():
        for h in range(GLA_H):
            sfin_ref[0, h] = S[h * GLA_DK:(h + 1) * GLA_DK, h * GLA_DV:(h + 1) * GLA_DV]


def _gla_tables(TT, c):
    L = jnp.asarray(_block_tril(TT, c), BF)
    E = jnp.asarray(_head_block_mask(GLA_DK, GLA_DV, GLA_H), BF)
    EA = jnp.asarray(_head_block_mask(GLA_DV, GLA_DV, GLA_H) / GLA_DV, BF)
    M = jnp.asarray(_head_block_mask(GLA_DK, GLA_DV, GLA_H), F32)
    return L, E, EA, M


def _gla_params(w_gate, b_gate, norm_w):
    wg = jnp.zeros((LANE, GLA_H * GLA_DK), F32).at[:GLA_GATE_RANK].set(w_gate).astype(BF)
    return wg, b_gate.reshape(1, -1), norm_w.reshape(1, -1)


def _const(shape):
    return pl.BlockSpec(shape, lambda *_: (0,) * len(shape))


def _gla_prompt_call(gin, B, T, w_gate, b_gate, norm_w):
    TT, c = 256, GLA_CHUNK
    nT = T // TT
    L, E, EA, M = _gla_tables(TT, c)
    wg, bg, nw = _gla_params(w_gate, b_gate, norm_w)
    PAD = 16
    return pl.pallas_call(
        functools.partial(_gla_prompt_kernel, c=c),
        grid=(B, nT),
        in_specs=[pl.BlockSpec((TT, GLA_IN_W), lambda b, t: (b * nT + t, 0)),
                  _const(wg.shape), _const(bg.shape), _const(nw.shape),
                  _const(L.shape), _const(E.shape), _const(EA.shape), _const(M.shape)],
        out_specs=[pl.BlockSpec((TT, GLA_WIDTH), lambda b, t: (b * nT + t, 0)),
                   pl.BlockSpec((1, GLA_H, GLA_DK, GLA_DV), lambda b, t: (b, 0, 0, 0))],
        out_shape=[jax.ShapeDtypeStruct((B * T, GLA_WIDTH), BF),
                   jax.ShapeDtypeStruct((B, GLA_H, GLA_DK, GLA_DV), F32)],
        scratch_shapes=[pltpu.VMEM((GLA_H * GLA_DK, GLA_H * GLA_DV), F32),
                        pltpu.VMEM((TT + PAD, LANE), F32),
                        pltpu.VMEM((TT + PAD, LANE), F32),
                        pltpu.VMEM((TT + PAD, 2 * LANE), F32),
                        pltpu.VMEM((TT, 2 * LANE), F32),
                        pltpu.VMEM((TT // c, GLA_H * GLA_DK, GLA_H * GLA_DV), F32),
                        pltpu.VMEM((TT // c, GLA_H * GLA_DK, GLA_H * GLA_DV), BF)],
        compiler_params=_cp(2),
        name="gla_prompt",
    )(gin, wg, bg, nw, L, E, EA, M)


def _rope(x, cos, sin_signed):
    lane = lax.broadcasted_iota(jnp.int32, (1, LANE), 1)
    first_half = (lane & (RET_DK - 1)) < RET_DK // 2
    out = []
    for p in range(2):
        xs = x[:, p * LANE:(p + 1) * LANE]
        up = pltpu.roll(xs, LANE - RET_DK // 2, 1)
        dn = pltpu.roll(xs, RET_DK // 2, 1)
        out.append(xs * cos + jnp.where(first_half, up, dn) * sin_signed)
    return jnp.concatenate(out, axis=1)


def _ret_front(x_ref, cos_ref, sin_ref):
    q = _rope(x_ref[:, 0:256], cos_ref[...], sin_ref[...])
    k = _rope(x_ref[:, 256:512], cos_ref[...], sin_ref[...]) * (RET_DK ** -0.5)
    v = x_ref[:, 512:768]
    rg = x_ref[:, 768:1024]
    return q, k, v, rg


def _ret_intra(q, k, v, D_ref):
    lane = lax.broadcasted_iota(jnp.int32, (1, RET_WIDTH), 1)
    kb = k.astype(BF)
    o = jnp.zeros(q.shape, F32)
    for h in range(RET_H):
        hm = (lane // RET_DK) == h
        s = _dot_nt(jnp.where(hm, q, 0.0).astype(BF), kb)
        p = (s * D_ref[h]).astype(BF)
        o = o + _dot(p, jnp.where(hm, v, 0.0).astype(BF))
    return o


def _ret_norm_gate(o, rg, nw_ref, EA_ref):
    mu = _dot_x3(o, EA_ref[...])
    d = o - mu
    var = _dot_x3(d * d, EA_ref[...])
    return d * lax.rsqrt(var + EPS) * nw_ref[...] * _silu(rg)


def _ret_prompt_kernel(x_ref, cos_ref, sin_ref, D_ref, rd_ref, kd_ref, G_ref, M_ref, EA_ref, nw_ref,
                       o_ref, sfin_ref, st_ref):
    t = pl.program_id(1)

    @pl.when(t == 0)
    def _():
        st_ref[...] = jnp.zeros_like(st_ref)

    q, k, v, rg = _ret_front(x_ref, cos_ref, sin_ref)
    o = _ret_intra(q, k, v, D_ref)
    S = st_ref[...]
    o = o + _dot((q * rd_ref[...]).astype(BF), S.astype(BF))
    u = _dot_tn((k * kd_ref[...]).astype(BF), v.astype(BF))
    S = S * G_ref[...] + u * M_ref[...]
    st_ref[...] = S
    o_ref[...] = _ret_norm_gate(o, rg, nw_ref, EA_ref).astype(o_ref.dtype)

    @pl.when(t == pl.num_programs(1) - 1)
    def _():
        for h in range(RET_H):
            sfin_ref[0, h] = S[h * RET_DK:(h + 1) * RET_DK, h * RET_DV:(h + 1) * RET_DV]


def _rope_tables(pos):
    half = RET_DK // 2
    inv = ROPE_BASE ** (-jnp.arange(half, dtype=F32) / half)
    ang = pos.astype(F32)[:, None] * inv[None, :]
    cos, sin = jnp.cos(ang), jnp.sin(ang)
    return jnp.tile(jnp.concatenate([cos, cos], 1), (1, 2)), jnp.tile(jnp.concatenate([-sin, sin], 1), (1, 2))


def _ret_log_gamma():
    return np.log(1.0 - 2.0 ** (-5.0 - np.arange(RET_H, dtype=np.float64)))


def _ret_prompt_call(rin, B, T, norm_w):
    TT = 256
    nT = T // TT
    cos, sin = _rope_tables(jnp.arange(T, dtype=jnp.int32))
    lg = _ret_log_gamma()
    i = np.arange(TT)
    dec = np.exp(lg[:, None, None] * (i[:, None] - i[None, :])[None]) * (i[:, None] >= i[None, :])[None]
    Dm = jnp.asarray(dec, F32)
    rd = jnp.asarray(np.repeat(np.exp(lg[None, :] * (i[:, None] + 1)), RET_DK, 1), F32)
    kd = jnp.asarray(np.repeat(np.exp(lg[None, :] * (TT - 1 - i[:, None])), RET_DK, 1), F32)
    M = _head_block_mask(RET_DK, RET_DV, RET_H)
    G = jnp.asarray(M * np.repeat(np.exp(lg * TT), RET_DK)[:, None], F32)
    M = jnp.asarray(M, F32)
    EA = jnp.asarray(_head_block_mask(RET_DV, RET_DV, RET_H) / RET_DV, BF)
    nw = norm_w.reshape(1, -1)
    return pl.pallas_call(
        _ret_prompt_kernel,
        grid=(B, nT),
        in_specs=[pl.BlockSpec((TT, RET_IN_W), lambda b, t: (b * nT + t, 0)),
                  pl.BlockSpec((TT, LANE), lambda b, t: (t, 0)),
                  pl.BlockSpec((TT, LANE), lambda b, t: (t, 0)),
                  _const(Dm.shape), _const(rd.shape), _const(kd.shape), _const(G.shape), _const(M.shape),
                  _const(EA.shape), _const(nw.shape)],
        out_specs=[pl.BlockSpec((TT, RET_WIDTH), lambda b, t: (b * nT + t, 0)),
                   pl.BlockSpec((1, RET_H, RET_DK, RET_DV), lambda b, t: (b, 0, 0, 0))],
        out_shape=[jax.ShapeDtypeStruct((B * T, RET_WIDTH), BF),
                   jax.ShapeDtypeStruct((B, RET_H, RET_DK, RET_DV), F32)],
        scratch_shapes=[pltpu.VMEM((RET_H * RET_DK, RET_H * RET_DV), F32)],
        compiler_params=_cp(2),
        name="ret_prompt",
    )(rin, cos, sin, Dm, rd, kd, G, M, EA, nw)


def _ssd_conv(xp_ref, cw_ref, cb_ref, TT):
    acc = cb_ref[...] + cw_ref[SSD_CONV_W - 1:SSD_CONV_W, :] * xp_ref[pl.ds(8, TT), :]
    for i in range(SSD_CONV_W - 1):
        acc = acc + cw_ref[i:i + 1, :] * xp_ref[pl.ds(8 - (SSD_CONV_W - 1) + i, TT), :]
    return acc


def _ssd_intra(xs, bm, cm, g, dt, Mk_ref):
    TT = xs.shape[0]
    gT = g.T
    dtT = dt.T
    lane = lax.broadcasted_iota(jnp.int32, (1, LANE), 1)
    lane2 = lax.broadcasted_iota(jnp.int32, (1, 2 * LANE), 1)
    causal = Mk_ref[...] > 0.0
    bmb = bm.astype(BF)
    o_parts = []
    for grp in range(SSD_G):
        cb = _dot_nt(jnp.where((lane // SSD_N) == grp, cm, 0.0).astype(BF), bmb)
        xg = xs[:, grp * 2 * LANE:(grp + 1) * 2 * LANE]
        og = jnp.zeros((TT, 2 * LANE), F32)
        for h4 in range(SSD_H // SSD_G):
            h = grp * (SSD_H // SSD_G) + h4
            dec = jnp.where(causal, jnp.exp(jnp.minimum(g[:, h:h + 1] - gT[h:h + 1, :], 0.0)), 0.0)
            p = (cb * dec * dtT[h:h + 1, :]).astype(BF)
            og = og + _dot(p, jnp.where((lane2 // SSD_P) == h4, xg, 0.0).astype(BF))
        o_parts.append(og)
    return jnp.concatenate(o_parts, axis=1)


def _ssd_prompt_kernel(x_ref, cw_ref, cb_ref, dtb_ref, alog_ref, dexp_ref, nw_ref, L_ref, Mk_ref, Eexp_ref, M2_ref,
                       o_ref, sfin_ref, cfin_ref, st_ref, xp_ref):
    TT = x_ref.shape[0]
    t = pl.program_id(1)

    @pl.when(t == 0)
    def _():
        st_ref[...] = jnp.zeros_like(st_ref)
        xp_ref[0:8, :] = jnp.zeros((8, SSD_CONV_DIM), F32)

    z = x_ref[:, 0:SSD_WIDTH]
    xp_ref[8:8 + TT, :] = x_ref[:, SSD_WIDTH:SSD_WIDTH + SSD_CONV_DIM]
    sdt = x_ref[:, SSD_WIDTH + SSD_CONV_DIM:SSD_IN_W]
    xbc = _silu(_ssd_conv(xp_ref, cw_ref, cb_ref, TT))
    tail = xp_ref[TT:TT + 8, :]
    xp_ref[0:8, :] = tail
    xs = xbc[:, 0:SSD_WIDTH]
    bm = xbc[:, SSD_WIDTH:SSD_WIDTH + LANE]
    cm = xbc[:, SSD_WIDTH + LANE:SSD_CONV_DIM]

    dt = _softplus(sdt + dtb_ref[...])
    la = dt * (-jnp.exp(alog_ref[...]))
    g = _dot_3x(L_ref[...], la)
    gl = g[TT - 1:TT, :]
    Eexp = Eexp_ref[...]
    eg_x = _dot_x2(jnp.exp(g), Eexp)
    cw_x = _dot_x2(dt * jnp.exp(gl - g), Eexp)
    egl_x = _dot_x2(jnp.exp(gl), Eexp)

    o = _ssd_intra(xs, bm, cm, g, dt, Mk_ref)

    S = st_ref[...]
    o = o + eg_x * _dot(cm.astype(BF), S.astype(BF))
    u = _dot_tn(bm.astype(BF), (xs * cw_x).astype(BF))
    S = S * egl_x + u * M2_ref[...]
    st_ref[...] = S

    y = (o + dexp_ref[...] * xs) * _silu(z)
    ms = jnp.mean(y * y, axis=-1, keepdims=True)
    o_ref[...] = (y * lax.rsqrt(ms + EPS) * nw_ref[...]).astype(o_ref.dtype)

    @pl.when(t == pl.num_programs(1) - 1)
    def _():
        for h in range(SSD_H):
            gi = h // (SSD_H // SSD_G)
            sfin_ref[0, h] = S[gi * SSD_N:(gi + 1) * SSD_N, h * SSD_P:(h + 1) * SSD_P]
        cfin_ref[0] = tail[8 - (SSD_CONV_W - 1):8, :]


def _pad_lanes(v, n=LANE):
    v = v.reshape(1, -1)
    return jnp.zeros((1, n), F32).at[:, :v.shape[1]].set(v)


def _ssd_tables(TT, c):
    L = jnp.asarray(_block_tril(TT, c), BF)
    Mk = jnp.asarray(_block_tril(TT, c), F32)
    e = np.zeros((LANE, SSD_WIDTH), np.float32)
    for h in range(SSD_H):
        e[h, h * SSD_P:(h + 1) * SSD_P] = 1.0
    M2 = np.zeros((SSD_G * SSD_N, SSD_WIDTH), np.float32)
    for h in range(SSD_H):
        gi = h // (SSD_H // SSD_G)
        M2[gi * SSD_N:(gi + 1) * SSD_N, h * SSD_P:(h + 1) * SSD_P] = 1.0
    return L, Mk, jnp.asarray(e, BF), jnp.asarray(M2, F32)


def _ssd_params(conv_w, conv_b, dt_bias, a_log, d, norm_w):
    return (conv_w, conv_b.reshape(1, -1), _pad_lanes(dt_bias), _pad_lanes(a_log),
            jnp.repeat(d, SSD_P).reshape(1, -1), norm_w.reshape(1, -1))


def _ssd_prompt_call(sin_, B, T, conv_w, conv_b, dt_bias, a_log, d, norm_w):
    TT = 256
    nT = T // TT
    L, Mk, Eexp, M2 = _ssd_tables(TT, TT)
    prm = _ssd_params(conv_w, conv_b, dt_bias, a_log, d, norm_w)
    return pl.pallas_call(
        _ssd_prompt_kernel,
        grid=(B, nT),
        in_specs=[pl.BlockSpec((TT, SSD_IN_W), lambda b, t: (b * nT + t, 0))]
                 + [_const(p.shape) for p in prm]
                 + [_const(L.shape), _const(Mk.shape), _const(Eexp.shape), _const(M2.shape)],
        out_specs=[pl.BlockSpec((TT, SSD_WIDTH), lambda b, t: (b * nT + t, 0)),
                   pl.BlockSpec((1, SSD_H, SSD_N, SSD_P), lambda b, t: (b, 0, 0, 0)),
                   pl.BlockSpec((1, SSD_CONV_W - 1, SSD_CONV_DIM), lambda b, t: (b, 0, 0))],
        out_shape=[jax.ShapeDtypeStruct((B * T, SSD_WIDTH), BF),
                   jax.ShapeDtypeStruct((B, SSD_H, SSD_N, SSD_P), F32),
                   jax.ShapeDtypeStruct((B, SSD_CONV_W - 1, SSD_CONV_DIM), F32)],
        scratch_shapes=[pltpu.VMEM((SSD_G * SSD_N, SSD_WIDTH), F32),
                        pltpu.VMEM((TT + 8, SSD_CONV_DIM), F32)],
        compiler_params=_cp(2),
        name="ssd_prompt",
    )(sin_, *prm, L, Mk, Eexp, M2)


SEQ_TILE = 8


def _tile_lanes(n_rep, width):
    return np.tile(np.eye(width, dtype=np.float32), (1, n_rep))


def _fold_head_blocks(ubd):
    a = ubd[:, 0:LANE] + ubd[:, LANE:2 * LANE]
    return (a + pltpu.roll(a, LANE // 2, 1))[:, 0:LANE // 2]


def _col_bcast(row8, ones_ref):
    first = lax.broadcasted_iota(jnp.int32, (8, 1), 0) == 0
    hi, mid, lo = _split3(jnp.where(first, row8, 0.0))
    ones = ones_ref[...]
    return _dot_tn(hi, ones) + (_dot_tn(mid, ones) + _dot_tn(lo, ones))


def _gla_sample_kernel(x_ref, s0_ref, wg_ref, bg_ref, nw_ref, L_ref, E_ref, EA_ref, M_ref, T4_ref, ones_ref,
                       o_ref, sn_ref, kp_ref, gp_ref, vp_ref, oi_ref, *, c):
    TT = x_ref.shape[0]
    PAD = kp_ref.shape[0] - TT
    q, k, v, r, g = _gla_front(x_ref, wg_ref, bg_ref, L_ref)
    kp_ref[0:PAD, :] = jnp.zeros((PAD, LANE), F32)
    gp_ref[0:PAD, :] = jnp.zeros((PAD, LANE), F32)
    vp_ref[0:PAD, :] = jnp.zeros((PAD, 2 * LANE), F32)
    kp_ref[PAD:PAD + TT, :] = k
    gp_ref[PAD:PAD + TT, :] = g
    vp_ref[PAD:PAD + TT, :] = v
    o = _gla_intra(q, g, kp_ref, gp_ref, vp_ref, E_ref, c)
    qe = (q * jnp.exp(g)).astype(BF)
    M = M_ref[...]
    for s in range(TT // c):
        lo = s * c
        S0 = s0_ref[s].reshape(GLA_H * GLA_DK, GLA_DV)
        Sbd = (_dot(S0.astype(BF), T4_ref[...]) * M).astype(BF)
        oi_ref[lo:lo + c, :] = _dot(qe[lo:lo + c, :], Sbd)
        gl = g[lo + c - 1:lo + c, :]
        ke = (k[lo:lo + c, :] * jnp.exp(gl - g[lo:lo + c, :])).astype(BF)
        u = _fold_head_blocks(_dot_tn(ke, v[lo:lo + c, :].astype(BF)) * M)
        acol = _col_bcast(jnp.broadcast_to(jnp.exp(gl), (8, LANE)), ones_ref)
        sn_ref[s] = (acol * S0 + u).reshape(GLA_H, GLA_DK, GLA_DV)
    o = o + oi_ref[...]
    o_ref[...] = _gla_norm_gate(o, r, nw_ref, EA_ref).astype(o_ref.dtype)


def _gla_sample_call(gin, s0, B, T, w_gate, b_gate, norm_w):
    TT = SEQ_TILE * T
    L, E, EA, _ = _gla_tables(TT, T)
    M = jnp.asarray(_head_block_mask(GLA_DK, GLA_DV, GLA_H), F32)
    T4 = jnp.asarray(_tile_lanes(GLA_H, GLA_DV), BF)
    ones = jnp.ones((8, GLA_DV), BF)
    wg, bg, nw = _gla_params(w_gate, b_gate, norm_w)
    PAD = 8
    sspec = pl.BlockSpec((SEQ_TILE, GLA_H, GLA_DK, GLA_DV), lambda i: (i, 0, 0, 0))
    return pl.pallas_call(
        functools.partial(_gla_sample_kernel, c=T),
        grid=(B // SEQ_TILE,),
        in_specs=[pl.BlockSpec((TT, GLA_IN_W), lambda i: (i, 0)), sspec,
                  _const(wg.shape), _const(bg.shape), _const(nw.shape),
                  _const(L.shape), _const(E.shape), _const(EA.shape), _const(M.shape), _const(T4.shape),
                  _const(ones.shape)],
        out_specs=[pl.BlockSpec((TT, GLA_WIDTH), lambda i: (i, 0)), sspec],
        out_shape=[jax.ShapeDtypeStruct((B * T, GLA_WIDTH), BF),
                   jax.ShapeDtypeStruct((B, GLA_H, GLA_DK, GLA_DV), F32)],
        scratch_shapes=[pltpu.VMEM((TT + PAD, LANE), F32),
                        pltpu.VMEM((TT + PAD, LANE), F32),
                        pltpu.VMEM((TT + PAD, 2 * LANE), F32),
                        pltpu.VMEM((TT, 2 * LANE), F32)],
        compiler_params=_cp(1),
        name="gla_sample",
    )(gin, s0, wg, bg, nw, L, E, EA, M, T4, ones)


def _ret_sample_kernel(x_ref, s0_ref, cos_ref, sin_ref, D_ref, rd_ref, kd_ref, G_ref, M_ref, EA_ref, nw_ref, T4_ref,
                       o_ref, sn_ref, oi_ref, *, c):
    TT = x_ref.shape[0]
    q, k, v, rg = _ret_front(x_ref, cos_ref, sin_ref)
    o = _ret_intra(q, k, v, D_ref)
    qd = (q * rd_ref[...]).astype(BF)
    kd = (k * kd_ref[...]).astype(BF)
    vb = v.astype(BF)
    M = M_ref[...]
    for s in range(TT // c):
        lo = s * c
        S0 = s0_ref[s].reshape(RET_H * RET_DK, RET_DV)
        Sbd = (_dot(S0.astype(BF), T4_ref[...]) * M).astype(BF)
        oi_ref[lo:lo + c, :] = _dot(qd[lo:lo + c, :], Sbd)
        u = _fold_head_blocks(_dot_tn(kd[lo:lo + c, :], vb[lo:lo + c, :]) * M)
        sn_ref[s] = (G_ref[...] * S0 + u).reshape(RET_H, RET_DK, RET_DV)
    o = o + oi_ref[...]
    o_ref[...] = _ret_norm_gate(o, rg, nw_ref, EA_ref).astype(o_ref.dtype)


def _ret_sample_call(rin, s0, B, T, norm_w):
    TT = SEQ_TILE * T
    cos, sin = _rope_tables(PAST_LEN + jnp.arange(T, dtype=jnp.int32))
    cos, sin = jnp.tile(cos, (SEQ_TILE, 1)), jnp.tile(sin, (SEQ_TILE, 1))
    lg = _ret_log_gamma()
    i = np.arange(TT)
    same = (i[:, None] // T == i[None, :] // T) & (i[:, None] >= i[None, :])
    Dm = jnp.asarray(np.exp(lg[:, None, None] * (i[:, None] - i[None, :])[None]) * same[None], F32)
    tt = i % T
    rd = jnp.asarray(np.repeat(np.exp(lg[None, :] * (tt[:, None] + 1)), RET_DK, 1), F32)
    kd = jnp.asarray(np.repeat(np.exp(lg[None, :] * (T - 1 - tt[:, None])), RET_DK, 1), F32)
    G = jnp.asarray(np.repeat(np.repeat(np.exp(lg * T), RET_DK)[:, None], RET_DV, 1), F32)
    M = jnp.asarray(_head_block_mask(RET_DK, RET_DV, RET_H), F32)
    EA = jnp.asarray(_head_block_mask(RET_DV, RET_DV, RET_H) / RET_DV, BF)
    T4 = jnp.asarray(_tile_lanes(RET_H, RET_DV), BF)
    nw = norm_w.reshape(1, -1)
    sspec = pl.BlockSpec((SEQ_TILE, RET_H, RET_DK, RET_DV), lambda i: (i, 0, 0, 0))
    consts = (cos, sin, Dm, rd, kd, G, M, EA, nw, T4)
    return pl.pallas_call(
        functools.partial(_ret_sample_kernel, c=T),
        grid=(B // SEQ_TILE,),
        in_specs=[pl.BlockSpec((TT, RET_IN_W), lambda i: (i, 0)), sspec] + [_const(a.shape) for a in consts],
        out_specs=[pl.BlockSpec((TT, RET_WIDTH), lambda i: (i, 0)), sspec],
        out_shape=[jax.ShapeDtypeStruct((B * T, RET_WIDTH), BF),
                   jax.ShapeDtypeStruct((B, RET_H, RET_DK, RET_DV), F32)],
        scratch_shapes=[pltpu.VMEM((TT, RET_WIDTH), F32)],
        compiler_params=_cp(1),
        name="ret_sample",
    )(rin, s0, *consts)


def _ssd_sample_kernel(x_ref, c0_ref, s0_ref, cw_ref, cb_ref, dtb_ref, alog_ref, dexp_ref, nw_ref,
                       L_ref, Mk_ref, Eexp_ref, Bl_ref, R2_ref, T8_ref, T8T_ref, M8_ref, ones_ref,
                       o_ref, sn_ref, cn_ref, xp_ref, oi_ref, *, c):
    TT = x_ref.shape[0]
    ns = TT // c
    RP = 2 * c
    xp_ref[...] = jnp.zeros_like(xp_ref)
    z = x_ref[:, 0:SSD_WIDTH]
    sdt = x_ref[:, SSD_WIDTH + SSD_CONV_DIM:SSD_IN_W]
    for s in range(ns):
        base = 8 + s * RP
        xp_ref[base + c - (SSD_CONV_W - 1):base + c, :] = c0_ref[s]
        xp_ref[base + c:base + RP, :] = x_ref[s * c:(s + 1) * c, SSD_WIDTH:SSD_WIDTH + SSD_CONV_DIM]
    conv = _ssd_conv(xp_ref, cw_ref, cb_ref, ns * RP)
    xbc = _silu(conv.reshape(ns, RP, SSD_CONV_DIM)[:, c:RP, :].reshape(TT, SSD_CONV_DIM))
    for s in range(ns):
        base = 8 + s * RP
        cn_ref[s] = xp_ref[base + RP - (SSD_CONV_W - 1):base + RP, :]
    xs = xbc[:, 0:SSD_WIDTH]
    bm = xbc[:, SSD_WIDTH:SSD_WIDTH + LANE]
    cm = xbc[:, SSD_WIDTH + LANE:SSD_CONV_DIM]

    dt = _softplus(sdt + dtb_ref[...])
    la = dt * (-jnp.exp(alog_ref[...]))
    g = _dot_3x(L_ref[...], la)
    gl = _dot_3x(Bl_ref[...], g)
    Eexp = Eexp_ref[...]
    eg_x = _dot_x2(jnp.exp(g), Eexp)
    cw_x = _dot_x2(dt * jnp.exp(gl - g), Eexp)
    egl_x = _dot_x2(jnp.exp(gl), Eexp)
    o = _ssd_intra(xs, bm, cm, g, dt, Mk_ref)

    Cx = _dot(cm.astype(BF), R2_ref[...])
    Bx = _dot(bm.astype(BF), R2_ref[...])
    Xw = xs * cw_x
    M8 = M8_ref[...]
    nh = SSD_H

    def rows_by_head(a):
        return jnp.concatenate([a] * nh, axis=0) * M8

    for s in range(ns):
        lo = s * c
        S0 = s0_ref[s].reshape(SSD_H * SSD_N, SSD_P)
        oi = _dot(rows_by_head(Cx[lo:lo + c, :]).astype(BF), S0.astype(BF))
        oix = _dot_x2(oi, T8_ref[...]) * M8
        acc = oix[0:c, :]
        for h in range(1, nh):
            acc = acc + oix[h * c:(h + 1) * c, :]
        oi_ref[lo:lo + c, :] = acc
        Xst = _dot(rows_by_head(Xw[lo:lo + c, :]).astype(BF), T8T_ref[...])
        u = _dot_tn(rows_by_head(Bx[lo:lo + c, :]).astype(BF), Xst.astype(BF))
        acol = _col_bcast(egl_x[lo:lo + c, :], ones_ref)
        sn_ref[s] = (acol * S0 + u).reshape(SSD_H, SSD_N, SSD_P)

    o = o + eg_x * oi_ref[...]
    y = (o + dexp_ref[...] * xs) * _silu(z)
    ms = jnp.mean(y * y, axis=-1, keepdims=True)
    o_ref[...] = (y * lax.rsqrt(ms + EPS) * nw_ref[...]).astype(o_ref.dtype)


def _ssd_sample_call(sin_, c0, s0, B, T, conv_w, conv_b, dt_bias, a_log, d, norm_w):
    TT = SEQ_TILE * T
    L, Mk, Eexp, _ = _ssd_tables(TT, T)
    i = np.arange(TT)
    Bl = jnp.asarray((i[None, :] == (i[:, None] // T) * T + T - 1).astype(np.float32), BF)
    hpg = SSD_H // SSD_G
    R2 = np.zeros((LANE, SSD_H * SSD_N), np.float32)
    for h in range(SSD_H):
        R2[(h // hpg) * SSD_N:(h // hpg + 1) * SSD_N, h * SSD_N:(h + 1) * SSD_N] = np.eye(SSD_N)
    T8 = _tile_lanes(SSD_H, SSD_P)
    M8 = _head_block_mask(T, SSD_P, SSD_H)
    tabs = (L, Mk, Eexp, Bl, jnp.asarray(R2, BF), jnp.asarray(T8, BF), jnp.asarray(T8.T, BF), jnp.asarray(M8, F32),
            jnp.ones((8, SSD_P), BF))
    prm = _ssd_params(conv_w, conv_b, dt_bias, a_log, d, norm_w)
    sspec = pl.BlockSpec((SEQ_TILE, SSD_H, SSD_N, SSD_P), lambda i: (i, 0, 0, 0))
    cspec = pl.BlockSpec((SEQ_TILE, SSD_CONV_W - 1, SSD_CONV_DIM), lambda i: (i, 0, 0))
    return pl.pallas_call(
        functools.partial(_ssd_sample_kernel, c=T),
        grid=(B // SEQ_TILE,),
        in_specs=[pl.BlockSpec((TT, SSD_IN_W), lambda i: (i, 0)), cspec, sspec]
                 + [_const(p.shape) for p in prm] + [_const(a.shape) for a in tabs],
        out_specs=[pl.BlockSpec((TT, SSD_WIDTH), lambda i: (i, 0)), sspec, cspec],
        out_shape=[jax.ShapeDtypeStruct((B * T, SSD_WIDTH), BF),
                   jax.ShapeDtypeStruct((B, SSD_H, SSD_N, SSD_P), F32),
                   jax.ShapeDtypeStruct((B, SSD_CONV_W - 1, SSD_CONV_DIM), F32)],
        scratch_shapes=[pltpu.VMEM((8 + SEQ_TILE * 2 * T, SSD_CONV_DIM), F32),
                        pltpu.VMEM((TT, SSD_WIDTH), F32)],
        compiler_params=_cp(1),
        name="ssd_sample",
    )(sin_, c0, s0, *prm, *tabs)


def _outproj_kernel(x_ref, g_ref, og_ref, or_ref, os_ref, w_ref, lg_ref, lb_ref, o_ref):
    bB, bT, D = x_ref.shape
    mix = (_dot(og_ref[...], w_ref[0:GLA_WIDTH, :])
           + _dot(or_ref[...], w_ref[GLA_WIDTH:GLA_WIDTH + RET_WIDTH, :])
           + _dot(os_ref[...], w_ref[GLA_WIDTH + RET_WIDTH:D, :]))
    y = ALPHA * x_ref[...] + g_ref[...] * mix.reshape(bB, bT, D)
    o_ref[...] = _layer_norm(y, lg_ref[...], lb_ref[...])


def _outproj_call(x3, g1, og, orr, os_, w_out, ln_g, ln_b):
    B, T, D = x3.shape
    bB, bT = _tok_tiles(B, T)
    nT = T // bT
    R = bB * bT
    xmap = lambda i, j: (i, j, 0)
    mmap = lambda i, j: (i, 0, 0)
    rmap = lambda i, j: (i * nT + j, 0)
    return pl.pallas_call(
        _outproj_kernel,
        grid=(B // bB, nT),
        in_specs=[pl.BlockSpec((bB, bT, D), xmap),
                  pl.BlockSpec((bB, 1, D), mmap),
                  pl.BlockSpec((R, GLA_WIDTH), rmap),
                  pl.BlockSpec((R, RET_WIDTH), rmap),
                  pl.BlockSpec((R, SSD_WIDTH), rmap),
                  _const((D, D)), _const((1, 1, D)), _const((1, 1, D))],
        out_specs=pl.BlockSpec((bB, bT, D), xmap),
        out_shape=jax.ShapeDtypeStruct((B, T, D), F32),
        compiler_params=_cp(2),
        name="out_proj_ln",
    )(x3, g1, og, orr, os_, w_out, ln_g.reshape(1, 1, D), ln_b.reshape(1, 1, D))


ROUTE_OFF = 8
MOE_CHUNK = 64


def _moe_route_t(lt):
    R = lt.shape[1]
    neg = jnp.float32(-jnp.inf)
    row8 = lax.broadcasted_iota(jnp.int32, (8, 1), 0)
    lg = jnp.where(row8 < MOE_GROUPS, lt[0:8, :], neg)
    mg = jnp.max(lg, axis=0, keepdims=True)
    gsel = jnp.min(jnp.where(lg == mg, row8, 8), axis=0, keepdims=True)
    g_gate = 1.0 / jnp.sum(jnp.exp(lg - mg), axis=0, keepdims=True)
    rowe = lax.broadcasted_iota(jnp.int32, (MOE_EXPERTS, 1), 0)
    le = jnp.where((rowe // MOE_PER_GROUP) == gsel, lt[ROUTE_OFF:ROUTE_OFF + MOE_EXPERTS, :], neg)
    m1 = jnp.max(le, axis=0, keepdims=True)
    i1 = jnp.min(jnp.where(le == m1, rowe, MOE_EXPERTS), axis=0, keepdims=True)
    le2 = jnp.where(rowe == i1, neg, le)
    m2 = jnp.max(le2, axis=0, keepdims=True)
    i2 = jnp.min(jnp.where(le2 == m2, rowe, MOE_EXPERTS), axis=0, keepdims=True)
    e2 = jnp.exp(m2 - m1)
    w1 = g_gate / (1.0 + e2)
    w2 = g_gate * e2 / (1.0 + e2)
    comb = jnp.where(rowe == i1, w1, jnp.where(rowe == i2, w2, 0.0))
    cg = comb[0:4, :]
    for g in range(1, MOE_GROUPS):
        cg = cg + comb[g * MOE_PER_GROUP:(g + 1) * MOE_PER_GROUP, :]
    return gsel, cg, comb


def _moe_kernel(x_ref, sc_ref, sh_ref, g_ref, wr_ref, br_ref, w1_ref, w3_ref, w2_ref, lg_ref, lb_ref, o_ref):
    bB, bT, D = x_ref.shape
    R = bB * bT
    x = x_ref[...]
    hb = (x * (1.0 + sc_ref[...]) + sh_ref[...]).reshape(R, D).astype(BF)
    _, _, comb = _moe_route_t(_dot_nt(wr_ref[...], hb) + br_ref[...])
    combT = jnp.concatenate([comb, jnp.zeros((LANE - MOE_EXPERTS, R), F32)], axis=0).T
    y = jnp.zeros((R, D), F32)
    for e in range(MOE_EXPERTS):
        hid = _silu(_dot(hb, w1_ref[e])) * _dot(hb, w3_ref[e])
        hid = hid * combT[:, e:e + 1]
        y = y + _dot(hid.astype(BF), w2_ref[e])
    z = ALPHA * x + g_ref[...] * y.reshape(bB, bT, D)
    o_ref[...] = _layer_norm(z, lg_ref[...], lb_ref[...])


def _resident(shape):
    return pl.BlockSpec(shape, lambda *_: (0,) * len(shape), pipeline_mode=pl.Buffered(1))


def _moe_call(x3, sc, sh, g2, wr, br, w1, w3, w2, ln_g, ln_b):
    B, T, D = x3.shape
    bB, bT = _tok_tiles(B, T)
    xmap = lambda i, j: (i, j, 0)
    mmap = lambda i, j: (i, 0, 0)
    return pl.pallas_call(
        _moe_kernel,
        grid=(B // bB, T // bT),
        in_specs=[pl.BlockSpec((bB, bT, D), xmap),
                  pl.BlockSpec((bB, 1, D), mmap), pl.BlockSpec((bB, 1, D), mmap), pl.BlockSpec((bB, 1, D), mmap),
                  _const(wr.shape), _const(br.shape),
                  _resident(w1.shape), _resident(w3.shape), _resident(w2.shape),
                  _const((1, 1, D)), _const((1, 1, D))],
        out_specs=pl.BlockSpec((bB, bT, D), xmap),
        out_shape=jax.ShapeDtypeStruct((B, T, D), F32),
        compiler_params=_cp(2),
        name="moe_ln",
    )(x3, sc, sh, g2, wr, br, w1, w3, w2, ln_g.reshape(1, 1, D), ln_b.reshape(1, 1, D))


def _router_params(w_group, b_group, w_expert, b_expert):
    wr = jnp.zeros((LANE, D_MODEL), F32).at[:MOE_GROUPS].set(w_group.T)
    wr = wr.at[ROUTE_OFF:ROUTE_OFF + MOE_EXPERTS].set(w_expert.T)
    br = jnp.zeros((LANE, 1), F32).at[:MOE_GROUPS, 0].set(b_group).at[ROUTE_OFF:ROUTE_OFF + MOE_EXPERTS, 0].set(b_expert)
    return wr.astype(BF), br


def kernel(x_prompt, x_sample, c_prompt, c_sample, state_gla, state_ret, state_ssd, state_conv, w_ada, b_ada, w_in, gla_w_gate, gla_b_gate, gla_norm, ret_norm, ssd_conv_w, ssd_conv_b, ssd_dt_bias, ssd_a_log, ssd_d, ssd_norm, w_out, ln1_g, ln1_b, moe_w_group, moe_b_group, moe_w_expert, moe_b_expert, moe_w1, moe_w3, moe_w2, ln2_g, ln2_b):
    Bp, Tp, D = x_prompt.shape
    Bs, Ts, _ = x_sample.shape
    n_ga = 128 + 128 + 256 + GLA_GATE_RANK
    w_in_p = jnp.concatenate(
        [w_in[:, :, :n_ga], jnp.zeros((DEPTH, D, LANE - GLA_GATE_RANK), F32), w_in[:, :, n_ga:],
         jnp.zeros((DEPTH, D, LANE - SSD_H), F32)], axis=-1).astype(BF)
    w_out_b = w_out.astype(BF)
    w1_b, w3_b, w2_b = moe_w1.astype(BF), moe_w3.astype(BF), moe_w2.astype(BF)

    mod = _mod_call(jnp.concatenate([c_prompt, c_sample], axis=0), w_ada, b_ada)

    def trunk(x, mod_rows, B, T, states):
        new = [[], [], [], []]
        for l in range(DEPTH):
            sh1, sc1, g1, sh2, sc2, g2 = (mod_rows[l][:, None, i * D:(i + 1) * D] for i in range(6))
            gin, rin, sin_ = _inproj_call(x, sc1, sh1, w_in_p[l])
            if states is None:
                og, s_gla = _gla_prompt_call(gin, B, T, gla_w_gate[l], gla_b_gate[l], gla_norm[l])
                orr, s_ret = _ret_prompt_call(rin, B, T, ret_norm[l])
                os_, s_ssd, s_conv = _ssd_prompt_call(sin_, B, T, ssd_conv_w[l], ssd_conv_b[l], ssd_dt_bias[l],
                                                      ssd_a_log[l], ssd_d[l], ssd_norm[l])
            else:
                og, s_gla = _gla_sample_call(gin, states[0][l], B, T, gla_w_gate[l], gla_b_gate[l], gla_norm[l])
                orr, s_ret = _ret_sample_call(rin, states[1][l], B, T, ret_norm[l])
                os_, s_ssd, s_conv = _ssd_sample_call(sin_, states[3][l], states[2][l], B, T, ssd_conv_w[l],
                                                      ssd_conv_b[l], ssd_dt_bias[l], ssd_a_log[l], ssd_d[l],
                                                      ssd_norm[l])
            x = _outproj_call(x, g1, og, orr, os_, w_out_b[l], ln1_g[l], ln1_b[l])
            wr, br = _router_params(moe_w_group[l], moe_b_group[l], moe_w_expert[l], moe_b_expert[l])
            x = _moe_call(x, sc2, sh2, g2, wr, br, w1_b[l], w3_b[l], w2_b[l], ln2_g[l], ln2_b[l])
            for acc, s in zip(new, (s_gla, s_ret, s_ssd, s_conv)):
                acc.append(s)
        return (x,) + tuple(jnp.stack(a) for a in new)

    y_p, gla_p, ret_p, ssd_p, conv_p = trunk(x_prompt, mod[:, :Bp], Bp, Tp, None)
    y_s, gla_s, ret_s, ssd_s, conv_s = trunk(x_sample, mod[:, Bp:], Bs, Ts,
                                             (state_gla, state_ret, state_ssd, state_conv))
    return (y_p, y_s, gla_p, ret_p, ssd_p, conv_p, gla_s, ret_s, ssd_s, conv_s)
```

```python
import functools
import math

import numpy as np
import jax
import jax.numpy as jnp
from jax import lax
from jax.experimental import pallas as pl
from jax.experimental.pallas import tpu as pltpu

F32 = jnp.float32
BF = jnp.bfloat16

D_MODEL = 1024
DEPTH = 2
PAST_LEN = 16384
GLA_H, GLA_DK, GLA_DV = 4, 32, 64
GLA_WIDTH = GLA_H * GLA_DV
GLA_GATE_RANK = 16
GLA_GATE_TEMP = 16.0
GLA_CHUNK = 16
RET_H, RET_DK, RET_DV = 4, 64, 64
RET_WIDTH = RET_H * RET_DV
ROPE_BASE = 10000.0
SSD_H, SSD_P, SSD_G, SSD_N = 8, 64, 2, 64
SSD_WIDTH = SSD_H * SSD_P
SSD_CONV_W = 4
SSD_CONV_DIM = SSD_WIDTH + 2 * SSD_G * SSD_N
MOE_GROUPS, MOE_PER_GROUP = 4, 4
MOE_EXPERTS = MOE_GROUPS * MOE_PER_GROUP
MOE_FF = 256
ALPHA = (2 * DEPTH) ** 0.25
EPS = 1e-5

LANE = 128
GLA_IN_W = 128 + 128 + 256 + LANE + 256
RET_IN_W = 4 * 256
SSD_IN_W = 512 + SSD_CONV_DIM + LANE
IN_W = GLA_IN_W + RET_IN_W + SSD_IN_W
VMEM_LIMIT = 56 * 1024 * 1024


def _cp(n_axes, vmem=VMEM_LIMIT):
    return pltpu.CompilerParams(dimension_semantics=("arbitrary",) * n_axes, vmem_limit_bytes=vmem)


def _dot(a, b):
    return jnp.dot(a, b, preferred_element_type=F32)


def _dot_nt(a, b):
    return lax.dot_general(a, b, (((1,), (1,)), ((), ())), preferred_element_type=F32)


def _dot_tn(a, b):
    return lax.dot_general(a, b, (((0,), (0,)), ((), ())), preferred_element_type=F32)


def _split3(x):
    hi = x.astype(BF)
    r = x - hi.astype(F32)
    mid = r.astype(BF)
    lo = (r - mid.astype(F32)).astype(BF)
    return hi, mid, lo


def _dot_x3(x, e):
    hi, mid, lo = _split3(x)
    return _dot(hi, e) + (_dot(mid, e) + _dot(lo, e))


def _dot_x2(x, e):
    hi = x.astype(BF)
    lo = (x - hi.astype(F32)).astype(BF)
    return _dot(hi, e) + _dot(lo, e)


def _dot_3x(e, x):
    hi, mid, lo = _split3(x)
    return _dot(e, hi) + (_dot(e, mid) + _dot(e, lo))


def _sigmoid(x):
    return 1.0 / (1.0 + jnp.exp(-x))


def _silu(x):
    return x * _sigmoid(x)


def _log_sigmoid(x):
    return jnp.minimum(x, 0.0) - jnp.log(1.0 + jnp.exp(-jnp.abs(x)))


def _softplus(x):
    return jnp.maximum(x, 0.0) + jnp.log(1.0 + jnp.exp(-jnp.abs(x)))


def _layer_norm(x, g, b):
    mu = jnp.mean(x, axis=-1, keepdims=True)
    d = x - mu
    var = jnp.mean(d * d, axis=-1, keepdims=True)
    return d * lax.rsqrt(var + EPS) * g + b


def _mod_kernel(c_ref, w_ref, b_ref, o_ref):
    s = _silu(c_ref[...]).astype(BF)
    o_ref[0] = _dot(s, w_ref[0].astype(BF)) + b_ref[0]


def _mod_call(c_all, w_ada, b_ada):
    R = c_all.shape[0]
    tn = 1536
    return pl.pallas_call(
        _mod_kernel,
        grid=(DEPTH, 6 * D_MODEL // tn),
        in_specs=[pl.BlockSpec((R, D_MODEL), lambda l, j: (0, 0)),
                  pl.BlockSpec((1, D_MODEL, tn), lambda l, j: (l, 0, j)),
                  pl.BlockSpec((1, 1, tn), lambda l, j: (l, 0, j))],
        out_specs=pl.BlockSpec((1, R, tn), lambda l, j: (l, 0, j)),
        out_shape=jax.ShapeDtypeStruct((DEPTH, R, 6 * D_MODEL), F32),
        compiler_params=_cp(2),
        name="ada_mod",
    )(c_all, w_ada, b_ada.reshape(DEPTH, 1, 6 * D_MODEL))


def _inproj_kernel(x_ref, sc_ref, sh_ref, w_ref, og_ref, or_ref, os_ref):
    bB, bT, D = x_ref.shape
    h = x_ref[...] * (1.0 + sc_ref[...]) + sh_ref[...]
    hb = h.reshape(bB * bT, D).astype(BF)
    og_ref[...] = _dot(hb, w_ref[:, 0:GLA_IN_W])
    or_ref[...] = _dot(hb, w_ref[:, GLA_IN_W:GLA_IN_W + RET_IN_W])
    os_ref[...] = _dot(hb, w_ref[:, GLA_IN_W + RET_IN_W:IN_W])


def _tok_tiles(B, T):
    if T >= 512:
        return 1, 512
    return 512 // T, T


def _inproj_call(x3, sc, sh, w_p):
    B, T, D = x3.shape
    bB, bT = _tok_tiles(B, T)
    nT = T // bT
    R = bB * bT
    N = B * T
    xmap = lambda i, j: (i, j, 0)
    mmap = lambda i, j: (i, 0, 0)
    omap = lambda i, j: (i * nT + j, 0)
    return pl.pallas_call(
        _inproj_kernel,
        grid=(B // bB, nT),
        in_specs=[pl.BlockSpec((bB, bT, D), xmap),
                  pl.BlockSpec((bB, 1, D), mmap),
                  pl.BlockSpec((bB, 1, D), mmap),
                  pl.BlockSpec((D, IN_W), lambda i, j: (0, 0))],
        out_specs=[pl.BlockSpec((R, GLA_IN_W), omap),
                   pl.BlockSpec((R, RET_IN_W), omap),
                   pl.BlockSpec((R, SSD_IN_W), omap)],
        out_shape=[jax.ShapeDtypeStruct((N, GLA_IN_W), F32),
                   jax.ShapeDtypeStruct((N, RET_IN_W), F32),
                   jax.ShapeDtypeStruct((N, SSD_IN_W), F32)],
        compiler_params=_cp(2),
        name="in_proj",
    )(x3, sc, sh, w_p)


def _head_block_mask(rows_per, cols_per, n):
    r = np.arange(rows_per * n)[:, None] // rows_per
    c = np.arange(cols_per * n)[None, :] // cols_per
    return (r == c).astype(np.float32)


def _block_tril(n, c):
    i = np.arange(n)[:, None]
    j = np.arange(n)[None, :]
    return ((i // c == j // c) & (j <= i)).astype(np.float32)


def _gla_front(x_ref, wg_ref, bg_ref, L_ref):
    q = x_ref[:, 0:128] * (GLA_DK ** -0.5)
    k = x_ref[:, 128:256]
    v = x_ref[:, 256:512]
    ga = x_ref[:, 512:640]
    r = x_ref[:, 640:896]
    gate = _dot(ga.astype(BF), wg_ref[...]) + bg_ref[...]
    la = _log_sigmoid(gate) * (1.0 / GLA_GATE_TEMP)
    g = _dot_3x(L_ref[...], la)
    return q, k, v, r, g


def _gla_intra(q, g, kp_ref, gp_ref, vp_ref, E_ref, c):
    TT = q.shape[0]
    PAD = kp_ref.shape[0] - TT
    pos = lax.broadcasted_iota(jnp.int32, (TT, 1), 0) & (c - 1)
    o = jnp.zeros((TT, 2 * LANE), F32)
    for s in range(c):
        ks = kp_ref[pl.ds(PAD - s, TT), :]
        gs = gp_ref[pl.ds(PAD - s, TT), :]
        vs = vp_ref[pl.ds(PAD - s, TT), :]
        w = jnp.where(pos >= s, q * ks * jnp.exp(jnp.minimum(g - gs, 0.0)), 0.0)
        o = o + _dot(w.astype(BF), E_ref[...]) * vs
    return o


def _gla_norm_gate(o, r, nw_ref, EA_ref):
    ms = _dot_x3(o * o, EA_ref[...])
    return o * lax.rsqrt(ms + EPS) * nw_ref[...] * _silu(r)


def _gla_prompt_kernel(x_ref, wg_ref, bg_ref, nw_ref, L_ref, E_ref, EA_ref, M_ref,
                       o_ref, sfin_ref, st_ref, kp_ref, gp_ref, vp_ref, oi_ref, u_ref, sb_ref, *, c):
    TT = x_ref.shape[0]
    nc = TT // c
    PAD = kp_ref.shape[0] - TT
    t = pl.program_id(1)

    @pl.when(t == 0)
    def _():
        st_ref[...] = jnp.zeros_like(st_ref)

    q, k, v, r, g = _gla_front(x_ref, wg_ref, bg_ref, L_ref)
    kp_ref[0:PAD, :] = jnp.zeros((PAD, LANE), F32)
    gp_ref[0:PAD, :] = jnp.zeros((PAD, LANE), F32)
    vp_ref[0:PAD, :] = jnp.zeros((PAD, 2 * LANE), F32)
    kp_ref[PAD:PAD + TT, :] = k
    gp_ref[PAD:PAD + TT, :] = g
    vp_ref[PAD:PAD + TT, :] = v
    o = _gla_intra(q, g, kp_ref, gp_ref, vp_ref, E_ref, c)

    M = M_ref[...]
    gl_all = gp_ref[pl.ds(PAD + c - 1, nc, stride=c), :]
    for n in range(nc):
        lo = n * c
        ke = (k[lo:lo + c, :] * jnp.exp(gl_all[n:n + 1, :] - g[lo:lo + c, :])).astype(BF)
        u_ref[n] = _dot_tn(ke, v[lo:lo + c, :].astype(BF)) * M
    a_cols = jnp.concatenate([jnp.exp(gl_all), jnp.zeros((LANE - nc, LANE), F32)], axis=0).T
    S = st_ref[...]
    for n in range(nc):
        sb_ref[n] = S.astype(BF)
        S = a_cols[:, n:n + 1] * S + u_ref[n]
    st_ref[...] = S
    qe = (q * jnp.exp(g)).astype(BF)
    for n in range(nc):
        lo = n * c
        oi_ref[lo:lo + c, :] = _dot(qe[lo:lo + c, :], sb_ref[n])
    o = o + oi_ref[...]
    o_ref[...] = _gla_norm_gate(o, r, nw_ref, EA_ref).astype(o_ref.dtype)

    @pl.when(t == pl.num_programs(1) - 1)
    def _():
        for h in range(GLA_H):
            sfin_ref[0, h] = S[h * GLA_DK:(h + 1) * GLA_DK, h * GLA_DV:(h + 1) * GLA_DV]


def _gla_tables(TT, c):
    L = jnp.asarray(_block_tril(TT, c), BF)
    E = jnp.asarray(_head_block_mask(GLA_DK, GLA_DV, GLA_H), BF)
    EA = jnp.asarray(_head_block_mask(GLA_DV, GLA_DV, GLA_H) / GLA_DV, BF)
    M = jnp.asarray(_head_block_mask(GLA_DK, GLA_DV, GLA_H), F32)
    return L, E, EA, M


def _gla_params(w_gate, b_gate, norm_w):
    wg = jnp.zeros((LANE, GLA_H * GLA_DK), F32).at[:GLA_GATE_RANK].set(w_gate).astype(BF)
    return wg, b_gate.reshape(1, -1), norm_w.reshape(1, -1)


def _const(shape):
    return pl.BlockSpec(shape, lambda *_: (0,) * len(shape))


def _gla_prompt_call(gin, B, T, w_gate, b_gate, norm_w):
    TT, c = 256, GLA_CHUNK
    nT = T // TT
    L, E, EA, M = _gla_tables(TT, c)
    wg, bg, nw = _gla_params(w_gate, b_gate, norm_w)
    PAD = 16
    return pl.pallas_call(
        functools.partial(_gla_prompt_kernel, c=c),
        grid=(B, nT),
        in_specs=[pl.BlockSpec((TT, GLA_IN_W), lambda b, t: (b * nT + t, 0)),
                  _const(wg.shape), _const(bg.shape), _const(nw.shape),
                  _const(L.shape), _const(E.shape), _const(EA.shape), _const(M.shape)],
        out_specs=[pl.BlockSpec((TT, GLA_WIDTH), lambda b, t: (b * nT + t, 0)),
                   pl.BlockSpec((1, GLA_H, GLA_DK, GLA_DV), lambda b, t: (b, 0, 0, 0))],
        out_shape=[jax.ShapeDtypeStruct((B * T, GLA_WIDTH), BF),
                   jax.ShapeDtypeStruct((B, GLA_H, GLA_DK, GLA_DV), F32)],
        scratch_shapes=[pltpu.VMEM((GLA_H * GLA_DK, GLA_H * GLA_DV), F32),
                        pltpu.VMEM((TT + PAD, LANE), F32),
                        pltpu.VMEM((TT + PAD, LANE), F32),
                        pltpu.VMEM((TT + PAD, 2 * LANE), F32),
                        pltpu.VMEM((TT, 2 * LANE), F32),
                        pltpu.VMEM((TT // c, GLA_H * GLA_DK, GLA_H * GLA_DV), F32),
                        pltpu.VMEM((TT // c, GLA_H * GLA_DK, GLA_H * GLA_DV), BF)],
        compiler_params=_cp(2),
        name="gla_prompt",
    )(gin, wg, bg, nw, L, E, EA, M)


def _rope(x, cos, sin_signed):
    lane = lax.broadcasted_iota(jnp.int32, (1, LANE), 1)
    first_half = (lane & (RET_DK - 1)) < RET_DK // 2
    out = []
    for p in range(2):
        xs = x[:, p * LANE:(p + 1) * LANE]
        up = pltpu.roll(xs, LANE - RET_DK // 2, 1)
        dn = pltpu.roll(xs, RET_DK // 2, 1)
        out.append(xs * cos + jnp.where(first_half, up, dn) * sin_signed)
    return jnp.concatenate(out, axis=1)


def _ret_front(x_ref, cos_ref, sin_ref):
    q = _rope(x_ref[:, 0:256], cos_ref[...], sin_ref[...])
    k = _rope(x_ref[:, 256:512], cos_ref[...], sin_ref[...]) * (RET_DK ** -0.5)
    v = x_ref[:, 512:768]
    rg = x_ref[:, 768:1024]
    return q, k, v, rg


def _ret_intra(q, k, v, D_ref):
    lane = lax.broadcasted_iota(jnp.int32, (1, RET_WIDTH), 1)
    kb = k.astype(BF)
    o = jnp.zeros(q.shape, F32)
    for h in range(RET_H):
        hm = (lane // RET_DK) == h
        s = _dot_nt(jnp.where(hm, q, 0.0).astype(BF), kb)
        p = (s * D_ref[h]).astype(BF)
        o = o + _dot(p, jnp.where(hm, v, 0.0).astype(BF))
    return o


def _ret_norm_gate(o, rg, nw_ref, EA_ref):
    mu = _dot_x3(o, EA_ref[...])
    d = o - mu
    var = _dot_x3(d * d, EA_ref[...])
    return d * lax.rsqrt(var + EPS) * nw_ref[...] * _silu(rg)


def _ret_prompt_kernel(x_ref, cos_ref, sin_ref, D_ref, rd_ref, kd_ref, G_ref, M_ref, EA_ref, nw_ref,
                       o_ref, sfin_ref, st_ref):
    t = pl.program_id(1)

    @pl.when(t == 0)
    def _():
        st_ref[...] = jnp.zeros_like(st_ref)

    q, k, v, rg = _ret_front(x_ref, cos_ref, sin_ref)
    o = _ret_intra(q, k, v, D_ref)
    S = st_ref[...]
    o = o + _dot((q * rd_ref[...]).astype(BF), S.astype(BF))
    u = _dot_tn((k * kd_ref[...]).astype(BF), v.astype(BF))
    S = S * G_ref[...] + u * M_ref[...]
    st_ref[...] = S
    o_ref[...] = _ret_norm_gate(o, rg, nw_ref, EA_ref).astype(o_ref.dtype)

    @pl.when(t == pl.num_programs(1) - 1)
    def _():
        for h in range(RET_H):
            sfin_ref[0, h] = S[h * RET_DK:(h + 1) * RET_DK, h * RET_DV:(h + 1) * RET_DV]


def _rope_tables(pos):
    half = RET_DK // 2
    inv = ROPE_BASE ** (-jnp.arange(half, dtype=F32) / half)
    ang = pos.astype(F32)[:, None] * inv[None, :]
    cos, sin = jnp.cos(ang), jnp.sin(ang)
    return jnp.tile(jnp.concatenate([cos, cos], 1), (1, 2)), jnp.tile(jnp.concatenate([-sin, sin], 1), (1, 2))


def _ret_log_gamma():
    return np.log(1.0 - 2.0 ** (-5.0 - np.arange(RET_H, dtype=np.float64)))


def _ret_prompt_call(rin, B, T, norm_w):
    TT = 256
    nT = T // TT
    cos, sin = _rope_tables(jnp.arange(T, dtype=jnp.int32))
    lg = _ret_log_gamma()
    i = np.arange(TT)
    dec = np.exp(lg[:, None, None] * (i[:, None] - i[None, :])[None]) * (i[:, None] >= i[None, :])[None]
    Dm = jnp.asarray(dec, F32)
    rd = jnp.asarray(np.repeat(np.exp(lg[None, :] * (i[:, None] + 1)), RET_DK, 1), F32)
    kd = jnp.asarray(np.repeat(np.exp(lg[None, :] * (TT - 1 - i[:, None])), RET_DK, 1), F32)
    M = _head_block_mask(RET_DK, RET_DV, RET_H)
    G = jnp.asarray(M * np.repeat(np.exp(lg * TT), RET_DK)[:, None], F32)
    M = jnp.asarray(M, F32)
    EA = jnp.asarray(_head_block_mask(RET_DV, RET_DV, RET_H) / RET_DV, BF)
    nw = norm_w.reshape(1, -1)
    return pl.pallas_call(
        _ret_prompt_kernel,
        grid=(B, nT),
        in_specs=[pl.BlockSpec((TT, RET_IN_W), lambda b, t: (b * nT + t, 0)),
                  pl.BlockSpec((TT, LANE), lambda b, t: (t, 0)),
                  pl.BlockSpec((TT, LANE), lambda b, t: (t, 0)),
                  _const(Dm.shape), _const(rd.shape), _const(kd.shape), _const(G.shape), _const(M.shape),
                  _const(EA.shape), _const(nw.shape)],
        out_specs=[pl.BlockSpec((TT, RET_WIDTH), lambda b, t: (b * nT + t, 0)),
                   pl.BlockSpec((1, RET_H, RET_DK, RET_DV), lambda b, t: (b, 0, 0, 0))],
        out_shape=[jax.ShapeDtypeStruct((B * T, RET_WIDTH), BF),
                   jax.ShapeDtypeStruct((B, RET_H, RET_DK, RET_DV), F32)],
        scratch_shapes=[pltpu.VMEM((RET_H * RET_DK, RET_H * RET_DV), F32)],
        compiler_params=_cp(2),
        name="ret_prompt",
    )(rin, cos, sin, Dm, rd, kd, G, M, EA, nw)


def _ssd_conv(xp_ref, cw_ref, cb_ref, TT):
    acc = cb_ref[...] + cw_ref[SSD_CONV_W - 1:SSD_CONV_W, :] * xp_ref[pl.ds(8, TT), :]
    for i in range(SSD_CONV_W - 1):
        acc = acc + cw_ref[i:i + 1, :] * xp_ref[pl.ds(8 - (SSD_CONV_W - 1) + i, TT), :]
    return acc


def _ssd_intra(xs, bm, cm, g, dt, Mk_ref):
    TT = xs.shape[0]
    gT = g.T
    dtT = dt.T
    lane = lax.broadcasted_iota(jnp.int32, (1, LANE), 1)
    lane2 = lax.broadcasted_iota(jnp.int32, (1, 2 * LANE), 1)
    causal = Mk_ref[...] > 0.0
    bmb = bm.astype(BF)
    o_parts = []
    for grp in range(SSD_G):
        cb = _dot_nt(jnp.where((lane // SSD_N) == grp, cm, 0.0).astype(BF), bmb)
        xg = xs[:, grp * 2 * LANE:(grp + 1) * 2 * LANE]
        og = jnp.zeros((TT, 2 * LANE), F32)
        for h4 in range(SSD_H // SSD_G):
            h = grp * (SSD_H // SSD_G) + h4
            dec = jnp.where(causal, jnp.exp(jnp.minimum(g[:, h:h + 1] - gT[h:h + 1, :], 0.0)), 0.0)
            p = (cb * dec * dtT[h:h + 1, :]).astype(BF)
            og = og + _dot(p, jnp.where((lane2 // SSD_P) == h4, xg, 0.0).astype(BF))
        o_parts.append(og)
    return jnp.concatenate(o_parts, axis=1)


def _ssd_prompt_kernel(x_ref, cw_ref, cb_ref, dtb_ref, alog_ref, dexp_ref, nw_ref, L_ref, Mk_ref, Eexp_ref, M2_ref,
                       o_ref, sfin_ref, cfin_ref, st_ref, xp_ref):
    TT = x_ref.shape[0]
    t = pl.program_id(1)

    @pl.when(t == 0)
    def _():
        st_ref[...] = jnp.zeros_like(st_ref)
        xp_ref[0:8, :] = jnp.zeros((8, SSD_CONV_DIM), F32)

    z = x_ref[:, 0:SSD_WIDTH]
    xp_ref[8:8 + TT, :] = x_ref[:, SSD_WIDTH:SSD_WIDTH + SSD_CONV_DIM]
    sdt = x_ref[:, SSD_WIDTH + SSD_CONV_DIM:SSD_IN_W]
    xbc = _silu(_ssd_conv(xp_ref, cw_ref, cb_ref, TT))
    tail = xp_ref[TT:TT + 8, :]
    xp_ref[0:8, :] = tail
    xs = xbc[:, 0:SSD_WIDTH]
    bm = xbc[:, SSD_WIDTH:SSD_WIDTH + LANE]
    cm = xbc[:, SSD_WIDTH + LANE:SSD_CONV_DIM]

    dt = _softplus(sdt + dtb_ref[...])
    la = dt * (-jnp.exp(alog_ref[...]))
    g = _dot_3x(L_ref[...], la)
    gl = g[TT - 1:TT, :]
    Eexp = Eexp_ref[...]
    eg_x = _dot_x2(jnp.exp(g), Eexp)
    cw_x = _dot_x2(dt * jnp.exp(gl - g), Eexp)
    egl_x = _dot_x2(jnp.exp(gl), Eexp)

    o = _ssd_intra(xs, bm, cm, g, dt, Mk_ref)

    S = st_ref[...]
    o = o + eg_x * _dot(cm.astype(BF), S.astype(BF))
    u = _dot_tn(bm.astype(BF), (xs * cw_x).astype(BF))
    S = S * egl_x + u * M2_ref[...]
    st_ref[...] = S

    y = (o + dexp_ref[...] * xs) * _silu(z)
    ms = jnp.mean(y * y, axis=-1, keepdims=True)
    o_ref[...] = (y * lax.rsqrt(ms + EPS) * nw_ref[...]).astype(o_ref.dtype)

    @pl.when(t == pl.num_programs(1) - 1)
    def _():
        for h in range(SSD_H):
            gi = h // (SSD_H // SSD_G)
            sfin_ref[0, h] = S[gi * SSD_N:(gi + 1) * SSD_N, h * SSD_P:(h + 1) * SSD_P]
        cfin_ref[0] = tail[8 - (SSD_CONV_W - 1):8, :]


def _pad_lanes(v, n=LANE):
    v = v.reshape(1, -1)
    return jnp.zeros((1, n), F32).at[:, :v.shape[1]].set(v)


def _ssd_tables(TT, c):
    L = jnp.asarray(_block_tril(TT, c), BF)
    Mk = jnp.asarray(_block_tril(TT, c), F32)
    e = np.zeros((LANE, SSD_WIDTH), np.float32)
    for h in range(SSD_H):
        e[h, h * SSD_P:(h + 1) * SSD_P] = 1.0
    M2 = np.zeros((SSD_G * SSD_N, SSD_WIDTH), np.float32)
    for h in range(SSD_H):
        gi = h // (SSD_H // SSD_G)
        M2[gi * SSD_N:(gi + 1) * SSD_N, h * SSD_P:(h + 1) * SSD_P] = 1.0
    return L, Mk, jnp.asarray(e, BF), jnp.asarray(M2, F32)


def _ssd_params(conv_w, conv_b, dt_bias, a_log, d, norm_w):
    return (conv_w, conv_b.reshape(1, -1), _pad_lanes(dt_bias), _pad_lanes(a_log),
            jnp.repeat(d, SSD_P).reshape(1, -1), norm_w.reshape(1, -1))


def _ssd_prompt_call(sin_, B, T, conv_w, conv_b, dt_bias, a_log, d, norm_w):
    TT = 256
    nT = T // TT
    L, Mk, Eexp, M2 = _ssd_tables(TT, TT)
    prm = _ssd_params(conv_w, conv_b, dt_bias, a_log, d, norm_w)
    return pl.pallas_call(
        _ssd_prompt_kernel,
        grid=(B, nT),
        in_specs=[pl.BlockSpec((TT, SSD_IN_W), lambda b, t: (b * nT + t, 0))]
                 + [_const(p.shape) for p in prm]
                 + [_const(L.shape), _const(Mk.shape), _const(Eexp.shape), _const(M2.shape)],
        out_specs=[pl.BlockSpec((TT, SSD_WIDTH), lambda b, t: (b * nT + t, 0)),
                   pl.BlockSpec((1, SSD_H, SSD_N, SSD_P), lambda b, t: (b, 0, 0, 0)),
                   pl.BlockSpec((1, SSD_CONV_W - 1, SSD_CONV_DIM), lambda b, t: (b, 0, 0))],
        out_shape=[jax.ShapeDtypeStruct((B * T, SSD_WIDTH), BF),
                   jax.ShapeDtypeStruct((B, SSD_H, SSD_N, SSD_P), F32),
                   jax.ShapeDtypeStruct((B, SSD_CONV_W - 1, SSD_CONV_DIM), F32)],
        scratch_shapes=[pltpu.VMEM((SSD_G * SSD_N, SSD_WIDTH), F32),
                        pltpu.VMEM((TT + 8, SSD_CONV_DIM), F32)],
        compiler_params=_cp(2),
        name="ssd_prompt",
    )(sin_, *prm, L, Mk, Eexp, M2)


SEQ_TILE = 8


def _tile_lanes(n_rep, width):
    return np.tile(np.eye(width, dtype=np.float32), (1, n_rep))


def _fold_head_blocks(ubd):
    a = ubd[:, 0:LANE] + ubd[:, LANE:2 * LANE]
    return (a + pltpu.roll(a, LANE // 2, 1))[:, 0:LANE // 2]


def _col_bcast(row8, ones_ref):
    first = lax.broadcasted_iota(jnp.int32, (8, 1), 0) == 0
    hi, mid, lo = _split3(jnp.where(first, row8, 0.0))
    ones = ones_ref[...]
    return _dot_tn(hi, ones) + (_dot_tn(mid, ones) + _dot_tn(lo, ones))


def _gla_sample_kernel(x_ref, s0_ref, wg_ref, bg_ref, nw_ref, L_ref, E_ref, EA_ref, M_ref, T4_ref, ones_ref,
                       o_ref, sn_ref, kp_ref, gp_ref, vp_ref, oi_ref, *, c):
    TT = x_ref.shape[0]
    PAD = kp_ref.shape[0] - TT
    q, k, v, r, g = _gla_front(x_ref, wg_ref, bg_ref, L_ref)
    kp_ref[0:PAD, :] = jnp.zeros((PAD, LANE), F32)
    gp_ref[0:PAD, :] = jnp.zeros((PAD, LANE), F32)
    vp_ref[0:PAD, :] = jnp.zeros((PAD, 2 * LANE), F32)
    kp_ref[PAD:PAD + TT, :] = k
    gp_ref[PAD:PAD + TT, :] = g
    vp_ref[PAD:PAD + TT, :] = v
    o = _gla_intra(q, g, kp_ref, gp_ref, vp_ref, E_ref, c)
    qe = (q * jnp.exp(g)).astype(BF)
    M = M_ref[...]
    for s in range(TT // c):
        lo = s * c
        S0 = s0_ref[s].reshape(GLA_H * GLA_DK, GLA_DV)
        Sbd = (_dot(S0.astype(BF), T4_ref[...]) * M).astype(BF)
        oi_ref[lo:lo + c, :] = _dot(qe[lo:lo + c, :], Sbd)
        gl = g[lo + c - 1:lo + c, :]
        ke = (k[lo:lo + c, :] * jnp.exp(gl - g[lo:lo + c, :])).astype(BF)
        u = _fold_head_blocks(_dot_tn(ke, v[lo:lo + c, :].astype(BF)) * M)
        acol = _col_bcast(jnp.broadcast_to(jnp.exp(gl), (8, LANE)), ones_ref)
        sn_ref[s] = (acol * S0 + u).reshape(GLA_H, GLA_DK, GLA_DV)
    o = o + oi_ref[...]
    o_ref[...] = _gla_norm_gate(o, r, nw_ref, EA_ref).astype(o_ref.dtype)


def _gla_sample_call(gin, s0, B, T, w_gate, b_gate, norm_w):
    TT = SEQ_TILE * T
    L, E, EA, _ = _gla_tables(TT, T)
    M = jnp.asarray(_head_block_mask(GLA_DK, GLA_DV, GLA_H), F32)
    T4 = jnp.asarray(_tile_lanes(GLA_H, GLA_DV), BF)
    ones = jnp.ones((8, GLA_DV), BF)
    wg, bg, nw = _gla_params(w_gate, b_gate, norm_w)
    PAD = 8
    sspec = pl.BlockSpec((SEQ_TILE, GLA_H, GLA_DK, GLA_DV), lambda i: (i, 0, 0, 0))
    return pl.pallas_call(
        functools.partial(_gla_sample_kernel, c=T),
        grid=(B // SEQ_TILE,),
        in_specs=[pl.BlockSpec((TT, GLA_IN_W), lambda i: (i, 0)), sspec,
                  _const(wg.shape), _const(bg.shape), _const(nw.shape),
                  _const(L.shape), _const(E.shape), _const(EA.shape), _const(M.shape), _const(T4.shape),
                  _const(ones.shape)],
        out_specs=[pl.BlockSpec((TT, GLA_WIDTH), lambda i: (i, 0)), sspec],
        out_shape=[jax.ShapeDtypeStruct((B * T, GLA_WIDTH), BF),
                   jax.ShapeDtypeStruct((B, GLA_H, GLA_DK, GLA_DV), F32)],
        scratch_shapes=[pltpu.VMEM((TT + PAD, LANE), F32),
                        pltpu.VMEM((TT + PAD, LANE), F32),
                        pltpu.VMEM((TT + PAD, 2 * LANE), F32),
                        pltpu.VMEM((TT, 2 * LANE), F32)],
        compiler_params=_cp(1),
        name="gla_sample",
    )(gin, s0, wg, bg, nw, L, E, EA, M, T4, ones)


def _ret_sample_kernel(x_ref, s0_ref, cos_ref, sin_ref, D_ref, rd_ref, kd_ref, G_ref, M_ref, EA_ref, nw_ref, T4_ref,
                       o_ref, sn_ref, oi_ref, *, c):
    TT = x_ref.shape[0]
    q, k, v, rg = _ret_front(x_ref, cos_ref, sin_ref)
    o = _ret_intra(q, k, v, D_ref)
    qd = (q * rd_ref[...]).astype(BF)
    kd = (k * kd_ref[...]).astype(BF)
    vb = v.astype(BF)
    M = M_ref[...]
    for s in range(TT // c):
        lo = s * c
        S0 = s0_ref[s].reshape(RET_H * RET_DK, RET_DV)
        Sbd = (_dot(S0.astype(BF), T4_ref[...]) * M).astype(BF)
        oi_ref[lo:lo + c, :] = _dot(qd[lo:lo + c, :], Sbd)
        u = _fold_head_blocks(_dot_tn(kd[lo:lo + c, :], vb[lo:lo + c, :]) * M)
        sn_ref[s] = (G_ref[...] * S0 + u).reshape(RET_H, RET_DK, RET_DV)
    o = o + oi_ref[...]
    o_ref[...] = _ret_norm_gate(o, rg, nw_ref, EA_ref).astype(o_ref.dtype)


def _ret_sample_call(rin, s0, B, T, norm_w):
    TT = SEQ_TILE * T
    cos, sin = _rope_tables(PAST_LEN + jnp.arange(T, dtype=jnp.int32))
    cos, sin = jnp.tile(cos, (SEQ_TILE, 1)), jnp.tile(sin, (SEQ_TILE, 1))
    lg = _ret_log_gamma()
    i = np.arange(TT)
    same = (i[:, None] // T == i[None, :] // T) & (i[:, None] >= i[None, :])
    Dm = jnp.asarray(np.exp(lg[:, None, None] * (i[:, None] - i[None, :])[None]) * same[None], F32)
    tt = i % T
    rd = jnp.asarray(np.repeat(np.exp(lg[None, :] * (tt[:, None] + 1)), RET_DK, 1), F32)
    kd = jnp.asarray(np.repeat(np.exp(lg[None, :] * (T - 1 - tt[:, None])), RET_DK, 1), F32)
    G = jnp.asarray(np.repeat(np.repeat(np.exp(lg * T), RET_DK)[:, None], RET_DV, 1), F32)
    M = jnp.asarray(_head_block_mask(RET_DK, RET_DV, RET_H), F32)
    EA = jnp.asarray(_head_block_mask(RET_DV, RET_DV, RET_H) / RET_DV, BF)
    T4 = jnp.asarray(_tile_lanes(RET_H, RET_DV), BF)
    nw = norm_w.reshape(1, -1)
    sspec = pl.BlockSpec((SEQ_TILE, RET_H, RET_DK, RET_DV), lambda i: (i, 0, 0, 0))
    consts = (cos, sin, Dm, rd, kd, G, M, EA, nw, T4)
    return pl.pallas_call(
        functools.partial(_ret_sample_kernel, c=T),
        grid=(B // SEQ_TILE,),
        in_specs=[pl.BlockSpec((TT, RET_IN_W), lambda i: (i, 0)), sspec] + [_const(a.shape) for a in consts],
        out_specs=[pl.BlockSpec((TT, RET_WIDTH), lambda i: (i, 0)), sspec],
        out_shape=[jax.ShapeDtypeStruct((B * T, RET_WIDTH), BF),
                   jax.ShapeDtypeStruct((B, RET_H, RET_DK, RET_DV), F32)],
        scratch_shapes=[pltpu.VMEM((TT, RET_WIDTH), F32)],
        compiler_params=_cp(1),
        name="ret_sample",
    )(rin, s0, *consts)


def _ssd_sample_kernel(x_ref, c0_ref, s0_ref, cw_ref, cb_ref, dtb_ref, alog_ref, dexp_ref, nw_ref,
                       L_ref, Mk_ref, Eexp_ref, Bl_ref, R2_ref, T8_ref, T8T_ref, M8_ref, ones_ref,
                       o_ref, sn_ref, cn_ref, xp_ref, oi_ref, *, c):
    TT = x_ref.shape[0]
    ns = TT // c
    RP = 2 * c
    xp_ref[...] = jnp.zeros_like(xp_ref)
    z = x_ref[:, 0:SSD_WIDTH]
    sdt = x_ref[:, SSD_WIDTH + SSD_CONV_DIM:SSD_IN_W]
    for s in range(ns):
        base = 8 + s * RP
        xp_ref[base + c - (SSD_CONV_W - 1):base + c, :] = c0_ref[s]
        xp_ref[base + c:base + RP, :] = x_ref[s * c:(s + 1) * c, SSD_WIDTH:SSD_WIDTH + SSD_CONV_DIM]
    conv = _ssd_conv(xp_ref, cw_ref, cb_ref, ns * RP)
    xbc = _silu(conv.reshape(ns, RP, SSD_CONV_DIM)[:, c:RP, :].reshape(TT, SSD_CONV_DIM))
    for s in range(ns):
        base = 8 + s * RP
        cn_ref[s] = xp_ref[base + RP - (SSD_CONV_W - 1):base + RP, :]
    xs = xbc[:, 0:SSD_WIDTH]
    bm = xbc[:, SSD_WIDTH:SSD_WIDTH + LANE]
    cm = xbc[:, SSD_WIDTH + LANE:SSD_CONV_DIM]

    dt = _softplus(sdt + dtb_ref[...])
    la = dt * (-jnp.exp(alog_ref[...]))
    g = _dot_3x(L_ref[...], la)
    gl = _dot_3x(Bl_ref[...], g)
    Eexp = Eexp_ref[...]
    eg_x = _dot_x2(jnp.exp(g), Eexp)
    cw_x = _dot_x2(dt * jnp.exp(gl - g), Eexp)
    egl_x = _dot_x2(jnp.exp(gl), Eexp)
    o = _ssd_intra(xs, bm, cm, g, dt, Mk_ref)

    Cx = _dot(cm.astype(BF), R2_ref[...])
    Bx = _dot(bm.astype(BF), R2_ref[...])
    Xw = xs * cw_x
    M8 = M8_ref[...]
    nh = SSD_H

    def rows_by_head(a):
        return jnp.concatenate([a] * nh, axis=0) * M8

    for s in range(ns):
        lo = s * c
        S0 = s0_ref[s].reshape(SSD_H * SSD_N, SSD_P)
        oi = _dot(rows_by_head(Cx[lo:lo + c, :]).astype(BF), S0.astype(BF))
        oix = _dot_x2(oi, T8_ref[...]) * M8
        acc = oix[0:c, :]
        for h in range(1, nh):
            acc = acc + oix[h * c:(h + 1) * c, :]
        oi_ref[lo:lo + c, :] = acc
        Xst = _dot(rows_by_head(Xw[lo:lo + c, :]).astype(BF), T8T_ref[...])
        u = _dot_tn(rows_by_head(Bx[lo:lo + c, :]).astype(BF), Xst.astype(BF))
        acol = _col_bcast(egl_x[lo:lo + c, :], ones_ref)
        sn_ref[s] = (acol * S0 + u).reshape(SSD_H, SSD_N, SSD_P)

    o = o + eg_x * oi_ref[...]
    y = (o + dexp_ref[...] * xs) * _silu(z)
    ms = jnp.mean(y * y, axis=-1, keepdims=True)
    o_ref[...] = (y * lax.rsqrt(ms + EPS) * nw_ref[...]).astype(o_ref.dtype)


def _ssd_sample_call(sin_, c0, s0, B, T, conv_w, conv_b, dt_bias, a_log, d, norm_w):
    TT = SEQ_TILE * T
    L, Mk, Eexp, _ = _ssd_tables(TT, T)
    i = np.arange(TT)
    Bl = jnp.asarray((i[None, :] == (i[:, None] // T) * T + T - 1).astype(np.float32), BF)
    hpg = SSD_H // SSD_G
    R2 = np.zeros((LANE, SSD_H * SSD_N), np.float32)
    for h in range(SSD_H):
        R2[(h // hpg) * SSD_N:(h // hpg + 1) * SSD_N, h * SSD_N:(h + 1) * SSD_N] = np.eye(SSD_N)
    T8 = _tile_lanes(SSD_H, SSD_P)
    M8 = _head_block_mask(T, SSD_P, SSD_H)
    tabs = (L, Mk, Eexp, Bl, jnp.asarray(R2, BF), jnp.asarray(T8, BF), jnp.asarray(T8.T, BF), jnp.asarray(M8, F32),
            jnp.ones((8, SSD_P), BF))
    prm = _ssd_params(conv_w, conv_b, dt_bias, a_log, d, norm_w)
    sspec = pl.BlockSpec((SEQ_TILE, SSD_H, SSD_N, SSD_P), lambda i: (i, 0, 0, 0))
    cspec = pl.BlockSpec((SEQ_TILE, SSD_CONV_W - 1, SSD_CONV_DIM), lambda i: (i, 0, 0))
    return pl.pallas_call(
        functools.partial(_ssd_sample_kernel, c=T),
        grid=(B // SEQ_TILE,),
        in_specs=[pl.BlockSpec((TT, SSD_IN_W), lambda i: (i, 0)), cspec, sspec]
                 + [_const(p.shape) for p in prm] + [_const(a.shape) for a in tabs],
        out_specs=[pl.BlockSpec((TT, SSD_WIDTH), lambda i: (i, 0)), sspec, cspec],
        out_shape=[jax.ShapeDtypeStruct((B * T, SSD_WIDTH), BF),
                   jax.ShapeDtypeStruct((B, SSD_H, SSD_N, SSD_P), F32),
                   jax.ShapeDtypeStruct((B, SSD_CONV_W - 1, SSD_CONV_DIM), F32)],
        scratch_shapes=[pltpu.VMEM((8 + SEQ_TILE * 2 * T, SSD_CONV_DIM), F32),
                        pltpu.VMEM((TT, SSD_WIDTH), F32)],
        compiler_params=_cp(1),
        name="ssd_sample",
    )(sin_, c0, s0, *prm, *tabs)


def _outproj_kernel(x_ref, g_ref, og_ref, or_ref, os_ref, w_ref, lg_ref, lb_ref, o_ref):
    bB, bT, D = x_ref.shape
    mix = (_dot(og_ref[...], w_ref[0:GLA_WIDTH, :])
           + _dot(or_ref[...], w_ref[GLA_WIDTH:GLA_WIDTH + RET_WIDTH, :])
           + _dot(os_ref[...], w_ref[GLA_WIDTH + RET_WIDTH:D, :]))
    y = ALPHA * x_ref[...] + g_ref[...] * mix.reshape(bB, bT, D)
    o_ref[...] = _layer_norm(y, lg_ref[...], lb_ref[...])


def _outproj_call(x3, g1, og, orr, os_, w_out, ln_g, ln_b):
    B, T, D = x3.shape
    bB, bT = _tok_tiles(B, T)
    nT = T // bT
    R = bB * bT
    xmap = lambda i, j: (i, j, 0)
    mmap = lambda i, j: (i, 0, 0)
    rmap = lambda i, j: (i * nT + j, 0)
    return pl.pallas_call(
        _outproj_kernel,
        grid=(B // bB, nT),
        in_specs=[pl.BlockSpec((bB, bT, D), xmap),
                  pl.BlockSpec((bB, 1, D), mmap),
                  pl.BlockSpec((R, GLA_WIDTH), rmap),
                  pl.BlockSpec((R, RET_WIDTH), rmap),
                  pl.BlockSpec((R, SSD_WIDTH), rmap),
                  _const((D, D)), _const((1, 1, D)), _const((1, 1, D))],
        out_specs=pl.BlockSpec((bB, bT, D), xmap),
        out_shape=jax.ShapeDtypeStruct((B, T, D), F32),
        compiler_params=_cp(2),
        name="out_proj_ln",
    )(x3, g1, og, orr, os_, w_out, ln_g.reshape(1, 1, D), ln_b.reshape(1, 1, D))


ROUTE_OFF = 8


def _moe_route_t(lt):
    R = lt.shape[1]
    neg = jnp.float32(-jnp.inf)
    row8 = lax.broadcasted_iota(jnp.int32, (8, 1), 0)
    lg = jnp.where(row8 < MOE_GROUPS, lt[0:8, :], neg)
    mg = jnp.max(lg, axis=0, keepdims=True)
    gsel = jnp.min(jnp.where(lg == mg, row8, 8), axis=0, keepdims=True)
    g_gate = 1.0 / jnp.sum(jnp.exp(lg - mg), axis=0, keepdims=True)
    rowe = lax.broadcasted_iota(jnp.int32, (MOE_EXPERTS, 1), 0)
    le = jnp.where((rowe // MOE_PER_GROUP) == gsel, lt[ROUTE_OFF:ROUTE_OFF + MOE_EXPERTS, :], neg)
    m1 = jnp.max(le, axis=0, keepdims=True)
    i1 = jnp.min(jnp.where(le == m1, rowe, MOE_EXPERTS), axis=0, keepdims=True)
    le2 = jnp.where(rowe == i1, neg, le)
    m2 = jnp.max(le2, axis=0, keepdims=True)
    i2 = jnp.min(jnp.where(le2 == m2, rowe, MOE_EXPERTS), axis=0, keepdims=True)
    e2 = jnp.exp(m2 - m1)
    w1 = g_gate / (1.0 + e2)
    w2 = g_gate * e2 / (1.0 + e2)
    comb = jnp.where(rowe == i1, w1, jnp.where(rowe == i2, w2, 0.0))
    cg = comb[0:4, :]
    for g in range(1, MOE_GROUPS):
        cg = cg + comb[g * MOE_PER_GROUP:(g + 1) * MOE_PER_GROUP, :]
    return gsel, cg, comb


MOE_SUB = 256
MOE_BLK = 16
MOE_ROWS = 256
MOE_NPS = MOE_SUB + MOE_GROUPS * MOE_BLK
MOE_MAXB = MOE_SUB // MOE_BLK + 1


def _moe_kernel(x_ref, sc_ref, sh_ref, g_ref, wr_ref, br_ref, us_ref, w1_ref, w3_ref, w2_ref, lg_ref, lb_ref,
                o_ref, hb_ref, cwb_ref, hp_ref, cwp_ref, yp_ref, pos_ref,
                cgrp_ref, fill_ref, cur_ref, na_ref, dst_ref, nb_ref, so_ref, *, n_steps):
    bB, bT, D = x_ref.shape
    R = bB * bT
    n_q = R // MOE_SUB
    s = pl.program_id(1)
    x = x_ref[...]
    row8 = lax.broadcasted_iota(jnp.int32, (8, 1), 0)
    slot = lax.broadcasted_iota(jnp.int32, (MOE_NPS, 1), 0).astype(F32)

    @pl.when((pl.program_id(0) == 0) & (s == 0))
    def _():
        hb_ref[...] = jnp.zeros_like(hb_ref)
        cwb_ref[...] = jnp.zeros_like(cwb_ref)
        yp_ref[...] = jnp.zeros_like(yp_ref)

    @pl.when(s == 0)
    def _():
        na_ref[0] = 0
        for g in range(MOE_GROUPS):
            cur_ref[g] = -1
            fill_ref[g] = 0

    @pl.when(s < n_steps)
    def _():
        h = (x * (1.0 + sc_ref[...]) + sh_ref[...]).reshape(R, D)
        for q in range(n_q):
            u = s * n_q + q
            hq = h[q * MOE_SUB:(q + 1) * MOE_SUB, :].astype(BF)
            gsel, cg, _ = _moe_route_t(_dot_nt(wr_ref[...], hq) + br_ref[...])
            onehot = jnp.where(row8 == gsel, 1.0, 0.0)
            rank = _dot(onehot.astype(BF), us_ref[...])
            cnt = jnp.sum(onehot, axis=1, keepdims=True)
            seg = jnp.ceil(cnt * (1.0 / MOE_BLK)) * MOE_BLK
            off = jnp.zeros((8, 1), F32)
            for g in range(1, MOE_GROUPS):
                off = off + jnp.where(row8 >= g, seg[g - 1:g, :], 0.0)
            pos = jnp.sum(onehot * (off + rank), axis=0, keepdims=True)
            pos_ref[u] = jnp.broadcast_to(pos, (8, MOE_SUB))
            perm = jnp.where(slot == pos, 1.0, 0.0).astype(BF)
            hp_ref[...] = _dot(perm, hq).astype(BF)
            cg8 = jnp.concatenate([cg, jnp.zeros((4, MOE_SUB), F32)], axis=0)
            cg_hi = cg8.astype(BF)
            cg_lo = (cg8 - cg_hi.astype(F32)).astype(BF)
            cwp_ref[...] = _dot_nt(perm, cg_hi) + _dot_nt(perm, cg_lo)
            for g in range(MOE_GROUPS):
                so = off[g, 0].astype(jnp.int32)
                nb = (seg[g, 0] * (1.0 / MOE_BLK)).astype(jnp.int32)
                so_ref[u * MOE_GROUPS + g] = so
                nb_ref[u * MOE_GROUPS + g] = nb

                def put(k, carry, g=g, so=so, u=u):
                    f = fill_ref[g]
                    c = cur_ref[g]
                    na = na_ref[0]
                    fresh = (c < 0) | (f >= MOE_ROWS)
                    c = jnp.where(fresh, na, c)
                    f = jnp.where(fresh, 0, f)
                    cgrp_ref[c] = g
                    na_ref[0] = jnp.where(fresh, na + 1, na)
                    dst = pl.multiple_of(c * MOE_ROWS + f, MOE_BLK)
                    src = pl.multiple_of(so + k * MOE_BLK, MOE_BLK)
                    hb_ref[pl.ds(dst, MOE_BLK), :] = hp_ref[pl.ds(src, MOE_BLK), :]
                    cwb_ref[pl.ds(dst, MOE_BLK), :] = cwp_ref[pl.ds(src, MOE_BLK), :]
                    dst_ref[(u * MOE_GROUPS + g) * MOE_MAXB + k] = dst
                    cur_ref[g] = c
                    fill_ref[g] = f + MOE_BLK
                    return carry

                lax.fori_loop(0, nb, put, 0)

    @pl.when(s == n_steps - 1)
    def _():
        def chunk(c, carry):
            g = cgrp_ref[c]
            start = pl.multiple_of(c * MOE_ROWS, MOE_ROWS)
            hc = hb_ref[pl.ds(start, MOE_ROWS), :]
            cw = cwb_ref[pl.ds(start, MOE_ROWS), :]
            acc = jnp.zeros((MOE_ROWS, D), F32)
            for j in range(MOE_PER_GROUP):
                e = g * MOE_PER_GROUP + j
                hid = _silu(_dot(hc, w1_ref[e])) * _dot(hc, w3_ref[e]) * cw[:, j:j + 1]
                acc = acc + _dot(hid.astype(BF), w2_ref[e])
            hb_ref[pl.ds(start, MOE_ROWS), :] = acc.astype(BF)
            return carry

        lax.fori_loop(0, na_ref[0], chunk, 0)

    @pl.when(s >= n_steps)
    def _():
        ys = []
        for q in range(n_q):
            u = (s - n_steps) * n_q + q
            for g in range(MOE_GROUPS):
                so = so_ref[u * MOE_GROUPS + g]

                def take(k, carry, g=g, so=so, u=u):
                    src = pl.multiple_of(dst_ref[(u * MOE_GROUPS + g) * MOE_MAXB + k], MOE_BLK)
                    dst = pl.multiple_of(so + k * MOE_BLK, MOE_BLK)
                    yp_ref[pl.ds(dst, MOE_BLK), :] = hb_ref[pl.ds(src, MOE_BLK), :]
                    return carry

                lax.fori_loop(0, nb_ref[u * MOE_GROUPS + g], take, 0)
            perm = jnp.where(slot == pos_ref[u][0:1, :], 1.0, 0.0).astype(BF)
            ys.append(_dot_tn(perm, yp_ref[...]))
        y = jnp.concatenate(ys, axis=0)
        z = ALPHA * x + g_ref[...] * y.reshape(bB, bT, D)
        o_ref[...] = _layer_norm(z, lg_ref[...], lb_ref[...])


def _resident(shape):
    return pl.BlockSpec(shape, lambda *_: (0,) * len(shape), pipeline_mode=pl.Buffered(1))


def _moe_call(x3, sc, sh, g2, wr, br, w1, w3, w2, ln_g, ln_b):
    B, T, D = x3.shape
    bB, bT = _tok_tiles(B, T)
    R = bB * bT
    if bB == 1:
        n_pools, n_steps = B, T // bT
        xmap = lambda p, s: (p, s % n_steps, 0)
        omap = lambda p, s: (p, jnp.maximum(s - n_steps, 0), 0)
        mmap = lambda p, s: (p, 0, 0)
    else:
        n_pools, n_steps = 1, B // bB
        xmap = lambda p, s: (s % n_steps, 0, 0)
        omap = lambda p, s: (jnp.maximum(s - n_steps, 0), 0, 0)
        mmap = xmap
    n_sub = n_steps * (R // MOE_SUB)
    n_chunks = n_sub * MOE_SUB // MOE_ROWS + MOE_GROUPS
    us = jnp.asarray(np.triu(np.ones((MOE_SUB, MOE_SUB), np.float32), 1), BF)
    smem = lambda n: pltpu.SMEM((n,), jnp.int32)
    return pl.pallas_call(
        functools.partial(_moe_kernel, n_steps=n_steps),
        grid=(n_pools, 2 * n_steps),
        in_specs=[pl.BlockSpec((bB, bT, D), xmap),
                  pl.BlockSpec((bB, 1, D), mmap), pl.BlockSpec((bB, 1, D), mmap), pl.BlockSpec((bB, 1, D), mmap),
                  _const(wr.shape), _const(br.shape), _const(us.shape),
                  _resident(w1.shape), _resident(w3.shape), _resident(w2.shape),
                  _const((1, 1, D)), _const((1, 1, D))],
        out_specs=pl.BlockSpec((bB, bT, D), omap),
        out_shape=jax.ShapeDtypeStruct((B, T, D), F32),
        scratch_shapes=[pltpu.VMEM((n_chunks * MOE_ROWS, D), BF), pltpu.VMEM((n_chunks * MOE_ROWS, 8), F32),
                        pltpu.VMEM((MOE_NPS, D), BF), pltpu.VMEM((MOE_NPS, 8), F32), pltpu.VMEM((MOE_NPS, D), BF),
                        pltpu.VMEM((n_sub, 8, MOE_SUB), F32),
                        smem(n_chunks), smem(MOE_GROUPS), smem(MOE_GROUPS), smem(1),
                        smem(n_sub * MOE_GROUPS * MOE_MAXB), smem(n_sub * MOE_GROUPS), smem(n_sub * MOE_GROUPS)],
        compiler_params=_cp(2),
        name="moe_ln",
    )(x3, sc, sh, g2, wr, br, us, w1, w3, w2, ln_g.reshape(1, 1, D), ln_b.reshape(1, 1, D))


def _router_params(w_group, b_group, w_expert, b_expert):
    wr = jnp.zeros((LANE, D_MODEL), F32).at[:MOE_GROUPS].set(w_group.T)
    wr = wr.at[ROUTE_OFF:ROUTE_OFF + MOE_EXPERTS].set(w_expert.T)
    br = jnp.zeros((LANE, 1), F32).at[:MOE_GROUPS, 0].set(b_group).at[ROUTE_OFF:ROUTE_OFF + MOE_EXPERTS, 0].set(b_expert)
    return wr.astype(BF), br


def kernel(x_prompt, x_sample, c_prompt, c_sample, state_gla, state_ret, state_ssd, state_conv, w_ada, b_ada, w_in, gla_w_gate, gla_b_gate, gla_norm, ret_norm, ssd_conv_w, ssd_conv_b, ssd_dt_bias, ssd_a_log, ssd_d, ssd_norm, w_out, ln1_g, ln1_b, moe_w_group, moe_b_group, moe_w_expert, moe_b_expert, moe_w1, moe_w3, moe_w2, ln2_g, ln2_b):
    Bp, Tp, D = x_prompt.shape
    Bs, Ts, _ = x_sample.shape
    n_ga = 128 + 128 + 256 + GLA_GATE_RANK
    w_in_p = jnp.concatenate(
        [w_in[:, :, :n_ga], jnp.zeros((DEPTH, D, LANE - GLA_GATE_RANK), F32), w_in[:, :, n_ga:],
         jnp.zeros((DEPTH, D, LANE - SSD_H), F32)], axis=-1).astype(BF)
    w_out_b = w_out.astype(BF)
    w1_b, w3_b, w2_b = moe_w1.astype(BF), moe_w3.astype(BF), moe_w2.astype(BF)

    mod = _mod_call(jnp.concatenate([c_prompt, c_sample], axis=0), w_ada, b_ada)

    def trunk(x, mod_rows, B, T, states):
        new = [[], [], [], []]
        for l in range(DEPTH):
            sh1, sc1, g1, sh2, sc2, g2 = (mod_rows[l][:, None, i * D:(i + 1) * D] for i in range(6))
            gin, rin, sin_ = _inproj_call(x, sc1, sh1, w_in_p[l])
            if states is None:
                og, s_gla = _gla_prompt_call(gin, B, T, gla_w_gate[l], gla_b_gate[l], gla_norm[l])
                orr, s_ret = _ret_prompt_call(rin, B, T, ret_norm[l])
                os_, s_ssd, s_conv = _ssd_prompt_call(sin_, B, T, ssd_conv_w[l], ssd_conv_b[l], ssd_dt_bias[l],
                                                      ssd_a_log[l], ssd_d[l], ssd_norm[l])
            else:
                og, s_gla = _gla_sample_call(gin, states[0][l], B, T, gla_w_gate[l], gla_b_gate[l], gla_norm[l])
                orr, s_ret = _ret_sample_call(rin, states[1][l], B, T, ret_norm[l])
                os_, s_ssd, s_conv = _ssd_sample_call(sin_, states[3][l], states[2][l], B, T, ssd_conv_w[l],
                                                      ssd_conv_b[l], ssd_dt_bias[l], ssd_a_log[l], ssd_d[l],
                                                      ssd_norm[l])
            x = _outproj_call(x, g1, og, orr, os_, w_out_b[l], ln1_g[l], ln1_b[l])
            wr, br = _router_params(moe_w_group[l], moe_b_group[l], moe_w_expert[l], moe_b_expert[l])
            x = _moe_call(x, sc2, sh2, g2, wr, br, w1_b[l], w3_b[l], w2_b[l], ln2_g[l], ln2_b[l])
            for acc, s in zip(new, (s_gla, s_ret, s_ssd, s_conv)):
                acc.append(s)
        return (x,) + tuple(jnp.stack(a) for a in new)

    y_p, gla_p, ret_p, ssd_p, conv_p = trunk(x_prompt, mod[:, :Bp], Bp, Tp, None)
    y_s, gla_s, ret_s, ssd_s, conv_s = trunk(x_sample, mod[:, Bp:], Bs, Ts,
                                             (state_gla, state_ret, state_ssd, state_conv))
    return (y_p, y_s, gla_p, ret_p, ssd_p, conv_p, gla_s, ret_s, ssd_s, conv_s)
```

```python
import functools
import math

import numpy as np
import jax
import jax.numpy as jnp
from jax import lax
from jax.experimental import pallas as pl
from jax.experimental.pallas import tpu as pltpu

F32 = jnp.float32
BF = jnp.bfloat16

D_MODEL = 1024
DEPTH = 2
PAST_LEN = 16384
GLA_H, GLA_DK, GLA_DV = 4, 32, 64
GLA_WIDTH = GLA_H * GLA_DV
GLA_GATE_RANK = 16
GLA_GATE_TEMP = 16.0
GLA_CHUNK = 16
RET_H, RET_DK, RET_DV = 4, 64, 64
RET_WIDTH = RET_H * RET_DV
ROPE_BASE = 10000.0
SSD_H, SSD_P, SSD_G, SSD_N = 8, 64, 2, 64
SSD_WIDTH = SSD_H * SSD_P
SSD_CONV_W = 4
SSD_CONV_DIM = SSD_WIDTH + 2 * SSD_G * SSD_N
MOE_GROUPS, MOE_PER_GROUP = 4, 4
MOE_EXPERTS = MOE_GROUPS * MOE_PER_GROUP
MOE_FF = 256
ALPHA = (2 * DEPTH) ** 0.25
EPS = 1e-5

LANE = 128
GLA_IN_W = 128 + 128 + 256 + LANE + 256
RET_IN_W = 4 * 256
SSD_IN_W = 512 + SSD_CONV_DIM + LANE
IN_W = GLA_IN_W + RET_IN_W + SSD_IN_W
VMEM_LIMIT = 56 * 1024 * 1024


def _cp(n_axes, vmem=VMEM_LIMIT):
    return pltpu.CompilerParams(dimension_semantics=("arbitrary",) * n_axes, vmem_limit_bytes=vmem)


def _dot(a, b):
    return jnp.dot(a, b, preferred_element_type=F32)


def _dot_nt(a, b):
    return lax.dot_general(a, b, (((1,), (1,)), ((), ())), preferred_element_type=F32)


def _dot_tn(a, b):
    return lax.dot_general(a, b, (((0,), (0,)), ((), ())), preferred_element_type=F32)


def _split3(x):
    hi = x.astype(BF)
    r = x - hi.astype(F32)
    mid = r.astype(BF)
    lo = (r - mid.astype(F32)).astype(BF)
    return hi, mid, lo


def _dot_x3(x, e):
    hi, mid, lo = _split3(x)
    return _dot(hi, e) + (_dot(mid, e) + _dot(lo, e))


def _dot_x2(x, e):
    hi = x.astype(BF)
    lo = (x - hi.astype(F32)).astype(BF)
    return _dot(hi, e) + _dot(lo, e)


def _dot_3x(e, x):
    hi, mid, lo = _split3(x)
    return _dot(e, hi) + (_dot(e, mid) + _dot(e, lo))


def _sigmoid(x):
    return 1.0 / (1.0 + jnp.exp(-x))


def _silu(x):
    return x * _sigmoid(x)


def _log_sigmoid(x):
    return jnp.minimum(x, 0.0) - jnp.log(1.0 + jnp.exp(-jnp.abs(x)))


def _softplus(x):
    return jnp.maximum(x, 0.0) + jnp.log(1.0 + jnp.exp(-jnp.abs(x)))


def _layer_norm(x, g, b):
    mu = jnp.mean(x, axis=-1, keepdims=True)
    d = x - mu
    var = jnp.mean(d * d, axis=-1, keepdims=True)
    return d * lax.rsqrt(var + EPS) * g + b


def _mod_kernel(c_ref, w_ref, b_ref, o_ref):
    s = _silu(c_ref[...]).astype(BF)
    o_ref[0] = _dot(s, w_ref[0].astype(BF)) + b_ref[0]


def _mod_call(c_all, w_ada, b_ada):
    R = c_all.shape[0]
    tn = 1536
    return pl.pallas_call(
        _mod_kernel,
        grid=(DEPTH, 6 * D_MODEL // tn),
        in_specs=[pl.BlockSpec((R, D_MODEL), lambda l, j: (0, 0)),
                  pl.BlockSpec((1, D_MODEL, tn), lambda l, j: (l, 0, j)),
                  pl.BlockSpec((1, 1, tn), lambda l, j: (l, 0, j))],
        out_specs=pl.BlockSpec((1, R, tn), lambda l, j: (l, 0, j)),
        out_shape=jax.ShapeDtypeStruct((DEPTH, R, 6 * D_MODEL), F32),
        compiler_params=_cp(2),
        name="ada_mod",
    )(c_all, w_ada, b_ada.reshape(DEPTH, 1, 6 * D_MODEL))


N_IN = 3096
N_GA = 128 + 128 + 256
N_DT = N_IN - SSD_H


def _inproj_kernel(x_ref, sc_ref, sh_ref, wt_ref, og_ref, or_ref, os_ref, w_ref):
    bB, bT, D = x_ref.shape

    @pl.when((pl.program_id(0) == 0) & (pl.program_id(1) == 0))
    def _():
        lane = lax.broadcasted_iota(jnp.int32, (1, LANE), 1)
        for j in range(N_GA // LANE):
            w_ref[:, j * LANE:(j + 1) * LANE] = wt_ref[j * LANE:(j + 1) * LANE, :].T.astype(BF)
        ga = wt_ref[N_GA:N_GA + LANE, :].T
        w_ref[:, N_GA:N_GA + LANE] = jnp.where(lane < GLA_GATE_RANK, ga, 0.0).astype(BF)
        src0, dst0 = N_GA + GLA_GATE_RANK, N_GA + LANE
        for j in range((N_DT - src0) // LANE):
            w_ref[:, dst0 + j * LANE:dst0 + (j + 1) * LANE] = \
                wt_ref[src0 + j * LANE:src0 + (j + 1) * LANE, :].T.astype(BF)
        dt = pltpu.roll(wt_ref[N_IN - LANE:N_IN, :].T, SSD_H, 1)
        w_ref[:, IN_W - LANE:IN_W] = jnp.where(lane < SSD_H, dt, 0.0).astype(BF)

    h = x_ref[...] * (1.0 + sc_ref[...]) + sh_ref[...]
    hb = h.reshape(bB * bT, D).astype(BF)
    og_ref[...] = _dot(hb, w_ref[:, 0:GLA_IN_W])
    or_ref[...] = _dot(hb, w_ref[:, GLA_IN_W:GLA_IN_W + RET_IN_W])
    os_ref[...] = _dot(hb, w_ref[:, GLA_IN_W + RET_IN_W:IN_W])


def _tok_tiles(B, T):
    if T >= 512:
        return 1, 512
    return 512 // T, T


def _inproj_call(x3, sc, sh, wt, l):
    B, T, D = x3.shape
    bB, bT = _tok_tiles(B, T)
    nT = T // bT
    R = bB * bT
    N = B * T
    xmap = lambda i, j: (i, j, 0)
    mmap = lambda i, j: (i, 0, 0)
    omap = lambda i, j: (i * nT + j, 0)
    return pl.pallas_call(
        _inproj_kernel,
        grid=(B // bB, nT),
        in_specs=[pl.BlockSpec((bB, bT, D), xmap),
                  pl.BlockSpec((bB, 1, D), mmap),
                  pl.BlockSpec((bB, 1, D), mmap),
                  _resident_layer(wt.shape, l)],
        out_specs=[pl.BlockSpec((R, GLA_IN_W), omap),
                   pl.BlockSpec((R, RET_IN_W), omap),
                   pl.BlockSpec((R, SSD_IN_W), omap)],
        out_shape=[jax.ShapeDtypeStruct((N, GLA_IN_W), F32),
                   jax.ShapeDtypeStruct((N, RET_IN_W), F32),
                   jax.ShapeDtypeStruct((N, SSD_IN_W), F32)],
        scratch_shapes=[pltpu.VMEM((D, IN_W), BF)],
        compiler_params=_cp(2),
        name="in_proj",
    )(x3, sc, sh, wt)


def _head_block_mask(rows_per, cols_per, n):
    r = np.arange(rows_per * n)[:, None] // rows_per
    c = np.arange(cols_per * n)[None, :] // cols_per
    return (r == c).astype(np.float32)


def _block_tril(n, c):
    i = np.arange(n)[:, None]
    j = np.arange(n)[None, :]
    return ((i // c == j // c) & (j <= i)).astype(np.float32)


def _gla_front(x_ref, wg_ref, bg_ref, L_ref):
    q = x_ref[:, 0:128] * (GLA_DK ** -0.5)
    k = x_ref[:, 128:256]
    v = x_ref[:, 256:512]
    ga = x_ref[:, 512:640]
    r = x_ref[:, 640:896]
    gate = _dot(ga.astype(BF), wg_ref[...]) + bg_ref[...]
    la = _log_sigmoid(gate) * (1.0 / GLA_GATE_TEMP)
    g = _dot_3x(L_ref[...], la)
    return q, k, v, r, g


def _gla_intra(q, g, kp_ref, gp_ref, vp_ref, E_ref, c):
    TT = q.shape[0]
    PAD = kp_ref.shape[0] - TT
    pos = lax.broadcasted_iota(jnp.int32, (TT, 1), 0) & (c - 1)
    o = jnp.zeros((TT, 2 * LANE), F32)
    for s in range(c):
        ks = kp_ref[pl.ds(PAD - s, TT), :]
        gs = gp_ref[pl.ds(PAD - s, TT), :]
        vs = vp_ref[pl.ds(PAD - s, TT), :]
        w = jnp.where(pos >= s, q * ks * jnp.exp(jnp.minimum(g - gs, 0.0)), 0.0)
        o = o + _dot(w.astype(BF), E_ref[...]) * vs
    return o


def _gla_norm_gate(o, r, nw_ref, EA_ref):
    ms = _dot_x3(o * o, EA_ref[...])
    return o * lax.rsqrt(ms + EPS) * nw_ref[...] * _silu(r)


def _gla_prompt_kernel(x_ref, wg_ref, bg_ref, nw_ref, L_ref, E_ref, EA_ref, M_ref,
                       o_ref, sfin_ref, st_ref, kp_ref, gp_ref, vp_ref, oi_ref, u_ref, sb_ref, *, c):
    TT = x_ref.shape[0]
    nc = TT // c
    PAD = kp_ref.shape[0] - TT
    t = pl.program_id(1)

    @pl.when(t == 0)
    def _():
        st_ref[...] = jnp.zeros_like(st_ref)

    q, k, v, r, g = _gla_front(x_ref, wg_ref, bg_ref, L_ref)
    kp_ref[0:PAD, :] = jnp.zeros((PAD, LANE), F32)
    gp_ref[0:PAD, :] = jnp.zeros((PAD, LANE), F32)
    vp_ref[0:PAD, :] = jnp.zeros((PAD, 2 * LANE), F32)
    kp_ref[PAD:PAD + TT, :] = k
    gp_ref[PAD:PAD + TT, :] = g
    vp_ref[PAD:PAD + TT, :] = v
    o = _gla_intra(q, g, kp_ref, gp_ref, vp_ref, E_ref, c)

    M = M_ref[...]
    gl_all = gp_ref[pl.ds(PAD + c - 1, nc, stride=c), :]
    for n in range(nc):
        lo = n * c
        ke = (k[lo:lo + c, :] * jnp.exp(gl_all[n:n + 1, :] - g[lo:lo + c, :])).astype(BF)
        u_ref[n] = _dot_tn(ke, v[lo:lo + c, :].astype(BF)) * M
    a_cols = jnp.concatenate([jnp.exp(gl_all), jnp.zeros((LANE - nc, LANE), F32)], axis=0).T
    S = st_ref[...]
    for n in range(nc):
        sb_ref[n] = S.astype(BF)
        S = a_cols[:, n:n + 1] * S + u_ref[n]
    st_ref[...] = S
    qe = (q * jnp.exp(g)).astype(BF)
    for n in range(nc):
        lo = n * c
        oi_ref[lo:lo + c, :] = _dot(qe[lo:lo + c, :], sb_ref[n])
    o = o + oi_ref[...]
    o_ref[...] = _gla_norm_gate(o, r, nw_ref, EA_ref).astype(o_ref.dtype)

    @pl.when(t == pl.num_programs(1) - 1)
    def _():
        for h in range(GLA_H):
            sfin_ref[0, h] = S[h * GLA_DK:(h + 1) * GLA_DK, h * GLA_DV:(h + 1) * GLA_DV]


def _gla_tables(TT, c):
    L = jnp.asarray(_block_tril(TT, c), BF)
    E = jnp.asarray(_head_block_mask(GLA_DK, GLA_DV, GLA_H), BF)
    EA = jnp.asarray(_head_block_mask(GLA_DV, GLA_DV, GLA_H) / GLA_DV, BF)
    M = jnp.asarray(_head_block_mask(GLA_DK, GLA_DV, GLA_H), F32)
    return L, E, EA, M


def _gla_params(w_gate, b_gate, norm_w):
    wg = jnp.zeros((LANE, GLA_H * GLA_DK), F32).at[:GLA_GATE_RANK].set(w_gate).astype(BF)
    return wg, b_gate.reshape(1, -1), norm_w.reshape(1, -1)


def _const(shape):
    return pl.BlockSpec(shape, lambda *_: (0,) * len(shape))


def _gla_prompt_call(gin, B, T, w_gate, b_gate, norm_w):
    TT, c = 256, GLA_CHUNK
    nT = T // TT
    L, E, EA, M = _gla_tables(TT, c)
    wg, bg, nw = _gla_params(w_gate, b_gate, norm_w)
    PAD = 16
    return pl.pallas_call(
        functools.partial(_gla_prompt_kernel, c=c),
        grid=(B, nT),
        in_specs=[pl.BlockSpec((TT, GLA_IN_W), lambda b, t: (b * nT + t, 0)),
                  _const(wg.shape), _const(bg.shape), _const(nw.shape),
                  _const(L.shape), _const(E.shape), _const(EA.shape), _const(M.shape)],
        out_specs=[pl.BlockSpec((TT, GLA_WIDTH), lambda b, t: (b * nT + t, 0)),
                   pl.BlockSpec((1, GLA_H, GLA_DK, GLA_DV), lambda b, t: (b, 0, 0, 0))],
        out_shape=[jax.ShapeDtypeStruct((B * T, GLA_WIDTH), BF),
                   jax.ShapeDtypeStruct((B, GLA_H, GLA_DK, GLA_DV), F32)],
        scratch_shapes=[pltpu.VMEM((GLA_H * GLA_DK, GLA_H * GLA_DV), F32),
                        pltpu.VMEM((TT + PAD, LANE), F32),
                        pltpu.VMEM((TT + PAD, LANE), F32),
                        pltpu.VMEM((TT + PAD, 2 * LANE), F32),
                        pltpu.VMEM((TT, 2 * LANE), F32),
                        pltpu.VMEM((TT // c, GLA_H * GLA_DK, GLA_H * GLA_DV), F32),
                        pltpu.VMEM((TT // c, GLA_H * GLA_DK, GLA_H * GLA_DV), BF)],
        compiler_params=_cp(2),
        name="gla_prompt",
    )(gin, wg, bg, nw, L, E, EA, M)


def _rope(x, cos, sin_signed):
    lane = lax.broadcasted_iota(jnp.int32, (1, LANE), 1)
    first_half = (lane & (RET_DK - 1)) < RET_DK // 2
    out = []
    for p in range(2):
        xs = x[:, p * LANE:(p + 1) * LANE]
        up = pltpu.roll(xs, LANE - RET_DK // 2, 1)
        dn = pltpu.roll(xs, RET_DK // 2, 1)
        out.append(xs * cos + jnp.where(first_half, up, dn) * sin_signed)
    return jnp.concatenate(out, axis=1)


def _ret_front(x_ref, cos_ref, sin_ref):
    q = _rope(x_ref[:, 0:256], cos_ref[...], sin_ref[...])
    k = _rope(x_ref[:, 256:512], cos_ref[...], sin_ref[...]) * (RET_DK ** -0.5)
    v = x_ref[:, 512:768]
    rg = x_ref[:, 768:1024]
    return q, k, v, rg


def _ret_intra(q, k, v, D_ref):
    lane = lax.broadcasted_iota(jnp.int32, (1, RET_WIDTH), 1)
    kb = k.astype(BF)
    o = jnp.zeros(q.shape, F32)
    for h in range(RET_H):
        hm = (lane // RET_DK) == h
        s = _dot_nt(jnp.where(hm, q, 0.0).astype(BF), kb)
        p = (s * D_ref[h]).astype(BF)
        o = o + _dot(p, jnp.where(hm, v, 0.0).astype(BF))
    return o


def _ret_norm_gate(o, rg, nw_ref, EA_ref):
    mu = _dot_x3(o, EA_ref[...])
    d = o - mu
    var = _dot_x3(d * d, EA_ref[...])
    return d * lax.rsqrt(var + EPS) * nw_ref[...] * _silu(rg)


def _ret_prompt_kernel(x_ref, cos_ref, sin_ref, D_ref, rd_ref, kd_ref, G_ref, M_ref, EA_ref, nw_ref,
                       o_ref, sfin_ref, st_ref):
    t = pl.program_id(1)

    @pl.when(t == 0)
    def _():
        st_ref[...] = jnp.zeros_like(st_ref)

    q, k, v, rg = _ret_front(x_ref, cos_ref, sin_ref)
    o = _ret_intra(q, k, v, D_ref)
    S = st_ref[...]
    o = o + _dot((q * rd_ref[...]).astype(BF), S.astype(BF))
    u = _dot_tn((k * kd_ref[...]).astype(BF), v.astype(BF))
    S = S * G_ref[...] + u * M_ref[...]
    st_ref[...] = S
    o_ref[...] = _ret_norm_gate(o, rg, nw_ref, EA_ref).astype(o_ref.dtype)

    @pl.when(t == pl.num_programs(1) - 1)
    def _():
        for h in range(RET_H):
            sfin_ref[0, h] = S[h * RET_DK:(h + 1) * RET_DK, h * RET_DV:(h + 1) * RET_DV]


def _rope_tables(pos):
    half = RET_DK // 2
    inv = ROPE_BASE ** (-jnp.arange(half, dtype=F32) / half)
    ang = pos.astype(F32)[:, None] * inv[None, :]
    cos, sin = jnp.cos(ang), jnp.sin(ang)
    return jnp.tile(jnp.concatenate([cos, cos], 1), (1, 2)), jnp.tile(jnp.concatenate([-sin, sin], 1), (1, 2))


def _ret_log_gamma():
    return np.log(1.0 - 2.0 ** (-5.0 - np.arange(RET_H, dtype=np.float64)))


def _ret_prompt_call(rin, B, T, norm_w):
    TT = 256
    nT = T // TT
    cos, sin = _rope_tables(jnp.arange(T, dtype=jnp.int32))
    lg = _ret_log_gamma()
    i = np.arange(TT)
    dec = np.exp(lg[:, None, None] * (i[:, None] - i[None, :])[None]) * (i[:, None] >= i[None, :])[None]
    Dm = jnp.asarray(dec, F32)
    rd = jnp.asarray(np.repeat(np.exp(lg[None, :] * (i[:, None] + 1)), RET_DK, 1), F32)
    kd = jnp.asarray(np.repeat(np.exp(lg[None, :] * (TT - 1 - i[:, None])), RET_DK, 1), F32)
    M = _head_block_mask(RET_DK, RET_DV, RET_H)
    G = jnp.asarray(M * np.repeat(np.exp(lg * TT), RET_DK)[:, None], F32)
    M = jnp.asarray(M, F32)
    EA = jnp.asarray(_head_block_mask(RET_DV, RET_DV, RET_H) / RET_DV, BF)
    nw = norm_w.reshape(1, -1)
    return pl.pallas_call(
        _ret_prompt_kernel,
        grid=(B, nT),
        in_specs=[pl.BlockSpec((TT, RET_IN_W), lambda b, t: (b * nT + t, 0)),
                  pl.BlockSpec((TT, LANE), lambda b, t: (t, 0)),
                  pl.BlockSpec((TT, LANE), lambda b, t: (t, 0)),
                  _const(Dm.shape), _const(rd.shape), _const(kd.shape), _const(G.shape), _const(M.shape),
                  _const(EA.shape), _const(nw.shape)],
        out_specs=[pl.BlockSpec((TT, RET_WIDTH), lambda b, t: (b * nT + t, 0)),
                   pl.BlockSpec((1, RET_H, RET_DK, RET_DV), lambda b, t: (b, 0, 0, 0))],
        out_shape=[jax.ShapeDtypeStruct((B * T, RET_WIDTH), BF),
                   jax.ShapeDtypeStruct((B, RET_H, RET_DK, RET_DV), F32)],
        scratch_shapes=[pltpu.VMEM((RET_H * RET_DK, RET_H * RET_DV), F32)],
        compiler_params=_cp(2),
        name="ret_prompt",
    )(rin, cos, sin, Dm, rd, kd, G, M, EA, nw)


def _ssd_conv(xp_ref, cw_ref, cb_ref, TT):
    acc = cb_ref[...] + cw_ref[SSD_CONV_W - 1:SSD_CONV_W, :] * xp_ref[pl.ds(8, TT), :]
    for i in range(SSD_CONV_W - 1):
        acc = acc + cw_ref[i:i + 1, :] * xp_ref[pl.ds(8 - (SSD_CONV_W - 1) + i, TT), :]
    return acc


def _ssd_intra(xs, bm, cm, g, dt, Mk_ref):
    TT = xs.shape[0]
    gT = g.T
    dtT = dt.T
    lane = lax.broadcasted_iota(jnp.int32, (1, LANE), 1)
    lane2 = lax.broadcasted_iota(jnp.int32, (1, 2 * LANE), 1)
    causal = Mk_ref[...] > 0.0
    bmb = bm.astype(BF)
    o_parts = []
    for grp in range(SSD_G):
        cb = _dot_nt(jnp.where((lane // SSD_N) == grp, cm, 0.0).astype(BF), bmb)
        xg = xs[:, grp * 2 * LANE:(grp + 1) * 2 * LANE]
        og = jnp.zeros((TT, 2 * LANE), F32)
        for h4 in range(SSD_H // SSD_G):
            h = grp * (SSD_H // SSD_G) + h4
            dec = jnp.where(causal, jnp.exp(jnp.minimum(g[:, h:h + 1] - gT[h:h + 1, :], 0.0)), 0.0)
            p = (cb * dec * dtT[h:h + 1, :]).astype(BF)
            og = og + _dot(p, jnp.where((lane2 // SSD_P) == h4, xg, 0.0).astype(BF))
        o_parts.append(og)
    return jnp.concatenate(o_parts, axis=1)


def _ssd_prompt_kernel(x_ref, cw_ref, cb_ref, dtb_ref, alog_ref, dexp_ref, nw_ref, L_ref, Mk_ref, Eexp_ref, M2_ref,
                       o_ref, sfin_ref, cfin_ref, st_ref, xp_ref):
    TT = x_ref.shape[0]
    t = pl.program_id(1)

    @pl.when(t == 0)
    def _():
        st_ref[...] = jnp.zeros_like(st_ref)
        xp_ref[0:8, :] = jnp.zeros((8, SSD_CONV_DIM), F32)

    z = x_ref[:, 0:SSD_WIDTH]
    xp_ref[8:8 + TT, :] = x_ref[:, SSD_WIDTH:SSD_WIDTH + SSD_CONV_DIM]
    sdt = x_ref[:, SSD_WIDTH + SSD_CONV_DIM:SSD_IN_W]
    xbc = _silu(_ssd_conv(xp_ref, cw_ref, cb_ref, TT))
    tail = xp_ref[TT:TT + 8, :]
    xp_ref[0:8, :] = tail
    xs = xbc[:, 0:SSD_WIDTH]
    bm = xbc[:, SSD_WIDTH:SSD_WIDTH + LANE]
    cm = xbc[:, SSD_WIDTH + LANE:SSD_CONV_DIM]

    dt = _softplus(sdt + dtb_ref[...])
    la = dt * (-jnp.exp(alog_ref[...]))
    g = _dot_3x(L_ref[...], la)
    gl = g[TT - 1:TT, :]
    Eexp = Eexp_ref[...]
    eg_x = _dot_x2(jnp.exp(g), Eexp)
    cw_x = _dot_x2(dt * jnp.exp(gl - g), Eexp)
    egl_x = _dot_x2(jnp.exp(gl), Eexp)

    o = _ssd_intra(xs, bm, cm, g, dt, Mk_ref)

    S = st_ref[...]
    o = o + eg_x * _dot(cm.astype(BF), S.astype(BF))
    u = _dot_tn(bm.astype(BF), (xs * cw_x).astype(BF))
    S = S * egl_x + u * M2_ref[...]
    st_ref[...] = S

    y = (o + dexp_ref[...] * xs) * _silu(z)
    ms = jnp.mean(y * y, axis=-1, keepdims=True)
    o_ref[...] = (y * lax.rsqrt(ms + EPS) * nw_ref[...]).astype(o_ref.dtype)

    @pl.when(t == pl.num_programs(1) - 1)
    def _():
        for h in range(SSD_H):
            gi = h // (SSD_H // SSD_G)
            sfin_ref[0, h] = S[gi * SSD_N:(gi + 1) * SSD_N, h * SSD_P:(h + 1) * SSD_P]
        cfin_ref[0] = tail[8 - (SSD_CONV_W - 1):8, :]


def _pad_lanes(v, n=LANE):
    v = v.reshape(1, -1)
    return jnp.zeros((1, n), F32).at[:, :v.shape[1]].set(v)


def _ssd_tables(TT, c):
    L = jnp.asarray(_block_tril(TT, c), BF)
    Mk = jnp.asarray(_block_tril(TT, c), F32)
    e = np.zeros((LANE, SSD_WIDTH), np.float32)
    for h in range(SSD_H):
        e[h, h * SSD_P:(h + 1) * SSD_P] = 1.0
    M2 = np.zeros((SSD_G * SSD_N, SSD_WIDTH), np.float32)
    for h in range(SSD_H):
        gi = h // (SSD_H // SSD_G)
        M2[gi * SSD_N:(gi + 1) * SSD_N, h * SSD_P:(h + 1) * SSD_P] = 1.0
    return L, Mk, jnp.asarray(e, BF), jnp.asarray(M2, F32)


def _ssd_params(conv_w, conv_b, dt_bias, a_log, d, norm_w):
    return (conv_w, conv_b.reshape(1, -1), _pad_lanes(dt_bias), _pad_lanes(a_log),
            jnp.repeat(d, SSD_P).reshape(1, -1), norm_w.reshape(1, -1))


def _ssd_prompt_call(sin_, B, T, conv_w, conv_b, dt_bias, a_log, d, norm_w):
    TT = 256
    nT = T // TT
    L, Mk, Eexp, M2 = _ssd_tables(TT, TT)
    prm = _ssd_params(conv_w, conv_b, dt_bias, a_log, d, norm_w)
    return pl.pallas_call(
        _ssd_prompt_kernel,
        grid=(B, nT),
        in_specs=[pl.BlockSpec((TT, SSD_IN_W), lambda b, t: (b * nT + t, 0))]
                 + [_const(p.shape) for p in prm]
                 + [_const(L.shape), _const(Mk.shape), _const(Eexp.shape), _const(M2.shape)],
        out_specs=[pl.BlockSpec((TT, SSD_WIDTH), lambda b, t: (b * nT + t, 0)),
                   pl.BlockSpec((1, SSD_H, SSD_N, SSD_P), lambda b, t: (b, 0, 0, 0)),
                   pl.BlockSpec((1, SSD_CONV_W - 1, SSD_CONV_DIM), lambda b, t: (b, 0, 0))],
        out_shape=[jax.ShapeDtypeStruct((B * T, SSD_WIDTH), BF),
                   jax.ShapeDtypeStruct((B, SSD_H, SSD_N, SSD_P), F32),
                   jax.ShapeDtypeStruct((B, SSD_CONV_W - 1, SSD_CONV_DIM), F32)],
        scratch_shapes=[pltpu.VMEM((SSD_G * SSD_N, SSD_WIDTH), F32),
                        pltpu.VMEM((TT + 8, SSD_CONV_DIM), F32)],
        compiler_params=_cp(2),
        name="ssd_prompt",
    )(sin_, *prm, L, Mk, Eexp, M2)


SEQ_TILE = 8


def _tile_lanes(n_rep, width):
    return np.tile(np.eye(width, dtype=np.float32), (1, n_rep))


def _fold_head_blocks(ubd):
    a = ubd[:, 0:LANE] + ubd[:, LANE:2 * LANE]
    return (a + pltpu.roll(a, LANE // 2, 1))[:, 0:LANE // 2]


def _col_bcast(row8, ones_ref):
    first = lax.broadcasted_iota(jnp.int32, (8, 1), 0) == 0
    hi, mid, lo = _split3(jnp.where(first, row8, 0.0))
    ones = ones_ref[...]
    return _dot_tn(hi, ones) + (_dot_tn(mid, ones) + _dot_tn(lo, ones))


def _gla_sample_kernel(x_ref, s0_ref, wg_ref, bg_ref, nw_ref, L_ref, E_ref, EA_ref, M_ref, T4_ref, ones_ref,
                       o_ref, sn_ref, kp_ref, gp_ref, vp_ref, oi_ref, *, c):
    TT = x_ref.shape[0]
    PAD = kp_ref.shape[0] - TT
    q, k, v, r, g = _gla_front(x_ref, wg_ref, bg_ref, L_ref)
    kp_ref[0:PAD, :] = jnp.zeros((PAD, LANE), F32)
    gp_ref[0:PAD, :] = jnp.zeros((PAD, LANE), F32)
    vp_ref[0:PAD, :] = jnp.zeros((PAD, 2 * LANE), F32)
    kp_ref[PAD:PAD + TT, :] = k
    gp_ref[PAD:PAD + TT, :] = g
    vp_ref[PAD:PAD + TT, :] = v
    o = _gla_intra(q, g, kp_ref, gp_ref, vp_ref, E_ref, c)
    qe = (q * jnp.exp(g)).astype(BF)
    M = M_ref[...]
    for s in range(TT // c):
        lo = s * c
        S0 = s0_ref[s].reshape(GLA_H * GLA_DK, GLA_DV)
        Sbd = (_dot(S0.astype(BF), T4_ref[...]) * M).astype(BF)
        oi_ref[lo:lo + c, :] = _dot(qe[lo:lo + c, :], Sbd)
        gl = g[lo + c - 1:lo + c, :]
        ke = (k[lo:lo + c, :] * jnp.exp(gl - g[lo:lo + c, :])).astype(BF)
        u = _fold_head_blocks(_dot_tn(ke, v[lo:lo + c, :].astype(BF)) * M)
        acol = _col_bcast(jnp.broadcast_to(jnp.exp(gl), (8, LANE)), ones_ref)
        sn_ref[s] = (acol * S0 + u).reshape(GLA_H, GLA_DK, GLA_DV)
    o = o + oi_ref[...]
    o_ref[...] = _gla_norm_gate(o, r, nw_ref, EA_ref).astype(o_ref.dtype)


def _gla_sample_call(gin, s0, B, T, w_gate, b_gate, norm_w):
    TT = SEQ_TILE * T
    L, E, EA, _ = _gla_tables(TT, T)
    M = jnp.asarray(_head_block_mask(GLA_DK, GLA_DV, GLA_H), F32)
    T4 = jnp.asarray(_tile_lanes(GLA_H, GLA_DV), BF)
    ones = jnp.ones((8, GLA_DV), BF)
    wg, bg, nw = _gla_params(w_gate, b_gate, norm_w)
    PAD = 8
    sspec = pl.BlockSpec((SEQ_TILE, GLA_H, GLA_DK, GLA_DV), lambda i: (i, 0, 0, 0))
    return pl.pallas_call(
        functools.partial(_gla_sample_kernel, c=T),
        grid=(B // SEQ_TILE,),
        in_specs=[pl.BlockSpec((TT, GLA_IN_W), lambda i: (i, 0)), sspec,
                  _const(wg.shape), _const(bg.shape), _const(nw.shape),
                  _const(L.shape), _const(E.shape), _const(EA.shape), _const(M.shape), _const(T4.shape),
                  _const(ones.shape)],
        out_specs=[pl.BlockSpec((TT, GLA_WIDTH), lambda i: (i, 0)), sspec],
        out_shape=[jax.ShapeDtypeStruct((B * T, GLA_WIDTH), BF),
                   jax.ShapeDtypeStruct((B, GLA_H, GLA_DK, GLA_DV), F32)],
        scratch_shapes=[pltpu.VMEM((TT + PAD, LANE), F32),
                        pltpu.VMEM((TT + PAD, LANE), F32),
                        pltpu.VMEM((TT + PAD, 2 * LANE), F32),
                        pltpu.VMEM((TT, 2 * LANE), F32)],
        compiler_params=_cp(1),
        name="gla_sample",
    )(gin, s0, wg, bg, nw, L, E, EA, M, T4, ones)


def _ret_sample_kernel(x_ref, s0_ref, cos_ref, sin_ref, D_ref, rd_ref, kd_ref, G_ref, M_ref, EA_ref, nw_ref, T4_ref,
                       o_ref, sn_ref, oi_ref, *, c):
    TT = x_ref.shape[0]
    q, k, v, rg = _ret_front(x_ref, cos_ref, sin_ref)
    o = _ret_intra(q, k, v, D_ref)
    qd = (q * rd_ref[...]).astype(BF)
    kd = (k * kd_ref[...]).astype(BF)
    vb = v.astype(BF)
    M = M_ref[...]
    for s in range(TT // c):
        lo = s * c
        S0 = s0_ref[s].reshape(RET_H * RET_DK, RET_DV)
        Sbd = (_dot(S0.astype(BF), T4_ref[...]) * M).astype(BF)
        oi_ref[lo:lo + c, :] = _dot(qd[lo:lo + c, :], Sbd)
        u = _fold_head_blocks(_dot_tn(kd[lo:lo + c, :], vb[lo:lo + c, :]) * M)
        sn_ref[s] = (G_ref[...] * S0 + u).reshape(RET_H, RET_DK, RET_DV)
    o = o + oi_ref[...]
    o_ref[...] = _ret_norm_gate(o, rg, nw_ref, EA_ref).astype(o_ref.dtype)


def _ret_sample_call(rin, s0, B, T, norm_w):
    TT = SEQ_TILE * T
    cos, sin = _rope_tables(PAST_LEN + jnp.arange(T, dtype=jnp.int32))
    cos, sin = jnp.tile(cos, (SEQ_TILE, 1)), jnp.tile(sin, (SEQ_TILE, 1))
    lg = _ret_log_gamma()
    i = np.arange(TT)
    same = (i[:, None] // T == i[None, :] // T) & (i[:, None] >= i[None, :])
    Dm = jnp.asarray(np.exp(lg[:, None, None] * (i[:, None] - i[None, :])[None]) * same[None], F32)
    tt = i % T
    rd = jnp.asarray(np.repeat(np.exp(lg[None, :] * (tt[:, None] + 1)), RET_DK, 1), F32)
    kd = jnp.asarray(np.repeat(np.exp(lg[None, :] * (T - 1 - tt[:, None])), RET_DK, 1), F32)
    G = jnp.asarray(np.repeat(np.repeat(np.exp(lg * T), RET_DK)[:, None], RET_DV, 1), F32)
    M = jnp.asarray(_head_block_mask(RET_DK, RET_DV, RET_H), F32)
    EA = jnp.asarray(_head_block_mask(RET_DV, RET_DV, RET_H) / RET_DV, BF)
    T4 = jnp.asarray(_tile_lanes(RET_H, RET_DV), BF)
    nw = norm_w.reshape(1, -1)
    sspec = pl.BlockSpec((SEQ_TILE, RET_H, RET_DK, RET_DV), lambda i: (i, 0, 0, 0))
    consts = (cos, sin, Dm, rd, kd, G, M, EA, nw, T4)
    return pl.pallas_call(
        functools.partial(_ret_sample_kernel, c=T),
        grid=(B // SEQ_TILE,),
        in_specs=[pl.BlockSpec((TT, RET_IN_W), lambda i: (i, 0)), sspec] + [_const(a.shape) for a in consts],
        out_specs=[pl.BlockSpec((TT, RET_WIDTH), lambda i: (i, 0)), sspec],
        out_shape=[jax.ShapeDtypeStruct((B * T, RET_WIDTH), BF),
                   jax.ShapeDtypeStruct((B, RET_H, RET_DK, RET_DV), F32)],
        scratch_shapes=[pltpu.VMEM((TT, RET_WIDTH), F32)],
        compiler_params=_cp(1),
        name="ret_sample",
    )(rin, s0, *consts)


def _ssd_sample_kernel(x_ref, c0_ref, s0_ref, cw_ref, cb_ref, dtb_ref, alog_ref, dexp_ref, nw_ref,
                       L_ref, Mk_ref, Eexp_ref, Bl_ref, R2_ref, T8_ref, T8T_ref, M8_ref, ones_ref,
                       o_ref, sn_ref, cn_ref, xp_ref, oi_ref, *, c):
    TT = x_ref.shape[0]
    ns = TT // c
    RP = 2 * c
    xp_ref[...] = jnp.zeros_like(xp_ref)
    z = x_ref[:, 0:SSD_WIDTH]
    sdt = x_ref[:, SSD_WIDTH + SSD_CONV_DIM:SSD_IN_W]
    for s in range(ns):
        base = 8 + s * RP
        xp_ref[base + c - (SSD_CONV_W - 1):base + c, :] = c0_ref[s]
        xp_ref[base + c:base + RP, :] = x_ref[s * c:(s + 1) * c, SSD_WIDTH:SSD_WIDTH + SSD_CONV_DIM]
    conv = _ssd_conv(xp_ref, cw_ref, cb_ref, ns * RP)
    xbc = _silu(conv.reshape(ns, RP, SSD_CONV_DIM)[:, c:RP, :].reshape(TT, SSD_CONV_DIM))
    for s in range(ns):
        base = 8 + s * RP
        cn_ref[s] = xp_ref[base + RP - (SSD_CONV_W - 1):base + RP, :]
    xs = xbc[:, 0:SSD_WIDTH]
    bm = xbc[:, SSD_WIDTH:SSD_WIDTH + LANE]
    cm = xbc[:, SSD_WIDTH + LANE:SSD_CONV_DIM]

    dt = _softplus(sdt + dtb_ref[...])
    la = dt * (-jnp.exp(alog_ref[...]))
    g = _dot_3x(L_ref[...], la)
    gl = _dot_3x(Bl_ref[...], g)
    Eexp = Eexp_ref[...]
    eg_x = _dot_x2(jnp.exp(g), Eexp)
    cw_x = _dot_x2(dt * jnp.exp(gl - g), Eexp)
    egl_x = _dot_x2(jnp.exp(gl), Eexp)
    o = _ssd_intra(xs, bm, cm, g, dt, Mk_ref)

    Cx = _dot(cm.astype(BF), R2_ref[...])
    Bx = _dot(bm.astype(BF), R2_ref[...])
    Xw = xs * cw_x
    M8 = M8_ref[...]
    nh = SSD_H

    def rows_by_head(a):
        return jnp.concatenate([a] * nh, axis=0) * M8

    for s in range(ns):
        lo = s * c
        S0 = s0_ref[s].reshape(SSD_H * SSD_N, SSD_P)
        oi = _dot(rows_by_head(Cx[lo:lo + c, :]).astype(BF), S0.astype(BF))
        oix = _dot_x2(oi, T8_ref[...]) * M8
        acc = oix[0:c, :]
        for h in range(1, nh):
            acc = acc + oix[h * c:(h + 1) * c, :]
        oi_ref[lo:lo + c, :] = acc
        Xst = _dot(rows_by_head(Xw[lo:lo + c, :]).astype(BF), T8T_ref[...])
        u = _dot_tn(rows_by_head(Bx[lo:lo + c, :]).astype(BF), Xst.astype(BF))
        acol = _col_bcast(egl_x[lo:lo + c, :], ones_ref)
        sn_ref[s] = (acol * S0 + u).reshape(SSD_H, SSD_N, SSD_P)

    o = o + eg_x * oi_ref[...]
    y = (o + dexp_ref[...] * xs) * _silu(z)
    ms = jnp.mean(y * y, axis=-1, keepdims=True)
    o_ref[...] = (y * lax.rsqrt(ms + EPS) * nw_ref[...]).astype(o_ref.dtype)


def _ssd_sample_call(sin_, c0, s0, B, T, conv_w, conv_b, dt_bias, a_log, d, norm_w):
    TT = SEQ_TILE * T
    L, Mk, Eexp, _ = _ssd_tables(TT, T)
    i = np.arange(TT)
    Bl = jnp.asarray((i[None, :] == (i[:, None] // T) * T + T - 1).astype(np.float32), BF)
    hpg = SSD_H // SSD_G
    R2 = np.zeros((LANE, SSD_H * SSD_N), np.float32)
    for h in range(SSD_H):
        R2[(h // hpg) * SSD_N:(h // hpg + 1) * SSD_N, h * SSD_N:(h + 1) * SSD_N] = np.eye(SSD_N)
    T8 = _tile_lanes(SSD_H, SSD_P)
    M8 = _head_block_mask(T, SSD_P, SSD_H)
    tabs = (L, Mk, Eexp, Bl, jnp.asarray(R2, BF), jnp.asarray(T8, BF), jnp.asarray(T8.T, BF), jnp.asarray(M8, F32),
            jnp.ones((8, SSD_P), BF))
    prm = _ssd_params(conv_w, conv_b, dt_bias, a_log, d, norm_w)
    sspec = pl.BlockSpec((SEQ_TILE, SSD_H, SSD_N, SSD_P), lambda i: (i, 0, 0, 0))
    cspec = pl.BlockSpec((SEQ_TILE, SSD_CONV_W - 1, SSD_CONV_DIM), lambda i: (i, 0, 0))
    return pl.pallas_call(
        functools.partial(_ssd_sample_kernel, c=T),
        grid=(B // SEQ_TILE,),
        in_specs=[pl.BlockSpec((TT, SSD_IN_W), lambda i: (i, 0)), cspec, sspec]
                 + [_const(p.shape) for p in prm] + [_const(a.shape) for a in tabs],
        out_specs=[pl.BlockSpec((TT, SSD_WIDTH), lambda i: (i, 0)), sspec, cspec],
        out_shape=[jax.ShapeDtypeStruct((B * T, SSD_WIDTH), BF),
                   jax.ShapeDtypeStruct((B, SSD_H, SSD_N, SSD_P), F32),
                   jax.ShapeDtypeStruct((B, SSD_CONV_W - 1, SSD_CONV_DIM), F32)],
        scratch_shapes=[pltpu.VMEM((8 + SEQ_TILE * 2 * T, SSD_CONV_DIM), F32),
                        pltpu.VMEM((TT, SSD_WIDTH), F32)],
        compiler_params=_cp(1),
        name="ssd_sample",
    )(sin_, c0, s0, *prm, *tabs)


def _outproj_kernel(x_ref, g_ref, og_ref, or_ref, os_ref, w_ref, lg_ref, lb_ref, o_ref):
    bB, bT, D = x_ref.shape
    mix = (_dot(og_ref[...], w_ref[0:GLA_WIDTH, :])
           + _dot(or_ref[...], w_ref[GLA_WIDTH:GLA_WIDTH + RET_WIDTH, :])
           + _dot(os_ref[...], w_ref[GLA_WIDTH + RET_WIDTH:D, :]))
    y = ALPHA * x_ref[...] + g_ref[...] * mix.reshape(bB, bT, D)
    o_ref[...] = _layer_norm(y, lg_ref[...], lb_ref[...])


def _outproj_call(x3, g1, og, orr, os_, w_out, ln_g, ln_b):
    B, T, D = x3.shape
    bB, bT = _tok_tiles(B, T)
    nT = T // bT
    R = bB * bT
    xmap = lambda i, j: (i, j, 0)
    mmap = lambda i, j: (i, 0, 0)
    rmap = lambda i, j: (i * nT + j, 0)
    return pl.pallas_call(
        _outproj_kernel,
        grid=(B // bB, nT),
        in_specs=[pl.BlockSpec((bB, bT, D), xmap),
                  pl.BlockSpec((bB, 1, D), mmap),
                  pl.BlockSpec((R, GLA_WIDTH), rmap),
                  pl.BlockSpec((R, RET_WIDTH), rmap),
                  pl.BlockSpec((R, SSD_WIDTH), rmap),
                  _const((D, D)), _const((1, 1, D)), _const((1, 1, D))],
        out_specs=pl.BlockSpec((bB, bT, D), xmap),
        out_shape=jax.ShapeDtypeStruct((B, T, D), F32),
        compiler_params=_cp(2),
        name="out_proj_ln",
    )(x3, g1, og, orr, os_, w_out, ln_g.reshape(1, 1, D), ln_b.reshape(1, 1, D))


ROUTE_OFF = 8


def _moe_route_t(lt):
    R = lt.shape[1]
    neg = jnp.float32(-jnp.inf)
    row8 = lax.broadcasted_iota(jnp.int32, (8, 1), 0)
    lg = jnp.where(row8 < MOE_GROUPS, lt[0:8, :], neg)
    mg = jnp.max(lg, axis=0, keepdims=True)
    gsel = jnp.min(jnp.where(lg == mg, row8, 8), axis=0, keepdims=True)
    g_gate = 1.0 / jnp.sum(jnp.exp(lg - mg), axis=0, keepdims=True)
    rowe = lax.broadcasted_iota(jnp.int32, (MOE_EXPERTS, 1), 0)
    le = jnp.where((rowe // MOE_PER_GROUP) == gsel, lt[ROUTE_OFF:ROUTE_OFF + MOE_EXPERTS, :], neg)
    m1 = jnp.max(le, axis=0, keepdims=True)
    i1 = jnp.min(jnp.where(le == m1, rowe, MOE_EXPERTS), axis=0, keepdims=True)
    le2 = jnp.where(rowe == i1, neg, le)
    m2 = jnp.max(le2, axis=0, keepdims=True)
    i2 = jnp.min(jnp.where(le2 == m2, rowe, MOE_EXPERTS), axis=0, keepdims=True)
    e2 = jnp.exp(m2 - m1)
    w1 = g_gate / (1.0 + e2)
    w2 = g_gate * e2 / (1.0 + e2)
    comb = jnp.where(rowe == i1, w1, jnp.where(rowe == i2, w2, 0.0))
    cg = comb[0:4, :]
    for g in range(1, MOE_GROUPS):
        cg = cg + comb[g * MOE_PER_GROUP:(g + 1) * MOE_PER_GROUP, :]
    return gsel, cg, comb


MOE_SUB = 256
MOE_BLK = 16
MOE_ROWS = 256
MOE_NPS = MOE_SUB + MOE_GROUPS * MOE_BLK
MOE_MAXB = MOE_SUB // MOE_BLK + 1


def _moe_kernel(x_ref, sc_ref, sh_ref, g_ref, wr_ref, br_ref, us_ref, w1_ref, w3_ref, w2_ref, lg_ref, lb_ref,
                o_ref, hb_ref, cwb_ref, hp_ref, cwp_ref, yp_ref, pos_ref,
                cgrp_ref, fill_ref, cur_ref, na_ref, dst_ref, nb_ref, so_ref, *, n_steps):
    bB, bT, D = x_ref.shape
    R = bB * bT
    n_q = R // MOE_SUB
    s = pl.program_id(1)
    x = x_ref[...]
    row8 = lax.broadcasted_iota(jnp.int32, (8, 1), 0)
    slot = lax.broadcasted_iota(jnp.int32, (MOE_NPS, 1), 0).astype(F32)

    @pl.when((pl.program_id(0) == 0) & (s == 0))
    def _():
        hb_ref[...] = jnp.zeros_like(hb_ref)
        cwb_ref[...] = jnp.zeros_like(cwb_ref)
        yp_ref[...] = jnp.zeros_like(yp_ref)

    @pl.when(s == 0)
    def _():
        na_ref[0] = 0
        for g in range(MOE_GROUPS):
            cur_ref[g] = -1
            fill_ref[g] = 0

    @pl.when(s < n_steps)
    def _():
        h = (x * (1.0 + sc_ref[...]) + sh_ref[...]).reshape(R, D)
        for q in range(n_q):
            u = s * n_q + q
            hq = h[q * MOE_SUB:(q + 1) * MOE_SUB, :].astype(BF)
            gsel, cg, _ = _moe_route_t(_dot_nt(wr_ref[...], hq) + br_ref[...])
            onehot = jnp.where(row8 == gsel, 1.0, 0.0)
            rank = _dot(onehot.astype(BF), us_ref[...])
            cnt = jnp.sum(onehot, axis=1, keepdims=True)
            seg = jnp.ceil(cnt * (1.0 / MOE_BLK)) * MOE_BLK
            off = jnp.zeros((8, 1), F32)
            for g in range(1, MOE_GROUPS):
                off = off + jnp.where(row8 >= g, seg[g - 1:g, :], 0.0)
            pos = jnp.sum(onehot * (off + rank), axis=0, keepdims=True)
            pos_ref[u] = jnp.broadcast_to(pos, (8, MOE_SUB))
            perm = jnp.where(slot == pos, 1.0, 0.0).astype(BF)
            hp_ref[...] = _dot(perm, hq).astype(BF)
            cg8 = jnp.concatenate([cg, jnp.zeros((4, MOE_SUB), F32)], axis=0)
            cg_hi = cg8.astype(BF)
            cg_lo = (cg8 - cg_hi.astype(F32)).astype(BF)
            cwp_ref[...] = _dot_nt(perm, cg_hi) + _dot_nt(perm, cg_lo)
            for g in range(MOE_GROUPS):
                so = off[g, 0].astype(jnp.int32)
                nb = (seg[g, 0] * (1.0 / MOE_BLK)).astype(jnp.int32)
                so_ref[u * MOE_GROUPS + g] = so
                nb_ref[u * MOE_GROUPS + g] = nb

                def put(k, carry, g=g, so=so, u=u):
                    f = fill_ref[g]
                    c = cur_ref[g]
                    na = na_ref[0]
                    fresh = (c < 0) | (f >= MOE_ROWS)
                    c = jnp.where(fresh, na, c)
                    f = jnp.where(fresh, 0, f)
                    cgrp_ref[c] = g
                    na_ref[0] = jnp.where(fresh, na + 1, na)
                    dst = pl.multiple_of(c * MOE_ROWS + f, MOE_BLK)
                    src = pl.multiple_of(so + k * MOE_BLK, MOE_BLK)
                    hb_ref[pl.ds(dst, MOE_BLK), :] = hp_ref[pl.ds(src, MOE_BLK), :]
                    cwb_ref[pl.ds(dst, MOE_BLK), :] = cwp_ref[pl.ds(src, MOE_BLK), :]
                    dst_ref[(u * MOE_GROUPS + g) * MOE_MAXB + k] = dst
                    cur_ref[g] = c
                    fill_ref[g] = f + MOE_BLK
                    return carry

                lax.fori_loop(0, nb, put, 0)

    @pl.when(s == n_steps - 1)
    def _():
        def chunk(c, carry):
            g = cgrp_ref[c]
            start = pl.multiple_of(c * MOE_ROWS, MOE_ROWS)
            hc = hb_ref[pl.ds(start, MOE_ROWS), :]
            cw = cwb_ref[pl.ds(start, MOE_ROWS), :]
            acc = jnp.zeros((MOE_ROWS, D), F32)
            for j in range(MOE_PER_GROUP):
                e = g * MOE_PER_GROUP + j
                hid = _silu(_dot(hc, w1_ref[e])) * _dot(hc, w3_ref[e]) * cw[:, j:j + 1]
                acc = acc + _dot(hid.astype(BF), w2_ref[e])
            hb_ref[pl.ds(start, MOE_ROWS), :] = acc.astype(BF)
            return carry

        lax.fori_loop(0, na_ref[0], chunk, 0)

    @pl.when(s >= n_steps)
    def _():
        ys = []
        for q in range(n_q):
            u = (s - n_steps) * n_q + q
            for g in range(MOE_GROUPS):
                so = so_ref[u * MOE_GROUPS + g]

                def take(k, carry, g=g, so=so, u=u):
                    src = pl.multiple_of(dst_ref[(u * MOE_GROUPS + g) * MOE_MAXB + k], MOE_BLK)
                    dst = pl.multiple_of(so + k * MOE_BLK, MOE_BLK)
                    yp_ref[pl.ds(dst, MOE_BLK), :] = hb_ref[pl.ds(src, MOE_BLK), :]
                    return carry

                lax.fori_loop(0, nb_ref[u * MOE_GROUPS + g], take, 0)
            perm = jnp.where(slot == pos_ref[u][0:1, :], 1.0, 0.0).astype(BF)
            ys.append(_dot_tn(perm, yp_ref[...]))
        y = jnp.concatenate(ys, axis=0)
        z = ALPHA * x + g_ref[...] * y.reshape(bB, bT, D)
        o_ref[...] = _layer_norm(z, lg_ref[...], lb_ref[...])


def _resident_layer(shape, l):
    return pl.BlockSpec((None,) + tuple(shape[1:]), lambda *_: (l,) + (0,) * (len(shape) - 1),
                        pipeline_mode=pl.Buffered(1))


def _moe_call(x3, sc, sh, g2, wr, br, w1, w3, w2, l, ln_g, ln_b):
    B, T, D = x3.shape
    bB, bT = _tok_tiles(B, T)
    R = bB * bT
    if bB == 1:
        n_pools, n_steps = B, T // bT
        xmap = lambda p, s: (p, s % n_steps, 0)
        omap = lambda p, s: (p, jnp.maximum(s - n_steps, 0), 0)
        mmap = lambda p, s: (p, 0, 0)
    else:
        n_pools, n_steps = 1, B // bB
        xmap = lambda p, s: (s % n_steps, 0, 0)
        omap = lambda p, s: (jnp.maximum(s - n_steps, 0), 0, 0)
        mmap = xmap
    n_sub = n_steps * (R // MOE_SUB)
    n_chunks = n_sub * MOE_SUB // MOE_ROWS + MOE_GROUPS
    us = jnp.asarray(np.triu(np.ones((MOE_SUB, MOE_SUB), np.float32), 1), BF)
    smem = lambda n: pltpu.SMEM((n,), jnp.int32)
    return pl.pallas_call(
        functools.partial(_moe_kernel, n_steps=n_steps),
        grid=(n_pools, 2 * n_steps),
        in_specs=[pl.BlockSpec((bB, bT, D), xmap),
                  pl.BlockSpec((bB, 1, D), mmap), pl.BlockSpec((bB, 1, D), mmap), pl.BlockSpec((bB, 1, D), mmap),
                  _const(wr.shape), _const(br.shape), _const(us.shape),
                  _resident_layer(w1.shape, l), _resident_layer(w3.shape, l), _resident_layer(w2.shape, l),
                  _const((1, 1, D)), _const((1, 1, D))],
        out_specs=pl.BlockSpec((bB, bT, D), omap),
        out_shape=jax.ShapeDtypeStruct((B, T, D), F32),
        scratch_shapes=[pltpu.VMEM((n_chunks * MOE_ROWS, D), BF), pltpu.VMEM((n_chunks * MOE_ROWS, 8), F32),
                        pltpu.VMEM((MOE_NPS, D), BF), pltpu.VMEM((MOE_NPS, 8), F32), pltpu.VMEM((MOE_NPS, D), BF),
                        pltpu.VMEM((n_sub, 8, MOE_SUB), F32),
                        smem(n_chunks), smem(MOE_GROUPS), smem(MOE_GROUPS), smem(1),
                        smem(n_sub * MOE_GROUPS * MOE_MAXB), smem(n_sub * MOE_GROUPS), smem(n_sub * MOE_GROUPS)],
        compiler_params=_cp(2),
        name="moe_ln",
    )(x3, sc, sh, g2, wr, br, us, w1, w3, w2, ln_g.reshape(1, 1, D), ln_b.reshape(1, 1, D))


def _router_params(w_group, b_group, w_expert, b_expert):
    wr = jnp.zeros((LANE, D_MODEL), F32).at[:MOE_GROUPS].set(w_group.T)
    wr = wr.at[ROUTE_OFF:ROUTE_OFF + MOE_EXPERTS].set(w_expert.T)
    br = jnp.zeros((LANE, 1), F32).at[:MOE_GROUPS, 0].set(b_group).at[ROUTE_OFF:ROUTE_OFF + MOE_EXPERTS, 0].set(b_expert)
    return wr.astype(BF), br


def kernel(x_prompt, x_sample, c_prompt, c_sample, state_gla, state_ret, state_ssd, state_conv, w_ada, b_ada, w_in, gla_w_gate, gla_b_gate, gla_norm, ret_norm, ssd_conv_w, ssd_conv_b, ssd_dt_bias, ssd_a_log, ssd_d, ssd_norm, w_out, ln1_g, ln1_b, moe_w_group, moe_b_group, moe_w_expert, moe_b_expert, moe_w1, moe_w3, moe_w2, ln2_g, ln2_b):
    Bp, Tp, D = x_prompt.shape
    Bs, Ts, _ = x_sample.shape
    w_in_t = jnp.swapaxes(w_in, 1, 2)
    w_out_b = w_out.astype(BF)
    w1_b, w3_b, w2_b = moe_w1.astype(BF), moe_w3.astype(BF), moe_w2.astype(BF)

    mod = _mod_call(jnp.concatenate([c_prompt, c_sample], axis=0), w_ada, b_ada)

    def trunk(x, mod_rows, B, T, states):
        new = [[], [], [], []]
        for l in range(DEPTH):
            sh1, sc1, g1, sh2, sc2, g2 = (mod_rows[l][:, None, i * D:(i + 1) * D] for i in range(6))
            gin, rin, sin_ = _inproj_call(x, sc1, sh1, w_in_t, l)
            if states is None:
                og, s_gla = _gla_prompt_call(gin, B, T, gla_w_gate[l], gla_b_gate[l], gla_norm[l])
                orr, s_ret = _ret_prompt_call(rin, B, T, ret_norm[l])
                os_, s_ssd, s_conv = _ssd_prompt_call(sin_, B, T, ssd_conv_w[l], ssd_conv_b[l], ssd_dt_bias[l],
                                                      ssd_a_log[l], ssd_d[l], ssd_norm[l])
            else:
                og, s_gla = _gla_sample_call(gin, states[0][l], B, T, gla_w_gate[l], gla_b_gate[l], gla_norm[l])
                orr, s_ret = _ret_sample_call(rin, states[1][l], B, T, ret_norm[l])
                os_, s_ssd, s_conv = _ssd_sample_call(sin_, states[3][l], states[2][l], B, T, ssd_conv_w[l],
                                                      ssd_conv_b[l], ssd_dt_bias[l], ssd_a_log[l], ssd_d[l],
                                                      ssd_norm[l])
            x = _outproj_call(x, g1, og, orr, os_, w_out_b[l], ln1_g[l], ln1_b[l])
            wr, br = _router_params(moe_w_group[l], moe_b_group[l], moe_w_expert[l], moe_b_expert[l])
            x = _moe_call(x, sc2, sh2, g2, wr, br, w1_b, w3_b, w2_b, l, ln2_g[l], ln2_b[l])
            for acc, s in zip(new, (s_gla, s_ret, s_ssd, s_conv)):
                acc.append(s)
        return (x,) + tuple(jnp.stack(a) for a in new)

    y_p, gla_p, ret_p, ssd_p, conv_p = trunk(x_prompt, mod[:, :Bp], Bp, Tp, None)
    y_s, gla_s, ret_s, ssd_s, conv_s = trunk(x_sample, mod[:, Bp:], Bs, Ts,
                                             (state_gla, state_ret, state_ssd, state_conv))
    return (y_p, y_s, gla_p, ret_p, ssd_p, conv_p, gla_s, ret_s, ssd_s, conv_s)
```

```python
import functools
import math

import numpy as np
import jax
import jax.numpy as jnp
from jax import lax
from jax.experimental import pallas as pl
from jax.experimental.pallas import tpu as pltpu

F32 = jnp.float32
BF = jnp.bfloat16

D_MODEL = 1024
DEPTH = 2
PAST_LEN = 16384
GLA_H, GLA_DK, GLA_DV = 4, 32, 64
GLA_WIDTH = GLA_H * GLA_DV
GLA_GATE_RANK = 16
GLA_GATE_TEMP = 16.0
GLA_CHUNK = 16
RET_H, RET_DK, RET_DV = 4, 64, 64
RET_WIDTH = RET_H * RET_DV
ROPE_BASE = 10000.0
SSD_H, SSD_P, SSD_G, SSD_N = 8, 64, 2, 64
SSD_WIDTH = SSD_H * SSD_P
SSD_CONV_W = 4
SSD_CONV_DIM = SSD_WIDTH + 2 * SSD_G * SSD_N
MOE_GROUPS, MOE_PER_GROUP = 4, 4
MOE_EXPERTS = MOE_GROUPS * MOE_PER_GROUP
MOE_FF = 256
ALPHA = (2 * DEPTH) ** 0.25
EPS = 1e-5

LANE = 128
GLA_IN_W = 128 + 128 + 256 + LANE + 256
RET_IN_W = 4 * 256
SSD_IN_W = 512 + SSD_CONV_DIM + LANE
IN_W = GLA_IN_W + RET_IN_W + SSD_IN_W
VMEM_LIMIT = 56 * 1024 * 1024


def _cp(n_axes, vmem=VMEM_LIMIT):
    return pltpu.CompilerParams(dimension_semantics=("arbitrary",) * n_axes, vmem_limit_bytes=vmem)


def _dot(a, b):
    return jnp.dot(a, b, preferred_element_type=F32)


def _dot_nt(a, b):
    return lax.dot_general(a, b, (((1,), (1,)), ((), ())), preferred_element_type=F32)


def _dot_tn(a, b):
    return lax.dot_general(a, b, (((0,), (0,)), ((), ())), preferred_element_type=F32)


def _split3(x):
    hi = x.astype(BF)
    r = x - hi.astype(F32)
    mid = r.astype(BF)
    lo = (r - mid.astype(F32)).astype(BF)
    return hi, mid, lo


def _dot_x3(x, e):
    hi, mid, lo = _split3(x)
    return _dot(hi, e) + (_dot(mid, e) + _dot(lo, e))


def _dot_x2(x, e):
    hi = x.astype(BF)
    lo = (x - hi.astype(F32)).astype(BF)
    return _dot(hi, e) + _dot(lo, e)


def _dot_3x(e, x):
    hi, mid, lo = _split3(x)
    return _dot(e, hi) + (_dot(e, mid) + _dot(e, lo))


def _sigmoid(x):
    return 1.0 / (1.0 + jnp.exp(-x))


def _silu(x):
    return x * _sigmoid(x)


def _log_sigmoid(x):
    return jnp.minimum(x, 0.0) - jnp.log(1.0 + jnp.exp(-jnp.abs(x)))


def _softplus(x):
    return jnp.maximum(x, 0.0) + jnp.log(1.0 + jnp.exp(-jnp.abs(x)))


def _layer_norm(x, g, b):
    mu = jnp.mean(x, axis=-1, keepdims=True)
    d = x - mu
    var = jnp.mean(d * d, axis=-1, keepdims=True)
    return d * lax.rsqrt(var + EPS) * g + b


def _mod_kernel(c_ref, w_ref, b_ref, o_ref):
    s = _silu(c_ref[...]).astype(BF)
    o_ref[0] = _dot(s, w_ref[0].astype(BF)) + b_ref[0]


def _mod_call(c_all, w_ada, b_ada):
    R = c_all.shape[0]
    tn = 1536
    return pl.pallas_call(
        _mod_kernel,
        grid=(DEPTH, 6 * D_MODEL // tn),
        in_specs=[pl.BlockSpec((R, D_MODEL), lambda l, j: (0, 0)),
                  pl.BlockSpec((1, D_MODEL, tn), lambda l, j: (l, 0, j)),
                  pl.BlockSpec((1, 1, tn), lambda l, j: (l, 0, j))],
        out_specs=pl.BlockSpec((1, R, tn), lambda l, j: (l, 0, j)),
        out_shape=jax.ShapeDtypeStruct((DEPTH, R, 6 * D_MODEL), F32),
        compiler_params=_cp(2),
        name="ada_mod",
    )(c_all, w_ada, b_ada.reshape(DEPTH, 1, 6 * D_MODEL))


N_IN = 3096
N_GA = 128 + 128 + 256
N_DT = N_IN - SSD_H


def _inproj_kernel(x_ref, sc_ref, sh_ref, wt_ref, og_ref, or_ref, os_ref, w_ref):
    bB, bT, D = x_ref.shape

    @pl.when((pl.program_id(0) == 0) & (pl.program_id(1) == 0))
    def _():
        lane = lax.broadcasted_iota(jnp.int32, (1, LANE), 1)
        for j in range(N_GA // LANE):
            w_ref[:, j * LANE:(j + 1) * LANE] = wt_ref[j * LANE:(j + 1) * LANE, :].T.astype(BF)
        ga = wt_ref[N_GA:N_GA + LANE, :].T
        w_ref[:, N_GA:N_GA + LANE] = jnp.where(lane < GLA_GATE_RANK, ga, 0.0).astype(BF)
        src0, dst0 = N_GA + GLA_GATE_RANK, N_GA + LANE
        for j in range((N_DT - src0) // LANE):
            w_ref[:, dst0 + j * LANE:dst0 + (j + 1) * LANE] = \
                wt_ref[src0 + j * LANE:src0 + (j + 1) * LANE, :].T.astype(BF)
        dt = pltpu.roll(wt_ref[N_IN - LANE:N_IN, :].T, SSD_H, 1)
        w_ref[:, IN_W - LANE:IN_W] = jnp.where(lane < SSD_H, dt, 0.0).astype(BF)

    h = x_ref[...] * (1.0 + sc_ref[...]) + sh_ref[...]
    hb = h.reshape(bB * bT, D).astype(BF)
    og_ref[...] = _dot(hb, w_ref[:, 0:GLA_IN_W])
    or_ref[...] = _dot(hb, w_ref[:, GLA_IN_W:GLA_IN_W + RET_IN_W])
    os_ref[...] = _dot(hb, w_ref[:, GLA_IN_W + RET_IN_W:IN_W])


def _tok_tiles(B, T):
    if T >= 512:
        return 1, 512
    return 512 // T, T


def _inproj_call(x3, sc, sh, wt, l):
    B, T, D = x3.shape
    bB, bT = _tok_tiles(B, T)
    nT = T // bT
    R = bB * bT
    N = B * T
    xmap = lambda i, j: (i, j, 0)
    mmap = lambda i, j: (i, 0, 0)
    omap = lambda i, j: (i * nT + j, 0)
    return pl.pallas_call(
        _inproj_kernel,
        grid=(B // bB, nT),
        in_specs=[pl.BlockSpec((bB, bT, D), xmap),
                  pl.BlockSpec((bB, 1, D), mmap),
                  pl.BlockSpec((bB, 1, D), mmap),
                  _resident_layer(wt.shape, l)],
        out_specs=[pl.BlockSpec((R, GLA_IN_W), omap),
                   pl.BlockSpec((R, RET_IN_W), omap),
                   pl.BlockSpec((R, SSD_IN_W), omap)],
        out_shape=[jax.ShapeDtypeStruct((N, GLA_IN_W), F32),
                   jax.ShapeDtypeStruct((N, RET_IN_W), F32),
                   jax.ShapeDtypeStruct((N, SSD_IN_W), F32)],
        scratch_shapes=[pltpu.VMEM((D, IN_W), BF)],
        compiler_params=_cp(2),
        name="in_proj",
    )(x3, sc, sh, wt)


def _head_block_mask(rows_per, cols_per, n):
    r = np.arange(rows_per * n)[:, None] // rows_per
    c = np.arange(cols_per * n)[None, :] // cols_per
    return (r == c).astype(np.float32)


def _block_tril(n, c):
    i = np.arange(n)[:, None]
    j = np.arange(n)[None, :]
    return ((i // c == j // c) & (j <= i)).astype(np.float32)


def _gla_front(x_ref, wg_ref, bg_ref, L_ref):
    q = x_ref[:, 0:128] * (GLA_DK ** -0.5)
    k = x_ref[:, 128:256]
    v = x_ref[:, 256:512]
    ga = x_ref[:, 512:640]
    r = x_ref[:, 640:896]
    gate = _dot(ga.astype(BF), wg_ref[...]) + bg_ref[...]
    la = _log_sigmoid(gate) * (1.0 / GLA_GATE_TEMP)
    g = _dot_3x(L_ref[...], la)
    return q, k, v, r, g


def _gla_intra(q, g, kp_ref, gp_ref, vp_ref, E_ref, c):
    TT = q.shape[0]
    PAD = kp_ref.shape[0] - TT
    pos = lax.broadcasted_iota(jnp.int32, (TT, 1), 0) & (c - 1)
    o = jnp.zeros((TT, 2 * LANE), F32)
    for s in range(c):
        ks = kp_ref[pl.ds(PAD - s, TT), :]
        gs = gp_ref[pl.ds(PAD - s, TT), :]
        vs = vp_ref[pl.ds(PAD - s, TT), :]
        w = jnp.where(pos >= s, q * ks * jnp.exp(jnp.minimum(g - gs, 0.0)), 0.0)
        o = o + _dot(w.astype(BF), E_ref[...]) * vs
    return o


def _gla_norm_gate(o, r, nw_ref, EA_ref):
    ms = _dot_x3(o * o, EA_ref[...])
    return o * lax.rsqrt(ms + EPS) * nw_ref[...] * _silu(r)


def _gla_prompt_kernel(x_ref, wg_ref, bg_ref, nw_ref, L_ref, E_ref, EA_ref, M_ref,
                       o_ref, sfin_ref, st_ref, kp_ref, gp_ref, vp_ref, oi_ref, u_ref, sb_ref, *, c):
    TT = x_ref.shape[0]
    nc = TT // c
    PAD = kp_ref.shape[0] - TT
    t = pl.program_id(1)

    @pl.when(t == 0)
    def _():
        st_ref[...] = jnp.zeros_like(st_ref)

    q, k, v, r, g = _gla_front(x_ref, wg_ref, bg_ref, L_ref)
    kp_ref[0:PAD, :] = jnp.zeros((PAD, LANE), F32)
    gp_ref[0:PAD, :] = jnp.zeros((PAD, LANE), F32)
    vp_ref[0:PAD, :] = jnp.zeros((PAD, 2 * LANE), F32)
    kp_ref[PAD:PAD + TT, :] = k
    gp_ref[PAD:PAD + TT, :] = g
    vp_ref[PAD:PAD + TT, :] = v
    o = _gla_intra(q, g, kp_ref, gp_ref, vp_ref, E_ref, c)

    M = M_ref[...]
    gl_all = gp_ref[pl.ds(PAD + c - 1, nc, stride=c), :]
    for n in range(nc):
        lo = n * c
        ke = (k[lo:lo + c, :] * jnp.exp(gl_all[n:n + 1, :] - g[lo:lo + c, :])).astype(BF)
        u_ref[n] = _dot_tn(ke, v[lo:lo + c, :].astype(BF)) * M
    a_cols = jnp.concatenate([jnp.exp(gl_all), jnp.zeros((LANE - nc, LANE), F32)], axis=0).T
    S = st_ref[...]
    for n in range(nc):
        sb_ref[n] = S.astype(BF)
        S = a_cols[:, n:n + 1] * S + u_ref[n]
    st_ref[...] = S
    qe = (q * jnp.exp(g)).astype(BF)
    for n in range(nc):
        lo = n * c
        oi_ref[lo:lo + c, :] = _dot(qe[lo:lo + c, :], sb_ref[n])
    o = o + oi_ref[...]
    o_ref[...] = _gla_norm_gate(o, r, nw_ref, EA_ref).astype(o_ref.dtype)

    @pl.when(t == pl.num_programs(1) - 1)
    def _():
        for h in range(GLA_H):
            sfin_ref[0, h] = S[h * GLA_DK:(h + 1) * GLA_DK, h * GLA_DV:(h + 1) * GLA_DV]


def _gla_tables(TT, c):
    L = jnp.asarray(_block_tril(TT, c), BF)
    E = jnp.asarray(_head_block_mask(GLA_DK, GLA_DV, GLA_H), BF)
    EA = jnp.asarray(_head_block_mask(GLA_DV, GLA_DV, GLA_H) / GLA_DV, BF)
    M = jnp.asarray(_head_block_mask(GLA_DK, GLA_DV, GLA_H), F32)
    return L, E, EA, M


def _gla_params(w_gate, b_gate, norm_w):
    wg = jnp.zeros((LANE, GLA_H * GLA_DK), F32).at[:GLA_GATE_RANK].set(w_gate).astype(BF)
    return wg, b_gate.reshape(1, -1), norm_w.reshape(1, -1)


def _const(shape):
    return pl.BlockSpec(shape, lambda *_: (0,) * len(shape))


def _gla_prompt_call(gin, B, T, w_gate, b_gate, norm_w):
    TT, c = 256, GLA_CHUNK
    nT = T // TT
    L, E, EA, M = _gla_tables(TT, c)
    wg, bg, nw = _gla_params(w_gate, b_gate, norm_w)
    PAD = 16
    return pl.pallas_call(
        functools.partial(_gla_prompt_kernel, c=c),
        grid=(B, nT),
        in_specs=[pl.BlockSpec((TT, GLA_IN_W), lambda b, t: (b * nT + t, 0)),
                  _const(wg.shape), _const(bg.shape), _const(nw.shape),
                  _const(L.shape), _const(E.shape), _const(EA.shape), _const(M.shape)],
        out_specs=[pl.BlockSpec((TT, GLA_WIDTH), lambda b, t: (b * nT + t, 0)),
                   pl.BlockSpec((1, GLA_H, GLA_DK, GLA_DV), lambda b, t: (b, 0, 0, 0))],
        out_shape=[jax.ShapeDtypeStruct((B * T, GLA_WIDTH), BF),
                   jax.ShapeDtypeStruct((B, GLA_H, GLA_DK, GLA_DV), F32)],
        scratch_shapes=[pltpu.VMEM((GLA_H * GLA_DK, GLA_H * GLA_DV), F32),
                        pltpu.VMEM((TT + PAD, LANE), F32),
                        pltpu.VMEM((TT + PAD, LANE), F32),
                        pltpu.VMEM((TT + PAD, 2 * LANE), F32),
                        pltpu.VMEM((TT, 2 * LANE), F32),
                        pltpu.VMEM((TT // c, GLA_H * GLA_DK, GLA_H * GLA_DV), F32),
                        pltpu.VMEM((TT // c, GLA_H * GLA_DK, GLA_H * GLA_DV), BF)],
        compiler_params=_cp(2),
        name="gla_prompt",
    )(gin, wg, bg, nw, L, E, EA, M)


def _rope(x, cos, sin_signed):
    lane = lax.broadcasted_iota(jnp.int32, (1, LANE), 1)
    first_half = (lane & (RET_DK - 1)) < RET_DK // 2
    out = []
    for p in range(2):
        xs = x[:, p * LANE:(p + 1) * LANE]
        up = pltpu.roll(xs, LANE - RET_DK // 2, 1)
        dn = pltpu.roll(xs, RET_DK // 2, 1)
        out.append(xs * cos + jnp.where(first_half, up, dn) * sin_signed)
    return jnp.concatenate(out, axis=1)


def _ret_front(x_ref, cos_ref, sin_ref):
    q = _rope(x_ref[:, 0:256], cos_ref[...], sin_ref[...])
    k = _rope(x_ref[:, 256:512], cos_ref[...], sin_ref[...]) * (RET_DK ** -0.5)
    v = x_ref[:, 512:768]
    rg = x_ref[:, 768:1024]
    return q, k, v, rg


def _ret_intra(q, k, v, D_ref):
    lane = lax.broadcasted_iota(jnp.int32, (1, RET_WIDTH), 1)
    kb = k.astype(BF)
    o = jnp.zeros(q.shape, F32)
    for h in range(RET_H):
        hm = (lane // RET_DK) == h
        s = _dot_nt(jnp.where(hm, q, 0.0).astype(BF), kb)
        p = (s * D_ref[h]).astype(BF)
        o = o + _dot(p, jnp.where(hm, v, 0.0).astype(BF))
    return o


def _ret_norm_gate(o, rg, nw_ref, EA_ref):
    mu = _dot_x3(o, EA_ref[...])
    d = o - mu
    var = _dot_x3(d * d, EA_ref[...])
    return d * lax.rsqrt(var + EPS) * nw_ref[...] * _silu(rg)


def _ret_prompt_kernel(x_ref, cos_ref, sin_ref, D_ref, rd_ref, kd_ref, G_ref, M_ref, EA_ref, nw_ref,
                       o_ref, sfin_ref, st_ref):
    t = pl.program_id(1)

    @pl.when(t == 0)
    def _():
        st_ref[...] = jnp.zeros_like(st_ref)

    q, k, v, rg = _ret_front(x_ref, cos_ref, sin_ref)
    o = _ret_intra(q, k, v, D_ref)
    S = st_ref[...]
    o = o + _dot((q * rd_ref[...]).astype(BF), S.astype(BF))
    u = _dot_tn((k * kd_ref[...]).astype(BF), v.astype(BF))
    S = S * G_ref[...] + u * M_ref[...]
    st_ref[...] = S
    o_ref[...] = _ret_norm_gate(o, rg, nw_ref, EA_ref).astype(o_ref.dtype)

    @pl.when(t == pl.num_programs(1) - 1)
    def _():
        for h in range(RET_H):
            sfin_ref[0, h] = S[h * RET_DK:(h + 1) * RET_DK, h * RET_DV:(h + 1) * RET_DV]


def _rope_tables(pos):
    half = RET_DK // 2
    inv = ROPE_BASE ** (-jnp.arange(half, dtype=F32) / half)
    ang = pos.astype(F32)[:, None] * inv[None, :]
    cos, sin = jnp.cos(ang), jnp.sin(ang)
    return jnp.tile(jnp.concatenate([cos, cos], 1), (1, 2)), jnp.tile(jnp.concatenate([-sin, sin], 1), (1, 2))


def _ret_log_gamma():
    return np.log(1.0 - 2.0 ** (-5.0 - np.arange(RET_H, dtype=np.float64)))


def _ret_prompt_call(rin, B, T, norm_w):
    TT = 256
    nT = T // TT
    cos, sin = _rope_tables(jnp.arange(T, dtype=jnp.int32))
    lg = _ret_log_gamma()
    i = np.arange(TT)
    dec = np.exp(lg[:, None, None] * (i[:, None] - i[None, :])[None]) * (i[:, None] >= i[None, :])[None]
    Dm = jnp.asarray(dec, F32)
    rd = jnp.asarray(np.repeat(np.exp(lg[None, :] * (i[:, None] + 1)), RET_DK, 1), F32)
    kd = jnp.asarray(np.repeat(np.exp(lg[None, :] * (TT - 1 - i[:, None])), RET_DK, 1), F32)
    M = _head_block_mask(RET_DK, RET_DV, RET_H)
    G = jnp.asarray(M * np.repeat(np.exp(lg * TT), RET_DK)[:, None], F32)
    M = jnp.asarray(M, F32)
    EA = jnp.asarray(_head_block_mask(RET_DV, RET_DV, RET_H) / RET_DV, BF)
    nw = norm_w.reshape(1, -1)
    return pl.pallas_call(
        _ret_prompt_kernel,
        grid=(B, nT),
        in_specs=[pl.BlockSpec((TT, RET_IN_W), lambda b, t: (b * nT + t, 0)),
                  pl.BlockSpec((TT, LANE), lambda b, t: (t, 0)),
                  pl.BlockSpec((TT, LANE), lambda b, t: (t, 0)),
                  _const(Dm.shape), _const(rd.shape), _const(kd.shape), _const(G.shape), _const(M.shape),
                  _const(EA.shape), _const(nw.shape)],
        out_specs=[pl.BlockSpec((TT, RET_WIDTH), lambda b, t: (b * nT + t, 0)),
                   pl.BlockSpec((1, RET_H, RET_DK, RET_DV), lambda b, t: (b, 0, 0, 0))],
        out_shape=[jax.ShapeDtypeStruct((B * T, RET_WIDTH), BF),
                   jax.ShapeDtypeStruct((B, RET_H, RET_DK, RET_DV), F32)],
        scratch_shapes=[pltpu.VMEM((RET_H * RET_DK, RET_H * RET_DV), F32)],
        compiler_params=_cp(2),
        name="ret_prompt",
    )(rin, cos, sin, Dm, rd, kd, G, M, EA, nw)


def _ssd_conv(xp_ref, cw_ref, cb_ref, TT):
    acc = cb_ref[...] + cw_ref[SSD_CONV_W - 1:SSD_CONV_W, :] * xp_ref[pl.ds(8, TT), :]
    for i in range(SSD_CONV_W - 1):
        acc = acc + cw_ref[i:i + 1, :] * xp_ref[pl.ds(8 - (SSD_CONV_W - 1) + i, TT), :]
    return acc


def _ssd_intra(xs, bm, cm, g, dt, Mk_ref):
    TT = xs.shape[0]
    gT = g.T
    dtT = dt.T
    lane = lax.broadcasted_iota(jnp.int32, (1, LANE), 1)
    lane2 = lax.broadcasted_iota(jnp.int32, (1, 2 * LANE), 1)
    causal = Mk_ref[...] > 0.0
    bmb = bm.astype(BF)
    o_parts = []
    for grp in range(SSD_G):
        cb = _dot_nt(jnp.where((lane // SSD_N) == grp, cm, 0.0).astype(BF), bmb)
        xg = xs[:, grp * 2 * LANE:(grp + 1) * 2 * LANE]
        og = jnp.zeros((TT, 2 * LANE), F32)
        for h4 in range(SSD_H // SSD_G):
            h = grp * (SSD_H // SSD_G) + h4
            dec = jnp.where(causal, jnp.exp(jnp.minimum(g[:, h:h + 1] - gT[h:h + 1, :], 0.0)), 0.0)
            p = (cb * dec * dtT[h:h + 1, :]).astype(BF)
            og = og + _dot(p, jnp.where((lane2 // SSD_P) == h4, xg, 0.0).astype(BF))
        o_parts.append(og)
    return jnp.concatenate(o_parts, axis=1)


def _ssd_prompt_kernel(x_ref, cw_ref, cb_ref, dtb_ref, alog_ref, dexp_ref, nw_ref, L_ref, Mk_ref, Eexp_ref, M2_ref,
                       o_ref, sfin_ref, cfin_ref, st_ref, xp_ref):
    TT = x_ref.shape[0]
    t = pl.program_id(1)

    @pl.when(t == 0)
    def _():
        st_ref[...] = jnp.zeros_like(st_ref)
        xp_ref[0:8, :] = jnp.zeros((8, SSD_CONV_DIM), F32)

    z = x_ref[:, 0:SSD_WIDTH]
    xp_ref[8:8 + TT, :] = x_ref[:, SSD_WIDTH:SSD_WIDTH + SSD_CONV_DIM]
    sdt = x_ref[:, SSD_WIDTH + SSD_CONV_DIM:SSD_IN_W]
    xbc = _silu(_ssd_conv(xp_ref, cw_ref, cb_ref, TT))
    tail = xp_ref[TT:TT + 8, :]
    xp_ref[0:8, :] = tail
    xs = xbc[:, 0:SSD_WIDTH]
    bm = xbc[:, SSD_WIDTH:SSD_WIDTH + LANE]
    cm = xbc[:, SSD_WIDTH + LANE:SSD_CONV_DIM]

    dt = _softplus(sdt + dtb_ref[...])
    la = dt * (-jnp.exp(alog_ref[...]))
    g = _dot_3x(L_ref[...], la)
    gl = g[TT - 1:TT, :]
    Eexp = Eexp_ref[...]
    eg_x = _dot_x2(jnp.exp(g), Eexp)
    cw_x = _dot_x2(dt * jnp.exp(gl - g), Eexp)
    egl_x = _dot_x2(jnp.exp(gl), Eexp)

    o = _ssd_intra(xs, bm, cm, g, dt, Mk_ref)

    S = st_ref[...]
    o = o + eg_x * _dot(cm.astype(BF), S.astype(BF))
    u = _dot_tn(bm.astype(BF), (xs * cw_x).astype(BF))
    S = S * egl_x + u * M2_ref[...]
    st_ref[...] = S

    y = (o + dexp_ref[...] * xs) * _silu(z)
    ms = jnp.mean(y * y, axis=-1, keepdims=True)
    o_ref[...] = (y * lax.rsqrt(ms + EPS) * nw_ref[...]).astype(o_ref.dtype)

    @pl.when(t == pl.num_programs(1) - 1)
    def _():
        for h in range(SSD_H):
            gi = h // (SSD_H // SSD_G)
            sfin_ref[0, h] = S[gi * SSD_N:(gi + 1) * SSD_N, h * SSD_P:(h + 1) * SSD_P]
        cfin_ref[0] = tail[8 - (SSD_CONV_W - 1):8, :]


def _pad_lanes(v, n=LANE):
    v = v.reshape(1, -1)
    return jnp.zeros((1, n), F32).at[:, :v.shape[1]].set(v)


def _ssd_tables(TT, c):
    L = jnp.asarray(_block_tril(TT, c), BF)
    Mk = jnp.asarray(_block_tril(TT, c), F32)
    e = np.zeros((LANE, SSD_WIDTH), np.float32)
    for h in range(SSD_H):
        e[h, h * SSD_P:(h + 1) * SSD_P] = 1.0
    M2 = np.zeros((SSD_G * SSD_N, SSD_WIDTH), np.float32)
    for h in range(SSD_H):
        gi = h // (SSD_H // SSD_G)
        M2[gi * SSD_N:(gi + 1) * SSD_N, h * SSD_P:(h + 1) * SSD_P] = 1.0
    return L, Mk, jnp.asarray(e, BF), jnp.asarray(M2, F32)


def _ssd_params(conv_w, conv_b, dt_bias, a_log, d, norm_w):
    return (conv_w, conv_b.reshape(1, -1), _pad_lanes(dt_bias), _pad_lanes(a_log),
            jnp.repeat(d, SSD_P).reshape(1, -1), norm_w.reshape(1, -1))


def _ssd_prompt_call(sin_, B, T, conv_w, conv_b, dt_bias, a_log, d, norm_w):
    TT = 256
    nT = T // TT
    L, Mk, Eexp, M2 = _ssd_tables(TT, TT)
    prm = _ssd_params(conv_w, conv_b, dt_bias, a_log, d, norm_w)
    return pl.pallas_call(
        _ssd_prompt_kernel,
        grid=(B, nT),
        in_specs=[pl.BlockSpec((TT, SSD_IN_W), lambda b, t: (b * nT + t, 0))]
                 + [_const(p.shape) for p in prm]
                 + [_const(L.shape), _const(Mk.shape), _const(Eexp.shape), _const(M2.shape)],
        out_specs=[pl.BlockSpec((TT, SSD_WIDTH), lambda b, t: (b * nT + t, 0)),
                   pl.BlockSpec((1, SSD_H, SSD_N, SSD_P), lambda b, t: (b, 0, 0, 0)),
                   pl.BlockSpec((1, SSD_CONV_W - 1, SSD_CONV_DIM), lambda b, t: (b, 0, 0))],
        out_shape=[jax.ShapeDtypeStruct((B * T, SSD_WIDTH), BF),
                   jax.ShapeDtypeStruct((B, SSD_H, SSD_N, SSD_P), F32),
                   jax.ShapeDtypeStruct((B, SSD_CONV_W - 1, SSD_CONV_DIM), F32)],
        scratch_shapes=[pltpu.VMEM((SSD_G * SSD_N, SSD_WIDTH), F32),
                        pltpu.VMEM((TT + 8, SSD_CONV_DIM), F32)],
        compiler_params=_cp(2),
        name="ssd_prompt",
    )(sin_, *prm, L, Mk, Eexp, M2)


SEQ_TILE = 8


def _tile_lanes(n_rep, width):
    return np.tile(np.eye(width, dtype=np.float32), (1, n_rep))


def _fold_head_blocks(ubd):
    a = ubd[:, 0:LANE] + ubd[:, LANE:2 * LANE]
    return (a + pltpu.roll(a, LANE // 2, 1))[:, 0:LANE // 2]


def _col_bcast(row8, ones_ref):
    first = lax.broadcasted_iota(jnp.int32, (8, 1), 0) == 0
    hi, mid, lo = _split3(jnp.where(first, row8, 0.0))
    ones = ones_ref[...]
    return _dot_tn(hi, ones) + (_dot_tn(mid, ones) + _dot_tn(lo, ones))


def _gla_sample_kernel(x_ref, s0_ref, wg_ref, bg_ref, nw_ref, L_ref, E_ref, EA_ref, M_ref, T4_ref, ones_ref,
                       o_ref, sn_ref, kp_ref, gp_ref, vp_ref, oi_ref, *, c):
    TT = x_ref.shape[0]
    PAD = kp_ref.shape[0] - TT
    q, k, v, r, g = _gla_front(x_ref, wg_ref, bg_ref, L_ref)
    kp_ref[0:PAD, :] = jnp.zeros((PAD, LANE), F32)
    gp_ref[0:PAD, :] = jnp.zeros((PAD, LANE), F32)
    vp_ref[0:PAD, :] = jnp.zeros((PAD, 2 * LANE), F32)
    kp_ref[PAD:PAD + TT, :] = k
    gp_ref[PAD:PAD + TT, :] = g
    vp_ref[PAD:PAD + TT, :] = v
    o = _gla_intra(q, g, kp_ref, gp_ref, vp_ref, E_ref, c)
    qe = (q * jnp.exp(g)).astype(BF)
    M = M_ref[...]
    for s in range(TT // c):
        lo = s * c
        S0 = s0_ref[s].reshape(GLA_H * GLA_DK, GLA_DV)
        Sbd = (_dot(S0.astype(BF), T4_ref[...]) * M).astype(BF)
        oi_ref[lo:lo + c, :] = _dot(qe[lo:lo + c, :], Sbd)
        gl = g[lo + c - 1:lo + c, :]
        ke = (k[lo:lo + c, :] * jnp.exp(gl - g[lo:lo + c, :])).astype(BF)
        u = _fold_head_blocks(_dot_tn(ke, v[lo:lo + c, :].astype(BF)) * M)
        acol = _col_bcast(jnp.broadcast_to(jnp.exp(gl), (8, LANE)), ones_ref)
        sn_ref[s] = (acol * S0 + u).reshape(GLA_H, GLA_DK, GLA_DV)
    o = o + oi_ref[...]
    o_ref[...] = _gla_norm_gate(o, r, nw_ref, EA_ref).astype(o_ref.dtype)


def _gla_sample_call(gin, s0, B, T, w_gate, b_gate, norm_w):
    TT = SEQ_TILE * T
    L, E, EA, _ = _gla_tables(TT, T)
    M = jnp.asarray(_head_block_mask(GLA_DK, GLA_DV, GLA_H), F32)
    T4 = jnp.asarray(_tile_lanes(GLA_H, GLA_DV), BF)
    ones = jnp.ones((8, GLA_DV), BF)
    wg, bg, nw = _gla_params(w_gate, b_gate, norm_w)
    PAD = 8
    sspec = pl.BlockSpec((SEQ_TILE, GLA_H, GLA_DK, GLA_DV), lambda i: (i, 0, 0, 0))
    return pl.pallas_call(
        functools.partial(_gla_sample_kernel, c=T),
        grid=(B // SEQ_TILE,),
        in_specs=[pl.BlockSpec((TT, GLA_IN_W), lambda i: (i, 0)), sspec,
                  _const(wg.shape), _const(bg.shape), _const(nw.shape),
                  _const(L.shape), _const(E.shape), _const(EA.shape), _const(M.shape), _const(T4.shape),
                  _const(ones.shape)],
        out_specs=[pl.BlockSpec((TT, GLA_WIDTH), lambda i: (i, 0)), sspec],
        out_shape=[jax.ShapeDtypeStruct((B * T, GLA_WIDTH), BF),
                   jax.ShapeDtypeStruct((B, GLA_H, GLA_DK, GLA_DV), F32)],
        scratch_shapes=[pltpu.VMEM((TT + PAD, LANE), F32),
                        pltpu.VMEM((TT + PAD, LANE), F32),
                        pltpu.VMEM((TT + PAD, 2 * LANE), F32),
                        pltpu.VMEM((TT, 2 * LANE), F32)],
        compiler_params=_cp(1),
        name="gla_sample",
    )(gin, s0, wg, bg, nw, L, E, EA, M, T4, ones)


def _ret_sample_kernel(x_ref, s0_ref, cos_ref, sin_ref, D_ref, rd_ref, kd_ref, G_ref, M_ref, EA_ref, nw_ref, T4_ref,
                       o_ref, sn_ref, oi_ref, *, c):
    TT = x_ref.shape[0]
    q, k, v, rg = _ret_front(x_ref, cos_ref, sin_ref)
    o = _ret_intra(q, k, v, D_ref)
    qd = (q * rd_ref[...]).astype(BF)
    kd = (k * kd_ref[...]).astype(BF)
    vb = v.astype(BF)
    M = M_ref[...]
    for s in range(TT // c):
        lo = s * c
        S0 = s0_ref[s].reshape(RET_H * RET_DK, RET_DV)
        Sbd = (_dot(S0.astype(BF), T4_ref[...]) * M).astype(BF)
        oi_ref[lo:lo + c, :] = _dot(qd[lo:lo + c, :], Sbd)
        u = _fold_head_blocks(_dot_tn(kd[lo:lo + c, :], vb[lo:lo + c, :]) * M)
        sn_ref[s] = (G_ref[...] * S0 + u).reshape(RET_H, RET_DK, RET_DV)
    o = o + oi_ref[...]
    o_ref[...] = _ret_norm_gate(o, rg, nw_ref, EA_ref).astype(o_ref.dtype)


def _ret_sample_call(rin, s0, B, T, norm_w):
    TT = SEQ_TILE * T
    cos, sin = _rope_tables(PAST_LEN + jnp.arange(T, dtype=jnp.int32))
    cos, sin = jnp.tile(cos, (SEQ_TILE, 1)), jnp.tile(sin, (SEQ_TILE, 1))
    lg = _ret_log_gamma()
    i = np.arange(TT)
    same = (i[:, None] // T == i[None, :] // T) & (i[:, None] >= i[None, :])
    Dm = jnp.asarray(np.exp(lg[:, None, None] * (i[:, None] - i[None, :])[None]) * same[None], F32)
    tt = i % T
    rd = jnp.asarray(np.repeat(np.exp(lg[None, :] * (tt[:, None] + 1)), RET_DK, 1), F32)
    kd = jnp.asarray(np.repeat(np.exp(lg[None, :] * (T - 1 - tt[:, None])), RET_DK, 1), F32)
    G = jnp.asarray(np.repeat(np.repeat(np.exp(lg * T), RET_DK)[:, None], RET_DV, 1), F32)
    M = jnp.asarray(_head_block_mask(RET_DK, RET_DV, RET_H), F32)
    EA = jnp.asarray(_head_block_mask(RET_DV, RET_DV, RET_H) / RET_DV, BF)
    T4 = jnp.asarray(_tile_lanes(RET_H, RET_DV), BF)
    nw = norm_w.reshape(1, -1)
    sspec = pl.BlockSpec((SEQ_TILE, RET_H, RET_DK, RET_DV), lambda i: (i, 0, 0, 0))
    consts = (cos, sin, Dm, rd, kd, G, M, EA, nw, T4)
    return pl.pallas_call(
        functools.partial(_ret_sample_kernel, c=T),
        grid=(B // SEQ_TILE,),
        in_specs=[pl.BlockSpec((TT, RET_IN_W), lambda i: (i, 0)), sspec] + [_const(a.shape) for a in consts],
        out_specs=[pl.BlockSpec((TT, RET_WIDTH), lambda i: (i, 0)), sspec],
        out_shape=[jax.ShapeDtypeStruct((B * T, RET_WIDTH), BF),
                   jax.ShapeDtypeStruct((B, RET_H, RET_DK, RET_DV), F32)],
        scratch_shapes=[pltpu.VMEM((TT, RET_WIDTH), F32)],
        compiler_params=_cp(1),
        name="ret_sample",
    )(rin, s0, *consts)


def _ssd_sample_kernel(x_ref, c0_ref, s0_ref, cw_ref, cb_ref, dtb_ref, alog_ref, dexp_ref, nw_ref,
                       L_ref, Mk_ref, Eexp_ref, Bl_ref, R2_ref, T8_ref, T8T_ref, M8_ref, ones_ref,
                       o_ref, sn_ref, cn_ref, xp_ref, oi_ref, *, c):
    TT = x_ref.shape[0]
    ns = TT // c
    RP = 2 * c
    xp_ref[...] = jnp.zeros_like(xp_ref)
    z = x_ref[:, 0:SSD_WIDTH]
    sdt = x_ref[:, SSD_WIDTH + SSD_CONV_DIM:SSD_IN_W]
    for s in range(ns):
        base = 8 + s * RP
        xp_ref[base + c - (SSD_CONV_W - 1):base + c, :] = c0_ref[s]
        xp_ref[base + c:base + RP, :] = x_ref[s * c:(s + 1) * c, SSD_WIDTH:SSD_WIDTH + SSD_CONV_DIM]
    conv = _ssd_conv(xp_ref, cw_ref, cb_ref, ns * RP)
    xbc = _silu(conv.reshape(ns, RP, SSD_CONV_DIM)[:, c:RP, :].reshape(TT, SSD_CONV_DIM))
    for s in range(ns):
        base = 8 + s * RP
        cn_ref[s] = xp_ref[base + RP - (SSD_CONV_W - 1):base + RP, :]
    xs = xbc[:, 0:SSD_WIDTH]
    bm = xbc[:, SSD_WIDTH:SSD_WIDTH + LANE]
    cm = xbc[:, SSD_WIDTH + LANE:SSD_CONV_DIM]

    dt = _softplus(sdt + dtb_ref[...])
    la = dt * (-jnp.exp(alog_ref[...]))
    g = _dot_3x(L_ref[...], la)
    gl = _dot_3x(Bl_ref[...], g)
    Eexp = Eexp_ref[...]
    eg_x = _dot_x2(jnp.exp(g), Eexp)
    cw_x = _dot_x2(dt * jnp.exp(gl - g), Eexp)
    egl_x = _dot_x2(jnp.exp(gl), Eexp)
    o = _ssd_intra(xs, bm, cm, g, dt, Mk_ref)

    Cx = _dot(cm.astype(BF), R2_ref[...])
    Bx = _dot(bm.astype(BF), R2_ref[...])
    Xw = xs * cw_x
    M8 = M8_ref[...]
    nh = SSD_H

    def rows_by_head(a):
        return jnp.concatenate([a] * nh, axis=0) * M8

    for s in range(ns):
        lo = s * c
        S0 = s0_ref[s].reshape(SSD_H * SSD_N, SSD_P)
        oi = _dot(rows_by_head(Cx[lo:lo + c, :]).astype(BF), S0.astype(BF))
        oix = _dot_x2(oi, T8_ref[...]) * M8
        acc = oix[0:c, :]
        for h in range(1, nh):
            acc = acc + oix[h * c:(h + 1) * c, :]
        oi_ref[lo:lo + c, :] = acc
        Xst = _dot(rows_by_head(Xw[lo:lo + c, :]).astype(BF), T8T_ref[...])
        u = _dot_tn(rows_by_head(Bx[lo:lo + c, :]).astype(BF), Xst.astype(BF))
        acol = _col_bcast(egl_x[lo:lo + c, :], ones_ref)
        sn_ref[s] = (acol * S0 + u).reshape(SSD_H, SSD_N, SSD_P)

    o = o + eg_x * oi_ref[...]
    y = (o + dexp_ref[...] * xs) * _silu(z)
    ms = jnp.mean(y * y, axis=-1, keepdims=True)
    o_ref[...] = (y * lax.rsqrt(ms + EPS) * nw_ref[...]).astype(o_ref.dtype)


def _ssd_sample_call(sin_, c0, s0, B, T, conv_w, conv_b, dt_bias, a_log, d, norm_w):
    TT = SEQ_TILE * T
    L, Mk, Eexp, _ = _ssd_tables(TT, T)
    i = np.arange(TT)
    Bl = jnp.asarray((i[None, :] == (i[:, None] // T) * T + T - 1).astype(np.float32), BF)
    hpg = SSD_H // SSD_G
    R2 = np.zeros((LANE, SSD_H * SSD_N), np.float32)
    for h in range(SSD_H):
        R2[(h // hpg) * SSD_N:(h // hpg + 1) * SSD_N, h * SSD_N:(h + 1) * SSD_N] = np.eye(SSD_N)
    T8 = _tile_lanes(SSD_H, SSD_P)
    M8 = _head_block_mask(T, SSD_P, SSD_H)
    tabs = (L, Mk, Eexp, Bl, jnp.asarray(R2, BF), jnp.asarray(T8, BF), jnp.asarray(T8.T, BF), jnp.asarray(M8, F32),
            jnp.ones((8, SSD_P), BF))
    prm = _ssd_params(conv_w, conv_b, dt_bias, a_log, d, norm_w)
    sspec = pl.BlockSpec((SEQ_TILE, SSD_H, SSD_N, SSD_P), lambda i: (i, 0, 0, 0))
    cspec = pl.BlockSpec((SEQ_TILE, SSD_CONV_W - 1, SSD_CONV_DIM), lambda i: (i, 0, 0))
    return pl.pallas_call(
        functools.partial(_ssd_sample_kernel, c=T),
        grid=(B // SEQ_TILE,),
        in_specs=[pl.BlockSpec((TT, SSD_IN_W), lambda i: (i, 0)), cspec, sspec]
                 + [_const(p.shape) for p in prm] + [_const(a.shape) for a in tabs],
        out_specs=[pl.BlockSpec((TT, SSD_WIDTH), lambda i: (i, 0)), sspec, cspec],
        out_shape=[jax.ShapeDtypeStruct((B * T, SSD_WIDTH), BF),
                   jax.ShapeDtypeStruct((B, SSD_H, SSD_N, SSD_P), F32),
                   jax.ShapeDtypeStruct((B, SSD_CONV_W - 1, SSD_CONV_DIM), F32)],
        scratch_shapes=[pltpu.VMEM((8 + SEQ_TILE * 2 * T, SSD_CONV_DIM), F32),
                        pltpu.VMEM((TT, SSD_WIDTH), F32)],
        compiler_params=_cp(1),
        name="ssd_sample",
    )(sin_, c0, s0, *prm, *tabs)


def _inproj_t_kernel(x_ref, sc_ref, sh_ref, wt_ref, og_ref, or_ref, os_ref, w_ref):
    nt, nb, D = x_ref.shape

    @pl.when(pl.program_id(0) == 0)
    def _():
        for src, dst, n in ((0, 0, N_GA + GLA_GATE_RANK), (N_GA + GLA_GATE_RANK, N_GA + LANE, N_DT - N_GA - GLA_GATE_RANK)):
            for r in range(0, n, 512):
                m = min(512, n - r)
                w_ref[dst + r:dst + r + m, :] = wt_ref[src + r:src + r + m, :].astype(BF)
        w_ref[N_GA + GLA_GATE_RANK:N_GA + LANE, :] = jnp.zeros((LANE - GLA_GATE_RANK, D), BF)
        tail = jnp.concatenate([wt_ref[N_DT:N_IN, :], jnp.zeros((LANE - SSD_H, D), F32)], axis=0)
        w_ref[IN_W - LANE:IN_W, :] = tail.astype(BF)

    h = x_ref[...] * (1.0 + sc_ref[...]) + sh_ref[...]
    for t in range(nt):
        ht = h[t].astype(BF)
        cols = slice(t * nb, (t + 1) * nb)
        og_ref[:, cols] = _dot_nt(w_ref[0:GLA_IN_W, :], ht)
        or_ref[:, cols] = _dot_nt(w_ref[GLA_IN_W:GLA_IN_W + RET_IN_W, :], ht)
        os_ref[:, cols] = _dot_nt(w_ref[GLA_IN_W + RET_IN_W:IN_W, :], ht)


def _inproj_t_call(xt, sc, sh, wt, l):
    T, B, D = xt.shape
    nt = 4
    cmap = lambda i: (0, i)
    return pl.pallas_call(
        _inproj_t_kernel,
        grid=(T // nt,),
        in_specs=[pl.BlockSpec((nt, B, D), lambda i: (i, 0, 0)),
                  pl.BlockSpec((1, B, D), lambda i: (0, 0, 0)),
                  pl.BlockSpec((1, B, D), lambda i: (0, 0, 0)),
                  _resident_layer(wt.shape, l)],
        out_specs=[pl.BlockSpec((GLA_IN_W, nt * B), cmap),
                   pl.BlockSpec((RET_IN_W, nt * B), cmap),
                   pl.BlockSpec((SSD_IN_W, nt * B), cmap)],
        out_shape=[jax.ShapeDtypeStruct((GLA_IN_W, T * B), F32),
                   jax.ShapeDtypeStruct((RET_IN_W, T * B), F32),
                   jax.ShapeDtypeStruct((SSD_IN_W, T * B), F32)],
        scratch_shapes=[pltpu.VMEM((IN_W, D), BF)],
        compiler_params=_cp(1),
        name="in_proj_t",
    )(xt, sc, sh, wt)


def _row_sum(x):
    return jnp.sum(x, axis=0, keepdims=True)


def _lane_state_readout(o, coef_ref, s0_ref, n_rows):
    nb = LANE
    half = len(o) // 2
    for part in range(2):
        def body(k8, accs, part=part):
            accs = list(accs)
            base = pl.multiple_of(k8 * 8, 8)
            grp = [coef_ref[pl.ds(base, 8), (part * half + i) * nb:(part * half + i + 1) * nb] for i in range(half)]
            for j in range(8):
                s0k = s0_ref[0, k8 * 8 + j]
                for i in range(half):
                    accs[i] = accs[i] + grp[i][j:j + 1, :] * s0k
            return tuple(accs)

        res = lax.fori_loop(0, n_rows // 8, body, tuple(o[part * half:(part + 1) * half]))
        o[part * half:(part + 1) * half] = list(res)
    return o


def _lane_state_update(sn_ref, s0_ref, decay_fn, coef_ref, val_fn, n_rows, T):
    nb = LANE

    def body(k8, carry):
        base = pl.multiple_of(k8 * 8, 8)
        grp = [coef_ref[pl.ds(base, 8), t * nb:(t + 1) * nb] for t in range(T)]
        dec = decay_fn(base)
        for j in range(8):
            dj = dec[j:j + 1, :] if dec.shape[0] == 8 else dec
            sk = dj * s0_ref[0, k8 * 8 + j]
            for t in range(T):
                sk = sk + grp[t][j:j + 1, :] * val_fn(t)
            sn_ref[0, k8 * 8 + j] = sk
        return carry

    lax.fori_loop(0, n_rows // 8, body, 0)


def _state_specs(shape, l):
    blk = (None, 1) + tuple(shape[2:])
    return pl.BlockSpec(blk, lambda h: (l, h, 0, 0, 0))


def _gla_t_kernel(x_ref, s0_ref, wg_ref, bg_ref, nw_ref, prev_ref, o_ref, sn_ref, qe_ref, ke_ref, a_ref, *, T):
    del prev_ref
    nb = LANE
    h = pl.program_id(0)
    r0 = pl.multiple_of(h * GLA_DK, GLA_DK)
    v0 = pl.multiple_of(h * GLA_DV, GLA_DV)
    q = x_ref[pl.ds(r0, GLA_DK), :] * (GLA_DK ** -0.5)
    k = x_ref[pl.ds(128 + r0, GLA_DK), :]
    gate = _dot(wg_ref[pl.ds(r0, GLA_DK), :], x_ref[512:640, :].astype(BF)) + bg_ref[pl.ds(r0, GLA_DK), :]
    la = _log_sigmoid(gate) * (1.0 / GLA_GATE_TEMP)
    gs = []
    acc = jnp.zeros((GLA_DK, nb), F32)
    for t in range(T):
        acc = acc + la[:, t * nb:(t + 1) * nb]
        gs.append(acc)
    gl = gs[T - 1]
    a_ref[...] = jnp.exp(gl)
    qs = [q[:, t * nb:(t + 1) * nb] for t in range(T)]
    ks = [k[:, t * nb:(t + 1) * nb] for t in range(T)]
    for t in range(T):
        qe_ref[:, t * nb:(t + 1) * nb] = qs[t] * jnp.exp(gs[t])
        ke_ref[:, t * nb:(t + 1) * nb] = ks[t] * jnp.exp(gl - gs[t])

    def vt(t):
        return x_ref[pl.ds(256 + v0, GLA_DV), t * nb:(t + 1) * nb]

    o = []
    for t in range(T):
        ot = jnp.zeros((GLA_DV, nb), F32)
        for u in range(t + 1):
            s = _row_sum(qs[t] * ks[u] * jnp.exp(gs[t] - gs[u]))
            ot = ot + s * vt(u)
        o.append(ot)

    o = _lane_state_readout(o, qe_ref, s0_ref, GLA_DK)
    _lane_state_update(sn_ref, s0_ref, lambda base: a_ref[pl.ds(base, 8), :], ke_ref, vt, GLA_DK, T)

    nw = nw_ref[pl.ds(v0, GLA_DV), :]
    for t in range(T):
        ms = jnp.mean(o[t] * o[t], axis=0, keepdims=True)
        r = x_ref[pl.ds(640 + v0, GLA_DV), t * nb:(t + 1) * nb]
        o_ref[:, t * nb:(t + 1) * nb] = (o[t] * lax.rsqrt(ms + EPS) * nw * _silu(r)).astype(o_ref.dtype)


def _gla_t_call(gT, s0, prev, l, T, w_gate, b_gate, norm_w):
    N = gT.shape[1]
    wg = jnp.zeros((GLA_H * GLA_DK, LANE), F32).at[:, :GLA_GATE_RANK].set(w_gate.T).astype(BF)
    bg = b_gate.reshape(-1, 1)
    nw = norm_w.reshape(-1, 1)
    sspec = _state_specs(s0.shape, l)
    ins = [gT, s0, wg, bg, nw]
    specs = [_const(gT.shape), sspec, _const(wg.shape), _const(bg.shape), _const(nw.shape)]
    aliases = {}
    if prev is not None:
        ins.append(prev)
        specs.append(pl.BlockSpec(memory_space=pl.ANY))
        aliases = {len(ins) - 1: 1}
    kern = functools.partial(_gla_t_kernel, T=T)
    if prev is None:
        kern = functools.partial(lambda *a, T: _gla_t_kernel(*a[:5], None, *a[5:], T=T), T=T)
    return pl.pallas_call(
        kern,
        grid=(GLA_H,),
        in_specs=specs,
        out_specs=[pl.BlockSpec((GLA_DV, N), lambda h: (h, 0)), sspec],
        out_shape=[jax.ShapeDtypeStruct((GLA_WIDTH, N), BF), jax.ShapeDtypeStruct(s0.shape, F32)],
        scratch_shapes=[pltpu.VMEM((GLA_DK, N), F32), pltpu.VMEM((GLA_DK, N), F32), pltpu.VMEM((GLA_DK, LANE), F32)],
        input_output_aliases=aliases,
        compiler_params=_cp(1),
        name="gla_t",
    )(*ins)


def _ret_t_kernel(x_ref, s0_ref, cos_ref, sin_ref, pw_ref, nw_ref, prev_ref, o_ref, sn_ref, qd_ref, kd_ref, *, T):
    del prev_ref
    nb = LANE
    h = pl.program_id(0)
    r0 = pl.multiple_of(h * RET_DK, RET_DK)
    half_k = RET_DK // 2
    cos, sin = cos_ref[...], sin_ref[...]

    def rope_t(base):
        x1 = x_ref[pl.ds(base + r0, half_k), :]
        x2 = x_ref[pl.ds(base + r0 + half_k, half_k), :]
        return jnp.concatenate([x1 * cos - x2 * sin, x1 * sin + x2 * cos], axis=0)

    q = rope_t(0)
    k = rope_t(256) * (RET_DK ** -0.5)
    pw = pw_ref[h]
    qs = [q[:, t * nb:(t + 1) * nb] for t in range(T)]
    ks = [k[:, t * nb:(t + 1) * nb] for t in range(T)]
    for t in range(T):
        qd_ref[:, t * nb:(t + 1) * nb] = qs[t] * pw[t + 1:t + 2, :]
        kd_ref[:, t * nb:(t + 1) * nb] = ks[t] * pw[T - 1 - t:T - t, :]

    def vt(t):
        return x_ref[pl.ds(512 + r0, RET_DV), t * nb:(t + 1) * nb]

    o = []
    for t in range(T):
        ot = jnp.zeros((RET_DV, nb), F32)
        for u in range(t + 1):
            s = _row_sum(qs[t] * ks[u]) * pw[t - u:t - u + 1, :]
            ot = ot + s * vt(u)
        o.append(ot)

    o = _lane_state_readout(o, qd_ref, s0_ref, RET_DK)
    _lane_state_update(sn_ref, s0_ref, lambda base: pw[T:T + 1, :], kd_ref, vt, RET_DK, T)

    nw = nw_ref[pl.ds(r0, RET_DV), :]
    for t in range(T):
        mu = jnp.mean(o[t], axis=0, keepdims=True)
        d = o[t] - mu
        var = jnp.mean(d * d, axis=0, keepdims=True)
        rg = x_ref[pl.ds(768 + r0, RET_DV), t * nb:(t + 1) * nb]
        o_ref[:, t * nb:(t + 1) * nb] = (d * lax.rsqrt(var + EPS) * nw * _silu(rg)).astype(o_ref.dtype)


def _ret_t_call(rT, s0, prev, l, T, norm_w):
    N = rT.shape[1]
    B = N // T
    half = RET_DK // 2
    inv = ROPE_BASE ** (-jnp.arange(half, dtype=F32) / half)
    ang = inv[:, None] * (PAST_LEN + jnp.arange(T, dtype=jnp.int32)).astype(F32)[None, :]
    cos = jnp.repeat(jnp.cos(ang), B, axis=1)
    sin = jnp.repeat(jnp.sin(ang), B, axis=1)
    lg = _ret_log_gamma()
    pw = jnp.asarray(np.repeat(np.exp(lg[:, None] * np.arange(16)[None, :])[:, :, None], LANE, axis=2), F32)
    nw = norm_w.reshape(-1, 1)
    sspec = _state_specs(s0.shape, l)
    ins = [rT, s0, cos, sin, pw, nw]
    specs = [_const(rT.shape), sspec, _const(cos.shape), _const(sin.shape), _const(pw.shape), _const(nw.shape)]
    aliases = {}
    kern = functools.partial(_ret_t_kernel, T=T)
    if prev is not None:
        ins.append(prev)
        specs.append(pl.BlockSpec(memory_space=pl.ANY))
        aliases = {len(ins) - 1: 1}
    else:
        kern = functools.partial(lambda *a, T: _ret_t_kernel(*a[:6], None, *a[6:], T=T), T=T)
    return pl.pallas_call(
        kern,
        grid=(RET_H,),
        in_specs=specs,
        out_specs=[pl.BlockSpec((RET_DV, N), lambda h: (h, 0)), sspec],
        out_shape=[jax.ShapeDtypeStruct((RET_WIDTH, N), BF), jax.ShapeDtypeStruct(s0.shape, F32)],
        scratch_shapes=[pltpu.VMEM((RET_DK, N), F32), pltpu.VMEM((RET_DK, N), F32)],
        input_output_aliases=aliases,
        compiler_params=_cp(1),
        name="ret_t",
    )(*ins)


def _ssd_t_kernel(x_ref, c0_ref, s0_ref, cw_ref, cb_ref, dtb_ref, alog_ref, d_ref, nw_ref, prevs_ref, prevc_ref,
                  o_ref, sn_ref, cn_ref, hist_ref, y_ref, ssq_ref, cm_ref, bw_ref, xw_ref, *, T):
    del prevs_ref, prevc_ref
    nb = LANE
    W1 = SSD_CONV_W - 1
    h = pl.program_id(0)
    XB = SSD_WIDTH

    @pl.when(h == 0)
    def _():
        ssq_ref[...] = jnp.zeros_like(ssq_ref)
        for i in range(W1):
            for j in range(SSD_CONV_DIM // LANE):
                hist_ref[j * LANE:(j + 1) * LANE, i * nb:(i + 1) * nb] = c0_ref[0, i][:, j * LANE:(j + 1) * LANE].T
                cn_ref[0, i, :, j * LANE:(j + 1) * LANE] = \
                    x_ref[XB + j * LANE:XB + (j + 1) * LANE, (T - W1 + i) * nb:(T - W1 + i + 1) * nb].T

    def conv_rows(ro):
        w = cw_ref[pl.ds(ro, 64), :]
        b = cb_ref[pl.ds(ro, 64), :]
        xx = [hist_ref[pl.ds(ro, 64), i * nb:(i + 1) * nb] for i in range(W1)]
        xx += [x_ref[pl.ds(XB + ro, 64), t * nb:(t + 1) * nb] for t in range(T)]
        out = []
        for t in range(T):
            acc = b + w[:, 0:1] * xx[t]
            for i in range(1, SSD_CONV_W):
                acc = acc + w[:, i:i + 1] * xx[t + i]
            out.append(_silu(acc))
        return out

    grp = h // (SSD_H // SSD_G)
    xs = conv_rows(pl.multiple_of(h * SSD_P, SSD_P))
    bm = conv_rows(pl.multiple_of(SSD_WIDTH + grp * SSD_N, SSD_N))
    cm = conv_rows(pl.multiple_of(SSD_WIDTH + SSD_G * SSD_N + grp * SSD_N, SSD_N))

    dt_all = _softplus(x_ref[pl.ds(XB + SSD_CONV_DIM + h, 1), :] + dtb_ref[pl.ds(h, 1), :])
    a = -jnp.exp(alog_ref[pl.ds(h, 1), :])
    dts = [dt_all[:, t * nb:(t + 1) * nb] for t in range(T)]
    gs = []
    acc = jnp.zeros((1, nb), F32)
    for t in range(T):
        acc = acc + dts[t] * a
        gs.append(acc)
    gl = gs[T - 1]

    o = []
    for t in range(T):
        ot = jnp.zeros((SSD_P, nb), F32)
        for u in range(t + 1):
            s = _row_sum(cm[t] * bm[u]) * (jnp.exp(gs[t] - gs[u]) * dts[u])
            ot = ot + s * xs[u]
        o.append(ot)

    for t in range(T):
        cm_ref[:, t * nb:(t + 1) * nb] = cm[t] * jnp.exp(gs[t])
        bw_ref[:, t * nb:(t + 1) * nb] = bm[t]
        xw_ref[:, t * nb:(t + 1) * nb] = xs[t] * (dts[t] * jnp.exp(gl - gs[t]))

    o = _lane_state_readout(o, cm_ref, s0_ref, SSD_N)
    egl = jnp.exp(gl)
    _lane_state_update(sn_ref, s0_ref, lambda base: egl, bw_ref, lambda t: xw_ref[:, t * nb:(t + 1) * nb], SSD_N, T)

    dd = d_ref[pl.ds(h, 1), :]
    p0 = pl.multiple_of(h * SSD_P, SSD_P)
    for t in range(T):
        z = x_ref[pl.ds(p0, SSD_P), t * nb:(t + 1) * nb]
        y = (o[t] + dd * xs[t]) * _silu(z)
        y_ref[pl.ds(p0, SSD_P), t * nb:(t + 1) * nb] = y
        ssq_ref[:, t * nb:(t + 1) * nb] += _row_sum(y * y)

    @pl.when(h == SSD_H - 1)
    def _():
        scale = lax.rsqrt(ssq_ref[...] * (1.0 / SSD_WIDTH) + EPS)
        o_ref[...] = (y_ref[...] * scale * nw_ref[...]).astype(o_ref.dtype)


def _ssd_t_call(sT, c0, s0, prev_s, prev_c, l, T, conv_w, conv_b, dt_bias, a_log, d, norm_w):
    N = sT.shape[1]
    col = lambda v: jnp.zeros((LANE, 1), F32).at[:SSD_H, 0].set(v)
    prm = (conv_w.T, conv_b.reshape(-1, 1), col(dt_bias), col(a_log), col(d), norm_w.reshape(-1, 1))
    sspec = _state_specs(s0.shape, l)
    cspec = pl.BlockSpec((1,) + tuple(c0.shape[1:]), lambda h: (l, 0, 0, 0))
    ins = [sT, c0, s0, *prm]
    specs = [_const(sT.shape), cspec, sspec] + [_const(p.shape) for p in prm]
    aliases = {}
    kern = functools.partial(_ssd_t_kernel, T=T)
    if prev_s is not None:
        ins += [prev_s, prev_c]
        specs += [pl.BlockSpec(memory_space=pl.ANY), pl.BlockSpec(memory_space=pl.ANY)]
        aliases = {len(ins) - 2: 1, len(ins) - 1: 2}
    else:
        kern = functools.partial(lambda *a, T: _ssd_t_kernel(*a[:9], None, None, *a[9:], T=T), T=T)
    return pl.pallas_call(
        kern,
        grid=(SSD_H,),
        in_specs=specs,
        out_specs=[_const((SSD_WIDTH, N)), sspec, cspec],
        out_shape=[jax.ShapeDtypeStruct((SSD_WIDTH, N), BF), jax.ShapeDtypeStruct(s0.shape, F32),
                   jax.ShapeDtypeStruct(c0.shape, F32)],
        scratch_shapes=[pltpu.VMEM((SSD_CONV_DIM, (SSD_CONV_W - 1) * LANE), F32),
                        pltpu.VMEM((SSD_WIDTH, N), F32), pltpu.VMEM((1, N), F32),
                        pltpu.VMEM((SSD_N, N), F32), pltpu.VMEM((SSD_N, N), F32), pltpu.VMEM((SSD_P, N), F32)],
        input_output_aliases=aliases,
        compiler_params=_cp(1),
        name="ssd_t",
    )(*ins)


def _outproj_t_kernel(x_ref, g_ref, og_ref, or_ref, os_ref, w_ref, lg_ref, lb_ref, o_ref):
    nt, nb, D = x_ref.shape
    for t in range(nt):
        cols = slice(t * nb, (t + 1) * nb)
        mix = (_dot_tn(og_ref[:, cols], w_ref[0:GLA_WIDTH, :])
               + _dot_tn(or_ref[:, cols], w_ref[GLA_WIDTH:GLA_WIDTH + RET_WIDTH, :])
               + _dot_tn(os_ref[:, cols], w_ref[GLA_WIDTH + RET_WIDTH:D, :]))
        y = ALPHA * x_ref[t] + g_ref[0] * mix
        o_ref[t] = _layer_norm(y, lg_ref[0], lb_ref[0])


def _outproj_t_call(xt, g1, ogT, orT, osT, w_out, ln_g, ln_b):
    T, B, D = xt.shape
    nt = 4
    cmap = lambda i: (0, i)
    return pl.pallas_call(
        _outproj_t_kernel,
        grid=(T // nt,),
        in_specs=[pl.BlockSpec((nt, B, D), lambda i: (i, 0, 0)),
                  pl.BlockSpec((1, B, D), lambda i: (0, 0, 0)),
                  pl.BlockSpec((GLA_WIDTH, nt * B), cmap),
                  pl.BlockSpec((RET_WIDTH, nt * B), cmap),
                  pl.BlockSpec((SSD_WIDTH, nt * B), cmap),
                  _const((D, D)), _const((1, 1, D)), _const((1, 1, D))],
        out_specs=pl.BlockSpec((nt, B, D), lambda i: (i, 0, 0)),
        out_shape=jax.ShapeDtypeStruct((T, B, D), F32),
        compiler_params=_cp(1),
        name="out_proj_ln_t",
    )(xt, g1, ogT, orT, osT, w_out, ln_g.reshape(1, 1, D), ln_b.reshape(1, 1, D))


def _outproj_kernel(x_ref, g_ref, og_ref, or_ref, os_ref, w_ref, lg_ref, lb_ref, o_ref):
    bB, bT, D = x_ref.shape
    mix = (_dot(og_ref[...], w_ref[0:GLA_WIDTH, :])
           + _dot(or_ref[...], w_ref[GLA_WIDTH:GLA_WIDTH + RET_WIDTH, :])
           + _dot(os_ref[...], w_ref[GLA_WIDTH + RET_WIDTH:D, :]))
    y = ALPHA * x_ref[...] + g_ref[...] * mix.reshape(bB, bT, D)
    o_ref[...] = _layer_norm(y, lg_ref[...], lb_ref[...])


def _outproj_call(x3, g1, og, orr, os_, w_out, ln_g, ln_b):
    B, T, D = x3.shape
    bB, bT = _tok_tiles(B, T)
    nT = T // bT
    R = bB * bT
    xmap = lambda i, j: (i, j, 0)
    mmap = lambda i, j: (i, 0, 0)
    rmap = lambda i, j: (i * nT + j, 0)
    return pl.pallas_call(
        _outproj_kernel,
        grid=(B // bB, nT),
        in_specs=[pl.BlockSpec((bB, bT, D), xmap),
                  pl.BlockSpec((bB, 1, D), mmap),
                  pl.BlockSpec((R, GLA_WIDTH), rmap),
                  pl.BlockSpec((R, RET_WIDTH), rmap),
                  pl.BlockSpec((R, SSD_WIDTH), rmap),
                  _const((D, D)), _const((1, 1, D)), _const((1, 1, D))],
        out_specs=pl.BlockSpec((bB, bT, D), xmap),
        out_shape=jax.ShapeDtypeStruct((B, T, D), F32),
        compiler_params=_cp(2),
        name="out_proj_ln",
    )(x3, g1, og, orr, os_, w_out, ln_g.reshape(1, 1, D), ln_b.reshape(1, 1, D))


ROUTE_OFF = 8


def _moe_route_t(lt):
    R = lt.shape[1]
    neg = jnp.float32(-jnp.inf)
    row8 = lax.broadcasted_iota(jnp.int32, (8, 1), 0)
    lg = jnp.where(row8 < MOE_GROUPS, lt[0:8, :], neg)
    mg = jnp.max(lg, axis=0, keepdims=True)
    gsel = jnp.min(jnp.where(lg == mg, row8, 8), axis=0, keepdims=True)
    g_gate = 1.0 / jnp.sum(jnp.exp(lg - mg), axis=0, keepdims=True)
    rowe = lax.broadcasted_iota(jnp.int32, (MOE_EXPERTS, 1), 0)
    le = jnp.where((rowe // MOE_PER_GROUP) == gsel, lt[ROUTE_OFF:ROUTE_OFF + MOE_EXPERTS, :], neg)
    m1 = jnp.max(le, axis=0, keepdims=True)
    i1 = jnp.min(jnp.where(le == m1, rowe, MOE_EXPERTS), axis=0, keepdims=True)
    le2 = jnp.where(rowe == i1, neg, le)
    m2 = jnp.max(le2, axis=0, keepdims=True)
    i2 = jnp.min(jnp.where(le2 == m2, rowe, MOE_EXPERTS), axis=0, keepdims=True)
    e2 = jnp.exp(m2 - m1)
    w1 = g_gate / (1.0 + e2)
    w2 = g_gate * e2 / (1.0 + e2)
    comb = jnp.where(rowe == i1, w1, jnp.where(rowe == i2, w2, 0.0))
    cg = comb[0:4, :]
    for g in range(1, MOE_GROUPS):
        cg = cg + comb[g * MOE_PER_GROUP:(g + 1) * MOE_PER_GROUP, :]
    return gsel, cg, comb


MOE_SUB = 256
MOE_BLK = 16
MOE_ROWS = 256
MOE_NPS = MOE_SUB + MOE_GROUPS * MOE_BLK
MOE_MAXB = MOE_SUB // MOE_BLK + 1


def _moe_kernel(x_ref, sc_ref, sh_ref, g_ref, wr_ref, br_ref, us_ref, w1_ref, w3_ref, w2_ref, lg_ref, lb_ref,
                o_ref, hb_ref, cwb_ref, hp_ref, cwp_ref, yp_ref, pos_ref,
                cgrp_ref, fill_ref, cur_ref, na_ref, dst_ref, nb_ref, so_ref, *, n_steps):
    bB, bT, D = x_ref.shape
    R = bB * bT
    n_q = R // MOE_SUB
    s = pl.program_id(1)
    x = x_ref[...]
    row8 = lax.broadcasted_iota(jnp.int32, (8, 1), 0)
    slot = lax.broadcasted_iota(jnp.int32, (MOE_NPS, 1), 0).astype(F32)

    @pl.when((pl.program_id(0) == 0) & (s == 0))
    def _():
        hb_ref[...] = jnp.zeros_like(hb_ref)
        cwb_ref[...] = jnp.zeros_like(cwb_ref)
        yp_ref[...] = jnp.zeros_like(yp_ref)

    @pl.when(s == 0)
    def _():
        na_ref[0] = 0
        for g in range(MOE_GROUPS):
            cur_ref[g] = -1
            fill_ref[g] = 0

    @pl.when(s < n_steps)
    def _():
        h = (x * (1.0 + sc_ref[...]) + sh_ref[...]).reshape(R, D)
        for q in range(n_q):
            u = s * n_q + q
            hq = h[q * MOE_SUB:(q + 1) * MOE_SUB, :].astype(BF)
            gsel, cg, _ = _moe_route_t(_dot_nt(wr_ref[...], hq) + br_ref[...])
            onehot = jnp.where(row8 == gsel, 1.0, 0.0)
            rank = _dot(onehot.astype(BF), us_ref[...])
            cnt = jnp.sum(onehot, axis=1, keepdims=True)
            seg = jnp.ceil(cnt * (1.0 / MOE_BLK)) * MOE_BLK
            off = jnp.zeros((8, 1), F32)
            for g in range(1, MOE_GROUPS):
                off = off + jnp.where(row8 >= g, seg[g - 1:g, :], 0.0)
            pos = jnp.sum(onehot * (off + rank), axis=0, keepdims=True)
            pos_ref[u] = jnp.broadcast_to(pos, (8, MOE_SUB))
            perm = jnp.where(slot == pos, 1.0, 0.0).astype(BF)
            hp_ref[...] = _dot(perm, hq).astype(BF)
            cg8 = jnp.concatenate([cg, jnp.zeros((4, MOE_SUB), F32)], axis=0)
            cg_hi = cg8.astype(BF)
            cg_lo = (cg8 - cg_hi.astype(F32)).astype(BF)
            cwp_ref[...] = _dot_nt(perm, cg_hi) + _dot_nt(perm, cg_lo)
            for g in range(MOE_GROUPS):
                so = off[g, 0].astype(jnp.int32)
                nb = (seg[g, 0] * (1.0 / MOE_BLK)).astype(jnp.int32)
                so_ref[u * MOE_GROUPS + g] = so
                nb_ref[u * MOE_GROUPS + g] = nb

                def put(k, carry, g=g, so=so, u=u):
                    f = fill_ref[g]
                    c = cur_ref[g]
                    na = na_ref[0]
                    fresh = (c < 0) | (f >= MOE_ROWS)
                    c = jnp.where(fresh, na, c)
                    f = jnp.where(fresh, 0, f)
                    cgrp_ref[c] = g
                    na_ref[0] = jnp.where(fresh, na + 1, na)
                    dst = pl.multiple_of(c * MOE_ROWS + f, MOE_BLK)
                    src = pl.multiple_of(so + k * MOE_BLK, MOE_BLK)
                    hb_ref[pl.ds(dst, MOE_BLK), :] = hp_ref[pl.ds(src, MOE_BLK), :]
                    cwb_ref[pl.ds(dst, MOE_BLK), :] = cwp_ref[pl.ds(src, MOE_BLK), :]
                    dst_ref[(u * MOE_GROUPS + g) * MOE_MAXB + k] = dst
                    cur_ref[g] = c
                    fill_ref[g] = f + MOE_BLK
                    return carry

                lax.fori_loop(0, nb, put, 0)

    @pl.when(s == n_steps - 1)
    def _():
        def chunk(c, carry):
            g = cgrp_ref[c]
            start = pl.multiple_of(c * MOE_ROWS, MOE_ROWS)
            hc = hb_ref[pl.ds(start, MOE_ROWS), :]
            cw = cwb_ref[pl.ds(start, MOE_ROWS), :]
            acc = jnp.zeros((MOE_ROWS, D), F32)
            for j in range(MOE_PER_GROUP):
                e = g * MOE_PER_GROUP + j
                hid = _silu(_dot(hc, w1_ref[e])) * _dot(hc, w3_ref[e]) * cw[:, j:j + 1]
                acc = acc + _dot(hid.astype(BF), w2_ref[e])
            hb_ref[pl.ds(start, MOE_ROWS), :] = acc.astype(BF)
            return carry

        lax.fori_loop(0, na_ref[0], chunk, 0)

    @pl.when(s >= n_steps)
    def _():
        ys = []
        for q in range(n_q):
            u = (s - n_steps) * n_q + q
            for g in range(MOE_GROUPS):
                so = so_ref[u * MOE_GROUPS + g]

                def take(k, carry, g=g, so=so, u=u):
                    src = pl.multiple_of(dst_ref[(u * MOE_GROUPS + g) * MOE_MAXB + k], MOE_BLK)
                    dst = pl.multiple_of(so + k * MOE_BLK, MOE_BLK)
                    yp_ref[pl.ds(dst, MOE_BLK), :] = hb_ref[pl.ds(src, MOE_BLK), :]
                    return carry

                lax.fori_loop(0, nb_ref[u * MOE_GROUPS + g], take, 0)
            perm = jnp.where(slot == pos_ref[u][0:1, :], 1.0, 0.0).astype(BF)
            ys.append(_dot_tn(perm, yp_ref[...]))
        y = jnp.concatenate(ys, axis=0)
        z = ALPHA * x + g_ref[...] * y.reshape(bB, bT, D)
        o_ref[...] = _layer_norm(z, lg_ref[...], lb_ref[...])


def _resident_layer(shape, l):
    return pl.BlockSpec((None,) + tuple(shape[1:]), lambda *_: (l,) + (0,) * (len(shape) - 1),
                        pipeline_mode=pl.Buffered(1))


def _moe_call(x3, sc, sh, g2, wr, br, w1, w3, w2, l, ln_g, ln_b):
    B, T, D = x3.shape
    bB, bT = _tok_tiles(B, T)
    R = bB * bT
    if bB == 1:
        n_pools, n_steps = B, T // bT
        xmap = lambda p, s: (p, s % n_steps, 0)
        omap = lambda p, s: (p, jnp.maximum(s - n_steps, 0), 0)
        mmap = lambda p, s: (p, 0, 0)
        mshape = (1, 1, D)
    else:
        n_pools, n_steps = 1, B // bB
        xmap = lambda p, s: (s % n_steps, 0, 0)
        omap = lambda p, s: (jnp.maximum(s - n_steps, 0), 0, 0)
        mmap = lambda p, s: (0, 0, 0)
        mshape = (1, bT, D)
    n_sub = n_steps * (R // MOE_SUB)
    n_chunks = n_sub * MOE_SUB // MOE_ROWS + MOE_GROUPS
    us = jnp.asarray(np.triu(np.ones((MOE_SUB, MOE_SUB), np.float32), 1), BF)
    smem = lambda n: pltpu.SMEM((n,), jnp.int32)
    return pl.pallas_call(
        functools.partial(_moe_kernel, n_steps=n_steps),
        grid=(n_pools, 2 * n_steps),
        in_specs=[pl.BlockSpec((bB, bT, D), xmap),
                  pl.BlockSpec(mshape, mmap), pl.BlockSpec(mshape, mmap), pl.BlockSpec(mshape, mmap),
                  _const(wr.shape), _const(br.shape), _const(us.shape),
                  _resident_layer(w1.shape, l), _resident_layer(w3.shape, l), _resident_layer(w2.shape, l),
                  _const((1, 1, D)), _const((1, 1, D))],
        out_specs=pl.BlockSpec((bB, bT, D), omap),
        out_shape=jax.ShapeDtypeStruct((B, T, D), F32),
        scratch_shapes=[pltpu.VMEM((n_chunks * MOE_ROWS, D), BF), pltpu.VMEM((n_chunks * MOE_ROWS, 8), F32),
                        pltpu.VMEM((MOE_NPS, D), BF), pltpu.VMEM((MOE_NPS, 8), F32), pltpu.VMEM((MOE_NPS, D), BF),
                        pltpu.VMEM((n_sub, 8, MOE_SUB), F32),
                        smem(n_chunks), smem(MOE_GROUPS), smem(MOE_GROUPS), smem(1),
                        smem(n_sub * MOE_GROUPS * MOE_MAXB), smem(n_sub * MOE_GROUPS), smem(n_sub * MOE_GROUPS)],
        compiler_params=_cp(2),
        name="moe_ln",
    )(x3, sc, sh, g2, wr, br, us, w1, w3, w2, ln_g.reshape(1, 1, D), ln_b.reshape(1, 1, D))


def _router_params(w_group, b_group, w_expert, b_expert):
    wr = jnp.zeros((LANE, D_MODEL), F32).at[:MOE_GROUPS].set(w_group.T)
    wr = wr.at[ROUTE_OFF:ROUTE_OFF + MOE_EXPERTS].set(w_expert.T)
    br = jnp.zeros((LANE, 1), F32).at[:MOE_GROUPS, 0].set(b_group).at[ROUTE_OFF:ROUTE_OFF + MOE_EXPERTS, 0].set(b_expert)
    return wr.astype(BF), br


def kernel(x_prompt, x_sample, c_prompt, c_sample, state_gla, state_ret, state_ssd, state_conv, w_ada, b_ada, w_in, gla_w_gate, gla_b_gate, gla_norm, ret_norm, ssd_conv_w, ssd_conv_b, ssd_dt_bias, ssd_a_log, ssd_d, ssd_norm, w_out, ln1_g, ln1_b, moe_w_group, moe_b_group, moe_w_expert, moe_b_expert, moe_w1, moe_w3, moe_w2, ln2_g, ln2_b):
    Bp, Tp, D = x_prompt.shape
    Bs, Ts, _ = x_sample.shape
    w_in_t = jnp.swapaxes(w_in, 1, 2)
    w_out_b = w_out.astype(BF)
    w1_b, w3_b, w2_b = moe_w1.astype(BF), moe_w3.astype(BF), moe_w2.astype(BF)

    mod = _mod_call(jnp.concatenate([c_prompt, c_sample], axis=0), w_ada, b_ada)

    def moe(x, sc2, sh2, g2, l):
        wr, br = _router_params(moe_w_group[l], moe_b_group[l], moe_w_expert[l], moe_b_expert[l])
        return _moe_call(x, sc2, sh2, g2, wr, br, w1_b, w3_b, w2_b, l, ln2_g[l], ln2_b[l])

    x = x_prompt
    new = [[], [], [], []]
    for l in range(DEPTH):
        sh1, sc1, g1, sh2, sc2, g2 = (mod[l, :Bp, None, i * D:(i + 1) * D] for i in range(6))
        gin, rin, sin_ = _inproj_call(x, sc1, sh1, w_in_t, l)
        og, s_gla = _gla_prompt_call(gin, Bp, Tp, gla_w_gate[l], gla_b_gate[l], gla_norm[l])
        orr, s_ret = _ret_prompt_call(rin, Bp, Tp, ret_norm[l])
        os_, s_ssd, s_conv = _ssd_prompt_call(sin_, Bp, Tp, ssd_conv_w[l], ssd_conv_b[l], ssd_dt_bias[l],
                                              ssd_a_log[l], ssd_d[l], ssd_norm[l])
        x = _outproj_call(x, g1, og, orr, os_, w_out_b[l], ln1_g[l], ln1_b[l])
        x = moe(x, sc2, sh2, g2, l)
        for acc, s in zip(new, (s_gla, s_ret, s_ssd, s_conv)):
            acc.append(s)
    y_p = x
    gla_p, ret_p, ssd_p, conv_p = (jnp.stack(a) for a in new)

    x = jnp.swapaxes(x_sample, 0, 1)
    sg = jnp.transpose(state_gla, (0, 2, 3, 4, 1))
    sr = jnp.transpose(state_ret, (0, 2, 3, 4, 1))
    ss = jnp.transpose(state_ssd, (0, 2, 3, 4, 1))
    cv = jnp.transpose(state_conv, (0, 2, 1, 3))
    gla_n = ret_n = ssd_n = conv_n = None
    for l in range(DEPTH):
        sh1, sc1, g1, sh2, sc2, g2 = (mod[l, None, Bp:, i * D:(i + 1) * D] for i in range(6))
        gT, rT, sT = _inproj_t_call(x, sc1, sh1, w_in_t, l)
        ogT, gla_n = _gla_t_call(gT, sg, gla_n, l, Ts, gla_w_gate[l], gla_b_gate[l], gla_norm[l])
        orT, ret_n = _ret_t_call(rT, sr, ret_n, l, Ts, ret_norm[l])
        osT, ssd_n, conv_n = _ssd_t_call(sT, cv, ss, ssd_n, conv_n, l, Ts, ssd_conv_w[l], ssd_conv_b[l],
                                         ssd_dt_bias[l], ssd_a_log[l], ssd_d[l], ssd_norm[l])
        x = _outproj_t_call(x, g1, ogT, orT, osT, w_out_b[l], ln1_g[l], ln1_b[l])
        x = moe(x, sc2, sh2, g2, l)
    y_s = jnp.swapaxes(x, 0, 1)
    gla_s = jnp.transpose(gla_n, (0, 4, 1, 2, 3))
    ret_s = jnp.transpose(ret_n, (0, 4, 1, 2, 3))
    ssd_s = jnp.transpose(ssd_n, (0, 4, 1, 2, 3))
    conv_s = jnp.transpose(conv_n, (0, 2, 1, 3))
    return (y_p, y_s, gla_p, ret_p, ssd_p, conv_p, gla_s, ret_s, ssd_s, conv_s)
```

```python
import functools
import math

import numpy as np
import jax
import jax.numpy as jnp
from jax import lax
from jax.experimental import pallas as pl
from jax.experimental.pallas import tpu as pltpu

F32 = jnp.float32
BF = jnp.bfloat16

D_MODEL = 1024
DEPTH = 2
PAST_LEN = 16384
GLA_H, GLA_DK, GLA_DV = 4, 32, 64
GLA_WIDTH = GLA_H * GLA_DV
GLA_GATE_RANK = 16
GLA_GATE_TEMP = 16.0
GLA_CHUNK = 16
RET_H, RET_DK, RET_DV = 4, 64, 64
RET_WIDTH = RET_H * RET_DV
ROPE_BASE = 10000.0
SSD_H, SSD_P, SSD_G, SSD_N = 8, 64, 2, 64
SSD_WIDTH = SSD_H * SSD_P
SSD_CONV_W = 4
SSD_CONV_DIM = SSD_WIDTH + 2 * SSD_G * SSD_N
MOE_GROUPS, MOE_PER_GROUP = 4, 4
MOE_EXPERTS = MOE_GROUPS * MOE_PER_GROUP
MOE_FF = 256
ALPHA = (2 * DEPTH) ** 0.25
EPS = 1e-5

LANE = 128
GLA_IN_W = 128 + 128 + 256 + LANE + 256
RET_IN_W = 4 * 256
SSD_IN_W = 512 + SSD_CONV_DIM + LANE
IN_W = GLA_IN_W + RET_IN_W + SSD_IN_W
VMEM_LIMIT = 56 * 1024 * 1024


def _cp(n_axes, vmem=VMEM_LIMIT):
    return pltpu.CompilerParams(dimension_semantics=("arbitrary",) * n_axes, vmem_limit_bytes=vmem)


def _dot(a, b):
    return jnp.dot(a, b, preferred_element_type=F32)


def _dot_nt(a, b):
    return lax.dot_general(a, b, (((1,), (1,)), ((), ())), preferred_element_type=F32)


def _dot_tn(a, b):
    return lax.dot_general(a, b, (((0,), (0,)), ((), ())), preferred_element_type=F32)


def _split3(x):
    hi = x.astype(BF)
    r = x - hi.astype(F32)
    mid = r.astype(BF)
    lo = (r - mid.astype(F32)).astype(BF)
    return hi, mid, lo


def _dot_x3(x, e):
    hi, mid, lo = _split3(x)
    return _dot(hi, e) + (_dot(mid, e) + _dot(lo, e))


def _dot_x2(x, e):
    hi = x.astype(BF)
    lo = (x - hi.astype(F32)).astype(BF)
    return _dot(hi, e) + _dot(lo, e)


def _dot_3x(e, x):
    hi, mid, lo = _split3(x)
    return _dot(e, hi) + (_dot(e, mid) + _dot(e, lo))


def _sigmoid(x):
    return 1.0 / (1.0 + jnp.exp(-x))


def _silu(x):
    return x * _sigmoid(x)


def _log_sigmoid(x):
    return jnp.minimum(x, 0.0) - jnp.log(1.0 + jnp.exp(-jnp.abs(x)))


def _softplus(x):
    return jnp.maximum(x, 0.0) + jnp.log(1.0 + jnp.exp(-jnp.abs(x)))


def _layer_norm(x, g, b):
    mu = jnp.mean(x, axis=-1, keepdims=True)
    d = x - mu
    var = jnp.mean(d * d, axis=-1, keepdims=True)
    return d * lax.rsqrt(var + EPS) * g + b


def _mod_kernel(c_ref, w_ref, b_ref, o_ref):
    s = _silu(c_ref[...]).astype(BF)
    o_ref[0] = _dot(s, w_ref[0].astype(BF)) + b_ref[0]


def _mod_call(c_all, w_ada, b_ada):
    R = c_all.shape[0]
    tn = 1536
    return pl.pallas_call(
        _mod_kernel,
        grid=(DEPTH, 6 * D_MODEL // tn),
        in_specs=[pl.BlockSpec((R, D_MODEL), lambda l, j: (0, 0)),
                  pl.BlockSpec((1, D_MODEL, tn), lambda l, j: (l, 0, j)),
                  pl.BlockSpec((1, 1, tn), lambda l, j: (l, 0, j))],
        out_specs=pl.BlockSpec((1, R, tn), lambda l, j: (l, 0, j)),
        out_shape=jax.ShapeDtypeStruct((DEPTH, R, 6 * D_MODEL), F32),
        compiler_params=_cp(2),
        name="ada_mod",
    )(c_all, w_ada, b_ada.reshape(DEPTH, 1, 6 * D_MODEL))


N_IN = 3096
N_GA = 128 + 128 + 256
N_DT = N_IN - SSD_H


def _inproj_kernel(x_ref, sc_ref, sh_ref, wt_ref, og_ref, or_ref, os_ref, w_ref):
    bB, bT, D = x_ref.shape

    @pl.when((pl.program_id(0) == 0) & (pl.program_id(1) == 0))
    def _():
        lane = lax.broadcasted_iota(jnp.int32, (1, LANE), 1)
        for j in range(N_GA // LANE):
            w_ref[:, j * LANE:(j + 1) * LANE] = wt_ref[j * LANE:(j + 1) * LANE, :].T.astype(BF)
        ga = wt_ref[N_GA:N_GA + LANE, :].T
        w_ref[:, N_GA:N_GA + LANE] = jnp.where(lane < GLA_GATE_RANK, ga, 0.0).astype(BF)
        src0, dst0 = N_GA + GLA_GATE_RANK, N_GA + LANE
        for j in range((N_DT - src0) // LANE):
            w_ref[:, dst0 + j * LANE:dst0 + (j + 1) * LANE] = \
                wt_ref[src0 + j * LANE:src0 + (j + 1) * LANE, :].T.astype(BF)
        dt = pltpu.roll(wt_ref[N_IN - LANE:N_IN, :].T, SSD_H, 1)
        w_ref[:, IN_W - LANE:IN_W] = jnp.where(lane < SSD_H, dt, 0.0).astype(BF)

    h = x_ref[...] * (1.0 + sc_ref[...]) + sh_ref[...]
    hb = h.reshape(bB * bT, D).astype(BF)
    og_ref[...] = _dot(hb, w_ref[:, 0:GLA_IN_W])
    or_ref[...] = _dot(hb, w_ref[:, GLA_IN_W:GLA_IN_W + RET_IN_W])
    os_ref[...] = _dot(hb, w_ref[:, GLA_IN_W + RET_IN_W:IN_W])


def _tok_tiles(B, T):
    if T >= 512:
        return 1, 512
    return 512 // T, T


def _inproj_call(x3, sc, sh, wt, l):
    B, T, D = x3.shape
    bB, bT = _tok_tiles(B, T)
    nT = T // bT
    R = bB * bT
    N = B * T
    xmap = lambda i, j: (i, j, 0)
    mmap = lambda i, j: (i, 0, 0)
    omap = lambda i, j: (i * nT + j, 0)
    return pl.pallas_call(
        _inproj_kernel,
        grid=(B // bB, nT),
        in_specs=[pl.BlockSpec((bB, bT, D), xmap),
                  pl.BlockSpec((bB, 1, D), mmap),
                  pl.BlockSpec((bB, 1, D), mmap),
                  _resident_layer(wt.shape, l)],
        out_specs=[pl.BlockSpec((R, GLA_IN_W), omap),
                   pl.BlockSpec((R, RET_IN_W), omap),
                   pl.BlockSpec((R, SSD_IN_W), omap)],
        out_shape=[jax.ShapeDtypeStruct((N, GLA_IN_W), F32),
                   jax.ShapeDtypeStruct((N, RET_IN_W), F32),
                   jax.ShapeDtypeStruct((N, SSD_IN_W), F32)],
        scratch_shapes=[pltpu.VMEM((D, IN_W), BF)],
        compiler_params=_cp(2),
        name="in_proj",
    )(x3, sc, sh, wt)


def _head_block_mask(rows_per, cols_per, n):
    r = np.arange(rows_per * n)[:, None] // rows_per
    c = np.arange(cols_per * n)[None, :] // cols_per
    return (r == c).astype(np.float32)


def _block_tril(n, c):
    i = np.arange(n)[:, None]
    j = np.arange(n)[None, :]
    return ((i // c == j // c) & (j <= i)).astype(np.float32)


def _gla_front(x_ref, wg_ref, bg_ref, L_ref):
    q = x_ref[:, 0:128] * (GLA_DK ** -0.5)
    k = x_ref[:, 128:256]
    v = x_ref[:, 256:512]
    ga = x_ref[:, 512:640]
    r = x_ref[:, 640:896]
    gate = _dot(ga.astype(BF), wg_ref[...]) + bg_ref[...]
    la = _log_sigmoid(gate) * (1.0 / GLA_GATE_TEMP)
    g = _dot_3x(L_ref[...], la)
    return q, k, v, r, g


def _gla_intra(q, g, kp_ref, gp_ref, vp_ref, E_ref, c):
    TT = q.shape[0]
    PAD = kp_ref.shape[0] - TT
    pos = lax.broadcasted_iota(jnp.int32, (TT, 1), 0) & (c - 1)
    o = jnp.zeros((TT, 2 * LANE), F32)
    for s in range(c):
        ks = kp_ref[pl.ds(PAD - s, TT), :]
        gs = gp_ref[pl.ds(PAD - s, TT), :]
        vs = vp_ref[pl.ds(PAD - s, TT), :]
        w = jnp.where(pos >= s, q * ks * jnp.exp(g - gs), 0.0)
        o = o + _dot(w.astype(BF), E_ref[...]) * vs
    return o


def _gla_norm_gate(o, r, nw_ref, EA_ref):
    ms = _dot_x3(o * o, EA_ref[...])
    return o * lax.rsqrt(ms + EPS) * nw_ref[...] * _silu(r)


def _gla_prompt_kernel(x_ref, wg_ref, bg_ref, nw_ref, L_ref, E_ref, EA_ref, M_ref,
                       o_ref, sfin_ref, st_ref, kp_ref, gp_ref, vp_ref, oi_ref, u_ref, sb_ref, *, c):
    TT = x_ref.shape[0]
    nc = TT // c
    PAD = kp_ref.shape[0] - TT
    t = pl.program_id(1)

    @pl.when(t == 0)
    def _():
        st_ref[...] = jnp.zeros_like(st_ref)

    q, k, v, r, g = _gla_front(x_ref, wg_ref, bg_ref, L_ref)
    kp_ref[0:PAD, :] = jnp.zeros((PAD, LANE), F32)
    gp_ref[0:PAD, :] = jnp.zeros((PAD, LANE), F32)
    vp_ref[0:PAD, :] = jnp.zeros((PAD, 2 * LANE), F32)
    kp_ref[PAD:PAD + TT, :] = k
    gp_ref[PAD:PAD + TT, :] = g
    vp_ref[PAD:PAD + TT, :] = v
    o = _gla_intra(q, g, kp_ref, gp_ref, vp_ref, E_ref, c)

    M = M_ref[...]
    gl_all = gp_ref[pl.ds(PAD + c - 1, nc, stride=c), :]
    for n in range(nc):
        lo = n * c
        ke = (k[lo:lo + c, :] * jnp.exp(gl_all[n:n + 1, :] - g[lo:lo + c, :])).astype(BF)
        u_ref[n] = _dot_tn(ke, v[lo:lo + c, :].astype(BF)) * M
    a_cols = jnp.concatenate([jnp.exp(gl_all), jnp.zeros((LANE - nc, LANE), F32)], axis=0).T
    S = st_ref[...]
    for n in range(nc):
        sb_ref[n] = S.astype(BF)
        S = a_cols[:, n:n + 1] * S + u_ref[n]
    st_ref[...] = S
    qe = (q * jnp.exp(g)).astype(BF)
    for n in range(nc):
        lo = n * c
        oi_ref[lo:lo + c, :] = _dot(qe[lo:lo + c, :], sb_ref[n])
    o = o + oi_ref[...]
    o_ref[...] = _gla_norm_gate(o, r, nw_ref, EA_ref).astype(o_ref.dtype)

    @pl.when(t == pl.num_programs(1) - 1)
    def _():
        for h in range(GLA_H):
            sfin_ref[0, h] = S[h * GLA_DK:(h + 1) * GLA_DK, h * GLA_DV:(h + 1) * GLA_DV]


def _gla_tables(TT, c):
    L = jnp.asarray(_block_tril(TT, c), BF)
    E = jnp.asarray(_head_block_mask(GLA_DK, GLA_DV, GLA_H), BF)
    EA = jnp.asarray(_head_block_mask(GLA_DV, GLA_DV, GLA_H) / GLA_DV, BF)
    M = jnp.asarray(_head_block_mask(GLA_DK, GLA_DV, GLA_H), F32)
    return L, E, EA, M


def _gla_params(w_gate, b_gate, norm_w):
    wg = jnp.zeros((LANE, GLA_H * GLA_DK), F32).at[:GLA_GATE_RANK].set(w_gate).astype(BF)
    return wg, b_gate.reshape(1, -1), norm_w.reshape(1, -1)


def _const(shape):
    return pl.BlockSpec(shape, lambda *_: (0,) * len(shape))


def _gla_prompt_call(gin, B, T, w_gate, b_gate, norm_w):
    TT, c = 256, GLA_CHUNK
    nT = T // TT
    L, E, EA, M = _gla_tables(TT, c)
    wg, bg, nw = _gla_params(w_gate, b_gate, norm_w)
    PAD = 16
    return pl.pallas_call(
        functools.partial(_gla_prompt_kernel, c=c),
        grid=(B, nT),
        in_specs=[pl.BlockSpec((TT, GLA_IN_W), lambda b, t: (b * nT + t, 0)),
                  _const(wg.shape), _const(bg.shape), _const(nw.shape),
                  _const(L.shape), _const(E.shape), _const(EA.shape), _const(M.shape)],
        out_specs=[pl.BlockSpec((TT, GLA_WIDTH), lambda b, t: (b * nT + t, 0)),
                   pl.BlockSpec((1, GLA_H, GLA_DK, GLA_DV), lambda b, t: (b, 0, 0, 0))],
        out_shape=[jax.ShapeDtypeStruct((B * T, GLA_WIDTH), BF),
                   jax.ShapeDtypeStruct((B, GLA_H, GLA_DK, GLA_DV), F32)],
        scratch_shapes=[pltpu.VMEM((GLA_H * GLA_DK, GLA_H * GLA_DV), F32),
                        pltpu.VMEM((TT + PAD, LANE), F32),
                        pltpu.VMEM((TT + PAD, LANE), F32),
                        pltpu.VMEM((TT + PAD, 2 * LANE), F32),
                        pltpu.VMEM((TT, 2 * LANE), F32),
                        pltpu.VMEM((TT // c, GLA_H * GLA_DK, GLA_H * GLA_DV), F32),
                        pltpu.VMEM((TT // c, GLA_H * GLA_DK, GLA_H * GLA_DV), BF)],
        compiler_params=_cp(2),
        name="gla_prompt",
    )(gin, wg, bg, nw, L, E, EA, M)


def _rope(x, cos, sin_signed):
    lane = lax.broadcasted_iota(jnp.int32, (1, LANE), 1)
    first_half = (lane & (RET_DK - 1)) < RET_DK // 2
    out = []
    for p in range(2):
        xs = x[:, p * LANE:(p + 1) * LANE]
        up = pltpu.roll(xs, LANE - RET_DK // 2, 1)
        dn = pltpu.roll(xs, RET_DK // 2, 1)
        out.append(xs * cos + jnp.where(first_half, up, dn) * sin_signed)
    return jnp.concatenate(out, axis=1)


def _ret_front(x_ref, cos_ref, sin_ref):
    q = _rope(x_ref[:, 0:256], cos_ref[...], sin_ref[...])
    k = _rope(x_ref[:, 256:512], cos_ref[...], sin_ref[...]) * (RET_DK ** -0.5)
    v = x_ref[:, 512:768]
    rg = x_ref[:, 768:1024]
    return q, k, v, rg


def _ret_intra(q, k, v, D_ref):
    lane = lax.broadcasted_iota(jnp.int32, (1, RET_WIDTH), 1)
    kb = k.astype(BF)
    o = jnp.zeros(q.shape, F32)
    for h in range(RET_H):
        hm = (lane // RET_DK) == h
        s = _dot_nt(jnp.where(hm, q, 0.0).astype(BF), kb)
        p = (s * D_ref[h]).astype(BF)
        o = o + _dot(p, jnp.where(hm, v, 0.0).astype(BF))
    return o


def _ret_norm_gate(o, rg, nw_ref, EA_ref):
    mu = _dot_x3(o, EA_ref[...])
    d = o - mu
    var = _dot_x3(d * d, EA_ref[...])
    return d * lax.rsqrt(var + EPS) * nw_ref[...] * _silu(rg)


def _ret_prompt_kernel(x_ref, cos_ref, sin_ref, D_ref, rd_ref, kd_ref, G_ref, M_ref, EA_ref, nw_ref,
                       o_ref, sfin_ref, st_ref):
    t = pl.program_id(1)

    @pl.when(t == 0)
    def _():
        st_ref[...] = jnp.zeros_like(st_ref)

    q, k, v, rg = _ret_front(x_ref, cos_ref, sin_ref)
    o = _ret_intra(q, k, v, D_ref)
    S = st_ref[...]
    o = o + _dot((q * rd_ref[...]).astype(BF), S.astype(BF))
    u = _dot_tn((k * kd_ref[...]).astype(BF), v.astype(BF))
    S = S * G_ref[...] + u * M_ref[...]
    st_ref[...] = S
    o_ref[...] = _ret_norm_gate(o, rg, nw_ref, EA_ref).astype(o_ref.dtype)

    @pl.when(t == pl.num_programs(1) - 1)
    def _():
        for h in range(RET_H):
            sfin_ref[0, h] = S[h * RET_DK:(h + 1) * RET_DK, h * RET_DV:(h + 1) * RET_DV]


def _rope_tables(pos):
    half = RET_DK // 2
    inv = ROPE_BASE ** (-jnp.arange(half, dtype=F32) / half)
    ang = pos.astype(F32)[:, None] * inv[None, :]
    cos, sin = jnp.cos(ang), jnp.sin(ang)
    return jnp.tile(jnp.concatenate([cos, cos], 1), (1, 2)), jnp.tile(jnp.concatenate([-sin, sin], 1), (1, 2))


def _ret_log_gamma():
    return np.log(1.0 - 2.0 ** (-5.0 - np.arange(RET_H, dtype=np.float64)))


def _ret_prompt_call(rin, B, T, norm_w):
    TT = 256
    nT = T // TT
    cos, sin = _rope_tables(jnp.arange(T, dtype=jnp.int32))
    lg = _ret_log_gamma()
    i = np.arange(TT)
    dec = np.exp(lg[:, None, None] * (i[:, None] - i[None, :])[None]) * (i[:, None] >= i[None, :])[None]
    Dm = jnp.asarray(dec, F32)
    rd = jnp.asarray(np.repeat(np.exp(lg[None, :] * (i[:, None] + 1)), RET_DK, 1), F32)
    kd = jnp.asarray(np.repeat(np.exp(lg[None, :] * (TT - 1 - i[:, None])), RET_DK, 1), F32)
    M = _head_block_mask(RET_DK, RET_DV, RET_H)
    G = jnp.asarray(M * np.repeat(np.exp(lg * TT), RET_DK)[:, None], F32)
    M = jnp.asarray(M, F32)
    EA = jnp.asarray(_head_block_mask(RET_DV, RET_DV, RET_H) / RET_DV, BF)
    nw = norm_w.reshape(1, -1)
    return pl.pallas_call(
        _ret_prompt_kernel,
        grid=(B, nT),
        in_specs=[pl.BlockSpec((TT, RET_IN_W), lambda b, t: (b * nT + t, 0)),
                  pl.BlockSpec((TT, LANE), lambda b, t: (t, 0)),
                  pl.BlockSpec((TT, LANE), lambda b, t: (t, 0)),
                  _const(Dm.shape), _const(rd.shape), _const(kd.shape), _const(G.shape), _const(M.shape),
                  _const(EA.shape), _const(nw.shape)],
        out_specs=[pl.BlockSpec((TT, RET_WIDTH), lambda b, t: (b * nT + t, 0)),
                   pl.BlockSpec((1, RET_H, RET_DK, RET_DV), lambda b, t: (b, 0, 0, 0))],
        out_shape=[jax.ShapeDtypeStruct((B * T, RET_WIDTH), BF),
                   jax.ShapeDtypeStruct((B, RET_H, RET_DK, RET_DV), F32)],
        scratch_shapes=[pltpu.VMEM((RET_H * RET_DK, RET_H * RET_DV), F32)],
        compiler_params=_cp(2),
        name="ret_prompt",
    )(rin, cos, sin, Dm, rd, kd, G, M, EA, nw)


def _ssd_conv(xp_ref, cw_ref, cb_ref, TT):
    acc = cb_ref[...] + cw_ref[SSD_CONV_W - 1:SSD_CONV_W, :] * xp_ref[pl.ds(8, TT), :]
    for i in range(SSD_CONV_W - 1):
        acc = acc + cw_ref[i:i + 1, :] * xp_ref[pl.ds(8 - (SSD_CONV_W - 1) + i, TT), :]
    return acc


def _ssd_intra(xs, bm, cm, g, dt, Mk_ref):
    TT = xs.shape[0]
    rT = (g - jnp.log(dt)).T
    lane = lax.broadcasted_iota(jnp.int32, (1, LANE), 1)
    lane2 = lax.broadcasted_iota(jnp.int32, (1, 2 * LANE), 1)
    causal = Mk_ref[...] > 0.0
    bmb = bm.astype(BF)
    zero = jnp.zeros((), BF)
    o_parts = []
    for grp in range(SSD_G):
        cb = _dot_nt(jnp.where((lane // SSD_N) == grp, cm, 0.0).astype(BF), bmb).astype(BF)
        xg = xs[:, grp * 2 * LANE:(grp + 1) * 2 * LANE].astype(BF)
        og = jnp.zeros((TT, 2 * LANE), F32)
        for h4 in range(SSD_H // SSD_G):
            h = grp * (SSD_H // SSD_G) + h4
            dec = jnp.where(causal, jnp.exp(g[:, h:h + 1] - rT[h:h + 1, :]), 0.0)
            p = cb * dec.astype(BF)
            og = og + _dot(p, jnp.where((lane2 // SSD_P) == h4, xg, zero))
        o_parts.append(og)
    return jnp.concatenate(o_parts, axis=1)


def _ssd_prompt_kernel(x_ref, cw_ref, cb_ref, dtb_ref, alog_ref, dexp_ref, nw_ref, L_ref, Mk_ref, Eexp_ref, M2_ref,
                       o_ref, sfin_ref, cfin_ref, st_ref, xp_ref):
    TT = x_ref.shape[0]
    t = pl.program_id(1)

    @pl.when(t == 0)
    def _():
        st_ref[...] = jnp.zeros_like(st_ref)
        xp_ref[0:8, :] = jnp.zeros((8, SSD_CONV_DIM), F32)

    z = x_ref[:, 0:SSD_WIDTH]
    xp_ref[8:8 + TT, :] = x_ref[:, SSD_WIDTH:SSD_WIDTH + SSD_CONV_DIM]
    sdt = x_ref[:, SSD_WIDTH + SSD_CONV_DIM:SSD_IN_W]
    xbc = _silu(_ssd_conv(xp_ref, cw_ref, cb_ref, TT))
    tail = xp_ref[TT:TT + 8, :]
    xp_ref[0:8, :] = tail
    xs = xbc[:, 0:SSD_WIDTH]
    bm = xbc[:, SSD_WIDTH:SSD_WIDTH + LANE]
    cm = xbc[:, SSD_WIDTH + LANE:SSD_CONV_DIM]

    dt = _softplus(sdt + dtb_ref[...])
    la = dt * (-jnp.exp(alog_ref[...]))
    g = _dot_3x(L_ref[...], la)
    gl = g[TT - 1:TT, :]
    Eexp = Eexp_ref[...]
    eg_x = _dot_x2(jnp.exp(g), Eexp)
    cw_x = _dot_x2(dt * jnp.exp(gl - g), Eexp)
    egl_x = _dot_x2(jnp.exp(gl), Eexp)

    o = _ssd_intra(xs, bm, cm, g, dt, Mk_ref)

    S = st_ref[...]
    o = o + eg_x * _dot(cm.astype(BF), S.astype(BF))
    u = _dot_tn(bm.astype(BF), (xs * cw_x).astype(BF))
    S = S * egl_x + u * M2_ref[...]
    st_ref[...] = S

    y = (o + dexp_ref[...] * xs) * _silu(z)
    ms = jnp.mean(y * y, axis=-1, keepdims=True)
    o_ref[...] = (y * lax.rsqrt(ms + EPS) * nw_ref[...]).astype(o_ref.dtype)

    @pl.when(t == pl.num_programs(1) - 1)
    def _():
        for h in range(SSD_H):
            gi = h // (SSD_H // SSD_G)
            sfin_ref[0, h] = S[gi * SSD_N:(gi + 1) * SSD_N, h * SSD_P:(h + 1) * SSD_P]
        cfin_ref[0] = tail[8 - (SSD_CONV_W - 1):8, :]


def _pad_lanes(v, n=LANE):
    v = v.reshape(1, -1)
    return jnp.zeros((1, n), F32).at[:, :v.shape[1]].set(v)


def _ssd_tables(TT, c):
    L = jnp.asarray(_block_tril(TT, c), BF)
    Mk = jnp.asarray(_block_tril(TT, c), F32)
    e = np.zeros((LANE, SSD_WIDTH), np.float32)
    for h in range(SSD_H):
        e[h, h * SSD_P:(h + 1) * SSD_P] = 1.0
    M2 = np.zeros((SSD_G * SSD_N, SSD_WIDTH), np.float32)
    for h in range(SSD_H):
        gi = h // (SSD_H // SSD_G)
        M2[gi * SSD_N:(gi + 1) * SSD_N, h * SSD_P:(h + 1) * SSD_P] = 1.0
    return L, Mk, jnp.asarray(e, BF), jnp.asarray(M2, F32)


def _ssd_params(conv_w, conv_b, dt_bias, a_log, d, norm_w):
    return (conv_w, conv_b.reshape(1, -1), _pad_lanes(dt_bias), _pad_lanes(a_log),
            jnp.repeat(d, SSD_P).reshape(1, -1), norm_w.reshape(1, -1))


def _ssd_prompt_call(sin_, B, T, conv_w, conv_b, dt_bias, a_log, d, norm_w):
    TT = 256
    nT = T // TT
    L, Mk, Eexp, M2 = _ssd_tables(TT, TT)
    prm = _ssd_params(conv_w, conv_b, dt_bias, a_log, d, norm_w)
    return pl.pallas_call(
        _ssd_prompt_kernel,
        grid=(B, nT),
        in_specs=[pl.BlockSpec((TT, SSD_IN_W), lambda b, t: (b * nT + t, 0))]
                 + [_const(p.shape) for p in prm]
                 + [_const(L.shape), _const(Mk.shape), _const(Eexp.shape), _const(M2.shape)],
        out_specs=[pl.BlockSpec((TT, SSD_WIDTH), lambda b, t: (b * nT + t, 0)),
                   pl.BlockSpec((1, SSD_H, SSD_N, SSD_P), lambda b, t: (b, 0, 0, 0)),
                   pl.BlockSpec((1, SSD_CONV_W - 1, SSD_CONV_DIM), lambda b, t: (b, 0, 0))],
        out_shape=[jax.ShapeDtypeStruct((B * T, SSD_WIDTH), BF),
                   jax.ShapeDtypeStruct((B, SSD_H, SSD_N, SSD_P), F32),
                   jax.ShapeDtypeStruct((B, SSD_CONV_W - 1, SSD_CONV_DIM), F32)],
        scratch_shapes=[pltpu.VMEM((SSD_G * SSD_N, SSD_WIDTH), F32),
                        pltpu.VMEM((TT + 8, SSD_CONV_DIM), F32)],
        compiler_params=_cp(2),
        name="ssd_prompt",
    )(sin_, *prm, L, Mk, Eexp, M2)


SEQ_TILE = 8


def _tile_lanes(n_rep, width):
    return np.tile(np.eye(width, dtype=np.float32), (1, n_rep))


def _fold_head_blocks(ubd):
    a = ubd[:, 0:LANE] + ubd[:, LANE:2 * LANE]
    return (a + pltpu.roll(a, LANE // 2, 1))[:, 0:LANE // 2]


def _col_bcast(row8, ones_ref):
    first = lax.broadcasted_iota(jnp.int32, (8, 1), 0) == 0
    hi, mid, lo = _split3(jnp.where(first, row8, 0.0))
    ones = ones_ref[...]
    return _dot_tn(hi, ones) + (_dot_tn(mid, ones) + _dot_tn(lo, ones))


def _gla_sample_kernel(x_ref, s0_ref, wg_ref, bg_ref, nw_ref, L_ref, E_ref, EA_ref, M_ref, T4_ref, ones_ref,
                       o_ref, sn_ref, kp_ref, gp_ref, vp_ref, oi_ref, *, c):
    TT = x_ref.shape[0]
    PAD = kp_ref.shape[0] - TT
    q, k, v, r, g = _gla_front(x_ref, wg_ref, bg_ref, L_ref)
    kp_ref[0:PAD, :] = jnp.zeros((PAD, LANE), F32)
    gp_ref[0:PAD, :] = jnp.zeros((PAD, LANE), F32)
    vp_ref[0:PAD, :] = jnp.zeros((PAD, 2 * LANE), F32)
    kp_ref[PAD:PAD + TT, :] = k
    gp_ref[PAD:PAD + TT, :] = g
    vp_ref[PAD:PAD + TT, :] = v
    o = _gla_intra(q, g, kp_ref, gp_ref, vp_ref, E_ref, c)
    qe = (q * jnp.exp(g)).astype(BF)
    M = M_ref[...]
    for s in range(TT // c):
        lo = s * c
        S0 = s0_ref[s].reshape(GLA_H * GLA_DK, GLA_DV)
        Sbd = (_dot(S0.astype(BF), T4_ref[...]) * M).astype(BF)
        oi_ref[lo:lo + c, :] = _dot(qe[lo:lo + c, :], Sbd)
        gl = g[lo + c - 1:lo + c, :]
        ke = (k[lo:lo + c, :] * jnp.exp(gl - g[lo:lo + c, :])).astype(BF)
        u = _fold_head_blocks(_dot_tn(ke, v[lo:lo + c, :].astype(BF)) * M)
        acol = _col_bcast(jnp.broadcast_to(jnp.exp(gl), (8, LANE)), ones_ref)
        sn_ref[s] = (acol * S0 + u).reshape(GLA_H, GLA_DK, GLA_DV)
    o = o + oi_ref[...]
    o_ref[...] = _gla_norm_gate(o, r, nw_ref, EA_ref).astype(o_ref.dtype)


def _gla_sample_call(gin, s0, B, T, w_gate, b_gate, norm_w):
    TT = SEQ_TILE * T
    L, E, EA, _ = _gla_tables(TT, T)
    M = jnp.asarray(_head_block_mask(GLA_DK, GLA_DV, GLA_H), F32)
    T4 = jnp.asarray(_tile_lanes(GLA_H, GLA_DV), BF)
    ones = jnp.ones((8, GLA_DV), BF)
    wg, bg, nw = _gla_params(w_gate, b_gate, norm_w)
    PAD = 8
    sspec = pl.BlockSpec((SEQ_TILE, GLA_H, GLA_DK, GLA_DV), lambda i: (i, 0, 0, 0))
    return pl.pallas_call(
        functools.partial(_gla_sample_kernel, c=T),
        grid=(B // SEQ_TILE,),
        in_specs=[pl.BlockSpec((TT, GLA_IN_W), lambda i: (i, 0)), sspec,
                  _const(wg.shape), _const(bg.shape), _const(nw.shape),
                  _const(L.shape), _const(E.shape), _const(EA.shape), _const(M.shape), _const(T4.shape),
                  _const(ones.shape)],
        out_specs=[pl.BlockSpec((TT, GLA_WIDTH), lambda i: (i, 0)), sspec],
        out_shape=[jax.ShapeDtypeStruct((B * T, GLA_WIDTH), BF),
                   jax.ShapeDtypeStruct((B, GLA_H, GLA_DK, GLA_DV), F32)],
        scratch_shapes=[pltpu.VMEM((TT + PAD, LANE), F32),
                        pltpu.VMEM((TT + PAD, LANE), F32),
                        pltpu.VMEM((TT + PAD, 2 * LANE), F32),
                        pltpu.VMEM((TT, 2 * LANE), F32)],
        compiler_params=_cp(1),
        name="gla_sample",
    )(gin, s0, wg, bg, nw, L, E, EA, M, T4, ones)


def _ret_sample_kernel(x_ref, s0_ref, cos_ref, sin_ref, D_ref, rd_ref, kd_ref, G_ref, M_ref, EA_ref, nw_ref, T4_ref,
                       o_ref, sn_ref, oi_ref, *, c):
    TT = x_ref.shape[0]
    q, k, v, rg = _ret_front(x_ref, cos_ref, sin_ref)
    o = _ret_intra(q, k, v, D_ref)
    qd = (q * rd_ref[...]).astype(BF)
    kd = (k * kd_ref[...]).astype(BF)
    vb = v.astype(BF)
    M = M_ref[...]
    for s in range(TT // c):
        lo = s * c
        S0 = s0_ref[s].reshape(RET_H * RET_DK, RET_DV)
        Sbd = (_dot(S0.astype(BF), T4_ref[...]) * M).astype(BF)
        oi_ref[lo:lo + c, :] = _dot(qd[lo:lo + c, :], Sbd)
        u = _fold_head_blocks(_dot_tn(kd[lo:lo + c, :], vb[lo:lo + c, :]) * M)
        sn_ref[s] = (G_ref[...] * S0 + u).reshape(RET_H, RET_DK, RET_DV)
    o = o + oi_ref[...]
    o_ref[...] = _ret_norm_gate(o, rg, nw_ref, EA_ref).astype(o_ref.dtype)


def _ret_sample_call(rin, s0, B, T, norm_w):
    TT = SEQ_TILE * T
    cos, sin = _rope_tables(PAST_LEN + jnp.arange(T, dtype=jnp.int32))
    cos, sin = jnp.tile(cos, (SEQ_TILE, 1)), jnp.tile(sin, (SEQ_TILE, 1))
    lg = _ret_log_gamma()
    i = np.arange(TT)
    same = (i[:, None] // T == i[None, :] // T) & (i[:, None] >= i[None, :])
    Dm = jnp.asarray(np.exp(lg[:, None, None] * (i[:, None] - i[None, :])[None]) * same[None], F32)
    tt = i % T
    rd = jnp.asarray(np.repeat(np.exp(lg[None, :] * (tt[:, None] + 1)), RET_DK, 1), F32)
    kd = jnp.asarray(np.repeat(np.exp(lg[None, :] * (T - 1 - tt[:, None])), RET_DK, 1), F32)
    G = jnp.asarray(np.repeat(np.repeat(np.exp(lg * T), RET_DK)[:, None], RET_DV, 1), F32)
    M = jnp.asarray(_head_block_mask(RET_DK, RET_DV, RET_H), F32)
    EA = jnp.asarray(_head_block_mask(RET_DV, RET_DV, RET_H) / RET_DV, BF)
    T4 = jnp.asarray(_tile_lanes(RET_H, RET_DV), BF)
    nw = norm_w.reshape(1, -1)
    sspec = pl.BlockSpec((SEQ_TILE, RET_H, RET_DK, RET_DV), lambda i: (i, 0, 0, 0))
    consts = (cos, sin, Dm, rd, kd, G, M, EA, nw, T4)
    return pl.pallas_call(
        functools.partial(_ret_sample_kernel, c=T),
        grid=(B // SEQ_TILE,),
        in_specs=[pl.BlockSpec((TT, RET_IN_W), lambda i: (i, 0)), sspec] + [_const(a.shape) for a in consts],
        out_specs=[pl.BlockSpec((TT, RET_WIDTH), lambda i: (i, 0)), sspec],
        out_shape=[jax.ShapeDtypeStruct((B * T, RET_WIDTH), BF),
                   jax.ShapeDtypeStruct((B, RET_H, RET_DK, RET_DV), F32)],
        scratch_shapes=[pltpu.VMEM((TT, RET_WIDTH), F32)],
        compiler_params=_cp(1),
        name="ret_sample",
    )(rin, s0, *consts)


def _ssd_sample_kernel(x_ref, c0_ref, s0_ref, cw_ref, cb_ref, dtb_ref, alog_ref, dexp_ref, nw_ref,
                       L_ref, Mk_ref, Eexp_ref, Bl_ref, R2_ref, T8_ref, T8T_ref, M8_ref, ones_ref,
                       o_ref, sn_ref, cn_ref, xp_ref, oi_ref, *, c):
    TT = x_ref.shape[0]
    ns = TT // c
    RP = 2 * c
    xp_ref[...] = jnp.zeros_like(xp_ref)
    z = x_ref[:, 0:SSD_WIDTH]
    sdt = x_ref[:, SSD_WIDTH + SSD_CONV_DIM:SSD_IN_W]
    for s in range(ns):
        base = 8 + s * RP
        xp_ref[base + c - (SSD_CONV_W - 1):base + c, :] = c0_ref[s]
        xp_ref[base + c:base + RP, :] = x_ref[s * c:(s + 1) * c, SSD_WIDTH:SSD_WIDTH + SSD_CONV_DIM]
    conv = _ssd_conv(xp_ref, cw_ref, cb_ref, ns * RP)
    xbc = _silu(conv.reshape(ns, RP, SSD_CONV_DIM)[:, c:RP, :].reshape(TT, SSD_CONV_DIM))
    for s in range(ns):
        base = 8 + s * RP
        cn_ref[s] = xp_ref[base + RP - (SSD_CONV_W - 1):base + RP, :]
    xs = xbc[:, 0:SSD_WIDTH]
    bm = xbc[:, SSD_WIDTH:SSD_WIDTH + LANE]
    cm = xbc[:, SSD_WIDTH + LANE:SSD_CONV_DIM]

    dt = _softplus(sdt + dtb_ref[...])
    la = dt * (-jnp.exp(alog_ref[...]))
    g = _dot_3x(L_ref[...], la)
    gl = _dot_3x(Bl_ref[...], g)
    Eexp = Eexp_ref[...]
    eg_x = _dot_x2(jnp.exp(g), Eexp)
    cw_x = _dot_x2(dt * jnp.exp(gl - g), Eexp)
    egl_x = _dot_x2(jnp.exp(gl), Eexp)
    o = _ssd_intra(xs, bm, cm, g, dt, Mk_ref)

    Cx = _dot(cm.astype(BF), R2_ref[...])
    Bx = _dot(bm.astype(BF), R2_ref[...])
    Xw = xs * cw_x
    M8 = M8_ref[...]
    nh = SSD_H

    def rows_by_head(a):
        return jnp.concatenate([a] * nh, axis=0) * M8

    for s in range(ns):
        lo = s * c
        S0 = s0_ref[s].reshape(SSD_H * SSD_N, SSD_P)
        oi = _dot(rows_by_head(Cx[lo:lo + c, :]).astype(BF), S0.astype(BF))
        oix = _dot_x2(oi, T8_ref[...]) * M8
        acc = oix[0:c, :]
        for h in range(1, nh):
            acc = acc + oix[h * c:(h + 1) * c, :]
        oi_ref[lo:lo + c, :] = acc
        Xst = _dot(rows_by_head(Xw[lo:lo + c, :]).astype(BF), T8T_ref[...])
        u = _dot_tn(rows_by_head(Bx[lo:lo + c, :]).astype(BF), Xst.astype(BF))
        acol = _col_bcast(egl_x[lo:lo + c, :], ones_ref)
        sn_ref[s] = (acol * S0 + u).reshape(SSD_H, SSD_N, SSD_P)

    o = o + eg_x * oi_ref[...]
    y = (o + dexp_ref[...] * xs) * _silu(z)
    ms = jnp.mean(y * y, axis=-1, keepdims=True)
    o_ref[...] = (y * lax.rsqrt(ms + EPS) * nw_ref[...]).astype(o_ref.dtype)


def _ssd_sample_call(sin_, c0, s0, B, T, conv_w, conv_b, dt_bias, a_log, d, norm_w):
    TT = SEQ_TILE * T
    L, Mk, Eexp, _ = _ssd_tables(TT, T)
    i = np.arange(TT)
    Bl = jnp.asarray((i[None, :] == (i[:, None] // T) * T + T - 1).astype(np.float32), BF)
    hpg = SSD_H // SSD_G
    R2 = np.zeros((LANE, SSD_H * SSD_N), np.float32)
    for h in range(SSD_H):
        R2[(h // hpg) * SSD_N:(h // hpg + 1) * SSD_N, h * SSD_N:(h + 1) * SSD_N] = np.eye(SSD_N)
    T8 = _tile_lanes(SSD_H, SSD_P)
    M8 = _head_block_mask(T, SSD_P, SSD_H)
    tabs = (L, Mk, Eexp, Bl, jnp.asarray(R2, BF), jnp.asarray(T8, BF), jnp.asarray(T8.T, BF), jnp.asarray(M8, F32),
            jnp.ones((8, SSD_P), BF))
    prm = _ssd_params(conv_w, conv_b, dt_bias, a_log, d, norm_w)
    sspec = pl.BlockSpec((SEQ_TILE, SSD_H, SSD_N, SSD_P), lambda i: (i, 0, 0, 0))
    cspec = pl.BlockSpec((SEQ_TILE, SSD_CONV_W - 1, SSD_CONV_DIM), lambda i: (i, 0, 0))
    return pl.pallas_call(
        functools.partial(_ssd_sample_kernel, c=T),
        grid=(B // SEQ_TILE,),
        in_specs=[pl.BlockSpec((TT, SSD_IN_W), lambda i: (i, 0)), cspec, sspec]
                 + [_const(p.shape) for p in prm] + [_const(a.shape) for a in tabs],
        out_specs=[pl.BlockSpec((TT, SSD_WIDTH), lambda i: (i, 0)), sspec, cspec],
        out_shape=[jax.ShapeDtypeStruct((B * T, SSD_WIDTH), BF),
                   jax.ShapeDtypeStruct((B, SSD_H, SSD_N, SSD_P), F32),
                   jax.ShapeDtypeStruct((B, SSD_CONV_W - 1, SSD_CONV_DIM), F32)],
        scratch_shapes=[pltpu.VMEM((8 + SEQ_TILE * 2 * T, SSD_CONV_DIM), F32),
                        pltpu.VMEM((TT, SSD_WIDTH), F32)],
        compiler_params=_cp(1),
        name="ssd_sample",
    )(sin_, c0, s0, *prm, *tabs)


def _inproj_t_kernel(x_ref, sc_ref, sh_ref, wt_ref, og_ref, or_ref, os_ref, w_ref):
    nt, nb, D = x_ref.shape

    @pl.when(pl.program_id(0) == 0)
    def _():
        for src, dst, n in ((0, 0, N_GA + GLA_GATE_RANK), (N_GA + GLA_GATE_RANK, N_GA + LANE, N_DT - N_GA - GLA_GATE_RANK)):
            for r in range(0, n, 512):
                m = min(512, n - r)
                w_ref[dst + r:dst + r + m, :] = wt_ref[src + r:src + r + m, :].astype(BF)
        w_ref[N_GA + GLA_GATE_RANK:N_GA + LANE, :] = jnp.zeros((LANE - GLA_GATE_RANK, D), BF)
        tail = jnp.concatenate([wt_ref[N_DT:N_IN, :], jnp.zeros((LANE - SSD_H, D), F32)], axis=0)
        w_ref[IN_W - LANE:IN_W, :] = tail.astype(BF)

    h = x_ref[...] * (1.0 + sc_ref[...]) + sh_ref[...]
    for t in range(nt):
        ht = h[t].astype(BF)
        cols = slice(t * nb, (t + 1) * nb)
        og_ref[:, cols] = _dot_nt(w_ref[0:GLA_IN_W, :], ht)
        or_ref[:, cols] = _dot_nt(w_ref[GLA_IN_W:GLA_IN_W + RET_IN_W, :], ht)
        os_ref[:, cols] = _dot_nt(w_ref[GLA_IN_W + RET_IN_W:IN_W, :], ht)


def _inproj_t_call(xt, sc, sh, wt, l):
    T, B, D = xt.shape
    nt = 4
    cmap = lambda i: (0, i)
    return pl.pallas_call(
        _inproj_t_kernel,
        grid=(T // nt,),
        in_specs=[pl.BlockSpec((nt, B, D), lambda i: (i, 0, 0)),
                  pl.BlockSpec((1, B, D), lambda i: (0, 0, 0)),
                  pl.BlockSpec((1, B, D), lambda i: (0, 0, 0)),
                  _resident_layer(wt.shape, l)],
        out_specs=[pl.BlockSpec((GLA_IN_W, nt * B), cmap),
                   pl.BlockSpec((RET_IN_W, nt * B), cmap),
                   pl.BlockSpec((SSD_IN_W, nt * B), cmap)],
        out_shape=[jax.ShapeDtypeStruct((GLA_IN_W, T * B), F32),
                   jax.ShapeDtypeStruct((RET_IN_W, T * B), F32),
                   jax.ShapeDtypeStruct((SSD_IN_W, T * B), F32)],
        scratch_shapes=[pltpu.VMEM((IN_W, D), BF)],
        compiler_params=_cp(1),
        name="in_proj_t",
    )(xt, sc, sh, wt)


def _row_sum(x):
    return jnp.sum(x, axis=0, keepdims=True)


def _lane_state_readout(o, coef_ref, s0_ref, n_rows):
    nb = LANE
    half = len(o) // 2
    for part in range(2):
        def body(k8, accs, part=part):
            accs = list(accs)
            base = pl.multiple_of(k8 * 8, 8)
            grp = [coef_ref[pl.ds(base, 8), (part * half + i) * nb:(part * half + i + 1) * nb] for i in range(half)]
            for j in range(8):
                s0k = s0_ref[0, k8 * 8 + j]
                for i in range(half):
                    accs[i] = accs[i] + grp[i][j:j + 1, :] * s0k
            return tuple(accs)

        res = lax.fori_loop(0, n_rows // 8, body, tuple(o[part * half:(part + 1) * half]))
        o[part * half:(part + 1) * half] = list(res)
    return o


def _lane_state_update(sn_ref, s0_ref, decay_fn, coef_ref, val_fn, n_rows, T):
    nb = LANE

    def body(k8, carry):
        base = pl.multiple_of(k8 * 8, 8)
        grp = [coef_ref[pl.ds(base, 8), t * nb:(t + 1) * nb] for t in range(T)]
        dec = decay_fn(base)
        for j in range(8):
            dj = dec[j:j + 1, :] if dec.shape[0] == 8 else dec
            sk = dj * s0_ref[0, k8 * 8 + j]
            for t in range(T):
                sk = sk + grp[t][j:j + 1, :] * val_fn(t)
            sn_ref[0, k8 * 8 + j] = sk
        return carry

    lax.fori_loop(0, n_rows // 8, body, 0)


def _state_specs(shape, l):
    blk = (None, 1) + tuple(shape[2:])
    return pl.BlockSpec(blk, lambda h: (l, h, 0, 0, 0))


def _gla_t_kernel(x_ref, s0_ref, wg_ref, bg_ref, nw_ref, prev_ref, o_ref, sn_ref, qe_ref, ke_ref, a_ref, *, T):
    del prev_ref
    nb = LANE
    h = pl.program_id(0)
    r0 = pl.multiple_of(h * GLA_DK, GLA_DK)
    v0 = pl.multiple_of(h * GLA_DV, GLA_DV)
    q = x_ref[pl.ds(r0, GLA_DK), :] * (GLA_DK ** -0.5)
    k = x_ref[pl.ds(128 + r0, GLA_DK), :]
    gate = _dot(wg_ref[pl.ds(r0, GLA_DK), :], x_ref[512:640, :].astype(BF)) + bg_ref[pl.ds(r0, GLA_DK), :]
    la = _log_sigmoid(gate) * (1.0 / GLA_GATE_TEMP)
    gs = []
    acc = jnp.zeros((GLA_DK, nb), F32)
    for t in range(T):
        acc = acc + la[:, t * nb:(t + 1) * nb]
        gs.append(acc)
    gl = gs[T - 1]
    a_ref[...] = jnp.exp(gl)
    qs = [q[:, t * nb:(t + 1) * nb] for t in range(T)]
    ks = [k[:, t * nb:(t + 1) * nb] for t in range(T)]
    for t in range(T):
        qe_ref[:, t * nb:(t + 1) * nb] = qs[t] * jnp.exp(gs[t])
        ke_ref[:, t * nb:(t + 1) * nb] = ks[t] * jnp.exp(gl - gs[t])

    def vt(t):
        return x_ref[pl.ds(256 + v0, GLA_DV), t * nb:(t + 1) * nb]

    o = []
    for t in range(T):
        ot = jnp.zeros((GLA_DV, nb), F32)
        for u in range(t + 1):
            s = _row_sum(qs[t] * ks[u] * jnp.exp(gs[t] - gs[u]))
            ot = ot + s * vt(u)
        o.append(ot)

    o = _lane_state_readout(o, qe_ref, s0_ref, GLA_DK)
    _lane_state_update(sn_ref, s0_ref, lambda base: a_ref[pl.ds(base, 8), :], ke_ref, vt, GLA_DK, T)

    nw = nw_ref[pl.ds(v0, GLA_DV), :]
    for t in range(T):
        ms = jnp.mean(o[t] * o[t], axis=0, keepdims=True)
        r = x_ref[pl.ds(640 + v0, GLA_DV), t * nb:(t + 1) * nb]
        o_ref[:, t * nb:(t + 1) * nb] = (o[t] * lax.rsqrt(ms + EPS) * nw * _silu(r)).astype(o_ref.dtype)


def _gla_t_call(gT, s0, prev, l, T, w_gate, b_gate, norm_w):
    N = gT.shape[1]
    wg = jnp.zeros((GLA_H * GLA_DK, LANE), F32).at[:, :GLA_GATE_RANK].set(w_gate.T).astype(BF)
    bg = b_gate.reshape(-1, 1)
    nw = norm_w.reshape(-1, 1)
    sspec = _state_specs(s0.shape, l)
    ins = [gT, s0, wg, bg, nw]
    specs = [_const(gT.shape), sspec, _const(wg.shape), _const(bg.shape), _const(nw.shape)]
    aliases = {}
    if prev is not None:
        ins.append(prev)
        specs.append(pl.BlockSpec(memory_space=pl.ANY))
        aliases = {len(ins) - 1: 1}
    kern = functools.partial(_gla_t_kernel, T=T)
    if prev is None:
        kern = functools.partial(lambda *a, T: _gla_t_kernel(*a[:5], None, *a[5:], T=T), T=T)
    return pl.pallas_call(
        kern,
        grid=(GLA_H,),
        in_specs=specs,
        out_specs=[pl.BlockSpec((GLA_DV, N), lambda h: (h, 0)), sspec],
        out_shape=[jax.ShapeDtypeStruct((GLA_WIDTH, N), BF), jax.ShapeDtypeStruct(s0.shape, F32)],
        scratch_shapes=[pltpu.VMEM((GLA_DK, N), F32), pltpu.VMEM((GLA_DK, N), F32), pltpu.VMEM((GLA_DK, LANE), F32)],
        input_output_aliases=aliases,
        compiler_params=_cp(1),
        name="gla_t",
    )(*ins)


def _ret_t_kernel(x_ref, s0_ref, cos_ref, sin_ref, pw_ref, nw_ref, prev_ref, o_ref, sn_ref, qd_ref, kd_ref, *, T):
    del prev_ref
    nb = LANE
    h = pl.program_id(0)
    r0 = pl.multiple_of(h * RET_DK, RET_DK)
    half_k = RET_DK // 2
    cos, sin = cos_ref[...], sin_ref[...]

    def rope_t(base):
        x1 = x_ref[pl.ds(base + r0, half_k), :]
        x2 = x_ref[pl.ds(base + r0 + half_k, half_k), :]
        return jnp.concatenate([x1 * cos - x2 * sin, x1 * sin + x2 * cos], axis=0)

    q = rope_t(0)
    k = rope_t(256) * (RET_DK ** -0.5)
    pw = pw_ref[h]
    qs = [q[:, t * nb:(t + 1) * nb] for t in range(T)]
    ks = [k[:, t * nb:(t + 1) * nb] for t in range(T)]
    for t in range(T):
        qd_ref[:, t * nb:(t + 1) * nb] = qs[t] * pw[t + 1:t + 2, :]
        kd_ref[:, t * nb:(t + 1) * nb] = ks[t] * pw[T - 1 - t:T - t, :]

    def vt(t):
        return x_ref[pl.ds(512 + r0, RET_DV), t * nb:(t + 1) * nb]

    o = []
    for t in range(T):
        ot = jnp.zeros((RET_DV, nb), F32)
        for u in range(t + 1):
            s = _row_sum(qs[t] * ks[u]) * pw[t - u:t - u + 1, :]
            ot = ot + s * vt(u)
        o.append(ot)

    o = _lane_state_readout(o, qd_ref, s0_ref, RET_DK)
    _lane_state_update(sn_ref, s0_ref, lambda base: pw[T:T + 1, :], kd_ref, vt, RET_DK, T)

    nw = nw_ref[pl.ds(r0, RET_DV), :]
    for t in range(T):
        mu = jnp.mean(o[t], axis=0, keepdims=True)
        d = o[t] - mu
        var = jnp.mean(d * d, axis=0, keepdims=True)
        rg = x_ref[pl.ds(768 + r0, RET_DV), t * nb:(t + 1) * nb]
        o_ref[:, t * nb:(t + 1) * nb] = (d * lax.rsqrt(var + EPS) * nw * _silu(rg)).astype(o_ref.dtype)


def _ret_t_call(rT, s0, prev, l, T, norm_w):
    N = rT.shape[1]
    B = N // T
    half = RET_DK // 2
    inv = ROPE_BASE ** (-jnp.arange(half, dtype=F32) / half)
    ang = inv[:, None] * (PAST_LEN + jnp.arange(T, dtype=jnp.int32)).astype(F32)[None, :]
    cos = jnp.repeat(jnp.cos(ang), B, axis=1)
    sin = jnp.repeat(jnp.sin(ang), B, axis=1)
    lg = _ret_log_gamma()
    pw = jnp.asarray(np.repeat(np.exp(lg[:, None] * np.arange(16)[None, :])[:, :, None], LANE, axis=2), F32)
    nw = norm_w.reshape(-1, 1)
    sspec = _state_specs(s0.shape, l)
    ins = [rT, s0, cos, sin, pw, nw]
    specs = [_const(rT.shape), sspec, _const(cos.shape), _const(sin.shape), _const(pw.shape), _const(nw.shape)]
    aliases = {}
    kern = functools.partial(_ret_t_kernel, T=T)
    if prev is not None:
        ins.append(prev)
        specs.append(pl.BlockSpec(memory_space=pl.ANY))
        aliases = {len(ins) - 1: 1}
    else:
        kern = functools.partial(lambda *a, T: _ret_t_kernel(*a[:6], None, *a[6:], T=T), T=T)
    return pl.pallas_call(
        kern,
        grid=(RET_H,),
        in_specs=specs,
        out_specs=[pl.BlockSpec((RET_DV, N), lambda h: (h, 0)), sspec],
        out_shape=[jax.ShapeDtypeStruct((RET_WIDTH, N), BF), jax.ShapeDtypeStruct(s0.shape, F32)],
        scratch_shapes=[pltpu.VMEM((RET_DK, N), F32), pltpu.VMEM((RET_DK, N), F32)],
        input_output_aliases=aliases,
        compiler_params=_cp(1),
        name="ret_t",
    )(*ins)


def _ssd_t_kernel(x_ref, c0_ref, s0_ref, cw_ref, cb_ref, dtb_ref, alog_ref, d_ref, nw_ref, prevs_ref, prevc_ref,
                  o_ref, sn_ref, cn_ref, hist_ref, y_ref, ssq_ref, cm_ref, bw_ref, xw_ref, *, T):
    del prevs_ref, prevc_ref
    nb = LANE
    W1 = SSD_CONV_W - 1
    h = pl.program_id(0)
    XB = SSD_WIDTH

    @pl.when(h == 0)
    def _():
        ssq_ref[...] = jnp.zeros_like(ssq_ref)
        for i in range(W1):
            for j in range(SSD_CONV_DIM // LANE):
                hist_ref[j * LANE:(j + 1) * LANE, i * nb:(i + 1) * nb] = c0_ref[0, i][:, j * LANE:(j + 1) * LANE].T
                cn_ref[0, i, :, j * LANE:(j + 1) * LANE] = \
                    x_ref[XB + j * LANE:XB + (j + 1) * LANE, (T - W1 + i) * nb:(T - W1 + i + 1) * nb].T

    def conv_rows(ro):
        w = cw_ref[pl.ds(ro, 64), :]
        b = cb_ref[pl.ds(ro, 64), :]
        xx = [hist_ref[pl.ds(ro, 64), i * nb:(i + 1) * nb] for i in range(W1)]
        xx += [x_ref[pl.ds(XB + ro, 64), t * nb:(t + 1) * nb] for t in range(T)]
        out = []
        for t in range(T):
            acc = b + w[:, 0:1] * xx[t]
            for i in range(1, SSD_CONV_W):
                acc = acc + w[:, i:i + 1] * xx[t + i]
            out.append(_silu(acc))
        return out

    grp = h // (SSD_H // SSD_G)
    xs = conv_rows(pl.multiple_of(h * SSD_P, SSD_P))
    bm = conv_rows(pl.multiple_of(SSD_WIDTH + grp * SSD_N, SSD_N))
    cm = conv_rows(pl.multiple_of(SSD_WIDTH + SSD_G * SSD_N + grp * SSD_N, SSD_N))

    dt_all = _softplus(x_ref[pl.ds(XB + SSD_CONV_DIM + h, 1), :] + dtb_ref[pl.ds(h, 1), :])
    a = -jnp.exp(alog_ref[pl.ds(h, 1), :])
    dts = [dt_all[:, t * nb:(t + 1) * nb] for t in range(T)]
    gs = []
    acc = jnp.zeros((1, nb), F32)
    for t in range(T):
        acc = acc + dts[t] * a
        gs.append(acc)
    gl = gs[T - 1]

    o = []
    for t in range(T):
        ot = jnp.zeros((SSD_P, nb), F32)
        for u in range(t + 1):
            s = _row_sum(cm[t] * bm[u]) * (jnp.exp(gs[t] - gs[u]) * dts[u])
            ot = ot + s * xs[u]
        o.append(ot)

    for t in range(T):
        cm_ref[:, t * nb:(t + 1) * nb] = cm[t] * jnp.exp(gs[t])
        bw_ref[:, t * nb:(t + 1) * nb] = bm[t]
        xw_ref[:, t * nb:(t + 1) * nb] = xs[t] * (dts[t] * jnp.exp(gl - gs[t]))

    o = _lane_state_readout(o, cm_ref, s0_ref, SSD_N)
    egl = jnp.exp(gl)
    _lane_state_update(sn_ref, s0_ref, lambda base: egl, bw_ref, lambda t: xw_ref[:, t * nb:(t + 1) * nb], SSD_N, T)

    dd = d_ref[pl.ds(h, 1), :]
    p0 = pl.multiple_of(h * SSD_P, SSD_P)
    for t in range(T):
        z = x_ref[pl.ds(p0, SSD_P), t * nb:(t + 1) * nb]
        y = (o[t] + dd * xs[t]) * _silu(z)
        y_ref[pl.ds(p0, SSD_P), t * nb:(t + 1) * nb] = y
        ssq_ref[:, t * nb:(t + 1) * nb] += _row_sum(y * y)

    @pl.when(h == SSD_H - 1)
    def _():
        scale = lax.rsqrt(ssq_ref[...] * (1.0 / SSD_WIDTH) + EPS)
        o_ref[...] = (y_ref[...] * scale * nw_ref[...]).astype(o_ref.dtype)


def _ssd_t_call(sT, c0, s0, prev_s, prev_c, l, T, conv_w, conv_b, dt_bias, a_log, d, norm_w):
    N = sT.shape[1]
    col = lambda v: jnp.zeros((LANE, 1), F32).at[:SSD_H, 0].set(v)
    prm = (conv_w.T, conv_b.reshape(-1, 1), col(dt_bias), col(a_log), col(d), norm_w.reshape(-1, 1))
    sspec = _state_specs(s0.shape, l)
    cspec = pl.BlockSpec((1,) + tuple(c0.shape[1:]), lambda h: (l, 0, 0, 0))
    ins = [sT, c0, s0, *prm]
    specs = [_const(sT.shape), cspec, sspec] + [_const(p.shape) for p in prm]
    aliases = {}
    kern = functools.partial(_ssd_t_kernel, T=T)
    if prev_s is not None:
        ins += [prev_s, prev_c]
        specs += [pl.BlockSpec(memory_space=pl.ANY), pl.BlockSpec(memory_space=pl.ANY)]
        aliases = {len(ins) - 2: 1, len(ins) - 1: 2}
    else:
        kern = functools.partial(lambda *a, T: _ssd_t_kernel(*a[:9], None, None, *a[9:], T=T), T=T)
    return pl.pallas_call(
        kern,
        grid=(SSD_H,),
        in_specs=specs,
        out_specs=[_const((SSD_WIDTH, N)), sspec, cspec],
        out_shape=[jax.ShapeDtypeStruct((SSD_WIDTH, N), BF), jax.ShapeDtypeStruct(s0.shape, F32),
                   jax.ShapeDtypeStruct(c0.shape, F32)],
        scratch_shapes=[pltpu.VMEM((SSD_CONV_DIM, (SSD_CONV_W - 1) * LANE), F32),
                        pltpu.VMEM((SSD_WIDTH, N), F32), pltpu.VMEM((1, N), F32),
                        pltpu.VMEM((SSD_N, N), F32), pltpu.VMEM((SSD_N, N), F32), pltpu.VMEM((SSD_P, N), F32)],
        input_output_aliases=aliases,
        compiler_params=_cp(1),
        name="ssd_t",
    )(*ins)


def _outproj_t_kernel(x_ref, g_ref, og_ref, or_ref, os_ref, w_ref, lg_ref, lb_ref, o_ref):
    nt, nb, D = x_ref.shape
    for t in range(nt):
        cols = slice(t * nb, (t + 1) * nb)
        mix = (_dot_tn(og_ref[:, cols], w_ref[0:GLA_WIDTH, :])
               + _dot_tn(or_ref[:, cols], w_ref[GLA_WIDTH:GLA_WIDTH + RET_WIDTH, :])
               + _dot_tn(os_ref[:, cols], w_ref[GLA_WIDTH + RET_WIDTH:D, :]))
        y = ALPHA * x_ref[t] + g_ref[0] * mix
        o_ref[t] = _layer_norm(y, lg_ref[0], lb_ref[0])


def _outproj_t_call(xt, g1, ogT, orT, osT, w_out, ln_g, ln_b):
    T, B, D = xt.shape
    nt = 4
    cmap = lambda i: (0, i)
    return pl.pallas_call(
        _outproj_t_kernel,
        grid=(T // nt,),
        in_specs=[pl.BlockSpec((nt, B, D), lambda i: (i, 0, 0)),
                  pl.BlockSpec((1, B, D), lambda i: (0, 0, 0)),
                  pl.BlockSpec((GLA_WIDTH, nt * B), cmap),
                  pl.BlockSpec((RET_WIDTH, nt * B), cmap),
                  pl.BlockSpec((SSD_WIDTH, nt * B), cmap),
                  _const((D, D)), _const((1, 1, D)), _const((1, 1, D))],
        out_specs=pl.BlockSpec((nt, B, D), lambda i: (i, 0, 0)),
        out_shape=jax.ShapeDtypeStruct((T, B, D), F32),
        compiler_params=_cp(1),
        name="out_proj_ln_t",
    )(xt, g1, ogT, orT, osT, w_out, ln_g.reshape(1, 1, D), ln_b.reshape(1, 1, D))


def _outproj_kernel(x_ref, g_ref, og_ref, or_ref, os_ref, w_ref, lg_ref, lb_ref, o_ref):
    bB, bT, D = x_ref.shape
    assert bB == 1
    n_part = 2
    for i in range(n_part):
        rows = slice(i * bT // n_part, (i + 1) * bT // n_part)
        mix = (_dot(og_ref[rows, :], w_ref[0:GLA_WIDTH, :])
               + _dot(or_ref[rows, :], w_ref[GLA_WIDTH:GLA_WIDTH + RET_WIDTH, :])
               + _dot(os_ref[rows, :], w_ref[GLA_WIDTH + RET_WIDTH:D, :]))
        y = ALPHA * x_ref[0, rows, :] + g_ref[0] * mix
        o_ref[0, rows, :] = _layer_norm(y, lg_ref[0], lb_ref[0])


def _outproj_call(x3, g1, og, orr, os_, w_out, ln_g, ln_b):
    B, T, D = x3.shape
    bB, bT = _tok_tiles(B, T)
    nT = T // bT
    R = bB * bT
    xmap = lambda i, j: (i, j, 0)
    mmap = lambda i, j: (i, 0, 0)
    rmap = lambda i, j: (i * nT + j, 0)
    return pl.pallas_call(
        _outproj_kernel,
        grid=(B // bB, nT),
        in_specs=[pl.BlockSpec((bB, bT, D), xmap),
                  pl.BlockSpec((bB, 1, D), mmap),
                  pl.BlockSpec((R, GLA_WIDTH), rmap),
                  pl.BlockSpec((R, RET_WIDTH), rmap),
                  pl.BlockSpec((R, SSD_WIDTH), rmap),
                  _const((D, D)), _const((1, 1, D)), _const((1, 1, D))],
        out_specs=pl.BlockSpec((bB, bT, D), xmap),
        out_shape=jax.ShapeDtypeStruct((B, T, D), F32),
        compiler_params=_cp(2),
        name="out_proj_ln",
    )(x3, g1, og, orr, os_, w_out, ln_g.reshape(1, 1, D), ln_b.reshape(1, 1, D))


ROUTE_OFF = 8


def _moe_route_t(lt):
    R = lt.shape[1]
    neg = jnp.float32(-jnp.inf)
    row8 = lax.broadcasted_iota(jnp.int32, (8, 1), 0)
    lg = jnp.where(row8 < MOE_GROUPS, lt[0:8, :], neg)
    mg = jnp.max(lg, axis=0, keepdims=True)
    gsel = jnp.min(jnp.where(lg == mg, row8, 8), axis=0, keepdims=True)
    g_gate = 1.0 / jnp.sum(jnp.exp(lg - mg), axis=0, keepdims=True)
    rowe = lax.broadcasted_iota(jnp.int32, (MOE_EXPERTS, 1), 0)
    le = jnp.where((rowe // MOE_PER_GROUP) == gsel, lt[ROUTE_OFF:ROUTE_OFF + MOE_EXPERTS, :], neg)
    m1 = jnp.max(le, axis=0, keepdims=True)
    i1 = jnp.min(jnp.where(le == m1, rowe, MOE_EXPERTS), axis=0, keepdims=True)
    le2 = jnp.where(rowe == i1, neg, le)
    m2 = jnp.max(le2, axis=0, keepdims=True)
    i2 = jnp.min(jnp.where(le2 == m2, rowe, MOE_EXPERTS), axis=0, keepdims=True)
    e2 = jnp.exp(m2 - m1)
    w1 = g_gate / (1.0 + e2)
    w2 = g_gate * e2 / (1.0 + e2)
    comb = jnp.where(rowe == i1, w1, jnp.where(rowe == i2, w2, 0.0))
    cg = comb[0:4, :]
    for g in range(1, MOE_GROUPS):
        cg = cg + comb[g * MOE_PER_GROUP:(g + 1) * MOE_PER_GROUP, :]
    return gsel, cg, comb


MOE_SUB = 256
MOE_BLK = 16
MOE_ROWS = 256
MOE_NPS = MOE_SUB + MOE_GROUPS * MOE_BLK
MOE_MAXB = MOE_SUB // MOE_BLK + 1


def _moe_kernel(x_ref, sc_ref, sh_ref, g_ref, wr_ref, br_ref, us_ref, w1_ref, w3_ref, w2_ref, lg_ref, lb_ref,
                o_ref, hb_ref, cwb_ref, hp_ref, cwp_ref, yp_ref, pos_ref,
                cgrp_ref, fill_ref, cur_ref, na_ref, dst_ref, nb_ref, so_ref, *, n_steps):
    bB, bT, D = x_ref.shape
    R = bB * bT
    n_q = R // MOE_SUB
    s = pl.program_id(1)
    x = x_ref[...]
    row8 = lax.broadcasted_iota(jnp.int32, (8, 1), 0)
    slot = lax.broadcasted_iota(jnp.int32, (MOE_NPS, 1), 0).astype(F32)

    @pl.when((pl.program_id(0) == 0) & (s == 0))
    def _():
        hb_ref[...] = jnp.zeros_like(hb_ref)
        cwb_ref[...] = jnp.zeros_like(cwb_ref)
        yp_ref[...] = jnp.zeros_like(yp_ref)

    @pl.when(s == 0)
    def _():
        na_ref[0] = 0
        for g in range(MOE_GROUPS):
            cur_ref[g] = -1
            fill_ref[g] = 0

    @pl.when(s < n_steps)
    def _():
        h = (x * (1.0 + sc_ref[...]) + sh_ref[...]).reshape(R, D)
        segs, offs = [], []
        for q in range(n_q):
            u = s * n_q + q
            hq = h[q * MOE_SUB:(q + 1) * MOE_SUB, :].astype(BF)
            gsel, cg, _ = _moe_route_t(_dot_nt(wr_ref[...], hq) + br_ref[...])
            onehot = jnp.where(row8 == gsel, 1.0, 0.0)
            rank = _dot(onehot.astype(BF), us_ref[...])
            cnt = jnp.sum(onehot, axis=1, keepdims=True)
            seg = jnp.ceil(cnt * (1.0 / MOE_BLK)) * MOE_BLK
            off = jnp.zeros((8, 1), F32)
            for g in range(1, MOE_GROUPS):
                off = off + jnp.where(row8 >= g, seg[g - 1:g, :], 0.0)
            pos = jnp.sum(onehot * (off + rank), axis=0, keepdims=True)
            pos_ref[u] = jnp.broadcast_to(pos, (8, MOE_SUB))
            perm = jnp.where(slot == pos, 1.0, 0.0).astype(BF)
            hp_ref[q] = _dot(perm, hq).astype(BF)
            cg8 = jnp.concatenate([cg, jnp.zeros((4, MOE_SUB), F32)], axis=0)
            cg_hi = cg8.astype(BF)
            cg_lo = (cg8 - cg_hi.astype(F32)).astype(BF)
            cwp_ref[q] = _dot_nt(perm, cg_hi) + _dot_nt(perm, cg_lo)
            segs.append(seg)
            offs.append(off)
        for q in range(n_q):
            u = s * n_q + q
            for g in range(MOE_GROUPS):
                so = offs[q][g, 0].astype(jnp.int32)
                nb = (segs[q][g, 0] * (1.0 / MOE_BLK)).astype(jnp.int32)
                so_ref[u * MOE_GROUPS + g] = so
                nb_ref[u * MOE_GROUPS + g] = nb

                def put(k, carry, g=g, so=so, u=u, q=q):
                    f = fill_ref[g]
                    c = cur_ref[g]
                    na = na_ref[0]
                    fresh = (c < 0) | (f >= MOE_ROWS)
                    c = jnp.where(fresh, na, c)
                    f = jnp.where(fresh, 0, f)
                    cgrp_ref[c] = g
                    na_ref[0] = jnp.where(fresh, na + 1, na)
                    dst = pl.multiple_of(c * MOE_ROWS + f, MOE_BLK)
                    src = pl.multiple_of(so + k * MOE_BLK, MOE_BLK)
                    hb_ref[pl.ds(dst, MOE_BLK), :] = hp_ref[q, pl.ds(src, MOE_BLK), :]
                    cwb_ref[pl.ds(dst, MOE_BLK), :] = cwp_ref[q, pl.ds(src, MOE_BLK), :]
                    dst_ref[(u * MOE_GROUPS + g) * MOE_MAXB + k] = dst
                    cur_ref[g] = c
                    fill_ref[g] = f + MOE_BLK
                    return carry

                lax.fori_loop(0, nb, put, 0)

    @pl.when(s == n_steps - 1)
    def _():
        def chunk(c, carry):
            g = cgrp_ref[c]
            start = pl.multiple_of(c * MOE_ROWS, MOE_ROWS)
            hc = hb_ref[pl.ds(start, MOE_ROWS), :]
            cw = cwb_ref[pl.ds(start, MOE_ROWS), :]
            acc = jnp.zeros((MOE_ROWS, D), F32)
            for j in range(MOE_PER_GROUP):
                e = g * MOE_PER_GROUP + j
                hid = _silu(_dot(hc, w1_ref[e])) * _dot(hc, w3_ref[e]) * cw[:, j:j + 1]
                acc = acc + _dot(hid.astype(BF), w2_ref[e])
            hb_ref[pl.ds(start, MOE_ROWS), :] = acc.astype(BF)
            return carry

        lax.fori_loop(0, na_ref[0], chunk, 0)

    @pl.when(s >= n_steps)
    def _():
        for q in range(n_q):
            u = (s - n_steps) * n_q + q
            for g in range(MOE_GROUPS):
                so = so_ref[u * MOE_GROUPS + g]

                def take(k, carry, g=g, so=so, u=u, q=q):
                    src = pl.multiple_of(dst_ref[(u * MOE_GROUPS + g) * MOE_MAXB + k], MOE_BLK)
                    dst = pl.multiple_of(so + k * MOE_BLK, MOE_BLK)
                    yp_ref[q, pl.ds(dst, MOE_BLK), :] = hb_ref[pl.ds(src, MOE_BLK), :]
                    return carry

                lax.fori_loop(0, nb_ref[u * MOE_GROUPS + g], take, 0)
        ys = []
        for q in range(n_q):
            u = (s - n_steps) * n_q + q
            perm = jnp.where(slot == pos_ref[u][0:1, :], 1.0, 0.0).astype(BF)
            ys.append(_dot_tn(perm, yp_ref[q]))
        y = jnp.concatenate(ys, axis=0)
        z = ALPHA * x + g_ref[...] * y.reshape(bB, bT, D)
        o_ref[...] = _layer_norm(z, lg_ref[...], lb_ref[...])


def _resident_layer(shape, l):
    return pl.BlockSpec((None,) + tuple(shape[1:]), lambda *_: (l,) + (0,) * (len(shape) - 1),
                        pipeline_mode=pl.Buffered(1))


def _moe_call(x3, sc, sh, g2, wr, br, w1, w3, w2, l, ln_g, ln_b):
    B, T, D = x3.shape
    bB, bT = _tok_tiles(B, T)
    R = bB * bT
    if bB == 1:
        spp = 2 if B % 2 == 0 else 1
        nT = T // bT
        n_pools, n_steps = B // spp, spp * nT
        xmap = lambda p, s: (p * spp + (s % n_steps) // nT, (s % n_steps) % nT, 0)
        omap = lambda p, s: (p * spp + jnp.maximum(s - n_steps, 0) // nT, jnp.maximum(s - n_steps, 0) % nT, 0)
        mmap = lambda p, s: (p * spp + (s % n_steps) // nT, 0, 0)
        mshape = (1, 1, D)
    else:
        n_pools, n_steps = 1, B // bB
        xmap = lambda p, s: (s % n_steps, 0, 0)
        omap = lambda p, s: (jnp.maximum(s - n_steps, 0), 0, 0)
        mmap = lambda p, s: (0, 0, 0)
        mshape = (1, bT, D)
    n_sub = n_steps * (R // MOE_SUB)
    n_chunks = n_sub * MOE_SUB // MOE_ROWS + MOE_GROUPS
    us = jnp.asarray(np.triu(np.ones((MOE_SUB, MOE_SUB), np.float32), 1), BF)
    smem = lambda n: pltpu.SMEM((n,), jnp.int32)
    return pl.pallas_call(
        functools.partial(_moe_kernel, n_steps=n_steps),
        grid=(n_pools, 2 * n_steps),
        in_specs=[pl.BlockSpec((bB, bT, D), xmap),
                  pl.BlockSpec(mshape, mmap), pl.BlockSpec(mshape, mmap), pl.BlockSpec(mshape, mmap),
                  _const(wr.shape), _const(br.shape), _const(us.shape),
                  _resident_layer(w1.shape, l), _resident_layer(w3.shape, l), _resident_layer(w2.shape, l),
                  _const((1, 1, D)), _const((1, 1, D))],
        out_specs=pl.BlockSpec((bB, bT, D), omap),
        out_shape=jax.ShapeDtypeStruct((B, T, D), F32),
        scratch_shapes=[pltpu.VMEM((n_chunks * MOE_ROWS, D), BF), pltpu.VMEM((n_chunks * MOE_ROWS, 8), F32),
                        pltpu.VMEM((R // MOE_SUB, MOE_NPS, D), BF), pltpu.VMEM((R // MOE_SUB, MOE_NPS, 8), F32),
                        pltpu.VMEM((R // MOE_SUB, MOE_NPS, D), BF),
                        pltpu.VMEM((n_sub, 8, MOE_SUB), F32),
                        smem(n_chunks), smem(MOE_GROUPS), smem(MOE_GROUPS), smem(1),
                        smem(n_sub * MOE_GROUPS * MOE_MAXB), smem(n_sub * MOE_GROUPS), smem(n_sub * MOE_GROUPS)],
        compiler_params=_cp(2),
        name="moe_ln",
    )(x3, sc, sh, g2, wr, br, us, w1, w3, w2, ln_g.reshape(1, 1, D), ln_b.reshape(1, 1, D))


def _router_params(w_group, b_group, w_expert, b_expert):
    wr = jnp.zeros((LANE, D_MODEL), F32).at[:MOE_GROUPS].set(w_group.T)
    wr = wr.at[ROUTE_OFF:ROUTE_OFF + MOE_EXPERTS].set(w_expert.T)
    br = jnp.zeros((LANE, 1), F32).at[:MOE_GROUPS, 0].set(b_group).at[ROUTE_OFF:ROUTE_OFF + MOE_EXPERTS, 0].set(b_expert)
    return wr.astype(BF), br


def kernel(x_prompt, x_sample, c_prompt, c_sample, state_gla, state_ret, state_ssd, state_conv, w_ada, b_ada, w_in, gla_w_gate, gla_b_gate, gla_norm, ret_norm, ssd_conv_w, ssd_conv_b, ssd_dt_bias, ssd_a_log, ssd_d, ssd_norm, w_out, ln1_g, ln1_b, moe_w_group, moe_b_group, moe_w_expert, moe_b_expert, moe_w1, moe_w3, moe_w2, ln2_g, ln2_b):
    Bp, Tp, D = x_prompt.shape
    Bs, Ts, _ = x_sample.shape
    w_in_t = jnp.swapaxes(w_in, 1, 2)
    w_out_b = w_out.astype(BF)
    w1_b, w3_b, w2_b = moe_w1.astype(BF), moe_w3.astype(BF), moe_w2.astype(BF)

    mod = _mod_call(jnp.concatenate([c_prompt, c_sample], axis=0), w_ada, b_ada)

    def moe(x, sc2, sh2, g2, l):
        wr, br = _router_params(moe_w_group[l], moe_b_group[l], moe_w_expert[l], moe_b_expert[l])
        return _moe_call(x, sc2, sh2, g2, wr, br, w1_b, w3_b, w2_b, l, ln2_g[l], ln2_b[l])

    x = x_prompt
    new = [[], [], [], []]
    for l in range(DEPTH):
        sh1, sc1, g1, sh2, sc2, g2 = (mod[l, :Bp, None, i * D:(i + 1) * D] for i in range(6))
        gin, rin, sin_ = _inproj_call(x, sc1, sh1, w_in_t, l)
        og, s_gla = _gla_prompt_call(gin, Bp, Tp, gla_w_gate[l], gla_b_gate[l], gla_norm[l])
        orr, s_ret = _ret_prompt_call(rin, Bp, Tp, ret_norm[l])
        os_, s_ssd, s_conv = _ssd_prompt_call(sin_, Bp, Tp, ssd_conv_w[l], ssd_conv_b[l], ssd_dt_bias[l],
                                              ssd_a_log[l], ssd_d[l], ssd_norm[l])
        x = _outproj_call(x, g1, og, orr, os_, w_out_b[l], ln1_g[l], ln1_b[l])
        x = moe(x, sc2, sh2, g2, l)
        for acc, s in zip(new, (s_gla, s_ret, s_ssd, s_conv)):
            acc.append(s)
    y_p = x
    gla_p, ret_p, ssd_p, conv_p = (jnp.stack(a) for a in new)

    x = jnp.swapaxes(x_sample, 0, 1)
    sg = jnp.transpose(state_gla, (0, 2, 3, 4, 1))
    sr = jnp.transpose(state_ret, (0, 2, 3, 4, 1))
    ss = jnp.transpose(state_ssd, (0, 2, 3, 4, 1))
    cv = jnp.transpose(state_conv, (0, 2, 1, 3))
    gla_n = ret_n = ssd_n = conv_n = None
    for l in range(DEPTH):
        sh1, sc1, g1, sh2, sc2, g2 = (mod[l, None, Bp:, i * D:(i + 1) * D] for i in range(6))
        gT, rT, sT = _inproj_t_call(x, sc1, sh1, w_in_t, l)
        ogT, gla_n = _gla_t_call(gT, sg, gla_n, l, Ts, gla_w_gate[l], gla_b_gate[l], gla_norm[l])
        orT, ret_n = _ret_t_call(rT, sr, ret_n, l, Ts, ret_norm[l])
        osT, ssd_n, conv_n = _ssd_t_call(sT, cv, ss, ssd_n, conv_n, l, Ts, ssd_conv_w[l], ssd_conv_b[l],
                                         ssd_dt_bias[l], ssd_a_log[l], ssd_d[l], ssd_norm[l])
        x = _outproj_t_call(x, g1, ogT, orT, osT, w_out_b[l], ln1_g[l], ln1_b[l])
        x = moe(x, sc2, sh2, g2, l)
    y_s = jnp.swapaxes(x, 0, 1)
    gla_s = jnp.transpose(gla_n, (0, 4, 1, 2, 3))
    ret_s = jnp.transpose(ret_n, (0, 4, 1, 2, 3))
    ssd_s = jnp.transpose(ssd_n, (0, 4, 1, 2, 3))
    conv_s = jnp.transpose(conv_n, (0, 2, 1, 3))
    return (y_p, y_s, gla_p, ret_p, ssd_p, conv_p, gla_s, ret_s, ssd_s, conv_s)
```

```python
import functools
import math

import numpy as np
import jax
import jax.numpy as jnp
from jax import lax
from jax.experimental import pallas as pl
from jax.experimental.pallas import tpu as pltpu

F32 = jnp.float32
BF = jnp.bfloat16

D_MODEL = 1024
DEPTH = 2
PAST_LEN = 16384
GLA_H, GLA_DK, GLA_DV = 4, 32, 64
GLA_WIDTH = GLA_H * GLA_DV
GLA_GATE_RANK = 16
GLA_GATE_TEMP = 16.0
GLA_CHUNK = 16
RET_H, RET_DK, RET_DV = 4, 64, 64
RET_WIDTH = RET_H * RET_DV
ROPE_BASE = 10000.0
SSD_H, SSD_P, SSD_G, SSD_N = 8, 64, 2, 64
SSD_WIDTH = SSD_H * SSD_P
SSD_CONV_W = 4
SSD_CONV_DIM = SSD_WIDTH + 2 * SSD_G * SSD_N
MOE_GROUPS, MOE_PER_GROUP = 4, 4
MOE_EXPERTS = MOE_GROUPS * MOE_PER_GROUP
MOE_FF = 256
ALPHA = (2 * DEPTH) ** 0.25
EPS = 1e-5

LANE = 128
GLA_IN_W = 128 + 128 + 256 + LANE + 256
RET_IN_W = 4 * 256
SSD_IN_W = 512 + SSD_CONV_DIM + LANE
IN_W = GLA_IN_W + RET_IN_W + SSD_IN_W
VMEM_LIMIT = 56 * 1024 * 1024


def _cp(n_axes, vmem=VMEM_LIMIT):
    return pltpu.CompilerParams(dimension_semantics=("arbitrary",) * n_axes, vmem_limit_bytes=vmem)


def _dot(a, b):
    return jnp.dot(a, b, preferred_element_type=F32)


def _dot_nt(a, b):
    return lax.dot_general(a, b, (((1,), (1,)), ((), ())), preferred_element_type=F32)


def _dot_tn(a, b):
    return lax.dot_general(a, b, (((0,), (0,)), ((), ())), preferred_element_type=F32)


def _split3(x):
    hi = x.astype(BF)
    r = x - hi.astype(F32)
    mid = r.astype(BF)
    lo = (r - mid.astype(F32)).astype(BF)
    return hi, mid, lo


def _dot_x3(x, e):
    hi, mid, lo = _split3(x)
    return _dot(hi, e) + (_dot(mid, e) + _dot(lo, e))


def _dot_x2(x, e):
    hi = x.astype(BF)
    lo = (x - hi.astype(F32)).astype(BF)
    return _dot(hi, e) + _dot(lo, e)


def _dot_3x(e, x):
    hi, mid, lo = _split3(x)
    return _dot(e, hi) + (_dot(e, mid) + _dot(e, lo))


def _sigmoid(x):
    return 1.0 / (1.0 + jnp.exp(-x))


def _silu(x):
    return x * _sigmoid(x)


def _log_sigmoid(x):
    return jnp.minimum(x, 0.0) - jnp.log(1.0 + jnp.exp(-jnp.abs(x)))


def _softplus(x):
    return jnp.maximum(x, 0.0) + jnp.log(1.0 + jnp.exp(-jnp.abs(x)))


def _layer_norm(x, g, b):
    mu = jnp.mean(x, axis=-1, keepdims=True)
    d = x - mu
    var = jnp.mean(d * d, axis=-1, keepdims=True)
    return d * lax.rsqrt(var + EPS) * g + b


def _mod_kernel(c_ref, w_ref, b_ref, o_ref):
    s = _silu(c_ref[...]).astype(BF)
    o_ref[0] = _dot(s, w_ref[0].astype(BF)) + b_ref[0]


def _mod_call(c_all, w_ada, b_ada):
    R = c_all.shape[0]
    tn = 1536
    return pl.pallas_call(
        _mod_kernel,
        grid=(DEPTH, 6 * D_MODEL // tn),
        in_specs=[pl.BlockSpec((R, D_MODEL), lambda l, j: (0, 0)),
                  pl.BlockSpec((1, D_MODEL, tn), lambda l, j: (l, 0, j)),
                  pl.BlockSpec((1, 1, tn), lambda l, j: (l, 0, j))],
        out_specs=pl.BlockSpec((1, R, tn), lambda l, j: (l, 0, j)),
        out_shape=jax.ShapeDtypeStruct((DEPTH, R, 6 * D_MODEL), F32),
        compiler_params=_cp(2),
        name="ada_mod",
    )(c_all, w_ada, b_ada.reshape(DEPTH, 1, 6 * D_MODEL))


N_IN = 3096
N_GA = 128 + 128 + 256
N_DT = N_IN - SSD_H


def _inproj_kernel(x_ref, sc_ref, sh_ref, wt_ref, og_ref, or_ref, os_ref, w_ref):
    bB, bT, D = x_ref.shape

    @pl.when((pl.program_id(0) == 0) & (pl.program_id(1) == 0))
    def _():
        lane = lax.broadcasted_iota(jnp.int32, (1, LANE), 1)
        for j in range(N_GA // LANE):
            w_ref[:, j * LANE:(j + 1) * LANE] = wt_ref[j * LANE:(j + 1) * LANE, :].T.astype(BF)
        ga = wt_ref[N_GA:N_GA + LANE, :].T
        w_ref[:, N_GA:N_GA + LANE] = jnp.where(lane < GLA_GATE_RANK, ga, 0.0).astype(BF)
        src0, dst0 = N_GA + GLA_GATE_RANK, N_GA + LANE
        for j in range((N_DT - src0) // LANE):
            w_ref[:, dst0 + j * LANE:dst0 + (j + 1) * LANE] = \
                wt_ref[src0 + j * LANE:src0 + (j + 1) * LANE, :].T.astype(BF)
        dt = pltpu.roll(wt_ref[N_IN - LANE:N_IN, :].T, SSD_H, 1)
        w_ref[:, IN_W - LANE:IN_W] = jnp.where(lane < SSD_H, dt, 0.0).astype(BF)

    h = x_ref[...] * (1.0 + sc_ref[...]) + sh_ref[...]
    hb = h.reshape(bB * bT, D).astype(BF)
    og_ref[...] = _dot(hb, w_ref[:, 0:GLA_IN_W])
    or_ref[...] = _dot(hb, w_ref[:, GLA_IN_W:GLA_IN_W + RET_IN_W])
    os_ref[...] = _dot(hb, w_ref[:, GLA_IN_W + RET_IN_W:IN_W])


def _tok_tiles(B, T):
    if T >= 512:
        return 1, 512
    return 512 // T, T


def _inproj_call(x3, sc, sh, wt, l):
    B, T, D = x3.shape
    bB, bT = _tok_tiles(B, T)
    nT = T // bT
    R = bB * bT
    N = B * T
    xmap = lambda i, j: (i, j, 0)
    mmap = lambda i, j: (i, 0, 0)
    omap = lambda i, j: (i * nT + j, 0)
    return pl.pallas_call(
        _inproj_kernel,
        grid=(B // bB, nT),
        in_specs=[pl.BlockSpec((bB, bT, D), xmap),
                  pl.BlockSpec((bB, 1, D), mmap),
                  pl.BlockSpec((bB, 1, D), mmap),
                  _resident_layer(wt.shape, l)],
        out_specs=[pl.BlockSpec((R, GLA_IN_W), omap),
                   pl.BlockSpec((R, RET_IN_W), omap),
                   pl.BlockSpec((R, SSD_IN_W), omap)],
        out_shape=[jax.ShapeDtypeStruct((N, GLA_IN_W), F32),
                   jax.ShapeDtypeStruct((N, RET_IN_W), F32),
                   jax.ShapeDtypeStruct((N, SSD_IN_W), F32)],
        scratch_shapes=[pltpu.VMEM((D, IN_W), BF)],
        compiler_params=_cp(2),
        name="in_proj",
    )(x3, sc, sh, wt)


def _head_block_mask(rows_per, cols_per, n):
    r = np.arange(rows_per * n)[:, None] // rows_per
    c = np.arange(cols_per * n)[None, :] // cols_per
    return (r == c).astype(np.float32)


def _block_tril(n, c):
    i = np.arange(n)[:, None]
    j = np.arange(n)[None, :]
    return ((i // c == j // c) & (j <= i)).astype(np.float32)


def _gla_front(x_ref, wg_ref, bg_ref, L_ref):
    q = x_ref[:, 0:128] * (GLA_DK ** -0.5)
    k = x_ref[:, 128:256]
    v = x_ref[:, 256:512]
    ga = x_ref[:, 512:640]
    r = x_ref[:, 640:896]
    gate = _dot(ga.astype(BF), wg_ref[...]) + bg_ref[...]
    la = _log_sigmoid(gate) * (1.0 / GLA_GATE_TEMP)
    n = L_ref.shape[0]
    g = jnp.concatenate([_dot_3x(L_ref[...], la[i:i + n, :]) for i in range(0, la.shape[0], n)], axis=0)
    return q, k, v, r, g


def _gla_intra(q, g, kp_ref, gp_ref, vp_ref, E_ref, c):
    TT = q.shape[0]
    PAD = kp_ref.shape[0] - TT
    pos = lax.broadcasted_iota(jnp.int32, (TT, 1), 0) & (c - 1)
    o = jnp.zeros((TT, 2 * LANE), F32)
    for s in range(c):
        ks = kp_ref[pl.ds(PAD - s, TT), :]
        gs = gp_ref[pl.ds(PAD - s, TT), :]
        vs = vp_ref[pl.ds(PAD - s, TT), :]
        w = jnp.where(pos >= s, q * ks * jnp.exp(g - gs), 0.0)
        o = o + _dot(w.astype(BF), E_ref[...]) * vs
    return o


def _gla_norm_gate(o, r, nw_ref, EA_ref):
    ms = _dot_x3(o * o, EA_ref[...])
    return o * lax.rsqrt(ms + EPS) * nw_ref[...] * _silu(r)


def _gla_prompt_kernel(x_ref, wg_ref, bg_ref, nw_ref, L_ref, E_ref, EA_ref, M_ref,
                       o_ref, sfin_ref, st_ref, kp_ref, gp_ref, vp_ref, oi_ref, u_ref, sb_ref, *, c):
    TT = x_ref.shape[0]
    nc = TT // c
    PAD = kp_ref.shape[0] - TT
    t = pl.program_id(1)

    @pl.when(t == 0)
    def _():
        st_ref[...] = jnp.zeros_like(st_ref)

    q, k, v, r, g = _gla_front(x_ref, wg_ref, bg_ref, L_ref)
    kp_ref[0:PAD, :] = jnp.zeros((PAD, LANE), F32)
    gp_ref[0:PAD, :] = jnp.zeros((PAD, LANE), F32)
    vp_ref[0:PAD, :] = jnp.zeros((PAD, 2 * LANE), F32)
    kp_ref[PAD:PAD + TT, :] = k
    gp_ref[PAD:PAD + TT, :] = g
    vp_ref[PAD:PAD + TT, :] = v
    o = _gla_intra(q, g, kp_ref, gp_ref, vp_ref, E_ref, c)

    M = M_ref[...]
    gl_all = gp_ref[pl.ds(PAD + c - 1, nc, stride=c), :]
    for n in range(nc):
        lo = n * c
        ke = (k[lo:lo + c, :] * jnp.exp(gl_all[n:n + 1, :] - g[lo:lo + c, :])).astype(BF)
        u_ref[n] = _dot_tn(ke, v[lo:lo + c, :].astype(BF)) * M
    a_cols = jnp.concatenate([jnp.exp(gl_all), jnp.zeros((LANE - nc, LANE), F32)], axis=0).T
    S = st_ref[...]
    for n in range(nc):
        sb_ref[n] = S.astype(BF)
        S = a_cols[:, n:n + 1] * S + u_ref[n]
    st_ref[...] = S
    qe = (q * jnp.exp(g)).astype(BF)
    for n in range(nc):
        lo = n * c
        oi_ref[lo:lo + c, :] = _dot(qe[lo:lo + c, :], sb_ref[n])
    o = o + oi_ref[...]
    o_ref[...] = _gla_norm_gate(o, r, nw_ref, EA_ref).astype(o_ref.dtype)

    @pl.when(t == pl.num_programs(1) - 1)
    def _():
        for h in range(GLA_H):
            sfin_ref[0, h] = S[h * GLA_DK:(h + 1) * GLA_DK, h * GLA_DV:(h + 1) * GLA_DV]


def _gla_tables(TT, c):
    L = jnp.asarray(_block_tril(TT, c), BF)
    E = jnp.asarray(_head_block_mask(GLA_DK, GLA_DV, GLA_H), BF)
    EA = jnp.asarray(_head_block_mask(GLA_DV, GLA_DV, GLA_H) / GLA_DV, BF)
    M = jnp.asarray(_head_block_mask(GLA_DK, GLA_DV, GLA_H), F32)
    return L, E, EA, M


def _gla_params(w_gate, b_gate, norm_w):
    wg = jnp.zeros((LANE, GLA_H * GLA_DK), F32).at[:GLA_GATE_RANK].set(w_gate).astype(BF)
    return wg, b_gate.reshape(1, -1), norm_w.reshape(1, -1)


def _const(shape):
    return pl.BlockSpec(shape, lambda *_: (0,) * len(shape))


def _gla_prompt_call(gin, B, T, w_gate, b_gate, norm_w):
    TT, c = 512, GLA_CHUNK
    nT = T // TT
    L, E, EA, M = _gla_tables(min(TT, 256), c)
    wg, bg, nw = _gla_params(w_gate, b_gate, norm_w)
    PAD = 16
    return pl.pallas_call(
        functools.partial(_gla_prompt_kernel, c=c),
        grid=(B, nT),
        in_specs=[pl.BlockSpec((TT, GLA_IN_W), lambda b, t: (b * nT + t, 0)),
                  _const(wg.shape), _const(bg.shape), _const(nw.shape),
                  _const(L.shape), _const(E.shape), _const(EA.shape), _const(M.shape)],
        out_specs=[pl.BlockSpec((TT, GLA_WIDTH), lambda b, t: (b * nT + t, 0)),
                   pl.BlockSpec((1, GLA_H, GLA_DK, GLA_DV), lambda b, t: (b, 0, 0, 0))],
        out_shape=[jax.ShapeDtypeStruct((B * T, GLA_WIDTH), BF),
                   jax.ShapeDtypeStruct((B, GLA_H, GLA_DK, GLA_DV), F32)],
        scratch_shapes=[pltpu.VMEM((GLA_H * GLA_DK, GLA_H * GLA_DV), F32),
                        pltpu.VMEM((TT + PAD, LANE), F32),
                        pltpu.VMEM((TT + PAD, LANE), F32),
                        pltpu.VMEM((TT + PAD, 2 * LANE), F32),
                        pltpu.VMEM((TT, 2 * LANE), F32),
                        pltpu.VMEM((TT // c, GLA_H * GLA_DK, GLA_H * GLA_DV), F32),
                        pltpu.VMEM((TT // c, GLA_H * GLA_DK, GLA_H * GLA_DV), BF)],
        compiler_params=_cp(2),
        name="gla_prompt",
    )(gin, wg, bg, nw, L, E, EA, M)


def _rope(x, cos, sin_signed):
    lane = lax.broadcasted_iota(jnp.int32, (1, LANE), 1)
    first_half = (lane & (RET_DK - 1)) < RET_DK // 2
    out = []
    for p in range(2):
        xs = x[:, p * LANE:(p + 1) * LANE]
        up = pltpu.roll(xs, LANE - RET_DK // 2, 1)
        dn = pltpu.roll(xs, RET_DK // 2, 1)
        out.append(xs * cos + jnp.where(first_half, up, dn) * sin_signed)
    return jnp.concatenate(out, axis=1)


def _ret_front(x_ref, cos_ref, sin_ref, rows=slice(None)):
    cos, sin = cos_ref[rows, :], sin_ref[rows, :]
    q = _rope(x_ref[rows, 0:256], cos, sin)
    k = _rope(x_ref[rows, 256:512], cos, sin) * (RET_DK ** -0.5)
    v = x_ref[rows, 512:768]
    rg = x_ref[rows, 768:1024]
    return q, k, v, rg


def _ret_intra(q, k, v, D_ref):
    lane = lax.broadcasted_iota(jnp.int32, (1, RET_WIDTH), 1)
    kb = k.astype(BF)
    o = jnp.zeros(q.shape, F32)
    for h in range(RET_H):
        hm = (lane // RET_DK) == h
        s = _dot_nt(jnp.where(hm, q, 0.0).astype(BF), kb)
        p = (s * D_ref[h]).astype(BF)
        o = o + _dot(p, jnp.where(hm, v, 0.0).astype(BF))
    return o


def _ret_norm_gate(o, rg, nw_ref, EA_ref):
    mu = _dot_x3(o, EA_ref[...])
    d = o - mu
    var = _dot_x3(d * d, EA_ref[...])
    return d * lax.rsqrt(var + EPS) * nw_ref[...] * _silu(rg)


def _ret_prompt_kernel(x_ref, cos_ref, sin_ref, D_ref, rd_ref, kd_ref, G_ref, M_ref, EA_ref, nw_ref,
                       o_ref, sfin_ref, st_ref):
    t = pl.program_id(1)

    @pl.when(t == 0)
    def _():
        st_ref[...] = jnp.zeros_like(st_ref)

    C = D_ref.shape[1]
    S = st_ref[...]
    for i in range(x_ref.shape[0] // C):
        rows = slice(i * C, (i + 1) * C)
        q, k, v, rg = _ret_front(x_ref, cos_ref, sin_ref, rows)
        o = _ret_intra(q, k, v, D_ref)
        o = o + _dot((q * rd_ref[...]).astype(BF), S.astype(BF))
        u = _dot_tn((k * kd_ref[...]).astype(BF), v.astype(BF))
        S = S * G_ref[...] + u * M_ref[...]
        o_ref[rows, :] = _ret_norm_gate(o, rg, nw_ref, EA_ref).astype(o_ref.dtype)
    st_ref[...] = S

    @pl.when(t == pl.num_programs(1) - 1)
    def _():
        for h in range(RET_H):
            sfin_ref[0, h] = S[h * RET_DK:(h + 1) * RET_DK, h * RET_DV:(h + 1) * RET_DV]


def _rope_tables(pos):
    half = RET_DK // 2
    inv = ROPE_BASE ** (-jnp.arange(half, dtype=F32) / half)
    ang = pos.astype(F32)[:, None] * inv[None, :]
    cos, sin = jnp.cos(ang), jnp.sin(ang)
    return jnp.tile(jnp.concatenate([cos, cos], 1), (1, 2)), jnp.tile(jnp.concatenate([-sin, sin], 1), (1, 2))


def _ret_log_gamma():
    return np.log(1.0 - 2.0 ** (-5.0 - np.arange(RET_H, dtype=np.float64)))


def _ret_prompt_call(rin, B, T, norm_w):
    C = 256
    TT = 4 * C if T % (4 * C) == 0 else C
    nT = T // TT
    cos, sin = _rope_tables(jnp.arange(T, dtype=jnp.int32))
    lg = _ret_log_gamma()
    i = np.arange(C)
    dec = np.exp(lg[:, None, None] * (i[:, None] - i[None, :])[None]) * (i[:, None] >= i[None, :])[None]
    Dm = jnp.asarray(dec, F32)
    rd = jnp.asarray(np.repeat(np.exp(lg[None, :] * (i[:, None] + 1)), RET_DK, 1), F32)
    kd = jnp.asarray(np.repeat(np.exp(lg[None, :] * (C - 1 - i[:, None])), RET_DK, 1), F32)
    M = _head_block_mask(RET_DK, RET_DV, RET_H)
    G = jnp.asarray(M * np.repeat(np.exp(lg * C), RET_DK)[:, None], F32)
    M = jnp.asarray(M, F32)
    EA = jnp.asarray(_head_block_mask(RET_DV, RET_DV, RET_H) / RET_DV, BF)
    nw = norm_w.reshape(1, -1)
    return pl.pallas_call(
        _ret_prompt_kernel,
        grid=(B, nT),
        in_specs=[pl.BlockSpec((TT, RET_IN_W), lambda b, t: (b * nT + t, 0)),
                  pl.BlockSpec((TT, LANE), lambda b, t: (t, 0)),
                  pl.BlockSpec((TT, LANE), lambda b, t: (t, 0)),
                  _const(Dm.shape), _const(rd.shape), _const(kd.shape), _const(G.shape), _const(M.shape),
                  _const(EA.shape), _const(nw.shape)],
        out_specs=[pl.BlockSpec((TT, RET_WIDTH), lambda b, t: (b * nT + t, 0)),
                   pl.BlockSpec((1, RET_H, RET_DK, RET_DV), lambda b, t: (b, 0, 0, 0))],
        out_shape=[jax.ShapeDtypeStruct((B * T, RET_WIDTH), BF),
                   jax.ShapeDtypeStruct((B, RET_H, RET_DK, RET_DV), F32)],
        scratch_shapes=[pltpu.VMEM((RET_H * RET_DK, RET_H * RET_DV), F32)],
        compiler_params=_cp(2),
        name="ret_prompt",
    )(rin, cos, sin, Dm, rd, kd, G, M, EA, nw)


def _ssd_conv(xp_ref, cw_ref, cb_ref, TT):
    acc = cb_ref[...] + cw_ref[SSD_CONV_W - 1:SSD_CONV_W, :] * xp_ref[pl.ds(8, TT), :]
    for i in range(SSD_CONV_W - 1):
        acc = acc + cw_ref[i:i + 1, :] * xp_ref[pl.ds(8 - (SSD_CONV_W - 1) + i, TT), :]
    return acc


def _ssd_intra(xs, bm, cm, g, dt, Mk_ref):
    TT = xs.shape[0]
    rT = (g - jnp.log(dt)).T
    lane = lax.broadcasted_iota(jnp.int32, (1, LANE), 1)
    lane2 = lax.broadcasted_iota(jnp.int32, (1, 2 * LANE), 1)
    causal = Mk_ref[...] > 0.0
    bmb = bm.astype(BF)
    zero = jnp.zeros((), BF)
    o_parts = []
    for grp in range(SSD_G):
        cb = _dot_nt(jnp.where((lane // SSD_N) == grp, cm, 0.0).astype(BF), bmb).astype(BF)
        xg = xs[:, grp * 2 * LANE:(grp + 1) * 2 * LANE].astype(BF)
        og = jnp.zeros((TT, 2 * LANE), F32)
        for h4 in range(SSD_H // SSD_G):
            h = grp * (SSD_H // SSD_G) + h4
            dec = jnp.where(causal, jnp.exp(g[:, h:h + 1] - rT[h:h + 1, :]), 0.0)
            p = cb * dec.astype(BF)
            og = og + _dot(p, jnp.where((lane2 // SSD_P) == h4, xg, zero))
        o_parts.append(og)
    return jnp.concatenate(o_parts, axis=1)


def _ssd_prompt_kernel(x_ref, cw_ref, cb_ref, dtb_ref, alog_ref, dexp_ref, nw_ref, L_ref, Mk_ref, Eexp_ref, M2_ref,
                       o_ref, sfin_ref, cfin_ref, st_ref, xp_ref):
    TT = x_ref.shape[0]
    t = pl.program_id(1)

    @pl.when(t == 0)
    def _():
        st_ref[...] = jnp.zeros_like(st_ref)
        xp_ref[0:8, :] = jnp.zeros((8, SSD_CONV_DIM), F32)

    z = x_ref[:, 0:SSD_WIDTH]
    xp_ref[8:8 + TT, :] = x_ref[:, SSD_WIDTH:SSD_WIDTH + SSD_CONV_DIM]
    sdt = x_ref[:, SSD_WIDTH + SSD_CONV_DIM:SSD_IN_W]
    xbc = _silu(_ssd_conv(xp_ref, cw_ref, cb_ref, TT))
    tail = xp_ref[TT:TT + 8, :]
    xp_ref[0:8, :] = tail
    dt_all = _softplus(sdt + dtb_ref[...])
    la_all = dt_all * (-jnp.exp(alog_ref[...]))
    Eexp = Eexp_ref[...]
    C = L_ref.shape[0]
    S = st_ref[...]
    for i in range(TT // C):
        rows = slice(i * C, (i + 1) * C)
        xs = xbc[rows, 0:SSD_WIDTH]
        bm = xbc[rows, SSD_WIDTH:SSD_WIDTH + LANE]
        cm = xbc[rows, SSD_WIDTH + LANE:SSD_CONV_DIM]
        dt = dt_all[rows, :]
        g = _dot_3x(L_ref[...], la_all[rows, :])
        gl = g[C - 1:C, :]
        eg_x = _dot_x2(jnp.exp(g), Eexp)
        cw_x = _dot_x2(dt * jnp.exp(gl - g), Eexp)
        egl_x = _dot_x2(jnp.exp(gl), Eexp)

        o = _ssd_intra(xs, bm, cm, g, dt, Mk_ref)
        o = o + eg_x * _dot(cm.astype(BF), S.astype(BF))
        u = _dot_tn(bm.astype(BF), (xs * cw_x).astype(BF))
        S = S * egl_x + u * M2_ref[...]

        y = (o + dexp_ref[...] * xs) * _silu(z[rows, :])
        ms = jnp.mean(y * y, axis=-1, keepdims=True)
        o_ref[rows, :] = (y * lax.rsqrt(ms + EPS) * nw_ref[...]).astype(o_ref.dtype)
    st_ref[...] = S

    @pl.when(t == pl.num_programs(1) - 1)
    def _():
        for h in range(SSD_H):
            gi = h // (SSD_H // SSD_G)
            sfin_ref[0, h] = S[gi * SSD_N:(gi + 1) * SSD_N, h * SSD_P:(h + 1) * SSD_P]
        cfin_ref[0] = tail[8 - (SSD_CONV_W - 1):8, :]


def _pad_lanes(v, n=LANE):
    v = v.reshape(1, -1)
    return jnp.zeros((1, n), F32).at[:, :v.shape[1]].set(v)


def _ssd_tables(TT, c):
    L = jnp.asarray(_block_tril(TT, c), BF)
    Mk = jnp.asarray(_block_tril(TT, c), F32)
    e = np.zeros((LANE, SSD_WIDTH), np.float32)
    for h in range(SSD_H):
        e[h, h * SSD_P:(h + 1) * SSD_P] = 1.0
    M2 = np.zeros((SSD_G * SSD_N, SSD_WIDTH), np.float32)
    for h in range(SSD_H):
        gi = h // (SSD_H // SSD_G)
        M2[gi * SSD_N:(gi + 1) * SSD_N, h * SSD_P:(h + 1) * SSD_P] = 1.0
    return L, Mk, jnp.asarray(e, BF), jnp.asarray(M2, F32)


def _ssd_params(conv_w, conv_b, dt_bias, a_log, d, norm_w):
    return (conv_w, conv_b.reshape(1, -1), _pad_lanes(dt_bias), _pad_lanes(a_log),
            jnp.repeat(d, SSD_P).reshape(1, -1), norm_w.reshape(1, -1))


def _ssd_prompt_call(sin_, B, T, conv_w, conv_b, dt_bias, a_log, d, norm_w):
    C = 256
    TT = 2 * C if T % (2 * C) == 0 else C
    nT = T // TT
    L, Mk, Eexp, M2 = _ssd_tables(C, C)
    prm = _ssd_params(conv_w, conv_b, dt_bias, a_log, d, norm_w)
    return pl.pallas_call(
        _ssd_prompt_kernel,
        grid=(B, nT),
        in_specs=[pl.BlockSpec((TT, SSD_IN_W), lambda b, t: (b * nT + t, 0))]
                 + [_const(p.shape) for p in prm]
                 + [_const(L.shape), _const(Mk.shape), _const(Eexp.shape), _const(M2.shape)],
        out_specs=[pl.BlockSpec((TT, SSD_WIDTH), lambda b, t: (b * nT + t, 0)),
                   pl.BlockSpec((1, SSD_H, SSD_N, SSD_P), lambda b, t: (b, 0, 0, 0)),
                   pl.BlockSpec((1, SSD_CONV_W - 1, SSD_CONV_DIM), lambda b, t: (b, 0, 0))],
        out_shape=[jax.ShapeDtypeStruct((B * T, SSD_WIDTH), BF),
                   jax.ShapeDtypeStruct((B, SSD_H, SSD_N, SSD_P), F32),
                   jax.ShapeDtypeStruct((B, SSD_CONV_W - 1, SSD_CONV_DIM), F32)],
        scratch_shapes=[pltpu.VMEM((SSD_G * SSD_N, SSD_WIDTH), F32),
                        pltpu.VMEM((TT + 8, SSD_CONV_DIM), F32)],
        compiler_params=_cp(2),
        name="ssd_prompt",
    )(sin_, *prm, L, Mk, Eexp, M2)


SEQ_TILE = 8


def _tile_lanes(n_rep, width):
    return np.tile(np.eye(width, dtype=np.float32), (1, n_rep))


def _fold_head_blocks(ubd):
    a = ubd[:, 0:LANE] + ubd[:, LANE:2 * LANE]
    return (a + pltpu.roll(a, LANE // 2, 1))[:, 0:LANE // 2]


def _col_bcast(row8, ones_ref):
    first = lax.broadcasted_iota(jnp.int32, (8, 1), 0) == 0
    hi, mid, lo = _split3(jnp.where(first, row8, 0.0))
    ones = ones_ref[...]
    return _dot_tn(hi, ones) + (_dot_tn(mid, ones) + _dot_tn(lo, ones))


def _gla_sample_kernel(x_ref, s0_ref, wg_ref, bg_ref, nw_ref, L_ref, E_ref, EA_ref, M_ref, T4_ref, ones_ref,
                       o_ref, sn_ref, kp_ref, gp_ref, vp_ref, oi_ref, *, c):
    TT = x_ref.shape[0]
    PAD = kp_ref.shape[0] - TT
    q, k, v, r, g = _gla_front(x_ref, wg_ref, bg_ref, L_ref)
    kp_ref[0:PAD, :] = jnp.zeros((PAD, LANE), F32)
    gp_ref[0:PAD, :] = jnp.zeros((PAD, LANE), F32)
    vp_ref[0:PAD, :] = jnp.zeros((PAD, 2 * LANE), F32)
    kp_ref[PAD:PAD + TT, :] = k
    gp_ref[PAD:PAD + TT, :] = g
    vp_ref[PAD:PAD + TT, :] = v
    o = _gla_intra(q, g, kp_ref, gp_ref, vp_ref, E_ref, c)
    qe = (q * jnp.exp(g)).astype(BF)
    M = M_ref[...]
    for s in range(TT // c):
        lo = s * c
        S0 = s0_ref[s].reshape(GLA_H * GLA_DK, GLA_DV)
        Sbd = (_dot(S0.astype(BF), T4_ref[...]) * M).astype(BF)
        oi_ref[lo:lo + c, :] = _dot(qe[lo:lo + c, :], Sbd)
        gl = g[lo + c - 1:lo + c, :]
        ke = (k[lo:lo + c, :] * jnp.exp(gl - g[lo:lo + c, :])).astype(BF)
        u = _fold_head_blocks(_dot_tn(ke, v[lo:lo + c, :].astype(BF)) * M)
        acol = _col_bcast(jnp.broadcast_to(jnp.exp(gl), (8, LANE)), ones_ref)
        sn_ref[s] = (acol * S0 + u).reshape(GLA_H, GLA_DK, GLA_DV)
    o = o + oi_ref[...]
    o_ref[...] = _gla_norm_gate(o, r, nw_ref, EA_ref).astype(o_ref.dtype)


def _gla_sample_call(gin, s0, B, T, w_gate, b_gate, norm_w):
    TT = SEQ_TILE * T
    L, E, EA, _ = _gla_tables(TT, T)
    M = jnp.asarray(_head_block_mask(GLA_DK, GLA_DV, GLA_H), F32)
    T4 = jnp.asarray(_tile_lanes(GLA_H, GLA_DV), BF)
    ones = jnp.ones((8, GLA_DV), BF)
    wg, bg, nw = _gla_params(w_gate, b_gate, norm_w)
    PAD = 8
    sspec = pl.BlockSpec((SEQ_TILE, GLA_H, GLA_DK, GLA_DV), lambda i: (i, 0, 0, 0))
    return pl.pallas_call(
        functools.partial(_gla_sample_kernel, c=T),
        grid=(B // SEQ_TILE,),
        in_specs=[pl.BlockSpec((TT, GLA_IN_W), lambda i: (i, 0)), sspec,
                  _const(wg.shape), _const(bg.shape), _const(nw.shape),
                  _const(L.shape), _const(E.shape), _const(EA.shape), _const(M.shape), _const(T4.shape),
                  _const(ones.shape)],
        out_specs=[pl.BlockSpec((TT, GLA_WIDTH), lambda i: (i, 0)), sspec],
        out_shape=[jax.ShapeDtypeStruct((B * T, GLA_WIDTH), BF),
                   jax.ShapeDtypeStruct((B, GLA_H, GLA_DK, GLA_DV), F32)],
        scratch_shapes=[pltpu.VMEM((TT + PAD, LANE), F32),
                        pltpu.VMEM((TT + PAD, LANE), F32),
                        pltpu.VMEM((TT + PAD, 2 * LANE), F32),
                        pltpu.VMEM((TT, 2 * LANE), F32)],
        compiler_params=_cp(1),
        name="gla_sample",
    )(gin, s0, wg, bg, nw, L, E, EA, M, T4, ones)


def _ret_sample_kernel(x_ref, s0_ref, cos_ref, sin_ref, D_ref, rd_ref, kd_ref, G_ref, M_ref, EA_ref, nw_ref, T4_ref,
                       o_ref, sn_ref, oi_ref, *, c):
    TT = x_ref.shape[0]
    q, k, v, rg = _ret_front(x_ref, cos_ref, sin_ref)
    o = _ret_intra(q, k, v, D_ref)
    qd = (q * rd_ref[...]).astype(BF)
    kd = (k * kd_ref[...]).astype(BF)
    vb = v.astype(BF)
    M = M_ref[...]
    for s in range(TT // c):
        lo = s * c
        S0 = s0_ref[s].reshape(RET_H * RET_DK, RET_DV)
        Sbd = (_dot(S0.astype(BF), T4_ref[...]) * M).astype(BF)
        oi_ref[lo:lo + c, :] = _dot(qd[lo:lo + c, :], Sbd)
        u = _fold_head_blocks(_dot_tn(kd[lo:lo + c, :], vb[lo:lo + c, :]) * M)
        sn_ref[s] = (G_ref[...] * S0 + u).reshape(RET_H, RET_DK, RET_DV)
    o = o + oi_ref[...]
    o_ref[...] = _ret_norm_gate(o, rg, nw_ref, EA_ref).astype(o_ref.dtype)


def _ret_sample_call(rin, s0, B, T, norm_w):
    TT = SEQ_TILE * T
    cos, sin = _rope_tables(PAST_LEN + jnp.arange(T, dtype=jnp.int32))
    cos, sin = jnp.tile(cos, (SEQ_TILE, 1)), jnp.tile(sin, (SEQ_TILE, 1))
    lg = _ret_log_gamma()
    i = np.arange(TT)
    same = (i[:, None] // T == i[None, :] // T) & (i[:, None] >= i[None, :])
    Dm = jnp.asarray(np.exp(lg[:, None, None] * (i[:, None] - i[None, :])[None]) * same[None], F32)
    tt = i % T
    rd = jnp.asarray(np.repeat(np.exp(lg[None, :] * (tt[:, None] + 1)), RET_DK, 1), F32)
    kd = jnp.asarray(np.repeat(np.exp(lg[None, :] * (T - 1 - tt[:, None])), RET_DK, 1), F32)
    G = jnp.asarray(np.repeat(np.repeat(np.exp(lg * T), RET_DK)[:, None], RET_DV, 1), F32)
    M = jnp.asarray(_head_block_mask(RET_DK, RET_DV, RET_H), F32)
    EA = jnp.asarray(_head_block_mask(RET_DV, RET_DV, RET_H) / RET_DV, BF)
    T4 = jnp.asarray(_tile_lanes(RET_H, RET_DV), BF)
    nw = norm_w.reshape(1, -1)
    sspec = pl.BlockSpec((SEQ_TILE, RET_H, RET_DK, RET_DV), lambda i: (i, 0, 0, 0))
    consts = (cos, sin, Dm, rd, kd, G, M, EA, nw, T4)
    return pl.pallas_call(
        functools.partial(_ret_sample_kernel, c=T),
        grid=(B // SEQ_TILE,),
        in_specs=[pl.BlockSpec((TT, RET_IN_W), lambda i: (i, 0)), sspec] + [_const(a.shape) for a in consts],
        out_specs=[pl.BlockSpec((TT, RET_WIDTH), lambda i: (i, 0)), sspec],
        out_shape=[jax.ShapeDtypeStruct((B * T, RET_WIDTH), BF),
                   jax.ShapeDtypeStruct((B, RET_H, RET_DK, RET_DV), F32)],
        scratch_shapes=[pltpu.VMEM((TT, RET_WIDTH), F32)],
        compiler_params=_cp(1),
        name="ret_sample",
    )(rin, s0, *consts)


def _ssd_sample_kernel(x_ref, c0_ref, s0_ref, cw_ref, cb_ref, dtb_ref, alog_ref, dexp_ref, nw_ref,
                       L_ref, Mk_ref, Eexp_ref, Bl_ref, R2_ref, T8_ref, T8T_ref, M8_ref, ones_ref,
                       o_ref, sn_ref, cn_ref, xp_ref, oi_ref, *, c):
    TT = x_ref.shape[0]
    ns = TT // c
    RP = 2 * c
    xp_ref[...] = jnp.zeros_like(xp_ref)
    z = x_ref[:, 0:SSD_WIDTH]
    sdt = x_ref[:, SSD_WIDTH + SSD_CONV_DIM:SSD_IN_W]
    for s in range(ns):
        base = 8 + s * RP
        xp_ref[base + c - (SSD_CONV_W - 1):base + c, :] = c0_ref[s]
        xp_ref[base + c:base + RP, :] = x_ref[s * c:(s + 1) * c, SSD_WIDTH:SSD_WIDTH + SSD_CONV_DIM]
    conv = _ssd_conv(xp_ref, cw_ref, cb_ref, ns * RP)
    xbc = _silu(conv.reshape(ns, RP, SSD_CONV_DIM)[:, c:RP, :].reshape(TT, SSD_CONV_DIM))
    for s in range(ns):
        base = 8 + s * RP
        cn_ref[s] = xp_ref[base + RP - (SSD_CONV_W - 1):base + RP, :]
    xs = xbc[:, 0:SSD_WIDTH]
    bm = xbc[:, SSD_WIDTH:SSD_WIDTH + LANE]
    cm = xbc[:, SSD_WIDTH + LANE:SSD_CONV_DIM]

    dt = _softplus(sdt + dtb_ref[...])
    la = dt * (-jnp.exp(alog_ref[...]))
    g = _dot_3x(L_ref[...], la)
    gl = _dot_3x(Bl_ref[...], g)
    Eexp = Eexp_ref[...]
    eg_x = _dot_x2(jnp.exp(g), Eexp)
    cw_x = _dot_x2(dt * jnp.exp(gl - g), Eexp)
    egl_x = _dot_x2(jnp.exp(gl), Eexp)
    o = _ssd_intra(xs, bm, cm, g, dt, Mk_ref)

    Cx = _dot(cm.astype(BF), R2_ref[...])
    Bx = _dot(bm.astype(BF), R2_ref[...])
    Xw = xs * cw_x
    M8 = M8_ref[...]
    nh = SSD_H

    def rows_by_head(a):
        return jnp.concatenate([a] * nh, axis=0) * M8

    for s in range(ns):
        lo = s * c
        S0 = s0_ref[s].reshape(SSD_H * SSD_N, SSD_P)
        oi = _dot(rows_by_head(Cx[lo:lo + c, :]).astype(BF), S0.astype(BF))
        oix = _dot_x2(oi, T8_ref[...]) * M8
        acc = oix[0:c, :]
        for h in range(1, nh):
            acc = acc + oix[h * c:(h + 1) * c, :]
        oi_ref[lo:lo + c, :] = acc
        Xst = _dot(rows_by_head(Xw[lo:lo + c, :]).astype(BF), T8T_ref[...])
        u = _dot_tn(rows_by_head(Bx[lo:lo + c, :]).astype(BF), Xst.astype(BF))
        acol = _col_bcast(egl_x[lo:lo + c, :], ones_ref)
        sn_ref[s] = (acol * S0 + u).reshape(SSD_H, SSD_N, SSD_P)

    o = o + eg_x * oi_ref[...]
    y = (o + dexp_ref[...] * xs) * _silu(z)
    ms = jnp.mean(y * y, axis=-1, keepdims=True)
    o_ref[...] = (y * lax.rsqrt(ms + EPS) * nw_ref[...]).astype(o_ref.dtype)


def _ssd_sample_call(sin_, c0, s0, B, T, conv_w, conv_b, dt_bias, a_log, d, norm_w):
    TT = SEQ_TILE * T
    L, Mk, Eexp, _ = _ssd_tables(TT, T)
    i = np.arange(TT)
    Bl = jnp.asarray((i[None, :] == (i[:, None] // T) * T + T - 1).astype(np.float32), BF)
    hpg = SSD_H // SSD_G
    R2 = np.zeros((LANE, SSD_H * SSD_N), np.float32)
    for h in range(SSD_H):
        R2[(h // hpg) * SSD_N:(h // hpg + 1) * SSD_N, h * SSD_N:(h + 1) * SSD_N] = np.eye(SSD_N)
    T8 = _tile_lanes(SSD_H, SSD_P)
    M8 = _head_block_mask(T, SSD_P, SSD_H)
    tabs = (L, Mk, Eexp, Bl, jnp.asarray(R2, BF), jnp.asarray(T8, BF), jnp.asarray(T8.T, BF), jnp.asarray(M8, F32),
            jnp.ones((8, SSD_P), BF))
    prm = _ssd_params(conv_w, conv_b, dt_bias, a_log, d, norm_w)
    sspec = pl.BlockSpec((SEQ_TILE, SSD_H, SSD_N, SSD_P), lambda i: (i, 0, 0, 0))
    cspec = pl.BlockSpec((SEQ_TILE, SSD_CONV_W - 1, SSD_CONV_DIM), lambda i: (i, 0, 0))
    return pl.pallas_call(
        functools.partial(_ssd_sample_kernel, c=T),
        grid=(B // SEQ_TILE,),
        in_specs=[pl.BlockSpec((TT, SSD_IN_W), lambda i: (i, 0)), cspec, sspec]
                 + [_const(p.shape) for p in prm] + [_const(a.shape) for a in tabs],
        out_specs=[pl.BlockSpec((TT, SSD_WIDTH), lambda i: (i, 0)), sspec, cspec],
        out_shape=[jax.ShapeDtypeStruct((B * T, SSD_WIDTH), BF),
                   jax.ShapeDtypeStruct((B, SSD_H, SSD_N, SSD_P), F32),
                   jax.ShapeDtypeStruct((B, SSD_CONV_W - 1, SSD_CONV_DIM), F32)],
        scratch_shapes=[pltpu.VMEM((8 + SEQ_TILE * 2 * T, SSD_CONV_DIM), F32),
                        pltpu.VMEM((TT, SSD_WIDTH), F32)],
        compiler_params=_cp(1),
        name="ssd_sample",
    )(sin_, c0, s0, *prm, *tabs)


def _inproj_t_kernel(x_ref, sc_ref, sh_ref, wt_ref, og_ref, or_ref, os_ref, w_ref):
    nt, nb, D = x_ref.shape

    @pl.when(pl.program_id(0) == 0)
    def _():
        for src, dst, n in ((0, 0, N_GA + GLA_GATE_RANK), (N_GA + GLA_GATE_RANK, N_GA + LANE, N_DT - N_GA - GLA_GATE_RANK)):
            for r in range(0, n, 512):
                m = min(512, n - r)
                w_ref[dst + r:dst + r + m, :] = wt_ref[src + r:src + r + m, :].astype(BF)
        w_ref[N_GA + GLA_GATE_RANK:N_GA + LANE, :] = jnp.zeros((LANE - GLA_GATE_RANK, D), BF)
        tail = jnp.concatenate([wt_ref[N_DT:N_IN, :], jnp.zeros((LANE - SSD_H, D), F32)], axis=0)
        w_ref[IN_W - LANE:IN_W, :] = tail.astype(BF)

    h = x_ref[...] * (1.0 + sc_ref[...]) + sh_ref[...]
    for t in range(nt):
        ht = h[t].astype(BF)
        cols = slice(t * nb, (t + 1) * nb)
        og_ref[:, cols] = _dot_nt(w_ref[0:GLA_IN_W, :], ht)
        or_ref[:, cols] = _dot_nt(w_ref[GLA_IN_W:GLA_IN_W + RET_IN_W, :], ht)
        os_ref[:, cols] = _dot_nt(w_ref[GLA_IN_W + RET_IN_W:IN_W, :], ht)


def _inproj_t_call(xt, sc, sh, wt, l):
    T, B, D = xt.shape
    nt = 4
    cmap = lambda i: (0, i)
    return pl.pallas_call(
        _inproj_t_kernel,
        grid=(T // nt,),
        in_specs=[pl.BlockSpec((nt, B, D), lambda i: (i, 0, 0)),
                  pl.BlockSpec((1, B, D), lambda i: (0, 0, 0)),
                  pl.BlockSpec((1, B, D), lambda i: (0, 0, 0)),
                  _resident_layer(wt.shape, l)],
        out_specs=[pl.BlockSpec((GLA_IN_W, nt * B), cmap),
                   pl.BlockSpec((RET_IN_W, nt * B), cmap),
                   pl.BlockSpec((SSD_IN_W, nt * B), cmap)],
        out_shape=[jax.ShapeDtypeStruct((GLA_IN_W, T * B), F32),
                   jax.ShapeDtypeStruct((RET_IN_W, T * B), F32),
                   jax.ShapeDtypeStruct((SSD_IN_W, T * B), F32)],
        scratch_shapes=[pltpu.VMEM((IN_W, D), BF)],
        compiler_params=_cp(1),
        name="in_proj_t",
    )(xt, sc, sh, wt)


def _row_sum(x):
    return jnp.sum(x, axis=0, keepdims=True)


def _lane_state_readout(o, coef_ref, s0_ref, n_rows):
    nb = LANE
    half = len(o) // 2
    for part in range(2):
        def body(k8, accs, part=part):
            accs = list(accs)
            base = pl.multiple_of(k8 * 8, 8)
            grp = [coef_ref[pl.ds(base, 8), (part * half + i) * nb:(part * half + i + 1) * nb] for i in range(half)]
            for j in range(8):
                s0k = s0_ref[0, k8 * 8 + j]
                for i in range(half):
                    accs[i] = accs[i] + grp[i][j:j + 1, :] * s0k
            return tuple(accs)

        res = lax.fori_loop(0, n_rows // 8, body, tuple(o[part * half:(part + 1) * half]))
        o[part * half:(part + 1) * half] = list(res)
    return o


def _lane_state_update(sn_ref, s0_ref, decay_fn, coef_ref, val_fn, n_rows, T):
    nb = LANE

    def body(k8, carry):
        base = pl.multiple_of(k8 * 8, 8)
        grp = [coef_ref[pl.ds(base, 8), t * nb:(t + 1) * nb] for t in range(T)]
        dec = decay_fn(base)
        for j in range(8):
            dj = dec[j:j + 1, :] if dec.shape[0] == 8 else dec
            sk = dj * s0_ref[0, k8 * 8 + j]
            for t in range(T):
                sk = sk + grp[t][j:j + 1, :] * val_fn(t)
            sn_ref[0, 0, k8 * 8 + j] = sk
        return carry

    lax.fori_loop(0, n_rows // 8, body, 0)


def _state_specs(shape, l, first):
    assert l == 0 or not first
    tail = tuple(shape[2:])
    in_spec = pl.BlockSpec((None, 1) + tail, lambda h: (l, h, 0, 0, 0))
    out_spec = pl.BlockSpec(((shape[0] if first else 1), 1) + tail, lambda h: (l, h, 0, 0, 0))
    return in_spec, out_spec


def _zero_later_layers(ref):
    ref[1:] = jnp.zeros((ref.shape[0] - 1,) + tuple(ref.shape[1:]), ref.dtype)


def _finish_state_call(kern, n_in, first, prevs):
    if first:
        return functools.partial(kern, first=True), [], {}
    wrapped = lambda *a, **kw: kern(*a[:n_in], *a[n_in + len(prevs):], first=False, **kw)
    specs = [pl.BlockSpec(memory_space=pl.ANY)] * len(prevs)
    return wrapped, specs, {n_in + i: 1 + i for i in range(len(prevs))}


def _gla_t_kernel(x_ref, s0_ref, wg_ref, bg_ref, nw_ref, o_ref, sn_ref, qe_ref, ke_ref, a_ref, *, T, first):
    nb = LANE
    if first:
        _zero_later_layers(sn_ref)
    h = pl.program_id(0)
    r0 = pl.multiple_of(h * GLA_DK, GLA_DK)
    v0 = pl.multiple_of(h * GLA_DV, GLA_DV)
    q = x_ref[pl.ds(r0, GLA_DK), :] * (GLA_DK ** -0.5)
    k = x_ref[pl.ds(128 + r0, GLA_DK), :]
    gate = _dot(wg_ref[pl.ds(r0, GLA_DK), :], x_ref[512:640, :].astype(BF)) + bg_ref[pl.ds(r0, GLA_DK), :]
    la = _log_sigmoid(gate) * (1.0 / GLA_GATE_TEMP)
    gs = []
    acc = jnp.zeros((GLA_DK, nb), F32)
    for t in range(T):
        acc = acc + la[:, t * nb:(t + 1) * nb]
        gs.append(acc)
    gl = gs[T - 1]
    a_ref[...] = jnp.exp(gl)
    qs = [q[:, t * nb:(t + 1) * nb] for t in range(T)]
    ks = [k[:, t * nb:(t + 1) * nb] for t in range(T)]
    for t in range(T):
        qe_ref[:, t * nb:(t + 1) * nb] = qs[t] * jnp.exp(gs[t])
        ke_ref[:, t * nb:(t + 1) * nb] = ks[t] * jnp.exp(gl - gs[t])

    def vt(t):
        return x_ref[pl.ds(256 + v0, GLA_DV), t * nb:(t + 1) * nb]

    o = []
    for t in range(T):
        ot = jnp.zeros((GLA_DV, nb), F32)
        for u in range(t + 1):
            s = _row_sum(qs[t] * ks[u] * jnp.exp(gs[t] - gs[u]))
            ot = ot + s * vt(u)
        o.append(ot)

    o = _lane_state_readout(o, qe_ref, s0_ref, GLA_DK)
    _lane_state_update(sn_ref, s0_ref, lambda base: a_ref[pl.ds(base, 8), :], ke_ref, vt, GLA_DK, T)

    nw = nw_ref[pl.ds(v0, GLA_DV), :]
    for t in range(T):
        ms = jnp.mean(o[t] * o[t], axis=0, keepdims=True)
        r = x_ref[pl.ds(640 + v0, GLA_DV), t * nb:(t + 1) * nb]
        o_ref[:, t * nb:(t + 1) * nb] = (o[t] * lax.rsqrt(ms + EPS) * nw * _silu(r)).astype(o_ref.dtype)


def _gla_t_call(gT, s0, prev, l, T, w_gate, b_gate, norm_w):
    N = gT.shape[1]
    wg = jnp.zeros((GLA_H * GLA_DK, LANE), F32).at[:, :GLA_GATE_RANK].set(w_gate.T).astype(BF)
    bg = b_gate.reshape(-1, 1)
    nw = norm_w.reshape(-1, 1)
    first = prev is None
    prevs = [] if first else [prev]
    s_in, s_out = _state_specs(s0.shape, l, first)
    ins = [gT, s0, wg, bg, nw]
    specs = [_const(gT.shape), s_in, _const(wg.shape), _const(bg.shape), _const(nw.shape)]
    kern, pspecs, aliases = _finish_state_call(functools.partial(_gla_t_kernel, T=T), len(ins), first, prevs)
    ins, specs = ins + prevs, specs + pspecs
    return pl.pallas_call(
        kern,
        grid=(GLA_H,),
        in_specs=specs,
        out_specs=[pl.BlockSpec((GLA_DV, N), lambda h: (h, 0)), s_out],
        out_shape=[jax.ShapeDtypeStruct((GLA_WIDTH, N), BF), jax.ShapeDtypeStruct(s0.shape, F32)],
        scratch_shapes=[pltpu.VMEM((GLA_DK, N), F32), pltpu.VMEM((GLA_DK, N), F32), pltpu.VMEM((GLA_DK, LANE), F32)],
        input_output_aliases=aliases,
        compiler_params=_cp(1),
        name="gla_t",
    )(*ins)


def _ret_t_kernel(x_ref, s0_ref, cos_ref, sin_ref, pw_ref, nw_ref, o_ref, sn_ref, qd_ref, kd_ref, *, T, first):
    nb = LANE
    if first:
        _zero_later_layers(sn_ref)
    h = pl.program_id(0)
    r0 = pl.multiple_of(h * RET_DK, RET_DK)
    half_k = RET_DK // 2
    cos, sin = cos_ref[...], sin_ref[...]

    def rope_t(base):
        x1 = x_ref[pl.ds(base + r0, half_k), :]
        x2 = x_ref[pl.ds(base + r0 + half_k, half_k), :]
        return jnp.concatenate([x1 * cos - x2 * sin, x1 * sin + x2 * cos], axis=0)

    q = rope_t(0)
    k = rope_t(256) * (RET_DK ** -0.5)
    pw = pw_ref[h]
    qs = [q[:, t * nb:(t + 1) * nb] for t in range(T)]
    ks = [k[:, t * nb:(t + 1) * nb] for t in range(T)]
    for t in range(T):
        qd_ref[:, t * nb:(t + 1) * nb] = qs[t] * pw[t + 1:t + 2, :]
        kd_ref[:, t * nb:(t + 1) * nb] = ks[t] * pw[T - 1 - t:T - t, :]

    def vt(t):
        return x_ref[pl.ds(512 + r0, RET_DV), t * nb:(t + 1) * nb]

    o = []
    for t in range(T):
        ot = jnp.zeros((RET_DV, nb), F32)
        for u in range(t + 1):
            s = _row_sum(qs[t] * ks[u]) * pw[t - u:t - u + 1, :]
            ot = ot + s * vt(u)
        o.append(ot)

    o = _lane_state_readout(o, qd_ref, s0_ref, RET_DK)
    _lane_state_update(sn_ref, s0_ref, lambda base: pw[T:T + 1, :], kd_ref, vt, RET_DK, T)

    nw = nw_ref[pl.ds(r0, RET_DV), :]
    for t in range(T):
        mu = jnp.mean(o[t], axis=0, keepdims=True)
        d = o[t] - mu
        var = jnp.mean(d * d, axis=0, keepdims=True)
        rg = x_ref[pl.ds(768 + r0, RET_DV), t * nb:(t + 1) * nb]
        o_ref[:, t * nb:(t + 1) * nb] = (d * lax.rsqrt(var + EPS) * nw * _silu(rg)).astype(o_ref.dtype)


def _ret_t_call(rT, s0, prev, l, T, norm_w):
    N = rT.shape[1]
    B = N // T
    half = RET_DK // 2
    inv = ROPE_BASE ** (-jnp.arange(half, dtype=F32) / half)
    ang = inv[:, None] * (PAST_LEN + jnp.arange(T, dtype=jnp.int32)).astype(F32)[None, :]
    cos = jnp.repeat(jnp.cos(ang), B, axis=1)
    sin = jnp.repeat(jnp.sin(ang), B, axis=1)
    lg = _ret_log_gamma()
    pw = jnp.asarray(np.repeat(np.exp(lg[:, None] * np.arange(16)[None, :])[:, :, None], LANE, axis=2), F32)
    nw = norm_w.reshape(-1, 1)
    first = prev is None
    prevs = [] if first else [prev]
    s_in, s_out = _state_specs(s0.shape, l, first)
    ins = [rT, s0, cos, sin, pw, nw]
    specs = [_const(rT.shape), s_in, _const(cos.shape), _const(sin.shape), _const(pw.shape), _const(nw.shape)]
    kern, pspecs, aliases = _finish_state_call(functools.partial(_ret_t_kernel, T=T), len(ins), first, prevs)
    ins, specs = ins + prevs, specs + pspecs
    return pl.pallas_call(
        kern,
        grid=(RET_H,),
        in_specs=specs,
        out_specs=[pl.BlockSpec((RET_DV, N), lambda h: (h, 0)), s_out],
        out_shape=[jax.ShapeDtypeStruct((RET_WIDTH, N), BF), jax.ShapeDtypeStruct(s0.shape, F32)],
        scratch_shapes=[pltpu.VMEM((RET_DK, N), F32), pltpu.VMEM((RET_DK, N), F32)],
        input_output_aliases=aliases,
        compiler_params=_cp(1),
        name="ret_t",
    )(*ins)


def _ssd_t_kernel(x_ref, c0_ref, s0_ref, cw_ref, cb_ref, dtb_ref, alog_ref, d_ref, nw_ref,
                  o_ref, sn_ref, cn_ref, hist_ref, y_ref, ssq_ref, cm_ref, bw_ref, xw_ref, *, T, first):
    nb = LANE
    W1 = SSD_CONV_W - 1
    h = pl.program_id(0)
    XB = SSD_WIDTH
    if first:
        _zero_later_layers(sn_ref)

    @pl.when(h == 0)
    def _():
        ssq_ref[...] = jnp.zeros_like(ssq_ref)
        if first:
            _zero_later_layers(cn_ref)
        for i in range(W1):
            for j in range(SSD_CONV_DIM // LANE):
                hist_ref[j * LANE:(j + 1) * LANE, i * nb:(i + 1) * nb] = c0_ref[0, i][:, j * LANE:(j + 1) * LANE].T
                cn_ref[0, i, :, j * LANE:(j + 1) * LANE] = \
                    x_ref[XB + j * LANE:XB + (j + 1) * LANE, (T - W1 + i) * nb:(T - W1 + i + 1) * nb].T

    def conv_rows(ro):
        w = cw_ref[pl.ds(ro, 64), :]
        b = cb_ref[pl.ds(ro, 64), :]
        xx = [hist_ref[pl.ds(ro, 64), i * nb:(i + 1) * nb] for i in range(W1)]
        xx += [x_ref[pl.ds(XB + ro, 64), t * nb:(t + 1) * nb] for t in range(T)]
        out = []
        for t in range(T):
            acc = b + w[:, 0:1] * xx[t]
            for i in range(1, SSD_CONV_W):
                acc = acc + w[:, i:i + 1] * xx[t + i]
            out.append(_silu(acc))
        return out

    grp = h // (SSD_H // SSD_G)
    xs = conv_rows(pl.multiple_of(h * SSD_P, SSD_P))
    bm = conv_rows(pl.multiple_of(SSD_WIDTH + grp * SSD_N, SSD_N))
    cm = conv_rows(pl.multiple_of(SSD_WIDTH + SSD_G * SSD_N + grp * SSD_N, SSD_N))

    dt_all = _softplus(x_ref[pl.ds(XB + SSD_CONV_DIM + h, 1), :] + dtb_ref[pl.ds(h, 1), :])
    a = -jnp.exp(alog_ref[pl.ds(h, 1), :])
    dts = [dt_all[:, t * nb:(t + 1) * nb] for t in range(T)]
    gs = []
    acc = jnp.zeros((1, nb), F32)
    for t in range(T):
        acc = acc + dts[t] * a
        gs.append(acc)
    gl = gs[T - 1]

    o = []
    for t in range(T):
        ot = jnp.zeros((SSD_P, nb), F32)
        for u in range(t + 1):
            s = _row_sum(cm[t] * bm[u]) * (jnp.exp(gs[t] - gs[u]) * dts[u])
            ot = ot + s * xs[u]
        o.append(ot)

    for t in range(T):
        cm_ref[:, t * nb:(t + 1) * nb] = cm[t] * jnp.exp(gs[t])
        bw_ref[:, t * nb:(t + 1) * nb] = bm[t]
        xw_ref[:, t * nb:(t + 1) * nb] = xs[t] * (dts[t] * jnp.exp(gl - gs[t]))

    o = _lane_state_readout(o, cm_ref, s0_ref, SSD_N)
    egl = jnp.exp(gl)
    _lane_state_update(sn_ref, s0_ref, lambda base: egl, bw_ref, lambda t: xw_ref[:, t * nb:(t + 1) * nb], SSD_N, T)

    dd = d_ref[pl.ds(h, 1), :]
    p0 = pl.multiple_of(h * SSD_P, SSD_P)
    for t in range(T):
        z = x_ref[pl.ds(p0, SSD_P), t * nb:(t + 1) * nb]
        y = (o[t] + dd * xs[t]) * _silu(z)
        y_ref[pl.ds(p0, SSD_P), t * nb:(t + 1) * nb] = y
        ssq_ref[:, t * nb:(t + 1) * nb] += _row_sum(y * y)

    @pl.when(h == SSD_H - 1)
    def _():
        scale = lax.rsqrt(ssq_ref[...] * (1.0 / SSD_WIDTH) + EPS)
        o_ref[...] = (y_ref[...] * scale * nw_ref[...]).astype(o_ref.dtype)


def _ssd_t_call(sT, c0, s0, prev_s, prev_c, l, T, conv_w, conv_b, dt_bias, a_log, d, norm_w):
    N = sT.shape[1]
    col = lambda v: jnp.zeros((LANE, 1), F32).at[:SSD_H, 0].set(v)
    prm = (conv_w.T, conv_b.reshape(-1, 1), col(dt_bias), col(a_log), col(d), norm_w.reshape(-1, 1))
    first = prev_s is None
    prevs = [] if first else [prev_s, prev_c]
    s_in, s_out = _state_specs(s0.shape, l, first)
    c_in = pl.BlockSpec((1,) + tuple(c0.shape[1:]), lambda h: (l, 0, 0, 0))
    c_out = pl.BlockSpec(((c0.shape[0] if first else 1),) + tuple(c0.shape[1:]), lambda h: (l, 0, 0, 0))
    ins = [sT, c0, s0, *prm]
    specs = [_const(sT.shape), c_in, s_in] + [_const(p.shape) for p in prm]
    kern, pspecs, aliases = _finish_state_call(functools.partial(_ssd_t_kernel, T=T), len(ins), first, prevs)
    ins, specs = ins + prevs, specs + pspecs
    return pl.pallas_call(
        kern,
        grid=(SSD_H,),
        in_specs=specs,
        out_specs=[_const((SSD_WIDTH, N)), s_out, c_out],
        out_shape=[jax.ShapeDtypeStruct((SSD_WIDTH, N), BF), jax.ShapeDtypeStruct(s0.shape, F32),
                   jax.ShapeDtypeStruct(c0.shape, F32)],
        scratch_shapes=[pltpu.VMEM((SSD_CONV_DIM, (SSD_CONV_W - 1) * LANE), F32),
                        pltpu.VMEM((SSD_WIDTH, N), F32), pltpu.VMEM((1, N), F32),
                        pltpu.VMEM((SSD_N, N), F32), pltpu.VMEM((SSD_N, N), F32), pltpu.VMEM((SSD_P, N), F32)],
        input_output_aliases=aliases,
        compiler_params=_cp(1),
        name="ssd_t",
    )(*ins)


def _outproj_t_kernel(x_ref, g_ref, og_ref, or_ref, os_ref, w_ref, lg_ref, lb_ref, o_ref):
    nt, nb, D = x_ref.shape
    for t in range(nt):
        cols = slice(t * nb, (t + 1) * nb)
        mix = (_dot_tn(og_ref[:, cols], w_ref[0:GLA_WIDTH, :])
               + _dot_tn(or_ref[:, cols], w_ref[GLA_WIDTH:GLA_WIDTH + RET_WIDTH, :])
               + _dot_tn(os_ref[:, cols], w_ref[GLA_WIDTH + RET_WIDTH:D, :]))
        y = ALPHA * x_ref[t] + g_ref[0] * mix
        o_ref[t] = _layer_norm(y, lg_ref[0], lb_ref[0])


def _outproj_t_call(xt, g1, ogT, orT, osT, w_out, ln_g, ln_b):
    T, B, D = xt.shape
    nt = 4
    cmap = lambda i: (0, i)
    return pl.pallas_call(
        _outproj_t_kernel,
        grid=(T // nt,),
        in_specs=[pl.BlockSpec((nt, B, D), lambda i: (i, 0, 0)),
                  pl.BlockSpec((1, B, D), lambda i: (0, 0, 0)),
                  pl.BlockSpec((GLA_WIDTH, nt * B), cmap),
                  pl.BlockSpec((RET_WIDTH, nt * B), cmap),
                  pl.BlockSpec((SSD_WIDTH, nt * B), cmap),
                  _const((D, D)), _const((1, 1, D)), _const((1, 1, D))],
        out_specs=pl.BlockSpec((nt, B, D), lambda i: (i, 0, 0)),
        out_shape=jax.ShapeDtypeStruct((T, B, D), F32),
        compiler_params=_cp(1),
        name="out_proj_ln_t",
    )(xt, g1, ogT, orT, osT, w_out, ln_g.reshape(1, 1, D), ln_b.reshape(1, 1, D))


def _outproj_kernel(x_ref, g_ref, og_ref, or_ref, os_ref, w_ref, lg_ref, lb_ref, o_ref):
    bB, bT, D = x_ref.shape
    assert bB == 1
    n_part = 2
    for i in range(n_part):
        rows = slice(i * bT // n_part, (i + 1) * bT // n_part)
        mix = (_dot(og_ref[rows, :], w_ref[0:GLA_WIDTH, :])
               + _dot(or_ref[rows, :], w_ref[GLA_WIDTH:GLA_WIDTH + RET_WIDTH, :])
               + _dot(os_ref[rows, :], w_ref[GLA_WIDTH + RET_WIDTH:D, :]))
        y = ALPHA * x_ref[0, rows, :] + g_ref[0] * mix
        o_ref[0, rows, :] = _layer_norm(y, lg_ref[0], lb_ref[0])


def _outproj_call(x3, g1, og, orr, os_, w_out, ln_g, ln_b):
    B, T, D = x3.shape
    bB, bT = _tok_tiles(B, T)
    nT = T // bT
    R = bB * bT
    xmap = lambda i, j: (i, j, 0)
    mmap = lambda i, j: (i, 0, 0)
    rmap = lambda i, j: (i * nT + j, 0)
    return pl.pallas_call(
        _outproj_kernel,
        grid=(B // bB, nT),
        in_specs=[pl.BlockSpec((bB, bT, D), xmap),
                  pl.BlockSpec((bB, 1, D), mmap),
                  pl.BlockSpec((R, GLA_WIDTH), rmap),
                  pl.BlockSpec((R, RET_WIDTH), rmap),
                  pl.BlockSpec((R, SSD_WIDTH), rmap),
                  _const((D, D)), _const((1, 1, D)), _const((1, 1, D))],
        out_specs=pl.BlockSpec((bB, bT, D), xmap),
        out_shape=jax.ShapeDtypeStruct((B, T, D), F32),
        compiler_params=_cp(2),
        name="out_proj_ln",
    )(x3, g1, og, orr, os_, w_out, ln_g.reshape(1, 1, D), ln_b.reshape(1, 1, D))


ROUTE_OFF = 8


def _moe_route_t(lt):
    R = lt.shape[1]
    neg = jnp.float32(-jnp.inf)
    row8 = lax.broadcasted_iota(jnp.int32, (8, 1), 0)
    lg = jnp.where(row8 < MOE_GROUPS, lt[0:8, :], neg)
    mg = jnp.max(lg, axis=0, keepdims=True)
    gsel = jnp.min(jnp.where(lg == mg, row8, 8), axis=0, keepdims=True)
    g_gate = 1.0 / jnp.sum(jnp.exp(lg - mg), axis=0, keepdims=True)
    rowe = lax.broadcasted_iota(jnp.int32, (MOE_EXPERTS, 1), 0)
    le = jnp.where((rowe // MOE_PER_GROUP) == gsel, lt[ROUTE_OFF:ROUTE_OFF + MOE_EXPERTS, :], neg)
    m1 = jnp.max(le, axis=0, keepdims=True)
    i1 = jnp.min(jnp.where(le == m1, rowe, MOE_EXPERTS), axis=0, keepdims=True)
    le2 = jnp.where(rowe == i1, neg, le)
    m2 = jnp.max(le2, axis=0, keepdims=True)
    i2 = jnp.min(jnp.where(le2 == m2, rowe, MOE_EXPERTS), axis=0, keepdims=True)
    e2 = jnp.exp(m2 - m1)
    w1 = g_gate / (1.0 + e2)
    w2 = g_gate * e2 / (1.0 + e2)
    comb = jnp.where(rowe == i1, w1, jnp.where(rowe == i2, w2, 0.0))
    cg = comb[0:4, :]
    for g in range(1, MOE_GROUPS):
        cg = cg + comb[g * MOE_PER_GROUP:(g + 1) * MOE_PER_GROUP, :]
    return gsel, cg, comb


MOE_SUB = 256
MOE_BLK = 16
MOE_ROWS = 256
MOE_NPS = MOE_SUB + MOE_GROUPS * MOE_BLK
MOE_MAXB = MOE_SUB // MOE_BLK + 1


def _moe_kernel(x_ref, sc_ref, sh_ref, g_ref, wr_ref, br_ref, us_ref, w1_ref, w3_ref, w2_ref, lg_ref, lb_ref,
                o_ref, hb_ref, cwb_ref, hp_ref, cwp_ref, yp_ref, pos_ref,
                cgrp_ref, fill_ref, cur_ref, na_ref, dst_ref, nb_ref, so_ref, *, n_steps):
    bB, bT, D = x_ref.shape
    R = bB * bT
    n_q = R // MOE_SUB
    s = pl.program_id(1)
    x = x_ref[...]
    row8 = lax.broadcasted_iota(jnp.int32, (8, 1), 0)
    slot = lax.broadcasted_iota(jnp.int32, (MOE_NPS, 1), 0).astype(F32)

    @pl.when((pl.program_id(0) == 0) & (s == 0))
    def _():
        hb_ref[...] = jnp.zeros_like(hb_ref)
        cwb_ref[...] = jnp.zeros_like(cwb_ref)
        yp_ref[...] = jnp.zeros_like(yp_ref)

    @pl.when(s == 0)
    def _():
        na_ref[0] = 0
        for g in range(MOE_GROUPS):
            cur_ref[g] = -1
            fill_ref[g] = 0

    @pl.when(s < n_steps)
    def _():
        h = (x * (1.0 + sc_ref[...]) + sh_ref[...]).reshape(R, D)
        segs, offs = [], []
        for q in range(n_q):
            u = s * n_q + q
            hq = h[q * MOE_SUB:(q + 1) * MOE_SUB, :].astype(BF)
            gsel, cg, _ = _moe_route_t(_dot_nt(wr_ref[...], hq) + br_ref[...])
            onehot = jnp.where(row8 == gsel, 1.0, 0.0)
            rank = _dot(onehot.astype(BF), us_ref[...])
            cnt = jnp.sum(onehot, axis=1, keepdims=True)
            seg = jnp.ceil(cnt * (1.0 / MOE_BLK)) * MOE_BLK
            off = jnp.zeros((8, 1), F32)
            for g in range(1, MOE_GROUPS):
                off = off + jnp.where(row8 >= g, seg[g - 1:g, :], 0.0)
            pos = jnp.sum(onehot * (off + rank), axis=0, keepdims=True)
            pos_ref[u] = jnp.broadcast_to(pos, (8, MOE_SUB))
            perm = jnp.where(slot == pos, 1.0, 0.0).astype(BF)
            hp_ref[q] = _dot(perm, hq).astype(BF)
            cg8 = jnp.concatenate([cg, jnp.zeros((4, MOE_SUB), F32)], axis=0)
            cg_hi = cg8.astype(BF)
            cg_lo = (cg8 - cg_hi.astype(F32)).astype(BF)
            cwp_ref[q] = _dot_nt(perm, cg_hi) + _dot_nt(perm, cg_lo)
            segs.append(seg)
            offs.append(off)
        for q in range(n_q):
            u = s * n_q + q
            for g in range(MOE_GROUPS):
                so = offs[q][g, 0].astype(jnp.int32)
                nb = (segs[q][g, 0] * (1.0 / MOE_BLK)).astype(jnp.int32)
                so_ref[u * MOE_GROUPS + g] = so
                nb_ref[u * MOE_GROUPS + g] = nb

                def put(k, carry, g=g, so=so, u=u, q=q):
                    f = fill_ref[g]
                    c = cur_ref[g]
                    na = na_ref[0]
                    fresh = (c < 0) | (f >= MOE_ROWS)
                    c = jnp.where(fresh, na, c)
                    f = jnp.where(fresh, 0, f)
                    cgrp_ref[c] = g
                    na_ref[0] = jnp.where(fresh, na + 1, na)
                    dst = pl.multiple_of(c * MOE_ROWS + f, MOE_BLK)
                    src = pl.multiple_of(so + k * MOE_BLK, MOE_BLK)
                    hb_ref[pl.ds(dst, MOE_BLK), :] = hp_ref[q, pl.ds(src, MOE_BLK), :]
                    cwb_ref[pl.ds(dst, MOE_BLK), :] = cwp_ref[q, pl.ds(src, MOE_BLK), :]
                    dst_ref[(u * MOE_GROUPS + g) * MOE_MAXB + k] = dst
                    cur_ref[g] = c
                    fill_ref[g] = f + MOE_BLK
                    return carry

                lax.fori_loop(0, nb, put, 0)

    @pl.when(s == n_steps - 1)
    def _():
        def chunk(c, carry):
            g = cgrp_ref[c]
            start = pl.multiple_of(c * MOE_ROWS, MOE_ROWS)
            hc = hb_ref[pl.ds(start, MOE_ROWS), :]
            cw = cwb_ref[pl.ds(start, MOE_ROWS), :]
            acc = jnp.zeros((MOE_ROWS, D), F32)
            for j in range(MOE_PER_GROUP):
                e = g * MOE_PER_GROUP + j
                hid = _silu(_dot(hc, w1_ref[e])) * _dot(hc, w3_ref[e]) * cw[:, j:j + 1]
                acc = acc + _dot(hid.astype(BF), w2_ref[e])
            hb_ref[pl.ds(start, MOE_ROWS), :] = acc.astype(BF)
            return carry

        lax.fori_loop(0, na_ref[0], chunk, 0)

    @pl.when(s >= n_steps)
    def _():
        for q in range(n_q):
            u = (s - n_steps) * n_q + q
            for g in range(MOE_GROUPS):
                so = so_ref[u * MOE_GROUPS + g]

                def take(k, carry, g=g, so=so, u=u, q=q):
                    src = pl.multiple_of(dst_ref[(u * MOE_GROUPS + g) * MOE_MAXB + k], MOE_BLK)
                    dst = pl.multiple_of(so + k * MOE_BLK, MOE_BLK)
                    yp_ref[q, pl.ds(dst, MOE_BLK), :] = hb_ref[pl.ds(src, MOE_BLK), :]
                    return carry

                lax.fori_loop(0, nb_ref[u * MOE_GROUPS + g], take, 0)
        ys = []
        for q in range(n_q):
            u = (s - n_steps) * n_q + q
            perm = jnp.where(slot == pos_ref[u][0:1, :], 1.0, 0.0).astype(BF)
            ys.append(_dot_tn(perm, yp_ref[q]))
        y = jnp.concatenate(ys, axis=0)
        z = ALPHA * x + g_ref[...] * y.reshape(bB, bT, D)
        o_ref[...] = _layer_norm(z, lg_ref[...], lb_ref[...])


def _resident_layer(shape, l):
    return pl.BlockSpec((None,) + tuple(shape[1:]), lambda *_: (l,) + (0,) * (len(shape) - 1),
                        pipeline_mode=pl.Buffered(1))


def _moe_call(x3, sc, sh, g2, wr, br, w1, w3, w2, l, ln_g, ln_b):
    B, T, D = x3.shape
    bB, bT = _tok_tiles(B, T)
    R = bB * bT
    if bB == 1:
        spp = 2 if B % 2 == 0 else 1
        nT = T // bT
        n_pools, n_steps = B // spp, spp * nT
        xmap = lambda p, s: (p * spp + (s % n_steps) // nT, (s % n_steps) % nT, 0)
        omap = lambda p, s: (p * spp + jnp.maximum(s - n_steps, 0) // nT, jnp.maximum(s - n_steps, 0) % nT, 0)
        mmap = lambda p, s: (p * spp + (s % n_steps) // nT, 0, 0)
        mshape = (1, 1, D)
    else:
        n_pools, n_steps = 1, B // bB
        xmap = lambda p, s: (s % n_steps, 0, 0)
        omap = lambda p, s: (jnp.maximum(s - n_steps, 0), 0, 0)
        mmap = lambda p, s: (0, 0, 0)
        mshape = (1, bT, D)
    n_sub = n_steps * (R // MOE_SUB)
    n_chunks = n_sub * MOE_SUB // MOE_ROWS + MOE_GROUPS
    us = jnp.asarray(np.triu(np.ones((MOE_SUB, MOE_SUB), np.float32), 1), BF)
    smem = lambda n: pltpu.SMEM((n,), jnp.int32)
    return pl.pallas_call(
        functools.partial(_moe_kernel, n_steps=n_steps),
        grid=(n_pools, 2 * n_steps),
        in_specs=[pl.BlockSpec((bB, bT, D), xmap),
                  pl.BlockSpec(mshape, mmap), pl.BlockSpec(mshape, mmap), pl.BlockSpec(mshape, mmap),
                  _const(wr.shape), _const(br.shape), _const(us.shape),
                  _resident_layer(w1.shape, l), _resident_layer(w3.shape, l), _resident_layer(w2.shape, l),
                  _const((1, 1, D)), _const((1, 1, D))],
        out_specs=pl.BlockSpec((bB, bT, D), omap),
        out_shape=jax.ShapeDtypeStruct((B, T, D), F32),
        scratch_shapes=[pltpu.VMEM((n_chunks * MOE_ROWS, D), BF), pltpu.VMEM((n_chunks * MOE_ROWS, 8), F32),
                        pltpu.VMEM((R // MOE_SUB, MOE_NPS, D), BF), pltpu.VMEM((R // MOE_SUB, MOE_NPS, 8), F32),
                        pltpu.VMEM((R // MOE_SUB, MOE_NPS, D), BF),
                        pltpu.VMEM((n_sub, 8, MOE_SUB), F32),
                        smem(n_chunks), smem(MOE_GROUPS), smem(MOE_GROUPS), smem(1),
                        smem(n_sub * MOE_GROUPS * MOE_MAXB), smem(n_sub * MOE_GROUPS), smem(n_sub * MOE_GROUPS)],
        compiler_params=_cp(2),
        name="moe_ln",
    )(x3, sc, sh, g2, wr, br, us, w1, w3, w2, ln_g.reshape(1, 1, D), ln_b.reshape(1, 1, D))


def _router_params(w_group, b_group, w_expert, b_expert):
    wr = jnp.zeros((LANE, D_MODEL), F32).at[:MOE_GROUPS].set(w_group.T)
    wr = wr.at[ROUTE_OFF:ROUTE_OFF + MOE_EXPERTS].set(w_expert.T)
    br = jnp.zeros((LANE, 1), F32).at[:MOE_GROUPS, 0].set(b_group).at[ROUTE_OFF:ROUTE_OFF + MOE_EXPERTS, 0].set(b_expert)
    return wr.astype(BF), br


def kernel(x_prompt, x_sample, c_prompt, c_sample, state_gla, state_ret, state_ssd, state_conv, w_ada, b_ada, w_in, gla_w_gate, gla_b_gate, gla_norm, ret_norm, ssd_conv_w, ssd_conv_b, ssd_dt_bias, ssd_a_log, ssd_d, ssd_norm, w_out, ln1_g, ln1_b, moe_w_group, moe_b_group, moe_w_expert, moe_b_expert, moe_w1, moe_w3, moe_w2, ln2_g, ln2_b):
    Bp, Tp, D = x_prompt.shape
    Bs, Ts, _ = x_sample.shape
    w_in_t = jnp.swapaxes(w_in, 1, 2)
    w_out_b = w_out.astype(BF)
    w1_b, w3_b, w2_b = moe_w1.astype(BF), moe_w3.astype(BF), moe_w2.astype(BF)

    mod = _mod_call(jnp.concatenate([c_prompt, c_sample], axis=0), w_ada, b_ada)

    def moe(x, sc2, sh2, g2, l):
        wr, br = _router_params(moe_w_group[l], moe_b_group[l], moe_w_expert[l], moe_b_expert[l])
        return _moe_call(x, sc2, sh2, g2, wr, br, w1_b, w3_b, w2_b, l, ln2_g[l], ln2_b[l])

    x = x_prompt
    new = [[], [], [], []]
    for l in range(DEPTH):
        sh1, sc1, g1, sh2, sc2, g2 = (mod[l, :Bp, None, i * D:(i + 1) * D] for i in range(6))
        gin, rin, sin_ = _inproj_call(x, sc1, sh1, w_in_t, l)
        og, s_gla = _gla_prompt_call(gin, Bp, Tp, gla_w_gate[l], gla_b_gate[l], gla_norm[l])
        orr, s_ret = _ret_prompt_call(rin, Bp, Tp, ret_norm[l])
        os_, s_ssd, s_conv = _ssd_prompt_call(sin_, Bp, Tp, ssd_conv_w[l], ssd_conv_b[l], ssd_dt_bias[l],
                                              ssd_a_log[l], ssd_d[l], ssd_norm[l])
        x = _outproj_call(x, g1, og, orr, os_, w_out_b[l], ln1_g[l], ln1_b[l])
        x = moe(x, sc2, sh2, g2, l)
        for acc, s in zip(new, (s_gla, s_ret, s_ssd, s_conv)):
            acc.append(s)
    y_p = x
    gla_p, ret_p, ssd_p, conv_p = (jnp.stack(a) for a in new)

    x = jnp.swapaxes(x_sample, 0, 1)
    sg = jnp.transpose(state_gla, (0, 2, 3, 4, 1))
    sr = jnp.transpose(state_ret, (0, 2, 3, 4, 1))
    ss = jnp.transpose(state_ssd, (0, 2, 3, 4, 1))
    cv = jnp.transpose(state_conv, (0, 2, 1, 3))
    gla_n = ret_n = ssd_n = conv_n = None
    for l in range(DEPTH):
        sh1, sc1, g1, sh2, sc2, g2 = (mod[l, None, Bp:, i * D:(i + 1) * D] for i in range(6))
        gT, rT, sT = _inproj_t_call(x, sc1, sh1, w_in_t, l)
        ogT, gla_n = _gla_t_call(gT, sg, gla_n, l, Ts, gla_w_gate[l], gla_b_gate[l], gla_norm[l])
        orT, ret_n = _ret_t_call(rT, sr, ret_n, l, Ts, ret_norm[l])
        osT, ssd_n, conv_n = _ssd_t_call(sT, cv, ss, ssd_n, conv_n, l, Ts, ssd_conv_w[l], ssd_conv_b[l],
                                         ssd_dt_bias[l], ssd_a_log[l], ssd_d[l], ssd_norm[l])
        x = _outproj_t_call(x, g1, ogT, orT, osT, w_out_b[l], ln1_g[l], ln1_b[l])
        x = moe(x, sc2, sh2, g2, l)
    y_s = jnp.swapaxes(x, 0, 1)
    gla_s = jnp.transpose(gla_n, (0, 4, 1, 2, 3))
    ret_s = jnp.transpose(ret_n, (0, 4, 1, 2, 3))
    ssd_s = jnp.transpose(ssd_n, (0, 4, 1, 2, 3))
    conv_s = jnp.transpose(conv_n, (0, 2, 1, 3))
    return (y_p, y_s, gla_p, ret_p, ssd_p, conv_p, gla_s, ret_s, ssd_s, conv_s)
```

```python
import functools
import math

import numpy as np
import jax
import jax.numpy as jnp
from jax import lax
from jax.experimental import pallas as pl
from jax.experimental.pallas import tpu as pltpu

F32 = jnp.float32
BF = jnp.bfloat16

D_MODEL = 1024
DEPTH = 2
PAST_LEN = 16384
GLA_H, GLA_DK, GLA_DV = 4, 32, 64
GLA_WIDTH = GLA_H * GLA_DV
GLA_GATE_RANK = 16
GLA_GATE_TEMP = 16.0
GLA_CHUNK = 16
RET_H, RET_DK, RET_DV = 4, 64, 64
RET_WIDTH = RET_H * RET_DV
ROPE_BASE = 10000.0
SSD_H, SSD_P, SSD_G, SSD_N = 8, 64, 2, 64
SSD_WIDTH = SSD_H * SSD_P
SSD_CONV_W = 4
SSD_CONV_DIM = SSD_WIDTH + 2 * SSD_G * SSD_N
MOE_GROUPS, MOE_PER_GROUP = 4, 4
MOE_EXPERTS = MOE_GROUPS * MOE_PER_GROUP
MOE_FF = 256
ALPHA = (2 * DEPTH) ** 0.25
EPS = 1e-5

LANE = 128
GLA_IN_W = 128 + 128 + 256 + LANE + 256
RET_IN_W = 4 * 256
SSD_IN_W = 512 + SSD_CONV_DIM + LANE
IN_W = GLA_IN_W + RET_IN_W + SSD_IN_W
VMEM_LIMIT = 56 * 1024 * 1024


def _cp(n_axes, vmem=VMEM_LIMIT):
    return pltpu.CompilerParams(dimension_semantics=("arbitrary",) * n_axes, vmem_limit_bytes=vmem)


def _dot(a, b):
    return jnp.dot(a, b, preferred_element_type=F32)


def _dot_nt(a, b):
    return lax.dot_general(a, b, (((1,), (1,)), ((), ())), preferred_element_type=F32)


def _dot_tn(a, b):
    return lax.dot_general(a, b, (((0,), (0,)), ((), ())), preferred_element_type=F32)


def _split3(x):
    hi = x.astype(BF)
    r = x - hi.astype(F32)
    mid = r.astype(BF)
    lo = (r - mid.astype(F32)).astype(BF)
    return hi, mid, lo


def _dot_x3(x, e):
    hi, mid, lo = _split3(x)
    return _dot(hi, e) + (_dot(mid, e) + _dot(lo, e))


def _dot_x2(x, e):
    hi = x.astype(BF)
    lo = (x - hi.astype(F32)).astype(BF)
    return _dot(hi, e) + _dot(lo, e)


def _dot_3x(e, x):
    hi, mid, lo = _split3(x)
    return _dot(e, hi) + (_dot(e, mid) + _dot(e, lo))


def _sigmoid(x):
    return 1.0 / (1.0 + jnp.exp(-x))


def _silu(x):
    return x * _sigmoid(x)


def _log_sigmoid(x):
    return jnp.minimum(x, 0.0) - jnp.log(1.0 + jnp.exp(-jnp.abs(x)))


def _softplus(x):
    return jnp.maximum(x, 0.0) + jnp.log(1.0 + jnp.exp(-jnp.abs(x)))


def _layer_norm(x, g, b):
    mu = jnp.mean(x, axis=-1, keepdims=True)
    d = x - mu
    var = jnp.mean(d * d, axis=-1, keepdims=True)
    return d * lax.rsqrt(var + EPS) * g + b


def _mod_kernel(c_ref, w_ref, b_ref, o_ref):
    s = _silu(c_ref[...]).astype(BF)
    o_ref[0] = _dot(s, w_ref[0].astype(BF)) + b_ref[0]


def _mod_call(c_all, w_ada, b_ada):
    R = c_all.shape[0]
    tn = 1536
    return pl.pallas_call(
        _mod_kernel,
        grid=(DEPTH, 6 * D_MODEL // tn),
        in_specs=[pl.BlockSpec((R, D_MODEL), lambda l, j: (0, 0)),
                  pl.BlockSpec((1, D_MODEL, tn), lambda l, j: (l, 0, j)),
                  pl.BlockSpec((1, 1, tn), lambda l, j: (l, 0, j))],
        out_specs=pl.BlockSpec((1, R, tn), lambda l, j: (l, 0, j)),
        out_shape=jax.ShapeDtypeStruct((DEPTH, R, 6 * D_MODEL), F32),
        compiler_params=_cp(2),
        name="ada_mod",
    )(c_all, w_ada, b_ada.reshape(DEPTH, 1, 6 * D_MODEL))


N_IN = 3096
N_GA = 128 + 128 + 256
N_DT = N_IN - SSD_H


def _inproj_kernel(x_ref, sc_ref, sh_ref, wt_ref, og_ref, or_ref, os_ref, w_ref):
    bB, bT, D = x_ref.shape

    @pl.when((pl.program_id(0) == 0) & (pl.program_id(1) == 0))
    def _():
        lane = lax.broadcasted_iota(jnp.int32, (1, LANE), 1)
        for j in range(N_GA // LANE):
            w_ref[:, j * LANE:(j + 1) * LANE] = wt_ref[j * LANE:(j + 1) * LANE, :].T.astype(BF)
        ga = wt_ref[N_GA:N_GA + LANE, :].T
        w_ref[:, N_GA:N_GA + LANE] = jnp.where(lane < GLA_GATE_RANK, ga, 0.0).astype(BF)
        src0, dst0 = N_GA + GLA_GATE_RANK, N_GA + LANE
        for j in range((N_DT - src0) // LANE):
            w_ref[:, dst0 + j * LANE:dst0 + (j + 1) * LANE] = \
                wt_ref[src0 + j * LANE:src0 + (j + 1) * LANE, :].T.astype(BF)
        dt = pltpu.roll(wt_ref[N_IN - LANE:N_IN, :].T, SSD_H, 1)
        w_ref[:, IN_W - LANE:IN_W] = jnp.where(lane < SSD_H, dt, 0.0).astype(BF)

    h = x_ref[...] * (1.0 + sc_ref[...]) + sh_ref[...]
    hb = h.reshape(bB * bT, D).astype(BF)
    og_ref[...] = _dot(hb, w_ref[:, 0:GLA_IN_W])
    or_ref[...] = _dot(hb, w_ref[:, GLA_IN_W:GLA_IN_W + RET_IN_W])
    os_ref[...] = _dot(hb, w_ref[:, GLA_IN_W + RET_IN_W:IN_W])


def _tok_tiles(B, T):
    if T >= 512:
        return 1, 512
    return 512 // T, T


def _inproj_call(x3, sc, sh, wt, l):
    B, T, D = x3.shape
    bB, bT = _tok_tiles(B, T)
    nT = T // bT
    R = bB * bT
    N = B * T
    xmap = lambda i, j: (i, j, 0)
    mmap = lambda i, j: (i, 0, 0)
    omap = lambda i, j: (i * nT + j, 0)
    return pl.pallas_call(
        _inproj_kernel,
        grid=(B // bB, nT),
        in_specs=[pl.BlockSpec((bB, bT, D), xmap),
                  pl.BlockSpec((bB, 1, D), mmap),
                  pl.BlockSpec((bB, 1, D), mmap),
                  _resident_layer(wt.shape, l)],
        out_specs=[pl.BlockSpec((R, GLA_IN_W), omap),
                   pl.BlockSpec((R, RET_IN_W), omap),
                   pl.BlockSpec((R, SSD_IN_W), omap)],
        out_shape=[jax.ShapeDtypeStruct((N, GLA_IN_W), F32),
                   jax.ShapeDtypeStruct((N, RET_IN_W), F32),
                   jax.ShapeDtypeStruct((N, SSD_IN_W), F32)],
        scratch_shapes=[pltpu.VMEM((D, IN_W), BF)],
        compiler_params=_cp(2),
        name="in_proj",
    )(x3, sc, sh, wt)


def _head_block_mask(rows_per, cols_per, n):
    r = np.arange(rows_per * n)[:, None] // rows_per
    c = np.arange(cols_per * n)[None, :] // cols_per
    return (r == c).astype(np.float32)


def _block_tril(n, c):
    i = np.arange(n)[:, None]
    j = np.arange(n)[None, :]
    return ((i // c == j // c) & (j <= i)).astype(np.float32)


def _gla_front(x_ref, wg_ref, bg_ref, L_ref):
    q = x_ref[:, 0:128] * (GLA_DK ** -0.5)
    k = x_ref[:, 128:256]
    v = x_ref[:, 256:512]
    ga = x_ref[:, 512:640]
    r = x_ref[:, 640:896]
    gate = _dot(ga.astype(BF), wg_ref[...]) + bg_ref[...]
    la = _log_sigmoid(gate) * (1.0 / GLA_GATE_TEMP)
    n = L_ref.shape[0]
    g = jnp.concatenate([_dot_3x(L_ref[...], la[i:i + n, :]) for i in range(0, la.shape[0], n)], axis=0)
    return q, k, v, r, g


def _gla_intra(q, g, kp_ref, gp_ref, vp_ref, E_ref, c):
    TT = q.shape[0]
    PAD = kp_ref.shape[0] - TT
    pos = lax.broadcasted_iota(jnp.int32, (TT, 1), 0) & (c - 1)
    o = jnp.zeros((TT, 2 * LANE), F32)
    for s in range(min(c, 8)):
        ks = kp_ref[pl.ds(PAD - s, TT), :]
        gs = gp_ref[pl.ds(PAD - s, TT), :]
        vs = vp_ref[pl.ds(PAD - s, TT), :]
        w = jnp.where(pos >= s, q * ks * jnp.exp(g - gs), 0.0)
        o = o + _dot(w.astype(BF), E_ref[...]) * vs
    if c <= 8:
        return o
    assert c == 16
    nc = TT // c

    def upper(x):
        return x.reshape(nc, 2, 8, x.shape[-1])[:, 1].reshape(nc * 8, x.shape[-1])

    qu, gu = upper(q), upper(g)
    posu = lax.broadcasted_iota(jnp.int32, (nc * 8, 1), 0) & 7
    ou = jnp.zeros((nc * 8, 2 * LANE), F32)
    for s in range(8, c):
        ks = upper(kp_ref[pl.ds(PAD - s, TT), :])
        gs = upper(gp_ref[pl.ds(PAD - s, TT), :])
        vs = upper(vp_ref[pl.ds(PAD - s, TT), :])
        w = jnp.where(posu >= s - 8, qu * ks * jnp.exp(gu - gs), 0.0)
        ou = ou + _dot(w.astype(BF), E_ref[...]) * vs
    ou = ou.reshape(nc, 1, 8, 2 * LANE)
    return o + jnp.concatenate([jnp.zeros_like(ou), ou], axis=1).reshape(TT, 2 * LANE)


def _gla_norm_gate(o, r, nw_ref, EA_ref):
    ms = _dot_x3(o * o, EA_ref[...])
    return o * lax.rsqrt(ms + EPS) * nw_ref[...] * _silu(r)


def _gla_prompt_kernel(x_ref, wg_ref, bg_ref, nw_ref, L_ref, E_ref, EA_ref, M_ref,
                       o_ref, sfin_ref, st_ref, kp_ref, gp_ref, vp_ref, oi_ref, u_ref, sb_ref, *, c):
    TT = x_ref.shape[0]
    nc = TT // c
    PAD = kp_ref.shape[0] - TT
    t = pl.program_id(1)

    @pl.when(t == 0)
    def _():
        st_ref[...] = jnp.zeros_like(st_ref)

    q, k, v, r, g = _gla_front(x_ref, wg_ref, bg_ref, L_ref)
    kp_ref[0:PAD, :] = jnp.zeros((PAD, LANE), F32)
    gp_ref[0:PAD, :] = jnp.zeros((PAD, LANE), F32)
    vp_ref[0:PAD, :] = jnp.zeros((PAD, 2 * LANE), F32)
    kp_ref[PAD:PAD + TT, :] = k
    gp_ref[PAD:PAD + TT, :] = g
    vp_ref[PAD:PAD + TT, :] = v
    o = _gla_intra(q, g, kp_ref, gp_ref, vp_ref, E_ref, c)

    M = M_ref[...]
    gl_all = gp_ref[pl.ds(PAD + c - 1, nc, stride=c), :]
    for n in range(nc):
        lo = n * c
        ke = (k[lo:lo + c, :] * jnp.exp(gl_all[n:n + 1, :] - g[lo:lo + c, :])).astype(BF)
        u_ref[n] = _dot_tn(ke, v[lo:lo + c, :].astype(BF)) * M
    a_cols = jnp.concatenate([jnp.exp(gl_all), jnp.zeros((LANE - nc, LANE), F32)], axis=0).T
    S = st_ref[...]
    for n in range(nc):
        sb_ref[n] = S.astype(BF)
        S = a_cols[:, n:n + 1] * S + u_ref[n]
    st_ref[...] = S
    qe = (q * jnp.exp(g)).astype(BF)
    for n in range(nc):
        lo = n * c
        oi_ref[lo:lo + c, :] = _dot(qe[lo:lo + c, :], sb_ref[n])
    o = o + oi_ref[...]
    o_ref[...] = _gla_norm_gate(o, r, nw_ref, EA_ref).astype(o_ref.dtype)

    @pl.when(t == pl.num_programs(1) - 1)
    def _():
        for h in range(GLA_H):
            sfin_ref[0, h] = S[h * GLA_DK:(h + 1) * GLA_DK, h * GLA_DV:(h + 1) * GLA_DV]


def _gla_tables(TT, c):
    L = jnp.asarray(_block_tril(TT, c), BF)
    E = jnp.asarray(_head_block_mask(GLA_DK, GLA_DV, GLA_H), BF)
    EA = jnp.asarray(_head_block_mask(GLA_DV, GLA_DV, GLA_H) / GLA_DV, BF)
    M = jnp.asarray(_head_block_mask(GLA_DK, GLA_DV, GLA_H), F32)
    return L, E, EA, M


def _gla_params(w_gate, b_gate, norm_w):
    wg = jnp.zeros((LANE, GLA_H * GLA_DK), F32).at[:GLA_GATE_RANK].set(w_gate).astype(BF)
    return wg, b_gate.reshape(1, -1), norm_w.reshape(1, -1)


def _const(shape):
    return pl.BlockSpec(shape, lambda *_: (0,) * len(shape))


def _gla_prompt_call(gin, B, T, w_gate, b_gate, norm_w):
    TT, c = (1024 if T % 1024 == 0 else 512), GLA_CHUNK
    nT = T // TT
    L, E, EA, M = _gla_tables(min(TT, 256), c)
    wg, bg, nw = _gla_params(w_gate, b_gate, norm_w)
    PAD = 16
    return pl.pallas_call(
        functools.partial(_gla_prompt_kernel, c=c),
        grid=(B, nT),
        in_specs=[pl.BlockSpec((TT, GLA_IN_W), lambda b, t: (b * nT + t, 0)),
                  _const(wg.shape), _const(bg.shape), _const(nw.shape),
                  _const(L.shape), _const(E.shape), _const(EA.shape), _const(M.shape)],
        out_specs=[pl.BlockSpec((TT, GLA_WIDTH), lambda b, t: (b * nT + t, 0)),
                   pl.BlockSpec((1, GLA_H, GLA_DK, GLA_DV), lambda b, t: (b, 0, 0, 0))],
        out_shape=[jax.ShapeDtypeStruct((B * T, GLA_WIDTH), BF),
                   jax.ShapeDtypeStruct((B, GLA_H, GLA_DK, GLA_DV), F32)],
        scratch_shapes=[pltpu.VMEM((GLA_H * GLA_DK, GLA_H * GLA_DV), F32),
                        pltpu.VMEM((TT + PAD, LANE), F32),
                        pltpu.VMEM((TT + PAD, LANE), F32),
                        pltpu.VMEM((TT + PAD, 2 * LANE), F32),
                        pltpu.VMEM((TT, 2 * LANE), F32),
                        pltpu.VMEM((TT // c, GLA_H * GLA_DK, GLA_H * GLA_DV), F32),
                        pltpu.VMEM((TT // c, GLA_H * GLA_DK, GLA_H * GLA_DV), BF)],
        compiler_params=_cp(2),
        name="gla_prompt",
    )(gin, wg, bg, nw, L, E, EA, M)


def _rope(x, cos, sin_signed):
    lane = lax.broadcasted_iota(jnp.int32, (1, LANE), 1)
    first_half = (lane & (RET_DK - 1)) < RET_DK // 2
    out = []
    for p in range(2):
        xs = x[:, p * LANE:(p + 1) * LANE]
        up = pltpu.roll(xs, LANE - RET_DK // 2, 1)
        dn = pltpu.roll(xs, RET_DK // 2, 1)
        out.append(xs * cos + jnp.where(first_half, up, dn) * sin_signed)
    return jnp.concatenate(out, axis=1)


def _ret_front(x_ref, cos_ref, sin_ref, rows=slice(None)):
    cos, sin = cos_ref[rows, :], sin_ref[rows, :]
    q = _rope(x_ref[rows, 0:256], cos, sin)
    k = _rope(x_ref[rows, 256:512], cos, sin) * (RET_DK ** -0.5)
    v = x_ref[rows, 512:768]
    rg = x_ref[rows, 768:1024]
    return q, k, v, rg


def _ret_intra(q, k, v, D_ref):
    lane = lax.broadcasted_iota(jnp.int32, (1, RET_WIDTH), 1)
    kb = k.astype(BF)
    o = jnp.zeros(q.shape, F32)
    for h in range(RET_H):
        hm = (lane // RET_DK) == h
        s = _dot_nt(jnp.where(hm, q, 0.0).astype(BF), kb)
        p = (s * D_ref[h]).astype(BF)
        o = o + _dot(p, jnp.where(hm, v, 0.0).astype(BF))
    return o


def _ret_norm_gate(o, rg, nw_ref, EA_ref):
    mu = _dot_x3(o, EA_ref[...])
    d = o - mu
    var = _dot_x3(d * d, EA_ref[...])
    return d * lax.rsqrt(var + EPS) * nw_ref[...] * _silu(rg)


def _ret_prompt_kernel(x_ref, cos_ref, sin_ref, D_ref, rd_ref, kd_ref, G_ref, M_ref, EA_ref, nw_ref,
                       o_ref, sfin_ref, st_ref):
    t = pl.program_id(1)

    @pl.when(t == 0)
    def _():
        st_ref[...] = jnp.zeros_like(st_ref)

    C = D_ref.shape[1]
    S = st_ref[...]
    for i in range(x_ref.shape[0] // C):
        rows = slice(i * C, (i + 1) * C)
        q, k, v, rg = _ret_front(x_ref, cos_ref, sin_ref, rows)
        o = _ret_intra(q, k, v, D_ref)
        o = o + _dot((q * rd_ref[...]).astype(BF), S.astype(BF))
        u = _dot_tn((k * kd_ref[...]).astype(BF), v.astype(BF))
        S = S * G_ref[...] + u * M_ref[...]
        o_ref[rows, :] = _ret_norm_gate(o, rg, nw_ref, EA_ref).astype(o_ref.dtype)
    st_ref[...] = S

    @pl.when(t == pl.num_programs(1) - 1)
    def _():
        for h in range(RET_H):
            sfin_ref[0, h] = S[h * RET_DK:(h + 1) * RET_DK, h * RET_DV:(h + 1) * RET_DV]


def _rope_tables(pos):
    half = RET_DK // 2
    inv = ROPE_BASE ** (-jnp.arange(half, dtype=F32) / half)
    ang = pos.astype(F32)[:, None] * inv[None, :]
    cos, sin = jnp.cos(ang), jnp.sin(ang)
    return jnp.tile(jnp.concatenate([cos, cos], 1), (1, 2)), jnp.tile(jnp.concatenate([-sin, sin], 1), (1, 2))


def _ret_log_gamma():
    return np.log(1.0 - 2.0 ** (-5.0 - np.arange(RET_H, dtype=np.float64)))


def _ret_prompt_call(rin, B, T, norm_w):
    C = 256
    TT = 4 * C if T % (4 * C) == 0 else C
    nT = T // TT
    cos, sin = _rope_tables(jnp.arange(T, dtype=jnp.int32))
    lg = _ret_log_gamma()
    i = np.arange(C)
    dec = np.exp(lg[:, None, None] * (i[:, None] - i[None, :])[None]) * (i[:, None] >= i[None, :])[None]
    Dm = jnp.asarray(dec, F32)
    rd = jnp.asarray(np.repeat(np.exp(lg[None, :] * (i[:, None] + 1)), RET_DK, 1), F32)
    kd = jnp.asarray(np.repeat(np.exp(lg[None, :] * (C - 1 - i[:, None])), RET_DK, 1), F32)
    M = _head_block_mask(RET_DK, RET_DV, RET_H)
    G = jnp.asarray(M * np.repeat(np.exp(lg * C), RET_DK)[:, None], F32)
    M = jnp.asarray(M, F32)
    EA = jnp.asarray(_head_block_mask(RET_DV, RET_DV, RET_H) / RET_DV, BF)
    nw = norm_w.reshape(1, -1)
    return pl.pallas_call(
        _ret_prompt_kernel,
        grid=(B, nT),
        in_specs=[pl.BlockSpec((TT, RET_IN_W), lambda b, t: (b * nT + t, 0)),
                  pl.BlockSpec((TT, LANE), lambda b, t: (t, 0)),
                  pl.BlockSpec((TT, LANE), lambda b, t: (t, 0)),
                  _const(Dm.shape), _const(rd.shape), _const(kd.shape), _const(G.shape), _const(M.shape),
                  _const(EA.shape), _const(nw.shape)],
        out_specs=[pl.BlockSpec((TT, RET_WIDTH), lambda b, t: (b * nT + t, 0)),
                   pl.BlockSpec((1, RET_H, RET_DK, RET_DV), lambda b, t: (b, 0, 0, 0))],
        out_shape=[jax.ShapeDtypeStruct((B * T, RET_WIDTH), BF),
                   jax.ShapeDtypeStruct((B, RET_H, RET_DK, RET_DV), F32)],
        scratch_shapes=[pltpu.VMEM((RET_H * RET_DK, RET_H * RET_DV), F32)],
        compiler_params=_cp(2),
        name="ret_prompt",
    )(rin, cos, sin, Dm, rd, kd, G, M, EA, nw)


def _ssd_conv(xp_ref, cw_ref, cb_ref, TT):
    acc = cb_ref[...] + cw_ref[SSD_CONV_W - 1:SSD_CONV_W, :] * xp_ref[pl.ds(8, TT), :]
    for i in range(SSD_CONV_W - 1):
        acc = acc + cw_ref[i:i + 1, :] * xp_ref[pl.ds(8 - (SSD_CONV_W - 1) + i, TT), :]
    return acc


def _ssd_intra(xs, bm, cm, g, dt, Mk_ref):
    TT = xs.shape[0]
    rT = (g - jnp.log(dt)).T
    lane = lax.broadcasted_iota(jnp.int32, (1, LANE), 1)
    lane2 = lax.broadcasted_iota(jnp.int32, (1, 2 * LANE), 1)
    causal = Mk_ref[...] > 0.0
    bmb = bm.astype(BF)
    zero = jnp.zeros((), BF)
    o_parts = []
    for grp in range(SSD_G):
        cb = _dot_nt(jnp.where((lane // SSD_N) == grp, cm, 0.0).astype(BF), bmb).astype(BF)
        xg = xs[:, grp * 2 * LANE:(grp + 1) * 2 * LANE].astype(BF)
        og = jnp.zeros((TT, 2 * LANE), F32)
        for h4 in range(SSD_H // SSD_G):
            h = grp * (SSD_H // SSD_G) + h4
            dec = jnp.where(causal, jnp.exp(g[:, h:h + 1] - rT[h:h + 1, :]), 0.0)
            p = cb * dec.astype(BF)
            og = og + _dot(p, jnp.where((lane2 // SSD_P) == h4, xg, zero))
        o_parts.append(og)
    return jnp.concatenate(o_parts, axis=1)


def _ssd_prompt_kernel(x_ref, cw_ref, cb_ref, dtb_ref, alog_ref, dexp_ref, nw_ref, L_ref, Mk_ref, Eexp_ref, M2_ref,
                       o_ref, sfin_ref, cfin_ref, st_ref, xp_ref):
    TT = x_ref.shape[0]
    t = pl.program_id(1)

    @pl.when(t == 0)
    def _():
        st_ref[...] = jnp.zeros_like(st_ref)
        xp_ref[0:8, :] = jnp.zeros((8, SSD_CONV_DIM), F32)

    z = x_ref[:, 0:SSD_WIDTH]
    xp_ref[8:8 + TT, :] = x_ref[:, SSD_WIDTH:SSD_WIDTH + SSD_CONV_DIM]
    sdt = x_ref[:, SSD_WIDTH + SSD_CONV_DIM:SSD_IN_W]
    xbc = _silu(_ssd_conv(xp_ref, cw_ref, cb_ref, TT))
    tail = xp_ref[TT:TT + 8, :]
    xp_ref[0:8, :] = tail
    dt_all = _softplus(sdt + dtb_ref[...])
    la_all = dt_all * (-jnp.exp(alog_ref[...]))
    Eexp = Eexp_ref[...]
    C = L_ref.shape[0]
    S = st_ref[...]
    for i in range(TT // C):
        rows = slice(i * C, (i + 1) * C)
        xs = xbc[rows, 0:SSD_WIDTH]
        bm = xbc[rows, SSD_WIDTH:SSD_WIDTH + LANE]
        cm = xbc[rows, SSD_WIDTH + LANE:SSD_CONV_DIM]
        dt = dt_all[rows, :]
        g = _dot_3x(L_ref[...], la_all[rows, :])
        gl = g[C - 1:C, :]
        eg_x = _dot_x2(jnp.exp(g), Eexp)
        cw_x = _dot_x2(dt * jnp.exp(gl - g), Eexp)
        egl_x = _dot_x2(jnp.exp(gl), Eexp)

        o = _ssd_intra(xs, bm, cm, g, dt, Mk_ref)
        o = o + eg_x * _dot(cm.astype(BF), S.astype(BF))
        u = _dot_tn(bm.astype(BF), (xs * cw_x).astype(BF))
        S = S * egl_x + u * M2_ref[...]

        y = (o + dexp_ref[...] * xs) * _silu(z[rows, :])
        ms = jnp.mean(y * y, axis=-1, keepdims=True)
        o_ref[rows, :] = (y * lax.rsqrt(ms + EPS) * nw_ref[...]).astype(o_ref.dtype)
    st_ref[...] = S

    @pl.when(t == pl.num_programs(1) - 1)
    def _():
        for h in range(SSD_H):
            gi = h // (SSD_H // SSD_G)
            sfin_ref[0, h] = S[gi * SSD_N:(gi + 1) * SSD_N, h * SSD_P:(h + 1) * SSD_P]
        cfin_ref[0] = tail[8 - (SSD_CONV_W - 1):8, :]


def _pad_lanes(v, n=LANE):
    v = v.reshape(1, -1)
    return jnp.zeros((1, n), F32).at[:, :v.shape[1]].set(v)


def _ssd_tables(TT, c):
    L = jnp.asarray(_block_tril(TT, c), BF)
    Mk = jnp.asarray(_block_tril(TT, c), F32)
    e = np.zeros((LANE, SSD_WIDTH), np.float32)
    for h in range(SSD_H):
        e[h, h * SSD_P:(h + 1) * SSD_P] = 1.0
    M2 = np.zeros((SSD_G * SSD_N, SSD_WIDTH), np.float32)
    for h in range(SSD_H):
        gi = h // (SSD_H // SSD_G)
        M2[gi * SSD_N:(gi + 1) * SSD_N, h * SSD_P:(h + 1) * SSD_P] = 1.0
    return L, Mk, jnp.asarray(e, BF), jnp.asarray(M2, F32)


def _ssd_params(conv_w, conv_b, dt_bias, a_log, d, norm_w):
    return (conv_w, conv_b.reshape(1, -1), _pad_lanes(dt_bias), _pad_lanes(a_log),
            jnp.repeat(d, SSD_P).reshape(1, -1), norm_w.reshape(1, -1))


def _ssd_prompt_call(sin_, B, T, conv_w, conv_b, dt_bias, a_log, d, norm_w):
    C = 256
    TT = 4 * C if T % (4 * C) == 0 else (2 * C if T % (2 * C) == 0 else C)
    nT = T // TT
    L, Mk, Eexp, M2 = _ssd_tables(C, C)
    prm = _ssd_params(conv_w, conv_b, dt_bias, a_log, d, norm_w)
    return pl.pallas_call(
        _ssd_prompt_kernel,
        grid=(B, nT),
        in_specs=[pl.BlockSpec((TT, SSD_IN_W), lambda b, t: (b * nT + t, 0))]
                 + [_const(p.shape) for p in prm]
                 + [_const(L.shape), _const(Mk.shape), _const(Eexp.shape), _const(M2.shape)],
        out_specs=[pl.BlockSpec((TT, SSD_WIDTH), lambda b, t: (b * nT + t, 0)),
                   pl.BlockSpec((1, SSD_H, SSD_N, SSD_P), lambda b, t: (b, 0, 0, 0)),
                   pl.BlockSpec((1, SSD_CONV_W - 1, SSD_CONV_DIM), lambda b, t: (b, 0, 0))],
        out_shape=[jax.ShapeDtypeStruct((B * T, SSD_WIDTH), BF),
                   jax.ShapeDtypeStruct((B, SSD_H, SSD_N, SSD_P), F32),
                   jax.ShapeDtypeStruct((B, SSD_CONV_W - 1, SSD_CONV_DIM), F32)],
        scratch_shapes=[pltpu.VMEM((SSD_G * SSD_N, SSD_WIDTH), F32),
                        pltpu.VMEM((TT + 8, SSD_CONV_DIM), F32)],
        compiler_params=_cp(2),
        name="ssd_prompt",
    )(sin_, *prm, L, Mk, Eexp, M2)


SEQ_TILE = 8


def _tile_lanes(n_rep, width):
    return np.tile(np.eye(width, dtype=np.float32), (1, n_rep))


def _fold_head_blocks(ubd):
    a = ubd[:, 0:LANE] + ubd[:, LANE:2 * LANE]
    return (a + pltpu.roll(a, LANE // 2, 1))[:, 0:LANE // 2]


def _col_bcast(row8, ones_ref):
    first = lax.broadcasted_iota(jnp.int32, (8, 1), 0) == 0
    hi, mid, lo = _split3(jnp.where(first, row8, 0.0))
    ones = ones_ref[...]
    return _dot_tn(hi, ones) + (_dot_tn(mid, ones) + _dot_tn(lo, ones))


def _gla_sample_kernel(x_ref, s0_ref, wg_ref, bg_ref, nw_ref, L_ref, E_ref, EA_ref, M_ref, T4_ref, ones_ref,
                       o_ref, sn_ref, kp_ref, gp_ref, vp_ref, oi_ref, *, c):
    TT = x_ref.shape[0]
    PAD = kp_ref.shape[0] - TT
    q, k, v, r, g = _gla_front(x_ref, wg_ref, bg_ref, L_ref)
    kp_ref[0:PAD, :] = jnp.zeros((PAD, LANE), F32)
    gp_ref[0:PAD, :] = jnp.zeros((PAD, LANE), F32)
    vp_ref[0:PAD, :] = jnp.zeros((PAD, 2 * LANE), F32)
    kp_ref[PAD:PAD + TT, :] = k
    gp_ref[PAD:PAD + TT, :] = g
    vp_ref[PAD:PAD + TT, :] = v
    o = _gla_intra(q, g, kp_ref, gp_ref, vp_ref, E_ref, c)
    qe = (q * jnp.exp(g)).astype(BF)
    M = M_ref[...]
    for s in range(TT // c):
        lo = s * c
        S0 = s0_ref[s].reshape(GLA_H * GLA_DK, GLA_DV)
        Sbd = (_dot(S0.astype(BF), T4_ref[...]) * M).astype(BF)
        oi_ref[lo:lo + c, :] = _dot(qe[lo:lo + c, :], Sbd)
        gl = g[lo + c - 1:lo + c, :]
        ke = (k[lo:lo + c, :] * jnp.exp(gl - g[lo:lo + c, :])).astype(BF)
        u = _fold_head_blocks(_dot_tn(ke, v[lo:lo + c, :].astype(BF)) * M)
        acol = _col_bcast(jnp.broadcast_to(jnp.exp(gl), (8, LANE)), ones_ref)
        sn_ref[s] = (acol * S0 + u).reshape(GLA_H, GLA_DK, GLA_DV)
    o = o + oi_ref[...]
    o_ref[...] = _gla_norm_gate(o, r, nw_ref, EA_ref).astype(o_ref.dtype)


def _gla_sample_call(gin, s0, B, T, w_gate, b_gate, norm_w):
    TT = SEQ_TILE * T
    L, E, EA, _ = _gla_tables(TT, T)
    M = jnp.asarray(_head_block_mask(GLA_DK, GLA_DV, GLA_H), F32)
    T4 = jnp.asarray(_tile_lanes(GLA_H, GLA_DV), BF)
    ones = jnp.ones((8, GLA_DV), BF)
    wg, bg, nw = _gla_params(w_gate, b_gate, norm_w)
    PAD = 8
    sspec = pl.BlockSpec((SEQ_TILE, GLA_H, GLA_DK, GLA_DV), lambda i: (i, 0, 0, 0))
    return pl.pallas_call(
        functools.partial(_gla_sample_kernel, c=T),
        grid=(B // SEQ_TILE,),
        in_specs=[pl.BlockSpec((TT, GLA_IN_W), lambda i: (i, 0)), sspec,
                  _const(wg.shape), _const(bg.shape), _const(nw.shape),
                  _const(L.shape), _const(E.shape), _const(EA.shape), _const(M.shape), _const(T4.shape),
                  _const(ones.shape)],
        out_specs=[pl.BlockSpec((TT, GLA_WIDTH), lambda i: (i, 0)), sspec],
        out_shape=[jax.ShapeDtypeStruct((B * T, GLA_WIDTH), BF),
                   jax.ShapeDtypeStruct((B, GLA_H, GLA_DK, GLA_DV), F32)],
        scratch_shapes=[pltpu.VMEM((TT + PAD, LANE), F32),
                        pltpu.VMEM((TT + PAD, LANE), F32),
                        pltpu.VMEM((TT + PAD, 2 * LANE), F32),
                        pltpu.VMEM((TT, 2 * LANE), F32)],
        compiler_params=_cp(1),
        name="gla_sample",
    )(gin, s0, wg, bg, nw, L, E, EA, M, T4, ones)


def _ret_sample_kernel(x_ref, s0_ref, cos_ref, sin_ref, D_ref, rd_ref, kd_ref, G_ref, M_ref, EA_ref, nw_ref, T4_ref,
                       o_ref, sn_ref, oi_ref, *, c):
    TT = x_ref.shape[0]
    q, k, v, rg = _ret_front(x_ref, cos_ref, sin_ref)
    o = _ret_intra(q, k, v, D_ref)
    qd = (q * rd_ref[...]).astype(BF)
    kd = (k * kd_ref[...]).astype(BF)
    vb = v.astype(BF)
    M = M_ref[...]
    for s in range(TT // c):
        lo = s * c
        S0 = s0_ref[s].reshape(RET_H * RET_DK, RET_DV)
        Sbd = (_dot(S0.astype(BF), T4_ref[...]) * M).astype(BF)
        oi_ref[lo:lo + c, :] = _dot(qd[lo:lo + c, :], Sbd)
        u = _fold_head_blocks(_dot_tn(kd[lo:lo + c, :], vb[lo:lo + c, :]) * M)
        sn_ref[s] = (G_ref[...] * S0 + u).reshape(RET_H, RET_DK, RET_DV)
    o = o + oi_ref[...]
    o_ref[...] = _ret_norm_gate(o, rg, nw_ref, EA_ref).astype(o_ref.dtype)


def _ret_sample_call(rin, s0, B, T, norm_w):
    TT = SEQ_TILE * T
    cos, sin = _rope_tables(PAST_LEN + jnp.arange(T, dtype=jnp.int32))
    cos, sin = jnp.tile(cos, (SEQ_TILE, 1)), jnp.tile(sin, (SEQ_TILE, 1))
    lg = _ret_log_gamma()
    i = np.arange(TT)
    same = (i[:, None] // T == i[None, :] // T) & (i[:, None] >= i[None, :])
    Dm = jnp.asarray(np.exp(lg[:, None, None] * (i[:, None] - i[None, :])[None]) * same[None], F32)
    tt = i % T
    rd = jnp.asarray(np.repeat(np.exp(lg[None, :] * (tt[:, None] + 1)), RET_DK, 1), F32)
    kd = jnp.asarray(np.repeat(np.exp(lg[None, :] * (T - 1 - tt[:, None])), RET_DK, 1), F32)
    G = jnp.asarray(np.repeat(np.repeat(np.exp(lg * T), RET_DK)[:, None], RET_DV, 1), F32)
    M = jnp.asarray(_head_block_mask(RET_DK, RET_DV, RET_H), F32)
    EA = jnp.asarray(_head_block_mask(RET_DV, RET_DV, RET_H) / RET_DV, BF)
    T4 = jnp.asarray(_tile_lanes(RET_H, RET_DV), BF)
    nw = norm_w.reshape(1, -1)
    sspec = pl.BlockSpec((SEQ_TILE, RET_H, RET_DK, RET_DV), lambda i: (i, 0, 0, 0))
    consts = (cos, sin, Dm, rd, kd, G, M, EA, nw, T4)
    return pl.pallas_call(
        functools.partial(_ret_sample_kernel, c=T),
        grid=(B // SEQ_TILE,),
        in_specs=[pl.BlockSpec((TT, RET_IN_W), lambda i: (i, 0)), sspec] + [_const(a.shape) for a in consts],
        out_specs=[pl.BlockSpec((TT, RET_WIDTH), lambda i: (i, 0)), sspec],
        out_shape=[jax.ShapeDtypeStruct((B * T, RET_WIDTH), BF),
                   jax.ShapeDtypeStruct((B, RET_H, RET_DK, RET_DV), F32)],
        scratch_shapes=[pltpu.VMEM((TT, RET_WIDTH), F32)],
        compiler_params=_cp(1),
        name="ret_sample",
    )(rin, s0, *consts)


def _ssd_sample_kernel(x_ref, c0_ref, s0_ref, cw_ref, cb_ref, dtb_ref, alog_ref, dexp_ref, nw_ref,
                       L_ref, Mk_ref, Eexp_ref, Bl_ref, R2_ref, T8_ref, T8T_ref, M8_ref, ones_ref,
                       o_ref, sn_ref, cn_ref, xp_ref, oi_ref, *, c):
    TT = x_ref.shape[0]
    ns = TT // c
    RP = 2 * c
    xp_ref[...] = jnp.zeros_like(xp_ref)
    z = x_ref[:, 0:SSD_WIDTH]
    sdt = x_ref[:, SSD_WIDTH + SSD_CONV_DIM:SSD_IN_W]
    for s in range(ns):
        base = 8 + s * RP
        xp_ref[base + c - (SSD_CONV_W - 1):base + c, :] = c0_ref[s]
        xp_ref[base + c:base + RP, :] = x_ref[s * c:(s + 1) * c, SSD_WIDTH:SSD_WIDTH + SSD_CONV_DIM]
    conv = _ssd_conv(xp_ref, cw_ref, cb_ref, ns * RP)
    xbc = _silu(conv.reshape(ns, RP, SSD_CONV_DIM)[:, c:RP, :].reshape(TT, SSD_CONV_DIM))
    for s in range(ns):
        base = 8 + s * RP
        cn_ref[s] = xp_ref[base + RP - (SSD_CONV_W - 1):base + RP, :]
    xs = xbc[:, 0:SSD_WIDTH]
    bm = xbc[:, SSD_WIDTH:SSD_WIDTH + LANE]
    cm = xbc[:, SSD_WIDTH + LANE:SSD_CONV_DIM]

    dt = _softplus(sdt + dtb_ref[...])
    la = dt * (-jnp.exp(alog_ref[...]))
    g = _dot_3x(L_ref[...], la)
    gl = _dot_3x(Bl_ref[...], g)
    Eexp = Eexp_ref[...]
    eg_x = _dot_x2(jnp.exp(g), Eexp)
    cw_x = _dot_x2(dt * jnp.exp(gl - g), Eexp)
    egl_x = _dot_x2(jnp.exp(gl), Eexp)
    o = _ssd_intra(xs, bm, cm, g, dt, Mk_ref)

    Cx = _dot(cm.astype(BF), R2_ref[...])
    Bx = _dot(bm.astype(BF), R2_ref[...])
    Xw = xs * cw_x
    M8 = M8_ref[...]
    nh = SSD_H

    def rows_by_head(a):
        return jnp.concatenate([a] * nh, axis=0) * M8

    for s in range(ns):
        lo = s * c
        S0 = s0_ref[s].reshape(SSD_H * SSD_N, SSD_P)
        oi = _dot(rows_by_head(Cx[lo:lo + c, :]).astype(BF), S0.astype(BF))
        oix = _dot_x2(oi, T8_ref[...]) * M8
        acc = oix[0:c, :]
        for h in range(1, nh):
            acc = acc + oix[h * c:(h + 1) * c, :]
        oi_ref[lo:lo + c, :] = acc
        Xst = _dot(rows_by_head(Xw[lo:lo + c, :]).astype(BF), T8T_ref[...])
        u = _dot_tn(rows_by_head(Bx[lo:lo + c, :]).astype(BF), Xst.astype(BF))
        acol = _col_bcast(egl_x[lo:lo + c, :], ones_ref)
        sn_ref[s] = (acol * S0 + u).reshape(SSD_H, SSD_N, SSD_P)

    o = o + eg_x * oi_ref[...]
    y = (o + dexp_ref[...] * xs) * _silu(z)
    ms = jnp.mean(y * y, axis=-1, keepdims=True)
    o_ref[...] = (y * lax.rsqrt(ms + EPS) * nw_ref[...]).astype(o_ref.dtype)


def _ssd_sample_call(sin_, c0, s0, B, T, conv_w, conv_b, dt_bias, a_log, d, norm_w):
    TT = SEQ_TILE * T
    L, Mk, Eexp, _ = _ssd_tables(TT, T)
    i = np.arange(TT)
    Bl = jnp.asarray((i[None, :] == (i[:, None] // T) * T + T - 1).astype(np.float32), BF)
    hpg = SSD_H // SSD_G
    R2 = np.zeros((LANE, SSD_H * SSD_N), np.float32)
    for h in range(SSD_H):
        R2[(h // hpg) * SSD_N:(h // hpg + 1) * SSD_N, h * SSD_N:(h + 1) * SSD_N] = np.eye(SSD_N)
    T8 = _tile_lanes(SSD_H, SSD_P)
    M8 = _head_block_mask(T, SSD_P, SSD_H)
    tabs = (L, Mk, Eexp, Bl, jnp.asarray(R2, BF), jnp.asarray(T8, BF), jnp.asarray(T8.T, BF), jnp.asarray(M8, F32),
            jnp.ones((8, SSD_P), BF))
    prm = _ssd_params(conv_w, conv_b, dt_bias, a_log, d, norm_w)
    sspec = pl.BlockSpec((SEQ_TILE, SSD_H, SSD_N, SSD_P), lambda i: (i, 0, 0, 0))
    cspec = pl.BlockSpec((SEQ_TILE, SSD_CONV_W - 1, SSD_CONV_DIM), lambda i: (i, 0, 0))
    return pl.pallas_call(
        functools.partial(_ssd_sample_kernel, c=T),
        grid=(B // SEQ_TILE,),
        in_specs=[pl.BlockSpec((TT, SSD_IN_W), lambda i: (i, 0)), cspec, sspec]
                 + [_const(p.shape) for p in prm] + [_const(a.shape) for a in tabs],
        out_specs=[pl.BlockSpec((TT, SSD_WIDTH), lambda i: (i, 0)), sspec, cspec],
        out_shape=[jax.ShapeDtypeStruct((B * T, SSD_WIDTH), BF),
                   jax.ShapeDtypeStruct((B, SSD_H, SSD_N, SSD_P), F32),
                   jax.ShapeDtypeStruct((B, SSD_CONV_W - 1, SSD_CONV_DIM), F32)],
        scratch_shapes=[pltpu.VMEM((8 + SEQ_TILE * 2 * T, SSD_CONV_DIM), F32),
                        pltpu.VMEM((TT, SSD_WIDTH), F32)],
        compiler_params=_cp(1),
        name="ssd_sample",
    )(sin_, c0, s0, *prm, *tabs)


def _inproj_t_kernel(x_ref, sc_ref, sh_ref, wt_ref, og_ref, or_ref, os_ref, w_ref):
    nt, nb, D = x_ref.shape

    @pl.when(pl.program_id(0) == 0)
    def _():
        for src, dst, n in ((0, 0, N_GA + GLA_GATE_RANK), (N_GA + GLA_GATE_RANK, N_GA + LANE, N_DT - N_GA - GLA_GATE_RANK)):
            for r in range(0, n, 512):
                m = min(512, n - r)
                w_ref[dst + r:dst + r + m, :] = wt_ref[src + r:src + r + m, :].astype(BF)
        w_ref[N_GA + GLA_GATE_RANK:N_GA + LANE, :] = jnp.zeros((LANE - GLA_GATE_RANK, D), BF)
        tail = jnp.concatenate([wt_ref[N_DT:N_IN, :], jnp.zeros((LANE - SSD_H, D), F32)], axis=0)
        w_ref[IN_W - LANE:IN_W, :] = tail.astype(BF)

    h = x_ref[...] * (1.0 + sc_ref[...]) + sh_ref[...]
    for t in range(nt):
        ht = h[t].astype(BF)
        cols = slice(t * nb, (t + 1) * nb)
        og_ref[:, cols] = _dot_nt(w_ref[0:GLA_IN_W, :], ht)
        or_ref[:, cols] = _dot_nt(w_ref[GLA_IN_W:GLA_IN_W + RET_IN_W, :], ht)
        os_ref[:, cols] = _dot_nt(w_ref[GLA_IN_W + RET_IN_W:IN_W, :], ht)


def _inproj_t_call(xt, sc, sh, wt, l):
    T, B, D = xt.shape
    nt = 4
    cmap = lambda i: (0, i)
    return pl.pallas_call(
        _inproj_t_kernel,
        grid=(T // nt,),
        in_specs=[pl.BlockSpec((nt, B, D), lambda i: (i, 0, 0)),
                  pl.BlockSpec((1, B, D), lambda i: (0, 0, 0)),
                  pl.BlockSpec((1, B, D), lambda i: (0, 0, 0)),
                  _resident_layer(wt.shape, l)],
        out_specs=[pl.BlockSpec((GLA_IN_W, nt * B), cmap),
                   pl.BlockSpec((RET_IN_W, nt * B), cmap),
                   pl.BlockSpec((SSD_IN_W, nt * B), cmap)],
        out_shape=[jax.ShapeDtypeStruct((GLA_IN_W, T * B), F32),
                   jax.ShapeDtypeStruct((RET_IN_W, T * B), F32),
                   jax.ShapeDtypeStruct((SSD_IN_W, T * B), F32)],
        scratch_shapes=[pltpu.VMEM((IN_W, D), BF)],
        compiler_params=_cp(1),
        name="in_proj_t",
    )(xt, sc, sh, wt)


def _row_sum(x):
    return jnp.sum(x, axis=0, keepdims=True)


def _lane_state_readout(o, coef_ref, s0_ref, n_rows):
    nb = LANE
    half = len(o) // 2
    for part in range(2):
        def body(k8, accs, part=part):
            accs = list(accs)
            base = pl.multiple_of(k8 * 8, 8)
            grp = [coef_ref[pl.ds(base, 8), (part * half + i) * nb:(part * half + i + 1) * nb] for i in range(half)]
            for j in range(8):
                s0k = s0_ref[0, k8 * 8 + j]
                for i in range(half):
                    accs[i] = accs[i] + grp[i][j:j + 1, :] * s0k
            return tuple(accs)

        res = lax.fori_loop(0, n_rows // 8, body, tuple(o[part * half:(part + 1) * half]))
        o[part * half:(part + 1) * half] = list(res)
    return o


def _lane_state_update(sn_ref, s0_ref, decay_fn, coef_ref, val_fn, n_rows, T):
    nb = LANE

    def body(k8, carry):
        base = pl.multiple_of(k8 * 8, 8)
        grp = [coef_ref[pl.ds(base, 8), t * nb:(t + 1) * nb] for t in range(T)]
        dec = decay_fn(base)
        for j in range(8):
            dj = dec[j:j + 1, :] if dec.shape[0] == 8 else dec
            sk = dj * s0_ref[0, k8 * 8 + j]
            for t in range(T):
                sk = sk + grp[t][j:j + 1, :] * val_fn(t)
            sn_ref[0, 0, k8 * 8 + j] = sk
        return carry

    lax.fori_loop(0, n_rows // 8, body, 0)


def _state_specs(shape, l, first):
    assert l == 0 or not first
    tail = tuple(shape[2:])
    in_spec = pl.BlockSpec((None, 1) + tail, lambda h: (l, h, 0, 0, 0))
    out_spec = pl.BlockSpec(((shape[0] if first else 1), 1) + tail, lambda h: (l, h, 0, 0, 0))
    return in_spec, out_spec


def _zero_later_layers(ref):
    ref[1:] = jnp.zeros((ref.shape[0] - 1,) + tuple(ref.shape[1:]), ref.dtype)


def _finish_state_call(kern, n_in, first, prevs):
    if first:
        return functools.partial(kern, first=True), [], {}
    wrapped = lambda *a, **kw: kern(*a[:n_in], *a[n_in + len(prevs):], first=False, **kw)
    specs = [pl.BlockSpec(memory_space=pl.ANY)] * len(prevs)
    return wrapped, specs, {n_in + i: 1 + i for i in range(len(prevs))}


def _gla_t_kernel(x_ref, s0_ref, wg_ref, bg_ref, nw_ref, o_ref, sn_ref, qe_ref, ke_ref, a_ref, *, T, first):
    nb = LANE
    if first:
        _zero_later_layers(sn_ref)
    h = pl.program_id(0)
    r0 = pl.multiple_of(h * GLA_DK, GLA_DK)
    v0 = pl.multiple_of(h * GLA_DV, GLA_DV)
    q = x_ref[pl.ds(r0, GLA_DK), :] * (GLA_DK ** -0.5)
    k = x_ref[pl.ds(128 + r0, GLA_DK), :]
    gate = _dot(wg_ref[pl.ds(r0, GLA_DK), :], x_ref[512:640, :].astype(BF)) + bg_ref[pl.ds(r0, GLA_DK), :]
    la = _log_sigmoid(gate) * (1.0 / GLA_GATE_TEMP)
    gs = []
    acc = jnp.zeros((GLA_DK, nb), F32)
    for t in range(T):
        acc = acc + la[:, t * nb:(t + 1) * nb]
        gs.append(acc)
    gl = gs[T - 1]
    a_ref[...] = jnp.exp(gl)
    qs = [q[:, t * nb:(t + 1) * nb] for t in range(T)]
    ks = [k[:, t * nb:(t + 1) * nb] for t in range(T)]
    for t in range(T):
        qe_ref[:, t * nb:(t + 1) * nb] = qs[t] * jnp.exp(gs[t])
        ke_ref[:, t * nb:(t + 1) * nb] = ks[t] * jnp.exp(gl - gs[t])

    def vt(t):
        return x_ref[pl.ds(256 + v0, GLA_DV), t * nb:(t + 1) * nb]

    o = []
    for t in range(T):
        ot = jnp.zeros((GLA_DV, nb), F32)
        for u in range(t + 1):
            s = _row_sum(qs[t] * ks[u] * jnp.exp(gs[t] - gs[u]))
            ot = ot + s * vt(u)
        o.append(ot)

    o = _lane_state_readout(o, qe_ref, s0_ref, GLA_DK)
    _lane_state_update(sn_ref, s0_ref, lambda base: a_ref[pl.ds(base, 8), :], ke_ref, vt, GLA_DK, T)

    nw = nw_ref[pl.ds(v0, GLA_DV), :]
    for t in range(T):
        ms = jnp.mean(o[t] * o[t], axis=0, keepdims=True)
        r = x_ref[pl.ds(640 + v0, GLA_DV), t * nb:(t + 1) * nb]
        o_ref[:, t * nb:(t + 1) * nb] = (o[t] * lax.rsqrt(ms + EPS) * nw * _silu(r)).astype(o_ref.dtype)


def _gla_t_call(gT, s0, prev, l, T, w_gate, b_gate, norm_w):
    N = gT.shape[1]
    wg = jnp.zeros((GLA_H * GLA_DK, LANE), F32).at[:, :GLA_GATE_RANK].set(w_gate.T).astype(BF)
    bg = b_gate.reshape(-1, 1)
    nw = norm_w.reshape(-1, 1)
    first = prev is None
    prevs = [] if first else [prev]
    s_in, s_out = _state_specs(s0.shape, l, first)
    ins = [gT, s0, wg, bg, nw]
    specs = [_const(gT.shape), s_in, _const(wg.shape), _const(bg.shape), _const(nw.shape)]
    kern, pspecs, aliases = _finish_state_call(functools.partial(_gla_t_kernel, T=T), len(ins), first, prevs)
    ins, specs = ins + prevs, specs + pspecs
    return pl.pallas_call(
        kern,
        grid=(GLA_H,),
        in_specs=specs,
        out_specs=[pl.BlockSpec((GLA_DV, N), lambda h: (h, 0)), s_out],
        out_shape=[jax.ShapeDtypeStruct((GLA_WIDTH, N), BF), jax.ShapeDtypeStruct(s0.shape, F32)],
        scratch_shapes=[pltpu.VMEM((GLA_DK, N), F32), pltpu.VMEM((GLA_DK, N), F32), pltpu.VMEM((GLA_DK, LANE), F32)],
        input_output_aliases=aliases,
        compiler_params=_cp(1),
        name="gla_t",
    )(*ins)


def _ret_t_kernel(x_ref, s0_ref, cos_ref, sin_ref, pw_ref, nw_ref, o_ref, sn_ref, qd_ref, kd_ref, *, T, first):
    nb = LANE
    if first:
        _zero_later_layers(sn_ref)
    h = pl.program_id(0)
    r0 = pl.multiple_of(h * RET_DK, RET_DK)
    half_k = RET_DK // 2
    cos, sin = cos_ref[...], sin_ref[...]

    def rope_t(base):
        x1 = x_ref[pl.ds(base + r0, half_k), :]
        x2 = x_ref[pl.ds(base + r0 + half_k, half_k), :]
        return jnp.concatenate([x1 * cos - x2 * sin, x1 * sin + x2 * cos], axis=0)

    q = rope_t(0)
    k = rope_t(256) * (RET_DK ** -0.5)
    pw = pw_ref[h]
    qs = [q[:, t * nb:(t + 1) * nb] for t in range(T)]
    ks = [k[:, t * nb:(t + 1) * nb] for t in range(T)]
    for t in range(T):
        qd_ref[:, t * nb:(t + 1) * nb] = qs[t] * pw[t + 1:t + 2, :]
        kd_ref[:, t * nb:(t + 1) * nb] = ks[t] * pw[T - 1 - t:T - t, :]

    def vt(t):
        return x_ref[pl.ds(512 + r0, RET_DV), t * nb:(t + 1) * nb]

    o = []
    for t in range(T):
        ot = jnp.zeros((RET_DV, nb), F32)
        for u in range(t + 1):
            s = _row_sum(qs[t] * ks[u]) * pw[t - u:t - u + 1, :]
            ot = ot + s * vt(u)
        o.append(ot)

    o = _lane_state_readout(o, qd_ref, s0_ref, RET_DK)
    _lane_state_update(sn_ref, s0_ref, lambda base: pw[T:T + 1, :], kd_ref, vt, RET_DK, T)

    nw = nw_ref[pl.ds(r0, RET_DV), :]
    for t in range(T):
        mu = jnp.mean(o[t], axis=0, keepdims=True)
        d = o[t] - mu
        var = jnp.mean(d * d, axis=0, keepdims=True)
        rg = x_ref[pl.ds(768 + r0, RET_DV), t * nb:(t + 1) * nb]
        o_ref[:, t * nb:(t + 1) * nb] = (d * lax.rsqrt(var + EPS) * nw * _silu(rg)).astype(o_ref.dtype)


def _ret_t_call(rT, s0, prev, l, T, norm_w):
    N = rT.shape[1]
    B = N // T
    half = RET_DK // 2
    inv = ROPE_BASE ** (-jnp.arange(half, dtype=F32) / half)
    ang = inv[:, None] * (PAST_LEN + jnp.arange(T, dtype=jnp.int32)).astype(F32)[None, :]
    cos = jnp.repeat(jnp.cos(ang), B, axis=1)
    sin = jnp.repeat(jnp.sin(ang), B, axis=1)
    lg = _ret_log_gamma()
    pw = jnp.asarray(np.repeat(np.exp(lg[:, None] * np.arange(16)[None, :])[:, :, None], LANE, axis=2), F32)
    nw = norm_w.reshape(-1, 1)
    first = prev is None
    prevs = [] if first else [prev]
    s_in, s_out = _state_specs(s0.shape, l, first)
    ins = [rT, s0, cos, sin, pw, nw]
    specs = [_const(rT.shape), s_in, _const(cos.shape), _const(sin.shape), _const(pw.shape), _const(nw.shape)]
    kern, pspecs, aliases = _finish_state_call(functools.partial(_ret_t_kernel, T=T), len(ins), first, prevs)
    ins, specs = ins + prevs, specs + pspecs
    return pl.pallas_call(
        kern,
        grid=(RET_H,),
        in_specs=specs,
        out_specs=[pl.BlockSpec((RET_DV, N), lambda h: (h, 0)), s_out],
        out_shape=[jax.ShapeDtypeStruct((RET_WIDTH, N), BF), jax.ShapeDtypeStruct(s0.shape, F32)],
        scratch_shapes=[pltpu.VMEM((RET_DK, N), F32), pltpu.VMEM((RET_DK, N), F32)],
        input_output_aliases=aliases,
        compiler_params=_cp(1),
        name="ret_t",
    )(*ins)


def _ssd_t_kernel(x_ref, c0_ref, s0_ref, cw_ref, cb_ref, dtb_ref, alog_ref, d_ref, nw_ref,
                  o_ref, sn_ref, cn_ref, hist_ref, y_ref, ssq_ref, cm_ref, bw_ref, xw_ref, *, T, first):
    nb = LANE
    W1 = SSD_CONV_W - 1
    h = pl.program_id(0)
    XB = SSD_WIDTH
    if first:
        _zero_later_layers(sn_ref)

    @pl.when(h == 0)
    def _():
        ssq_ref[...] = jnp.zeros_like(ssq_ref)
        if first:
            _zero_later_layers(cn_ref)
        for i in range(W1):
            for j in range(SSD_CONV_DIM // LANE):
                hist_ref[j * LANE:(j + 1) * LANE, i * nb:(i + 1) * nb] = c0_ref[0, i][:, j * LANE:(j + 1) * LANE].T
                cn_ref[0, i, :, j * LANE:(j + 1) * LANE] = \
                    x_ref[XB + j * LANE:XB + (j + 1) * LANE, (T - W1 + i) * nb:(T - W1 + i + 1) * nb].T

    def conv_rows(ro):
        w = cw_ref[pl.ds(ro, 64), :]
        b = cb_ref[pl.ds(ro, 64), :]
        xx = [hist_ref[pl.ds(ro, 64), i * nb:(i + 1) * nb] for i in range(W1)]
        xx += [x_ref[pl.ds(XB + ro, 64), t * nb:(t + 1) * nb] for t in range(T)]
        out = []
        for t in range(T):
            acc = b + w[:, 0:1] * xx[t]
            for i in range(1, SSD_CONV_W):
                acc = acc + w[:, i:i + 1] * xx[t + i]
            out.append(_silu(acc))
        return out

    grp = h // (SSD_H // SSD_G)
    xs = conv_rows(pl.multiple_of(h * SSD_P, SSD_P))
    bm = conv_rows(pl.multiple_of(SSD_WIDTH + grp * SSD_N, SSD_N))
    cm = conv_rows(pl.multiple_of(SSD_WIDTH + SSD_G * SSD_N + grp * SSD_N, SSD_N))

    dt_all = _softplus(x_ref[pl.ds(XB + SSD_CONV_DIM + h, 1), :] + dtb_ref[pl.ds(h, 1), :])
    a = -jnp.exp(alog_ref[pl.ds(h, 1), :])
    dts = [dt_all[:, t * nb:(t + 1) * nb] for t in range(T)]
    gs = []
    acc = jnp.zeros((1, nb), F32)
    for t in range(T):
        acc = acc + dts[t] * a
        gs.append(acc)
    gl = gs[T - 1]

    o = []
    for t in range(T):
        ot = jnp.zeros((SSD_P, nb), F32)
        for u in range(t + 1):
            s = _row_sum(cm[t] * bm[u]) * (jnp.exp(gs[t] - gs[u]) * dts[u])
            ot = ot + s * xs[u]
        o.append(ot)

    for t in range(T):
        cm_ref[:, t * nb:(t + 1) * nb] = cm[t] * jnp.exp(gs[t])
        bw_ref[:, t * nb:(t + 1) * nb] = bm[t]
        xw_ref[:, t * nb:(t + 1) * nb] = xs[t] * (dts[t] * jnp.exp(gl - gs[t]))

    o = _lane_state_readout(o, cm_ref, s0_ref, SSD_N)
    egl = jnp.exp(gl)
    _lane_state_update(sn_ref, s0_ref, lambda base: egl, bw_ref, lambda t: xw_ref[:, t * nb:(t + 1) * nb], SSD_N, T)

    dd = d_ref[pl.ds(h, 1), :]
    p0 = pl.multiple_of(h * SSD_P, SSD_P)
    for t in range(T):
        z = x_ref[pl.ds(p0, SSD_P), t * nb:(t + 1) * nb]
        y = (o[t] + dd * xs[t]) * _silu(z)
        y_ref[pl.ds(p0, SSD_P), t * nb:(t + 1) * nb] = y
        ssq_ref[:, t * nb:(t + 1) * nb] += _row_sum(y * y)

    @pl.when(h == SSD_H - 1)
    def _():
        scale = lax.rsqrt(ssq_ref[...] * (1.0 / SSD_WIDTH) + EPS)
        o_ref[...] = (y_ref[...] * scale * nw_ref[...]).astype(o_ref.dtype)


def _ssd_t_call(sT, c0, s0, prev_s, prev_c, l, T, conv_w, conv_b, dt_bias, a_log, d, norm_w):
    N = sT.shape[1]
    col = lambda v: jnp.zeros((LANE, 1), F32).at[:SSD_H, 0].set(v)
    prm = (conv_w.T, conv_b.reshape(-1, 1), col(dt_bias), col(a_log), col(d), norm_w.reshape(-1, 1))
    first = prev_s is None
    prevs = [] if first else [prev_s, prev_c]
    s_in, s_out = _state_specs(s0.shape, l, first)
    c_in = pl.BlockSpec((1,) + tuple(c0.shape[1:]), lambda h: (l, 0, 0, 0))
    c_out = pl.BlockSpec(((c0.shape[0] if first else 1),) + tuple(c0.shape[1:]), lambda h: (l, 0, 0, 0))
    ins = [sT, c0, s0, *prm]
    specs = [_const(sT.shape), c_in, s_in] + [_const(p.shape) for p in prm]
    kern, pspecs, aliases = _finish_state_call(functools.partial(_ssd_t_kernel, T=T), len(ins), first, prevs)
    ins, specs = ins + prevs, specs + pspecs
    return pl.pallas_call(
        kern,
        grid=(SSD_H,),
        in_specs=specs,
        out_specs=[_const((SSD_WIDTH, N)), s_out, c_out],
        out_shape=[jax.ShapeDtypeStruct((SSD_WIDTH, N), BF), jax.ShapeDtypeStruct(s0.shape, F32),
                   jax.ShapeDtypeStruct(c0.shape, F32)],
        scratch_shapes=[pltpu.VMEM((SSD_CONV_DIM, (SSD_CONV_W - 1) * LANE), F32),
                        pltpu.VMEM((SSD_WIDTH, N), F32), pltpu.VMEM((1, N), F32),
                        pltpu.VMEM((SSD_N, N), F32), pltpu.VMEM((SSD_N, N), F32), pltpu.VMEM((SSD_P, N), F32)],
        input_output_aliases=aliases,
        compiler_params=_cp(1),
        name="ssd_t",
    )(*ins)


def _outproj_t_kernel(x_ref, g_ref, og_ref, or_ref, os_ref, w_ref, lg_ref, lb_ref, o_ref):
    nt, nb, D = x_ref.shape
    for t in range(nt):
        cols = slice(t * nb, (t + 1) * nb)
        mix = (_dot_tn(og_ref[:, cols], w_ref[0:GLA_WIDTH, :])
               + _dot_tn(or_ref[:, cols], w_ref[GLA_WIDTH:GLA_WIDTH + RET_WIDTH, :])
               + _dot_tn(os_ref[:, cols], w_ref[GLA_WIDTH + RET_WIDTH:D, :]))
        y = ALPHA * x_ref[t] + g_ref[0] * mix
        o_ref[t] = _layer_norm(y, lg_ref[0], lb_ref[0])


def _outproj_t_call(xt, g1, ogT, orT, osT, w_out, ln_g, ln_b):
    T, B, D = xt.shape
    nt = 4
    cmap = lambda i: (0, i)
    return pl.pallas_call(
        _outproj_t_kernel,
        grid=(T // nt,),
        in_specs=[pl.BlockSpec((nt, B, D), lambda i: (i, 0, 0)),
                  pl.BlockSpec((1, B, D), lambda i: (0, 0, 0)),
                  pl.BlockSpec((GLA_WIDTH, nt * B), cmap),
                  pl.BlockSpec((RET_WIDTH, nt * B), cmap),
                  pl.BlockSpec((SSD_WIDTH, nt * B), cmap),
                  _const((D, D)), _const((1, 1, D)), _const((1, 1, D))],
        out_specs=pl.BlockSpec((nt, B, D), lambda i: (i, 0, 0)),
        out_shape=jax.ShapeDtypeStruct((T, B, D), F32),
        compiler_params=_cp(1),
        name="out_proj_ln_t",
    )(xt, g1, ogT, orT, osT, w_out, ln_g.reshape(1, 1, D), ln_b.reshape(1, 1, D))


def _outproj_kernel(x_ref, g_ref, og_ref, or_ref, os_ref, w_ref, lg_ref, lb_ref, o_ref):
    bB, bT, D = x_ref.shape
    assert bB == 1
    n_part = 2
    for i in range(n_part):
        rows = slice(i * bT // n_part, (i + 1) * bT // n_part)
        mix = (_dot(og_ref[rows, :], w_ref[0:GLA_WIDTH, :])
               + _dot(or_ref[rows, :], w_ref[GLA_WIDTH:GLA_WIDTH + RET_WIDTH, :])
               + _dot(os_ref[rows, :], w_ref[GLA_WIDTH + RET_WIDTH:D, :]))
        y = ALPHA * x_ref[0, rows, :] + g_ref[0] * mix
        o_ref[0, rows, :] = _layer_norm(y, lg_ref[0], lb_ref[0])


def _outproj_call(x3, g1, og, orr, os_, w_out, ln_g, ln_b):
    B, T, D = x3.shape
    bB, bT = _tok_tiles(B, T)
    nT = T // bT
    R = bB * bT
    xmap = lambda i, j: (i, j, 0)
    mmap = lambda i, j: (i, 0, 0)
    rmap = lambda i, j: (i * nT + j, 0)
    return pl.pallas_call(
        _outproj_kernel,
        grid=(B // bB, nT),
        in_specs=[pl.BlockSpec((bB, bT, D), xmap),
                  pl.BlockSpec((bB, 1, D), mmap),
                  pl.BlockSpec((R, GLA_WIDTH), rmap),
                  pl.BlockSpec((R, RET_WIDTH), rmap),
                  pl.BlockSpec((R, SSD_WIDTH), rmap),
                  _const((D, D)), _const((1, 1, D)), _const((1, 1, D))],
        out_specs=pl.BlockSpec((bB, bT, D), xmap),
        out_shape=jax.ShapeDtypeStruct((B, T, D), F32),
        compiler_params=_cp(2),
        name="out_proj_ln",
    )(x3, g1, og, orr, os_, w_out, ln_g.reshape(1, 1, D), ln_b.reshape(1, 1, D))


ROUTE_OFF = 8


def _moe_route_t(lt):
    R = lt.shape[1]
    neg = jnp.float32(-jnp.inf)
    row8 = lax.broadcasted_iota(jnp.int32, (8, 1), 0)
    lg = jnp.where(row8 < MOE_GROUPS, lt[0:8, :], neg)
    mg = jnp.max(lg, axis=0, keepdims=True)
    gsel = jnp.min(jnp.where(lg == mg, row8, 8), axis=0, keepdims=True)
    g_gate = 1.0 / jnp.sum(jnp.exp(lg - mg), axis=0, keepdims=True)
    rowe = lax.broadcasted_iota(jnp.int32, (MOE_EXPERTS, 1), 0)
    le = jnp.where((rowe // MOE_PER_GROUP) == gsel, lt[ROUTE_OFF:ROUTE_OFF + MOE_EXPERTS, :], neg)
    m1 = jnp.max(le, axis=0, keepdims=True)
    i1 = jnp.min(jnp.where(le == m1, rowe, MOE_EXPERTS), axis=0, keepdims=True)
    le2 = jnp.where(rowe == i1, neg, le)
    m2 = jnp.max(le2, axis=0, keepdims=True)
    i2 = jnp.min(jnp.where(le2 == m2, rowe, MOE_EXPERTS), axis=0, keepdims=True)
    e2 = jnp.exp(m2 - m1)
    w1 = g_gate / (1.0 + e2)
    w2 = g_gate * e2 / (1.0 + e2)
    comb = jnp.where(rowe == i1, w1, jnp.where(rowe == i2, w2, 0.0))
    cg = comb[0:4, :]
    for g in range(1, MOE_GROUPS):
        cg = cg + comb[g * MOE_PER_GROUP:(g + 1) * MOE_PER_GROUP, :]
    return gsel, cg, comb


MOE_SUB = 256
MOE_BLK = 16
MOE_ROWS = 256
MOE_NPS = MOE_SUB + MOE_GROUPS * MOE_BLK
assert MOE_SUB <= MOE_ROWS


def _moe_kernel(x_ref, sc_ref, sh_ref, g_ref, wr_ref, br_ref, us_ref, w1_ref, w3_ref, w2_ref, lg_ref, lb_ref,
                o_ref, hb_ref, cwb_ref, hp_ref, cwp_ref, yp_ref, pos_ref,
                cgrp_ref, fill_ref, cur_ref, na_ref, nb_ref, so_ref, nfa_ref, dsa_ref, dsb_ref, *, n_steps):
    bB, bT, D = x_ref.shape
    R = bB * bT
    n_q = R // MOE_SUB
    s = pl.program_id(1)
    x = x_ref[...]
    row8 = lax.broadcasted_iota(jnp.int32, (8, 1), 0)
    slot = lax.broadcasted_iota(jnp.int32, (MOE_NPS, 1), 0).astype(F32)

    @pl.when((pl.program_id(0) == 0) & (s == 0))
    def _():
        hb_ref[...] = jnp.zeros_like(hb_ref)
        cwb_ref[...] = jnp.zeros_like(cwb_ref)
        yp_ref[...] = jnp.zeros_like(yp_ref)

    @pl.when(s == 0)
    def _():
        na_ref[0] = 0
        for g in range(MOE_GROUPS):
            cur_ref[g] = -1
            fill_ref[g] = 0

    @pl.when(s < n_steps)
    def _():
        h = (x * (1.0 + sc_ref[...]) + sh_ref[...]).reshape(R, D)
        segs, offs = [], []
        for q in range(n_q):
            u = s * n_q + q
            hq = h[q * MOE_SUB:(q + 1) * MOE_SUB, :].astype(BF)
            gsel, cg, _ = _moe_route_t(_dot_nt(wr_ref[...], hq) + br_ref[...])
            onehot = jnp.where(row8 == gsel, 1.0, 0.0)
            rank = _dot(onehot.astype(BF), us_ref[...])
            cnt = jnp.sum(onehot, axis=1, keepdims=True)
            seg = jnp.ceil(cnt * (1.0 / MOE_BLK)) * MOE_BLK
            off = jnp.zeros((8, 1), F32)
            for g in range(1, MOE_GROUPS):
                off = off + jnp.where(row8 >= g, seg[g - 1:g, :], 0.0)
            pos = jnp.sum(onehot * (off + rank), axis=0, keepdims=True)
            pos_ref[u] = jnp.broadcast_to(pos, (8, MOE_SUB))
            perm = jnp.where(slot == pos, 1.0, 0.0).astype(BF)
            hp_ref[q] = _dot(perm, hq).astype(BF)
            cg8 = jnp.concatenate([cg, jnp.zeros((4, MOE_SUB), F32)], axis=0)
            cg_hi = cg8.astype(BF)
            cg_lo = (cg8 - cg_hi.astype(F32)).astype(BF)
            cwp_ref[q] = _dot_nt(perm, cg_hi) + _dot_nt(perm, cg_lo)
            segs.append(seg)
            offs.append(off)
        for q in range(n_q):
            u = s * n_q + q
            for g in range(MOE_GROUPS):
                so = offs[q][g, 0].astype(jnp.int32)
                nb = (segs[q][g, 0] * (1.0 / MOE_BLK)).astype(jnp.int32)
                f = fill_ref[g]
                c = cur_ref[g]
                na = na_ref[0]
                room = jnp.where(c < 0, 0, (MOE_ROWS - f) // MOE_BLK)
                n_a = jnp.minimum(nb, room)
                n_b = nb - n_a
                base_a = c * MOE_ROWS + f
                base_b = na * MOE_ROWS
                idx = u * MOE_GROUPS + g
                so_ref[idx] = so
                nb_ref[idx] = nb
                nfa_ref[idx] = n_a
                dsa_ref[idx] = base_a
                dsb_ref[idx] = base_b

                def put(k, carry, so=so, q=q, n_a=n_a, base_a=base_a, base_b=base_b):
                    dst = pl.multiple_of(jnp.where(k < n_a, base_a + k * MOE_BLK, base_b + (k - n_a) * MOE_BLK), MOE_BLK)
                    src = pl.multiple_of(so + k * MOE_BLK, MOE_BLK)
                    hb_ref[pl.ds(dst, MOE_BLK), :] = hp_ref[q, pl.ds(src, MOE_BLK), :]
                    cwb_ref[pl.ds(dst, MOE_BLK), :] = cwp_ref[q, pl.ds(src, MOE_BLK), :]
                    return carry

                lax.fori_loop(0, nb, put, 0)

                @pl.when(n_b > 0)
                def _(g=g, na=na, n_b=n_b):
                    cgrp_ref[na] = g
                    na_ref[0] = na + 1
                    cur_ref[g] = na
                    fill_ref[g] = n_b * MOE_BLK

                @pl.when(n_b == 0)
                def _(g=g, f=f, n_a=n_a):
                    fill_ref[g] = f + n_a * MOE_BLK

    @pl.when(s == n_steps - 1)
    def _():
        def chunk(c, carry):
            g = cgrp_ref[c]
            start = pl.multiple_of(c * MOE_ROWS, MOE_ROWS)
            hc = hb_ref[pl.ds(start, MOE_ROWS), :]
            cw = cwb_ref[pl.ds(start, MOE_ROWS), :]
            hids = []
            for j in range(MOE_PER_GROUP):
                e = g * MOE_PER_GROUP + j
                hid = _silu(_dot(hc, w1_ref[e])) * _dot(hc, w3_ref[e]) * cw[:, j:j + 1]
                hids.append(hid.astype(BF))
            w2g = w2_ref[pl.ds(g * MOE_PER_GROUP, MOE_PER_GROUP)].reshape(MOE_PER_GROUP * MOE_FF, D)
            hb_ref[pl.ds(start, MOE_ROWS), :] = _dot(jnp.concatenate(hids, axis=1), w2g).astype(BF)
            return carry

        lax.fori_loop(0, na_ref[0], chunk, 0)

    @pl.when(s >= n_steps)
    def _():
        for q in range(n_q):
            u = (s - n_steps) * n_q + q
            for g in range(MOE_GROUPS):
                idx = u * MOE_GROUPS + g
                so, n_a, base_a, base_b = so_ref[idx], nfa_ref[idx], dsa_ref[idx], dsb_ref[idx]

                def take(k, carry, so=so, q=q, n_a=n_a, base_a=base_a, base_b=base_b):
                    src = pl.multiple_of(jnp.where(k < n_a, base_a + k * MOE_BLK, base_b + (k - n_a) * MOE_BLK), MOE_BLK)
                    dst = pl.multiple_of(so + k * MOE_BLK, MOE_BLK)
                    yp_ref[q, pl.ds(dst, MOE_BLK), :] = hb_ref[pl.ds(src, MOE_BLK), :]
                    return carry

                lax.fori_loop(0, nb_ref[idx], take, 0)
        ys = []
        for q in range(n_q):
            u = (s - n_steps) * n_q + q
            perm = jnp.where(slot == pos_ref[u][0:1, :], 1.0, 0.0).astype(BF)
            ys.append(_dot_tn(perm, yp_ref[q]))
        y = jnp.concatenate(ys, axis=0)
        z = ALPHA * x + g_ref[...] * y.reshape(bB, bT, D)
        o_ref[...] = _layer_norm(z, lg_ref[...], lb_ref[...])


def _resident_layer(shape, l):
    return pl.BlockSpec((None,) + tuple(shape[1:]), lambda *_: (l,) + (0,) * (len(shape) - 1),
                        pipeline_mode=pl.Buffered(1))


def _moe_call(x3, sc, sh, g2, wr, br, w1, w3, w2, l, ln_g, ln_b):
    B, T, D = x3.shape
    bB, bT = _tok_tiles(B, T)
    R = bB * bT
    if bB == 1:
        spp = 2 if B % 2 == 0 else 1
        nT = T // bT
        n_pools, n_steps = B // spp, spp * nT
        xmap = lambda p, s: (p * spp + (s % n_steps) // nT, (s % n_steps) % nT, 0)
        omap = lambda p, s: (p * spp + jnp.maximum(s - n_steps, 0) // nT, jnp.maximum(s - n_steps, 0) % nT, 0)
        mmap = lambda p, s: (p * spp + (s % n_steps) // nT, 0, 0)
        mshape = (1, 1, D)
    else:
        n_pools, n_steps = 1, B // bB
        xmap = lambda p, s: (s % n_steps, 0, 0)
        omap = lambda p, s: (jnp.maximum(s - n_steps, 0), 0, 0)
        mmap = lambda p, s: (0, 0, 0)
        mshape = (1, bT, D)
    n_sub = n_steps * (R // MOE_SUB)
    n_chunks = pl.cdiv(n_sub * (MOE_SUB + MOE_GROUPS * (MOE_BLK - 1)), MOE_ROWS) + MOE_GROUPS
    us = jnp.asarray(np.triu(np.ones((MOE_SUB, MOE_SUB), np.float32), 1), BF)
    smem = lambda n: pltpu.SMEM((n,), jnp.int32)
    return pl.pallas_call(
        functools.partial(_moe_kernel, n_steps=n_steps),
        grid=(n_pools, 2 * n_steps),
        in_specs=[pl.BlockSpec((bB, bT, D), xmap),
                  pl.BlockSpec(mshape, mmap), pl.BlockSpec(mshape, mmap), pl.BlockSpec(mshape, mmap),
                  _const(wr.shape), _const(br.shape), _const(us.shape),
                  _resident_layer(w1.shape, l), _resident_layer(w3.shape, l), _resident_layer(w2.shape, l),
                  _const((1, 1, D)), _const((1, 1, D))],
        out_specs=pl.BlockSpec((bB, bT, D), omap),
        out_shape=jax.ShapeDtypeStruct((B, T, D), F32),
        scratch_shapes=[pltpu.VMEM((n_chunks * MOE_ROWS, D), BF), pltpu.VMEM((n_chunks * MOE_ROWS, 8), F32),
                        pltpu.VMEM((R // MOE_SUB, MOE_NPS, D), BF), pltpu.VMEM((R // MOE_SUB, MOE_NPS, 8), F32),
                        pltpu.VMEM((R // MOE_SUB, MOE_NPS, D), BF),
                        pltpu.VMEM((n_sub, 8, MOE_SUB), F32),
                        smem(n_chunks), smem(MOE_GROUPS), smem(MOE_GROUPS), smem(1),
                        *[smem(n_sub * MOE_GROUPS) for _ in range(5)]],
        compiler_params=_cp(2),
        name="moe_ln",
    )(x3, sc, sh, g2, wr, br, us, w1, w3, w2, ln_g.reshape(1, 1, D), ln_b.reshape(1, 1, D))


def _router_params(w_group, b_group, w_expert, b_expert):
    wr = jnp.zeros((LANE, D_MODEL), F32).at[:MOE_GROUPS].set(w_group.T)
    wr = wr.at[ROUTE_OFF:ROUTE_OFF + MOE_EXPERTS].set(w_expert.T)
    br = jnp.zeros((LANE, 1), F32).at[:MOE_GROUPS, 0].set(b_group).at[ROUTE_OFF:ROUTE_OFF + MOE_EXPERTS, 0].set(b_expert)
    return wr.astype(BF), br


def kernel(x_prompt, x_sample, c_prompt, c_sample, state_gla, state_ret, state_ssd, state_conv, w_ada, b_ada, w_in, gla_w_gate, gla_b_gate, gla_norm, ret_norm, ssd_conv_w, ssd_conv_b, ssd_dt_bias, ssd_a_log, ssd_d, ssd_norm, w_out, ln1_g, ln1_b, moe_w_group, moe_b_group, moe_w_expert, moe_b_expert, moe_w1, moe_w3, moe_w2, ln2_g, ln2_b):
    Bp, Tp, D = x_prompt.shape
    Bs, Ts, _ = x_sample.shape
    w_in_t = jnp.swapaxes(w_in, 1, 2)
    w_out_b = w_out.astype(BF)
    w1_b, w3_b, w2_b = moe_w1.astype(BF), moe_w3.astype(BF), moe_w2.astype(BF)

    mod = _mod_call(jnp.concatenate([c_prompt, c_sample], axis=0), w_ada, b_ada)

    def moe(x, sc2, sh2, g2, l):
        wr, br = _router_params(moe_w_group[l], moe_b_group[l], moe_w_expert[l], moe_b_expert[l])
        return _moe_call(x, sc2, sh2, g2, wr, br, w1_b, w3_b, w2_b, l, ln2_g[l], ln2_b[l])

    x = x_prompt
    new = [[], [], [], []]
    for l in range(DEPTH):
        sh1, sc1, g1, sh2, sc2, g2 = (mod[l, :Bp, None, i * D:(i + 1) * D] for i in range(6))
        gin, rin, sin_ = _inproj_call(x, sc1, sh1, w_in_t, l)
        og, s_gla = _gla_prompt_call(gin, Bp, Tp, gla_w_gate[l], gla_b_gate[l], gla_norm[l])
        orr, s_ret = _ret_prompt_call(rin, Bp, Tp, ret_norm[l])
        os_, s_ssd, s_conv = _ssd_prompt_call(sin_, Bp, Tp, ssd_conv_w[l], ssd_conv_b[l], ssd_dt_bias[l],
                                              ssd_a_log[l], ssd_d[l], ssd_norm[l])
        x = _outproj_call(x, g1, og, orr, os_, w_out_b[l], ln1_g[l], ln1_b[l])
        x = moe(x, sc2, sh2, g2, l)
        for acc, s in zip(new, (s_gla, s_ret, s_ssd, s_conv)):
            acc.append(s)
    y_p = x
    gla_p, ret_p, ssd_p, conv_p = (jnp.stack(a) for a in new)

    x = jnp.swapaxes(x_sample, 0, 1)
    sg = jnp.transpose(state_gla, (0, 2, 3, 4, 1))
    sr = jnp.transpose(state_ret, (0, 2, 3, 4, 1))
    ss = jnp.transpose(state_ssd, (0, 2, 3, 4, 1))
    cv = jnp.transpose(state_conv, (0, 2, 1, 3))
    gla_n = ret_n = ssd_n = conv_n = None
    for l in range(DEPTH):
        sh1, sc1, g1, sh2, sc2, g2 = (mod[l, None, Bp:, i * D:(i + 1) * D] for i in range(6))
        gT, rT, sT = _inproj_t_call(x, sc1, sh1, w_in_t, l)
        ogT, gla_n = _gla_t_call(gT, sg, gla_n, l, Ts, gla_w_gate[l], gla_b_gate[l], gla_norm[l])
        orT, ret_n = _ret_t_call(rT, sr, ret_n, l, Ts, ret_norm[l])
        osT, ssd_n, conv_n = _ssd_t_call(sT, cv, ss, ssd_n, conv_n, l, Ts, ssd_conv_w[l], ssd_conv_b[l],
                                         ssd_dt_bias[l], ssd_a_log[l], ssd_d[l], ssd_norm[l])
        x = _outproj_t_call(x, g1, ogT, orT, osT, w_out_b[l], ln1_g[l], ln1_b[l])
        x = moe(x, sc2, sh2, g2, l)
    y_s = jnp.swapaxes(x, 0, 1)
    gla_s = jnp.transpose(gla_n, (0, 4, 1, 2, 3))
    ret_s = jnp.transpose(ret_n, (0, 4, 1, 2, 3))
    ssd_s = jnp.transpose(ssd_n, (0, 4, 1, 2, 3))
    conv_s = jnp.transpose(conv_n, (0, 2, 1, 3))
    return (y_p, y_s, gla_p, ret_p, ssd_p, conv_p, gla_s, ret_s, ssd_s, conv_s)
```

```python
import functools
import math

import numpy as np
import jax
import jax.numpy as jnp
from jax import lax
from jax.experimental import pallas as pl
from jax.experimental.pallas import tpu as pltpu

F32 = jnp.float32
BF = jnp.bfloat16

D_MODEL = 1024
DEPTH = 2
PAST_LEN = 16384
GLA_H, GLA_DK, GLA_DV = 4, 32, 64
GLA_WIDTH = GLA_H * GLA_DV
GLA_GATE_RANK = 16
GLA_GATE_TEMP = 16.0
GLA_CHUNK = 16
RET_H, RET_DK, RET_DV = 4, 64, 64
RET_WIDTH = RET_H * RET_DV
ROPE_BASE = 10000.0
SSD_H, SSD_P, SSD_G, SSD_N = 8, 64, 2, 64
SSD_WIDTH = SSD_H * SSD_P
SSD_CONV_W = 4
SSD_CONV_DIM = SSD_WIDTH + 2 * SSD_G * SSD_N
MOE_GROUPS, MOE_PER_GROUP = 4, 4
MOE_EXPERTS = MOE_GROUPS * MOE_PER_GROUP
MOE_FF = 256
ALPHA = (2 * DEPTH) ** 0.25
EPS = 1e-5

LANE = 128
GLA_IN_W = 128 + 128 + 256 + LANE + 256
RET_IN_W = 4 * 256
SSD_IN_W = 512 + SSD_CONV_DIM + LANE
IN_W = GLA_IN_W + RET_IN_W + SSD_IN_W
VMEM_LIMIT = 56 * 1024 * 1024


def _cp(n_axes, vmem=VMEM_LIMIT):
    return pltpu.CompilerParams(dimension_semantics=("arbitrary",) * n_axes, vmem_limit_bytes=vmem)


def _dot(a, b):
    return jnp.dot(a, b, preferred_element_type=F32)


def _dot_nt(a, b):
    return lax.dot_general(a, b, (((1,), (1,)), ((), ())), preferred_element_type=F32)


def _dot_tn(a, b):
    return lax.dot_general(a, b, (((0,), (0,)), ((), ())), preferred_element_type=F32)


def _split3(x):
    hi = x.astype(BF)
    r = x - hi.astype(F32)
    mid = r.astype(BF)
    lo = (r - mid.astype(F32)).astype(BF)
    return hi, mid, lo


def _dot_x3(x, e):
    hi, mid, lo = _split3(x)
    return _dot(hi, e) + (_dot(mid, e) + _dot(lo, e))


def _dot_x2(x, e):
    hi = x.astype(BF)
    lo = (x - hi.astype(F32)).astype(BF)
    return _dot(hi, e) + _dot(lo, e)


def _dot_3x(e, x):
    hi, mid, lo = _split3(x)
    return _dot(e, hi) + (_dot(e, mid) + _dot(e, lo))


def _sigmoid(x):
    return 1.0 / (1.0 + jnp.exp(-x))


def _silu(x):
    return x * _sigmoid(x)


def _log_sigmoid(x):
    return jnp.minimum(x, 0.0) - jnp.log(1.0 + jnp.exp(-jnp.abs(x)))


def _softplus(x):
    return jnp.maximum(x, 0.0) + jnp.log(1.0 + jnp.exp(-jnp.abs(x)))


def _layer_norm(x, g, b):
    mu = jnp.mean(x, axis=-1, keepdims=True)
    d = x - mu
    var = jnp.mean(d * d, axis=-1, keepdims=True)
    return d * lax.rsqrt(var + EPS) * g + b


def _mod_kernel(c_ref, w_ref, b_ref, o_ref):
    s = _silu(c_ref[...]).astype(BF)
    o_ref[0] = _dot(s, w_ref[0].astype(BF)) + b_ref[0]


def _mod_call(c_all, w_ada, b_ada):
    R = c_all.shape[0]
    tn = 1536
    return pl.pallas_call(
        _mod_kernel,
        grid=(DEPTH, 6 * D_MODEL // tn),
        in_specs=[pl.BlockSpec((R, D_MODEL), lambda l, j: (0, 0)),
                  pl.BlockSpec((1, D_MODEL, tn), lambda l, j: (l, 0, j)),
                  pl.BlockSpec((1, 1, tn), lambda l, j: (l, 0, j))],
        out_specs=pl.BlockSpec((1, R, tn), lambda l, j: (l, 0, j)),
        out_shape=jax.ShapeDtypeStruct((DEPTH, R, 6 * D_MODEL), F32),
        compiler_params=_cp(2),
        name="ada_mod",
    )(c_all, w_ada, b_ada.reshape(DEPTH, 1, 6 * D_MODEL))


N_IN = 3096
N_GA = 128 + 128 + 256
N_DT = N_IN - SSD_H


def _inproj_kernel(x_ref, sc_ref, sh_ref, wt_ref, og_ref, or_ref, os_ref, w_ref):
    bB, bT, D = x_ref.shape

    @pl.when((pl.program_id(0) == 0) & (pl.program_id(1) == 0))
    def _():
        lane = lax.broadcasted_iota(jnp.int32, (1, LANE), 1)
        for j in range(N_GA // LANE):
            w_ref[:, j * LANE:(j + 1) * LANE] = wt_ref[j * LANE:(j + 1) * LANE, :].T.astype(BF)
        ga = wt_ref[N_GA:N_GA + LANE, :].T
        w_ref[:, N_GA:N_GA + LANE] = jnp.where(lane < GLA_GATE_RANK, ga, 0.0).astype(BF)
        src0, dst0 = N_GA + GLA_GATE_RANK, N_GA + LANE
        for j in range((N_DT - src0) // LANE):
            w_ref[:, dst0 + j * LANE:dst0 + (j + 1) * LANE] = \
                wt_ref[src0 + j * LANE:src0 + (j + 1) * LANE, :].T.astype(BF)
        dt = pltpu.roll(wt_ref[N_IN - LANE:N_IN, :].T, SSD_H, 1)
        w_ref[:, IN_W - LANE:IN_W] = jnp.where(lane < SSD_H, dt, 0.0).astype(BF)

    h = x_ref[...] * (1.0 + sc_ref[...]) + sh_ref[...]
    hb = h.reshape(bB * bT, D).astype(BF)
    og_ref[...] = _dot(hb, w_ref[:, 0:GLA_IN_W])
    or_ref[...] = _dot(hb, w_ref[:, GLA_IN_W:GLA_IN_W + RET_IN_W])
    os_ref[...] = _dot(hb, w_ref[:, GLA_IN_W + RET_IN_W:IN_W])


def _tok_tiles(B, T):
    if T >= 512:
        return 1, 512
    return 512 // T, T


def _inproj_call(x3, sc, sh, wt, l):
    B, T, D = x3.shape
    bB, bT = _tok_tiles(B, T)
    nT = T // bT
    R = bB * bT
    N = B * T
    xmap = lambda i, j: (i, j, 0)
    mmap = lambda i, j: (i, 0, 0)
    omap = lambda i, j: (i * nT + j, 0)
    return pl.pallas_call(
        _inproj_kernel,
        grid=(B // bB, nT),
        in_specs=[pl.BlockSpec((bB, bT, D), xmap),
                  pl.BlockSpec((bB, 1, D), mmap),
                  pl.BlockSpec((bB, 1, D), mmap),
                  _resident_layer(wt.shape, l)],
        out_specs=[pl.BlockSpec((R, GLA_IN_W), omap),
                   pl.BlockSpec((R, RET_IN_W), omap),
                   pl.BlockSpec((R, SSD_IN_W), omap)],
        out_shape=[jax.ShapeDtypeStruct((N, GLA_IN_W), F32),
                   jax.ShapeDtypeStruct((N, RET_IN_W), F32),
                   jax.ShapeDtypeStruct((N, SSD_IN_W), F32)],
        scratch_shapes=[pltpu.VMEM((D, IN_W), BF)],
        compiler_params=_cp(2),
        name="in_proj",
    )(x3, sc, sh, wt)


def _head_block_mask(rows_per, cols_per, n):
    r = np.arange(rows_per * n)[:, None] // rows_per
    c = np.arange(cols_per * n)[None, :] // cols_per
    return (r == c).astype(np.float32)


def _block_tril(n, c):
    i = np.arange(n)[:, None]
    j = np.arange(n)[None, :]
    return ((i // c == j // c) & (j <= i)).astype(np.float32)


def _gla_front(x_ref, wg_ref, bg_ref, L_ref):
    q = x_ref[:, 0:128] * (GLA_DK ** -0.5)
    k = x_ref[:, 128:256]
    v = x_ref[:, 256:512]
    ga = x_ref[:, 512:640]
    r = x_ref[:, 640:896]
    gate = _dot(ga.astype(BF), wg_ref[...]) + bg_ref[...]
    la = _log_sigmoid(gate) * (1.0 / GLA_GATE_TEMP)
    n = L_ref.shape[0]
    g = jnp.concatenate([_dot_3x(L_ref[...], la[i:i + n, :]) for i in range(0, la.shape[0], n)], axis=0)
    return q, k, v, r, g


def _gla_intra(q, g, kp_ref, gp_ref, vp_ref, E_ref, c):
    TT = q.shape[0]
    PAD = kp_ref.shape[0] - TT
    pos = lax.broadcasted_iota(jnp.int32, (TT, 1), 0) & (c - 1)
    o = jnp.zeros((TT, 2 * LANE), F32)
    for s in range(min(c, 8)):
        ks = kp_ref[pl.ds(PAD - s, TT), :]
        gs = gp_ref[pl.ds(PAD - s, TT), :]
        vs = vp_ref[pl.ds(PAD - s, TT), :]
        w = jnp.where(pos >= s, q * ks * jnp.exp(g - gs), 0.0)
        o = o + _dot(w.astype(BF), E_ref[...]) * vs
    if c <= 8:
        return o
    assert c == 16
    nc = TT // c

    def upper(x):
        return x.reshape(nc, 2, 8, x.shape[-1])[:, 1].reshape(nc * 8, x.shape[-1])

    qu, gu = upper(q), upper(g)
    posu = lax.broadcasted_iota(jnp.int32, (nc * 8, 1), 0) & 7
    ou = jnp.zeros((nc * 8, 2 * LANE), F32)
    for s in range(8, c):
        ks = upper(kp_ref[pl.ds(PAD - s, TT), :])
        gs = upper(gp_ref[pl.ds(PAD - s, TT), :])
        vs = upper(vp_ref[pl.ds(PAD - s, TT), :])
        w = jnp.where(posu >= s - 8, qu * ks * jnp.exp(gu - gs), 0.0)
        ou = ou + _dot(w.astype(BF), E_ref[...]) * vs
    ou = ou.reshape(nc, 1, 8, 2 * LANE)
    return o + jnp.concatenate([jnp.zeros_like(ou), ou], axis=1).reshape(TT, 2 * LANE)


def _gla_norm_gate(o, r, nw_ref, EA_ref):
    ms = _dot_x3(o * o, EA_ref[...])
    return o * lax.rsqrt(ms + EPS) * nw_ref[...] * _silu(r)


def _gla_prompt_kernel(x_ref, wg_ref, bg_ref, nw_ref, L_ref, E_ref, EA_ref, M_ref,
                       o_ref, sfin_ref, st_ref, kp_ref, gp_ref, vp_ref, oi_ref, u_ref, sb_ref, *, c):
    TT = x_ref.shape[0]
    nc = TT // c
    PAD = kp_ref.shape[0] - TT
    t = pl.program_id(1)

    @pl.when(t == 0)
    def _():
        st_ref[...] = jnp.zeros_like(st_ref)

    q, k, v, r, g = _gla_front(x_ref, wg_ref, bg_ref, L_ref)
    kp_ref[0:PAD, :] = jnp.zeros((PAD, LANE), F32)
    gp_ref[0:PAD, :] = jnp.zeros((PAD, LANE), F32)
    vp_ref[0:PAD, :] = jnp.zeros((PAD, 2 * LANE), F32)
    kp_ref[PAD:PAD + TT, :] = k
    gp_ref[PAD:PAD + TT, :] = g
    vp_ref[PAD:PAD + TT, :] = v
    o = _gla_intra(q, g, kp_ref, gp_ref, vp_ref, E_ref, c)

    M = M_ref[...]
    gl_all = gp_ref[pl.ds(PAD + c - 1, nc, stride=c), :]
    for n in range(nc):
        lo = n * c
        ke = (k[lo:lo + c, :] * jnp.exp(gl_all[n:n + 1, :] - g[lo:lo + c, :])).astype(BF)
        u_ref[n] = _dot_tn(ke, v[lo:lo + c, :].astype(BF)) * M
    a_cols = jnp.concatenate([jnp.exp(gl_all), jnp.zeros((LANE - nc, LANE), F32)], axis=0).T
    S = st_ref[...]
    for n in range(nc):
        sb_ref[n] = S.astype(BF)
        S = a_cols[:, n:n + 1] * S + u_ref[n]
    st_ref[...] = S
    qe = (q * jnp.exp(g)).astype(BF)
    for n in range(nc):
        lo = n * c
        oi_ref[lo:lo + c, :] = _dot(qe[lo:lo + c, :], sb_ref[n])
    o = o + oi_ref[...]
    o_ref[...] = _gla_norm_gate(o, r, nw_ref, EA_ref).astype(o_ref.dtype)

    @pl.when(t == pl.num_programs(1) - 1)
    def _():
        for h in range(GLA_H):
            sfin_ref[0, h] = S[h * GLA_DK:(h + 1) * GLA_DK, h * GLA_DV:(h + 1) * GLA_DV]


def _gla_tables(TT, c):
    L = jnp.asarray(_block_tril(TT, c), BF)
    E = jnp.asarray(_head_block_mask(GLA_DK, GLA_DV, GLA_H), BF)
    EA = jnp.asarray(_head_block_mask(GLA_DV, GLA_DV, GLA_H) / GLA_DV, BF)
    M = jnp.asarray(_head_block_mask(GLA_DK, GLA_DV, GLA_H), F32)
    return L, E, EA, M


def _gla_params(w_gate, b_gate, norm_w):
    wg = jnp.zeros((LANE, GLA_H * GLA_DK), F32).at[:GLA_GATE_RANK].set(w_gate).astype(BF)
    return wg, b_gate.reshape(1, -1), norm_w.reshape(1, -1)


def _const(shape):
    return pl.BlockSpec(shape, lambda *_: (0,) * len(shape))


def _gla_prompt_call(gin, B, T, w_gate, b_gate, norm_w):
    TT, c = (1024 if T % 1024 == 0 else 512), GLA_CHUNK
    nT = T // TT
    L, E, EA, M = _gla_tables(min(TT, 256), c)
    wg, bg, nw = _gla_params(w_gate, b_gate, norm_w)
    PAD = 16
    return pl.pallas_call(
        functools.partial(_gla_prompt_kernel, c=c),
        grid=(B, nT),
        in_specs=[pl.BlockSpec((TT, GLA_IN_W), lambda b, t: (b * nT + t, 0)),
                  _const(wg.shape), _const(bg.shape), _const(nw.shape),
                  _const(L.shape), _const(E.shape), _const(EA.shape), _const(M.shape)],
        out_specs=[pl.BlockSpec((TT, GLA_WIDTH), lambda b, t: (b * nT + t, 0)),
                   pl.BlockSpec((1, GLA_H, GLA_DK, GLA_DV), lambda b, t: (b, 0, 0, 0))],
        out_shape=[jax.ShapeDtypeStruct((B * T, GLA_WIDTH), BF),
                   jax.ShapeDtypeStruct((B, GLA_H, GLA_DK, GLA_DV), F32)],
        scratch_shapes=[pltpu.VMEM((GLA_H * GLA_DK, GLA_H * GLA_DV), F32),
                        pltpu.VMEM((TT + PAD, LANE), F32),
                        pltpu.VMEM((TT + PAD, LANE), F32),
                        pltpu.VMEM((TT + PAD, 2 * LANE), F32),
                        pltpu.VMEM((TT, 2 * LANE), F32),
                        pltpu.VMEM((TT // c, GLA_H * GLA_DK, GLA_H * GLA_DV), F32),
                        pltpu.VMEM((TT // c, GLA_H * GLA_DK, GLA_H * GLA_DV), BF)],
        compiler_params=_cp(2),
        name="gla_prompt",
    )(gin, wg, bg, nw, L, E, EA, M)


def _rope(x, cos, sin_signed):
    lane = lax.broadcasted_iota(jnp.int32, (1, LANE), 1)
    first_half = (lane & (RET_DK - 1)) < RET_DK // 2
    out = []
    for p in range(2):
        xs = x[:, p * LANE:(p + 1) * LANE]
        up = pltpu.roll(xs, LANE - RET_DK // 2, 1)
        dn = pltpu.roll(xs, RET_DK // 2, 1)
        out.append(xs * cos + jnp.where(first_half, up, dn) * sin_signed)
    return jnp.concatenate(out, axis=1)


def _ret_front(x_ref, cos_ref, sin_ref, rows=slice(None)):
    cos, sin = cos_ref[rows, :], sin_ref[rows, :]
    q = _rope(x_ref[rows, 0:256], cos, sin)
    k = _rope(x_ref[rows, 256:512], cos, sin) * (RET_DK ** -0.5)
    v = x_ref[rows, 512:768]
    rg = x_ref[rows, 768:1024]
    return q, k, v, rg


def _ret_intra(q, k, v, D_ref):
    lane = lax.broadcasted_iota(jnp.int32, (1, RET_WIDTH), 1)
    kb = k.astype(BF)
    o = jnp.zeros(q.shape, F32)
    for h in range(RET_H):
        hm = (lane // RET_DK) == h
        s = _dot_nt(jnp.where(hm, q, 0.0).astype(BF), kb)
        p = (s * D_ref[h]).astype(BF)
        o = o + _dot(p, jnp.where(hm, v, 0.0).astype(BF))
    return o


def _ret_norm_gate(o, rg, nw_ref, EA_ref):
    mu = _dot_x3(o, EA_ref[...])
    d = o - mu
    var = _dot_x3(d * d, EA_ref[...])
    return d * lax.rsqrt(var + EPS) * nw_ref[...] * _silu(rg)


def _ret_prompt_kernel(x_ref, cos_ref, sin_ref, D_ref, rd_ref, kd_ref, G_ref, M_ref, EA_ref, nw_ref,
                       o_ref, sfin_ref, st_ref):
    t = pl.program_id(1)

    @pl.when(t == 0)
    def _():
        st_ref[...] = jnp.zeros_like(st_ref)

    C = D_ref.shape[1]
    S = st_ref[...]
    for i in range(x_ref.shape[0] // C):
        rows = slice(i * C, (i + 1) * C)
        q, k, v, rg = _ret_front(x_ref, cos_ref, sin_ref, rows)
        o = _ret_intra(q, k, v, D_ref)
        o = o + _dot((q * rd_ref[...]).astype(BF), S.astype(BF))
        u = _dot_tn((k * kd_ref[...]).astype(BF), v.astype(BF))
        S = S * G_ref[...] + u * M_ref[...]
        o_ref[rows, :] = _ret_norm_gate(o, rg, nw_ref, EA_ref).astype(o_ref.dtype)
    st_ref[...] = S

    @pl.when(t == pl.num_programs(1) - 1)
    def _():
        for h in range(RET_H):
            sfin_ref[0, h] = S[h * RET_DK:(h + 1) * RET_DK, h * RET_DV:(h + 1) * RET_DV]


def _rope_tables(pos):
    half = RET_DK // 2
    inv = ROPE_BASE ** (-jnp.arange(half, dtype=F32) / half)
    ang = pos.astype(F32)[:, None] * inv[None, :]
    cos, sin = jnp.cos(ang), jnp.sin(ang)
    return jnp.tile(jnp.concatenate([cos, cos], 1), (1, 2)), jnp.tile(jnp.concatenate([-sin, sin], 1), (1, 2))


def _ret_log_gamma():
    return np.log(1.0 - 2.0 ** (-5.0 - np.arange(RET_H, dtype=np.float64)))


def _ret_prompt_call(rin, B, T, norm_w):
    C = 256
    TT = 4 * C if T % (4 * C) == 0 else C
    nT = T // TT
    cos, sin = _rope_tables(jnp.arange(T, dtype=jnp.int32))
    lg = _ret_log_gamma()
    i = np.arange(C)
    dec = np.exp(lg[:, None, None] * (i[:, None] - i[None, :])[None]) * (i[:, None] >= i[None, :])[None]
    Dm = jnp.asarray(dec, F32)
    rd = jnp.asarray(np.repeat(np.exp(lg[None, :] * (i[:, None] + 1)), RET_DK, 1), F32)
    kd = jnp.asarray(np.repeat(np.exp(lg[None, :] * (C - 1 - i[:, None])), RET_DK, 1), F32)
    M = _head_block_mask(RET_DK, RET_DV, RET_H)
    G = jnp.asarray(M * np.repeat(np.exp(lg * C), RET_DK)[:, None], F32)
    M = jnp.asarray(M, F32)
    EA = jnp.asarray(_head_block_mask(RET_DV, RET_DV, RET_H) / RET_DV, BF)
    nw = norm_w.reshape(1, -1)
    return pl.pallas_call(
        _ret_prompt_kernel,
        grid=(B, nT),
        in_specs=[pl.BlockSpec((TT, RET_IN_W), lambda b, t: (b * nT + t, 0)),
                  pl.BlockSpec((TT, LANE), lambda b, t: (t, 0)),
                  pl.BlockSpec((TT, LANE), lambda b, t: (t, 0)),
                  _const(Dm.shape), _const(rd.shape), _const(kd.shape), _const(G.shape), _const(M.shape),
                  _const(EA.shape), _const(nw.shape)],
        out_specs=[pl.BlockSpec((TT, RET_WIDTH), lambda b, t: (b * nT + t, 0)),
                   pl.BlockSpec((1, RET_H, RET_DK, RET_DV), lambda b, t: (b, 0, 0, 0))],
        out_shape=[jax.ShapeDtypeStruct((B * T, RET_WIDTH), BF),
                   jax.ShapeDtypeStruct((B, RET_H, RET_DK, RET_DV), F32)],
        scratch_shapes=[pltpu.VMEM((RET_H * RET_DK, RET_H * RET_DV), F32)],
        compiler_params=_cp(2),
        name="ret_prompt",
    )(rin, cos, sin, Dm, rd, kd, G, M, EA, nw)


def _ssd_conv(xp_ref, cw_ref, cb_ref, TT):
    acc = cb_ref[...] + cw_ref[SSD_CONV_W - 1:SSD_CONV_W, :] * xp_ref[pl.ds(8, TT), :]
    for i in range(SSD_CONV_W - 1):
        acc = acc + cw_ref[i:i + 1, :] * xp_ref[pl.ds(8 - (SSD_CONV_W - 1) + i, TT), :]
    return acc


def _ssd_intra(xs, bm, cm, g, dt, Mk_ref):
    TT = xs.shape[0]
    rT = (g - jnp.log(dt)).T
    lane = lax.broadcasted_iota(jnp.int32, (1, LANE), 1)
    lane2 = lax.broadcasted_iota(jnp.int32, (1, 2 * LANE), 1)
    causal = Mk_ref[...] > 0.0
    bmb = bm.astype(BF)
    zero = jnp.zeros((), BF)
    o_parts = []
    for grp in range(SSD_G):
        cb = _dot_nt(jnp.where((lane // SSD_N) == grp, cm, 0.0).astype(BF), bmb).astype(BF)
        xg = xs[:, grp * 2 * LANE:(grp + 1) * 2 * LANE].astype(BF)
        og = jnp.zeros((TT, 2 * LANE), F32)
        for h4 in range(SSD_H // SSD_G):
            h = grp * (SSD_H // SSD_G) + h4
            dec = jnp.where(causal, jnp.exp(g[:, h:h + 1] - rT[h:h + 1, :]), 0.0)
            p = cb * dec.astype(BF)
            og = og + _dot(p, jnp.where((lane2 // SSD_P) == h4, xg, zero))
        o_parts.append(og)
    return jnp.concatenate(o_parts, axis=1)


def _ssd_prompt_kernel(x_ref, cw_ref, cb_ref, dtb_ref, alog_ref, dexp_ref, nw_ref, L_ref, Mk_ref, Eexp_ref, M2_ref,
                       res_ref, g1_ref, og_ref, or_ref, wo_ref, lg_ref, lb_ref,
                       o_ref, sfin_ref, cfin_ref, st_ref, xp_ref):
    TT = x_ref.shape[0]
    t = pl.program_id(1)

    @pl.when(t == 0)
    def _():
        st_ref[...] = jnp.zeros_like(st_ref)
        xp_ref[0:8, :] = jnp.zeros((8, SSD_CONV_DIM), F32)

    z = x_ref[:, 0:SSD_WIDTH]
    xp_ref[8:8 + TT, :] = x_ref[:, SSD_WIDTH:SSD_WIDTH + SSD_CONV_DIM]
    sdt = x_ref[:, SSD_WIDTH + SSD_CONV_DIM:SSD_IN_W]
    xbc = _silu(_ssd_conv(xp_ref, cw_ref, cb_ref, TT))
    tail = xp_ref[TT:TT + 8, :]
    xp_ref[0:8, :] = tail
    dt_all = _softplus(sdt + dtb_ref[...])
    la_all = dt_all * (-jnp.exp(alog_ref[...]))
    Eexp = Eexp_ref[...]
    C = L_ref.shape[0]
    S = st_ref[...]
    for i in range(TT // C):
        rows = slice(i * C, (i + 1) * C)
        xs = xbc[rows, 0:SSD_WIDTH]
        bm = xbc[rows, SSD_WIDTH:SSD_WIDTH + LANE]
        cm = xbc[rows, SSD_WIDTH + LANE:SSD_CONV_DIM]
        dt = dt_all[rows, :]
        g = _dot_3x(L_ref[...], la_all[rows, :])
        gl = g[C - 1:C, :]
        eg_x = _dot_x2(jnp.exp(g), Eexp)
        cw_x = _dot_x2(dt * jnp.exp(gl - g), Eexp)
        egl_x = _dot_x2(jnp.exp(gl), Eexp)

        o = _ssd_intra(xs, bm, cm, g, dt, Mk_ref)
        o = o + eg_x * _dot(cm.astype(BF), S.astype(BF))
        u = _dot_tn(bm.astype(BF), (xs * cw_x).astype(BF))
        S = S * egl_x + u * M2_ref[...]

        y = (o + dexp_ref[...] * xs) * _silu(z[rows, :])
        ms = jnp.mean(y * y, axis=-1, keepdims=True)
        o_ssd = (y * lax.rsqrt(ms + EPS) * nw_ref[...]).astype(BF)
        merged = jnp.concatenate([og_ref[rows, :], or_ref[rows, :], o_ssd], axis=1)
        mix = _dot(merged, wo_ref[...])
        o_ref[0, rows, :] = _layer_norm(ALPHA * res_ref[0, rows, :] + g1_ref[0] * mix, lg_ref[0], lb_ref[0])
    st_ref[...] = S

    @pl.when(t == pl.num_programs(1) - 1)
    def _():
        for h in range(SSD_H):
            gi = h // (SSD_H // SSD_G)
            sfin_ref[0, h] = S[gi * SSD_N:(gi + 1) * SSD_N, h * SSD_P:(h + 1) * SSD_P]
        cfin_ref[0] = tail[8 - (SSD_CONV_W - 1):8, :]


def _pad_lanes(v, n=LANE):
    v = v.reshape(1, -1)
    return jnp.zeros((1, n), F32).at[:, :v.shape[1]].set(v)


def _ssd_tables(TT, c):
    L = jnp.asarray(_block_tril(TT, c), BF)
    Mk = jnp.asarray(_block_tril(TT, c), F32)
    e = np.zeros((LANE, SSD_WIDTH), np.float32)
    for h in range(SSD_H):
        e[h, h * SSD_P:(h + 1) * SSD_P] = 1.0
    M2 = np.zeros((SSD_G * SSD_N, SSD_WIDTH), np.float32)
    for h in range(SSD_H):
        gi = h // (SSD_H // SSD_G)
        M2[gi * SSD_N:(gi + 1) * SSD_N, h * SSD_P:(h + 1) * SSD_P] = 1.0
    return L, Mk, jnp.asarray(e, BF), jnp.asarray(M2, F32)


def _ssd_params(conv_w, conv_b, dt_bias, a_log, d, norm_w):
    return (conv_w, conv_b.reshape(1, -1), _pad_lanes(dt_bias), _pad_lanes(a_log),
            jnp.repeat(d, SSD_P).reshape(1, -1), norm_w.reshape(1, -1))


def _ssd_prompt_call(sin_, x3, g1, og, orr, w_out, ln_g, ln_b, conv_w, conv_b, dt_bias, a_log, d, norm_w):
    B, T, D = x3.shape
    C = 256
    TT = 4 * C if T % (4 * C) == 0 else (2 * C if T % (2 * C) == 0 else C)
    nT = T // TT
    L, Mk, Eexp, M2 = _ssd_tables(C, C)
    prm = _ssd_params(conv_w, conv_b, dt_bias, a_log, d, norm_w)
    rmap = lambda b, t: (b * nT + t, 0)
    return pl.pallas_call(
        _ssd_prompt_kernel,
        grid=(B, nT),
        in_specs=[pl.BlockSpec((TT, SSD_IN_W), rmap)]
                 + [_const(p.shape) for p in prm]
                 + [_const(L.shape), _const(Mk.shape), _const(Eexp.shape), _const(M2.shape)]
                 + [pl.BlockSpec((1, TT, D), lambda b, t: (b, t, 0)),
                    pl.BlockSpec((1, 1, D), lambda b, t: (b, 0, 0)),
                    pl.BlockSpec((TT, GLA_WIDTH), rmap), pl.BlockSpec((TT, RET_WIDTH), rmap),
                    _const((D, D)), _const((1, 1, D)), _const((1, 1, D))],
        out_specs=[pl.BlockSpec((1, TT, D), lambda b, t: (b, t, 0)),
                   pl.BlockSpec((1, SSD_H, SSD_N, SSD_P), lambda b, t: (b, 0, 0, 0)),
                   pl.BlockSpec((1, SSD_CONV_W - 1, SSD_CONV_DIM), lambda b, t: (b, 0, 0))],
        out_shape=[jax.ShapeDtypeStruct((B, T, D), F32),
                   jax.ShapeDtypeStruct((B, SSD_H, SSD_N, SSD_P), F32),
                   jax.ShapeDtypeStruct((B, SSD_CONV_W - 1, SSD_CONV_DIM), F32)],
        scratch_shapes=[pltpu.VMEM((SSD_G * SSD_N, SSD_WIDTH), F32),
                        pltpu.VMEM((TT + 8, SSD_CONV_DIM), F32)],
        compiler_params=_cp(2),
        name="ssd_outproj_ln",
    )(sin_, *prm, L, Mk, Eexp, M2, x3, g1, og, orr, w_out, ln_g.reshape(1, 1, D), ln_b.reshape(1, 1, D))


SEQ_TILE = 8


def _tile_lanes(n_rep, width):
    return np.tile(np.eye(width, dtype=np.float32), (1, n_rep))


def _fold_head_blocks(ubd):
    a = ubd[:, 0:LANE] + ubd[:, LANE:2 * LANE]
    return (a + pltpu.roll(a, LANE // 2, 1))[:, 0:LANE // 2]


def _col_bcast(row8, ones_ref):
    first = lax.broadcasted_iota(jnp.int32, (8, 1), 0) == 0
    hi, mid, lo = _split3(jnp.where(first, row8, 0.0))
    ones = ones_ref[...]
    return _dot_tn(hi, ones) + (_dot_tn(mid, ones) + _dot_tn(lo, ones))


def _gla_sample_kernel(x_ref, s0_ref, wg_ref, bg_ref, nw_ref, L_ref, E_ref, EA_ref, M_ref, T4_ref, ones_ref,
                       o_ref, sn_ref, kp_ref, gp_ref, vp_ref, oi_ref, *, c):
    TT = x_ref.shape[0]
    PAD = kp_ref.shape[0] - TT
    q, k, v, r, g = _gla_front(x_ref, wg_ref, bg_ref, L_ref)
    kp_ref[0:PAD, :] = jnp.zeros((PAD, LANE), F32)
    gp_ref[0:PAD, :] = jnp.zeros((PAD, LANE), F32)
    vp_ref[0:PAD, :] = jnp.zeros((PAD, 2 * LANE), F32)
    kp_ref[PAD:PAD + TT, :] = k
    gp_ref[PAD:PAD + TT, :] = g
    vp_ref[PAD:PAD + TT, :] = v
    o = _gla_intra(q, g, kp_ref, gp_ref, vp_ref, E_ref, c)
    qe = (q * jnp.exp(g)).astype(BF)
    M = M_ref[...]
    for s in range(TT // c):
        lo = s * c
        S0 = s0_ref[s].reshape(GLA_H * GLA_DK, GLA_DV)
        Sbd = (_dot(S0.astype(BF), T4_ref[...]) * M).astype(BF)
        oi_ref[lo:lo + c, :] = _dot(qe[lo:lo + c, :], Sbd)
        gl = g[lo + c - 1:lo + c, :]
        ke = (k[lo:lo + c, :] * jnp.exp(gl - g[lo:lo + c, :])).astype(BF)
        u = _fold_head_blocks(_dot_tn(ke, v[lo:lo + c, :].astype(BF)) * M)
        acol = _col_bcast(jnp.broadcast_to(jnp.exp(gl), (8, LANE)), ones_ref)
        sn_ref[s] = (acol * S0 + u).reshape(GLA_H, GLA_DK, GLA_DV)
    o = o + oi_ref[...]
    o_ref[...] = _gla_norm_gate(o, r, nw_ref, EA_ref).astype(o_ref.dtype)


def _gla_sample_call(gin, s0, B, T, w_gate, b_gate, norm_w):
    TT = SEQ_TILE * T
    L, E, EA, _ = _gla_tables(TT, T)
    M = jnp.asarray(_head_block_mask(GLA_DK, GLA_DV, GLA_H), F32)
    T4 = jnp.asarray(_tile_lanes(GLA_H, GLA_DV), BF)
    ones = jnp.ones((8, GLA_DV), BF)
    wg, bg, nw = _gla_params(w_gate, b_gate, norm_w)
    PAD = 8
    sspec = pl.BlockSpec((SEQ_TILE, GLA_H, GLA_DK, GLA_DV), lambda i: (i, 0, 0, 0))
    return pl.pallas_call(
        functools.partial(_gla_sample_kernel, c=T),
        grid=(B // SEQ_TILE,),
        in_specs=[pl.BlockSpec((TT, GLA_IN_W), lambda i: (i, 0)), sspec,
                  _const(wg.shape), _const(bg.shape), _const(nw.shape),
                  _const(L.shape), _const(E.shape), _const(EA.shape), _const(M.shape), _const(T4.shape),
                  _const(ones.shape)],
        out_specs=[pl.BlockSpec((TT, GLA_WIDTH), lambda i: (i, 0)), sspec],
        out_shape=[jax.ShapeDtypeStruct((B * T, GLA_WIDTH), BF),
                   jax.ShapeDtypeStruct((B, GLA_H, GLA_DK, GLA_DV), F32)],
        scratch_shapes=[pltpu.VMEM((TT + PAD, LANE), F32),
                        pltpu.VMEM((TT + PAD, LANE), F32),
                        pltpu.VMEM((TT + PAD, 2 * LANE), F32),
                        pltpu.VMEM((TT, 2 * LANE), F32)],
        compiler_params=_cp(1),
        name="gla_sample",
    )(gin, s0, wg, bg, nw, L, E, EA, M, T4, ones)


def _ret_sample_kernel(x_ref, s0_ref, cos_ref, sin_ref, D_ref, rd_ref, kd_ref, G_ref, M_ref, EA_ref, nw_ref, T4_ref,
                       o_ref, sn_ref, oi_ref, *, c):
    TT = x_ref.shape[0]
    q, k, v, rg = _ret_front(x_ref, cos_ref, sin_ref)
    o = _ret_intra(q, k, v, D_ref)
    qd = (q * rd_ref[...]).astype(BF)
    kd = (k * kd_ref[...]).astype(BF)
    vb = v.astype(BF)
    M = M_ref[...]
    for s in range(TT // c):
        lo = s * c
        S0 = s0_ref[s].reshape(RET_H * RET_DK, RET_DV)
        Sbd = (_dot(S0.astype(BF), T4_ref[...]) * M).astype(BF)
        oi_ref[lo:lo + c, :] = _dot(qd[lo:lo + c, :], Sbd)
        u = _fold_head_blocks(_dot_tn(kd[lo:lo + c, :], vb[lo:lo + c, :]) * M)
        sn_ref[s] = (G_ref[...] * S0 + u).reshape(RET_H, RET_DK, RET_DV)
    o = o + oi_ref[...]
    o_ref[...] = _ret_norm_gate(o, rg, nw_ref, EA_ref).astype(o_ref.dtype)


def _ret_sample_call(rin, s0, B, T, norm_w):
    TT = SEQ_TILE * T
    cos, sin = _rope_tables(PAST_LEN + jnp.arange(T, dtype=jnp.int32))
    cos, sin = jnp.tile(cos, (SEQ_TILE, 1)), jnp.tile(sin, (SEQ_TILE, 1))
    lg = _ret_log_gamma()
    i = np.arange(TT)
    same = (i[:, None] // T == i[None, :] // T) & (i[:, None] >= i[None, :])
    Dm = jnp.asarray(np.exp(lg[:, None, None] * (i[:, None] - i[None, :])[None]) * same[None], F32)
    tt = i % T
    rd = jnp.asarray(np.repeat(np.exp(lg[None, :] * (tt[:, None] + 1)), RET_DK, 1), F32)
    kd = jnp.asarray(np.repeat(np.exp(lg[None, :] * (T - 1 - tt[:, None])), RET_DK, 1), F32)
    G = jnp.asarray(np.repeat(np.repeat(np.exp(lg * T), RET_DK)[:, None], RET_DV, 1), F32)
    M = jnp.asarray(_head_block_mask(RET_DK, RET_DV, RET_H), F32)
    EA = jnp.asarray(_head_block_mask(RET_DV, RET_DV, RET_H) / RET_DV, BF)
    T4 = jnp.asarray(_tile_lanes(RET_H, RET_DV), BF)
    nw = norm_w.reshape(1, -1)
    sspec = pl.BlockSpec((SEQ_TILE, RET_H, RET_DK, RET_DV), lambda i: (i, 0, 0, 0))
    consts = (cos, sin, Dm, rd, kd, G, M, EA, nw, T4)
    return pl.pallas_call(
        functools.partial(_ret_sample_kernel, c=T),
        grid=(B // SEQ_TILE,),
        in_specs=[pl.BlockSpec((TT, RET_IN_W), lambda i: (i, 0)), sspec] + [_const(a.shape) for a in consts],
        out_specs=[pl.BlockSpec((TT, RET_WIDTH), lambda i: (i, 0)), sspec],
        out_shape=[jax.ShapeDtypeStruct((B * T, RET_WIDTH), BF),
                   jax.ShapeDtypeStruct((B, RET_H, RET_DK, RET_DV), F32)],
        scratch_shapes=[pltpu.VMEM((TT, RET_WIDTH), F32)],
        compiler_params=_cp(1),
        name="ret_sample",
    )(rin, s0, *consts)


def _ssd_sample_kernel(x_ref, c0_ref, s0_ref, cw_ref, cb_ref, dtb_ref, alog_ref, dexp_ref, nw_ref,
                       L_ref, Mk_ref, Eexp_ref, Bl_ref, R2_ref, T8_ref, T8T_ref, M8_ref, ones_ref,
                       o_ref, sn_ref, cn_ref, xp_ref, oi_ref, *, c):
    TT = x_ref.shape[0]
    ns = TT // c
    RP = 2 * c
    xp_ref[...] = jnp.zeros_like(xp_ref)
    z = x_ref[:, 0:SSD_WIDTH]
    sdt = x_ref[:, SSD_WIDTH + SSD_CONV_DIM:SSD_IN_W]
    for s in range(ns):
        base = 8 + s * RP
        xp_ref[base + c - (SSD_CONV_W - 1):base + c, :] = c0_ref[s]
        xp_ref[base + c:base + RP, :] = x_ref[s * c:(s + 1) * c, SSD_WIDTH:SSD_WIDTH + SSD_CONV_DIM]
    conv = _ssd_conv(xp_ref, cw_ref, cb_ref, ns * RP)
    xbc = _silu(conv.reshape(ns, RP, SSD_CONV_DIM)[:, c:RP, :].reshape(TT, SSD_CONV_DIM))
    for s in range(ns):
        base = 8 + s * RP
        cn_ref[s] = xp_ref[base + RP - (SSD_CONV_W - 1):base + RP, :]
    xs = xbc[:, 0:SSD_WIDTH]
    bm = xbc[:, SSD_WIDTH:SSD_WIDTH + LANE]
    cm = xbc[:, SSD_WIDTH + LANE:SSD_CONV_DIM]

    dt = _softplus(sdt + dtb_ref[...])
    la = dt * (-jnp.exp(alog_ref[...]))
    g = _dot_3x(L_ref[...], la)
    gl = _dot_3x(Bl_ref[...], g)
    Eexp = Eexp_ref[...]
    eg_x = _dot_x2(jnp.exp(g), Eexp)
    cw_x = _dot_x2(dt * jnp.exp(gl - g), Eexp)
    egl_x = _dot_x2(jnp.exp(gl), Eexp)
    o = _ssd_intra(xs, bm, cm, g, dt, Mk_ref)

    Cx = _dot(cm.astype(BF), R2_ref[...])
    Bx = _dot(bm.astype(BF), R2_ref[...])
    Xw = xs * cw_x
    M8 = M8_ref[...]
    nh = SSD_H

    def rows_by_head(a):
        return jnp.concatenate([a] * nh, axis=0) * M8

    for s in range(ns):
        lo = s * c
        S0 = s0_ref[s].reshape(SSD_H * SSD_N, SSD_P)
        oi = _dot(rows_by_head(Cx[lo:lo + c, :]).astype(BF), S0.astype(BF))
        oix = _dot_x2(oi, T8_ref[...]) * M8
        acc = oix[0:c, :]
        for h in range(1, nh):
            acc = acc + oix[h * c:(h + 1) * c, :]
        oi_ref[lo:lo + c, :] = acc
        Xst = _dot(rows_by_head(Xw[lo:lo + c, :]).astype(BF), T8T_ref[...])
        u = _dot_tn(rows_by_head(Bx[lo:lo + c, :]).astype(BF), Xst.astype(BF))
        acol = _col_bcast(egl_x[lo:lo + c, :], ones_ref)
        sn_ref[s] = (acol * S0 + u).reshape(SSD_H, SSD_N, SSD_P)

    o = o + eg_x * oi_ref[...]
    y = (o + dexp_ref[...] * xs) * _silu(z)
    ms = jnp.mean(y * y, axis=-1, keepdims=True)
    o_ref[...] = (y * lax.rsqrt(ms + EPS) * nw_ref[...]).astype(o_ref.dtype)


def _ssd_sample_call(sin_, c0, s0, B, T, conv_w, conv_b, dt_bias, a_log, d, norm_w):
    TT = SEQ_TILE * T
    L, Mk, Eexp, _ = _ssd_tables(TT, T)
    i = np.arange(TT)
    Bl = jnp.asarray((i[None, :] == (i[:, None] // T) * T + T - 1).astype(np.float32), BF)
    hpg = SSD_H // SSD_G
    R2 = np.zeros((LANE, SSD_H * SSD_N), np.float32)
    for h in range(SSD_H):
        R2[(h // hpg) * SSD_N:(h // hpg + 1) * SSD_N, h * SSD_N:(h + 1) * SSD_N] = np.eye(SSD_N)
    T8 = _tile_lanes(SSD_H, SSD_P)
    M8 = _head_block_mask(T, SSD_P, SSD_H)
    tabs = (L, Mk, Eexp, Bl, jnp.asarray(R2, BF), jnp.asarray(T8, BF), jnp.asarray(T8.T, BF), jnp.asarray(M8, F32),
            jnp.ones((8, SSD_P), BF))
    prm = _ssd_params(conv_w, conv_b, dt_bias, a_log, d, norm_w)
    sspec = pl.BlockSpec((SEQ_TILE, SSD_H, SSD_N, SSD_P), lambda i: (i, 0, 0, 0))
    cspec = pl.BlockSpec((SEQ_TILE, SSD_CONV_W - 1, SSD_CONV_DIM), lambda i: (i, 0, 0))
    return pl.pallas_call(
        functools.partial(_ssd_sample_kernel, c=T),
        grid=(B // SEQ_TILE,),
        in_specs=[pl.BlockSpec((TT, SSD_IN_W), lambda i: (i, 0)), cspec, sspec]
                 + [_const(p.shape) for p in prm] + [_const(a.shape) for a in tabs],
        out_specs=[pl.BlockSpec((TT, SSD_WIDTH), lambda i: (i, 0)), sspec, cspec],
        out_shape=[jax.ShapeDtypeStruct((B * T, SSD_WIDTH), BF),
                   jax.ShapeDtypeStruct((B, SSD_H, SSD_N, SSD_P), F32),
                   jax.ShapeDtypeStruct((B, SSD_CONV_W - 1, SSD_CONV_DIM), F32)],
        scratch_shapes=[pltpu.VMEM((8 + SEQ_TILE * 2 * T, SSD_CONV_DIM), F32),
                        pltpu.VMEM((TT, SSD_WIDTH), F32)],
        compiler_params=_cp(1),
        name="ssd_sample",
    )(sin_, c0, s0, *prm, *tabs)


def _inproj_t_kernel(x_ref, sc_ref, sh_ref, wt_ref, og_ref, or_ref, os_ref, w_ref):
    nt, nb, D = x_ref.shape

    @pl.when(pl.program_id(0) == 0)
    def _():
        for src, dst, n in ((0, 0, N_GA + GLA_GATE_RANK), (N_GA + GLA_GATE_RANK, N_GA + LANE, N_DT - N_GA - GLA_GATE_RANK)):
            for r in range(0, n, 512):
                m = min(512, n - r)
                w_ref[dst + r:dst + r + m, :] = wt_ref[src + r:src + r + m, :].astype(BF)
        w_ref[N_GA + GLA_GATE_RANK:N_GA + LANE, :] = jnp.zeros((LANE - GLA_GATE_RANK, D), BF)
        tail = jnp.concatenate([wt_ref[N_DT:N_IN, :], jnp.zeros((LANE - SSD_H, D), F32)], axis=0)
        w_ref[IN_W - LANE:IN_W, :] = tail.astype(BF)

    h = x_ref[...] * (1.0 + sc_ref[...]) + sh_ref[...]
    for t in range(0, nt, 2):
        ht = h[t:t + 2].reshape(2 * nb, D).astype(BF)
        cols = slice(t * nb, (t + 2) * nb)
        og_ref[:, cols] = _dot_nt(w_ref[0:GLA_IN_W, :], ht)
        or_ref[:, cols] = _dot_nt(w_ref[GLA_IN_W:GLA_IN_W + RET_IN_W, :], ht)
        os_ref[:, cols] = _dot_nt(w_ref[GLA_IN_W + RET_IN_W:IN_W, :], ht)


def _inproj_t_call(xt, sc, sh, wt, l):
    T, B, D = xt.shape
    nt = 4
    cmap = lambda i: (0, i)
    return pl.pallas_call(
        _inproj_t_kernel,
        grid=(T // nt,),
        in_specs=[pl.BlockSpec((nt, B, D), lambda i: (i, 0, 0)),
                  pl.BlockSpec((1, B, D), lambda i: (0, 0, 0)),
                  pl.BlockSpec((1, B, D), lambda i: (0, 0, 0)),
                  _resident_layer(wt.shape, l)],
        out_specs=[pl.BlockSpec((GLA_IN_W, nt * B), cmap),
                   pl.BlockSpec((RET_IN_W, nt * B), cmap),
                   pl.BlockSpec((SSD_IN_W, nt * B), cmap)],
        out_shape=[jax.ShapeDtypeStruct((GLA_IN_W, T * B), F32),
                   jax.ShapeDtypeStruct((RET_IN_W, T * B), F32),
                   jax.ShapeDtypeStruct((SSD_IN_W, T * B), F32)],
        scratch_shapes=[pltpu.VMEM((IN_W, D), BF)],
        compiler_params=_cp(1),
        name="in_proj_t",
    )(xt, sc, sh, wt)


def _row_sum(x):
    return jnp.sum(x, axis=0, keepdims=True)


def _lane_state_readout(o, coef_ref, s0_ref, n_rows):
    nb = LANE
    half = len(o) // 2
    for part in range(2):
        def body(k8, accs, part=part):
            accs = list(accs)
            base = pl.multiple_of(k8 * 8, 8)
            grp = [coef_ref[pl.ds(base, 8), (part * half + i) * nb:(part * half + i + 1) * nb] for i in range(half)]
            for j in range(8):
                s0k = s0_ref[0, k8 * 8 + j]
                for i in range(half):
                    accs[i] = accs[i] + grp[i][j:j + 1, :] * s0k
            return tuple(accs)

        res = lax.fori_loop(0, n_rows // 8, body, tuple(o[part * half:(part + 1) * half]))
        o[part * half:(part + 1) * half] = list(res)
    return o


def _lane_state_update(sn_ref, s0_ref, decay_fn, coef_ref, val_fn, n_rows, T):
    nb = LANE

    def body(k8, carry):
        base = pl.multiple_of(k8 * 8, 8)
        grp = [coef_ref[pl.ds(base, 8), t * nb:(t + 1) * nb] for t in range(T)]
        dec = decay_fn(base)
        for j in range(8):
            dj = dec[j:j + 1, :] if dec.shape[0] == 8 else dec
            sk = dj * s0_ref[0, k8 * 8 + j]
            for t in range(T):
                sk = sk + grp[t][j:j + 1, :] * val_fn(t)
            sn_ref[0, 0, k8 * 8 + j] = sk
        return carry

    lax.fori_loop(0, n_rows // 8, body, 0)


def _state_specs(shape, l, first):
    assert l == 0 or not first
    tail = tuple(shape[2:])
    in_spec = pl.BlockSpec((None, 1) + tail, lambda h: (l, h, 0, 0, 0))
    out_spec = pl.BlockSpec(((shape[0] if first else 1), 1) + tail, lambda h: (l, h, 0, 0, 0))
    return in_spec, out_spec


def _zero_later_layers(ref):
    ref[1:] = jnp.zeros((ref.shape[0] - 1,) + tuple(ref.shape[1:]), ref.dtype)


def _finish_state_call(kern, n_in, first, prevs):
    if first:
        return functools.partial(kern, first=True), [], {}
    wrapped = lambda *a, **kw: kern(*a[:n_in], *a[n_in + len(prevs):], first=False, **kw)
    specs = [pl.BlockSpec(memory_space=pl.ANY)] * len(prevs)
    return wrapped, specs, {n_in + i: 1 + i for i in range(len(prevs))}


def _gla_t_kernel(x_ref, s0_ref, wg_ref, bg_ref, nw_ref, o_ref, sn_ref, qe_ref, ke_ref, a_ref, *, T, first):
    nb = LANE
    if first:
        _zero_later_layers(sn_ref)
    h = pl.program_id(0)
    r0 = pl.multiple_of(h * GLA_DK, GLA_DK)
    v0 = pl.multiple_of(h * GLA_DV, GLA_DV)
    q = x_ref[pl.ds(r0, GLA_DK), :] * (GLA_DK ** -0.5)
    k = x_ref[pl.ds(128 + r0, GLA_DK), :]
    gate = _dot(wg_ref[pl.ds(r0, GLA_DK), :], x_ref[512:640, :].astype(BF)) + bg_ref[pl.ds(r0, GLA_DK), :]
    la = _log_sigmoid(gate) * (1.0 / GLA_GATE_TEMP)
    gs = []
    acc = jnp.zeros((GLA_DK, nb), F32)
    for t in range(T):
        acc = acc + la[:, t * nb:(t + 1) * nb]
        gs.append(acc)
    gl = gs[T - 1]
    a_ref[...] = jnp.exp(gl)
    qs = [q[:, t * nb:(t + 1) * nb] for t in range(T)]
    ks = [k[:, t * nb:(t + 1) * nb] for t in range(T)]
    for t in range(T):
        qe_ref[:, t * nb:(t + 1) * nb] = qs[t] * jnp.exp(gs[t])
        ke_ref[:, t * nb:(t + 1) * nb] = ks[t] * jnp.exp(gl - gs[t])

    def vt(t):
        return x_ref[pl.ds(256 + v0, GLA_DV), t * nb:(t + 1) * nb]

    o = []
    for t in range(T):
        ot = jnp.zeros((GLA_DV, nb), F32)
        for u in range(t + 1):
            s = _row_sum(qs[t] * ks[u] * jnp.exp(gs[t] - gs[u]))
            ot = ot + s * vt(u)
        o.append(ot)

    o = _lane_state_readout(o, qe_ref, s0_ref, GLA_DK)
    _lane_state_update(sn_ref, s0_ref, lambda base: a_ref[pl.ds(base, 8), :], ke_ref, vt, GLA_DK, T)

    nw = nw_ref[pl.ds(v0, GLA_DV), :]
    for t in range(T):
        ms = jnp.mean(o[t] * o[t], axis=0, keepdims=True)
        r = x_ref[pl.ds(640 + v0, GLA_DV), t * nb:(t + 1) * nb]
        o_ref[:, t * nb:(t + 1) * nb] = (o[t] * lax.rsqrt(ms + EPS) * nw * _silu(r)).astype(o_ref.dtype)


def _gla_t_call(gT, s0, prev, l, T, w_gate, b_gate, norm_w):
    N = gT.shape[1]
    wg = jnp.zeros((GLA_H * GLA_DK, LANE), F32).at[:, :GLA_GATE_RANK].set(w_gate.T).astype(BF)
    bg = b_gate.reshape(-1, 1)
    nw = norm_w.reshape(-1, 1)
    first = prev is None
    prevs = [] if first else [prev]
    s_in, s_out = _state_specs(s0.shape, l, first)
    ins = [gT, s0, wg, bg, nw]
    specs = [_const(gT.shape), s_in, _const(wg.shape), _const(bg.shape), _const(nw.shape)]
    kern, pspecs, aliases = _finish_state_call(functools.partial(_gla_t_kernel, T=T), len(ins), first, prevs)
    ins, specs = ins + prevs, specs + pspecs
    return pl.pallas_call(
        kern,
        grid=(GLA_H,),
        in_specs=specs,
        out_specs=[pl.BlockSpec((GLA_DV, N), lambda h: (h, 0)), s_out],
        out_shape=[jax.ShapeDtypeStruct((GLA_WIDTH, N), BF), jax.ShapeDtypeStruct(s0.shape, F32)],
        scratch_shapes=[pltpu.VMEM((GLA_DK, N), F32), pltpu.VMEM((GLA_DK, N), F32), pltpu.VMEM((GLA_DK, LANE), F32)],
        input_output_aliases=aliases,
        compiler_params=_cp(1),
        name="gla_t",
    )(*ins)


def _ret_t_kernel(x_ref, s0_ref, cos_ref, sin_ref, pw_ref, nw_ref, o_ref, sn_ref, qd_ref, kd_ref, *, T, first):
    nb = LANE
    if first:
        _zero_later_layers(sn_ref)
    h = pl.program_id(0)
    r0 = pl.multiple_of(h * RET_DK, RET_DK)
    half_k = RET_DK // 2
    cos, sin = cos_ref[...], sin_ref[...]

    def rope_t(base):
        x1 = x_ref[pl.ds(base + r0, half_k), :]
        x2 = x_ref[pl.ds(base + r0 + half_k, half_k), :]
        return jnp.concatenate([x1 * cos - x2 * sin, x1 * sin + x2 * cos], axis=0)

    q = rope_t(0)
    k = rope_t(256) * (RET_DK ** -0.5)
    pw = pw_ref[h]
    qs = [q[:, t * nb:(t + 1) * nb] for t in range(T)]
    ks = [k[:, t * nb:(t + 1) * nb] for t in range(T)]
    for t in range(T):
        qd_ref[:, t * nb:(t + 1) * nb] = qs[t] * pw[t + 1:t + 2, :]
        kd_ref[:, t * nb:(t + 1) * nb] = ks[t] * pw[T - 1 - t:T - t, :]

    def vt(t):
        return x_ref[pl.ds(512 + r0, RET_DV), t * nb:(t + 1) * nb]

    o = []
    for t in range(T):
        ot = jnp.zeros((RET_DV, nb), F32)
        for u in range(t + 1):
            s = _row_sum(qs[t] * ks[u]) * pw[t - u:t - u + 1, :]
            ot = ot + s * vt(u)
        o.append(ot)

    o = _lane_state_readout(o, qd_ref, s0_ref, RET_DK)
    _lane_state_update(sn_ref, s0_ref, lambda base: pw[T:T + 1, :], kd_ref, vt, RET_DK, T)

    nw = nw_ref[pl.ds(r0, RET_DV), :]
    for t in range(T):
        mu = jnp.mean(o[t], axis=0, keepdims=True)
        d = o[t] - mu
        var = jnp.mean(d * d, axis=0, keepdims=True)
        rg = x_ref[pl.ds(768 + r0, RET_DV), t * nb:(t + 1) * nb]
        o_ref[:, t * nb:(t + 1) * nb] = (d * lax.rsqrt(var + EPS) * nw * _silu(rg)).astype(o_ref.dtype)


def _ret_t_call(rT, s0, prev, l, T, norm_w):
    N = rT.shape[1]
    B = N // T
    half = RET_DK // 2
    inv = ROPE_BASE ** (-jnp.arange(half, dtype=F32) / half)
    ang = inv[:, None] * (PAST_LEN + jnp.arange(T, dtype=jnp.int32)).astype(F32)[None, :]
    cos = jnp.repeat(jnp.cos(ang), B, axis=1)
    sin = jnp.repeat(jnp.sin(ang), B, axis=1)
    lg = _ret_log_gamma()
    pw = jnp.asarray(np.repeat(np.exp(lg[:, None] * np.arange(16)[None, :])[:, :, None], LANE, axis=2), F32)
    nw = norm_w.reshape(-1, 1)
    first = prev is None
    prevs = [] if first else [prev]
    s_in, s_out = _state_specs(s0.shape, l, first)
    ins = [rT, s0, cos, sin, pw, nw]
    specs = [_const(rT.shape), s_in, _const(cos.shape), _const(sin.shape), _const(pw.shape), _const(nw.shape)]
    kern, pspecs, aliases = _finish_state_call(functools.partial(_ret_t_kernel, T=T), len(ins), first, prevs)
    ins, specs = ins + prevs, specs + pspecs
    return pl.pallas_call(
        kern,
        grid=(RET_H,),
        in_specs=specs,
        out_specs=[pl.BlockSpec((RET_DV, N), lambda h: (h, 0)), s_out],
        out_shape=[jax.ShapeDtypeStruct((RET_WIDTH, N), BF), jax.ShapeDtypeStruct(s0.shape, F32)],
        scratch_shapes=[pltpu.VMEM((RET_DK, N), F32), pltpu.VMEM((RET_DK, N), F32)],
        input_output_aliases=aliases,
        compiler_params=_cp(1),
        name="ret_t",
    )(*ins)


def _ssd_t_kernel(x_ref, c0_ref, s0_ref, cw_ref, cb_ref, dtb_ref, alog_ref, d_ref, nw_ref,
                  o_ref, sn_ref, cn_ref, hist_ref, y_ref, ssq_ref, cm_ref, bw_ref, xw_ref, *, T, first):
    nb = LANE
    W1 = SSD_CONV_W - 1
    h = pl.program_id(0)
    XB = SSD_WIDTH
    if first:
        _zero_later_layers(sn_ref)

    @pl.when(h == 0)
    def _():
        ssq_ref[...] = jnp.zeros_like(ssq_ref)
        if first:
            _zero_later_layers(cn_ref)
        for i in range(W1):
            for j in range(SSD_CONV_DIM // LANE):
                hist_ref[j * LANE:(j + 1) * LANE, i * nb:(i + 1) * nb] = c0_ref[0, i][:, j * LANE:(j + 1) * LANE].T
                cn_ref[0, i, :, j * LANE:(j + 1) * LANE] = \
                    x_ref[XB + j * LANE:XB + (j + 1) * LANE, (T - W1 + i) * nb:(T - W1 + i + 1) * nb].T

    def conv_rows(ro):
        w = cw_ref[pl.ds(ro, 64), :]
        b = cb_ref[pl.ds(ro, 64), :]
        xx = [hist_ref[pl.ds(ro, 64), i * nb:(i + 1) * nb] for i in range(W1)]
        xx += [x_ref[pl.ds(XB + ro, 64), t * nb:(t + 1) * nb] for t in range(T)]
        out = []
        for t in range(T):
            acc = b + w[:, 0:1] * xx[t]
            for i in range(1, SSD_CONV_W):
                acc = acc + w[:, i:i + 1] * xx[t + i]
            out.append(_silu(acc))
        return out

    grp = h // (SSD_H // SSD_G)
    xs = conv_rows(pl.multiple_of(h * SSD_P, SSD_P))
    bm = conv_rows(pl.multiple_of(SSD_WIDTH + grp * SSD_N, SSD_N))
    cm = conv_rows(pl.multiple_of(SSD_WIDTH + SSD_G * SSD_N + grp * SSD_N, SSD_N))

    dt_all = _softplus(x_ref[pl.ds(XB + SSD_CONV_DIM + h, 1), :] + dtb_ref[pl.ds(h, 1), :])
    a = -jnp.exp(alog_ref[pl.ds(h, 1), :])
    dts = [dt_all[:, t * nb:(t + 1) * nb] for t in range(T)]
    gs = []
    acc = jnp.zeros((1, nb), F32)
    for t in range(T):
        acc = acc + dts[t] * a
        gs.append(acc)
    gl = gs[T - 1]

    o = []
    for t in range(T):
        ot = jnp.zeros((SSD_P, nb), F32)
        for u in range(t + 1):
            s = _row_sum(cm[t] * bm[u]) * (jnp.exp(gs[t] - gs[u]) * dts[u])
            ot = ot + s * xs[u]
        o.append(ot)

    for t in range(T):
        cm_ref[:, t * nb:(t + 1) * nb] = cm[t] * jnp.exp(gs[t])
        bw_ref[:, t * nb:(t + 1) * nb] = bm[t]
        xw_ref[:, t * nb:(t + 1) * nb] = xs[t] * (dts[t] * jnp.exp(gl - gs[t]))

    o = _lane_state_readout(o, cm_ref, s0_ref, SSD_N)
    egl = jnp.exp(gl)
    _lane_state_update(sn_ref, s0_ref, lambda base: egl, bw_ref, lambda t: xw_ref[:, t * nb:(t + 1) * nb], SSD_N, T)

    dd = d_ref[pl.ds(h, 1), :]
    p0 = pl.multiple_of(h * SSD_P, SSD_P)
    for t in range(T):
        z = x_ref[pl.ds(p0, SSD_P), t * nb:(t + 1) * nb]
        y = (o[t] + dd * xs[t]) * _silu(z)
        y_ref[pl.ds(p0, SSD_P), t * nb:(t + 1) * nb] = y
        ssq_ref[:, t * nb:(t + 1) * nb] += _row_sum(y * y)

    @pl.when(h == SSD_H - 1)
    def _():
        scale = lax.rsqrt(ssq_ref[...] * (1.0 / SSD_WIDTH) + EPS)
        o_ref[...] = (y_ref[...] * scale * nw_ref[...]).astype(o_ref.dtype)


def _ssd_t_call(sT, c0, s0, prev_s, prev_c, l, T, conv_w, conv_b, dt_bias, a_log, d, norm_w):
    N = sT.shape[1]
    col = lambda v: jnp.zeros((LANE, 1), F32).at[:SSD_H, 0].set(v)
    prm = (conv_w.T, conv_b.reshape(-1, 1), col(dt_bias), col(a_log), col(d), norm_w.reshape(-1, 1))
    first = prev_s is None
    prevs = [] if first else [prev_s, prev_c]
    s_in, s_out = _state_specs(s0.shape, l, first)
    c_in = pl.BlockSpec((1,) + tuple(c0.shape[1:]), lambda h: (l, 0, 0, 0))
    c_out = pl.BlockSpec(((c0.shape[0] if first else 1),) + tuple(c0.shape[1:]), lambda h: (l, 0, 0, 0))
    ins = [sT, c0, s0, *prm]
    specs = [_const(sT.shape), c_in, s_in] + [_const(p.shape) for p in prm]
    kern, pspecs, aliases = _finish_state_call(functools.partial(_ssd_t_kernel, T=T), len(ins), first, prevs)
    ins, specs = ins + prevs, specs + pspecs
    return pl.pallas_call(
        kern,
        grid=(SSD_H,),
        in_specs=specs,
        out_specs=[_const((SSD_WIDTH, N)), s_out, c_out],
        out_shape=[jax.ShapeDtypeStruct((SSD_WIDTH, N), BF), jax.ShapeDtypeStruct(s0.shape, F32),
                   jax.ShapeDtypeStruct(c0.shape, F32)],
        scratch_shapes=[pltpu.VMEM((SSD_CONV_DIM, (SSD_CONV_W - 1) * LANE), F32),
                        pltpu.VMEM((SSD_WIDTH, N), F32), pltpu.VMEM((1, N), F32),
                        pltpu.VMEM((SSD_N, N), F32), pltpu.VMEM((SSD_N, N), F32), pltpu.VMEM((SSD_P, N), F32)],
        input_output_aliases=aliases,
        compiler_params=_cp(1),
        name="ssd_t",
    )(*ins)


def _outproj_t_kernel(x_ref, g_ref, og_ref, or_ref, os_ref, w_ref, lg_ref, lb_ref, o_ref):
    nt, nb, D = x_ref.shape
    for t in range(nt):
        cols = slice(t * nb, (t + 1) * nb)
        mix = (_dot_tn(og_ref[:, cols], w_ref[0:GLA_WIDTH, :])
               + _dot_tn(or_ref[:, cols], w_ref[GLA_WIDTH:GLA_WIDTH + RET_WIDTH, :])
               + _dot_tn(os_ref[:, cols], w_ref[GLA_WIDTH + RET_WIDTH:D, :]))
        y = ALPHA * x_ref[t] + g_ref[0] * mix
        o_ref[t] = _layer_norm(y, lg_ref[0], lb_ref[0])


def _outproj_t_call(xt, g1, ogT, orT, osT, w_out, ln_g, ln_b):
    T, B, D = xt.shape
    nt = 4
    cmap = lambda i: (0, i)
    return pl.pallas_call(
        _outproj_t_kernel,
        grid=(T // nt,),
        in_specs=[pl.BlockSpec((nt, B, D), lambda i: (i, 0, 0)),
                  pl.BlockSpec((1, B, D), lambda i: (0, 0, 0)),
                  pl.BlockSpec((GLA_WIDTH, nt * B), cmap),
                  pl.BlockSpec((RET_WIDTH, nt * B), cmap),
                  pl.BlockSpec((SSD_WIDTH, nt * B), cmap),
                  _const((D, D)), _const((1, 1, D)), _const((1, 1, D))],
        out_specs=pl.BlockSpec((nt, B, D), lambda i: (i, 0, 0)),
        out_shape=jax.ShapeDtypeStruct((T, B, D), F32),
        compiler_params=_cp(1),
        name="out_proj_ln_t",
    )(xt, g1, ogT, orT, osT, w_out, ln_g.reshape(1, 1, D), ln_b.reshape(1, 1, D))


def _outproj_kernel(x_ref, g_ref, og_ref, or_ref, os_ref, w_ref, lg_ref, lb_ref, o_ref):
    bB, bT, D = x_ref.shape
    assert bB == 1
    n_part = 2
    for i in range(n_part):
        rows = slice(i * bT // n_part, (i + 1) * bT // n_part)
        mix = (_dot(og_ref[rows, :], w_ref[0:GLA_WIDTH, :])
               + _dot(or_ref[rows, :], w_ref[GLA_WIDTH:GLA_WIDTH + RET_WIDTH, :])
               + _dot(os_ref[rows, :], w_ref[GLA_WIDTH + RET_WIDTH:D, :]))
        y = ALPHA * x_ref[0, rows, :] + g_ref[0] * mix
        o_ref[0, rows, :] = _layer_norm(y, lg_ref[0], lb_ref[0])


def _outproj_call(x3, g1, og, orr, os_, w_out, ln_g, ln_b):
    B, T, D = x3.shape
    bB, bT = _tok_tiles(B, T)
    nT = T // bT
    R = bB * bT
    xmap = lambda i, j: (i, j, 0)
    mmap = lambda i, j: (i, 0, 0)
    rmap = lambda i, j: (i * nT + j, 0)
    return pl.pallas_call(
        _outproj_kernel,
        grid=(B // bB, nT),
        in_specs=[pl.BlockSpec((bB, bT, D), xmap),
                  pl.BlockSpec((bB, 1, D), mmap),
                  pl.BlockSpec((R, GLA_WIDTH), rmap),
                  pl.BlockSpec((R, RET_WIDTH), rmap),
                  pl.BlockSpec((R, SSD_WIDTH), rmap),
                  _const((D, D)), _const((1, 1, D)), _const((1, 1, D))],
        out_specs=pl.BlockSpec((bB, bT, D), xmap),
        out_shape=jax.ShapeDtypeStruct((B, T, D), F32),
        compiler_params=_cp(2),
        name="out_proj_ln",
    )(x3, g1, og, orr, os_, w_out, ln_g.reshape(1, 1, D), ln_b.reshape(1, 1, D))


ROUTE_OFF = 8


def _moe_route_t(lt):
    R = lt.shape[1]
    neg = jnp.float32(-jnp.inf)
    row8 = lax.broadcasted_iota(jnp.int32, (8, 1), 0)
    lg = jnp.where(row8 < MOE_GROUPS, lt[0:8, :], neg)
    mg = jnp.max(lg, axis=0, keepdims=True)
    gsel = jnp.min(jnp.where(lg == mg, row8, 8), axis=0, keepdims=True)
    g_gate = 1.0 / jnp.sum(jnp.exp(lg - mg), axis=0, keepdims=True)
    rowe = lax.broadcasted_iota(jnp.int32, (MOE_EXPERTS, 1), 0)
    le = jnp.where((rowe // MOE_PER_GROUP) == gsel, lt[ROUTE_OFF:ROUTE_OFF + MOE_EXPERTS, :], neg)
    m1 = jnp.max(le, axis=0, keepdims=True)
    i1 = jnp.min(jnp.where(le == m1, rowe, MOE_EXPERTS), axis=0, keepdims=True)
    le2 = jnp.where(rowe == i1, neg, le)
    m2 = jnp.max(le2, axis=0, keepdims=True)
    i2 = jnp.min(jnp.where(le2 == m2, rowe, MOE_EXPERTS), axis=0, keepdims=True)
    e2 = jnp.exp(m2 - m1)
    w1 = g_gate / (1.0 + e2)
    w2 = g_gate * e2 / (1.0 + e2)
    comb = jnp.where(rowe == i1, w1, jnp.where(rowe == i2, w2, 0.0))
    cg = comb[0:4, :]
    for g in range(1, MOE_GROUPS):
        cg = cg + comb[g * MOE_PER_GROUP:(g + 1) * MOE_PER_GROUP, :]
    return gsel, cg, comb


MOE_SUB = 256
MOE_BLK = 16
MOE_ROWS = 256
MOE_NPS = MOE_SUB + MOE_GROUPS * MOE_BLK
assert MOE_SUB <= MOE_ROWS


def _moe_kernel(x_ref, sc_ref, sh_ref, g_ref, wr_ref, br_ref, us_ref, w1_ref, w3_ref, w2_ref, lg_ref, lb_ref,
                o_ref, hb_ref, cwb_ref, hp_ref, cwp_ref, yp_ref, pos_ref,
                cgrp_ref, fill_ref, cur_ref, na_ref, nb_ref, so_ref, nfa_ref, dsa_ref, dsb_ref, *, n_steps):
    bB, bT, D = x_ref.shape
    R = bB * bT
    n_q = R // MOE_SUB
    s = pl.program_id(1)
    x = x_ref[...]
    row8 = lax.broadcasted_iota(jnp.int32, (8, 1), 0)
    slot = lax.broadcasted_iota(jnp.int32, (MOE_NPS, 1), 0).astype(F32)

    @pl.when((pl.program_id(0) == 0) & (s == 0))
    def _():
        hb_ref[...] = jnp.zeros_like(hb_ref)
        cwb_ref[...] = jnp.zeros_like(cwb_ref)
        yp_ref[...] = jnp.zeros_like(yp_ref)

    @pl.when(s == 0)
    def _():
        na_ref[0] = 0
        for g in range(MOE_GROUPS):
            cur_ref[g] = -1
            fill_ref[g] = 0

    @pl.when(s < n_steps)
    def _():
        h = (x * (1.0 + sc_ref[...]) + sh_ref[...]).reshape(R, D)
        segs, offs = [], []
        for q in range(n_q):
            u = s * n_q + q
            hq = h[q * MOE_SUB:(q + 1) * MOE_SUB, :].astype(BF)
            gsel, cg, _ = _moe_route_t(_dot_nt(wr_ref[...], hq) + br_ref[...])
            onehot = jnp.where(row8 == gsel, 1.0, 0.0)
            rank = _dot(onehot.astype(BF), us_ref[...])
            cnt = jnp.sum(onehot, axis=1, keepdims=True)
            seg = jnp.ceil(cnt * (1.0 / MOE_BLK)) * MOE_BLK
            off = jnp.zeros((8, 1), F32)
            for g in range(1, MOE_GROUPS):
                off = off + jnp.where(row8 >= g, seg[g - 1:g, :], 0.0)
            pos = jnp.sum(onehot * (off + rank), axis=0, keepdims=True)
            pos_ref[u] = jnp.broadcast_to(pos, (8, MOE_SUB))
            perm = jnp.where(slot == pos, 1.0, 0.0).astype(BF)
            hp_ref[q] = _dot(perm, hq).astype(BF)
            cg8 = jnp.concatenate([cg, jnp.zeros((4, MOE_SUB), F32)], axis=0)
            cg_hi = cg8.astype(BF)
            cg_lo = (cg8 - cg_hi.astype(F32)).astype(BF)
            cwp_ref[q] = _dot_nt(perm, cg_hi) + _dot_nt(perm, cg_lo)
            segs.append(seg)
            offs.append(off)
        for q in range(n_q):
            u = s * n_q + q
            for g in range(MOE_GROUPS):
                so = offs[q][g, 0].astype(jnp.int32)
                nb = (segs[q][g, 0] * (1.0 / MOE_BLK)).astype(jnp.int32)
                f = fill_ref[g]
                c = cur_ref[g]
                na = na_ref[0]
                room = jnp.where(c < 0, 0, (MOE_ROWS - f) // MOE_BLK)
                n_a = jnp.minimum(nb, room)
                n_b = nb - n_a
                base_a = c * MOE_ROWS + f
                base_b = na * MOE_ROWS
                idx = u * MOE_GROUPS + g
                so_ref[idx] = so
                nb_ref[idx] = nb
                nfa_ref[idx] = n_a
                dsa_ref[idx] = base_a
                dsb_ref[idx] = base_b

                def put(k, carry, so=so, q=q, n_a=n_a, base_a=base_a, base_b=base_b):
                    dst = pl.multiple_of(jnp.where(k < n_a, base_a + k * MOE_BLK, base_b + (k - n_a) * MOE_BLK), MOE_BLK)
                    src = pl.multiple_of(so + k * MOE_BLK, MOE_BLK)
                    hb_ref[pl.ds(dst, MOE_BLK), :] = hp_ref[q, pl.ds(src, MOE_BLK), :]
                    cwb_ref[pl.ds(dst, MOE_BLK), :] = cwp_ref[q, pl.ds(src, MOE_BLK), :]
                    return carry

                lax.fori_loop(0, nb, put, 0)

                @pl.when(n_b > 0)
                def _(g=g, na=na, n_b=n_b):
                    cgrp_ref[na] = g
                    na_ref[0] = na + 1
                    cur_ref[g] = na
                    fill_ref[g] = n_b * MOE_BLK

                @pl.when(n_b == 0)
                def _(g=g, f=f, n_a=n_a):
                    fill_ref[g] = f + n_a * MOE_BLK

    @pl.when(s == n_steps - 1)
    def _():
        def chunk(c, carry):
            g = cgrp_ref[c]
            start = pl.multiple_of(c * MOE_ROWS, MOE_ROWS)
            hc = hb_ref[pl.ds(start, MOE_ROWS), :]
            cw = cwb_ref[pl.ds(start, MOE_ROWS), :]
            hids = []
            for j in range(MOE_PER_GROUP):
                e = g * MOE_PER_GROUP + j
                hid = _silu(_dot(hc, w1_ref[e])) * _dot(hc, w3_ref[e]) * cw[:, j:j + 1]
                hids.append(hid.astype(BF))
            w2g = w2_ref[pl.ds(g * MOE_PER_GROUP, MOE_PER_GROUP)].reshape(MOE_PER_GROUP * MOE_FF, D)
            hb_ref[pl.ds(start, MOE_ROWS), :] = _dot(jnp.concatenate(hids, axis=1), w2g).astype(BF)
            return carry

        lax.fori_loop(0, na_ref[0], chunk, 0)

    @pl.when(s >= n_steps)
    def _():
        for q in range(n_q):
            u = (s - n_steps) * n_q + q
            for g in range(MOE_GROUPS):
                idx = u * MOE_GROUPS + g
                so, n_a, base_a, base_b = so_ref[idx], nfa_ref[idx], dsa_ref[idx], dsb_ref[idx]

                def take(k, carry, so=so, q=q, n_a=n_a, base_a=base_a, base_b=base_b):
                    src = pl.multiple_of(jnp.where(k < n_a, base_a + k * MOE_BLK, base_b + (k - n_a) * MOE_BLK), MOE_BLK)
                    dst = pl.multiple_of(so + k * MOE_BLK, MOE_BLK)
                    yp_ref[q, pl.ds(dst, MOE_BLK), :] = hb_ref[pl.ds(src, MOE_BLK), :]
                    return carry

                lax.fori_loop(0, nb_ref[idx], take, 0)
        ys = []
        for q in range(n_q):
            u = (s - n_steps) * n_q + q
            perm = jnp.where(slot == pos_ref[u][0:1, :], 1.0, 0.0).astype(BF)
            ys.append(_dot_tn(perm, yp_ref[q]))
        y = jnp.concatenate(ys, axis=0)
        z = ALPHA * x + g_ref[...] * y.reshape(bB, bT, D)
        o_ref[...] = _layer_norm(z, lg_ref[...], lb_ref[...])


def _resident_layer(shape, l):
    return pl.BlockSpec((None,) + tuple(shape[1:]), lambda *_: (l,) + (0,) * (len(shape) - 1),
                        pipeline_mode=pl.Buffered(1))


def _moe_call(x3, sc, sh, g2, wr, br, w1, w3, w2, l, ln_g, ln_b):
    B, T, D = x3.shape
    bB, bT = _tok_tiles(B, T)
    R = bB * bT
    if bB == 1:
        spp = 2 if B % 2 == 0 else 1
        nT = T // bT
        n_pools, n_steps = B // spp, spp * nT
        xmap = lambda p, s: (p * spp + (s % n_steps) // nT, (s % n_steps) % nT, 0)
        omap = lambda p, s: (p * spp + jnp.maximum(s - n_steps, 0) // nT, jnp.maximum(s - n_steps, 0) % nT, 0)
        mmap = lambda p, s: (p * spp + (s % n_steps) // nT, 0, 0)
        mshape = (1, 1, D)
    else:
        n_pools, n_steps = 1, B // bB
        xmap = lambda p, s: (s % n_steps, 0, 0)
        omap = lambda p, s: (jnp.maximum(s - n_steps, 0), 0, 0)
        mmap = lambda p, s: (0, 0, 0)
        mshape = (1, bT, D)
    n_sub = n_steps * (R // MOE_SUB)
    n_chunks = pl.cdiv(n_sub * (MOE_SUB + MOE_GROUPS * (MOE_BLK - 1)), MOE_ROWS) + MOE_GROUPS
    us = jnp.asarray(np.triu(np.ones((MOE_SUB, MOE_SUB), np.float32), 1), BF)
    smem = lambda n: pltpu.SMEM((n,), jnp.int32)
    return pl.pallas_call(
        functools.partial(_moe_kernel, n_steps=n_steps),
        grid=(n_pools, 2 * n_steps),
        in_specs=[pl.BlockSpec((bB, bT, D), xmap),
                  pl.BlockSpec(mshape, mmap), pl.BlockSpec(mshape, mmap), pl.BlockSpec(mshape, mmap),
                  _const(wr.shape), _const(br.shape), _const(us.shape),
                  _resident_layer(w1.shape, l), _resident_layer(w3.shape, l), _resident_layer(w2.shape, l),
                  _const((1, 1, D)), _const((1, 1, D))],
        out_specs=pl.BlockSpec((bB, bT, D), omap),
        out_shape=jax.ShapeDtypeStruct((B, T, D), F32),
        scratch_shapes=[pltpu.VMEM((n_chunks * MOE_ROWS, D), BF), pltpu.VMEM((n_chunks * MOE_ROWS, 8), F32),
                        pltpu.VMEM((R // MOE_SUB, MOE_NPS, D), BF), pltpu.VMEM((R // MOE_SUB, MOE_NPS, 8), F32),
                        pltpu.VMEM((R // MOE_SUB, MOE_NPS, D), BF),
                        pltpu.VMEM((n_sub, 8, MOE_SUB), F32),
                        smem(n_chunks), smem(MOE_GROUPS), smem(MOE_GROUPS), smem(1),
                        *[smem(n_sub * MOE_GROUPS) for _ in range(5)]],
        compiler_params=_cp(2),
        name="moe_ln",
    )(x3, sc, sh, g2, wr, br, us, w1, w3, w2, ln_g.reshape(1, 1, D), ln_b.reshape(1, 1, D))


def _router_params(w_group, b_group, w_expert, b_expert):
    wr = jnp.zeros((LANE, D_MODEL), F32).at[:MOE_GROUPS].set(w_group.T)
    wr = wr.at[ROUTE_OFF:ROUTE_OFF + MOE_EXPERTS].set(w_expert.T)
    br = jnp.zeros((LANE, 1), F32).at[:MOE_GROUPS, 0].set(b_group).at[ROUTE_OFF:ROUTE_OFF + MOE_EXPERTS, 0].set(b_expert)
    return wr.astype(BF), br


def kernel(x_prompt, x_sample, c_prompt, c_sample, state_gla, state_ret, state_ssd, state_conv, w_ada, b_ada, w_in, gla_w_gate, gla_b_gate, gla_norm, ret_norm, ssd_conv_w, ssd_conv_b, ssd_dt_bias, ssd_a_log, ssd_d, ssd_norm, w_out, ln1_g, ln1_b, moe_w_group, moe_b_group, moe_w_expert, moe_b_expert, moe_w1, moe_w3, moe_w2, ln2_g, ln2_b):
    Bp, Tp, D = x_prompt.shape
    Bs, Ts, _ = x_sample.shape
    w_in_t = jnp.swapaxes(w_in, 1, 2)
    w_out_b = w_out.astype(BF)
    w1_b, w3_b, w2_b = moe_w1.astype(BF), moe_w3.astype(BF), moe_w2.astype(BF)

    mod = _mod_call(jnp.concatenate([c_prompt, c_sample], axis=0), w_ada, b_ada)

    def moe(x, sc2, sh2, g2, l):
        wr, br = _router_params(moe_w_group[l], moe_b_group[l], moe_w_expert[l], moe_b_expert[l])
        return _moe_call(x, sc2, sh2, g2, wr, br, w1_b, w3_b, w2_b, l, ln2_g[l], ln2_b[l])

    x = x_prompt
    new = [[], [], [], []]
    for l in range(DEPTH):
        sh1, sc1, g1, sh2, sc2, g2 = (mod[l, :Bp, None, i * D:(i + 1) * D] for i in range(6))
        gin, rin, sin_ = _inproj_call(x, sc1, sh1, w_in_t, l)
        og, s_gla = _gla_prompt_call(gin, Bp, Tp, gla_w_gate[l], gla_b_gate[l], gla_norm[l])
        orr, s_ret = _ret_prompt_call(rin, Bp, Tp, ret_norm[l])
        x, s_ssd, s_conv = _ssd_prompt_call(sin_, x, g1, og, orr, w_out_b[l], ln1_g[l], ln1_b[l], ssd_conv_w[l],
                                            ssd_conv_b[l], ssd_dt_bias[l], ssd_a_log[l], ssd_d[l], ssd_norm[l])
        x = moe(x, sc2, sh2, g2, l)
        for acc, s in zip(new, (s_gla, s_ret, s_ssd, s_conv)):
            acc.append(s)
    y_p = x
    gla_p, ret_p, ssd_p, conv_p = (jnp.stack(a) for a in new)

    x = jnp.swapaxes(x_sample, 0, 1)
    sg = jnp.transpose(state_gla, (0, 2, 3, 4, 1))
    sr = jnp.transpose(state_ret, (0, 2, 3, 4, 1))
    ss = jnp.transpose(state_ssd, (0, 2, 3, 4, 1))
    cv = jnp.transpose(state_conv, (0, 2, 1, 3))
    gla_n = ret_n = ssd_n = conv_n = None
    for l in range(DEPTH):
        sh1, sc1, g1, sh2, sc2, g2 = (mod[l, None, Bp:, i * D:(i + 1) * D] for i in range(6))
        gT, rT, sT = _inproj_t_call(x, sc1, sh1, w_in_t, l)
        ogT, gla_n = _gla_t_call(gT, sg, gla_n, l, Ts, gla_w_gate[l], gla_b_gate[l], gla_norm[l])
        orT, ret_n = _ret_t_call(rT, sr, ret_n, l, Ts, ret_norm[l])
        osT, ssd_n, conv_n = _ssd_t_call(sT, cv, ss, ssd_n, conv_n, l, Ts, ssd_conv_w[l], ssd_conv_b[l],
                                         ssd_dt_bias[l], ssd_a_log[l], ssd_d[l], ssd_norm[l])
        x = _outproj_t_call(x, g1, ogT, orT, osT, w_out_b[l], ln1_g[l], ln1_b[l])
        x = moe(x, sc2, sh2, g2, l)
    y_s = jnp.swapaxes(x, 0, 1)
    gla_s = jnp.transpose(gla_n, (0, 4, 1, 2, 3))
    ret_s = jnp.transpose(ret_n, (0, 4, 1, 2, 3))
    ssd_s = jnp.transpose(ssd_n, (0, 4, 1, 2, 3))
    conv_s = jnp.transpose(conv_n, (0, 2, 1, 3))
    return (y_p, y_s, gla_p, ret_p, ssd_p, conv_p, gla_s, ret_s, ssd_s, conv_s)
```

```python
import functools

import numpy as np
import jax
import jax.numpy as jnp
from jax import lax
from jax.experimental import pallas as pl
from jax.experimental.pallas import tpu as pltpu

F32 = jnp.float32
BF = jnp.bfloat16

D_MODEL = 1024
DEPTH = 2
PAST_LEN = 16384
GLA_H, GLA_DK, GLA_DV = 4, 32, 64
GLA_WIDTH = GLA_H * GLA_DV
GLA_GATE_RANK = 16
GLA_GATE_TEMP = 16.0
GLA_CHUNK = 16
RET_H, RET_DK, RET_DV = 4, 64, 64
RET_WIDTH = RET_H * RET_DV
ROPE_BASE = 10000.0
SSD_H, SSD_P, SSD_G, SSD_N = 8, 64, 2, 64
SSD_WIDTH = SSD_H * SSD_P
SSD_CONV_W = 4
SSD_CONV_DIM = SSD_WIDTH + 2 * SSD_G * SSD_N
MOE_GROUPS, MOE_PER_GROUP = 4, 4
MOE_EXPERTS = MOE_GROUPS * MOE_PER_GROUP
MOE_FF = 256
ALPHA = (2 * DEPTH) ** 0.25
EPS = 1e-5

LANE = 128
GLA_IN_W = 128 + 128 + 256 + LANE + 256
RET_IN_W = 4 * 256
SSD_IN_W = 512 + SSD_CONV_DIM + LANE
IN_W = GLA_IN_W + RET_IN_W + SSD_IN_W
VMEM_LIMIT = 56 * 1024 * 1024


def _cp(n_axes, vmem=VMEM_LIMIT):
    return pltpu.CompilerParams(dimension_semantics=("arbitrary",) * n_axes, vmem_limit_bytes=vmem)


def _dot(a, b):
    return jnp.dot(a, b, preferred_element_type=F32)


def _dot_nt(a, b):
    return lax.dot_general(a, b, (((1,), (1,)), ((), ())), preferred_element_type=F32)


def _dot_tn(a, b):
    return lax.dot_general(a, b, (((0,), (0,)), ((), ())), preferred_element_type=F32)


def _split3(x):
    hi = x.astype(BF)
    r = x - hi.astype(F32)
    mid = r.astype(BF)
    lo = (r - mid.astype(F32)).astype(BF)
    return hi, mid, lo


def _dot_x3(x, e):
    hi, mid, lo = _split3(x)
    return _dot(hi, e) + (_dot(mid, e) + _dot(lo, e))


def _dot_x2(x, e):
    hi = x.astype(BF)
    lo = (x - hi.astype(F32)).astype(BF)
    return _dot(hi, e) + _dot(lo, e)


def _dot_3x(e, x):
    hi, mid, lo = _split3(x)
    return _dot(e, hi) + (_dot(e, mid) + _dot(e, lo))


def _sigmoid(x):
    return 1.0 / (1.0 + jnp.exp(-x))


def _silu(x):
    return x * _sigmoid(x)


def _log_sigmoid(x):
    return jnp.minimum(x, 0.0) - jnp.log(1.0 + jnp.exp(-jnp.abs(x)))


def _softplus(x):
    return jnp.maximum(x, 0.0) + jnp.log(1.0 + jnp.exp(-jnp.abs(x)))


def _layer_norm(x, g, b):
    mu = jnp.mean(x, axis=-1, keepdims=True)
    d = x - mu
    var = jnp.mean(d * d, axis=-1, keepdims=True)
    return d * lax.rsqrt(var + EPS) * g + b


def _const(shape):
    return pl.BlockSpec(shape, lambda *_: (0,) * len(shape))


def _resident_layer(shape, l):
    return pl.BlockSpec((None,) + tuple(shape[1:]), lambda *_: (l,) + (0,) * (len(shape) - 1),
                        pipeline_mode=pl.Buffered(1))


def _mod_kernel(c_ref, w_ref, b_ref, o_ref):
    s = _silu(c_ref[...]).astype(BF)
    o_ref[0] = _dot(s, w_ref[0].astype(BF)) + b_ref[0]


def _mod_call(c_all, w_ada, b_ada):
    R = c_all.shape[0]
    tn = 1536
    return pl.pallas_call(
        _mod_kernel,
        grid=(DEPTH, 6 * D_MODEL // tn),
        in_specs=[pl.BlockSpec((R, D_MODEL), lambda l, j: (0, 0)),
                  pl.BlockSpec((1, D_MODEL, tn), lambda l, j: (l, 0, j)),
                  pl.BlockSpec((1, 1, tn), lambda l, j: (l, 0, j))],
        out_specs=pl.BlockSpec((1, R, tn), lambda l, j: (l, 0, j)),
        out_shape=jax.ShapeDtypeStruct((DEPTH, R, 6 * D_MODEL), F32),
        compiler_params=_cp(2),
        name="ada_mod",
    )(c_all, w_ada, b_ada.reshape(DEPTH, 1, 6 * D_MODEL))


N_IN = 3096
N_GA = 128 + 128 + 256
N_DT = N_IN - SSD_H


def _inproj_kernel(x_ref, sc_ref, sh_ref, wt_ref, og_ref, or_ref, os_ref, w_ref):
    bB, bT, D = x_ref.shape

    @pl.when((pl.program_id(0) == 0) & (pl.program_id(1) == 0))
    def _():
        lane = lax.broadcasted_iota(jnp.int32, (1, LANE), 1)
        for j in range(N_GA // LANE):
            w_ref[:, j * LANE:(j + 1) * LANE] = wt_ref[j * LANE:(j + 1) * LANE, :].T.astype(BF)
        ga = wt_ref[N_GA:N_GA + LANE, :].T
        w_ref[:, N_GA:N_GA + LANE] = jnp.where(lane < GLA_GATE_RANK, ga, 0.0).astype(BF)
        src0, dst0 = N_GA + GLA_GATE_RANK, N_GA + LANE
        for j in range((N_DT - src0) // LANE):
            w_ref[:, dst0 + j * LANE:dst0 + (j + 1) * LANE] = \
                wt_ref[src0 + j * LANE:src0 + (j + 1) * LANE, :].T.astype(BF)
        dt = pltpu.roll(wt_ref[N_IN - LANE:N_IN, :].T, SSD_H, 1)
        w_ref[:, IN_W - LANE:IN_W] = jnp.where(lane < SSD_H, dt, 0.0).astype(BF)

    h = x_ref[...] * (1.0 + sc_ref[...]) + sh_ref[...]
    hb = h.reshape(bB * bT, D).astype(BF)
    og_ref[...] = _dot(hb, w_ref[:, 0:GLA_IN_W])
    or_ref[...] = _dot(hb, w_ref[:, GLA_IN_W:GLA_IN_W + RET_IN_W])
    os_ref[...] = _dot(hb, w_ref[:, GLA_IN_W + RET_IN_W:IN_W])


def _tok_tiles(B, T):
    if T >= 512:
        return 1, 512
    return 512 // T, T


def _inproj_call(x3, sc, sh, wt, l):
    B, T, D = x3.shape
    bB, bT = _tok_tiles(B, T)
    nT = T // bT
    R = bB * bT
    N = B * T
    xmap = lambda i, j: (i, j, 0)
    mmap = lambda i, j: (i, 0, 0)
    omap = lambda i, j: (i * nT + j, 0)
    return pl.pallas_call(
        _inproj_kernel,
        grid=(B // bB, nT),
        in_specs=[pl.BlockSpec((bB, bT, D), xmap),
                  pl.BlockSpec((bB, 1, D), mmap),
                  pl.BlockSpec((bB, 1, D), mmap),
                  _resident_layer(wt.shape, l)],
        out_specs=[pl.BlockSpec((R, GLA_IN_W), omap),
                   pl.BlockSpec((R, RET_IN_W), omap),
                   pl.BlockSpec((R, SSD_IN_W), omap)],
        out_shape=[jax.ShapeDtypeStruct((N, GLA_IN_W), F32),
                   jax.ShapeDtypeStruct((N, RET_IN_W), F32),
                   jax.ShapeDtypeStruct((N, SSD_IN_W), F32)],
        scratch_shapes=[pltpu.VMEM((D, IN_W), BF)],
        compiler_params=_cp(2),
        name="in_proj",
    )(x3, sc, sh, wt)


def _head_block_mask(rows_per, cols_per, n):
    r = np.arange(rows_per * n)[:, None] // rows_per
    c = np.arange(cols_per * n)[None, :] // cols_per
    return (r == c).astype(np.float32)


def _block_tril(n, c):
    i = np.arange(n)[:, None]
    j = np.arange(n)[None, :]
    return ((i // c == j // c) & (j <= i)).astype(np.float32)


def _gla_front(x_ref, wg_ref, bg_ref, L_ref):
    q = x_ref[:, 0:128] * (GLA_DK ** -0.5)
    k = x_ref[:, 128:256]
    v = x_ref[:, 256:512]
    ga = x_ref[:, 512:640]
    r = x_ref[:, 640:896]
    gate = _dot(ga.astype(BF), wg_ref[...]) + bg_ref[...]
    la = _log_sigmoid(gate) * (1.0 / GLA_GATE_TEMP)
    n = L_ref.shape[0]
    g = jnp.concatenate([_dot_3x(L_ref[...], la[i:i + n, :]) for i in range(0, la.shape[0], n)], axis=0)
    return q, k, v, r, g


def _gla_intra(q, g, kp_ref, gp_ref, vp_ref, E_ref, c):
    TT = q.shape[0]
    PAD = kp_ref.shape[0] - TT
    pos = lax.broadcasted_iota(jnp.int32, (TT, 1), 0) & (c - 1)
    o = jnp.zeros((TT, 2 * LANE), F32)
    for s in range(min(c, 8)):
        ks = kp_ref[pl.ds(PAD - s, TT), :]
        gs = gp_ref[pl.ds(PAD - s, TT), :]
        vs = vp_ref[pl.ds(PAD - s, TT), :]
        w = jnp.where(pos >= s, q * ks * jnp.exp(g - gs), 0.0)
        o = o + _dot(w.astype(BF), E_ref[...]) * vs
    if c <= 8:
        return o
    assert c == 16
    nc = TT // c

    def upper(x):
        return x.reshape(nc, 2, 8, x.shape[-1])[:, 1].reshape(nc * 8, x.shape[-1])

    qu, gu = upper(q), upper(g)
    posu = lax.broadcasted_iota(jnp.int32, (nc * 8, 1), 0) & 7
    ou = jnp.zeros((nc * 8, 2 * LANE), F32)
    for s in range(8, c):
        ks = upper(kp_ref[pl.ds(PAD - s, TT), :])
        gs = upper(gp_ref[pl.ds(PAD - s, TT), :])
        vs = upper(vp_ref[pl.ds(PAD - s, TT), :])
        w = jnp.where(posu >= s - 8, qu * ks * jnp.exp(gu - gs), 0.0)
        ou = ou + _dot(w.astype(BF), E_ref[...]) * vs
    ou = ou.reshape(nc, 1, 8, 2 * LANE)
    return o + jnp.concatenate([jnp.zeros_like(ou), ou], axis=1).reshape(TT, 2 * LANE)


def _gla_norm_gate(o, r, nw_ref, EA_ref):
    ms = _dot_x3(o * o, EA_ref[...])
    return o * lax.rsqrt(ms + EPS) * nw_ref[...] * _silu(r)


def _gla_prompt_kernel(x_ref, wg_ref, bg_ref, nw_ref, L_ref, E_ref, EA_ref, M_ref,
                       o_ref, sfin_ref, st_ref, kp_ref, gp_ref, vp_ref, oi_ref, u_ref, sb_ref, *, c):
    TT = x_ref.shape[0]
    nc = TT // c
    PAD = kp_ref.shape[0] - TT
    t = pl.program_id(1)

    @pl.when(t == 0)
    def _():
        st_ref[...] = jnp.zeros_like(st_ref)

    q, k, v, r, g = _gla_front(x_ref, wg_ref, bg_ref, L_ref)
    kp_ref[0:PAD, :] = jnp.zeros((PAD, LANE), F32)
    gp_ref[0:PAD, :] = jnp.zeros((PAD, LANE), F32)
    vp_ref[0:PAD, :] = jnp.zeros((PAD, 2 * LANE), F32)
    kp_ref[PAD:PAD + TT, :] = k
    gp_ref[PAD:PAD + TT, :] = g
    vp_ref[PAD:PAD + TT, :] = v
    o = _gla_intra(q, g, kp_ref, gp_ref, vp_ref, E_ref, c)

    M = M_ref[...]
    gl_all = gp_ref[pl.ds(PAD + c - 1, nc, stride=c), :]
    for n in range(nc):
        lo = n * c
        ke = (k[lo:lo + c, :] * jnp.exp(gl_all[n:n + 1, :] - g[lo:lo + c, :])).astype(BF)
        u_ref[n] = _dot_tn(ke, v[lo:lo + c, :].astype(BF)) * M
    a_cols = jnp.concatenate([jnp.exp(gl_all), jnp.zeros((LANE - nc, LANE), F32)], axis=0).T
    S = st_ref[...]
    for n in range(nc):
        sb_ref[n] = S.astype(BF)
        S = a_cols[:, n:n + 1] * S + u_ref[n]
    st_ref[...] = S
    qe = (q * jnp.exp(g)).astype(BF)
    for n in range(nc):
        lo = n * c
        oi_ref[lo:lo + c, :] = _dot(qe[lo:lo + c, :], sb_ref[n])
    o = o + oi_ref[...]
    o_ref[...] = _gla_norm_gate(o, r, nw_ref, EA_ref).astype(o_ref.dtype)

    @pl.when(t == pl.num_programs(1) - 1)
    def _():
        for h in range(GLA_H):
            sfin_ref[0, h] = S[h * GLA_DK:(h + 1) * GLA_DK, h * GLA_DV:(h + 1) * GLA_DV]


def _gla_tables(TT, c):
    L = jnp.asarray(_block_tril(TT, c), BF)
    E = jnp.asarray(_head_block_mask(GLA_DK, GLA_DV, GLA_H), BF)
    EA = jnp.asarray(_head_block_mask(GLA_DV, GLA_DV, GLA_H) / GLA_DV, BF)
    M = jnp.asarray(_head_block_mask(GLA_DK, GLA_DV, GLA_H), F32)
    return L, E, EA, M


def _gla_params(w_gate, b_gate, norm_w):
    wg = jnp.zeros((LANE, GLA_H * GLA_DK), F32).at[:GLA_GATE_RANK].set(w_gate).astype(BF)
    return wg, b_gate.reshape(1, -1), norm_w.reshape(1, -1)


def _gla_prompt_call(gin, B, T, w_gate, b_gate, norm_w):
    TT, c = (1024 if T % 1024 == 0 else 512), GLA_CHUNK
    nT = T // TT
    L, E, EA, M = _gla_tables(min(TT, 256), c)
    wg, bg, nw = _gla_params(w_gate, b_gate, norm_w)
    PAD = 16
    return pl.pallas_call(
        functools.partial(_gla_prompt_kernel, c=c),
        grid=(B, nT),
        in_specs=[pl.BlockSpec((TT, GLA_IN_W), lambda b, t: (b * nT + t, 0)),
                  _const(wg.shape), _const(bg.shape), _const(nw.shape),
                  _const(L.shape), _const(E.shape), _const(EA.shape), _const(M.shape)],
        out_specs=[pl.BlockSpec((TT, GLA_WIDTH), lambda b, t: (b * nT + t, 0)),
                   pl.BlockSpec((1, GLA_H, GLA_DK, GLA_DV), lambda b, t: (b, 0, 0, 0))],
        out_shape=[jax.ShapeDtypeStruct((B * T, GLA_WIDTH), BF),
                   jax.ShapeDtypeStruct((B, GLA_H, GLA_DK, GLA_DV), F32)],
        scratch_shapes=[pltpu.VMEM((GLA_H * GLA_DK, GLA_H * GLA_DV), F32),
                        pltpu.VMEM((TT + PAD, LANE), F32),
                        pltpu.VMEM((TT + PAD, LANE), F32),
                        pltpu.VMEM((TT + PAD, 2 * LANE), F32),
                        pltpu.VMEM((TT, 2 * LANE), F32),
                        pltpu.VMEM((TT // c, GLA_H * GLA_DK, GLA_H * GLA_DV), F32),
                        pltpu.VMEM((TT // c, GLA_H * GLA_DK, GLA_H * GLA_DV), BF)],
        compiler_params=_cp(2),
        name="gla_prompt",
    )(gin, wg, bg, nw, L, E, EA, M)


def _rope(x, cos, sin_signed):
    lane = lax.broadcasted_iota(jnp.int32, (1, LANE), 1)
    first_half = (lane & (RET_DK - 1)) < RET_DK // 2
    out = []
    for p in range(2):
        xs = x[:, p * LANE:(p + 1) * LANE]
        up = pltpu.roll(xs, LANE - RET_DK // 2, 1)
        dn = pltpu.roll(xs, RET_DK // 2, 1)
        out.append(xs * cos + jnp.where(first_half, up, dn) * sin_signed)
    return jnp.concatenate(out, axis=1)


def _ret_front(x_ref, cos_ref, sin_ref, rows=slice(None)):
    cos, sin = cos_ref[rows, :], sin_ref[rows, :]
    q = _rope(x_ref[rows, 0:256], cos, sin)
    k = _rope(x_ref[rows, 256:512], cos, sin) * (RET_DK ** -0.5)
    v = x_ref[rows, 512:768]
    rg = x_ref[rows, 768:1024]
    return q, k, v, rg


def _ret_intra(q, k, v, D_ref):
    lane = lax.broadcasted_iota(jnp.int32, (1, RET_WIDTH), 1)
    kb = k.astype(BF)
    o = jnp.zeros(q.shape, F32)
    for h in range(RET_H):
        hm = (lane // RET_DK) == h
        s = _dot_nt(jnp.where(hm, q, 0.0).astype(BF), kb)
        p = (s * D_ref[h]).astype(BF)
        o = o + _dot(p, jnp.where(hm, v, 0.0).astype(BF))
    return o


def _ret_norm_gate(o, rg, nw_ref, EA_ref):
    mu = _dot_x3(o, EA_ref[...])
    d = o - mu
    var = _dot_x3(d * d, EA_ref[...])
    return d * lax.rsqrt(var + EPS) * nw_ref[...] * _silu(rg)


def _ret_prompt_kernel(x_ref, cos_ref, sin_ref, D_ref, rd_ref, kd_ref, G_ref, M_ref, EA_ref, nw_ref,
                       o_ref, sfin_ref, st_ref):
    t = pl.program_id(1)

    @pl.when(t == 0)
    def _():
        st_ref[...] = jnp.zeros_like(st_ref)

    C = D_ref.shape[1]
    S = st_ref[...]
    for i in range(x_ref.shape[0] // C):
        rows = slice(i * C, (i + 1) * C)
        q, k, v, rg = _ret_front(x_ref, cos_ref, sin_ref, rows)
        o = _ret_intra(q, k, v, D_ref)
        o = o + _dot((q * rd_ref[...]).astype(BF), S.astype(BF))
        u = _dot_tn((k * kd_ref[...]).astype(BF), v.astype(BF))
        S = S * G_ref[...] + u * M_ref[...]
        o_ref[rows, :] = _ret_norm_gate(o, rg, nw_ref, EA_ref).astype(o_ref.dtype)
    st_ref[...] = S

    @pl.when(t == pl.num_programs(1) - 1)
    def _():
        for h in range(RET_H):
            sfin_ref[0, h] = S[h * RET_DK:(h + 1) * RET_DK, h * RET_DV:(h + 1) * RET_DV]


def _rope_tables(pos):
    half = RET_DK // 2
    inv = ROPE_BASE ** (-jnp.arange(half, dtype=F32) / half)
    ang = pos.astype(F32)[:, None] * inv[None, :]
    cos, sin = jnp.cos(ang), jnp.sin(ang)
    return jnp.tile(jnp.concatenate([cos, cos], 1), (1, 2)), jnp.tile(jnp.concatenate([-sin, sin], 1), (1, 2))


def _ret_log_gamma():
    return np.log(1.0 - 2.0 ** (-5.0 - np.arange(RET_H, dtype=np.float64)))


def _ret_prompt_call(rin, B, T, norm_w):
    C = 256
    TT = 4 * C if T % (4 * C) == 0 else C
    nT = T // TT
    cos, sin = _rope_tables(jnp.arange(T, dtype=jnp.int32))
    lg = _ret_log_gamma()
    i = np.arange(C)
    dec = np.exp(lg[:, None, None] * (i[:, None] - i[None, :])[None]) * (i[:, None] >= i[None, :])[None]
    Dm = jnp.asarray(dec, F32)
    rd = jnp.asarray(np.repeat(np.exp(lg[None, :] * (i[:, None] + 1)), RET_DK, 1), F32)
    kd = jnp.asarray(np.repeat(np.exp(lg[None, :] * (C - 1 - i[:, None])), RET_DK, 1), F32)
    M = _head_block_mask(RET_DK, RET_DV, RET_H)
    G = jnp.asarray(M * np.repeat(np.exp(lg * C), RET_DK)[:, None], F32)
    M = jnp.asarray(M, F32)
    EA = jnp.asarray(_head_block_mask(RET_DV, RET_DV, RET_H) / RET_DV, BF)
    nw = norm_w.reshape(1, -1)
    return pl.pallas_call(
        _ret_prompt_kernel,
        grid=(B, nT),
        in_specs=[pl.BlockSpec((TT, RET_IN_W), lambda b, t: (b * nT + t, 0)),
                  pl.BlockSpec((TT, LANE), lambda b, t: (t, 0)),
                  pl.BlockSpec((TT, LANE), lambda b, t: (t, 0)),
                  _const(Dm.shape), _const(rd.shape), _const(kd.shape), _const(G.shape), _const(M.shape),
                  _const(EA.shape), _const(nw.shape)],
        out_specs=[pl.BlockSpec((TT, RET_WIDTH), lambda b, t: (b * nT + t, 0)),
                   pl.BlockSpec((1, RET_H, RET_DK, RET_DV), lambda b, t: (b, 0, 0, 0))],
        out_shape=[jax.ShapeDtypeStruct((B * T, RET_WIDTH), BF),
                   jax.ShapeDtypeStruct((B, RET_H, RET_DK, RET_DV), F32)],
        scratch_shapes=[pltpu.VMEM((RET_H * RET_DK, RET_H * RET_DV), F32)],
        compiler_params=_cp(2),
        name="ret_prompt",
    )(rin, cos, sin, Dm, rd, kd, G, M, EA, nw)


def _ssd_conv(xp_ref, cw_ref, cb_ref, TT):
    acc = cb_ref[...] + cw_ref[SSD_CONV_W - 1:SSD_CONV_W, :] * xp_ref[pl.ds(8, TT), :]
    for i in range(SSD_CONV_W - 1):
        acc = acc + cw_ref[i:i + 1, :] * xp_ref[pl.ds(8 - (SSD_CONV_W - 1) + i, TT), :]
    return acc


def _ssd_intra(xs, bm, cm, g, dt, Mk_ref):
    TT = xs.shape[0]
    rT = (g - jnp.log(dt)).T
    lane = lax.broadcasted_iota(jnp.int32, (1, LANE), 1)
    lane2 = lax.broadcasted_iota(jnp.int32, (1, 2 * LANE), 1)
    causal = Mk_ref[...] > 0.0
    bmb = bm.astype(BF)
    zero = jnp.zeros((), BF)
    o_parts = []
    for grp in range(SSD_G):
        cb = _dot_nt(jnp.where((lane // SSD_N) == grp, cm, 0.0).astype(BF), bmb).astype(BF)
        xg = xs[:, grp * 2 * LANE:(grp + 1) * 2 * LANE].astype(BF)
        og = jnp.zeros((TT, 2 * LANE), F32)
        for h4 in range(SSD_H // SSD_G):
            h = grp * (SSD_H // SSD_G) + h4
            dec = jnp.where(causal, jnp.exp(g[:, h:h + 1] - rT[h:h + 1, :]), 0.0)
            p = cb * dec.astype(BF)
            og = og + _dot(p, jnp.where((lane2 // SSD_P) == h4, xg, zero))
        o_parts.append(og)
    return jnp.concatenate(o_parts, axis=1)


def _ssd_prompt_kernel(x_ref, cw_ref, cb_ref, dtb_ref, alog_ref, dexp_ref, nw_ref, L_ref, Mk_ref, Eexp_ref, M2_ref,
                       res_ref, g1_ref, og_ref, or_ref, wo_ref, lg_ref, lb_ref,
                       o_ref, sfin_ref, cfin_ref, st_ref, xp_ref):
    TT = x_ref.shape[0]
    t = pl.program_id(1)

    @pl.when(t == 0)
    def _():
        st_ref[...] = jnp.zeros_like(st_ref)
        xp_ref[0:8, :] = jnp.zeros((8, SSD_CONV_DIM), F32)

    z = x_ref[:, 0:SSD_WIDTH]
    xp_ref[8:8 + TT, :] = x_ref[:, SSD_WIDTH:SSD_WIDTH + SSD_CONV_DIM]
    sdt = x_ref[:, SSD_WIDTH + SSD_CONV_DIM:SSD_IN_W]
    xbc = _silu(_ssd_conv(xp_ref, cw_ref, cb_ref, TT))
    tail = xp_ref[TT:TT + 8, :]
    xp_ref[0:8, :] = tail
    dt_all = _softplus(sdt + dtb_ref[...])
    la_all = dt_all * (-jnp.exp(alog_ref[...]))
    Eexp = Eexp_ref[...]
    C = L_ref.shape[0]
    S = st_ref[...]
    for i in range(TT // C):
        rows = slice(i * C, (i + 1) * C)
        xs = xbc[rows, 0:SSD_WIDTH]
        bm = xbc[rows, SSD_WIDTH:SSD_WIDTH + LANE]
        cm = xbc[rows, SSD_WIDTH + LANE:SSD_CONV_DIM]
        dt = dt_all[rows, :]
        g = _dot_3x(L_ref[...], la_all[rows, :])
        gl = g[C - 1:C, :]
        eg_x = _dot_x2(jnp.exp(g), Eexp)
        cw_x = _dot_x2(dt * jnp.exp(gl - g), Eexp)
        egl_x = _dot_x2(jnp.exp(gl), Eexp)

        o = _ssd_intra(xs, bm, cm, g, dt, Mk_ref)
        o = o + eg_x * _dot(cm.astype(BF), S.astype(BF))
        u = _dot_tn(bm.astype(BF), (xs * cw_x).astype(BF))
        S = S * egl_x + u * M2_ref[...]

        y = (o + dexp_ref[...] * xs) * _silu(z[rows, :])
        ms = jnp.mean(y * y, axis=-1, keepdims=True)
        o_ssd = (y * lax.rsqrt(ms + EPS) * nw_ref[...]).astype(BF)
        merged = jnp.concatenate([og_ref[rows, :], or_ref[rows, :], o_ssd], axis=1)
        mix = _dot(merged, wo_ref[...])
        o_ref[0, rows, :] = _layer_norm(ALPHA * res_ref[0, rows, :] + g1_ref[0] * mix, lg_ref[0], lb_ref[0])
    st_ref[...] = S

    @pl.when(t == pl.num_programs(1) - 1)
    def _():
        for h in range(SSD_H):
            gi = h // (SSD_H // SSD_G)
            sfin_ref[0, h] = S[gi * SSD_N:(gi + 1) * SSD_N, h * SSD_P:(h + 1) * SSD_P]
        cfin_ref[0] = tail[8 - (SSD_CONV_W - 1):8, :]


def _pad_lanes(v, n=LANE):
    v = v.reshape(1, -1)
    return jnp.zeros((1, n), F32).at[:, :v.shape[1]].set(v)


def _ssd_tables(TT, c):
    L = jnp.asarray(_block_tril(TT, c), BF)
    Mk = jnp.asarray(_block_tril(TT, c), F32)
    e = np.zeros((LANE, SSD_WIDTH), np.float32)
    for h in range(SSD_H):
        e[h, h * SSD_P:(h + 1) * SSD_P] = 1.0
    M2 = np.zeros((SSD_G * SSD_N, SSD_WIDTH), np.float32)
    for h in range(SSD_H):
        gi = h // (SSD_H // SSD_G)
        M2[gi * SSD_N:(gi + 1) * SSD_N, h * SSD_P:(h + 1) * SSD_P] = 1.0
    return L, Mk, jnp.asarray(e, BF), jnp.asarray(M2, F32)


def _ssd_params(conv_w, conv_b, dt_bias, a_log, d, norm_w):
    return (conv_w, conv_b.reshape(1, -1), _pad_lanes(dt_bias), _pad_lanes(a_log),
            jnp.repeat(d, SSD_P).reshape(1, -1), norm_w.reshape(1, -1))


def _ssd_prompt_call(sin_, x3, g1, og, orr, w_out, ln_g, ln_b, conv_w, conv_b, dt_bias, a_log, d, norm_w):
    B, T, D = x3.shape
    C = 256
    TT = 4 * C if T % (4 * C) == 0 else (2 * C if T % (2 * C) == 0 else C)
    nT = T // TT
    L, Mk, Eexp, M2 = _ssd_tables(C, C)
    prm = _ssd_params(conv_w, conv_b, dt_bias, a_log, d, norm_w)
    rmap = lambda b, t: (b * nT + t, 0)
    return pl.pallas_call(
        _ssd_prompt_kernel,
        grid=(B, nT),
        in_specs=[pl.BlockSpec((TT, SSD_IN_W), rmap)]
                 + [_const(p.shape) for p in prm]
                 + [_const(L.shape), _const(Mk.shape), _const(Eexp.shape), _const(M2.shape)]
                 + [pl.BlockSpec((1, TT, D), lambda b, t: (b, t, 0)),
                    pl.BlockSpec((1, 1, D), lambda b, t: (b, 0, 0)),
                    pl.BlockSpec((TT, GLA_WIDTH), rmap), pl.BlockSpec((TT, RET_WIDTH), rmap),
                    _const((D, D)), _const((1, 1, D)), _const((1, 1, D))],
        out_specs=[pl.BlockSpec((1, TT, D), lambda b, t: (b, t, 0)),
                   pl.BlockSpec((1, SSD_H, SSD_N, SSD_P), lambda b, t: (b, 0, 0, 0)),
                   pl.BlockSpec((1, SSD_CONV_W - 1, SSD_CONV_DIM), lambda b, t: (b, 0, 0))],
        out_shape=[jax.ShapeDtypeStruct((B, T, D), F32),
                   jax.ShapeDtypeStruct((B, SSD_H, SSD_N, SSD_P), F32),
                   jax.ShapeDtypeStruct((B, SSD_CONV_W - 1, SSD_CONV_DIM), F32)],
        scratch_shapes=[pltpu.VMEM((SSD_G * SSD_N, SSD_WIDTH), F32),
                        pltpu.VMEM((TT + 8, SSD_CONV_DIM), F32)],
        compiler_params=_cp(2),
        name="ssd_outproj_ln",
    )(sin_, *prm, L, Mk, Eexp, M2, x3, g1, og, orr, w_out, ln_g.reshape(1, 1, D), ln_b.reshape(1, 1, D))


def _inproj_t_kernel(x_ref, sc_ref, sh_ref, wt_ref, og_ref, or_ref, os_ref, w_ref):
    nt, nb, D = x_ref.shape

    @pl.when(pl.program_id(0) == 0)
    def _():
        for src, dst, n in ((0, 0, N_GA + GLA_GATE_RANK), (N_GA + GLA_GATE_RANK, N_GA + LANE, N_DT - N_GA - GLA_GATE_RANK)):
            for r in range(0, n, 512):
                m = min(512, n - r)
                w_ref[dst + r:dst + r + m, :] = wt_ref[src + r:src + r + m, :].astype(BF)
        w_ref[N_GA + GLA_GATE_RANK:N_GA + LANE, :] = jnp.zeros((LANE - GLA_GATE_RANK, D), BF)
        tail = jnp.concatenate([wt_ref[N_DT:N_IN, :], jnp.zeros((LANE - SSD_H, D), F32)], axis=0)
        w_ref[IN_W - LANE:IN_W, :] = tail.astype(BF)

    h = x_ref[...] * (1.0 + sc_ref[...]) + sh_ref[...]
    for t in range(0, nt, 2):
        ht = h[t:t + 2].reshape(2 * nb, D).astype(BF)
        cols = slice(t * nb, (t + 2) * nb)
        og_ref[:, cols] = _dot_nt(w_ref[0:GLA_IN_W, :], ht)
        or_ref[:, cols] = _dot_nt(w_ref[GLA_IN_W:GLA_IN_W + RET_IN_W, :], ht)
        os_ref[:, cols] = _dot_nt(w_ref[GLA_IN_W + RET_IN_W:IN_W, :], ht)


def _inproj_t_call(xt, sc, sh, wt, l):
    T, B, D = xt.shape
    nt = 4
    cmap = lambda i: (0, i)
    return pl.pallas_call(
        _inproj_t_kernel,
        grid=(T // nt,),
        in_specs=[pl.BlockSpec((nt, B, D), lambda i: (i, 0, 0)),
                  pl.BlockSpec((1, B, D), lambda i: (0, 0, 0)),
                  pl.BlockSpec((1, B, D), lambda i: (0, 0, 0)),
                  _resident_layer(wt.shape, l)],
        out_specs=[pl.BlockSpec((GLA_IN_W, nt * B), cmap),
                   pl.BlockSpec((RET_IN_W, nt * B), cmap),
                   pl.BlockSpec((SSD_IN_W, nt * B), cmap)],
        out_shape=[jax.ShapeDtypeStruct((GLA_IN_W, T * B), F32),
                   jax.ShapeDtypeStruct((RET_IN_W, T * B), F32),
                   jax.ShapeDtypeStruct((SSD_IN_W, T * B), F32)],
        scratch_shapes=[pltpu.VMEM((IN_W, D), BF)],
        compiler_params=_cp(1),
        name="in_proj_t",
    )(xt, sc, sh, wt)


def _row_sum(x):
    return jnp.sum(x, axis=0, keepdims=True)


def _lane_state_readout(o, coef_ref, s0_ref, n_rows):
    nb = LANE
    half = len(o) // 2
    for part in range(2):
        def body(k8, accs, part=part):
            accs = list(accs)
            base = pl.multiple_of(k8 * 8, 8)
            grp = [coef_ref[pl.ds(base, 8), (part * half + i) * nb:(part * half + i + 1) * nb] for i in range(half)]
            for j in range(8):
                s0k = s0_ref[0, k8 * 8 + j]
                for i in range(half):
                    accs[i] = accs[i] + grp[i][j:j + 1, :] * s0k
            return tuple(accs)

        res = lax.fori_loop(0, n_rows // 8, body, tuple(o[part * half:(part + 1) * half]))
        o[part * half:(part + 1) * half] = list(res)
    return o


def _lane_state_update(sn_ref, s0_ref, decay_fn, coef_ref, val_fn, n_rows, T):
    nb = LANE

    def body(k8, carry):
        base = pl.multiple_of(k8 * 8, 8)
        grp = [coef_ref[pl.ds(base, 8), t * nb:(t + 1) * nb] for t in range(T)]
        dec = decay_fn(base)
        for j in range(8):
            dj = dec[j:j + 1, :] if dec.shape[0] == 8 else dec
            sk = dj * s0_ref[0, k8 * 8 + j]
            for t in range(T):
                sk = sk + grp[t][j:j + 1, :] * val_fn(t)
            sn_ref[0, 0, k8 * 8 + j] = sk
        return carry

    lax.fori_loop(0, n_rows // 8, body, 0)


def _state_specs(shape, l, first):
    assert l == 0 or not first
    tail = tuple(shape[2:])
    in_spec = pl.BlockSpec((None, 1) + tail, lambda h: (l, h, 0, 0, 0))
    out_spec = pl.BlockSpec(((shape[0] if first else 1), 1) + tail, lambda h: (l, h, 0, 0, 0))
    return in_spec, out_spec


def _zero_later_layers(ref):
    ref[1:] = jnp.zeros((ref.shape[0] - 1,) + tuple(ref.shape[1:]), ref.dtype)


def _finish_state_call(kern, n_in, first, prevs):
    if first:
        return functools.partial(kern, first=True), [], {}
    wrapped = lambda *a, **kw: kern(*a[:n_in], *a[n_in + len(prevs):], first=False, **kw)
    specs = [pl.BlockSpec(memory_space=pl.ANY)] * len(prevs)
    return wrapped, specs, {n_in + i: 1 + i for i in range(len(prevs))}


def _gla_t_kernel(x_ref, s0_ref, wg_ref, bg_ref, nw_ref, o_ref, sn_ref, qe_ref, ke_ref, a_ref, *, T, first):
    nb = LANE
    if first:
        _zero_later_layers(sn_ref)
    h = pl.program_id(0)
    r0 = pl.multiple_of(h * GLA_DK, GLA_DK)
    v0 = pl.multiple_of(h * GLA_DV, GLA_DV)
    q = x_ref[pl.ds(r0, GLA_DK), :] * (GLA_DK ** -0.5)
    k = x_ref[pl.ds(128 + r0, GLA_DK), :]
    gate = _dot(wg_ref[pl.ds(r0, GLA_DK), :], x_ref[512:640, :].astype(BF)) + bg_ref[pl.ds(r0, GLA_DK), :]
    la = _log_sigmoid(gate) * (1.0 / GLA_GATE_TEMP)
    gs = []
    acc = jnp.zeros((GLA_DK, nb), F32)
    for t in range(T):
        acc = acc + la[:, t * nb:(t + 1) * nb]
        gs.append(acc)
    gl = gs[T - 1]
    a_ref[...] = jnp.exp(gl)
    qs = [q[:, t * nb:(t + 1) * nb] for t in range(T)]
    ks = [k[:, t * nb:(t + 1) * nb] for t in range(T)]
    for t in range(T):
        qe_ref[:, t * nb:(t + 1) * nb] = qs[t] * jnp.exp(gs[t])
        ke_ref[:, t * nb:(t + 1) * nb] = ks[t] * jnp.exp(gl - gs[t])

    def vt(t):
        return x_ref[pl.ds(256 + v0, GLA_DV), t * nb:(t + 1) * nb]

    o = []
    for t in range(T):
        ot = jnp.zeros((GLA_DV, nb), F32)
        for u in range(t + 1):
            s = _row_sum(qs[t] * ks[u] * jnp.exp(gs[t] - gs[u]))
            ot = ot + s * vt(u)
        o.append(ot)

    o = _lane_state_readout(o, qe_ref, s0_ref, GLA_DK)
    _lane_state_update(sn_ref, s0_ref, lambda base: a_ref[pl.ds(base, 8), :], ke_ref, vt, GLA_DK, T)

    nw = nw_ref[pl.ds(v0, GLA_DV), :]
    for t in range(T):
        ms = jnp.mean(o[t] * o[t], axis=0, keepdims=True)
        r = x_ref[pl.ds(640 + v0, GLA_DV), t * nb:(t + 1) * nb]
        o_ref[:, t * nb:(t + 1) * nb] = (o[t] * lax.rsqrt(ms + EPS) * nw * _silu(r)).astype(o_ref.dtype)


def _gla_t_call(gT, s0, prev, l, T, w_gate, b_gate, norm_w):
    N = gT.shape[1]
    wg = jnp.zeros((GLA_H * GLA_DK, LANE), F32).at[:, :GLA_GATE_RANK].set(w_gate.T).astype(BF)
    bg = b_gate.reshape(-1, 1)
    nw = norm_w.reshape(-1, 1)
    first = prev is None
    prevs = [] if first else [prev]
    s_in, s_out = _state_specs(s0.shape, l, first)
    ins = [gT, s0, wg, bg, nw]
    specs = [_const(gT.shape), s_in, _const(wg.shape), _const(bg.shape), _const(nw.shape)]
    kern, pspecs, aliases = _finish_state_call(functools.partial(_gla_t_kernel, T=T), len(ins), first, prevs)
    ins, specs = ins + prevs, specs + pspecs
    return pl.pallas_call(
        kern,
        grid=(GLA_H,),
        in_specs=specs,
        out_specs=[pl.BlockSpec((GLA_DV, N), lambda h: (h, 0)), s_out],
        out_shape=[jax.ShapeDtypeStruct((GLA_WIDTH, N), BF), jax.ShapeDtypeStruct(s0.shape, F32)],
        scratch_shapes=[pltpu.VMEM((GLA_DK, N), F32), pltpu.VMEM((GLA_DK, N), F32), pltpu.VMEM((GLA_DK, LANE), F32)],
        input_output_aliases=aliases,
        compiler_params=_cp(1),
        name="gla_t",
    )(*ins)


def _ret_t_kernel(x_ref, s0_ref, cos_ref, sin_ref, pw_ref, nw_ref, o_ref, sn_ref, qd_ref, kd_ref, *, T, first):
    nb = LANE
    if first:
        _zero_later_layers(sn_ref)
    h = pl.program_id(0)
    r0 = pl.multiple_of(h * RET_DK, RET_DK)
    half_k = RET_DK // 2
    cos, sin = cos_ref[...], sin_ref[...]

    def rope_t(base):
        x1 = x_ref[pl.ds(base + r0, half_k), :]
        x2 = x_ref[pl.ds(base + r0 + half_k, half_k), :]
        return jnp.concatenate([x1 * cos - x2 * sin, x1 * sin + x2 * cos], axis=0)

    q = rope_t(0)
    k = rope_t(256) * (RET_DK ** -0.5)
    pw = pw_ref[h]
    qs = [q[:, t * nb:(t + 1) * nb] for t in range(T)]
    ks = [k[:, t * nb:(t + 1) * nb] for t in range(T)]
    for t in range(T):
        qd_ref[:, t * nb:(t + 1) * nb] = qs[t] * pw[t + 1:t + 2, :]
        kd_ref[:, t * nb:(t + 1) * nb] = ks[t] * pw[T - 1 - t:T - t, :]

    def vt(t):
        return x_ref[pl.ds(512 + r0, RET_DV), t * nb:(t + 1) * nb]

    o = []
    for t in range(T):
        ot = jnp.zeros((RET_DV, nb), F32)
        for u in range(t + 1):
            s = _row_sum(qs[t] * ks[u]) * pw[t - u:t - u + 1, :]
            ot = ot + s * vt(u)
        o.append(ot)

    o = _lane_state_readout(o, qd_ref, s0_ref, RET_DK)
    _lane_state_update(sn_ref, s0_ref, lambda base: pw[T:T + 1, :], kd_ref, vt, RET_DK, T)

    nw = nw_ref[pl.ds(r0, RET_DV), :]
    for t in range(T):
        mu = jnp.mean(o[t], axis=0, keepdims=True)
        d = o[t] - mu
        var = jnp.mean(d * d, axis=0, keepdims=True)
        rg = x_ref[pl.ds(768 + r0, RET_DV), t * nb:(t + 1) * nb]
        o_ref[:, t * nb:(t + 1) * nb] = (d * lax.rsqrt(var + EPS) * nw * _silu(rg)).astype(o_ref.dtype)


def _ret_t_call(rT, s0, prev, l, T, norm_w):
    N = rT.shape[1]
    B = N // T
    half = RET_DK // 2
    inv = ROPE_BASE ** (-jnp.arange(half, dtype=F32) / half)
    ang = inv[:, None] * (PAST_LEN + jnp.arange(T, dtype=jnp.int32)).astype(F32)[None, :]
    cos = jnp.repeat(jnp.cos(ang), B, axis=1)
    sin = jnp.repeat(jnp.sin(ang), B, axis=1)
    lg = _ret_log_gamma()
    pw = jnp.asarray(np.repeat(np.exp(lg[:, None] * np.arange(16)[None, :])[:, :, None], LANE, axis=2), F32)
    nw = norm_w.reshape(-1, 1)
    first = prev is None
    prevs = [] if first else [prev]
    s_in, s_out = _state_specs(s0.shape, l, first)
    ins = [rT, s0, cos, sin, pw, nw]
    specs = [_const(rT.shape), s_in, _const(cos.shape), _const(sin.shape), _const(pw.shape), _const(nw.shape)]
    kern, pspecs, aliases = _finish_state_call(functools.partial(_ret_t_kernel, T=T), len(ins), first, prevs)
    ins, specs = ins + prevs, specs + pspecs
    return pl.pallas_call(
        kern,
        grid=(RET_H,),
        in_specs=specs,
        out_specs=[pl.BlockSpec((RET_DV, N), lambda h: (h, 0)), s_out],
        out_shape=[jax.ShapeDtypeStruct((RET_WIDTH, N), BF), jax.ShapeDtypeStruct(s0.shape, F32)],
        scratch_shapes=[pltpu.VMEM((RET_DK, N), F32), pltpu.VMEM((RET_DK, N), F32)],
        input_output_aliases=aliases,
        compiler_params=_cp(1),
        name="ret_t",
    )(*ins)


def _ssd_t_kernel(x_ref, c0_ref, s0_ref, cw_ref, cb_ref, dtb_ref, alog_ref, d_ref, nw_ref,
                  o_ref, sn_ref, cn_ref, hist_ref, y_ref, ssq_ref, cm_ref, bw_ref, xw_ref, *, T, first):
    nb = LANE
    W1 = SSD_CONV_W - 1
    h = pl.program_id(0)
    XB = SSD_WIDTH
    if first:
        _zero_later_layers(sn_ref)

    @pl.when(h == 0)
    def _():
        ssq_ref[...] = jnp.zeros_like(ssq_ref)
        if first:
            _zero_later_layers(cn_ref)
        for i in range(W1):
            for j in range(SSD_CONV_DIM // LANE):
                hist_ref[j * LANE:(j + 1) * LANE, i * nb:(i + 1) * nb] = c0_ref[0, i][:, j * LANE:(j + 1) * LANE].T
                cn_ref[0, i, :, j * LANE:(j + 1) * LANE] = \
                    x_ref[XB + j * LANE:XB + (j + 1) * LANE, (T - W1 + i) * nb:(T - W1 + i + 1) * nb].T

    def conv_rows(ro):
        w = cw_ref[pl.ds(ro, 64), :]
        b = cb_ref[pl.ds(ro, 64), :]
        xx = [hist_ref[pl.ds(ro, 64), i * nb:(i + 1) * nb] for i in range(W1)]
        xx += [x_ref[pl.ds(XB + ro, 64), t * nb:(t + 1) * nb] for t in range(T)]
        out = []
        for t in range(T):
            acc = b + w[:, 0:1] * xx[t]
            for i in range(1, SSD_CONV_W):
                acc = acc + w[:, i:i + 1] * xx[t + i]
            out.append(_silu(acc))
        return out

    grp = h // (SSD_H // SSD_G)
    xs = conv_rows(pl.multiple_of(h * SSD_P, SSD_P))
    bm = conv_rows(pl.multiple_of(SSD_WIDTH + grp * SSD_N, SSD_N))
    cm = conv_rows(pl.multiple_of(SSD_WIDTH + SSD_G * SSD_N + grp * SSD_N, SSD_N))

    dt_all = _softplus(x_ref[pl.ds(XB + SSD_CONV_DIM + h, 1), :] + dtb_ref[pl.ds(h, 1), :])
    a = -jnp.exp(alog_ref[pl.ds(h, 1), :])
    dts = [dt_all[:, t * nb:(t + 1) * nb] for t in range(T)]
    gs = []
    acc = jnp.zeros((1, nb), F32)
    for t in range(T):
        acc = acc + dts[t] * a
        gs.append(acc)
    gl = gs[T - 1]

    o = []
    for t in range(T):
        ot = jnp.zeros((SSD_P, nb), F32)
        for u in range(t + 1):
            s = _row_sum(cm[t] * bm[u]) * (jnp.exp(gs[t] - gs[u]) * dts[u])
            ot = ot + s * xs[u]
        o.append(ot)

    for t in range(T):
        cm_ref[:, t * nb:(t + 1) * nb] = cm[t] * jnp.exp(gs[t])
        bw_ref[:, t * nb:(t + 1) * nb] = bm[t]
        xw_ref[:, t * nb:(t + 1) * nb] = xs[t] * (dts[t] * jnp.exp(gl - gs[t]))

    o = _lane_state_readout(o, cm_ref, s0_ref, SSD_N)
    egl = jnp.exp(gl)
    _lane_state_update(sn_ref, s0_ref, lambda base: egl, bw_ref, lambda t: xw_ref[:, t * nb:(t + 1) * nb], SSD_N, T)

    dd = d_ref[pl.ds(h, 1), :]
    p0 = pl.multiple_of(h * SSD_P, SSD_P)
    for t in range(T):
        z = x_ref[pl.ds(p0, SSD_P), t * nb:(t + 1) * nb]
        y = (o[t] + dd * xs[t]) * _silu(z)
        y_ref[pl.ds(p0, SSD_P), t * nb:(t + 1) * nb] = y
        ssq_ref[:, t * nb:(t + 1) * nb] += _row_sum(y * y)

    @pl.when(h == SSD_H - 1)
    def _():
        scale = lax.rsqrt(ssq_ref[...] * (1.0 / SSD_WIDTH) + EPS)
        o_ref[...] = (y_ref[...] * scale * nw_ref[...]).astype(o_ref.dtype)


def _ssd_t_call(sT, c0, s0, prev_s, prev_c, l, T, conv_w, conv_b, dt_bias, a_log, d, norm_w):
    N = sT.shape[1]
    col = lambda v: jnp.zeros((LANE, 1), F32).at[:SSD_H, 0].set(v)
    prm = (conv_w.T, conv_b.reshape(-1, 1), col(dt_bias), col(a_log), col(d), norm_w.reshape(-1, 1))
    first = prev_s is None
    prevs = [] if first else [prev_s, prev_c]
    s_in, s_out = _state_specs(s0.shape, l, first)
    c_in = pl.BlockSpec((1,) + tuple(c0.shape[1:]), lambda h: (l, 0, 0, 0))
    c_out = pl.BlockSpec(((c0.shape[0] if first else 1),) + tuple(c0.shape[1:]), lambda h: (l, 0, 0, 0))
    ins = [sT, c0, s0, *prm]
    specs = [_const(sT.shape), c_in, s_in] + [_const(p.shape) for p in prm]
    kern, pspecs, aliases = _finish_state_call(functools.partial(_ssd_t_kernel, T=T), len(ins), first, prevs)
    ins, specs = ins + prevs, specs + pspecs
    return pl.pallas_call(
        kern,
        grid=(SSD_H,),
        in_specs=specs,
        out_specs=[_const((SSD_WIDTH, N)), s_out, c_out],
        out_shape=[jax.ShapeDtypeStruct((SSD_WIDTH, N), BF), jax.ShapeDtypeStruct(s0.shape, F32),
                   jax.ShapeDtypeStruct(c0.shape, F32)],
        scratch_shapes=[pltpu.VMEM((SSD_CONV_DIM, (SSD_CONV_W - 1) * LANE), F32),
                        pltpu.VMEM((SSD_WIDTH, N), F32), pltpu.VMEM((1, N), F32),
                        pltpu.VMEM((SSD_N, N), F32), pltpu.VMEM((SSD_N, N), F32), pltpu.VMEM((SSD_P, N), F32)],
        input_output_aliases=aliases,
        compiler_params=_cp(1),
        name="ssd_t",
    )(*ins)


def _outproj_t_kernel(x_ref, g_ref, og_ref, or_ref, os_ref, w_ref, lg_ref, lb_ref, o_ref):
    nt, nb, D = x_ref.shape
    for t in range(nt):
        cols = slice(t * nb, (t + 1) * nb)
        mix = (_dot_tn(og_ref[:, cols], w_ref[0:GLA_WIDTH, :])
               + _dot_tn(or_ref[:, cols], w_ref[GLA_WIDTH:GLA_WIDTH + RET_WIDTH, :])
               + _dot_tn(os_ref[:, cols], w_ref[GLA_WIDTH + RET_WIDTH:D, :]))
        y = ALPHA * x_ref[t] + g_ref[0] * mix
        o_ref[t] = _layer_norm(y, lg_ref[0], lb_ref[0])


def _outproj_t_call(xt, g1, ogT, orT, osT, w_out, ln_g, ln_b):
    T, B, D = xt.shape
    nt = 4
    cmap = lambda i: (0, i)
    return pl.pallas_call(
        _outproj_t_kernel,
        grid=(T // nt,),
        in_specs=[pl.BlockSpec((nt, B, D), lambda i: (i, 0, 0)),
                  pl.BlockSpec((1, B, D), lambda i: (0, 0, 0)),
                  pl.BlockSpec((GLA_WIDTH, nt * B), cmap),
                  pl.BlockSpec((RET_WIDTH, nt * B), cmap),
                  pl.BlockSpec((SSD_WIDTH, nt * B), cmap),
                  _const((D, D)), _const((1, 1, D)), _const((1, 1, D))],
        out_specs=pl.BlockSpec((nt, B, D), lambda i: (i, 0, 0)),
        out_shape=jax.ShapeDtypeStruct((T, B, D), F32),
        compiler_params=_cp(1),
        name="out_proj_ln_t",
    )(xt, g1, ogT, orT, osT, w_out, ln_g.reshape(1, 1, D), ln_b.reshape(1, 1, D))


ROUTE_OFF = 8


def _moe_route_t(lt):
    neg = jnp.float32(-jnp.inf)
    row8 = lax.broadcasted_iota(jnp.int32, (8, 1), 0)
    lg = jnp.where(row8 < MOE_GROUPS, lt[0:8, :], neg)
    mg = jnp.max(lg, axis=0, keepdims=True)
    gsel = jnp.min(jnp.where(lg == mg, row8, 8), axis=0, keepdims=True)
    g_gate = 1.0 / jnp.sum(jnp.exp(lg - mg), axis=0, keepdims=True)
    rowe = lax.broadcasted_iota(jnp.int32, (MOE_EXPERTS, 1), 0)
    le = jnp.where((rowe // MOE_PER_GROUP) == gsel, lt[ROUTE_OFF:ROUTE_OFF + MOE_EXPERTS, :], neg)
    m1 = jnp.max(le, axis=0, keepdims=True)
    i1 = jnp.min(jnp.where(le == m1, rowe, MOE_EXPERTS), axis=0, keepdims=True)
    le2 = jnp.where(rowe == i1, neg, le)
    m2 = jnp.max(le2, axis=0, keepdims=True)
    i2 = jnp.min(jnp.where(le2 == m2, rowe, MOE_EXPERTS), axis=0, keepdims=True)
    e2 = jnp.exp(m2 - m1)
    w1 = g_gate / (1.0 + e2)
    w2 = g_gate * e2 / (1.0 + e2)
    comb = jnp.where(rowe == i1, w1, jnp.where(rowe == i2, w2, 0.0))
    cg = comb[0:4, :]
    for g in range(1, MOE_GROUPS):
        cg = cg + comb[g * MOE_PER_GROUP:(g + 1) * MOE_PER_GROUP, :]
    return gsel, cg, comb


MOE_SUB = 256
MOE_BLK = 16
MOE_ROWS = 256
MOE_NPS = MOE_SUB + MOE_GROUPS * MOE_BLK
assert MOE_SUB <= MOE_ROWS


def _moe_kernel(x_ref, sc_ref, sh_ref, g_ref, wr_ref, br_ref, us_ref, w1_ref, w3_ref, w2_ref, lg_ref, lb_ref,
                o_ref, hb_ref, cwb_ref, hp_ref, cwp_ref, yp_ref, pos_ref,
                cgrp_ref, fill_ref, cur_ref, na_ref, nb_ref, so_ref, nfa_ref, dsa_ref, dsb_ref, *, n_steps):
    bB, bT, D = x_ref.shape
    R = bB * bT
    n_q = R // MOE_SUB
    s = pl.program_id(1)
    x = x_ref[...]
    row8 = lax.broadcasted_iota(jnp.int32, (8, 1), 0)
    slot = lax.broadcasted_iota(jnp.int32, (MOE_NPS, 1), 0).astype(F32)

    @pl.when((pl.program_id(0) == 0) & (s == 0))
    def _():
        hb_ref[...] = jnp.zeros_like(hb_ref)
        cwb_ref[...] = jnp.zeros_like(cwb_ref)
        yp_ref[...] = jnp.zeros_like(yp_ref)

    @pl.when(s == 0)
    def _():
        na_ref[0] = 0
        for g in range(MOE_GROUPS):
            cur_ref[g] = -1
            fill_ref[g] = 0

    @pl.when(s < n_steps)
    def _():
        h = (x * (1.0 + sc_ref[...]) + sh_ref[...]).reshape(R, D)
        segs, offs = [], []
        for q in range(n_q):
            u = s * n_q + q
            hq = h[q * MOE_SUB:(q + 1) * MOE_SUB, :].astype(BF)
            gsel, cg, _ = _moe_route_t(_dot_nt(wr_ref[...], hq) + br_ref[...])
            onehot = jnp.where(row8 == gsel, 1.0, 0.0)
            rank = _dot(onehot.astype(BF), us_ref[...])
            cnt = jnp.sum(onehot, axis=1, keepdims=True)
            seg = jnp.ceil(cnt * (1.0 / MOE_BLK)) * MOE_BLK
            off = jnp.zeros((8, 1), F32)
            for g in range(1, MOE_GROUPS):
                off = off + jnp.where(row8 >= g, seg[g - 1:g, :], 0.0)
            pos = jnp.sum(onehot * (off + rank), axis=0, keepdims=True)
            pos_ref[u] = jnp.broadcast_to(pos, (8, MOE_SUB))
            perm = jnp.where(slot == pos, 1.0, 0.0).astype(BF)
            hp_ref[q] = _dot(perm, hq).astype(BF)
            cg8 = jnp.concatenate([cg, jnp.zeros((4, MOE_SUB), F32)], axis=0)
            cg_hi = cg8.astype(BF)
            cg_lo = (cg8 - cg_hi.astype(F32)).astype(BF)
            cwp_ref[q] = _dot_nt(perm, cg_hi) + _dot_nt(perm, cg_lo)
            segs.append(seg)
            offs.append(off)
        for q in range(n_q):
            u = s * n_q + q
            for g in range(MOE_GROUPS):
                so = offs[q][g, 0].astype(jnp.int32)
                nb = (segs[q][g, 0] * (1.0 / MOE_BLK)).astype(jnp.int32)
                f = fill_ref[g]
                c = cur_ref[g]
                na = na_ref[0]
                room = jnp.where(c < 0, 0, (MOE_ROWS - f) // MOE_BLK)
                n_a = jnp.minimum(nb, room)
                n_b = nb - n_a
                base_a = c * MOE_ROWS + f
                base_b = na * MOE_ROWS
                idx = u * MOE_GROUPS + g
                so_ref[idx] = so
                nb_ref[idx] = nb
                nfa_ref[idx] = n_a
                dsa_ref[idx] = base_a
                dsb_ref[idx] = base_b

                def put(k, carry, so=so, q=q, n_a=n_a, base_a=base_a, base_b=base_b):
                    dst = pl.multiple_of(jnp.where(k < n_a, base_a + k * MOE_BLK, base_b + (k - n_a) * MOE_BLK), MOE_BLK)
                    src = pl.multiple_of(so + k * MOE_BLK, MOE_BLK)
                    hb_ref[pl.ds(dst, MOE_BLK), :] = hp_ref[q, pl.ds(src, MOE_BLK), :]
                    cwb_ref[pl.ds(dst, MOE_BLK), :] = cwp_ref[q, pl.ds(src, MOE_BLK), :]
                    return carry

                lax.fori_loop(0, nb, put, 0)

                @pl.when(n_b > 0)
                def _(g=g, na=na, n_b=n_b):
                    cgrp_ref[na] = g
                    na_ref[0] = na + 1
                    cur_ref[g] = na
                    fill_ref[g] = n_b * MOE_BLK

                @pl.when(n_b == 0)
                def _(g=g, f=f, n_a=n_a):
                    fill_ref[g] = f + n_a * MOE_BLK

    @pl.when(s == n_steps - 1)
    def _():
        def chunk(c, carry):
            g = cgrp_ref[c]
            start = pl.multiple_of(c * MOE_ROWS, MOE_ROWS)
            hc = hb_ref[pl.ds(start, MOE_ROWS), :]
            cw = cwb_ref[pl.ds(start, MOE_ROWS), :]
            hids = []
            for j in range(MOE_PER_GROUP):
                e = g * MOE_PER_GROUP + j
                hid = _silu(_dot(hc, w1_ref[e])) * _dot(hc, w3_ref[e]) * cw[:, j:j + 1]
                hids.append(hid.astype(BF))
            w2g = w2_ref[pl.ds(g * MOE_PER_GROUP, MOE_PER_GROUP)].reshape(MOE_PER_GROUP * MOE_FF, D)
            hb_ref[pl.ds(start, MOE_ROWS), :] = _dot(jnp.concatenate(hids, axis=1), w2g).astype(BF)
            return carry

        lax.fori_loop(0, na_ref[0], chunk, 0)

    @pl.when(s >= n_steps)
    def _():
        for q in range(n_q):
            u = (s - n_steps) * n_q + q
            for g in range(MOE_GROUPS):
                idx = u * MOE_GROUPS + g
                so, n_a, base_a, base_b = so_ref[idx], nfa_ref[idx], dsa_ref[idx], dsb_ref[idx]

                def take(k, carry, so=so, q=q, n_a=n_a, base_a=base_a, base_b=base_b):
                    src = pl.multiple_of(jnp.where(k < n_a, base_a + k * MOE_BLK, base_b + (k - n_a) * MOE_BLK), MOE_BLK)
                    dst = pl.multiple_of(so + k * MOE_BLK, MOE_BLK)
                    yp_ref[q, pl.ds(dst, MOE_BLK), :] = hb_ref[pl.ds(src, MOE_BLK), :]
                    return carry

                lax.fori_loop(0, nb_ref[idx], take, 0)
        ys = []
        for q in range(n_q):
            u = (s - n_steps) * n_q + q
            perm = jnp.where(slot == pos_ref[u][0:1, :], 1.0, 0.0).astype(BF)
            ys.append(_dot_tn(perm, yp_ref[q]))
        y = jnp.concatenate(ys, axis=0)
        z = ALPHA * x + g_ref[...] * y.reshape(bB, bT, D)
        o_ref[...] = _layer_norm(z, lg_ref[...], lb_ref[...])


def _moe_call(x3, sc, sh, g2, wr, br, w1, w3, w2, l, ln_g, ln_b):
    B, T, D = x3.shape
    bB, bT = _tok_tiles(B, T)
    R = bB * bT
    if bB == 1:
        spp = 2 if B % 2 == 0 else 1
        nT = T // bT
        n_pools, n_steps = B // spp, spp * nT
        xmap = lambda p, s: (p * spp + (s % n_steps) // nT, (s % n_steps) % nT, 0)
        omap = lambda p, s: (p * spp + jnp.maximum(s - n_steps, 0) // nT, jnp.maximum(s - n_steps, 0) % nT, 0)
        mmap = lambda p, s: (p * spp + (s % n_steps) // nT, 0, 0)
        mshape = (1, 1, D)
    else:
        n_pools, n_steps = 1, B // bB
        xmap = lambda p, s: (s % n_steps, 0, 0)
        omap = lambda p, s: (jnp.maximum(s - n_steps, 0), 0, 0)
        mmap = lambda p, s: (0, 0, 0)
        mshape = (1, bT, D)
    n_sub = n_steps * (R // MOE_SUB)
    n_chunks = pl.cdiv(n_sub * (MOE_SUB + MOE_GROUPS * (MOE_BLK - 1)), MOE_ROWS) + MOE_GROUPS
    us = jnp.asarray(np.triu(np.ones((MOE_SUB, MOE_SUB), np.float32), 1), BF)
    smem = lambda n: pltpu.SMEM((n,), jnp.int32)
    return pl.pallas_call(
        functools.partial(_moe_kernel, n_steps=n_steps),
        grid=(n_pools, 2 * n_steps),
        in_specs=[pl.BlockSpec((bB, bT, D), xmap),
                  pl.BlockSpec(mshape, mmap), pl.BlockSpec(mshape, mmap), pl.BlockSpec(mshape, mmap),
                  _const(wr.shape), _const(br.shape), _const(us.shape),
                  _resident_layer(w1.shape, l), _resident_layer(w3.shape, l), _resident_layer(w2.shape, l),
                  _const((1, 1, D)), _const((1, 1, D))],
        out_specs=pl.BlockSpec((bB, bT, D), omap),
        out_shape=jax.ShapeDtypeStruct((B, T, D), F32),
        scratch_shapes=[pltpu.VMEM((n_chunks * MOE_ROWS, D), BF), pltpu.VMEM((n_chunks * MOE_ROWS, 8), F32),
                        pltpu.VMEM((R // MOE_SUB, MOE_NPS, D), BF), pltpu.VMEM((R // MOE_SUB, MOE_NPS, 8), F32),
                        pltpu.VMEM((R // MOE_SUB, MOE_NPS, D), BF),
                        pltpu.VMEM((n_sub, 8, MOE_SUB), F32),
                        smem(n_chunks), smem(MOE_GROUPS), smem(MOE_GROUPS), smem(1),
                        *[smem(n_sub * MOE_GROUPS) for _ in range(5)]],
        compiler_params=_cp(2),
        name="moe_ln",
    )(x3, sc, sh, g2, wr, br, us, w1, w3, w2, ln_g.reshape(1, 1, D), ln_b.reshape(1, 1, D))


def _router_params(w_group, b_group, w_expert, b_expert):
    wr = jnp.zeros((LANE, D_MODEL), F32).at[:MOE_GROUPS].set(w_group.T)
    wr = wr.at[ROUTE_OFF:ROUTE_OFF + MOE_EXPERTS].set(w_expert.T)
    br = jnp.zeros((LANE, 1), F32).at[:MOE_GROUPS, 0].set(b_group).at[ROUTE_OFF:ROUTE_OFF + MOE_EXPERTS, 0].set(b_expert)
    return wr.astype(BF), br


def kernel(x_prompt, x_sample, c_prompt, c_sample, state_gla, state_ret, state_ssd, state_conv, w_ada, b_ada, w_in, gla_w_gate, gla_b_gate, gla_norm, ret_norm, ssd_conv_w, ssd_conv_b, ssd_dt_bias, ssd_a_log, ssd_d, ssd_norm, w_out, ln1_g, ln1_b, moe_w_group, moe_b_group, moe_w_expert, moe_b_expert, moe_w1, moe_w3, moe_w2, ln2_g, ln2_b):
    Bp, Tp, D = x_prompt.shape
    Bs, Ts, _ = x_sample.shape
    w_in_t = jnp.swapaxes(w_in, 1, 2)
    w_out_b = w_out.astype(BF)
    w1_b, w3_b, w2_b = moe_w1.astype(BF), moe_w3.astype(BF), moe_w2.astype(BF)

    mod = _mod_call(jnp.concatenate([c_prompt, c_sample], axis=0), w_ada, b_ada)

    def moe(x, sc2, sh2, g2, l):
        wr, br = _router_params(moe_w_group[l], moe_b_group[l], moe_w_expert[l], moe_b_expert[l])
        return _moe_call(x, sc2, sh2, g2, wr, br, w1_b, w3_b, w2_b, l, ln2_g[l], ln2_b[l])

    x = x_prompt
    new = [[], [], [], []]
    for l in range(DEPTH):
        sh1, sc1, g1, sh2, sc2, g2 = (mod[l, :Bp, None, i * D:(i + 1) * D] for i in range(6))
        gin, rin, sin_ = _inproj_call(x, sc1, sh1, w_in_t, l)
        og, s_gla = _gla_prompt_call(gin, Bp, Tp, gla_w_gate[l], gla_b_gate[l], gla_norm[l])
        orr, s_ret = _ret_prompt_call(rin, Bp, Tp, ret_norm[l])
        x, s_ssd, s_conv = _ssd_prompt_call(sin_, x, g1, og, orr, w_out_b[l], ln1_g[l], ln1_b[l], ssd_conv_w[l],
                                            ssd_conv_b[l], ssd_dt_bias[l], ssd_a_log[l], ssd_d[l], ssd_norm[l])
        x = moe(x, sc2, sh2, g2, l)
        for acc, s in zip(new, (s_gla, s_ret, s_ssd, s_conv)):
            acc.append(s)
    y_p = x
    gla_p, ret_p, ssd_p, conv_p = (jnp.stack(a) for a in new)

    x = jnp.swapaxes(x_sample, 0, 1)
    sg = jnp.transpose(state_gla, (0, 2, 3, 4, 1))
    sr = jnp.transpose(state_ret, (0, 2, 3, 4, 1))
    ss = jnp.transpose(state_ssd, (0, 2, 3, 4, 1))
    cv = jnp.transpose(state_conv, (0, 2, 1, 3))
    gla_n = ret_n = ssd_n = conv_n = None
    for l in range(DEPTH):
        sh1, sc1, g1, sh2, sc2, g2 = (mod[l, None, Bp:, i * D:(i + 1) * D] for i in range(6))
        gT, rT, sT = _inproj_t_call(x, sc1, sh1, w_in_t, l)
        ogT, gla_n = _gla_t_call(gT, sg, gla_n, l, Ts, gla_w_gate[l], gla_b_gate[l], gla_norm[l])
        orT, ret_n = _ret_t_call(rT, sr, ret_n, l, Ts, ret_norm[l])
        osT, ssd_n, conv_n = _ssd_t_call(sT, cv, ss, ssd_n, conv_n, l, Ts, ssd_conv_w[l], ssd_conv_b[l],
                                         ssd_dt_bias[l], ssd_a_log[l], ssd_d[l], ssd_norm[l])
        x = _outproj_t_call(x, g1, ogT, orT, osT, w_out_b[l], ln1_g[l], ln1_b[l])
        x = moe(x, sc2, sh2, g2, l)
    y_s = jnp.swapaxes(x, 0, 1)
    gla_s = jnp.transpose(gla_n, (0, 4, 1, 2, 3))
    ret_s = jnp.transpose(ret_n, (0, 4, 1, 2, 3))
    ssd_s = jnp.transpose(ssd_n, (0, 4, 1, 2, 3))
    conv_s = jnp.transpose(conv_n, (0, 2, 1, 3))
    return (y_p, y_s, gla_p, ret_p, ssd_p, conv_p, gla_s, ret_s, ssd_s, conv_s)
```

```python
import functools

import numpy as np
import jax
import jax.numpy as jnp
from jax import lax
from jax.experimental import pallas as pl
from jax.experimental.pallas import tpu as pltpu

F32 = jnp.float32
BF = jnp.bfloat16

D_MODEL = 1024
DEPTH = 2
PAST_LEN = 16384
GLA_H, GLA_DK, GLA_DV = 4, 32, 64
GLA_WIDTH = GLA_H * GLA_DV
GLA_GATE_RANK = 16
GLA_GATE_TEMP = 16.0
GLA_CHUNK = 16
RET_H, RET_DK, RET_DV = 4, 64, 64
RET_WIDTH = RET_H * RET_DV
ROPE_BASE = 10000.0
SSD_H, SSD_P, SSD_G, SSD_N = 8, 64, 2, 64
SSD_WIDTH = SSD_H * SSD_P
SSD_CONV_W = 4
SSD_CONV_DIM = SSD_WIDTH + 2 * SSD_G * SSD_N
MOE_GROUPS, MOE_PER_GROUP = 4, 4
MOE_EXPERTS = MOE_GROUPS * MOE_PER_GROUP
MOE_FF = 256
ALPHA = (2 * DEPTH) ** 0.25
EPS = 1e-5

LANE = 128
GLA_IN_W = 128 + 128 + 256 + LANE + 256
RET_IN_W = 4 * 256
SSD_IN_W = 512 + SSD_CONV_DIM + LANE
IN_W = GLA_IN_W + RET_IN_W + SSD_IN_W
VMEM_LIMIT = 56 * 1024 * 1024


def _cp(n_axes, vmem=VMEM_LIMIT):
    return pltpu.CompilerParams(dimension_semantics=("arbitrary",) * n_axes, vmem_limit_bytes=vmem)


def _dot(a, b):
    return jnp.dot(a, b, preferred_element_type=F32)


def _dot_nt(a, b):
    return lax.dot_general(a, b, (((1,), (1,)), ((), ())), preferred_element_type=F32)


def _dot_tn(a, b):
    return lax.dot_general(a, b, (((0,), (0,)), ((), ())), preferred_element_type=F32)


def _split3(x):
    hi = x.astype(BF)
    r = x - hi.astype(F32)
    mid = r.astype(BF)
    lo = (r - mid.astype(F32)).astype(BF)
    return hi, mid, lo


def _dot_x3(x, e):
    hi, mid, lo = _split3(x)
    return _dot(hi, e) + (_dot(mid, e) + _dot(lo, e))


def _dot_x2(x, e):
    hi = x.astype(BF)
    lo = (x - hi.astype(F32)).astype(BF)
    return _dot(hi, e) + _dot(lo, e)


def _dot_3x(e, x):
    hi, mid, lo = _split3(x)
    return _dot(e, hi) + (_dot(e, mid) + _dot(e, lo))


def _sigmoid(x):
    return 1.0 / (1.0 + jnp.exp(-x))


def _silu(x):
    return x * _sigmoid(x)


def _log_sigmoid(x):
    return jnp.minimum(x, 0.0) - jnp.log(1.0 + jnp.exp(-jnp.abs(x)))


def _softplus(x):
    return jnp.maximum(x, 0.0) + jnp.log(1.0 + jnp.exp(-jnp.abs(x)))


def _layer_norm(x, g, b):
    mu = jnp.mean(x, axis=-1, keepdims=True)
    d = x - mu
    var = jnp.mean(d * d, axis=-1, keepdims=True)
    return d * lax.rsqrt(var + EPS) * g + b


def _const(shape):
    return pl.BlockSpec(shape, lambda *_: (0,) * len(shape))


def _resident_layer(shape, l):
    return pl.BlockSpec((None,) + tuple(shape[1:]), lambda *_: (l,) + (0,) * (len(shape) - 1),
                        pipeline_mode=pl.Buffered(1))


def _mod_kernel(c_ref, w_ref, b_ref, o_ref):
    s = _silu(c_ref[...]).astype(BF)
    o_ref[0] = _dot(s, w_ref[0].astype(BF)) + b_ref[0]


def _mod_call(c_all, w_ada, b_ada):
    R = c_all.shape[0]
    tn = 1536
    return pl.pallas_call(
        _mod_kernel,
        grid=(DEPTH, 6 * D_MODEL // tn),
        in_specs=[pl.BlockSpec((R, D_MODEL), lambda l, j: (0, 0)),
                  pl.BlockSpec((1, D_MODEL, tn), lambda l, j: (l, 0, j)),
                  pl.BlockSpec((1, 1, tn), lambda l, j: (l, 0, j))],
        out_specs=pl.BlockSpec((1, R, tn), lambda l, j: (l, 0, j)),
        out_shape=jax.ShapeDtypeStruct((DEPTH, R, 6 * D_MODEL), F32),
        compiler_params=_cp(2),
        name="ada_mod",
    )(c_all, w_ada, b_ada.reshape(DEPTH, 1, 6 * D_MODEL))


N_IN = 3096
N_GA = 128 + 128 + 256
N_DT = N_IN - SSD_H


def _inproj_kernel(x_ref, sc_ref, sh_ref, wt_ref, og_ref, or_ref, os_ref, w_ref):
    bB, bT, D = x_ref.shape

    @pl.when((pl.program_id(0) == 0) & (pl.program_id(1) == 0))
    def _():
        lane = lax.broadcasted_iota(jnp.int32, (1, LANE), 1)
        for j in range(N_GA // LANE):
            w_ref[:, j * LANE:(j + 1) * LANE] = wt_ref[j * LANE:(j + 1) * LANE, :].T.astype(BF)
        ga = wt_ref[N_GA:N_GA + LANE, :].T
        w_ref[:, N_GA:N_GA + LANE] = jnp.where(lane < GLA_GATE_RANK, ga, 0.0).astype(BF)
        src0, dst0 = N_GA + GLA_GATE_RANK, N_GA + LANE
        for j in range((N_DT - src0) // LANE):
            w_ref[:, dst0 + j * LANE:dst0 + (j + 1) * LANE] = \
                wt_ref[src0 + j * LANE:src0 + (j + 1) * LANE, :].T.astype(BF)
        dt = pltpu.roll(wt_ref[N_IN - LANE:N_IN, :].T, SSD_H, 1)
        w_ref[:, IN_W - LANE:IN_W] = jnp.where(lane < SSD_H, dt, 0.0).astype(BF)

    h = x_ref[...] * (1.0 + sc_ref[...]) + sh_ref[...]
    hb = h.reshape(bB * bT, D).astype(BF)
    og_ref[...] = _dot(hb, w_ref[:, 0:GLA_IN_W])
    or_ref[...] = _dot(hb, w_ref[:, GLA_IN_W:GLA_IN_W + RET_IN_W])
    os_ref[...] = _dot(hb, w_ref[:, GLA_IN_W + RET_IN_W:IN_W])


def _tok_tiles(B, T):
    if T >= 512:
        return 1, 512
    return 512 // T, T


def _inproj_call(x3, sc, sh, wt, l):
    B, T, D = x3.shape
    bB, bT = _tok_tiles(B, T)
    nT = T // bT
    R = bB * bT
    N = B * T
    xmap = lambda i, j: (i, j, 0)
    mmap = lambda i, j: (i, 0, 0)
    omap = lambda i, j: (i * nT + j, 0)
    return pl.pallas_call(
        _inproj_kernel,
        grid=(B // bB, nT),
        in_specs=[pl.BlockSpec((bB, bT, D), xmap),
                  pl.BlockSpec((bB, 1, D), mmap),
                  pl.BlockSpec((bB, 1, D), mmap),
                  _resident_layer(wt.shape, l)],
        out_specs=[pl.BlockSpec((R, GLA_IN_W), omap),
                   pl.BlockSpec((R, RET_IN_W), omap),
                   pl.BlockSpec((R, SSD_IN_W), omap)],
        out_shape=[jax.ShapeDtypeStruct((N, GLA_IN_W), F32),
                   jax.ShapeDtypeStruct((N, RET_IN_W), F32),
                   jax.ShapeDtypeStruct((N, SSD_IN_W), F32)],
        scratch_shapes=[pltpu.VMEM((D, IN_W), BF)],
        compiler_params=_cp(2),
        name="in_proj",
    )(x3, sc, sh, wt)


def _head_block_mask(rows_per, cols_per, n):
    r = np.arange(rows_per * n)[:, None] // rows_per
    c = np.arange(cols_per * n)[None, :] // cols_per
    return (r == c).astype(np.float32)


def _block_tril(n, c):
    i = np.arange(n)[:, None]
    j = np.arange(n)[None, :]
    return ((i // c == j // c) & (j <= i)).astype(np.float32)


def _gla_front(x_ref, wg_ref, bg_ref, L_ref):
    q = x_ref[:, 0:128] * (GLA_DK ** -0.5)
    k = x_ref[:, 128:256]
    v = x_ref[:, 256:512]
    ga = x_ref[:, 512:640]
    r = x_ref[:, 640:896]
    gate = _dot(ga.astype(BF), wg_ref[...]) + bg_ref[...]
    la = _log_sigmoid(gate) * (1.0 / GLA_GATE_TEMP)
    n = L_ref.shape[0]
    g = jnp.concatenate([_dot_3x(L_ref[...], la[i:i + n, :]) for i in range(0, la.shape[0], n)], axis=0)
    return q, k, v, r, g


def _gla_intra(q, g, kp_ref, gp_ref, vp_ref, E_ref, c):
    TT = q.shape[0]
    PAD = kp_ref.shape[0] - TT
    pos = lax.broadcasted_iota(jnp.int32, (TT, 1), 0) & (c - 1)
    o = jnp.zeros((TT, 2 * LANE), F32)
    for s in range(min(c, 8)):
        ks = kp_ref[pl.ds(PAD - s, TT), :]
        gs = gp_ref[pl.ds(PAD - s, TT), :]
        vs = vp_ref[pl.ds(PAD - s, TT), :]
        w = jnp.where(pos >= s, q * ks * jnp.exp(g - gs), 0.0)
        o = o + _dot(w.astype(BF), E_ref[...]) * vs
    if c <= 8:
        return o
    assert c == 16
    nc = TT // c

    def upper(x):
        return x.reshape(nc, 2, 8, x.shape[-1])[:, 1].reshape(nc * 8, x.shape[-1])

    qu, gu = upper(q), upper(g)
    posu = lax.broadcasted_iota(jnp.int32, (nc * 8, 1), 0) & 7
    ou = jnp.zeros((nc * 8, 2 * LANE), F32)
    for s in range(8, c):
        ks = upper(kp_ref[pl.ds(PAD - s, TT), :])
        gs = upper(gp_ref[pl.ds(PAD - s, TT), :])
        vs = upper(vp_ref[pl.ds(PAD - s, TT), :])
        w = jnp.where(posu >= s - 8, qu * ks * jnp.exp(gu - gs), 0.0)
        ou = ou + _dot(w.astype(BF), E_ref[...]) * vs
    ou = ou.reshape(nc, 1, 8, 2 * LANE)
    return o + jnp.concatenate([jnp.zeros_like(ou), ou], axis=1).reshape(TT, 2 * LANE)


def _gla_norm_gate(o, r, nw_ref, EA_ref):
    ms = _dot_x3(o * o, EA_ref[...])
    return o * lax.rsqrt(ms + EPS) * nw_ref[...] * _silu(r)


def _gla_prompt_kernel(x_ref, wg_ref, bg_ref, nw_ref, L_ref, E_ref, EA_ref, M_ref,
                       o_ref, sfin_ref, st_ref, kp_ref, gp_ref, vp_ref, oi_ref, u_ref, sb_ref, *, c):
    TT = x_ref.shape[0]
    nc = TT // c
    PAD = kp_ref.shape[0] - TT
    t = pl.program_id(1)

    @pl.when(t == 0)
    def _():
        st_ref[...] = jnp.zeros_like(st_ref)

    q, k, v, r, g = _gla_front(x_ref, wg_ref, bg_ref, L_ref)
    kp_ref[0:PAD, :] = jnp.zeros((PAD, LANE), F32)
    gp_ref[0:PAD, :] = jnp.zeros((PAD, LANE), F32)
    vp_ref[0:PAD, :] = jnp.zeros((PAD, 2 * LANE), F32)
    kp_ref[PAD:PAD + TT, :] = k
    gp_ref[PAD:PAD + TT, :] = g
    vp_ref[PAD:PAD + TT, :] = v
    o = _gla_intra(q, g, kp_ref, gp_ref, vp_ref, E_ref, c)

    M = M_ref[...]
    gl_all = gp_ref[pl.ds(PAD + c - 1, nc, stride=c), :]
    for n in range(nc):
        lo = n * c
        ke = (k[lo:lo + c, :] * jnp.exp(gl_all[n:n + 1, :] - g[lo:lo + c, :])).astype(BF)
        u_ref[n] = _dot_tn(ke, v[lo:lo + c, :].astype(BF)) * M
    a_cols = jnp.concatenate([jnp.exp(gl_all), jnp.zeros((LANE - nc, LANE), F32)], axis=0).T
    S = st_ref[...]
    for n in range(nc):
        sb_ref[n] = S.astype(BF)
        S = a_cols[:, n:n + 1] * S + u_ref[n]
    st_ref[...] = S
    qe = (q * jnp.exp(g)).astype(BF)
    for n in range(nc):
        lo = n * c
        oi_ref[lo:lo + c, :] = _dot(qe[lo:lo + c, :], sb_ref[n])
    o = o + oi_ref[...]
    o_ref[...] = _gla_norm_gate(o, r, nw_ref, EA_ref).astype(o_ref.dtype)

    @pl.when(t == pl.num_programs(1) - 1)
    def _():
        for h in range(GLA_H):
            sfin_ref[0, h] = S[h * GLA_DK:(h + 1) * GLA_DK, h * GLA_DV:(h + 1) * GLA_DV]


def _gla_tables(TT, c):
    L = jnp.asarray(_block_tril(TT, c), BF)
    E = jnp.asarray(_head_block_mask(GLA_DK, GLA_DV, GLA_H), BF)
    EA = jnp.asarray(_head_block_mask(GLA_DV, GLA_DV, GLA_H) / GLA_DV, BF)
    M = jnp.asarray(_head_block_mask(GLA_DK, GLA_DV, GLA_H), F32)
    return L, E, EA, M


def _gla_params(w_gate, b_gate, norm_w):
    wg = jnp.zeros((LANE, GLA_H * GLA_DK), F32).at[:GLA_GATE_RANK].set(w_gate).astype(BF)
    return wg, b_gate.reshape(1, -1), norm_w.reshape(1, -1)


def _gla_prompt_call(gin, B, T, w_gate, b_gate, norm_w):
    TT, c = (1024 if T % 1024 == 0 else 512), GLA_CHUNK
    nT = T // TT
    L, E, EA, M = _gla_tables(min(TT, 256), c)
    wg, bg, nw = _gla_params(w_gate, b_gate, norm_w)
    PAD = 16
    return pl.pallas_call(
        functools.partial(_gla_prompt_kernel, c=c),
        grid=(B, nT),
        in_specs=[pl.BlockSpec((TT, GLA_IN_W), lambda b, t: (b * nT + t, 0)),
                  _const(wg.shape), _const(bg.shape), _const(nw.shape),
                  _const(L.shape), _const(E.shape), _const(EA.shape), _const(M.shape)],
        out_specs=[pl.BlockSpec((TT, GLA_WIDTH), lambda b, t: (b * nT + t, 0)),
                   pl.BlockSpec((1, GLA_H, GLA_DK, GLA_DV), lambda b, t: (b, 0, 0, 0))],
        out_shape=[jax.ShapeDtypeStruct((B * T, GLA_WIDTH), BF),
                   jax.ShapeDtypeStruct((B, GLA_H, GLA_DK, GLA_DV), F32)],
        scratch_shapes=[pltpu.VMEM((GLA_H * GLA_DK, GLA_H * GLA_DV), F32),
                        pltpu.VMEM((TT + PAD, LANE), F32),
                        pltpu.VMEM((TT + PAD, LANE), F32),
                        pltpu.VMEM((TT + PAD, 2 * LANE), F32),
                        pltpu.VMEM((TT, 2 * LANE), F32),
                        pltpu.VMEM((TT // c, GLA_H * GLA_DK, GLA_H * GLA_DV), F32),
                        pltpu.VMEM((TT // c, GLA_H * GLA_DK, GLA_H * GLA_DV), BF)],
        compiler_params=_cp(2),
        name="gla_prompt",
    )(gin, wg, bg, nw, L, E, EA, M)


def _rope(x, cos, sin_signed):
    lane = lax.broadcasted_iota(jnp.int32, (1, LANE), 1)
    first_half = (lane & (RET_DK - 1)) < RET_DK // 2
    out = []
    for p in range(2):
        xs = x[:, p * LANE:(p + 1) * LANE]
        up = pltpu.roll(xs, LANE - RET_DK // 2, 1)
        dn = pltpu.roll(xs, RET_DK // 2, 1)
        out.append(xs * cos + jnp.where(first_half, up, dn) * sin_signed)
    return jnp.concatenate(out, axis=1)


def _ret_front(x_ref, cos_ref, sin_ref, rows=slice(None)):
    cos, sin = cos_ref[rows, :], sin_ref[rows, :]
    q = _rope(x_ref[rows, 0:256], cos, sin)
    k = _rope(x_ref[rows, 256:512], cos, sin) * (RET_DK ** -0.5)
    v = x_ref[rows, 512:768]
    rg = x_ref[rows, 768:1024]
    return q, k, v, rg


def _ret_intra(q, k, v, D_ref):
    lane = lax.broadcasted_iota(jnp.int32, (1, RET_WIDTH), 1)
    kb = k.astype(BF)
    o = jnp.zeros(q.shape, F32)
    for h in range(RET_H):
        hm = (lane // RET_DK) == h
        s = _dot_nt(jnp.where(hm, q, 0.0).astype(BF), kb)
        p = (s * D_ref[h]).astype(BF)
        o = o + _dot(p, jnp.where(hm, v, 0.0).astype(BF))
    return o


def _ret_norm_gate(o, rg, nw_ref, EA_ref):
    mu = _dot_x3(o, EA_ref[...])
    d = o - mu
    var = _dot_x3(d * d, EA_ref[...])
    return d * lax.rsqrt(var + EPS) * nw_ref[...] * _silu(rg)


def _ret_prompt_kernel(x_ref, cos_ref, sin_ref, D_ref, rd_ref, kd_ref, G_ref, M_ref, EA_ref, nw_ref,
                       o_ref, sfin_ref, st_ref):
    t = pl.program_id(1)

    @pl.when(t == 0)
    def _():
        st_ref[...] = jnp.zeros_like(st_ref)

    C = D_ref.shape[1]
    S = st_ref[...]
    for i in range(x_ref.shape[0] // C):
        rows = slice(i * C, (i + 1) * C)
        q, k, v, rg = _ret_front(x_ref, cos_ref, sin_ref, rows)
        o = _ret_intra(q, k, v, D_ref)
        o = o + _dot((q * rd_ref[...]).astype(BF), S.astype(BF))
        u = _dot_tn((k * kd_ref[...]).astype(BF), v.astype(BF))
        S = S * G_ref[...] + u * M_ref[...]
        o_ref[rows, :] = _ret_norm_gate(o, rg, nw_ref, EA_ref).astype(o_ref.dtype)
    st_ref[...] = S

    @pl.when(t == pl.num_programs(1) - 1)
    def _():
        for h in range(RET_H):
            sfin_ref[0, h] = S[h * RET_DK:(h + 1) * RET_DK, h * RET_DV:(h + 1) * RET_DV]


def _rope_tables(pos):
    half = RET_DK // 2
    inv = ROPE_BASE ** (-jnp.arange(half, dtype=F32) / half)
    ang = pos.astype(F32)[:, None] * inv[None, :]
    cos, sin = jnp.cos(ang), jnp.sin(ang)
    return jnp.tile(jnp.concatenate([cos, cos], 1), (1, 2)), jnp.tile(jnp.concatenate([-sin, sin], 1), (1, 2))


def _ret_log_gamma():
    return np.log(1.0 - 2.0 ** (-5.0 - np.arange(RET_H, dtype=np.float64)))


def _ret_prompt_call(rin, B, T, norm_w):
    C = 256
    TT = next(n * C for n in (8, 4, 1) if T % (n * C) == 0)
    nT = T // TT
    cos, sin = _rope_tables(jnp.arange(T, dtype=jnp.int32))
    lg = _ret_log_gamma()
    i = np.arange(C)
    dec = np.exp(lg[:, None, None] * (i[:, None] - i[None, :])[None]) * (i[:, None] >= i[None, :])[None]
    Dm = jnp.asarray(dec, F32)
    rd = jnp.asarray(np.repeat(np.exp(lg[None, :] * (i[:, None] + 1)), RET_DK, 1), F32)
    kd = jnp.asarray(np.repeat(np.exp(lg[None, :] * (C - 1 - i[:, None])), RET_DK, 1), F32)
    M = _head_block_mask(RET_DK, RET_DV, RET_H)
    G = jnp.asarray(M * np.repeat(np.exp(lg * C), RET_DK)[:, None], F32)
    M = jnp.asarray(M, F32)
    EA = jnp.asarray(_head_block_mask(RET_DV, RET_DV, RET_H) / RET_DV, BF)
    nw = norm_w.reshape(1, -1)
    return pl.pallas_call(
        _ret_prompt_kernel,
        grid=(B, nT),
        in_specs=[pl.BlockSpec((TT, RET_IN_W), lambda b, t: (b * nT + t, 0)),
                  pl.BlockSpec((TT, LANE), lambda b, t: (t, 0)),
                  pl.BlockSpec((TT, LANE), lambda b, t: (t, 0)),
                  _const(Dm.shape), _const(rd.shape), _const(kd.shape), _const(G.shape), _const(M.shape),
                  _const(EA.shape), _const(nw.shape)],
        out_specs=[pl.BlockSpec((TT, RET_WIDTH), lambda b, t: (b * nT + t, 0)),
                   pl.BlockSpec((1, RET_H, RET_DK, RET_DV), lambda b, t: (b, 0, 0, 0))],
        out_shape=[jax.ShapeDtypeStruct((B * T, RET_WIDTH), BF),
                   jax.ShapeDtypeStruct((B, RET_H, RET_DK, RET_DV), F32)],
        scratch_shapes=[pltpu.VMEM((RET_H * RET_DK, RET_H * RET_DV), F32)],
        compiler_params=_cp(2),
        name="ret_prompt",
    )(rin, cos, sin, Dm, rd, kd, G, M, EA, nw)


def _ssd_conv(xp_ref, cw_ref, cb_ref, TT):
    acc = cb_ref[...] + cw_ref[SSD_CONV_W - 1:SSD_CONV_W, :] * xp_ref[pl.ds(8, TT), :]
    for i in range(SSD_CONV_W - 1):
        acc = acc + cw_ref[i:i + 1, :] * xp_ref[pl.ds(8 - (SSD_CONV_W - 1) + i, TT), :]
    return acc


def _ssd_intra(xs, bm, cm, g, dt, Mk_ref):
    TT = xs.shape[0]
    rT = (g - jnp.log(dt)).T
    lane = lax.broadcasted_iota(jnp.int32, (1, LANE), 1)
    lane2 = lax.broadcasted_iota(jnp.int32, (1, 2 * LANE), 1)
    causal = Mk_ref[...] > 0.0
    bmb = bm.astype(BF)
    zero = jnp.zeros((), BF)
    o_parts = []
    for grp in range(SSD_G):
        cb = _dot_nt(jnp.where((lane // SSD_N) == grp, cm, 0.0).astype(BF), bmb).astype(BF)
        xg = xs[:, grp * 2 * LANE:(grp + 1) * 2 * LANE].astype(BF)
        og = jnp.zeros((TT, 2 * LANE), F32)
        for h4 in range(SSD_H // SSD_G):
            h = grp * (SSD_H // SSD_G) + h4
            dec = jnp.where(causal, jnp.exp(g[:, h:h + 1] - rT[h:h + 1, :]), 0.0)
            p = cb * dec.astype(BF)
            og = og + _dot(p, jnp.where((lane2 // SSD_P) == h4, xg, zero))
        o_parts.append(og)
    return jnp.concatenate(o_parts, axis=1)


def _ssd_prompt_kernel(x_ref, cw_ref, cb_ref, dtb_ref, alog_ref, dexp_ref, nw_ref, L_ref, Mk_ref, Eexp_ref, M2_ref,
                       res_ref, g1_ref, og_ref, or_ref, wo_ref, lg_ref, lb_ref,
                       o_ref, sfin_ref, cfin_ref, st_ref, xp_ref):
    TT = x_ref.shape[0]
    t = pl.program_id(1)

    @pl.when(t == 0)
    def _():
        st_ref[...] = jnp.zeros_like(st_ref)
        xp_ref[0:8, :] = jnp.zeros((8, SSD_CONV_DIM), F32)

    z = x_ref[:, 0:SSD_WIDTH]
    xp_ref[8:8 + TT, :] = x_ref[:, SSD_WIDTH:SSD_WIDTH + SSD_CONV_DIM]
    sdt = x_ref[:, SSD_WIDTH + SSD_CONV_DIM:SSD_IN_W]
    xbc = _silu(_ssd_conv(xp_ref, cw_ref, cb_ref, TT))
    tail = xp_ref[TT:TT + 8, :]
    xp_ref[0:8, :] = tail
    dt_all = _softplus(sdt + dtb_ref[...])
    la_all = dt_all * (-jnp.exp(alog_ref[...]))
    Eexp = Eexp_ref[...]
    C = L_ref.shape[0]
    S = st_ref[...]
    for i in range(TT // C):
        rows = slice(i * C, (i + 1) * C)
        xs = xbc[rows, 0:SSD_WIDTH]
        bm = xbc[rows, SSD_WIDTH:SSD_WIDTH + LANE]
        cm = xbc[rows, SSD_WIDTH + LANE:SSD_CONV_DIM]
        dt = dt_all[rows, :]
        g = _dot_3x(L_ref[...], la_all[rows, :])
        gl = g[C - 1:C, :]
        eg_x = _dot_x2(jnp.exp(g), Eexp)
        cw_x = _dot_x2(dt * jnp.exp(gl - g), Eexp)
        egl_x = _dot_x2(jnp.exp(gl), Eexp)

        o = _ssd_intra(xs, bm, cm, g, dt, Mk_ref)
        o = o + eg_x * _dot(cm.astype(BF), S.astype(BF))
        u = _dot_tn(bm.astype(BF), (xs * cw_x).astype(BF))
        S = S * egl_x + u * M2_ref[...]

        y = (o + dexp_ref[...] * xs) * _silu(z[rows, :])
        ms = jnp.mean(y * y, axis=-1, keepdims=True)
        o_ssd = (y * lax.rsqrt(ms + EPS) * nw_ref[...]).astype(BF)
        merged = jnp.concatenate([og_ref[rows, :], or_ref[rows, :], o_ssd], axis=1)
        mix = _dot(merged, wo_ref[...])
        o_ref[0, rows, :] = _layer_norm(ALPHA * res_ref[0, rows, :] + g1_ref[0] * mix, lg_ref[0], lb_ref[0])
    st_ref[...] = S

    @pl.when(t == pl.num_programs(1) - 1)
    def _():
        for h in range(SSD_H):
            gi = h // (SSD_H // SSD_G)
            sfin_ref[0, h] = S[gi * SSD_N:(gi + 1) * SSD_N, h * SSD_P:(h + 1) * SSD_P]
        cfin_ref[0] = tail[8 - (SSD_CONV_W - 1):8, :]


def _pad_lanes(v, n=LANE):
    v = v.reshape(1, -1)
    return jnp.zeros((1, n), F32).at[:, :v.shape[1]].set(v)


def _ssd_tables(TT, c):
    L = jnp.asarray(_block_tril(TT, c), BF)
    Mk = jnp.asarray(_block_tril(TT, c), F32)
    e = np.zeros((LANE, SSD_WIDTH), np.float32)
    for h in range(SSD_H):
        e[h, h * SSD_P:(h + 1) * SSD_P] = 1.0
    M2 = np.zeros((SSD_G * SSD_N, SSD_WIDTH), np.float32)
    for h in range(SSD_H):
        gi = h // (SSD_H // SSD_G)
        M2[gi * SSD_N:(gi + 1) * SSD_N, h * SSD_P:(h + 1) * SSD_P] = 1.0
    return L, Mk, jnp.asarray(e, BF), jnp.asarray(M2, F32)


def _ssd_params(conv_w, conv_b, dt_bias, a_log, d, norm_w):
    return (conv_w, conv_b.reshape(1, -1), _pad_lanes(dt_bias), _pad_lanes(a_log),
            jnp.repeat(d, SSD_P).reshape(1, -1), norm_w.reshape(1, -1))


def _ssd_prompt_call(sin_, x3, g1, og, orr, w_out, ln_g, ln_b, conv_w, conv_b, dt_bias, a_log, d, norm_w):
    B, T, D = x3.shape
    C = 256
    TT = 4 * C if T % (4 * C) == 0 else (2 * C if T % (2 * C) == 0 else C)
    nT = T // TT
    L, Mk, Eexp, M2 = _ssd_tables(C, C)
    prm = _ssd_params(conv_w, conv_b, dt_bias, a_log, d, norm_w)
    rmap = lambda b, t: (b * nT + t, 0)
    return pl.pallas_call(
        _ssd_prompt_kernel,
        grid=(B, nT),
        in_specs=[pl.BlockSpec((TT, SSD_IN_W), rmap)]
                 + [_const(p.shape) for p in prm]
                 + [_const(L.shape), _const(Mk.shape), _const(Eexp.shape), _const(M2.shape)]
                 + [pl.BlockSpec((1, TT, D), lambda b, t: (b, t, 0)),
                    pl.BlockSpec((1, 1, D), lambda b, t: (b, 0, 0)),
                    pl.BlockSpec((TT, GLA_WIDTH), rmap), pl.BlockSpec((TT, RET_WIDTH), rmap),
                    _const((D, D)), _const((1, 1, D)), _const((1, 1, D))],
        out_specs=[pl.BlockSpec((1, TT, D), lambda b, t: (b, t, 0)),
                   pl.BlockSpec((1, SSD_H, SSD_N, SSD_P), lambda b, t: (b, 0, 0, 0)),
                   pl.BlockSpec((1, SSD_CONV_W - 1, SSD_CONV_DIM), lambda b, t: (b, 0, 0))],
        out_shape=[jax.ShapeDtypeStruct((B, T, D), F32),
                   jax.ShapeDtypeStruct((B, SSD_H, SSD_N, SSD_P), F32),
                   jax.ShapeDtypeStruct((B, SSD_CONV_W - 1, SSD_CONV_DIM), F32)],
        scratch_shapes=[pltpu.VMEM((SSD_G * SSD_N, SSD_WIDTH), F32),
                        pltpu.VMEM((TT + 8, SSD_CONV_DIM), F32)],
        compiler_params=_cp(2),
        name="ssd_outproj_ln",
    )(sin_, *prm, L, Mk, Eexp, M2, x3, g1, og, orr, w_out, ln_g.reshape(1, 1, D), ln_b.reshape(1, 1, D))


def _inproj_t_kernel(x_ref, sc_ref, sh_ref, wt_ref, og_ref, or_ref, os_ref, w_ref):
    nt, nb, D = x_ref.shape

    @pl.when(pl.program_id(0) == 0)
    def _():
        for src, dst, n in ((0, 0, N_GA + GLA_GATE_RANK), (N_GA + GLA_GATE_RANK, N_GA + LANE, N_DT - N_GA - GLA_GATE_RANK)):
            for r in range(0, n, 512):
                m = min(512, n - r)
                w_ref[dst + r:dst + r + m, :] = wt_ref[src + r:src + r + m, :].astype(BF)
        w_ref[N_GA + GLA_GATE_RANK:N_GA + LANE, :] = jnp.zeros((LANE - GLA_GATE_RANK, D), BF)
        tail = jnp.concatenate([wt_ref[N_DT:N_IN, :], jnp.zeros((LANE - SSD_H, D), F32)], axis=0)
        w_ref[IN_W - LANE:IN_W, :] = tail.astype(BF)

    h = x_ref[...] * (1.0 + sc_ref[...]) + sh_ref[...]
    for t in range(0, nt, 2):
        ht = h[t:t + 2].reshape(2 * nb, D).astype(BF)
        cols = slice(t * nb, (t + 2) * nb)
        og_ref[:, cols] = _dot_nt(w_ref[0:GLA_IN_W, :], ht)
        or_ref[:, cols] = _dot_nt(w_ref[GLA_IN_W:GLA_IN_W + RET_IN_W, :], ht)
        os_ref[:, cols] = _dot_nt(w_ref[GLA_IN_W + RET_IN_W:IN_W, :], ht)


def _inproj_t_call(xt, sc, sh, wt, l):
    T, B, D = xt.shape
    nt = 4
    cmap = lambda i: (0, i)
    return pl.pallas_call(
        _inproj_t_kernel,
        grid=(T // nt,),
        in_specs=[pl.BlockSpec((nt, B, D), lambda i: (i, 0, 0)),
                  pl.BlockSpec((1, B, D), lambda i: (0, 0, 0)),
                  pl.BlockSpec((1, B, D), lambda i: (0, 0, 0)),
                  _resident_layer(wt.shape, l)],
        out_specs=[pl.BlockSpec((GLA_IN_W, nt * B), cmap),
                   pl.BlockSpec((RET_IN_W, nt * B), cmap),
                   pl.BlockSpec((SSD_IN_W, nt * B), cmap)],
        out_shape=[jax.ShapeDtypeStruct((GLA_IN_W, T * B), F32),
                   jax.ShapeDtypeStruct((RET_IN_W, T * B), F32),
                   jax.ShapeDtypeStruct((SSD_IN_W, T * B), F32)],
        scratch_shapes=[pltpu.VMEM((IN_W, D), BF)],
        compiler_params=_cp(1),
        name="in_proj_t",
    )(xt, sc, sh, wt)


def _row_sum(x):
    return jnp.sum(x, axis=0, keepdims=True)


def _lane_state_readout(o, coef_ref, s0_ref, n_rows):
    nb = LANE
    half = len(o) // 2
    for part in range(2):
        def body(k8, accs, part=part):
            accs = list(accs)
            base = pl.multiple_of(k8 * 8, 8)
            grp = [coef_ref[pl.ds(base, 8), (part * half + i) * nb:(part * half + i + 1) * nb] for i in range(half)]
            for j in range(8):
                s0k = s0_ref[0, k8 * 8 + j]
                for i in range(half):
                    accs[i] = accs[i] + grp[i][j:j + 1, :] * s0k
            return tuple(accs)

        res = lax.fori_loop(0, n_rows // 8, body, tuple(o[part * half:(part + 1) * half]))
        o[part * half:(part + 1) * half] = list(res)
    return o


def _lane_state_update(sn_ref, s0_ref, decay_fn, coef_ref, val_fn, n_rows, T):
    nb = LANE

    def body(k8, carry):
        base = pl.multiple_of(k8 * 8, 8)
        grp = [coef_ref[pl.ds(base, 8), t * nb:(t + 1) * nb] for t in range(T)]
        dec = decay_fn(base)
        for j in range(8):
            dj = dec[j:j + 1, :] if dec.shape[0] == 8 else dec
            sk = dj * s0_ref[0, k8 * 8 + j]
            for t in range(T):
                sk = sk + grp[t][j:j + 1, :] * val_fn(t)
            sn_ref[0, 0, k8 * 8 + j] = sk
        return carry

    lax.fori_loop(0, n_rows // 8, body, 0)


def _state_specs(shape, l, first):
    assert l == 0 or not first
    tail = tuple(shape[2:])
    in_spec = pl.BlockSpec((None, 1) + tail, lambda h: (l, h, 0, 0, 0))
    out_spec = pl.BlockSpec(((shape[0] if first else 1), 1) + tail, lambda h: (l, h, 0, 0, 0))
    return in_spec, out_spec


def _zero_later_layers(ref):
    ref[1:] = jnp.zeros((ref.shape[0] - 1,) + tuple(ref.shape[1:]), ref.dtype)


def _finish_state_call(kern, n_in, first, prevs):
    if first:
        return functools.partial(kern, first=True), [], {}
    wrapped = lambda *a, **kw: kern(*a[:n_in], *a[n_in + len(prevs):], first=False, **kw)
    specs = [pl.BlockSpec(memory_space=pl.ANY)] * len(prevs)
    return wrapped, specs, {n_in + i: 1 + i for i in range(len(prevs))}


def _gla_t_kernel(x_ref, s0_ref, wg_ref, bg_ref, nw_ref, o_ref, sn_ref, qe_ref, ke_ref, a_ref, *, T, first):
    nb = LANE
    if first:
        _zero_later_layers(sn_ref)
    h = pl.program_id(0)
    r0 = pl.multiple_of(h * GLA_DK, GLA_DK)
    v0 = pl.multiple_of(h * GLA_DV, GLA_DV)
    q = x_ref[pl.ds(r0, GLA_DK), :] * (GLA_DK ** -0.5)
    k = x_ref[pl.ds(128 + r0, GLA_DK), :]
    gate = _dot(wg_ref[pl.ds(r0, GLA_DK), :], x_ref[512:640, :].astype(BF)) + bg_ref[pl.ds(r0, GLA_DK), :]
    la = _log_sigmoid(gate) * (1.0 / GLA_GATE_TEMP)
    gs = []
    acc = jnp.zeros((GLA_DK, nb), F32)
    for t in range(T):
        acc = acc + la[:, t * nb:(t + 1) * nb]
        gs.append(acc)
    gl = gs[T - 1]
    a_ref[...] = jnp.exp(gl)
    qs = [q[:, t * nb:(t + 1) * nb] for t in range(T)]
    ks = [k[:, t * nb:(t + 1) * nb] for t in range(T)]
    for t in range(T):
        qe_ref[:, t * nb:(t + 1) * nb] = qs[t] * jnp.exp(gs[t])
        ke_ref[:, t * nb:(t + 1) * nb] = ks[t] * jnp.exp(gl - gs[t])

    def vt(t):
        return x_ref[pl.ds(256 + v0, GLA_DV), t * nb:(t + 1) * nb]

    o = []
    for t in range(T):
        ot = jnp.zeros((GLA_DV, nb), F32)
        for u in range(t + 1):
            s = _row_sum(qs[t] * ks[u] * jnp.exp(gs[t] - gs[u]))
            ot = ot + s * vt(u)
        o.append(ot)

    o = _lane_state_readout(o, qe_ref, s0_ref, GLA_DK)
    _lane_state_update(sn_ref, s0_ref, lambda base: a_ref[pl.ds(base, 8), :], ke_ref, vt, GLA_DK, T)

    nw = nw_ref[pl.ds(v0, GLA_DV), :]
    for t in range(T):
        ms = jnp.mean(o[t] * o[t], axis=0, keepdims=True)
        r = x_ref[pl.ds(640 + v0, GLA_DV), t * nb:(t + 1) * nb]
        o_ref[:, t * nb:(t + 1) * nb] = (o[t] * lax.rsqrt(ms + EPS) * nw * _silu(r)).astype(o_ref.dtype)


def _gla_t_call(gT, s0, prev, l, T, w_gate, b_gate, norm_w):
    N = gT.shape[1]
    wg = jnp.zeros((GLA_H * GLA_DK, LANE), F32).at[:, :GLA_GATE_RANK].set(w_gate.T).astype(BF)
    bg = b_gate.reshape(-1, 1)
    nw = norm_w.reshape(-1, 1)
    first = prev is None
    prevs = [] if first else [prev]
    s_in, s_out = _state_specs(s0.shape, l, first)
    ins = [gT, s0, wg, bg, nw]
    specs = [_const(gT.shape), s_in, _const(wg.shape), _const(bg.shape), _const(nw.shape)]
    kern, pspecs, aliases = _finish_state_call(functools.partial(_gla_t_kernel, T=T), len(ins), first, prevs)
    ins, specs = ins + prevs, specs + pspecs
    return pl.pallas_call(
        kern,
        grid=(GLA_H,),
        in_specs=specs,
        out_specs=[pl.BlockSpec((GLA_DV, N), lambda h: (h, 0)), s_out],
        out_shape=[jax.ShapeDtypeStruct((GLA_WIDTH, N), BF), jax.ShapeDtypeStruct(s0.shape, F32)],
        scratch_shapes=[pltpu.VMEM((GLA_DK, N), F32), pltpu.VMEM((GLA_DK, N), F32), pltpu.VMEM((GLA_DK, LANE), F32)],
        input_output_aliases=aliases,
        compiler_params=_cp(1),
        name="gla_t",
    )(*ins)


def _ret_t_kernel(x_ref, s0_ref, cos_ref, sin_ref, pw_ref, nw_ref, o_ref, sn_ref, qd_ref, kd_ref, *, T, first):
    nb = LANE
    if first:
        _zero_later_layers(sn_ref)
    h = pl.program_id(0)
    r0 = pl.multiple_of(h * RET_DK, RET_DK)
    half_k = RET_DK // 2
    cos, sin = cos_ref[...], sin_ref[...]

    def rope_t(base):
        x1 = x_ref[pl.ds(base + r0, half_k), :]
        x2 = x_ref[pl.ds(base + r0 + half_k, half_k), :]
        return jnp.concatenate([x1 * cos - x2 * sin, x1 * sin + x2 * cos], axis=0)

    q = rope_t(0)
    k = rope_t(256) * (RET_DK ** -0.5)
    pw = pw_ref[h]
    qs = [q[:, t * nb:(t + 1) * nb] for t in range(T)]
    ks = [k[:, t * nb:(t + 1) * nb] for t in range(T)]
    for t in range(T):
        qd_ref[:, t * nb:(t + 1) * nb] = qs[t] * pw[t + 1:t + 2, :]
        kd_ref[:, t * nb:(t + 1) * nb] = ks[t] * pw[T - 1 - t:T - t, :]

    def vt(t):
        return x_ref[pl.ds(512 + r0, RET_DV), t * nb:(t + 1) * nb]

    o = []
    for t in range(T):
        ot = jnp.zeros((RET_DV, nb), F32)
        for u in range(t + 1):
            s = _row_sum(qs[t] * ks[u]) * pw[t - u:t - u + 1, :]
            ot = ot + s * vt(u)
        o.append(ot)

    o = _lane_state_readout(o, qd_ref, s0_ref, RET_DK)
    _lane_state_update(sn_ref, s0_ref, lambda base: pw[T:T + 1, :], kd_ref, vt, RET_DK, T)

    nw = nw_ref[pl.ds(r0, RET_DV), :]
    for t in range(T):
        mu = jnp.mean(o[t], axis=0, keepdims=True)
        d = o[t] - mu
        var = jnp.mean(d * d, axis=0, keepdims=True)
        rg = x_ref[pl.ds(768 + r0, RET_DV), t * nb:(t + 1) * nb]
        o_ref[:, t * nb:(t + 1) * nb] = (d * lax.rsqrt(var + EPS) * nw * _silu(rg)).astype(o_ref.dtype)


def _ret_t_call(rT, s0, prev, l, T, norm_w):
    N = rT.shape[1]
    B = N // T
    half = RET_DK // 2
    inv = ROPE_BASE ** (-jnp.arange(half, dtype=F32) / half)
    ang = inv[:, None] * (PAST_LEN + jnp.arange(T, dtype=jnp.int32)).astype(F32)[None, :]
    cos = jnp.repeat(jnp.cos(ang), B, axis=1)
    sin = jnp.repeat(jnp.sin(ang), B, axis=1)
    lg = _ret_log_gamma()
    pw = jnp.asarray(np.repeat(np.exp(lg[:, None] * np.arange(16)[None, :])[:, :, None], LANE, axis=2), F32)
    nw = norm_w.reshape(-1, 1)
    first = prev is None
    prevs = [] if first else [prev]
    s_in, s_out = _state_specs(s0.shape, l, first)
    ins = [rT, s0, cos, sin, pw, nw]
    specs = [_const(rT.shape), s_in, _const(cos.shape), _const(sin.shape), _const(pw.shape), _const(nw.shape)]
    kern, pspecs, aliases = _finish_state_call(functools.partial(_ret_t_kernel, T=T), len(ins), first, prevs)
    ins, specs = ins + prevs, specs + pspecs
    return pl.pallas_call(
        kern,
        grid=(RET_H,),
        in_specs=specs,
        out_specs=[pl.BlockSpec((RET_DV, N), lambda h: (h, 0)), s_out],
        out_shape=[jax.ShapeDtypeStruct((RET_WIDTH, N), BF), jax.ShapeDtypeStruct(s0.shape, F32)],
        scratch_shapes=[pltpu.VMEM((RET_DK, N), F32), pltpu.VMEM((RET_DK, N), F32)],
        input_output_aliases=aliases,
        compiler_params=_cp(1),
        name="ret_t",
    )(*ins)


def _ssd_t_kernel(x_ref, c0_ref, s0_ref, cw_ref, cb_ref, dtb_ref, alog_ref, d_ref, nw_ref,
                  o_ref, sn_ref, cn_ref, hist_ref, y_ref, ssq_ref, cm_ref, bw_ref, xw_ref, *, T, first):
    nb = LANE
    W1 = SSD_CONV_W - 1
    h = pl.program_id(0)
    XB = SSD_WIDTH
    if first:
        _zero_later_layers(sn_ref)

    @pl.when(h == 0)
    def _():
        ssq_ref[...] = jnp.zeros_like(ssq_ref)
        if first:
            _zero_later_layers(cn_ref)
        for i in range(W1):
            for j in range(SSD_CONV_DIM // LANE):
                hist_ref[j * LANE:(j + 1) * LANE, i * nb:(i + 1) * nb] = c0_ref[0, i][:, j * LANE:(j + 1) * LANE].T
                cn_ref[0, i, :, j * LANE:(j + 1) * LANE] = \
                    x_ref[XB + j * LANE:XB + (j + 1) * LANE, (T - W1 + i) * nb:(T - W1 + i + 1) * nb].T

    def conv_rows(ro):
        w = cw_ref[pl.ds(ro, 64), :]
        b = cb_ref[pl.ds(ro, 64), :]
        xx = [hist_ref[pl.ds(ro, 64), i * nb:(i + 1) * nb] for i in range(W1)]
        xx += [x_ref[pl.ds(XB + ro, 64), t * nb:(t + 1) * nb] for t in range(T)]
        out = []
        for t in range(T):
            acc = b + w[:, 0:1] * xx[t]
            for i in range(1, SSD_CONV_W):
                acc = acc + w[:, i:i + 1] * xx[t + i]
            out.append(_silu(acc))
        return out

    grp = h // (SSD_H // SSD_G)
    xs = conv_rows(pl.multiple_of(h * SSD_P, SSD_P))
    bm = conv_rows(pl.multiple_of(SSD_WIDTH + grp * SSD_N, SSD_N))
    cm = conv_rows(pl.multiple_of(SSD_WIDTH + SSD_G * SSD_N + grp * SSD_N, SSD_N))

    dt_all = _softplus(x_ref[pl.ds(XB + SSD_CONV_DIM + h, 1), :] + dtb_ref[pl.ds(h, 1), :])
    a = -jnp.exp(alog_ref[pl.ds(h, 1), :])
    dts = [dt_all[:, t * nb:(t + 1) * nb] for t in range(T)]
    gs = []
    acc = jnp.zeros((1, nb), F32)
    for t in range(T):
        acc = acc + dts[t] * a
        gs.append(acc)
    gl = gs[T - 1]

    o = []
    for t in range(T):
        ot = jnp.zeros((SSD_P, nb), F32)
        for u in range(t + 1):
            s = _row_sum(cm[t] * bm[u]) * (jnp.exp(gs[t] - gs[u]) * dts[u])
            ot = ot + s * xs[u]
        o.append(ot)

    for t in range(T):
        cm_ref[:, t * nb:(t + 1) * nb] = cm[t] * jnp.exp(gs[t])
        bw_ref[:, t * nb:(t + 1) * nb] = bm[t]
        xw_ref[:, t * nb:(t + 1) * nb] = xs[t] * (dts[t] * jnp.exp(gl - gs[t]))

    o = _lane_state_readout(o, cm_ref, s0_ref, SSD_N)
    egl = jnp.exp(gl)
    _lane_state_update(sn_ref, s0_ref, lambda base: egl, bw_ref, lambda t: xw_ref[:, t * nb:(t + 1) * nb], SSD_N, T)

    dd = d_ref[pl.ds(h, 1), :]
    p0 = pl.multiple_of(h * SSD_P, SSD_P)
    for t in range(T):
        z = x_ref[pl.ds(p0, SSD_P), t * nb:(t + 1) * nb]
        y = (o[t] + dd * xs[t]) * _silu(z)
        y_ref[pl.ds(p0, SSD_P), t * nb:(t + 1) * nb] = y
        ssq_ref[:, t * nb:(t + 1) * nb] += _row_sum(y * y)

    @pl.when(h == SSD_H - 1)
    def _():
        scale = lax.rsqrt(ssq_ref[...] * (1.0 / SSD_WIDTH) + EPS)
        o_ref[...] = (y_ref[...] * scale * nw_ref[...]).astype(o_ref.dtype)


def _ssd_t_call(sT, c0, s0, prev_s, prev_c, l, T, conv_w, conv_b, dt_bias, a_log, d, norm_w):
    N = sT.shape[1]
    col = lambda v: jnp.zeros((LANE, 1), F32).at[:SSD_H, 0].set(v)
    prm = (conv_w.T, conv_b.reshape(-1, 1), col(dt_bias), col(a_log), col(d), norm_w.reshape(-1, 1))
    first = prev_s is None
    prevs = [] if first else [prev_s, prev_c]
    s_in, s_out = _state_specs(s0.shape, l, first)
    c_in = pl.BlockSpec((1,) + tuple(c0.shape[1:]), lambda h: (l, 0, 0, 0))
    c_out = pl.BlockSpec(((c0.shape[0] if first else 1),) + tuple(c0.shape[1:]), lambda h: (l, 0, 0, 0))
    ins = [sT, c0, s0, *prm]
    specs = [_const(sT.shape), c_in, s_in] + [_const(p.shape) for p in prm]
    kern, pspecs, aliases = _finish_state_call(functools.partial(_ssd_t_kernel, T=T), len(ins), first, prevs)
    ins, specs = ins + prevs, specs + pspecs
    return pl.pallas_call(
        kern,
        grid=(SSD_H,),
        in_specs=specs,
        out_specs=[_const((SSD_WIDTH, N)), s_out, c_out],
        out_shape=[jax.ShapeDtypeStruct((SSD_WIDTH, N), BF), jax.ShapeDtypeStruct(s0.shape, F32),
                   jax.ShapeDtypeStruct(c0.shape, F32)],
        scratch_shapes=[pltpu.VMEM((SSD_CONV_DIM, (SSD_CONV_W - 1) * LANE), F32),
                        pltpu.VMEM((SSD_WIDTH, N), F32), pltpu.VMEM((1, N), F32),
                        pltpu.VMEM((SSD_N, N), F32), pltpu.VMEM((SSD_N, N), F32), pltpu.VMEM((SSD_P, N), F32)],
        input_output_aliases=aliases,
        compiler_params=_cp(1),
        name="ssd_t",
    )(*ins)


def _outproj_t_kernel(x_ref, g_ref, og_ref, or_ref, os_ref, w_ref, lg_ref, lb_ref, o_ref):
    nt, nb, D = x_ref.shape
    for t in range(nt):
        cols = slice(t * nb, (t + 1) * nb)
        mix = (_dot_tn(og_ref[:, cols], w_ref[0:GLA_WIDTH, :])
               + _dot_tn(or_ref[:, cols], w_ref[GLA_WIDTH:GLA_WIDTH + RET_WIDTH, :])
               + _dot_tn(os_ref[:, cols], w_ref[GLA_WIDTH + RET_WIDTH:D, :]))
        y = ALPHA * x_ref[t] + g_ref[0] * mix
        o_ref[t] = _layer_norm(y, lg_ref[0], lb_ref[0])


def _outproj_t_call(xt, g1, ogT, orT, osT, w_out, ln_g, ln_b):
    T, B, D = xt.shape
    nt = 4
    cmap = lambda i: (0, i)
    return pl.pallas_call(
        _outproj_t_kernel,
        grid=(T // nt,),
        in_specs=[pl.BlockSpec((nt, B, D), lambda i: (i, 0, 0)),
                  pl.BlockSpec((1, B, D), lambda i: (0, 0, 0)),
                  pl.BlockSpec((GLA_WIDTH, nt * B), cmap),
                  pl.BlockSpec((RET_WIDTH, nt * B), cmap),
                  pl.BlockSpec((SSD_WIDTH, nt * B), cmap),
                  _const((D, D)), _const((1, 1, D)), _const((1, 1, D))],
        out_specs=pl.BlockSpec((nt, B, D), lambda i: (i, 0, 0)),
        out_shape=jax.ShapeDtypeStruct((T, B, D), F32),
        compiler_params=_cp(1),
        name="out_proj_ln_t",
    )(xt, g1, ogT, orT, osT, w_out, ln_g.reshape(1, 1, D), ln_b.reshape(1, 1, D))


ROUTE_OFF = 8


def _moe_route_t(lt):
    neg = jnp.float32(-jnp.inf)
    row8 = lax.broadcasted_iota(jnp.int32, (8, 1), 0)
    lg = jnp.where(row8 < MOE_GROUPS, lt[0:8, :], neg)
    mg = jnp.max(lg, axis=0, keepdims=True)
    gsel = jnp.min(jnp.where(lg == mg, row8, 8), axis=0, keepdims=True)
    g_gate = 1.0 / jnp.sum(jnp.exp(lg - mg), axis=0, keepdims=True)
    rowe = lax.broadcasted_iota(jnp.int32, (MOE_EXPERTS, 1), 0)
    le = jnp.where((rowe // MOE_PER_GROUP) == gsel, lt[ROUTE_OFF:ROUTE_OFF + MOE_EXPERTS, :], neg)
    m1 = jnp.max(le, axis=0, keepdims=True)
    i1 = jnp.min(jnp.where(le == m1, rowe, MOE_EXPERTS), axis=0, keepdims=True)
    le2 = jnp.where(rowe == i1, neg, le)
    m2 = jnp.max(le2, axis=0, keepdims=True)
    i2 = jnp.min(jnp.where(le2 == m2, rowe, MOE_EXPERTS), axis=0, keepdims=True)
    e2 = jnp.exp(m2 - m1)
    w1 = g_gate / (1.0 + e2)
    w2 = g_gate * e2 / (1.0 + e2)
    comb = jnp.where(rowe == i1, w1, jnp.where(rowe == i2, w2, 0.0))
    cg = comb[0:4, :]
    for g in range(1, MOE_GROUPS):
        cg = cg + comb[g * MOE_PER_GROUP:(g + 1) * MOE_PER_GROUP, :]
    return gsel, cg, comb


MOE_SUB = 256
MOE_BLK = 16
MOE_ROWS = 256
MOE_NPS = MOE_SUB + MOE_GROUPS * MOE_BLK
assert MOE_SUB <= MOE_ROWS


def _moe_kernel(x_ref, sc_ref, sh_ref, g_ref, wr_ref, br_ref, us_ref, w1_ref, w3_ref, w2_ref, lg_ref, lb_ref,
                o_ref, hb_ref, cwb_ref, hp_ref, cwp_ref, yp_ref, pos_ref,
                cgrp_ref, fill_ref, cur_ref, na_ref, nb_ref, so_ref, nfa_ref, dsa_ref, dsb_ref, *, n_steps):
    bB, bT, D = x_ref.shape
    R = bB * bT
    n_q = R // MOE_SUB
    s = pl.program_id(1)
    x = x_ref[...]
    row8 = lax.broadcasted_iota(jnp.int32, (8, 1), 0)
    slot = lax.broadcasted_iota(jnp.int32, (MOE_NPS, 1), 0).astype(F32)

    @pl.when((pl.program_id(0) == 0) & (s == 0))
    def _():
        hb_ref[...] = jnp.zeros_like(hb_ref)
        cwb_ref[...] = jnp.zeros_like(cwb_ref)
        yp_ref[...] = jnp.zeros_like(yp_ref)

    @pl.when(s == 0)
    def _():
        na_ref[0] = 0
        for g in range(MOE_GROUPS):
            cur_ref[g] = -1
            fill_ref[g] = 0

    @pl.when(s < n_steps)
    def _():
        h = (x * (1.0 + sc_ref[...]) + sh_ref[...]).reshape(R, D)
        segs, offs = [], []
        for q in range(n_q):
            u = s * n_q + q
            hq = h[q * MOE_SUB:(q + 1) * MOE_SUB, :].astype(BF)
            gsel, cg, _ = _moe_route_t(_dot_nt(wr_ref[...], hq) + br_ref[...])
            onehot = jnp.where(row8 == gsel, 1.0, 0.0)
            rank = _dot(onehot.astype(BF), us_ref[...])
            cnt = jnp.sum(onehot, axis=1, keepdims=True)
            seg = jnp.ceil(cnt * (1.0 / MOE_BLK)) * MOE_BLK
            off = jnp.zeros((8, 1), F32)
            for g in range(1, MOE_GROUPS):
                off = off + jnp.where(row8 >= g, seg[g - 1:g, :], 0.0)
            pos = jnp.sum(onehot * (off + rank), axis=0, keepdims=True)
            pos_ref[u] = jnp.broadcast_to(pos, (8, MOE_SUB))
            perm = jnp.where(slot == pos, 1.0, 0.0).astype(BF)
            hp_ref[q] = _dot(perm, hq).astype(BF)
            cg8 = jnp.concatenate([cg, jnp.zeros((4, MOE_SUB), F32)], axis=0)
            cg_hi = cg8.astype(BF)
            cg_lo = (cg8 - cg_hi.astype(F32)).astype(BF)
            cwp_ref[q] = _dot_nt(perm, cg_hi) + _dot_nt(perm, cg_lo)
            segs.append(seg)
            offs.append(off)
        for q in range(n_q):
            u = s * n_q + q
            for g in range(MOE_GROUPS):
                so = offs[q][g, 0].astype(jnp.int32)
                nb = (segs[q][g, 0] * (1.0 / MOE_BLK)).astype(jnp.int32)
                f = fill_ref[g]
                c = cur_ref[g]
                na = na_ref[0]
                room = jnp.where(c < 0, 0, (MOE_ROWS - f) // MOE_BLK)
                n_a = jnp.minimum(nb, room)
                n_b = nb - n_a
                base_a = c * MOE_ROWS + f
                base_b = na * MOE_ROWS
                idx = u * MOE_GROUPS + g
                so_ref[idx] = so
                nb_ref[idx] = nb
                nfa_ref[idx] = n_a
                dsa_ref[idx] = base_a
                dsb_ref[idx] = base_b

                def put(k, carry, so=so, q=q, n_a=n_a, base_a=base_a, base_b=base_b):
                    dst = pl.multiple_of(jnp.where(k < n_a, base_a + k * MOE_BLK, base_b + (k - n_a) * MOE_BLK), MOE_BLK)
                    src = pl.multiple_of(so + k * MOE_BLK, MOE_BLK)
                    hb_ref[pl.ds(dst, MOE_BLK), :] = hp_ref[q, pl.ds(src, MOE_BLK), :]
                    cwb_ref[pl.ds(dst, MOE_BLK), :] = cwp_ref[q, pl.ds(src, MOE_BLK), :]
                    return carry

                lax.fori_loop(0, nb, put, 0)

                @pl.when(n_b > 0)
                def _(g=g, na=na, n_b=n_b):
                    cgrp_ref[na] = g
                    na_ref[0] = na + 1
                    cur_ref[g] = na
                    fill_ref[g] = n_b * MOE_BLK

                @pl.when(n_b == 0)
                def _(g=g, f=f, n_a=n_a):
                    fill_ref[g] = f + n_a * MOE_BLK

    @pl.when(s == n_steps - 1)
    def _():
        def chunk(c, carry):
            g = cgrp_ref[c]
            start = pl.multiple_of(c * MOE_ROWS, MOE_ROWS)
            hc = hb_ref[pl.ds(start, MOE_ROWS), :]
            cw = cwb_ref[pl.ds(start, MOE_ROWS), :]
            hids = []
            for j in range(MOE_PER_GROUP):
                e = g * MOE_PER_GROUP + j
                hid = _silu(_dot(hc, w1_ref[e])) * _dot(hc, w3_ref[e]) * cw[:, j:j + 1]
                hids.append(hid.astype(BF))
            w2g = w2_ref[pl.ds(g * MOE_PER_GROUP, MOE_PER_GROUP)].reshape(MOE_PER_GROUP * MOE_FF, D)
            hb_ref[pl.ds(start, MOE_ROWS), :] = _dot(jnp.concatenate(hids, axis=1), w2g).astype(BF)
            return carry

        lax.fori_loop(0, na_ref[0], chunk, 0)

    @pl.when(s >= n_steps)
    def _():
        for q in range(n_q):
            u = (s - n_steps) * n_q + q
            for g in range(MOE_GROUPS):
                idx = u * MOE_GROUPS + g
                so, n_a, base_a, base_b = so_ref[idx], nfa_ref[idx], dsa_ref[idx], dsb_ref[idx]

                def take(k, carry, so=so, q=q, n_a=n_a, base_a=base_a, base_b=base_b):
                    src = pl.multiple_of(jnp.where(k < n_a, base_a + k * MOE_BLK, base_b + (k - n_a) * MOE_BLK), MOE_BLK)
                    dst = pl.multiple_of(so + k * MOE_BLK, MOE_BLK)
                    yp_ref[q, pl.ds(dst, MOE_BLK), :] = hb_ref[pl.ds(src, MOE_BLK), :]
                    return carry

                lax.fori_loop(0, nb_ref[idx], take, 0)
        ys = []
        for q in range(n_q):
            u = (s - n_steps) * n_q + q
            perm = jnp.where(slot == pos_ref[u][0:1, :], 1.0, 0.0).astype(BF)
            ys.append(_dot_tn(perm, yp_ref[q]))
        y = jnp.concatenate(ys, axis=0)
        z = ALPHA * x + g_ref[...] * y.reshape(bB, bT, D)
        o_ref[...] = _layer_norm(z, lg_ref[...], lb_ref[...])


def _moe_call(x3, sc, sh, g2, wr, br, w1, w3, w2, l, ln_g, ln_b):
    B, T, D = x3.shape
    bB, bT = _tok_tiles(B, T)
    R = bB * bT
    if bB == 1:
        spp = 2 if B % 2 == 0 else 1
        nT = T // bT
        n_pools, n_steps = B // spp, spp * nT
        xmap = lambda p, s: (p * spp + (s % n_steps) // nT, (s % n_steps) % nT, 0)
        omap = lambda p, s: (p * spp + jnp.maximum(s - n_steps, 0) // nT, jnp.maximum(s - n_steps, 0) % nT, 0)
        mmap = lambda p, s: (p * spp + (s % n_steps) // nT, 0, 0)
        mshape = (1, 1, D)
    else:
        n_pools, n_steps = 1, B // bB
        xmap = lambda p, s: (s % n_steps, 0, 0)
        omap = lambda p, s: (jnp.maximum(s - n_steps, 0), 0, 0)
        mmap = lambda p, s: (0, 0, 0)
        mshape = (1, bT, D)
    n_sub = n_steps * (R // MOE_SUB)
    n_chunks = pl.cdiv(n_sub * (MOE_SUB + MOE_GROUPS * (MOE_BLK - 1)), MOE_ROWS) + MOE_GROUPS
    us = jnp.asarray(np.triu(np.ones((MOE_SUB, MOE_SUB), np.float32), 1), BF)
    smem = lambda n: pltpu.SMEM((n,), jnp.int32)
    return pl.pallas_call(
        functools.partial(_moe_kernel, n_steps=n_steps),
        grid=(n_pools, 2 * n_steps),
        in_specs=[pl.BlockSpec((bB, bT, D), xmap),
                  pl.BlockSpec(mshape, mmap), pl.BlockSpec(mshape, mmap), pl.BlockSpec(mshape, mmap),
                  _const(wr.shape), _const(br.shape), _const(us.shape),
                  _resident_layer(w1.shape, l), _resident_layer(w3.shape, l), _resident_layer(w2.shape, l),
                  _const((1, 1, D)), _const((1, 1, D))],
        out_specs=pl.BlockSpec((bB, bT, D), omap),
        out_shape=jax.ShapeDtypeStruct((B, T, D), F32),
        scratch_shapes=[pltpu.VMEM((n_chunks * MOE_ROWS, D), BF), pltpu.VMEM((n_chunks * MOE_ROWS, 8), F32),
                        pltpu.VMEM((R // MOE_SUB, MOE_NPS, D), BF), pltpu.VMEM((R // MOE_SUB, MOE_NPS, 8), F32),
                        pltpu.VMEM((R // MOE_SUB, MOE_NPS, D), BF),
                        pltpu.VMEM((n_sub, 8, MOE_SUB), F32),
                        smem(n_chunks), smem(MOE_GROUPS), smem(MOE_GROUPS), smem(1),
                        *[smem(n_sub * MOE_GROUPS) for _ in range(5)]],
        compiler_params=_cp(2),
        name="moe_ln",
    )(x3, sc, sh, g2, wr, br, us, w1, w3, w2, ln_g.reshape(1, 1, D), ln_b.reshape(1, 1, D))


def _router_params(w_group, b_group, w_expert, b_expert):
    wr = jnp.zeros((LANE, D_MODEL), F32).at[:MOE_GROUPS].set(w_group.T)
    wr = wr.at[ROUTE_OFF:ROUTE_OFF + MOE_EXPERTS].set(w_expert.T)
    br = jnp.zeros((LANE, 1), F32).at[:MOE_GROUPS, 0].set(b_group).at[ROUTE_OFF:ROUTE_OFF + MOE_EXPERTS, 0].set(b_expert)
    return wr.astype(BF), br


def kernel(x_prompt, x_sample, c_prompt, c_sample, state_gla, state_ret, state_ssd, state_conv, w_ada, b_ada, w_in, gla_w_gate, gla_b_gate, gla_norm, ret_norm, ssd_conv_w, ssd_conv_b, ssd_dt_bias, ssd_a_log, ssd_d, ssd_norm, w_out, ln1_g, ln1_b, moe_w_group, moe_b_group, moe_w_expert, moe_b_expert, moe_w1, moe_w3, moe_w2, ln2_g, ln2_b):
    Bp, Tp, D = x_prompt.shape
    Bs, Ts, _ = x_sample.shape
    w_in_t = jnp.swapaxes(w_in, 1, 2)
    w_out_b = w_out.astype(BF)
    w1_b, w3_b, w2_b = moe_w1.astype(BF), moe_w3.astype(BF), moe_w2.astype(BF)

    mod = _mod_call(jnp.concatenate([c_prompt, c_sample], axis=0), w_ada, b_ada)

    def moe(x, sc2, sh2, g2, l):
        wr, br = _router_params(moe_w_group[l], moe_b_group[l], moe_w_expert[l], moe_b_expert[l])
        return _moe_call(x, sc2, sh2, g2, wr, br, w1_b, w3_b, w2_b, l, ln2_g[l], ln2_b[l])

    x = x_prompt
    new = [[], [], [], []]
    for l in range(DEPTH):
        sh1, sc1, g1, sh2, sc2, g2 = (mod[l, :Bp, None, i * D:(i + 1) * D] for i in range(6))
        gin, rin, sin_ = _inproj_call(x, sc1, sh1, w_in_t, l)
        og, s_gla = _gla_prompt_call(gin, Bp, Tp, gla_w_gate[l], gla_b_gate[l], gla_norm[l])
        orr, s_ret = _ret_prompt_call(rin, Bp, Tp, ret_norm[l])
        x, s_ssd, s_conv = _ssd_prompt_call(sin_, x, g1, og, orr, w_out_b[l], ln1_g[l], ln1_b[l], ssd_conv_w[l],
                                            ssd_conv_b[l], ssd_dt_bias[l], ssd_a_log[l], ssd_d[l], ssd_norm[l])
        x = moe(x, sc2, sh2, g2, l)
        for acc, s in zip(new, (s_gla, s_ret, s_ssd, s_conv)):
            acc.append(s)
    y_p = x
    gla_p, ret_p, ssd_p, conv_p = (jnp.stack(a) for a in new)

    x = jnp.swapaxes(x_sample, 0, 1)
    sg = jnp.transpose(state_gla, (0, 2, 3, 4, 1))
    sr = jnp.transpose(state_ret, (0, 2, 3, 4, 1))
    ss = jnp.transpose(state_ssd, (0, 2, 3, 4, 1))
    cv = jnp.transpose(state_conv, (0, 2, 1, 3))
    gla_n = ret_n = ssd_n = conv_n = None
    for l in range(DEPTH):
        sh1, sc1, g1, sh2, sc2, g2 = (mod[l, None, Bp:, i * D:(i + 1) * D] for i in range(6))
        gT, rT, sT = _inproj_t_call(x, sc1, sh1, w_in_t, l)
        ogT, gla_n = _gla_t_call(gT, sg, gla_n, l, Ts, gla_w_gate[l], gla_b_gate[l], gla_norm[l])
        orT, ret_n = _ret_t_call(rT, sr, ret_n, l, Ts, ret_norm[l])
        osT, ssd_n, conv_n = _ssd_t_call(sT, cv, ss, ssd_n, conv_n, l, Ts, ssd_conv_w[l], ssd_conv_b[l],
                                         ssd_dt_bias[l], ssd_a_log[l], ssd_d[l], ssd_norm[l])
        x = _outproj_t_call(x, g1, ogT, orT, osT, w_out_b[l], ln1_g[l], ln1_b[l])
        x = moe(x, sc2, sh2, g2, l)
    y_s = jnp.swapaxes(x, 0, 1)
    gla_s = jnp.transpose(gla_n, (0, 4, 1, 2, 3))
    ret_s = jnp.transpose(ret_n, (0, 4, 1, 2, 3))
    ssd_s = jnp.transpose(ssd_n, (0, 4, 1, 2, 3))
    conv_s = jnp.transpose(conv_n, (0, 2, 1, 3))
    return (y_p, y_s, gla_p, ret_p, ssd_p, conv_p, gla_s, ret_s, ssd_s, conv_s)
```

```python
import functools

import numpy as np
import jax
import jax.numpy as jnp
from jax import lax
from jax.experimental import pallas as pl
from jax.experimental.pallas import tpu as pltpu

F32 = jnp.float32
BF = jnp.bfloat16

D_MODEL = 1024
DEPTH = 2
PAST_LEN = 16384
GLA_H, GLA_DK, GLA_DV = 4, 32, 64
GLA_WIDTH = GLA_H * GLA_DV
GLA_GATE_RANK = 16
GLA_GATE_TEMP = 16.0
GLA_CHUNK = 16
RET_H, RET_DK, RET_DV = 4, 64, 64
RET_WIDTH = RET_H * RET_DV
ROPE_BASE = 10000.0
SSD_H, SSD_P, SSD_G, SSD_N = 8, 64, 2, 64
SSD_WIDTH = SSD_H * SSD_P
SSD_CONV_W = 4
SSD_CONV_DIM = SSD_WIDTH + 2 * SSD_G * SSD_N
MOE_GROUPS, MOE_PER_GROUP = 4, 4
MOE_EXPERTS = MOE_GROUPS * MOE_PER_GROUP
MOE_FF = 256
ALPHA = (2 * DEPTH) ** 0.25
EPS = 1e-5

LANE = 128
GLA_IN_W = 128 + 128 + 256 + LANE + 256
RET_IN_W = 4 * 256
SSD_IN_W = 512 + SSD_CONV_DIM + LANE
IN_W = GLA_IN_W + RET_IN_W + SSD_IN_W
VMEM_LIMIT = 56 * 1024 * 1024


def _cp(n_axes, vmem=VMEM_LIMIT):
    return pltpu.CompilerParams(dimension_semantics=("arbitrary",) * n_axes, vmem_limit_bytes=vmem)


def _dot(a, b):
    return jnp.dot(a, b, preferred_element_type=F32)


def _dot_nt(a, b):
    return lax.dot_general(a, b, (((1,), (1,)), ((), ())), preferred_element_type=F32)


def _dot_tn(a, b):
    return lax.dot_general(a, b, (((0,), (0,)), ((), ())), preferred_element_type=F32)


def _split3(x):
    hi = x.astype(BF)
    r = x - hi.astype(F32)
    mid = r.astype(BF)
    lo = (r - mid.astype(F32)).astype(BF)
    return hi, mid, lo


def _dot_x3(x, e):
    hi, mid, lo = _split3(x)
    return _dot(hi, e) + (_dot(mid, e) + _dot(lo, e))


def _dot_x2(x, e):
    hi = x.astype(BF)
    lo = (x - hi.astype(F32)).astype(BF)
    return _dot(hi, e) + _dot(lo, e)


def _dot_3x(e, x):
    hi, mid, lo = _split3(x)
    return _dot(e, hi) + (_dot(e, mid) + _dot(e, lo))


def _sigmoid(x):
    return 1.0 / (1.0 + jnp.exp(-x))


def _silu(x):
    return x * _sigmoid(x)


def _log_sigmoid(x):
    return jnp.minimum(x, 0.0) - jnp.log(1.0 + jnp.exp(-jnp.abs(x)))


def _softplus(x):
    return jnp.maximum(x, 0.0) + jnp.log(1.0 + jnp.exp(-jnp.abs(x)))


def _layer_norm(x, g, b):
    mu = jnp.mean(x, axis=-1, keepdims=True)
    d = x - mu
    var = jnp.mean(d * d, axis=-1, keepdims=True)
    return d * lax.rsqrt(var + EPS) * g + b


def _const(shape):
    return pl.BlockSpec(shape, lambda *_: (0,) * len(shape))


def _resident_layer(shape, l):
    return pl.BlockSpec((None,) + tuple(shape[1:]), lambda *_: (l,) + (0,) * (len(shape) - 1),
                        pipeline_mode=pl.Buffered(1))


def _mod_kernel(c_ref, w_ref, b_ref, o_ref):
    s = _silu(c_ref[...]).astype(BF)
    o_ref[0] = _dot(s, w_ref[0].astype(BF)) + b_ref[0]


def _mod_call(c_all, w_ada, b_ada):
    R = c_all.shape[0]
    tn = 1536
    return pl.pallas_call(
        _mod_kernel,
        grid=(DEPTH, 6 * D_MODEL // tn),
        in_specs=[pl.BlockSpec((R, D_MODEL), lambda l, j: (0, 0)),
                  pl.BlockSpec((1, D_MODEL, tn), lambda l, j: (l, 0, j)),
                  pl.BlockSpec((1, 1, tn), lambda l, j: (l, 0, j))],
        out_specs=pl.BlockSpec((1, R, tn), lambda l, j: (l, 0, j)),
        out_shape=jax.ShapeDtypeStruct((DEPTH, R, 6 * D_MODEL), F32),
        compiler_params=_cp(2),
        name="ada_mod",
    )(c_all, w_ada, b_ada.reshape(DEPTH, 1, 6 * D_MODEL))


N_IN = 3096
N_GA = 128 + 128 + 256
N_DT = N_IN - SSD_H


def _inproj_kernel(x_ref, sc_ref, sh_ref, wt_ref, og_ref, or_ref, os_ref, w_ref):
    bB, bT, D = x_ref.shape

    @pl.when((pl.program_id(0) == 0) & (pl.program_id(1) == 0))
    def _():
        lane = lax.broadcasted_iota(jnp.int32, (1, LANE), 1)
        for j in range(N_GA // LANE):
            w_ref[:, j * LANE:(j + 1) * LANE] = wt_ref[j * LANE:(j + 1) * LANE, :].T.astype(BF)
        ga = wt_ref[N_GA:N_GA + LANE, :].T
        w_ref[:, N_GA:N_GA + LANE] = jnp.where(lane < GLA_GATE_RANK, ga, 0.0).astype(BF)
        src0, dst0 = N_GA + GLA_GATE_RANK, N_GA + LANE
        for j in range((N_DT - src0) // LANE):
            w_ref[:, dst0 + j * LANE:dst0 + (j + 1) * LANE] = \
                wt_ref[src0 + j * LANE:src0 + (j + 1) * LANE, :].T.astype(BF)
        dt = pltpu.roll(wt_ref[N_IN - LANE:N_IN, :].T, SSD_H, 1)
        w_ref[:, IN_W - LANE:IN_W] = jnp.where(lane < SSD_H, dt, 0.0).astype(BF)

    h = x_ref[...] * (1.0 + sc_ref[...]) + sh_ref[...]
    hb = h.reshape(bB * bT, D).astype(BF)
    cut = 2 * GLA_IN_W
    pa = _dot(hb, w_ref[:, 0:cut])
    pb = _dot(hb, w_ref[:, cut:IN_W])
    og_ref[...] = pa[:, 0:GLA_IN_W]
    or_ref[:, 0:cut - GLA_IN_W] = pa[:, GLA_IN_W:cut]
    or_ref[:, cut - GLA_IN_W:RET_IN_W] = pb[:, 0:GLA_IN_W + RET_IN_W - cut]
    os_ref[...] = pb[:, GLA_IN_W + RET_IN_W - cut:IN_W - cut]


def _tok_tiles(B, T):
    if T >= 512:
        return 1, 512
    return 512 // T, T


def _inproj_call(x3, sc, sh, wt, l):
    B, T, D = x3.shape
    bB, bT = _tok_tiles(B, T)
    nT = T // bT
    R = bB * bT
    N = B * T
    xmap = lambda i, j: (i, j, 0)
    mmap = lambda i, j: (i, 0, 0)
    omap = lambda i, j: (i * nT + j, 0)
    return pl.pallas_call(
        _inproj_kernel,
        grid=(B // bB, nT),
        in_specs=[pl.BlockSpec((bB, bT, D), xmap),
                  pl.BlockSpec((bB, 1, D), mmap),
                  pl.BlockSpec((bB, 1, D), mmap),
                  _resident_layer(wt.shape, l)],
        out_specs=[pl.BlockSpec((R, GLA_IN_W), omap),
                   pl.BlockSpec((R, RET_IN_W), omap),
                   pl.BlockSpec((R, SSD_IN_W), omap)],
        out_shape=[jax.ShapeDtypeStruct((N, GLA_IN_W), F32),
                   jax.ShapeDtypeStruct((N, RET_IN_W), F32),
                   jax.ShapeDtypeStruct((N, SSD_IN_W), F32)],
        scratch_shapes=[pltpu.VMEM((D, IN_W), BF)],
        compiler_params=_cp(2),
        name="in_proj",
    )(x3, sc, sh, wt)


def _head_block_mask(rows_per, cols_per, n):
    r = np.arange(rows_per * n)[:, None] // rows_per
    c = np.arange(cols_per * n)[None, :] // cols_per
    return (r == c).astype(np.float32)


def _block_tril(n, c):
    i = np.arange(n)[:, None]
    j = np.arange(n)[None, :]
    return ((i // c == j // c) & (j <= i)).astype(np.float32)


def _gla_front(x_ref, wg_ref, bg_ref, L_ref):
    q = x_ref[:, 0:128] * (GLA_DK ** -0.5)
    k = x_ref[:, 128:256]
    v = x_ref[:, 256:512]
    ga = x_ref[:, 512:640]
    r = x_ref[:, 640:896]
    gate = _dot(ga.astype(BF), wg_ref[...]) + bg_ref[...]
    la = _log_sigmoid(gate) * (1.0 / GLA_GATE_TEMP)
    n = L_ref.shape[0]
    g = jnp.concatenate([_dot_3x(L_ref[...], la[i:i + n, :]) for i in range(0, la.shape[0], n)], axis=0)
    return q, k, v, r, g


def _gla_intra(q, g, kp_ref, gp_ref, vp_ref, E_ref, c):
    TT = q.shape[0]
    PAD = kp_ref.shape[0] - TT
    pos = lax.broadcasted_iota(jnp.int32, (TT, 1), 0) & (c - 1)
    o = jnp.zeros((TT, 2 * LANE), F32)
    for s in range(min(c, 8)):
        ks = kp_ref[pl.ds(PAD - s, TT), :]
        gs = gp_ref[pl.ds(PAD - s, TT), :]
        vs = vp_ref[pl.ds(PAD - s, TT), :]
        w = jnp.where(pos >= s, q * ks * jnp.exp(g - gs), 0.0)
        o = o + _dot(w.astype(BF), E_ref[...]) * vs
    if c <= 8:
        return o
    assert c == 16
    nc = TT // c

    def upper(x):
        return x.reshape(nc, 2, 8, x.shape[-1])[:, 1].reshape(nc * 8, x.shape[-1])

    qu, gu = upper(q), upper(g)
    posu = lax.broadcasted_iota(jnp.int32, (nc * 8, 1), 0) & 7
    ou = jnp.zeros((nc * 8, 2 * LANE), F32)
    for s in range(8, c):
        ks = upper(kp_ref[pl.ds(PAD - s, TT), :])
        gs = upper(gp_ref[pl.ds(PAD - s, TT), :])
        vs = upper(vp_ref[pl.ds(PAD - s, TT), :])
        w = jnp.where(posu >= s - 8, qu * ks * jnp.exp(gu - gs), 0.0)
        ou = ou + _dot(w.astype(BF), E_ref[...]) * vs
    ou = ou.reshape(nc, 1, 8, 2 * LANE)
    return o + jnp.concatenate([jnp.zeros_like(ou), ou], axis=1).reshape(TT, 2 * LANE)


def _gla_norm_gate(o, r, nw_ref, EA_ref):
    ms = _dot_x3(o * o, EA_ref[...])
    return o * lax.rsqrt(ms + EPS) * nw_ref[...] * _silu(r)


def _gla_prompt_kernel(x_ref, wg_ref, bg_ref, nw_ref, L_ref, E_ref, EA_ref, M_ref,
                       o_ref, sfin_ref, st_ref, kp_ref, gp_ref, vp_ref, oi_ref, u_ref, sb_ref, *, c):
    TT = x_ref.shape[0]
    nc = TT // c
    PAD = kp_ref.shape[0] - TT
    t = pl.program_id(1)

    @pl.when(t == 0)
    def _():
        st_ref[...] = jnp.zeros_like(st_ref)

    q, k, v, r, g = _gla_front(x_ref, wg_ref, bg_ref, L_ref)
    kp_ref[0:PAD, :] = jnp.zeros((PAD, LANE), F32)
    gp_ref[0:PAD, :] = jnp.zeros((PAD, LANE), F32)
    vp_ref[0:PAD, :] = jnp.zeros((PAD, 2 * LANE), F32)
    kp_ref[PAD:PAD + TT, :] = k
    gp_ref[PAD:PAD + TT, :] = g
    vp_ref[PAD:PAD + TT, :] = v
    o = _gla_intra(q, g, kp_ref, gp_ref, vp_ref, E_ref, c)

    M = M_ref[...]
    gl_all = gp_ref[pl.ds(PAD + c - 1, nc, stride=c), :]
    for n in range(nc):
        lo = n * c
        ke = (k[lo:lo + c, :] * jnp.exp(gl_all[n:n + 1, :] - g[lo:lo + c, :])).astype(BF)
        u_ref[n] = _dot_tn(ke, v[lo:lo + c, :].astype(BF)) * M
    a_cols = jnp.concatenate([jnp.exp(gl_all), jnp.zeros((LANE - nc, LANE), F32)], axis=0).T
    S = st_ref[...]
    for n in range(nc):
        sb_ref[n] = S.astype(BF)
        S = a_cols[:, n:n + 1] * S + u_ref[n]
    st_ref[...] = S
    qe = (q * jnp.exp(g)).astype(BF)
    for n in range(nc):
        lo = n * c
        oi_ref[lo:lo + c, :] = _dot(qe[lo:lo + c, :], sb_ref[n])
    o = o + oi_ref[...]
    o_ref[...] = _gla_norm_gate(o, r, nw_ref, EA_ref).astype(o_ref.dtype)

    @pl.when(t == pl.num_programs(1) - 1)
    def _():
        for h in range(GLA_H):
            sfin_ref[0, h] = S[h * GLA_DK:(h + 1) * GLA_DK, h * GLA_DV:(h + 1) * GLA_DV]


def _gla_tables(TT, c):
    L = jnp.asarray(_block_tril(TT, c), BF)
    E = jnp.asarray(_head_block_mask(GLA_DK, GLA_DV, GLA_H), BF)
    EA = jnp.asarray(_head_block_mask(GLA_DV, GLA_DV, GLA_H) / GLA_DV, BF)
    M = jnp.asarray(_head_block_mask(GLA_DK, GLA_DV, GLA_H), F32)
    return L, E, EA, M


def _gla_params(w_gate, b_gate, norm_w):
    wg = jnp.zeros((LANE, GLA_H * GLA_DK), F32).at[:GLA_GATE_RANK].set(w_gate).astype(BF)
    return wg, b_gate.reshape(1, -1), norm_w.reshape(1, -1)


def _gla_prompt_call(gin, B, T, w_gate, b_gate, norm_w):
    TT, c = (1024 if T % 1024 == 0 else 512), GLA_CHUNK
    nT = T // TT
    L, E, EA, M = _gla_tables(min(TT, 256), c)
    wg, bg, nw = _gla_params(w_gate, b_gate, norm_w)
    PAD = 16
    return pl.pallas_call(
        functools.partial(_gla_prompt_kernel, c=c),
        grid=(B, nT),
        in_specs=[pl.BlockSpec((TT, GLA_IN_W), lambda b, t: (b * nT + t, 0)),
                  _const(wg.shape), _const(bg.shape), _const(nw.shape),
                  _const(L.shape), _const(E.shape), _const(EA.shape), _const(M.shape)],
        out_specs=[pl.BlockSpec((TT, GLA_WIDTH), lambda b, t: (b * nT + t, 0)),
                   pl.BlockSpec((1, GLA_H, GLA_DK, GLA_DV), lambda b, t: (b, 0, 0, 0))],
        out_shape=[jax.ShapeDtypeStruct((B * T, GLA_WIDTH), BF),
                   jax.ShapeDtypeStruct((B, GLA_H, GLA_DK, GLA_DV), F32)],
        scratch_shapes=[pltpu.VMEM((GLA_H * GLA_DK, GLA_H * GLA_DV), F32),
                        pltpu.VMEM((TT + PAD, LANE), F32),
                        pltpu.VMEM((TT + PAD, LANE), F32),
                        pltpu.VMEM((TT + PAD, 2 * LANE), F32),
                        pltpu.VMEM((TT, 2 * LANE), F32),
                        pltpu.VMEM((TT // c, GLA_H * GLA_DK, GLA_H * GLA_DV), F32),
                        pltpu.VMEM((TT // c, GLA_H * GLA_DK, GLA_H * GLA_DV), BF)],
        compiler_params=_cp(2),
        name="gla_prompt",
    )(gin, wg, bg, nw, L, E, EA, M)


def _rope(x, cos, sin_signed):
    lane = lax.broadcasted_iota(jnp.int32, (1, LANE), 1)
    first_half = (lane & (RET_DK - 1)) < RET_DK // 2
    out = []
    for p in range(2):
        xs = x[:, p * LANE:(p + 1) * LANE]
        up = pltpu.roll(xs, LANE - RET_DK // 2, 1)
        dn = pltpu.roll(xs, RET_DK // 2, 1)
        out.append(xs * cos + jnp.where(first_half, up, dn) * sin_signed)
    return jnp.concatenate(out, axis=1)


def _ret_front(x_ref, cos_ref, sin_ref, rows=slice(None)):
    cos, sin = cos_ref[rows, :], sin_ref[rows, :]
    q = _rope(x_ref[rows, 0:256], cos, sin)
    k = _rope(x_ref[rows, 256:512], cos, sin) * (RET_DK ** -0.5)
    v = x_ref[rows, 512:768]
    rg = x_ref[rows, 768:1024]
    return q, k, v, rg


def _ret_intra(q, k, v, D_ref):
    lane = lax.broadcasted_iota(jnp.int32, (1, RET_WIDTH), 1)
    kb = k.astype(BF)
    o = jnp.zeros(q.shape, F32)
    for h in range(RET_H):
        hm = (lane // RET_DK) == h
        s = _dot_nt(jnp.where(hm, q, 0.0).astype(BF), kb)
        p = (s * D_ref[h]).astype(BF)
        o = o + _dot(p, jnp.where(hm, v, 0.0).astype(BF))
    return o


def _ret_norm_gate(o, rg, nw_ref, EA_ref):
    mu = _dot_x3(o, EA_ref[...])
    d = o - mu
    var = _dot_x3(d * d, EA_ref[...])
    return d * lax.rsqrt(var + EPS) * nw_ref[...] * _silu(rg)


def _ret_prompt_kernel(x_ref, cos_ref, sin_ref, D_ref, rd_ref, kd_ref, G_ref, M_ref, EA_ref, nw_ref,
                       o_ref, sfin_ref, st_ref):
    t = pl.program_id(1)

    @pl.when(t == 0)
    def _():
        st_ref[...] = jnp.zeros_like(st_ref)

    C = D_ref.shape[1]
    S = st_ref[...]
    for i in range(x_ref.shape[0] // C):
        rows = slice(i * C, (i + 1) * C)
        q, k, v, rg = _ret_front(x_ref, cos_ref, sin_ref, rows)
        o = _ret_intra(q, k, v, D_ref)
        o = o + _dot((q * rd_ref[...]).astype(BF), S.astype(BF))
        u = _dot_tn((k * kd_ref[...]).astype(BF), v.astype(BF))
        S = S * G_ref[...] + u * M_ref[...]
        o_ref[rows, :] = _ret_norm_gate(o, rg, nw_ref, EA_ref).astype(o_ref.dtype)
    st_ref[...] = S

    @pl.when(t == pl.num_programs(1) - 1)
    def _():
        for h in range(RET_H):
            sfin_ref[0, h] = S[h * RET_DK:(h + 1) * RET_DK, h * RET_DV:(h + 1) * RET_DV]


def _rope_tables(pos):
    half = RET_DK // 2
    inv = ROPE_BASE ** (-jnp.arange(half, dtype=F32) / half)
    ang = pos.astype(F32)[:, None] * inv[None, :]
    cos, sin = jnp.cos(ang), jnp.sin(ang)
    return jnp.tile(jnp.concatenate([cos, cos], 1), (1, 2)), jnp.tile(jnp.concatenate([-sin, sin], 1), (1, 2))


def _ret_log_gamma():
    return np.log(1.0 - 2.0 ** (-5.0 - np.arange(RET_H, dtype=np.float64)))


def _ret_prompt_call(rin, B, T, norm_w):
    C = 256
    TT = next(n * C for n in (8, 4, 1) if T % (n * C) == 0)
    nT = T // TT
    cos, sin = _rope_tables(jnp.arange(T, dtype=jnp.int32))
    lg = _ret_log_gamma()
    i = np.arange(C)
    dec = np.exp(lg[:, None, None] * (i[:, None] - i[None, :])[None]) * (i[:, None] >= i[None, :])[None]
    Dm = jnp.asarray(dec, F32)
    rd = jnp.asarray(np.repeat(np.exp(lg[None, :] * (i[:, None] + 1)), RET_DK, 1), F32)
    kd = jnp.asarray(np.repeat(np.exp(lg[None, :] * (C - 1 - i[:, None])), RET_DK, 1), F32)
    M = _head_block_mask(RET_DK, RET_DV, RET_H)
    G = jnp.asarray(M * np.repeat(np.exp(lg * C), RET_DK)[:, None], F32)
    M = jnp.asarray(M, F32)
    EA = jnp.asarray(_head_block_mask(RET_DV, RET_DV, RET_H) / RET_DV, BF)
    nw = norm_w.reshape(1, -1)
    return pl.pallas_call(
        _ret_prompt_kernel,
        grid=(B, nT),
        in_specs=[pl.BlockSpec((TT, RET_IN_W), lambda b, t: (b * nT + t, 0)),
                  pl.BlockSpec((TT, LANE), lambda b, t: (t, 0)),
                  pl.BlockSpec((TT, LANE), lambda b, t: (t, 0)),
                  _const(Dm.shape), _const(rd.shape), _const(kd.shape), _const(G.shape), _const(M.shape),
                  _const(EA.shape), _const(nw.shape)],
        out_specs=[pl.BlockSpec((TT, RET_WIDTH), lambda b, t: (b * nT + t, 0)),
                   pl.BlockSpec((1, RET_H, RET_DK, RET_DV), lambda b, t: (b, 0, 0, 0))],
        out_shape=[jax.ShapeDtypeStruct((B * T, RET_WIDTH), BF),
                   jax.ShapeDtypeStruct((B, RET_H, RET_DK, RET_DV), F32)],
        scratch_shapes=[pltpu.VMEM((RET_H * RET_DK, RET_H * RET_DV), F32)],
        compiler_params=_cp(2),
        name="ret_prompt",
    )(rin, cos, sin, Dm, rd, kd, G, M, EA, nw)


def _ssd_conv(xp_ref, cw_ref, cb_ref, TT):
    acc = cb_ref[...] + cw_ref[SSD_CONV_W - 1:SSD_CONV_W, :] * xp_ref[pl.ds(8, TT), :]
    for i in range(SSD_CONV_W - 1):
        acc = acc + cw_ref[i:i + 1, :] * xp_ref[pl.ds(8 - (SSD_CONV_W - 1) + i, TT), :]
    return acc


def _ssd_intra(xs, bm, cm, g, dt, Mk_ref):
    TT = xs.shape[0]
    rT = (g - jnp.log(dt)).T
    lane = lax.broadcasted_iota(jnp.int32, (1, LANE), 1)
    lane2 = lax.broadcasted_iota(jnp.int32, (1, 2 * LANE), 1)
    causal = Mk_ref[...] > 0.0
    bmb = bm.astype(BF)
    zero = jnp.zeros((), BF)
    o_parts = []
    for grp in range(SSD_G):
        cb = _dot_nt(jnp.where((lane // SSD_N) == grp, cm, 0.0).astype(BF), bmb).astype(BF)
        xg = xs[:, grp * 2 * LANE:(grp + 1) * 2 * LANE].astype(BF)
        og = jnp.zeros((TT, 2 * LANE), F32)
        for h4 in range(SSD_H // SSD_G):
            h = grp * (SSD_H // SSD_G) + h4
            dec = jnp.where(causal, jnp.exp(g[:, h:h + 1] - rT[h:h + 1, :]), 0.0)
            p = cb * dec.astype(BF)
            og = og + _dot(p, jnp.where((lane2 // SSD_P) == h4, xg, zero))
        o_parts.append(og)
    return jnp.concatenate(o_parts, axis=1)


def _ssd_prompt_kernel(x_ref, cw_ref, cb_ref, dtb_ref, alog_ref, dexp_ref, nw_ref, L_ref, Mk_ref, Eexp_ref, M2_ref,
                       res_ref, g1_ref, og_ref, or_ref, wo_ref, lg_ref, lb_ref,
                       o_ref, sfin_ref, cfin_ref, st_ref, xp_ref):
    TT = x_ref.shape[0]
    t = pl.program_id(1)

    @pl.when(t == 0)
    def _():
        st_ref[...] = jnp.zeros_like(st_ref)
        xp_ref[0:8, :] = jnp.zeros((8, SSD_CONV_DIM), F32)

    z = x_ref[:, 0:SSD_WIDTH]
    xp_ref[8:8 + TT, :] = x_ref[:, SSD_WIDTH:SSD_WIDTH + SSD_CONV_DIM]
    sdt = x_ref[:, SSD_WIDTH + SSD_CONV_DIM:SSD_IN_W]
    xbc = _silu(_ssd_conv(xp_ref, cw_ref, cb_ref, TT))
    tail = xp_ref[TT:TT + 8, :]
    xp_ref[0:8, :] = tail
    dt_all = _softplus(sdt + dtb_ref[...])
    la_all = dt_all * (-jnp.exp(alog_ref[...]))
    Eexp = Eexp_ref[...]
    C = L_ref.shape[0]
    S = st_ref[...]
    for i in range(TT // C):
        rows = slice(i * C, (i + 1) * C)
        xs = xbc[rows, 0:SSD_WIDTH]
        bm = xbc[rows, SSD_WIDTH:SSD_WIDTH + LANE]
        cm = xbc[rows, SSD_WIDTH + LANE:SSD_CONV_DIM]
        dt = dt_all[rows, :]
        g = _dot_3x(L_ref[...], la_all[rows, :])
        gl = g[C - 1:C, :]
        eg_x = _dot_x2(jnp.exp(g), Eexp)
        cw_x = _dot_x2(dt * jnp.exp(gl - g), Eexp)
        egl_x = _dot_x2(jnp.exp(gl), Eexp)

        o = _ssd_intra(xs, bm, cm, g, dt, Mk_ref)
        o = o + eg_x * _dot(cm.astype(BF), S.astype(BF))
        u = _dot_tn(bm.astype(BF), (xs * cw_x).astype(BF))
        S = S * egl_x + u * M2_ref[...]

        y = (o + dexp_ref[...] * xs) * _silu(z[rows, :])
        ms = jnp.mean(y * y, axis=-1, keepdims=True)
        o_ssd = (y * lax.rsqrt(ms + EPS) * nw_ref[...]).astype(BF)
        merged = jnp.concatenate([og_ref[rows, :], or_ref[rows, :], o_ssd], axis=1)
        mix = _dot(merged, wo_ref[...])
        o_ref[0, rows, :] = _layer_norm(ALPHA * res_ref[0, rows, :] + g1_ref[0] * mix, lg_ref[0], lb_ref[0])
    st_ref[...] = S

    @pl.when(t == pl.num_programs(1) - 1)
    def _():
        for h in range(SSD_H):
            gi = h // (SSD_H // SSD_G)
            sfin_ref[0, h] = S[gi * SSD_N:(gi + 1) * SSD_N, h * SSD_P:(h + 1) * SSD_P]
        cfin_ref[0] = tail[8 - (SSD_CONV_W - 1):8, :]


def _pad_lanes(v, n=LANE):
    v = v.reshape(1, -1)
    return jnp.zeros((1, n), F32).at[:, :v.shape[1]].set(v)


def _ssd_tables(TT, c):
    L = jnp.asarray(_block_tril(TT, c), BF)
    Mk = jnp.asarray(_block_tril(TT, c), F32)
    e = np.zeros((LANE, SSD_WIDTH), np.float32)
    for h in range(SSD_H):
        e[h, h * SSD_P:(h + 1) * SSD_P] = 1.0
    M2 = np.zeros((SSD_G * SSD_N, SSD_WIDTH), np.float32)
    for h in range(SSD_H):
        gi = h // (SSD_H // SSD_G)
        M2[gi * SSD_N:(gi + 1) * SSD_N, h * SSD_P:(h + 1) * SSD_P] = 1.0
    return L, Mk, jnp.asarray(e, BF), jnp.asarray(M2, F32)


def _ssd_params(conv_w, conv_b, dt_bias, a_log, d, norm_w):
    return (conv_w, conv_b.reshape(1, -1), _pad_lanes(dt_bias), _pad_lanes(a_log),
            jnp.repeat(d, SSD_P).reshape(1, -1), norm_w.reshape(1, -1))


def _ssd_prompt_call(sin_, x3, g1, og, orr, w_out, ln_g, ln_b, conv_w, conv_b, dt_bias, a_log, d, norm_w):
    B, T, D = x3.shape
    C = 256
    TT = 4 * C if T % (4 * C) == 0 else (2 * C if T % (2 * C) == 0 else C)
    nT = T // TT
    L, Mk, Eexp, M2 = _ssd_tables(C, C)
    prm = _ssd_params(conv_w, conv_b, dt_bias, a_log, d, norm_w)
    rmap = lambda b, t: (b * nT + t, 0)
    return pl.pallas_call(
        _ssd_prompt_kernel,
        grid=(B, nT),
        in_specs=[pl.BlockSpec((TT, SSD_IN_W), rmap)]
                 + [_const(p.shape) for p in prm]
                 + [_const(L.shape), _const(Mk.shape), _const(Eexp.shape), _const(M2.shape)]
                 + [pl.BlockSpec((1, TT, D), lambda b, t: (b, t, 0)),
                    pl.BlockSpec((1, 1, D), lambda b, t: (b, 0, 0)),
                    pl.BlockSpec((TT, GLA_WIDTH), rmap), pl.BlockSpec((TT, RET_WIDTH), rmap),
                    _const((D, D)), _const((1, 1, D)), _const((1, 1, D))],
        out_specs=[pl.BlockSpec((1, TT, D), lambda b, t: (b, t, 0)),
                   pl.BlockSpec((1, SSD_H, SSD_N, SSD_P), lambda b, t: (b, 0, 0, 0)),
                   pl.BlockSpec((1, SSD_CONV_W - 1, SSD_CONV_DIM), lambda b, t: (b, 0, 0))],
        out_shape=[jax.ShapeDtypeStruct((B, T, D), F32),
                   jax.ShapeDtypeStruct((B, SSD_H, SSD_N, SSD_P), F32),
                   jax.ShapeDtypeStruct((B, SSD_CONV_W - 1, SSD_CONV_DIM), F32)],
        scratch_shapes=[pltpu.VMEM((SSD_G * SSD_N, SSD_WIDTH), F32),
                        pltpu.VMEM((TT + 8, SSD_CONV_DIM), F32)],
        compiler_params=_cp(2),
        name="ssd_outproj_ln",
    )(sin_, *prm, L, Mk, Eexp, M2, x3, g1, og, orr, w_out, ln_g.reshape(1, 1, D), ln_b.reshape(1, 1, D))


def _inproj_t_kernel(x_ref, sc_ref, sh_ref, wt_ref, og_ref, or_ref, os_ref, w_ref):
    nt, nb, D = x_ref.shape

    @pl.when(pl.program_id(0) == 0)
    def _():
        for src, dst, n in ((0, 0, N_GA + GLA_GATE_RANK), (N_GA + GLA_GATE_RANK, N_GA + LANE, N_DT - N_GA - GLA_GATE_RANK)):
            for r in range(0, n, 512):
                m = min(512, n - r)
                w_ref[dst + r:dst + r + m, :] = wt_ref[src + r:src + r + m, :].astype(BF)
        w_ref[N_GA + GLA_GATE_RANK:N_GA + LANE, :] = jnp.zeros((LANE - GLA_GATE_RANK, D), BF)
        tail = jnp.concatenate([wt_ref[N_DT:N_IN, :], jnp.zeros((LANE - SSD_H, D), F32)], axis=0)
        w_ref[IN_W - LANE:IN_W, :] = tail.astype(BF)

    h = x_ref[...] * (1.0 + sc_ref[...]) + sh_ref[...]
    for t in range(0, nt, 2):
        ht = h[t:t + 2].reshape(2 * nb, D).astype(BF)
        cols = slice(t * nb, (t + 2) * nb)
        og_ref[:, cols] = _dot_nt(w_ref[0:GLA_IN_W, :], ht)
        or_ref[:, cols] = _dot_nt(w_ref[GLA_IN_W:GLA_IN_W + RET_IN_W, :], ht)
        os_ref[:, cols] = _dot_nt(w_ref[GLA_IN_W + RET_IN_W:IN_W, :], ht)


def _inproj_t_call(xt, sc, sh, wt, l):
    T, B, D = xt.shape
    nt = 4
    cmap = lambda i: (0, i)
    return pl.pallas_call(
        _inproj_t_kernel,
        grid=(T // nt,),
        in_specs=[pl.BlockSpec((nt, B, D), lambda i: (i, 0, 0)),
                  pl.BlockSpec((1, B, D), lambda i: (0, 0, 0)),
                  pl.BlockSpec((1, B, D), lambda i: (0, 0, 0)),
                  _resident_layer(wt.shape, l)],
        out_specs=[pl.BlockSpec((GLA_IN_W, nt * B), cmap),
                   pl.BlockSpec((RET_IN_W, nt * B), cmap),
                   pl.BlockSpec((SSD_IN_W, nt * B), cmap)],
        out_shape=[jax.ShapeDtypeStruct((GLA_IN_W, T * B), F32),
                   jax.ShapeDtypeStruct((RET_IN_W, T * B), F32),
                   jax.ShapeDtypeStruct((SSD_IN_W, T * B), F32)],
        scratch_shapes=[pltpu.VMEM((IN_W, D), BF)],
        compiler_params=_cp(1),
        name="in_proj_t",
    )(xt, sc, sh, wt)


def _row_sum(x):
    return jnp.sum(x, axis=0, keepdims=True)


def _lane_state_readout(o, coef_ref, s0_ref, n_rows):
    nb = LANE
    half = len(o) // 2
    for part in range(2):
        def body(k8, accs, part=part):
            accs = list(accs)
            base = pl.multiple_of(k8 * 8, 8)
            grp = [coef_ref[pl.ds(base, 8), (part * half + i) * nb:(part * half + i + 1) * nb] for i in range(half)]
            for j in range(8):
                s0k = s0_ref[0, k8 * 8 + j]
                for i in range(half):
                    accs[i] = accs[i] + grp[i][j:j + 1, :] * s0k
            return tuple(accs)

        res = lax.fori_loop(0, n_rows // 8, body, tuple(o[part * half:(part + 1) * half]))
        o[part * half:(part + 1) * half] = list(res)
    return o


def _lane_state_update(sn_ref, s0_ref, decay_fn, coef_ref, val_fn, n_rows, T):
    nb = LANE

    def body(k8, carry):
        base = pl.multiple_of(k8 * 8, 8)
        grp = [coef_ref[pl.ds(base, 8), t * nb:(t + 1) * nb] for t in range(T)]
        dec = decay_fn(base)
        for j in range(8):
            dj = dec[j:j + 1, :] if dec.shape[0] == 8 else dec
            sk = dj * s0_ref[0, k8 * 8 + j]
            for t in range(T):
                sk = sk + grp[t][j:j + 1, :] * val_fn(t)
            sn_ref[0, 0, k8 * 8 + j] = sk
        return carry

    lax.fori_loop(0, n_rows // 8, body, 0)


def _state_specs(shape, l, first):
    assert l == 0 or not first
    tail = tuple(shape[2:])
    in_spec = pl.BlockSpec((None, 1) + tail, lambda h: (l, h, 0, 0, 0))
    out_spec = pl.BlockSpec(((shape[0] if first else 1), 1) + tail, lambda h: (l, h, 0, 0, 0))
    return in_spec, out_spec


def _zero_later_layers(ref):
    ref[1:] = jnp.zeros((ref.shape[0] - 1,) + tuple(ref.shape[1:]), ref.dtype)


def _finish_state_call(kern, n_in, first, prevs):
    if first:
        return functools.partial(kern, first=True), [], {}
    wrapped = lambda *a, **kw: kern(*a[:n_in], *a[n_in + len(prevs):], first=False, **kw)
    specs = [pl.BlockSpec(memory_space=pl.ANY)] * len(prevs)
    return wrapped, specs, {n_in + i: 1 + i for i in range(len(prevs))}


def _gla_t_kernel(x_ref, s0_ref, wg_ref, bg_ref, nw_ref, o_ref, sn_ref, qe_ref, ke_ref, a_ref, *, T, first):
    nb = LANE
    if first:
        _zero_later_layers(sn_ref)
    h = pl.program_id(0)
    r0 = pl.multiple_of(h * GLA_DK, GLA_DK)
    v0 = pl.multiple_of(h * GLA_DV, GLA_DV)
    q = x_ref[pl.ds(r0, GLA_DK), :] * (GLA_DK ** -0.5)
    k = x_ref[pl.ds(128 + r0, GLA_DK), :]
    gate = _dot(wg_ref[pl.ds(r0, GLA_DK), :], x_ref[512:640, :].astype(BF)) + bg_ref[pl.ds(r0, GLA_DK), :]
    la = _log_sigmoid(gate) * (1.0 / GLA_GATE_TEMP)
    gs = []
    acc = jnp.zeros((GLA_DK, nb), F32)
    for t in range(T):
        acc = acc + la[:, t * nb:(t + 1) * nb]
        gs.append(acc)
    gl = gs[T - 1]
    a_ref[...] = jnp.exp(gl)
    qs = [q[:, t * nb:(t + 1) * nb] for t in range(T)]
    ks = [k[:, t * nb:(t + 1) * nb] for t in range(T)]
    for t in range(T):
        qe_ref[:, t * nb:(t + 1) * nb] = qs[t] * jnp.exp(gs[t])
        ke_ref[:, t * nb:(t + 1) * nb] = ks[t] * jnp.exp(gl - gs[t])

    def vt(t):
        return x_ref[pl.ds(256 + v0, GLA_DV), t * nb:(t + 1) * nb]

    o = []
    for t in range(T):
        ot = jnp.zeros((GLA_DV, nb), F32)
        for u in range(t + 1):
            s = _row_sum(qs[t] * ks[u] * jnp.exp(gs[t] - gs[u]))
            ot = ot + s * vt(u)
        o.append(ot)

    o = _lane_state_readout(o, qe_ref, s0_ref, GLA_DK)
    _lane_state_update(sn_ref, s0_ref, lambda base: a_ref[pl.ds(base, 8), :], ke_ref, vt, GLA_DK, T)

    nw = nw_ref[pl.ds(v0, GLA_DV), :]
    for t in range(T):
        ms = jnp.mean(o[t] * o[t], axis=0, keepdims=True)
        r = x_ref[pl.ds(640 + v0, GLA_DV), t * nb:(t + 1) * nb]
        o_ref[:, t * nb:(t + 1) * nb] = (o[t] * lax.rsqrt(ms + EPS) * nw * _silu(r)).astype(o_ref.dtype)


def _gla_t_call(gT, s0, prev, l, T, w_gate, b_gate, norm_w):
    N = gT.shape[1]
    wg = jnp.zeros((GLA_H * GLA_DK, LANE), F32).at[:, :GLA_GATE_RANK].set(w_gate.T).astype(BF)
    bg = b_gate.reshape(-1, 1)
    nw = norm_w.reshape(-1, 1)
    first = prev is None
    prevs = [] if first else [prev]
    s_in, s_out = _state_specs(s0.shape, l, first)
    ins = [gT, s0, wg, bg, nw]
    specs = [_const(gT.shape), s_in, _const(wg.shape), _const(bg.shape), _const(nw.shape)]
    kern, pspecs, aliases = _finish_state_call(functools.partial(_gla_t_kernel, T=T), len(ins), first, prevs)
    ins, specs = ins + prevs, specs + pspecs
    return pl.pallas_call(
        kern,
        grid=(GLA_H,),
        in_specs=specs,
        out_specs=[pl.BlockSpec((GLA_DV, N), lambda h: (h, 0)), s_out],
        out_shape=[jax.ShapeDtypeStruct((GLA_WIDTH, N), BF), jax.ShapeDtypeStruct(s0.shape, F32)],
        scratch_shapes=[pltpu.VMEM((GLA_DK, N), F32), pltpu.VMEM((GLA_DK, N), F32), pltpu.VMEM((GLA_DK, LANE), F32)],
        input_output_aliases=aliases,
        compiler_params=_cp(1),
        name="gla_t",
    )(*ins)


def _ret_t_kernel(x_ref, s0_ref, cos_ref, sin_ref, pw_ref, nw_ref, o_ref, sn_ref, qd_ref, kd_ref, *, T, first):
    nb = LANE
    if first:
        _zero_later_layers(sn_ref)
    h = pl.program_id(0)
    r0 = pl.multiple_of(h * RET_DK, RET_DK)
    half_k = RET_DK // 2
    cos, sin = cos_ref[...], sin_ref[...]

    def rope_t(base):
        x1 = x_ref[pl.ds(base + r0, half_k), :]
        x2 = x_ref[pl.ds(base + r0 + half_k, half_k), :]
        return jnp.concatenate([x1 * cos - x2 * sin, x1 * sin + x2 * cos], axis=0)

    q = rope_t(0)
    k = rope_t(256) * (RET_DK ** -0.5)
    pw = pw_ref[h]
    qs = [q[:, t * nb:(t + 1) * nb] for t in range(T)]
    ks = [k[:, t * nb:(t + 1) * nb] for t in range(T)]
    for t in range(T):
        qd_ref[:, t * nb:(t + 1) * nb] = qs[t] * pw[t + 1:t + 2, :]
        kd_ref[:, t * nb:(t + 1) * nb] = ks[t] * pw[T - 1 - t:T - t, :]

    def vt(t):
        return x_ref[pl.ds(512 + r0, RET_DV), t * nb:(t + 1) * nb]

    o = []
    for t in range(T):
        ot = jnp.zeros((RET_DV, nb), F32)
        for u in range(t + 1):
            s = _row_sum(qs[t] * ks[u]) * pw[t - u:t - u + 1, :]
            ot = ot + s * vt(u)
        o.append(ot)

    o = _lane_state_readout(o, qd_ref, s0_ref, RET_DK)
    _lane_state_update(sn_ref, s0_ref, lambda base: pw[T:T + 1, :], kd_ref, vt, RET_DK, T)

    nw = nw_ref[pl.ds(r0, RET_DV), :]
    for t in range(T):
        mu = jnp.mean(o[t], axis=0, keepdims=True)
        d = o[t] - mu
        var = jnp.mean(d * d, axis=0, keepdims=True)
        rg = x_ref[pl.ds(768 + r0, RET_DV), t * nb:(t + 1) * nb]
        o_ref[:, t * nb:(t + 1) * nb] = (d * lax.rsqrt(var + EPS) * nw * _silu(rg)).astype(o_ref.dtype)


def _ret_t_call(rT, s0, prev, l, T, norm_w):
    N = rT.shape[1]
    B = N // T
    half = RET_DK // 2
    inv = ROPE_BASE ** (-jnp.arange(half, dtype=F32) / half)
    ang = inv[:, None] * (PAST_LEN + jnp.arange(T, dtype=jnp.int32)).astype(F32)[None, :]
    cos = jnp.repeat(jnp.cos(ang), B, axis=1)
    sin = jnp.repeat(jnp.sin(ang), B, axis=1)
    lg = _ret_log_gamma()
    pw = jnp.asarray(np.repeat(np.exp(lg[:, None] * np.arange(16)[None, :])[:, :, None], LANE, axis=2), F32)
    nw = norm_w.reshape(-1, 1)
    first = prev is None
    prevs = [] if first else [prev]
    s_in, s_out = _state_specs(s0.shape, l, first)
    ins = [rT, s0, cos, sin, pw, nw]
    specs = [_const(rT.shape), s_in, _const(cos.shape), _const(sin.shape), _const(pw.shape), _const(nw.shape)]
    kern, pspecs, aliases = _finish_state_call(functools.partial(_ret_t_kernel, T=T), len(ins), first, prevs)
    ins, specs = ins + prevs, specs + pspecs
    return pl.pallas_call(
        kern,
        grid=(RET_H,),
        in_specs=specs,
        out_specs=[pl.BlockSpec((RET_DV, N), lambda h: (h, 0)), s_out],
        out_shape=[jax.ShapeDtypeStruct((RET_WIDTH, N), BF), jax.ShapeDtypeStruct(s0.shape, F32)],
        scratch_shapes=[pltpu.VMEM((RET_DK, N), F32), pltpu.VMEM((RET_DK, N), F32)],
        input_output_aliases=aliases,
        compiler_params=_cp(1),
        name="ret_t",
    )(*ins)


def _ssd_t_kernel(x_ref, c0_ref, s0_ref, cw_ref, cb_ref, dtb_ref, alog_ref, d_ref, nw_ref,
                  o_ref, sn_ref, cn_ref, hist_ref, y_ref, ssq_ref, cm_ref, bw_ref, xw_ref, *, T, first):
    nb = LANE
    W1 = SSD_CONV_W - 1
    h = pl.program_id(0)
    XB = SSD_WIDTH
    if first:
        _zero_later_layers(sn_ref)

    @pl.when(h == 0)
    def _():
        ssq_ref[...] = jnp.zeros_like(ssq_ref)
        if first:
            _zero_later_layers(cn_ref)
        for i in range(W1):
            for j in range(SSD_CONV_DIM // LANE):
                hist_ref[j * LANE:(j + 1) * LANE, i * nb:(i + 1) * nb] = c0_ref[0, i][:, j * LANE:(j + 1) * LANE].T
                cn_ref[0, i, :, j * LANE:(j + 1) * LANE] = \
                    x_ref[XB + j * LANE:XB + (j + 1) * LANE, (T - W1 + i) * nb:(T - W1 + i + 1) * nb].T

    def conv_rows(ro):
        w = cw_ref[pl.ds(ro, 64), :]
        b = cb_ref[pl.ds(ro, 64), :]
        xx = [hist_ref[pl.ds(ro, 64), i * nb:(i + 1) * nb] for i in range(W1)]
        xx += [x_ref[pl.ds(XB + ro, 64), t * nb:(t + 1) * nb] for t in range(T)]
        out = []
        for t in range(T):
            acc = b + w[:, 0:1] * xx[t]
            for i in range(1, SSD_CONV_W):
                acc = acc + w[:, i:i + 1] * xx[t + i]
            out.append(_silu(acc))
        return out

    grp = h // (SSD_H // SSD_G)
    xs = conv_rows(pl.multiple_of(h * SSD_P, SSD_P))
    bm = conv_rows(pl.multiple_of(SSD_WIDTH + grp * SSD_N, SSD_N))
    cm = conv_rows(pl.multiple_of(SSD_WIDTH + SSD_G * SSD_N + grp * SSD_N, SSD_N))

    dt_all = _softplus(x_ref[pl.ds(XB + SSD_CONV_DIM + h, 1), :] + dtb_ref[pl.ds(h, 1), :])
    a = -jnp.exp(alog_ref[pl.ds(h, 1), :])
    dts = [dt_all[:, t * nb:(t + 1) * nb] for t in range(T)]
    gs = []
    acc = jnp.zeros((1, nb), F32)
    for t in range(T):
        acc = acc + dts[t] * a
        gs.append(acc)
    gl = gs[T - 1]

    o = []
    for t in range(T):
        ot = jnp.zeros((SSD_P, nb), F32)
        for u in range(t + 1):
            s = _row_sum(cm[t] * bm[u]) * (jnp.exp(gs[t] - gs[u]) * dts[u])
            ot = ot + s * xs[u]
        o.append(ot)

    for t in range(T):
        cm_ref[:, t * nb:(t + 1) * nb] = cm[t] * jnp.exp(gs[t])
        bw_ref[:, t * nb:(t + 1) * nb] = bm[t]
        xw_ref[:, t * nb:(t + 1) * nb] = xs[t] * (dts[t] * jnp.exp(gl - gs[t]))

    o = _lane_state_readout(o, cm_ref, s0_ref, SSD_N)
    egl = jnp.exp(gl)
    _lane_state_update(sn_ref, s0_ref, lambda base: egl, bw_ref, lambda t: xw_ref[:, t * nb:(t + 1) * nb], SSD_N, T)

    dd = d_ref[pl.ds(h, 1), :]
    p0 = pl.multiple_of(h * SSD_P, SSD_P)
    for t in range(T):
        z = x_ref[pl.ds(p0, SSD_P), t * nb:(t + 1) * nb]
        y = (o[t] + dd * xs[t]) * _silu(z)
        y_ref[pl.ds(p0, SSD_P), t * nb:(t + 1) * nb] = y
        ssq_ref[:, t * nb:(t + 1) * nb] += _row_sum(y * y)

    @pl.when(h == SSD_H - 1)
    def _():
        scale = lax.rsqrt(ssq_ref[...] * (1.0 / SSD_WIDTH) + EPS)
        o_ref[...] = (y_ref[...] * scale * nw_ref[...]).astype(o_ref.dtype)


def _ssd_t_call(sT, c0, s0, prev_s, prev_c, l, T, conv_w, conv_b, dt_bias, a_log, d, norm_w):
    N = sT.shape[1]
    col = lambda v: jnp.zeros((LANE, 1), F32).at[:SSD_H, 0].set(v)
    prm = (conv_w.T, conv_b.reshape(-1, 1), col(dt_bias), col(a_log), col(d), norm_w.reshape(-1, 1))
    first = prev_s is None
    prevs = [] if first else [prev_s, prev_c]
    s_in, s_out = _state_specs(s0.shape, l, first)
    c_in = pl.BlockSpec((1,) + tuple(c0.shape[1:]), lambda h: (l, 0, 0, 0))
    c_out = pl.BlockSpec(((c0.shape[0] if first else 1),) + tuple(c0.shape[1:]), lambda h: (l, 0, 0, 0))
    ins = [sT, c0, s0, *prm]
    specs = [_const(sT.shape), c_in, s_in] + [_const(p.shape) for p in prm]
    kern, pspecs, aliases = _finish_state_call(functools.partial(_ssd_t_kernel, T=T), len(ins), first, prevs)
    ins, specs = ins + prevs, specs + pspecs
    return pl.pallas_call(
        kern,
        grid=(SSD_H,),
        in_specs=specs,
        out_specs=[_const((SSD_WIDTH, N)), s_out, c_out],
        out_shape=[jax.ShapeDtypeStruct((SSD_WIDTH, N), BF), jax.ShapeDtypeStruct(s0.shape, F32),
                   jax.ShapeDtypeStruct(c0.shape, F32)],
        scratch_shapes=[pltpu.VMEM((SSD_CONV_DIM, (SSD_CONV_W - 1) * LANE), F32),
                        pltpu.VMEM((SSD_WIDTH, N), F32), pltpu.VMEM((1, N), F32),
                        pltpu.VMEM((SSD_N, N), F32), pltpu.VMEM((SSD_N, N), F32), pltpu.VMEM((SSD_P, N), F32)],
        input_output_aliases=aliases,
        compiler_params=_cp(1),
        name="ssd_t",
    )(*ins)


def _outproj_t_kernel(x_ref, g_ref, og_ref, or_ref, os_ref, w_ref, lg_ref, lb_ref, o_ref):
    nt, nb, D = x_ref.shape
    for t in range(nt):
        cols = slice(t * nb, (t + 1) * nb)
        mix = (_dot_tn(og_ref[:, cols], w_ref[0:GLA_WIDTH, :])
               + _dot_tn(or_ref[:, cols], w_ref[GLA_WIDTH:GLA_WIDTH + RET_WIDTH, :])
               + _dot_tn(os_ref[:, cols], w_ref[GLA_WIDTH + RET_WIDTH:D, :]))
        y = ALPHA * x_ref[t] + g_ref[0] * mix
        o_ref[t] = _layer_norm(y, lg_ref[0], lb_ref[0])


def _outproj_t_call(xt, g1, ogT, orT, osT, w_out, ln_g, ln_b):
    T, B, D = xt.shape
    nt = 4
    cmap = lambda i: (0, i)
    return pl.pallas_call(
        _outproj_t_kernel,
        grid=(T // nt,),
        in_specs=[pl.BlockSpec((nt, B, D), lambda i: (i, 0, 0)),
                  pl.BlockSpec((1, B, D), lambda i: (0, 0, 0)),
                  pl.BlockSpec((GLA_WIDTH, nt * B), cmap),
                  pl.BlockSpec((RET_WIDTH, nt * B), cmap),
                  pl.BlockSpec((SSD_WIDTH, nt * B), cmap),
                  _const((D, D)), _const((1, 1, D)), _const((1, 1, D))],
        out_specs=pl.BlockSpec((nt, B, D), lambda i: (i, 0, 0)),
        out_shape=jax.ShapeDtypeStruct((T, B, D), F32),
        compiler_params=_cp(1),
        name="out_proj_ln_t",
    )(xt, g1, ogT, orT, osT, w_out, ln_g.reshape(1, 1, D), ln_b.reshape(1, 1, D))


ROUTE_OFF = 8


def _moe_route_t(lt):
    neg = jnp.float32(-jnp.inf)
    row8 = lax.broadcasted_iota(jnp.int32, (8, 1), 0)
    lg = jnp.where(row8 < MOE_GROUPS, lt[0:8, :], neg)
    mg = jnp.max(lg, axis=0, keepdims=True)
    gsel = jnp.min(jnp.where(lg == mg, row8, 8), axis=0, keepdims=True)
    g_gate = 1.0 / jnp.sum(jnp.exp(lg - mg), axis=0, keepdims=True)
    rowe = lax.broadcasted_iota(jnp.int32, (MOE_EXPERTS, 1), 0)
    le = jnp.where((rowe // MOE_PER_GROUP) == gsel, lt[ROUTE_OFF:ROUTE_OFF + MOE_EXPERTS, :], neg)
    m1 = jnp.max(le, axis=0, keepdims=True)
    i1 = jnp.min(jnp.where(le == m1, rowe, MOE_EXPERTS), axis=0, keepdims=True)
    le2 = jnp.where(rowe == i1, neg, le)
    m2 = jnp.max(le2, axis=0, keepdims=True)
    i2 = jnp.min(jnp.where(le2 == m2, rowe, MOE_EXPERTS), axis=0, keepdims=True)
    e2 = jnp.exp(m2 - m1)
    w1 = g_gate / (1.0 + e2)
    w2 = g_gate * e2 / (1.0 + e2)
    comb = jnp.where(rowe == i1, w1, jnp.where(rowe == i2, w2, 0.0))
    cg = comb[0:4, :]
    for g in range(1, MOE_GROUPS):
        cg = cg + comb[g * MOE_PER_GROUP:(g + 1) * MOE_PER_GROUP, :]
    return gsel, cg, comb


MOE_SUB = 256
MOE_BLK = 16
MOE_ROWS = 256
MOE_NPS = MOE_SUB + MOE_GROUPS * MOE_BLK
assert MOE_SUB <= MOE_ROWS


def _moe_kernel(x_ref, sc_ref, sh_ref, g_ref, wr_ref, br_ref, us_ref, w1_ref, w3_ref, w2_ref, lg_ref, lb_ref,
                o_ref, hb_ref, cwb_ref, hp_ref, cwp_ref, yp_ref, pos_ref,
                cgrp_ref, fill_ref, cur_ref, na_ref, nb_ref, so_ref, nfa_ref, dsa_ref, dsb_ref, *, n_steps):
    bB, bT, D = x_ref.shape
    R = bB * bT
    n_q = R // MOE_SUB
    s = pl.program_id(1)
    x = x_ref[...]
    row8 = lax.broadcasted_iota(jnp.int32, (8, 1), 0)
    slot = lax.broadcasted_iota(jnp.int32, (MOE_NPS, 1), 0).astype(F32)

    @pl.when((pl.program_id(0) == 0) & (s == 0))
    def _():
        hb_ref[...] = jnp.zeros_like(hb_ref)
        cwb_ref[...] = jnp.zeros_like(cwb_ref)
        yp_ref[...] = jnp.zeros_like(yp_ref)

    @pl.when(s == 0)
    def _():
        na_ref[0] = 0
        for g in range(MOE_GROUPS):
            cur_ref[g] = -1
            fill_ref[g] = 0

    @pl.when(s < n_steps)
    def _():
        h = (x * (1.0 + sc_ref[...]) + sh_ref[...]).reshape(R, D)
        segs, offs = [], []
        for q in range(n_q):
            u = s * n_q + q
            hq = h[q * MOE_SUB:(q + 1) * MOE_SUB, :].astype(BF)
            gsel, cg, _ = _moe_route_t(_dot_nt(wr_ref[...], hq) + br_ref[...])
            onehot = jnp.where(row8 == gsel, 1.0, 0.0)
            rank = _dot(onehot.astype(BF), us_ref[...])
            cnt = jnp.sum(onehot, axis=1, keepdims=True)
            seg = jnp.ceil(cnt * (1.0 / MOE_BLK)) * MOE_BLK
            off = jnp.zeros((8, 1), F32)
            for g in range(1, MOE_GROUPS):
                off = off + jnp.where(row8 >= g, seg[g - 1:g, :], 0.0)
            pos = jnp.sum(onehot * (off + rank), axis=0, keepdims=True)
            pos_ref[u] = jnp.broadcast_to(pos, (8, MOE_SUB))
            perm = jnp.where(slot == pos, 1.0, 0.0).astype(BF)
            hp_ref[q] = _dot(perm, hq).astype(BF)
            cg8 = jnp.concatenate([cg, jnp.zeros((4, MOE_SUB), F32)], axis=0)
            cg_hi = cg8.astype(BF)
            cg_lo = (cg8 - cg_hi.astype(F32)).astype(BF)
            cwp_ref[q] = _dot_nt(perm, cg_hi) + _dot_nt(perm, cg_lo)
            segs.append(seg)
            offs.append(off)
        for q in range(n_q):
            u = s * n_q + q
            for g in range(MOE_GROUPS):
                so = offs[q][g, 0].astype(jnp.int32)
                nb = (segs[q][g, 0] * (1.0 / MOE_BLK)).astype(jnp.int32)
                f = fill_ref[g]
                c = cur_ref[g]
                na = na_ref[0]
                room = jnp.where(c < 0, 0, (MOE_ROWS - f) // MOE_BLK)
                n_a = jnp.minimum(nb, room)
                n_b = nb - n_a
                base_a = c * MOE_ROWS + f
                base_b = na * MOE_ROWS
                idx = u * MOE_GROUPS + g
                so_ref[idx] = so
                nb_ref[idx] = nb
                nfa_ref[idx] = n_a
                dsa_ref[idx] = base_a
                dsb_ref[idx] = base_b

                def put(k, carry, so=so, q=q, n_a=n_a, base_a=base_a, base_b=base_b):
                    dst = pl.multiple_of(jnp.where(k < n_a, base_a + k * MOE_BLK, base_b + (k - n_a) * MOE_BLK), MOE_BLK)
                    src = pl.multiple_of(so + k * MOE_BLK, MOE_BLK)
                    hb_ref[pl.ds(dst, MOE_BLK), :] = hp_ref[q, pl.ds(src, MOE_BLK), :]
                    cwb_ref[pl.ds(dst, MOE_BLK), :] = cwp_ref[q, pl.ds(src, MOE_BLK), :]
                    return carry

                lax.fori_loop(0, nb, put, 0)

                @pl.when(n_b > 0)
                def _(g=g, na=na, n_b=n_b):
                    cgrp_ref[na] = g
                    na_ref[0] = na + 1
                    cur_ref[g] = na
                    fill_ref[g] = n_b * MOE_BLK

                @pl.when(n_b == 0)
                def _(g=g, f=f, n_a=n_a):
                    fill_ref[g] = f + n_a * MOE_BLK

    @pl.when(s == n_steps - 1)
    def _():
        def chunk(c, carry):
            g = cgrp_ref[c]
            start = pl.multiple_of(c * MOE_ROWS, MOE_ROWS)
            hc = hb_ref[pl.ds(start, MOE_ROWS), :]
            cw = cwb_ref[pl.ds(start, MOE_ROWS), :]
            hids = []
            for j in range(MOE_PER_GROUP):
                e = g * MOE_PER_GROUP + j
                hid = _silu(_dot(hc, w1_ref[e])) * _dot(hc, w3_ref[e]) * cw[:, j:j + 1]
                hids.append(hid.astype(BF))
            w2g = w2_ref[pl.ds(g * MOE_PER_GROUP, MOE_PER_GROUP)].reshape(MOE_PER_GROUP * MOE_FF, D)
            hb_ref[pl.ds(start, MOE_ROWS), :] = _dot(jnp.concatenate(hids, axis=1), w2g).astype(BF)
            return carry

        lax.fori_loop(0, na_ref[0], chunk, 0)

    @pl.when(s >= n_steps)
    def _():
        for q in range(n_q):
            u = (s - n_steps) * n_q + q
            for g in range(MOE_GROUPS):
                idx = u * MOE_GROUPS + g
                so, n_a, base_a, base_b = so_ref[idx], nfa_ref[idx], dsa_ref[idx], dsb_ref[idx]

                def take(k, carry, so=so, q=q, n_a=n_a, base_a=base_a, base_b=base_b):
                    src = pl.multiple_of(jnp.where(k < n_a, base_a + k * MOE_BLK, base_b + (k - n_a) * MOE_BLK), MOE_BLK)
                    dst = pl.multiple_of(so + k * MOE_BLK, MOE_BLK)
                    yp_ref[q, pl.ds(dst, MOE_BLK), :] = hb_ref[pl.ds(src, MOE_BLK), :]
                    return carry

                lax.fori_loop(0, nb_ref[idx], take, 0)
        ys = []
        for q in range(n_q):
            u = (s - n_steps) * n_q + q
            perm = jnp.where(slot == pos_ref[u][0:1, :], 1.0, 0.0).astype(BF)
            ys.append(_dot_tn(perm, yp_ref[q]))
        y = jnp.concatenate(ys, axis=0)
        z = ALPHA * x + g_ref[...] * y.reshape(bB, bT, D)
        o_ref[...] = _layer_norm(z, lg_ref[...], lb_ref[...])


def _moe_call(x3, sc, sh, g2, wr, br, w1, w3, w2, l, ln_g, ln_b):
    B, T, D = x3.shape
    bB, bT = _tok_tiles(B, T)
    R = bB * bT
    if bB == 1:
        spp = 2 if B % 2 == 0 else 1
        nT = T // bT
        n_pools, n_steps = B // spp, spp * nT
        xmap = lambda p, s: (p * spp + (s % n_steps) // nT, (s % n_steps) % nT, 0)
        omap = lambda p, s: (p * spp + jnp.maximum(s - n_steps, 0) // nT, jnp.maximum(s - n_steps, 0) % nT, 0)
        mmap = lambda p, s: (p * spp + (s % n_steps) // nT, 0, 0)
        mshape = (1, 1, D)
    else:
        n_pools, n_steps = 1, B // bB
        xmap = lambda p, s: (s % n_steps, 0, 0)
        omap = lambda p, s: (jnp.maximum(s - n_steps, 0), 0, 0)
        mmap = lambda p, s: (0, 0, 0)
        mshape = (1, bT, D)
    n_sub = n_steps * (R // MOE_SUB)
    n_chunks = pl.cdiv(n_sub * (MOE_SUB + MOE_GROUPS * (MOE_BLK - 1)), MOE_ROWS) + MOE_GROUPS
    us = jnp.asarray(np.triu(np.ones((MOE_SUB, MOE_SUB), np.float32), 1), BF)
    smem = lambda n: pltpu.SMEM((n,), jnp.int32)
    return pl.pallas_call(
        functools.partial(_moe_kernel, n_steps=n_steps),
        grid=(n_pools, 2 * n_steps),
        in_specs=[pl.BlockSpec((bB, bT, D), xmap),
                  pl.BlockSpec(mshape, mmap), pl.BlockSpec(mshape, mmap), pl.BlockSpec(mshape, mmap),
                  _const(wr.shape), _const(br.shape), _const(us.shape),
                  _resident_layer(w1.shape, l), _resident_layer(w3.shape, l), _resident_layer(w2.shape, l),
                  _const((1, 1, D)), _const((1, 1, D))],
        out_specs=pl.BlockSpec((bB, bT, D), omap),
        out_shape=jax.ShapeDtypeStruct((B, T, D), F32),
        scratch_shapes=[pltpu.VMEM((n_chunks * MOE_ROWS, D), BF), pltpu.VMEM((n_chunks * MOE_ROWS, 8), F32),
                        pltpu.VMEM((R // MOE_SUB, MOE_NPS, D), BF), pltpu.VMEM((R // MOE_SUB, MOE_NPS, 8), F32),
                        pltpu.VMEM((R // MOE_SUB, MOE_NPS, D), BF),
                        pltpu.VMEM((n_sub, 8, MOE_SUB), F32),
                        smem(n_chunks), smem(MOE_GROUPS), smem(MOE_GROUPS), smem(1),
                        *[smem(n_sub * MOE_GROUPS) for _ in range(5)]],
        compiler_params=_cp(2),
        name="moe_ln",
    )(x3, sc, sh, g2, wr, br, us, w1, w3, w2, ln_g.reshape(1, 1, D), ln_b.reshape(1, 1, D))


def _router_params(w_group, b_group, w_expert, b_expert):
    wr = jnp.zeros((LANE, D_MODEL), F32).at[:MOE_GROUPS].set(w_group.T)
    wr = wr.at[ROUTE_OFF:ROUTE_OFF + MOE_EXPERTS].set(w_expert.T)
    br = jnp.zeros((LANE, 1), F32).at[:MOE_GROUPS, 0].set(b_group).at[ROUTE_OFF:ROUTE_OFF + MOE_EXPERTS, 0].set(b_expert)
    return wr.astype(BF), br


def kernel(x_prompt, x_sample, c_prompt, c_sample, state_gla, state_ret, state_ssd, state_conv, w_ada, b_ada, w_in, gla_w_gate, gla_b_gate, gla_norm, ret_norm, ssd_conv_w, ssd_conv_b, ssd_dt_bias, ssd_a_log, ssd_d, ssd_norm, w_out, ln1_g, ln1_b, moe_w_group, moe_b_group, moe_w_expert, moe_b_expert, moe_w1, moe_w3, moe_w2, ln2_g, ln2_b):
    Bp, Tp, D = x_prompt.shape
    Bs, Ts, _ = x_sample.shape
    w_in_t = jnp.swapaxes(w_in, 1, 2)
    w_out_b = w_out.astype(BF)
    w1_b, w3_b, w2_b = moe_w1.astype(BF), moe_w3.astype(BF), moe_w2.astype(BF)

    mod = _mod_call(jnp.concatenate([c_prompt, c_sample], axis=0), w_ada, b_ada)

    def moe(x, sc2, sh2, g2, l):
        wr, br = _router_params(moe_w_group[l], moe_b_group[l], moe_w_expert[l], moe_b_expert[l])
        return _moe_call(x, sc2, sh2, g2, wr, br, w1_b, w3_b, w2_b, l, ln2_g[l], ln2_b[l])

    x = x_prompt
    new = [[], [], [], []]
    for l in range(DEPTH):
        sh1, sc1, g1, sh2, sc2, g2 = (mod[l, :Bp, None, i * D:(i + 1) * D] for i in range(6))
        gin, rin, sin_ = _inproj_call(x, sc1, sh1, w_in_t, l)
        og, s_gla = _gla_prompt_call(gin, Bp, Tp, gla_w_gate[l], gla_b_gate[l], gla_norm[l])
        orr, s_ret = _ret_prompt_call(rin, Bp, Tp, ret_norm[l])
        x, s_ssd, s_conv = _ssd_prompt_call(sin_, x, g1, og, orr, w_out_b[l], ln1_g[l], ln1_b[l], ssd_conv_w[l],
                                            ssd_conv_b[l], ssd_dt_bias[l], ssd_a_log[l], ssd_d[l], ssd_norm[l])
        x = moe(x, sc2, sh2, g2, l)
        for acc, s in zip(new, (s_gla, s_ret, s_ssd, s_conv)):
            acc.append(s)
    y_p = x
    gla_p, ret_p, ssd_p, conv_p = (jnp.stack(a) for a in new)

    x = jnp.swapaxes(x_sample, 0, 1)
    sg = jnp.transpose(state_gla, (0, 2, 3, 4, 1))
    sr = jnp.transpose(state_ret, (0, 2, 3, 4, 1))
    ss = jnp.transpose(state_ssd, (0, 2, 3, 4, 1))
    cv = jnp.transpose(state_conv, (0, 2, 1, 3))
    gla_n = ret_n = ssd_n = conv_n = None
    for l in range(DEPTH):
        sh1, sc1, g1, sh2, sc2, g2 = (mod[l, None, Bp:, i * D:(i + 1) * D] for i in range(6))
        gT, rT, sT = _inproj_t_call(x, sc1, sh1, w_in_t, l)
        ogT, gla_n = _gla_t_call(gT, sg, gla_n, l, Ts, gla_w_gate[l], gla_b_gate[l], gla_norm[l])
        orT, ret_n = _ret_t_call(rT, sr, ret_n, l, Ts, ret_norm[l])
        osT, ssd_n, conv_n = _ssd_t_call(sT, cv, ss, ssd_n, conv_n, l, Ts, ssd_conv_w[l], ssd_conv_b[l],
                                         ssd_dt_bias[l], ssd_a_log[l], ssd_d[l], ssd_norm[l])
        x = _outproj_t_call(x, g1, ogT, orT, osT, w_out_b[l], ln1_g[l], ln1_b[l])
        x = moe(x, sc2, sh2, g2, l)
    y_s = jnp.swapaxes(x, 0, 1)
    gla_s = jnp.transpose(gla_n, (0, 4, 1, 2, 3))
    ret_s = jnp.transpose(ret_n, (0, 4, 1, 2, 3))
    ssd_s = jnp.transpose(ssd_n, (0, 4, 1, 2, 3))
    conv_s = jnp.transpose(conv_n, (0, 2, 1, 3))
    return (y_p, y_s, gla_p, ret_p, ssd_p, conv_p, gla_s, ret_s, ssd_s, conv_s)
```

```python
import functools

import numpy as np
import jax
import jax.numpy as jnp
from jax import lax
from jax.experimental import pallas as pl
from jax.experimental.pallas import tpu as pltpu

F32 = jnp.float32
BF = jnp.bfloat16

D_MODEL = 1024
DEPTH = 2
PAST_LEN = 16384
GLA_H, GLA_DK, GLA_DV = 4, 32, 64
GLA_WIDTH = GLA_H * GLA_DV
GLA_GATE_RANK = 16
GLA_GATE_TEMP = 16.0
GLA_CHUNK = 16
RET_H, RET_DK, RET_DV = 4, 64, 64
RET_WIDTH = RET_H * RET_DV
ROPE_BASE = 10000.0
SSD_H, SSD_P, SSD_G, SSD_N = 8, 64, 2, 64
SSD_WIDTH = SSD_H * SSD_P
SSD_CONV_W = 4
SSD_CONV_DIM = SSD_WIDTH + 2 * SSD_G * SSD_N
MOE_GROUPS, MOE_PER_GROUP = 4, 4
MOE_EXPERTS = MOE_GROUPS * MOE_PER_GROUP
MOE_FF = 256
ALPHA = (2 * DEPTH) ** 0.25
EPS = 1e-5

LANE = 128
GLA_IN_W = 128 + 128 + 256 + LANE + 256
RET_IN_W = 4 * 256
SSD_IN_W = 512 + SSD_CONV_DIM + LANE
IN_W = GLA_IN_W + RET_IN_W + SSD_IN_W
VMEM_LIMIT = 56 * 1024 * 1024


def _cp(n_axes, vmem=VMEM_LIMIT):
    return pltpu.CompilerParams(dimension_semantics=("arbitrary",) * n_axes, vmem_limit_bytes=vmem)


def _dot(a, b):
    return jnp.dot(a, b, preferred_element_type=F32)


def _dot_nt(a, b):
    return lax.dot_general(a, b, (((1,), (1,)), ((), ())), preferred_element_type=F32)


def _dot_tn(a, b):
    return lax.dot_general(a, b, (((0,), (0,)), ((), ())), preferred_element_type=F32)


def _split3(x):
    hi = x.astype(BF)
    r = x - hi.astype(F32)
    mid = r.astype(BF)
    lo = (r - mid.astype(F32)).astype(BF)
    return hi, mid, lo


def _dot_x2(x, e):
    hi = x.astype(BF)
    lo = (x - hi.astype(F32)).astype(BF)
    return _dot(hi, e) + _dot(lo, e)


def _dot_3x(e, x):
    hi, mid, lo = _split3(x)
    return _dot(e, hi) + (_dot(e, mid) + _dot(e, lo))


def _sigmoid(x):
    return 1.0 / (1.0 + jnp.exp(-x))


def _silu(x):
    return x * _sigmoid(x)


def _log_sigmoid(x):
    return jnp.minimum(x, 0.0) - jnp.log(1.0 + jnp.exp(-jnp.abs(x)))


def _softplus(x):
    return jnp.maximum(x, 0.0) + jnp.log(1.0 + jnp.exp(-jnp.abs(x)))


def _layer_norm(x, g, b):
    mu = jnp.mean(x, axis=-1, keepdims=True)
    d = x - mu
    var = jnp.mean(d * d, axis=-1, keepdims=True)
    return d * lax.rsqrt(var + EPS) * g + b


def _const(shape):
    return pl.BlockSpec(shape, lambda *_: (0,) * len(shape))


def _resident_layer(shape, l):
    return pl.BlockSpec((None,) + tuple(shape[1:]), lambda *_: (l,) + (0,) * (len(shape) - 1),
                        pipeline_mode=pl.Buffered(1))


def _mod_kernel(c_ref, w_ref, b_ref, o_ref):
    s = _silu(c_ref[...]).astype(BF)
    o_ref[0] = _dot(s, w_ref[0].astype(BF)) + b_ref[0]


def _mod_call(c_all, w_ada, b_ada):
    R = c_all.shape[0]
    tn = 1536
    return pl.pallas_call(
        _mod_kernel,
        grid=(DEPTH, 6 * D_MODEL // tn),
        in_specs=[pl.BlockSpec((R, D_MODEL), lambda l, j: (0, 0)),
                  pl.BlockSpec((1, D_MODEL, tn), lambda l, j: (l, 0, j)),
                  pl.BlockSpec((1, 1, tn), lambda l, j: (l, 0, j))],
        out_specs=pl.BlockSpec((1, R, tn), lambda l, j: (l, 0, j)),
        out_shape=jax.ShapeDtypeStruct((DEPTH, R, 6 * D_MODEL), F32),
        compiler_params=_cp(2),
        name="ada_mod",
    )(c_all, w_ada, b_ada.reshape(DEPTH, 1, 6 * D_MODEL))


N_IN = 3096
N_GA = 128 + 128 + 256
N_DT = N_IN - SSD_H


def _inproj_kernel(x_ref, sc_ref, sh_ref, wt_ref, og_ref, or_ref, os_ref, w_ref):
    bB, bT, D = x_ref.shape

    @pl.when((pl.program_id(0) == 0) & (pl.program_id(1) == 0))
    def _():
        lane = lax.broadcasted_iota(jnp.int32, (1, LANE), 1)
        for j in range(N_GA // LANE):
            w_ref[:, j * LANE:(j + 1) * LANE] = wt_ref[j * LANE:(j + 1) * LANE, :].T.astype(BF)
        ga = wt_ref[N_GA:N_GA + LANE, :].T
        w_ref[:, N_GA:N_GA + LANE] = jnp.where(lane < GLA_GATE_RANK, ga, 0.0).astype(BF)
        src0, dst0 = N_GA + GLA_GATE_RANK, N_GA + LANE
        for j in range((N_DT - src0) // LANE):
            w_ref[:, dst0 + j * LANE:dst0 + (j + 1) * LANE] = \
                wt_ref[src0 + j * LANE:src0 + (j + 1) * LANE, :].T.astype(BF)
        dt = pltpu.roll(wt_ref[N_IN - LANE:N_IN, :].T, SSD_H, 1)
        w_ref[:, IN_W - LANE:IN_W] = jnp.where(lane < SSD_H, dt, 0.0).astype(BF)

    h = x_ref[...] * (1.0 + sc_ref[...]) + sh_ref[...]
    hb = h.reshape(bB * bT, D).astype(BF)
    cut = 2 * GLA_IN_W
    pa = _dot(hb, w_ref[:, 0:cut])
    pb = _dot(hb, w_ref[:, cut:IN_W])
    og_ref[...] = pa[:, 0:GLA_IN_W]
    or_ref[:, 0:cut - GLA_IN_W] = pa[:, GLA_IN_W:cut]
    or_ref[:, cut - GLA_IN_W:RET_IN_W] = pb[:, 0:GLA_IN_W + RET_IN_W - cut]
    os_ref[...] = pb[:, GLA_IN_W + RET_IN_W - cut:IN_W - cut]


def _tok_tiles(B, T):
    if T >= 512:
        return 1, 512
    return 512 // T, T


def _inproj_call(x3, sc, sh, wt, l):
    B, T, D = x3.shape
    bB, bT = _tok_tiles(B, T)
    nT = T // bT
    R = bB * bT
    N = B * T
    xmap = lambda i, j: (i, j, 0)
    mmap = lambda i, j: (i, 0, 0)
    omap = lambda i, j: (i * nT + j, 0)
    return pl.pallas_call(
        _inproj_kernel,
        grid=(B // bB, nT),
        in_specs=[pl.BlockSpec((bB, bT, D), xmap),
                  pl.BlockSpec((bB, 1, D), mmap),
                  pl.BlockSpec((bB, 1, D), mmap),
                  _resident_layer(wt.shape, l)],
        out_specs=[pl.BlockSpec((R, GLA_IN_W), omap),
                   pl.BlockSpec((R, RET_IN_W), omap),
                   pl.BlockSpec((R, SSD_IN_W), omap)],
        out_shape=[jax.ShapeDtypeStruct((N, GLA_IN_W), F32),
                   jax.ShapeDtypeStruct((N, RET_IN_W), F32),
                   jax.ShapeDtypeStruct((N, SSD_IN_W), F32)],
        scratch_shapes=[pltpu.VMEM((D, IN_W), BF)],
        compiler_params=_cp(2),
        name="in_proj",
    )(x3, sc, sh, wt)


def _head_block_mask(rows_per, cols_per, n):
    r = np.arange(rows_per * n)[:, None] // rows_per
    c = np.arange(cols_per * n)[None, :] // cols_per
    return (r == c).astype(np.float32)


def _block_tril(n, c):
    i = np.arange(n)[:, None]
    j = np.arange(n)[None, :]
    return ((i // c == j // c) & (j <= i)).astype(np.float32)


def _gla_front(x_ref, wg_ref, bg_ref, L_ref):
    q = x_ref[:, 0:128] * (GLA_DK ** -0.5)
    k = x_ref[:, 128:256]
    v = x_ref[:, 256:512]
    ga = x_ref[:, 512:640]
    r = x_ref[:, 640:896]
    gate = _dot(ga.astype(BF), wg_ref[...]) + bg_ref[...]
    la = _log_sigmoid(gate) * (1.0 / GLA_GATE_TEMP)
    n = L_ref.shape[0]
    g = jnp.concatenate([_dot_3x(L_ref[...], la[i:i + n, :]) for i in range(0, la.shape[0], n)], axis=0)
    return q, k, v, r, g


def _gla_intra(q, g, kp_ref, gp_ref, vp_ref, E_ref, c):
    TT = q.shape[0]
    PAD = kp_ref.shape[0] - TT
    pos = lax.broadcasted_iota(jnp.int32, (TT, 1), 0) & (c - 1)
    o = jnp.zeros((TT, 2 * LANE), F32)
    for s in range(min(c, 8)):
        ks = kp_ref[pl.ds(PAD - s, TT), :]
        gs = gp_ref[pl.ds(PAD - s, TT), :]
        vs = vp_ref[pl.ds(PAD - s, TT), :]
        w = jnp.where(pos >= s, q * ks * jnp.exp(g - gs), 0.0)
        o = o + _dot(w.astype(BF), E_ref[...]) * vs
    if c <= 8:
        return o
    assert c == 16
    nc = TT // c

    def upper(x):
        return x.reshape(nc, 2, 8, x.shape[-1])[:, 1].reshape(nc * 8, x.shape[-1])

    qu, gu = upper(q), upper(g)
    posu = lax.broadcasted_iota(jnp.int32, (nc * 8, 1), 0) & 7
    ou = jnp.zeros((nc * 8, 2 * LANE), F32)
    for s in range(8, c):
        ks = upper(kp_ref[pl.ds(PAD - s, TT), :])
        gs = upper(gp_ref[pl.ds(PAD - s, TT), :])
        vs = upper(vp_ref[pl.ds(PAD - s, TT), :])
        w = jnp.where(posu >= s - 8, qu * ks * jnp.exp(gu - gs), 0.0)
        ou = ou + _dot(w.astype(BF), E_ref[...]) * vs
    ou = ou.reshape(nc, 1, 8, 2 * LANE)
    return o + jnp.concatenate([jnp.zeros_like(ou), ou], axis=1).reshape(TT, 2 * LANE)


def _gla_norm_gate(o, r, nw_ref, EA_ref):
    ms = _dot_x2(o * o, EA_ref[...])
    return o * lax.rsqrt(ms + EPS) * nw_ref[...] * _silu(r)


def _gla_prompt_kernel(x_ref, wg_ref, bg_ref, nw_ref, L_ref, E_ref, EA_ref, M_ref,
                       o_ref, sfin_ref, st_ref, kp_ref, gp_ref, vp_ref, oi_ref, u_ref, sb_ref, *, c):
    TT = x_ref.shape[0]
    nc = TT // c
    PAD = kp_ref.shape[0] - TT
    t = pl.program_id(1)

    @pl.when(t == 0)
    def _():
        st_ref[...] = jnp.zeros_like(st_ref)

    q, k, v, r, g = _gla_front(x_ref, wg_ref, bg_ref, L_ref)
    kp_ref[0:PAD, :] = jnp.zeros((PAD, LANE), F32)
    gp_ref[0:PAD, :] = jnp.zeros((PAD, LANE), F32)
    vp_ref[0:PAD, :] = jnp.zeros((PAD, 2 * LANE), F32)
    kp_ref[PAD:PAD + TT, :] = k
    gp_ref[PAD:PAD + TT, :] = g
    vp_ref[PAD:PAD + TT, :] = v
    o = _gla_intra(q, g, kp_ref, gp_ref, vp_ref, E_ref, c)

    M = M_ref[...]
    gl_all = gp_ref[pl.ds(PAD + c - 1, nc, stride=c), :]
    for n in range(nc):
        lo = n * c
        ke = (k[lo:lo + c, :] * jnp.exp(gl_all[n:n + 1, :] - g[lo:lo + c, :])).astype(BF)
        u_ref[n] = _dot_tn(ke, v[lo:lo + c, :].astype(BF)) * M
    a_cols = jnp.concatenate([jnp.exp(gl_all), jnp.zeros((LANE - nc, LANE), F32)], axis=0).T
    S = st_ref[...]
    for n in range(nc):
        sb_ref[n] = S.astype(BF)
        S = a_cols[:, n:n + 1] * S + u_ref[n]
    st_ref[...] = S
    qe = (q * jnp.exp(g)).astype(BF)
    for n in range(nc):
        lo = n * c
        oi_ref[lo:lo + c, :] = _dot(qe[lo:lo + c, :], sb_ref[n])
    o = o + oi_ref[...]
    o_ref[...] = _gla_norm_gate(o, r, nw_ref, EA_ref).astype(o_ref.dtype)

    @pl.when(t == pl.num_programs(1) - 1)
    def _():
        for h in range(GLA_H):
            sfin_ref[0, h] = S[h * GLA_DK:(h + 1) * GLA_DK, h * GLA_DV:(h + 1) * GLA_DV]


def _gla_tables(TT, c):
    L = jnp.asarray(_block_tril(TT, c), BF)
    E = jnp.asarray(_head_block_mask(GLA_DK, GLA_DV, GLA_H), BF)
    EA = jnp.asarray(_head_block_mask(GLA_DV, GLA_DV, GLA_H) / GLA_DV, BF)
    M = jnp.asarray(_head_block_mask(GLA_DK, GLA_DV, GLA_H), F32)
    return L, E, EA, M


def _gla_params(w_gate, b_gate, norm_w):
    wg = jnp.zeros((LANE, GLA_H * GLA_DK), F32).at[:GLA_GATE_RANK].set(w_gate).astype(BF)
    return wg, b_gate.reshape(1, -1), norm_w.reshape(1, -1)


def _gla_prompt_call(gin, B, T, w_gate, b_gate, norm_w):
    TT, c = (1024 if T % 1024 == 0 else 512), GLA_CHUNK
    nT = T // TT
    L, E, EA, M = _gla_tables(min(TT, 256), c)
    wg, bg, nw = _gla_params(w_gate, b_gate, norm_w)
    PAD = 16
    return pl.pallas_call(
        functools.partial(_gla_prompt_kernel, c=c),
        grid=(B, nT),
        in_specs=[pl.BlockSpec((TT, GLA_IN_W), lambda b, t: (b * nT + t, 0)),
                  _const(wg.shape), _const(bg.shape), _const(nw.shape),
                  _const(L.shape), _const(E.shape), _const(EA.shape), _const(M.shape)],
        out_specs=[pl.BlockSpec((TT, GLA_WIDTH), lambda b, t: (b * nT + t, 0)),
                   pl.BlockSpec((1, GLA_H, GLA_DK, GLA_DV), lambda b, t: (b, 0, 0, 0))],
        out_shape=[jax.ShapeDtypeStruct((B * T, GLA_WIDTH), BF),
                   jax.ShapeDtypeStruct((B, GLA_H, GLA_DK, GLA_DV), F32)],
        scratch_shapes=[pltpu.VMEM((GLA_H * GLA_DK, GLA_H * GLA_DV), F32),
                        pltpu.VMEM((TT + PAD, LANE), F32),
                        pltpu.VMEM((TT + PAD, LANE), F32),
                        pltpu.VMEM((TT + PAD, 2 * LANE), F32),
                        pltpu.VMEM((TT, 2 * LANE), F32),
                        pltpu.VMEM((TT // c, GLA_H * GLA_DK, GLA_H * GLA_DV), F32),
                        pltpu.VMEM((TT // c, GLA_H * GLA_DK, GLA_H * GLA_DV), BF)],
        compiler_params=_cp(2),
        name="gla_prompt",
    )(gin, wg, bg, nw, L, E, EA, M)


def _rope(x, cos, sin_signed):
    lane = lax.broadcasted_iota(jnp.int32, (1, LANE), 1)
    first_half = (lane & (RET_DK - 1)) < RET_DK // 2
    out = []
    for p in range(2):
        xs = x[:, p * LANE:(p + 1) * LANE]
        up = pltpu.roll(xs, LANE - RET_DK // 2, 1)
        dn = pltpu.roll(xs, RET_DK // 2, 1)
        out.append(xs * cos + jnp.where(first_half, up, dn) * sin_signed)
    return jnp.concatenate(out, axis=1)


def _ret_front(x_ref, cos_ref, sin_ref, rows=slice(None)):
    cos, sin = cos_ref[rows, :], sin_ref[rows, :]
    q = _rope(x_ref[rows, 0:256], cos, sin)
    k = _rope(x_ref[rows, 256:512], cos, sin) * (RET_DK ** -0.5)
    v = x_ref[rows, 512:768]
    rg = x_ref[rows, 768:1024]
    return q, k, v, rg


def _ret_intra(q, k, v, D_ref):
    lane = lax.broadcasted_iota(jnp.int32, (1, RET_WIDTH), 1)
    qb, kb, vb = q.astype(BF), k.astype(BF), v.astype(BF)
    zero = jnp.zeros((), BF)
    o = jnp.zeros(q.shape, F32)
    for h in range(RET_H):
        hm = (lane // RET_DK) == h
        s = _dot_nt(jnp.where(hm, qb, zero), kb)
        p = (s * D_ref[h]).astype(BF)
        o = o + _dot(p, jnp.where(hm, vb, zero))
    return o


def _ret_norm_gate(o, rg, nw_ref, EA_ref):
    mu = _dot_x2(o, EA_ref[...])
    d = o - mu
    var = _dot_x2(d * d, EA_ref[...])
    return d * lax.rsqrt(var + EPS) * nw_ref[...] * _silu(rg)


def _ret_prompt_kernel(x_ref, cos_ref, sin_ref, D_ref, rd_ref, kd_ref, G_ref, M_ref, EA_ref, nw_ref,
                       o_ref, sfin_ref, st_ref):
    t = pl.program_id(1)

    @pl.when(t == 0)
    def _():
        st_ref[...] = jnp.zeros_like(st_ref)

    C = D_ref.shape[1]
    S = st_ref[...]
    for i in range(x_ref.shape[0] // C):
        rows = slice(i * C, (i + 1) * C)
        q, k, v, rg = _ret_front(x_ref, cos_ref, sin_ref, rows)
        o = _ret_intra(q, k, v, D_ref)
        o = o + _dot((q * rd_ref[...]).astype(BF), S.astype(BF))
        u = _dot_tn((k * kd_ref[...]).astype(BF), v.astype(BF))
        S = S * G_ref[...] + u * M_ref[...]
        o_ref[rows, :] = _ret_norm_gate(o, rg, nw_ref, EA_ref).astype(o_ref.dtype)
    st_ref[...] = S

    @pl.when(t == pl.num_programs(1) - 1)
    def _():
        for h in range(RET_H):
            sfin_ref[0, h] = S[h * RET_DK:(h + 1) * RET_DK, h * RET_DV:(h + 1) * RET_DV]


def _rope_tables(pos):
    half = RET_DK // 2
    inv = ROPE_BASE ** (-jnp.arange(half, dtype=F32) / half)
    ang = pos.astype(F32)[:, None] * inv[None, :]
    cos, sin = jnp.cos(ang), jnp.sin(ang)
    return jnp.tile(jnp.concatenate([cos, cos], 1), (1, 2)), jnp.tile(jnp.concatenate([-sin, sin], 1), (1, 2))


def _ret_log_gamma():
    return np.log(1.0 - 2.0 ** (-5.0 - np.arange(RET_H, dtype=np.float64)))


def _ret_prompt_call(rin, B, T, norm_w):
    C = 256
    TT = next(n * C for n in (8, 4, 1) if T % (n * C) == 0)
    nT = T // TT
    cos, sin = _rope_tables(jnp.arange(T, dtype=jnp.int32))
    lg = _ret_log_gamma()
    i = np.arange(C)
    dec = np.exp(lg[:, None, None] * (i[:, None] - i[None, :])[None]) * (i[:, None] >= i[None, :])[None]
    Dm = jnp.asarray(dec, F32)
    rd = jnp.asarray(np.repeat(np.exp(lg[None, :] * (i[:, None] + 1)), RET_DK, 1), F32)
    kd = jnp.asarray(np.repeat(np.exp(lg[None, :] * (C - 1 - i[:, None])), RET_DK, 1), F32)
    M = _head_block_mask(RET_DK, RET_DV, RET_H)
    G = jnp.asarray(M * np.repeat(np.exp(lg * C), RET_DK)[:, None], F32)
    M = jnp.asarray(M, F32)
    EA = jnp.asarray(_head_block_mask(RET_DV, RET_DV, RET_H) / RET_DV, BF)
    nw = norm_w.reshape(1, -1)
    return pl.pallas_call(
        _ret_prompt_kernel,
        grid=(B, nT),
        in_specs=[pl.BlockSpec((TT, RET_IN_W), lambda b, t: (b * nT + t, 0)),
                  pl.BlockSpec((TT, LANE), lambda b, t: (t, 0)),
                  pl.BlockSpec((TT, LANE), lambda b, t: (t, 0)),
                  _const(Dm.shape), _const(rd.shape), _const(kd.shape), _const(G.shape), _const(M.shape),
                  _const(EA.shape), _const(nw.shape)],
        out_specs=[pl.BlockSpec((TT, RET_WIDTH), lambda b, t: (b * nT + t, 0)),
                   pl.BlockSpec((1, RET_H, RET_DK, RET_DV), lambda b, t: (b, 0, 0, 0))],
        out_shape=[jax.ShapeDtypeStruct((B * T, RET_WIDTH), BF),
                   jax.ShapeDtypeStruct((B, RET_H, RET_DK, RET_DV), F32)],
        scratch_shapes=[pltpu.VMEM((RET_H * RET_DK, RET_H * RET_DV), F32)],
        compiler_params=_cp(2),
        name="ret_prompt",
    )(rin, cos, sin, Dm, rd, kd, G, M, EA, nw)


def _ssd_conv(xp_ref, cw_ref, cb_ref, TT):
    acc = cb_ref[...] + cw_ref[SSD_CONV_W - 1:SSD_CONV_W, :] * xp_ref[pl.ds(8, TT), :]
    for i in range(SSD_CONV_W - 1):
        acc = acc + cw_ref[i:i + 1, :] * xp_ref[pl.ds(8 - (SSD_CONV_W - 1) + i, TT), :]
    return acc


def _ssd_intra(xs, bm, cm, g, dt, Mk_ref):
    TT = xs.shape[0]
    rT = (g - jnp.log(dt)).T
    lane = lax.broadcasted_iota(jnp.int32, (1, LANE), 1)
    lane2 = lax.broadcasted_iota(jnp.int32, (1, 2 * LANE), 1)
    causal = Mk_ref[...] > 0.0
    bmb = bm.astype(BF)
    zero = jnp.zeros((), BF)
    o_parts = []
    for grp in range(SSD_G):
        cb = _dot_nt(jnp.where((lane // SSD_N) == grp, cm, 0.0).astype(BF), bmb).astype(BF)
        xg = xs[:, grp * 2 * LANE:(grp + 1) * 2 * LANE].astype(BF)
        og = jnp.zeros((TT, 2 * LANE), F32)
        for h4 in range(SSD_H // SSD_G):
            h = grp * (SSD_H // SSD_G) + h4
            dec = jnp.where(causal, jnp.exp(g[:, h:h + 1] - rT[h:h + 1, :]), 0.0)
            p = cb * dec.astype(BF)
            og = og + _dot(p, jnp.where((lane2 // SSD_P) == h4, xg, zero))
        o_parts.append(og)
    return jnp.concatenate(o_parts, axis=1)


def _ssd_prompt_kernel(x_ref, cw_ref, cb_ref, dtb_ref, alog_ref, dexp_ref, nw_ref, L_ref, Mk_ref, Eexp_ref, M2_ref,
                       res_ref, g1_ref, og_ref, or_ref, wo_ref, lg_ref, lb_ref,
                       o_ref, sfin_ref, cfin_ref, st_ref, xp_ref):
    TT = x_ref.shape[0]
    t = pl.program_id(1)

    @pl.when(t == 0)
    def _():
        st_ref[...] = jnp.zeros_like(st_ref)
        xp_ref[0:8, :] = jnp.zeros((8, SSD_CONV_DIM), F32)

    z = x_ref[:, 0:SSD_WIDTH]
    xp_ref[8:8 + TT, :] = x_ref[:, SSD_WIDTH:SSD_WIDTH + SSD_CONV_DIM]
    sdt = x_ref[:, SSD_WIDTH + SSD_CONV_DIM:SSD_IN_W]
    xbc = _silu(_ssd_conv(xp_ref, cw_ref, cb_ref, TT))
    tail = xp_ref[TT:TT + 8, :]
    xp_ref[0:8, :] = tail
    dt_all = _softplus(sdt + dtb_ref[...])
    la_all = dt_all * (-jnp.exp(alog_ref[...]))
    Eexp = Eexp_ref[...]
    C = L_ref.shape[0]
    S = st_ref[...]
    for i in range(TT // C):
        rows = slice(i * C, (i + 1) * C)
        xs = xbc[rows, 0:SSD_WIDTH]
        bm = xbc[rows, SSD_WIDTH:SSD_WIDTH + LANE]
        cm = xbc[rows, SSD_WIDTH + LANE:SSD_CONV_DIM]
        dt = dt_all[rows, :]
        g = _dot_3x(L_ref[...], la_all[rows, :])
        gl = g[C - 1:C, :]
        eg_x = _dot_x2(jnp.exp(g), Eexp)
        cw_x = _dot_x2(dt * jnp.exp(gl - g), Eexp)
        egl_x = _dot_x2(jnp.exp(gl), Eexp)

        o = _ssd_intra(xs, bm, cm, g, dt, Mk_ref)
        o = o + eg_x * _dot(cm.astype(BF), S.astype(BF))
        u = _dot_tn(bm.astype(BF), (xs * cw_x).astype(BF))
        S = S * egl_x + u * M2_ref[...]

        y = (o + dexp_ref[...] * xs) * _silu(z[rows, :])
        ms = jnp.mean(y * y, axis=-1, keepdims=True)
        o_ssd = (y * lax.rsqrt(ms + EPS) * nw_ref[...]).astype(BF)
        merged = jnp.concatenate([og_ref[rows, :], or_ref[rows, :], o_ssd], axis=1)
        mix = _dot(merged, wo_ref[...])
        o_ref[0, rows, :] = _layer_norm(ALPHA * res_ref[0, rows, :] + g1_ref[0] * mix, lg_ref[0], lb_ref[0])
    st_ref[...] = S

    @pl.when(t == pl.num_programs(1) - 1)
    def _():
        for h in range(SSD_H):
            gi = h // (SSD_H // SSD_G)
            sfin_ref[0, h] = S[gi * SSD_N:(gi + 1) * SSD_N, h * SSD_P:(h + 1) * SSD_P]
        cfin_ref[0] = tail[8 - (SSD_CONV_W - 1):8, :]


def _pad_lanes(v, n=LANE):
    v = v.reshape(1, -1)
    return jnp.zeros((1, n), F32).at[:, :v.shape[1]].set(v)


def _ssd_tables(TT, c):
    L = jnp.asarray(_block_tril(TT, c), BF)
    Mk = jnp.asarray(_block_tril(TT, c), F32)
    e = np.zeros((LANE, SSD_WIDTH), np.float32)
    for h in range(SSD_H):
        e[h, h * SSD_P:(h + 1) * SSD_P] = 1.0
    M2 = np.zeros((SSD_G * SSD_N, SSD_WIDTH), np.float32)
    for h in range(SSD_H):
        gi = h // (SSD_H // SSD_G)
        M2[gi * SSD_N:(gi + 1) * SSD_N, h * SSD_P:(h + 1) * SSD_P] = 1.0
    return L, Mk, jnp.asarray(e, BF), jnp.asarray(M2, F32)


def _ssd_params(conv_w, conv_b, dt_bias, a_log, d, norm_w):
    return (conv_w, conv_b.reshape(1, -1), _pad_lanes(dt_bias), _pad_lanes(a_log),
            jnp.repeat(d, SSD_P).reshape(1, -1), norm_w.reshape(1, -1))


def _ssd_prompt_call(sin_, x3, g1, og, orr, w_out, ln_g, ln_b, conv_w, conv_b, dt_bias, a_log, d, norm_w):
    B, T, D = x3.shape
    C = 256
    TT = 4 * C if T % (4 * C) == 0 else (2 * C if T % (2 * C) == 0 else C)
    nT = T // TT
    L, Mk, Eexp, M2 = _ssd_tables(C, C)
    prm = _ssd_params(conv_w, conv_b, dt_bias, a_log, d, norm_w)
    rmap = lambda b, t: (b * nT + t, 0)
    return pl.pallas_call(
        _ssd_prompt_kernel,
        grid=(B, nT),
        in_specs=[pl.BlockSpec((TT, SSD_IN_W), rmap)]
                 + [_const(p.shape) for p in prm]
                 + [_const(L.shape), _const(Mk.shape), _const(Eexp.shape), _const(M2.shape)]
                 + [pl.BlockSpec((1, TT, D), lambda b, t: (b, t, 0)),
                    pl.BlockSpec((1, 1, D), lambda b, t: (b, 0, 0)),
                    pl.BlockSpec((TT, GLA_WIDTH), rmap), pl.BlockSpec((TT, RET_WIDTH), rmap),
                    _const((D, D)), _const((1, 1, D)), _const((1, 1, D))],
        out_specs=[pl.BlockSpec((1, TT, D), lambda b, t: (b, t, 0)),
                   pl.BlockSpec((1, SSD_H, SSD_N, SSD_P), lambda b, t: (b, 0, 0, 0)),
                   pl.BlockSpec((1, SSD_CONV_W - 1, SSD_CONV_DIM), lambda b, t: (b, 0, 0))],
        out_shape=[jax.ShapeDtypeStruct((B, T, D), F32),
                   jax.ShapeDtypeStruct((B, SSD_H, SSD_N, SSD_P), F32),
                   jax.ShapeDtypeStruct((B, SSD_CONV_W - 1, SSD_CONV_DIM), F32)],
        scratch_shapes=[pltpu.VMEM((SSD_G * SSD_N, SSD_WIDTH), F32),
                        pltpu.VMEM((TT + 8, SSD_CONV_DIM), F32)],
        compiler_params=_cp(2),
        name="ssd_outproj_ln",
    )(sin_, *prm, L, Mk, Eexp, M2, x3, g1, og, orr, w_out, ln_g.reshape(1, 1, D), ln_b.reshape(1, 1, D))


def _inproj_t_kernel(x_ref, sc_ref, sh_ref, wt_ref, og_ref, or_ref, os_ref, w_ref):
    nt, nb, D = x_ref.shape

    @pl.when(pl.program_id(0) == 0)
    def _():
        for src, dst, n in ((0, 0, N_GA + GLA_GATE_RANK), (N_GA + GLA_GATE_RANK, N_GA + LANE, N_DT - N_GA - GLA_GATE_RANK)):
            for r in range(0, n, 512):
                m = min(512, n - r)
                w_ref[dst + r:dst + r + m, :] = wt_ref[src + r:src + r + m, :].astype(BF)
        w_ref[N_GA + GLA_GATE_RANK:N_GA + LANE, :] = jnp.zeros((LANE - GLA_GATE_RANK, D), BF)
        tail = jnp.concatenate([wt_ref[N_DT:N_IN, :], jnp.zeros((LANE - SSD_H, D), F32)], axis=0)
        w_ref[IN_W - LANE:IN_W, :] = tail.astype(BF)

    h = x_ref[...] * (1.0 + sc_ref[...]) + sh_ref[...]
    for t in range(0, nt, 2):
        ht = h[t:t + 2].reshape(2 * nb, D).astype(BF)
        cols = slice(t * nb, (t + 2) * nb)
        og_ref[:, cols] = _dot_nt(w_ref[0:GLA_IN_W, :], ht)
        or_ref[:, cols] = _dot_nt(w_ref[GLA_IN_W:GLA_IN_W + RET_IN_W, :], ht)
        os_ref[:, cols] = _dot_nt(w_ref[GLA_IN_W + RET_IN_W:IN_W, :], ht)


def _inproj_t_call(xt, sc, sh, wt, l):
    T, B, D = xt.shape
    nt = 4
    cmap = lambda i: (0, i)
    return pl.pallas_call(
        _inproj_t_kernel,
        grid=(T // nt,),
        in_specs=[pl.BlockSpec((nt, B, D), lambda i: (i, 0, 0)),
                  pl.BlockSpec((1, B, D), lambda i: (0, 0, 0)),
                  pl.BlockSpec((1, B, D), lambda i: (0, 0, 0)),
                  _resident_layer(wt.shape, l)],
        out_specs=[pl.BlockSpec((GLA_IN_W, nt * B), cmap),
                   pl.BlockSpec((RET_IN_W, nt * B), cmap),
                   pl.BlockSpec((SSD_IN_W, nt * B), cmap)],
        out_shape=[jax.ShapeDtypeStruct((GLA_IN_W, T * B), F32),
                   jax.ShapeDtypeStruct((RET_IN_W, T * B), F32),
                   jax.ShapeDtypeStruct((SSD_IN_W, T * B), F32)],
        scratch_shapes=[pltpu.VMEM((IN_W, D), BF)],
        compiler_params=_cp(1),
        name="in_proj_t",
    )(xt, sc, sh, wt)


def _row_sum(x):
    return jnp.sum(x, axis=0, keepdims=True)


def _lane_state_readout(o, coef_ref, s0_ref, n_rows):
    nb = LANE
    half = len(o) // 2
    for part in range(2):
        def body(k8, accs, part=part):
            accs = list(accs)
            base = pl.multiple_of(k8 * 8, 8)
            grp = [coef_ref[pl.ds(base, 8), (part * half + i) * nb:(part * half + i + 1) * nb] for i in range(half)]
            for j in range(8):
                s0k = s0_ref[0, k8 * 8 + j]
                for i in range(half):
                    accs[i] = accs[i] + grp[i][j:j + 1, :] * s0k
            return tuple(accs)

        res = lax.fori_loop(0, n_rows // 8, body, tuple(o[part * half:(part + 1) * half]))
        o[part * half:(part + 1) * half] = list(res)
    return o


def _lane_state_update(sn_ref, s0_ref, decay_fn, coef_ref, val_fn, n_rows, T):
    nb = LANE

    def body(k8, carry):
        base = pl.multiple_of(k8 * 8, 8)
        grp = [coef_ref[pl.ds(base, 8), t * nb:(t + 1) * nb] for t in range(T)]
        dec = decay_fn(base)
        for j in range(8):
            dj = dec[j:j + 1, :] if dec.shape[0] == 8 else dec
            sk = dj * s0_ref[0, k8 * 8 + j]
            for t in range(T):
                sk = sk + grp[t][j:j + 1, :] * val_fn(t)
            sn_ref[0, 0, k8 * 8 + j] = sk
        return carry

    lax.fori_loop(0, n_rows // 8, body, 0)


def _state_specs(shape, l, first):
    assert l == 0 or not first
    tail = tuple(shape[2:])
    in_spec = pl.BlockSpec((None, 1) + tail, lambda h: (l, h, 0, 0, 0))
    out_spec = pl.BlockSpec(((shape[0] if first else 1), 1) + tail, lambda h: (l, h, 0, 0, 0))
    return in_spec, out_spec


def _zero_later_layers(ref):
    ref[1:] = jnp.zeros((ref.shape[0] - 1,) + tuple(ref.shape[1:]), ref.dtype)


def _finish_state_call(kern, n_in, first, prevs):
    if first:
        return functools.partial(kern, first=True), [], {}
    wrapped = lambda *a, **kw: kern(*a[:n_in], *a[n_in + len(prevs):], first=False, **kw)
    specs = [pl.BlockSpec(memory_space=pl.ANY)] * len(prevs)
    return wrapped, specs, {n_in + i: 1 + i for i in range(len(prevs))}


def _gla_t_kernel(x_ref, s0_ref, wg_ref, bg_ref, nw_ref, o_ref, sn_ref, qe_ref, ke_ref, a_ref, *, T, first):
    nb = LANE
    if first:
        _zero_later_layers(sn_ref)
    h = pl.program_id(0)
    r0 = pl.multiple_of(h * GLA_DK, GLA_DK)
    v0 = pl.multiple_of(h * GLA_DV, GLA_DV)
    q = x_ref[pl.ds(r0, GLA_DK), :] * (GLA_DK ** -0.5)
    k = x_ref[pl.ds(128 + r0, GLA_DK), :]
    gate = _dot(wg_ref[pl.ds(r0, GLA_DK), :], x_ref[512:640, :].astype(BF)) + bg_ref[pl.ds(r0, GLA_DK), :]
    la = _log_sigmoid(gate) * (1.0 / GLA_GATE_TEMP)
    gs = []
    acc = jnp.zeros((GLA_DK, nb), F32)
    for t in range(T):
        acc = acc + la[:, t * nb:(t + 1) * nb]
        gs.append(acc)
    gl = gs[T - 1]
    a_ref[...] = jnp.exp(gl)
    qs = [q[:, t * nb:(t + 1) * nb] for t in range(T)]
    ks = [k[:, t * nb:(t + 1) * nb] for t in range(T)]
    for t in range(T):
        qe_ref[:, t * nb:(t + 1) * nb] = qs[t] * jnp.exp(gs[t])
        ke_ref[:, t * nb:(t + 1) * nb] = ks[t] * jnp.exp(gl - gs[t])

    def vt(t):
        return x_ref[pl.ds(256 + v0, GLA_DV), t * nb:(t + 1) * nb]

    o = []
    for t in range(T):
        ot = jnp.zeros((GLA_DV, nb), F32)
        for u in range(t + 1):
            s = _row_sum(qs[t] * ks[u] * jnp.exp(gs[t] - gs[u]))
            ot = ot + s * vt(u)
        o.append(ot)

    o = _lane_state_readout(o, qe_ref, s0_ref, GLA_DK)
    _lane_state_update(sn_ref, s0_ref, lambda base: a_ref[pl.ds(base, 8), :], ke_ref, vt, GLA_DK, T)

    nw = nw_ref[pl.ds(v0, GLA_DV), :]
    for t in range(T):
        ms = jnp.mean(o[t] * o[t], axis=0, keepdims=True)
        r = x_ref[pl.ds(640 + v0, GLA_DV), t * nb:(t + 1) * nb]
        o_ref[:, t * nb:(t + 1) * nb] = (o[t] * lax.rsqrt(ms + EPS) * nw * _silu(r)).astype(o_ref.dtype)


def _gla_t_call(gT, s0, prev, l, T, w_gate, b_gate, norm_w):
    N = gT.shape[1]
    wg = jnp.zeros((GLA_H * GLA_DK, LANE), F32).at[:, :GLA_GATE_RANK].set(w_gate.T).astype(BF)
    bg = b_gate.reshape(-1, 1)
    nw = norm_w.reshape(-1, 1)
    first = prev is None
    prevs = [] if first else [prev]
    s_in, s_out = _state_specs(s0.shape, l, first)
    ins = [gT, s0, wg, bg, nw]
    specs = [_const(gT.shape), s_in, _const(wg.shape), _const(bg.shape), _const(nw.shape)]
    kern, pspecs, aliases = _finish_state_call(functools.partial(_gla_t_kernel, T=T), len(ins), first, prevs)
    ins, specs = ins + prevs, specs + pspecs
    return pl.pallas_call(
        kern,
        grid=(GLA_H,),
        in_specs=specs,
        out_specs=[pl.BlockSpec((GLA_DV, N), lambda h: (h, 0)), s_out],
        out_shape=[jax.ShapeDtypeStruct((GLA_WIDTH, N), BF), jax.ShapeDtypeStruct(s0.shape, F32)],
        scratch_shapes=[pltpu.VMEM((GLA_DK, N), F32), pltpu.VMEM((GLA_DK, N), F32), pltpu.VMEM((GLA_DK, LANE), F32)],
        input_output_aliases=aliases,
        compiler_params=_cp(1),
        name="gla_t",
    )(*ins)


def _ret_t_kernel(x_ref, s0_ref, cos_ref, sin_ref, pw_ref, nw_ref, o_ref, sn_ref, qd_ref, kd_ref, *, T, first):
    nb = LANE
    if first:
        _zero_later_layers(sn_ref)
    h = pl.program_id(0)
    r0 = pl.multiple_of(h * RET_DK, RET_DK)
    half_k = RET_DK // 2
    cos, sin = cos_ref[...], sin_ref[...]

    def rope_t(base):
        x1 = x_ref[pl.ds(base + r0, half_k), :]
        x2 = x_ref[pl.ds(base + r0 + half_k, half_k), :]
        return jnp.concatenate([x1 * cos - x2 * sin, x1 * sin + x2 * cos], axis=0)

    q = rope_t(0)
    k = rope_t(256) * (RET_DK ** -0.5)
    pw = pw_ref[h]
    qs = [q[:, t * nb:(t + 1) * nb] for t in range(T)]
    ks = [k[:, t * nb:(t + 1) * nb] for t in range(T)]
    for t in range(T):
        qd_ref[:, t * nb:(t + 1) * nb] = qs[t] * pw[t + 1:t + 2, :]
        kd_ref[:, t * nb:(t + 1) * nb] = ks[t] * pw[T - 1 - t:T - t, :]

    def vt(t):
        return x_ref[pl.ds(512 + r0, RET_DV), t * nb:(t + 1) * nb]

    o = []
    for t in range(T):
        ot = jnp.zeros((RET_DV, nb), F32)
        for u in range(t + 1):
            s = _row_sum(qs[t] * ks[u]) * pw[t - u:t - u + 1, :]
            ot = ot + s * vt(u)
        o.append(ot)

    o = _lane_state_readout(o, qd_ref, s0_ref, RET_DK)
    _lane_state_update(sn_ref, s0_ref, lambda base: pw[T:T + 1, :], kd_ref, vt, RET_DK, T)

    nw = nw_ref[pl.ds(r0, RET_DV), :]
    for t in range(T):
        mu = jnp.mean(o[t], axis=0, keepdims=True)
        d = o[t] - mu
        var = jnp.mean(d * d, axis=0, keepdims=True)
        rg = x_ref[pl.ds(768 + r0, RET_DV), t * nb:(t + 1) * nb]
        o_ref[:, t * nb:(t + 1) * nb] = (d * lax.rsqrt(var + EPS) * nw * _silu(rg)).astype(o_ref.dtype)


def _ret_t_call(rT, s0, prev, l, T, norm_w):
    N = rT.shape[1]
    B = N // T
    half = RET_DK // 2
    inv = ROPE_BASE ** (-jnp.arange(half, dtype=F32) / half)
    ang = inv[:, None] * (PAST_LEN + jnp.arange(T, dtype=jnp.int32)).astype(F32)[None, :]
    cos = jnp.repeat(jnp.cos(ang), B, axis=1)
    sin = jnp.repeat(jnp.sin(ang), B, axis=1)
    lg = _ret_log_gamma()
    pw = jnp.asarray(np.repeat(np.exp(lg[:, None] * np.arange(16)[None, :])[:, :, None], LANE, axis=2), F32)
    nw = norm_w.reshape(-1, 1)
    first = prev is None
    prevs = [] if first else [prev]
    s_in, s_out = _state_specs(s0.shape, l, first)
    ins = [rT, s0, cos, sin, pw, nw]
    specs = [_const(rT.shape), s_in, _const(cos.shape), _const(sin.shape), _const(pw.shape), _const(nw.shape)]
    kern, pspecs, aliases = _finish_state_call(functools.partial(_ret_t_kernel, T=T), len(ins), first, prevs)
    ins, specs = ins + prevs, specs + pspecs
    return pl.pallas_call(
        kern,
        grid=(RET_H,),
        in_specs=specs,
        out_specs=[pl.BlockSpec((RET_DV, N), lambda h: (h, 0)), s_out],
        out_shape=[jax.ShapeDtypeStruct((RET_WIDTH, N), BF), jax.ShapeDtypeStruct(s0.shape, F32)],
        scratch_shapes=[pltpu.VMEM((RET_DK, N), F32), pltpu.VMEM((RET_DK, N), F32)],
        input_output_aliases=aliases,
        compiler_params=_cp(1),
        name="ret_t",
    )(*ins)


def _ssd_t_kernel(x_ref, c0_ref, s0_ref, cw_ref, cb_ref, dtb_ref, alog_ref, d_ref, nw_ref,
                  o_ref, sn_ref, cn_ref, hist_ref, y_ref, ssq_ref, cm_ref, bw_ref, xw_ref, *, T, first):
    nb = LANE
    W1 = SSD_CONV_W - 1
    h = pl.program_id(0)
    XB = SSD_WIDTH
    if first:
        _zero_later_layers(sn_ref)

    @pl.when(h == 0)
    def _():
        ssq_ref[...] = jnp.zeros_like(ssq_ref)
        if first:
            _zero_later_layers(cn_ref)
        for i in range(W1):
            for j in range(SSD_CONV_DIM // LANE):
                hist_ref[j * LANE:(j + 1) * LANE, i * nb:(i + 1) * nb] = c0_ref[0, i][:, j * LANE:(j + 1) * LANE].T
                cn_ref[0, i, :, j * LANE:(j + 1) * LANE] = \
                    x_ref[XB + j * LANE:XB + (j + 1) * LANE, (T - W1 + i) * nb:(T - W1 + i + 1) * nb].T

    def conv_rows(ro):
        w = cw_ref[pl.ds(ro, 64), :]
        b = cb_ref[pl.ds(ro, 64), :]
        xx = [hist_ref[pl.ds(ro, 64), i * nb:(i + 1) * nb] for i in range(W1)]
        xx += [x_ref[pl.ds(XB + ro, 64), t * nb:(t + 1) * nb] for t in range(T)]
        out = []
        for t in range(T):
            acc = b + w[:, 0:1] * xx[t]
            for i in range(1, SSD_CONV_W):
                acc = acc + w[:, i:i + 1] * xx[t + i]
            out.append(_silu(acc))
        return out

    grp = h // (SSD_H // SSD_G)
    xs = conv_rows(pl.multiple_of(h * SSD_P, SSD_P))
    bm = conv_rows(pl.multiple_of(SSD_WIDTH + grp * SSD_N, SSD_N))
    cm = conv_rows(pl.multiple_of(SSD_WIDTH + SSD_G * SSD_N + grp * SSD_N, SSD_N))

    dt_all = _softplus(x_ref[pl.ds(XB + SSD_CONV_DIM + h, 1), :] + dtb_ref[pl.ds(h, 1), :])
    a = -jnp.exp(alog_ref[pl.ds(h, 1), :])
    dts = [dt_all[:, t * nb:(t + 1) * nb] for t in range(T)]
    gs = []
    acc = jnp.zeros((1, nb), F32)
    for t in range(T):
        acc = acc + dts[t] * a
        gs.append(acc)
    gl = gs[T - 1]

    o = []
    for t in range(T):
        ot = jnp.zeros((SSD_P, nb), F32)
        for u in range(t + 1):
            s = _row_sum(cm[t] * bm[u]) * (jnp.exp(gs[t] - gs[u]) * dts[u])
            ot = ot + s * xs[u]
        o.append(ot)

    for t in range(T):
        cm_ref[:, t * nb:(t + 1) * nb] = cm[t] * jnp.exp(gs[t])
        bw_ref[:, t * nb:(t + 1) * nb] = bm[t]
        xw_ref[:, t * nb:(t + 1) * nb] = xs[t] * (dts[t] * jnp.exp(gl - gs[t]))

    o = _lane_state_readout(o, cm_ref, s0_ref, SSD_N)
    egl = jnp.exp(gl)
    _lane_state_update(sn_ref, s0_ref, lambda base: egl, bw_ref, lambda t: xw_ref[:, t * nb:(t + 1) * nb], SSD_N, T)

    dd = d_ref[pl.ds(h, 1), :]
    p0 = pl.multiple_of(h * SSD_P, SSD_P)
    for t in range(T):
        z = x_ref[pl.ds(p0, SSD_P), t * nb:(t + 1) * nb]
        y = (o[t] + dd * xs[t]) * _silu(z)
        y_ref[pl.ds(p0, SSD_P), t * nb:(t + 1) * nb] = y
        ssq_ref[:, t * nb:(t + 1) * nb] += _row_sum(y * y)

    @pl.when(h == SSD_H - 1)
    def _():
        scale = lax.rsqrt(ssq_ref[...] * (1.0 / SSD_WIDTH) + EPS)
        o_ref[...] = (y_ref[...] * scale * nw_ref[...]).astype(o_ref.dtype)


def _ssd_t_call(sT, c0, s0, prev_s, prev_c, l, T, conv_w, conv_b, dt_bias, a_log, d, norm_w):
    N = sT.shape[1]
    col = lambda v: jnp.zeros((LANE, 1), F32).at[:SSD_H, 0].set(v)
    prm = (conv_w.T, conv_b.reshape(-1, 1), col(dt_bias), col(a_log), col(d), norm_w.reshape(-1, 1))
    first = prev_s is None
    prevs = [] if first else [prev_s, prev_c]
    s_in, s_out = _state_specs(s0.shape, l, first)
    c_in = pl.BlockSpec((1,) + tuple(c0.shape[1:]), lambda h: (l, 0, 0, 0))
    c_out = pl.BlockSpec(((c0.shape[0] if first else 1),) + tuple(c0.shape[1:]), lambda h: (l, 0, 0, 0))
    ins = [sT, c0, s0, *prm]
    specs = [_const(sT.shape), c_in, s_in] + [_const(p.shape) for p in prm]
    kern, pspecs, aliases = _finish_state_call(functools.partial(_ssd_t_kernel, T=T), len(ins), first, prevs)
    ins, specs = ins + prevs, specs + pspecs
    return pl.pallas_call(
        kern,
        grid=(SSD_H,),
        in_specs=specs,
        out_specs=[_const((SSD_WIDTH, N)), s_out, c_out],
        out_shape=[jax.ShapeDtypeStruct((SSD_WIDTH, N), BF), jax.ShapeDtypeStruct(s0.shape, F32),
                   jax.ShapeDtypeStruct(c0.shape, F32)],
        scratch_shapes=[pltpu.VMEM((SSD_CONV_DIM, (SSD_CONV_W - 1) * LANE), F32),
                        pltpu.VMEM((SSD_WIDTH, N), F32), pltpu.VMEM((1, N), F32),
                        pltpu.VMEM((SSD_N, N), F32), pltpu.VMEM((SSD_N, N), F32), pltpu.VMEM((SSD_P, N), F32)],
        input_output_aliases=aliases,
        compiler_params=_cp(1),
        name="ssd_t",
    )(*ins)


def _outproj_t_kernel(x_ref, g_ref, og_ref, or_ref, os_ref, w_ref, lg_ref, lb_ref, o_ref):
    nt, nb, D = x_ref.shape
    for t in range(nt):
        cols = slice(t * nb, (t + 1) * nb)
        mix = (_dot_tn(og_ref[:, cols], w_ref[0:GLA_WIDTH, :])
               + _dot_tn(or_ref[:, cols], w_ref[GLA_WIDTH:GLA_WIDTH + RET_WIDTH, :])
               + _dot_tn(os_ref[:, cols], w_ref[GLA_WIDTH + RET_WIDTH:D, :]))
        y = ALPHA * x_ref[t] + g_ref[0] * mix
        o_ref[t] = _layer_norm(y, lg_ref[0], lb_ref[0])


def _outproj_t_call(xt, g1, ogT, orT, osT, w_out, ln_g, ln_b):
    T, B, D = xt.shape
    nt = 4
    cmap = lambda i: (0, i)
    return pl.pallas_call(
        _outproj_t_kernel,
        grid=(T // nt,),
        in_specs=[pl.BlockSpec((nt, B, D), lambda i: (i, 0, 0)),
                  pl.BlockSpec((1, B, D), lambda i: (0, 0, 0)),
                  pl.BlockSpec((GLA_WIDTH, nt * B), cmap),
                  pl.BlockSpec((RET_WIDTH, nt * B), cmap),
                  pl.BlockSpec((SSD_WIDTH, nt * B), cmap),
                  _const((D, D)), _const((1, 1, D)), _const((1, 1, D))],
        out_specs=pl.BlockSpec((nt, B, D), lambda i: (i, 0, 0)),
        out_shape=jax.ShapeDtypeStruct((T, B, D), F32),
        compiler_params=_cp(1),
        name="out_proj_ln_t",
    )(xt, g1, ogT, orT, osT, w_out, ln_g.reshape(1, 1, D), ln_b.reshape(1, 1, D))


ROUTE_OFF = 8


def _moe_route_t(lt):
    neg = jnp.float32(-jnp.inf)
    row8 = lax.broadcasted_iota(jnp.int32, (8, 1), 0)
    lg = jnp.where(row8 < MOE_GROUPS, lt[0:8, :], neg)
    mg = jnp.max(lg, axis=0, keepdims=True)
    gsel = jnp.min(jnp.where(lg == mg, row8, 8), axis=0, keepdims=True)
    g_gate = 1.0 / jnp.sum(jnp.exp(lg - mg), axis=0, keepdims=True)
    rowe = lax.broadcasted_iota(jnp.int32, (MOE_EXPERTS, 1), 0)
    le = jnp.where((rowe // MOE_PER_GROUP) == gsel, lt[ROUTE_OFF:ROUTE_OFF + MOE_EXPERTS, :], neg)
    m1 = jnp.max(le, axis=0, keepdims=True)
    i1 = jnp.min(jnp.where(le == m1, rowe, MOE_EXPERTS), axis=0, keepdims=True)
    le2 = jnp.where(rowe == i1, neg, le)
    m2 = jnp.max(le2, axis=0, keepdims=True)
    i2 = jnp.min(jnp.where(le2 == m2, rowe, MOE_EXPERTS), axis=0, keepdims=True)
    e2 = jnp.exp(m2 - m1)
    w1 = g_gate / (1.0 + e2)
    w2 = g_gate * e2 / (1.0 + e2)
    comb = jnp.where(rowe == i1, w1, jnp.where(rowe == i2, w2, 0.0))
    cg = comb[0:4, :]
    for g in range(1, MOE_GROUPS):
        cg = cg + comb[g * MOE_PER_GROUP:(g + 1) * MOE_PER_GROUP, :]
    return gsel, cg, comb


MOE_SUB = 256
MOE_BLK = 16
MOE_ROWS = 256
MOE_NPS = MOE_SUB + MOE_GROUPS * MOE_BLK
assert MOE_SUB <= MOE_ROWS


def _moe_kernel(x_ref, sc_ref, sh_ref, g_ref, wr_ref, br_ref, us_ref, w1_ref, w3_ref, w2_ref, lg_ref, lb_ref,
                o_ref, hb_ref, cwb_ref, hp_ref, cwp_ref, yp_ref, pos_ref,
                cgrp_ref, fill_ref, cur_ref, na_ref, nb_ref, so_ref, nfa_ref, dsa_ref, dsb_ref, *, n_steps):
    bB, bT, D = x_ref.shape
    R = bB * bT
    n_q = R // MOE_SUB
    s = pl.program_id(1)
    x = x_ref[...]
    row8 = lax.broadcasted_iota(jnp.int32, (8, 1), 0)
    slot = lax.broadcasted_iota(jnp.int32, (MOE_NPS, 1), 0).astype(F32)

    @pl.when((pl.program_id(0) == 0) & (s == 0))
    def _():
        hb_ref[...] = jnp.zeros_like(hb_ref)
        cwb_ref[...] = jnp.zeros_like(cwb_ref)
        yp_ref[...] = jnp.zeros_like(yp_ref)

    @pl.when(s == 0)
    def _():
        na_ref[0] = 0
        for g in range(MOE_GROUPS):
            cur_ref[g] = -1
            fill_ref[g] = 0

    @pl.when(s < n_steps)
    def _():
        h = (x * (1.0 + sc_ref[...]) + sh_ref[...]).reshape(R, D)
        segs, offs = [], []
        for q in range(n_q):
            u = s * n_q + q
            hq = h[q * MOE_SUB:(q + 1) * MOE_SUB, :].astype(BF)
            gsel, cg, _ = _moe_route_t(_dot_nt(wr_ref[...], hq) + br_ref[...])
            onehot = jnp.where(row8 == gsel, 1.0, 0.0)
            rank = _dot(onehot.astype(BF), us_ref[...])
            cnt = jnp.sum(onehot, axis=1, keepdims=True)
            seg = jnp.ceil(cnt * (1.0 / MOE_BLK)) * MOE_BLK
            off = jnp.zeros((8, 1), F32)
            for g in range(1, MOE_GROUPS):
                off = off + jnp.where(row8 >= g, seg[g - 1:g, :], 0.0)
            pos = jnp.sum(onehot * (off + rank), axis=0, keepdims=True)
            pos_ref[u] = jnp.broadcast_to(pos, (8, MOE_SUB))
            perm = jnp.where(slot == pos, 1.0, 0.0).astype(BF)
            hp_ref[q] = _dot(perm, hq).astype(BF)
            cg8 = jnp.concatenate([cg, jnp.zeros((4, MOE_SUB), F32)], axis=0)
            cg_hi = cg8.astype(BF)
            cg_lo = (cg8 - cg_hi.astype(F32)).astype(BF)
            cwp_ref[q] = _dot_nt(perm, cg_hi) + _dot_nt(perm, cg_lo)
            segs.append(seg)
            offs.append(off)
        for q in range(n_q):
            u = s * n_q + q
            for g in range(MOE_GROUPS):
                so = offs[q][g, 0].astype(jnp.int32)
                nb = (segs[q][g, 0] * (1.0 / MOE_BLK)).astype(jnp.int32)
                f = fill_ref[g]
                c = cur_ref[g]
                na = na_ref[0]
                room = jnp.where(c < 0, 0, (MOE_ROWS - f) // MOE_BLK)
                n_a = jnp.minimum(nb, room)
                n_b = nb - n_a
                base_a = c * MOE_ROWS + f
                base_b = na * MOE_ROWS
                idx = u * MOE_GROUPS + g
                so_ref[idx] = so
                nb_ref[idx] = nb
                nfa_ref[idx] = n_a
                dsa_ref[idx] = base_a
                dsb_ref[idx] = base_b

                def put(k, carry, so=so, q=q, n_a=n_a, base_a=base_a, base_b=base_b):
                    dst = pl.multiple_of(jnp.where(k < n_a, base_a + k * MOE_BLK, base_b + (k - n_a) * MOE_BLK), MOE_BLK)
                    src = pl.multiple_of(so + k * MOE_BLK, MOE_BLK)
                    hb_ref[pl.ds(dst, MOE_BLK), :] = hp_ref[q, pl.ds(src, MOE_BLK), :]
                    cwb_ref[pl.ds(dst, MOE_BLK), :] = cwp_ref[q, pl.ds(src, MOE_BLK), :]
                    return carry

                lax.fori_loop(0, nb, put, 0)

                @pl.when(n_b > 0)
                def _(g=g, na=na, n_b=n_b):
                    cgrp_ref[na] = g
                    na_ref[0] = na + 1
                    cur_ref[g] = na
                    fill_ref[g] = n_b * MOE_BLK

                @pl.when(n_b == 0)
                def _(g=g, f=f, n_a=n_a):
                    fill_ref[g] = f + n_a * MOE_BLK

    @pl.when(s == n_steps - 1)
    def _():
        def chunk(c, carry):
            g = cgrp_ref[c]
            start = pl.multiple_of(c * MOE_ROWS, MOE_ROWS)
            hc = hb_ref[pl.ds(start, MOE_ROWS), :]
            cw = cwb_ref[pl.ds(start, MOE_ROWS), :]
            hids = []
            for j in range(MOE_PER_GROUP):
                e = g * MOE_PER_GROUP + j
                hid = _silu(_dot(hc, w1_ref[e])) * _dot(hc, w3_ref[e]) * cw[:, j:j + 1]
                hids.append(hid.astype(BF))
            w2g = w2_ref[pl.ds(g * MOE_PER_GROUP, MOE_PER_GROUP)].reshape(MOE_PER_GROUP * MOE_FF, D)
            hb_ref[pl.ds(start, MOE_ROWS), :] = _dot(jnp.concatenate(hids, axis=1), w2g).astype(BF)
            return carry

        lax.fori_loop(0, na_ref[0], chunk, 0)

    @pl.when(s >= n_steps)
    def _():
        for q in range(n_q):
            u = (s - n_steps) * n_q + q
            for g in range(MOE_GROUPS):
                idx = u * MOE_GROUPS + g
                so, n_a, base_a, base_b = so_ref[idx], nfa_ref[idx], dsa_ref[idx], dsb_ref[idx]

                def take(k, carry, so=so, q=q, n_a=n_a, base_a=base_a, base_b=base_b):
                    src = pl.multiple_of(jnp.where(k < n_a, base_a + k * MOE_BLK, base_b + (k - n_a) * MOE_BLK), MOE_BLK)
                    dst = pl.multiple_of(so + k * MOE_BLK, MOE_BLK)
                    yp_ref[q, pl.ds(dst, MOE_BLK), :] = hb_ref[pl.ds(src, MOE_BLK), :]
                    return carry

                lax.fori_loop(0, nb_ref[idx], take, 0)
        ys = []
        for q in range(n_q):
            u = (s - n_steps) * n_q + q
            perm = jnp.where(slot == pos_ref[u][0:1, :], 1.0, 0.0).astype(BF)
            ys.append(_dot_tn(perm, yp_ref[q]))
        y = jnp.concatenate(ys, axis=0)
        z = ALPHA * x + g_ref[...] * y.reshape(bB, bT, D)
        o_ref[...] = _layer_norm(z, lg_ref[...], lb_ref[...])


def _moe_call(x3, sc, sh, g2, wr, br, w1, w3, w2, l, ln_g, ln_b):
    B, T, D = x3.shape
    bB, bT = _tok_tiles(B, T)
    R = bB * bT
    if bB == 1:
        spp = 2 if B % 2 == 0 else 1
        nT = T // bT
        n_pools, n_steps = B // spp, spp * nT
        xmap = lambda p, s: (p * spp + (s % n_steps) // nT, (s % n_steps) % nT, 0)
        omap = lambda p, s: (p * spp + jnp.maximum(s - n_steps, 0) // nT, jnp.maximum(s - n_steps, 0) % nT, 0)
        mmap = lambda p, s: (p * spp + (s % n_steps) // nT, 0, 0)
        mshape = (1, 1, D)
    else:
        n_pools, n_steps = 1, B // bB
        xmap = lambda p, s: (s % n_steps, 0, 0)
        omap = lambda p, s: (jnp.maximum(s - n_steps, 0), 0, 0)
        mmap = lambda p, s: (0, 0, 0)
        mshape = (1, bT, D)
    n_sub = n_steps * (R // MOE_SUB)
    n_chunks = pl.cdiv(n_sub * (MOE_SUB + MOE_GROUPS * (MOE_BLK - 1)), MOE_ROWS) + MOE_GROUPS
    us = jnp.asarray(np.triu(np.ones((MOE_SUB, MOE_SUB), np.float32), 1), BF)
    smem = lambda n: pltpu.SMEM((n,), jnp.int32)
    return pl.pallas_call(
        functools.partial(_moe_kernel, n_steps=n_steps),
        grid=(n_pools, 2 * n_steps),
        in_specs=[pl.BlockSpec((bB, bT, D), xmap),
                  pl.BlockSpec(mshape, mmap), pl.BlockSpec(mshape, mmap), pl.BlockSpec(mshape, mmap),
                  _const(wr.shape), _const(br.shape), _const(us.shape),
                  _resident_layer(w1.shape, l), _resident_layer(w3.shape, l), _resident_layer(w2.shape, l),
                  _const((1, 1, D)), _const((1, 1, D))],
        out_specs=pl.BlockSpec((bB, bT, D), omap),
        out_shape=jax.ShapeDtypeStruct((B, T, D), F32),
        scratch_shapes=[pltpu.VMEM((n_chunks * MOE_ROWS, D), BF), pltpu.VMEM((n_chunks * MOE_ROWS, 8), F32),
                        pltpu.VMEM((R // MOE_SUB, MOE_NPS, D), BF), pltpu.VMEM((R // MOE_SUB, MOE_NPS, 8), F32),
                        pltpu.VMEM((R // MOE_SUB, MOE_NPS, D), BF),
                        pltpu.VMEM((n_sub, 8, MOE_SUB), F32),
                        smem(n_chunks), smem(MOE_GROUPS), smem(MOE_GROUPS), smem(1),
                        *[smem(n_sub * MOE_GROUPS) for _ in range(5)]],
        compiler_params=_cp(2),
        name="moe_ln",
    )(x3, sc, sh, g2, wr, br, us, w1, w3, w2, ln_g.reshape(1, 1, D), ln_b.reshape(1, 1, D))


def _router_params(w_group, b_group, w_expert, b_expert):
    wr = jnp.zeros((LANE, D_MODEL), F32).at[:MOE_GROUPS].set(w_group.T)
    wr = wr.at[ROUTE_OFF:ROUTE_OFF + MOE_EXPERTS].set(w_expert.T)
    br = jnp.zeros((LANE, 1), F32).at[:MOE_GROUPS, 0].set(b_group).at[ROUTE_OFF:ROUTE_OFF + MOE_EXPERTS, 0].set(b_expert)
    return wr.astype(BF), br


def kernel(x_prompt, x_sample, c_prompt, c_sample, state_gla, state_ret, state_ssd, state_conv, w_ada, b_ada, w_in, gla_w_gate, gla_b_gate, gla_norm, ret_norm, ssd_conv_w, ssd_conv_b, ssd_dt_bias, ssd_a_log, ssd_d, ssd_norm, w_out, ln1_g, ln1_b, moe_w_group, moe_b_group, moe_w_expert, moe_b_expert, moe_w1, moe_w3, moe_w2, ln2_g, ln2_b):
    Bp, Tp, D = x_prompt.shape
    Bs, Ts, _ = x_sample.shape
    w_in_t = jnp.swapaxes(w_in, 1, 2)
    w_out_b = w_out.astype(BF)
    w1_b, w3_b, w2_b = moe_w1.astype(BF), moe_w3.astype(BF), moe_w2.astype(BF)

    mod = _mod_call(jnp.concatenate([c_prompt, c_sample], axis=0), w_ada, b_ada)

    def moe(x, sc2, sh2, g2, l):
        wr, br = _router_params(moe_w_group[l], moe_b_group[l], moe_w_expert[l], moe_b_expert[l])
        return _moe_call(x, sc2, sh2, g2, wr, br, w1_b, w3_b, w2_b, l, ln2_g[l], ln2_b[l])

    x = x_prompt
    new = [[], [], [], []]
    for l in range(DEPTH):
        sh1, sc1, g1, sh2, sc2, g2 = (mod[l, :Bp, None, i * D:(i + 1) * D] for i in range(6))
        gin, rin, sin_ = _inproj_call(x, sc1, sh1, w_in_t, l)
        og, s_gla = _gla_prompt_call(gin, Bp, Tp, gla_w_gate[l], gla_b_gate[l], gla_norm[l])
        orr, s_ret = _ret_prompt_call(rin, Bp, Tp, ret_norm[l])
        x, s_ssd, s_conv = _ssd_prompt_call(sin_, x, g1, og, orr, w_out_b[l], ln1_g[l], ln1_b[l], ssd_conv_w[l],
                                            ssd_conv_b[l], ssd_dt_bias[l], ssd_a_log[l], ssd_d[l], ssd_norm[l])
        x = moe(x, sc2, sh2, g2, l)
        for acc, s in zip(new, (s_gla, s_ret, s_ssd, s_conv)):
            acc.append(s)
    y_p = x
    gla_p, ret_p, ssd_p, conv_p = (jnp.stack(a) for a in new)

    x = jnp.swapaxes(x_sample, 0, 1)
    sg = jnp.transpose(state_gla, (0, 2, 3, 4, 1))
    sr = jnp.transpose(state_ret, (0, 2, 3, 4, 1))
    ss = jnp.transpose(state_ssd, (0, 2, 3, 4, 1))
    cv = jnp.transpose(state_conv, (0, 2, 1, 3))
    gla_n = ret_n = ssd_n = conv_n = None
    for l in range(DEPTH):
        sh1, sc1, g1, sh2, sc2, g2 = (mod[l, None, Bp:, i * D:(i + 1) * D] for i in range(6))
        gT, rT, sT = _inproj_t_call(x, sc1, sh1, w_in_t, l)
        ogT, gla_n = _gla_t_call(gT, sg, gla_n, l, Ts, gla_w_gate[l], gla_b_gate[l], gla_norm[l])
        orT, ret_n = _ret_t_call(rT, sr, ret_n, l, Ts, ret_norm[l])
        osT, ssd_n, conv_n = _ssd_t_call(sT, cv, ss, ssd_n, conv_n, l, Ts, ssd_conv_w[l], ssd_conv_b[l],
                                         ssd_dt_bias[l], ssd_a_log[l], ssd_d[l], ssd_norm[l])
        x = _outproj_t_call(x, g1, ogT, orT, osT, w_out_b[l], ln1_g[l], ln1_b[l])
        x = moe(x, sc2, sh2, g2, l)
    y_s = jnp.swapaxes(x, 0, 1)
    gla_s = jnp.transpose(gla_n, (0, 4, 1, 2, 3))
    ret_s = jnp.transpose(ret_n, (0, 4, 1, 2, 3))
    ssd_s = jnp.transpose(ssd_n, (0, 4, 1, 2, 3))
    conv_s = jnp.transpose(conv_n, (0, 2, 1, 3))
    return (y_p, y_s, gla_p, ret_p, ssd_p, conv_p, gla_s, ret_s, ssd_s, conv_s)
```

```python
import functools

import numpy as np
import jax
import jax.numpy as jnp
from jax import lax
from jax.experimental import pallas as pl
from jax.experimental.pallas import tpu as pltpu

F32 = jnp.float32
BF = jnp.bfloat16

D_MODEL = 1024
DEPTH = 2
PAST_LEN = 16384
GLA_H, GLA_DK, GLA_DV = 4, 32, 64
GLA_WIDTH = GLA_H * GLA_DV
GLA_GATE_RANK = 16
GLA_GATE_TEMP = 16.0
GLA_CHUNK = 16
RET_H, RET_DK, RET_DV = 4, 64, 64
RET_WIDTH = RET_H * RET_DV
ROPE_BASE = 10000.0
SSD_H, SSD_P, SSD_G, SSD_N = 8, 64, 2, 64
SSD_WIDTH = SSD_H * SSD_P
SSD_CONV_W = 4
SSD_CONV_DIM = SSD_WIDTH + 2 * SSD_G * SSD_N
MOE_GROUPS, MOE_PER_GROUP = 4, 4
MOE_EXPERTS = MOE_GROUPS * MOE_PER_GROUP
MOE_FF = 256
ALPHA = (2 * DEPTH) ** 0.25
EPS = 1e-5
LOG2E = 1.4426950408889634

LANE = 128
GLA_IN_W = 128 + 128 + 256 + LANE + 256
RET_IN_W = 4 * 256
SSD_IN_W = 512 + SSD_CONV_DIM + LANE
IN_W = GLA_IN_W + RET_IN_W + SSD_IN_W
VMEM_LIMIT = 56 * 1024 * 1024


def _cp(n_axes, vmem=VMEM_LIMIT):
    return pltpu.CompilerParams(dimension_semantics=("arbitrary",) * n_axes, vmem_limit_bytes=vmem)


def _dot(a, b):
    return jnp.dot(a, b, preferred_element_type=F32)


def _dot_nt(a, b):
    return lax.dot_general(a, b, (((1,), (1,)), ((), ())), preferred_element_type=F32)


def _dot_tn(a, b):
    return lax.dot_general(a, b, (((0,), (0,)), ((), ())), preferred_element_type=F32)


def _split3(x):
    hi = x.astype(BF)
    r = x - hi.astype(F32)
    mid = r.astype(BF)
    lo = (r - mid.astype(F32)).astype(BF)
    return hi, mid, lo


def _dot_x2(x, e):
    hi = x.astype(BF)
    lo = (x - hi.astype(F32)).astype(BF)
    return _dot(hi, e) + _dot(lo, e)


def _dot_3x(e, x):
    hi, mid, lo = _split3(x)
    return _dot(e, hi) + (_dot(e, mid) + _dot(e, lo))


def _sigmoid(x):
    return 1.0 / (1.0 + jnp.exp(-x))


def _silu(x):
    return x * _sigmoid(x)


def _log_sigmoid(x):
    return jnp.minimum(x, 0.0) - jnp.log(1.0 + jnp.exp(-jnp.abs(x)))


def _softplus(x):
    return jnp.maximum(x, 0.0) + jnp.log(1.0 + jnp.exp(-jnp.abs(x)))


def _layer_norm(x, g, b):
    mu = jnp.mean(x, axis=-1, keepdims=True)
    d = x - mu
    var = jnp.mean(d * d, axis=-1, keepdims=True)
    return d * lax.rsqrt(var + EPS) * g + b


def _const(shape):
    return pl.BlockSpec(shape, lambda *_: (0,) * len(shape))


def _resident_layer(shape, l):
    return pl.BlockSpec((None,) + tuple(shape[1:]), lambda *_: (l,) + (0,) * (len(shape) - 1),
                        pipeline_mode=pl.Buffered(1))


def _mod_kernel(c_ref, w_ref, b_ref, o_ref):
    s = _silu(c_ref[...]).astype(BF)
    o_ref[0] = _dot(s, w_ref[0].astype(BF)) + b_ref[0]


def _mod_call(c_all, w_ada, b_ada):
    R = c_all.shape[0]
    tn = 1536
    return pl.pallas_call(
        _mod_kernel,
        grid=(DEPTH, 6 * D_MODEL // tn),
        in_specs=[pl.BlockSpec((R, D_MODEL), lambda l, j: (0, 0)),
                  pl.BlockSpec((1, D_MODEL, tn), lambda l, j: (l, 0, j)),
                  pl.BlockSpec((1, 1, tn), lambda l, j: (l, 0, j))],
        out_specs=pl.BlockSpec((1, R, tn), lambda l, j: (l, 0, j)),
        out_shape=jax.ShapeDtypeStruct((DEPTH, R, 6 * D_MODEL), F32),
        compiler_params=_cp(2),
        name="ada_mod",
    )(c_all, w_ada, b_ada.reshape(DEPTH, 1, 6 * D_MODEL))


N_IN = 3096
N_GA = 128 + 128 + 256
N_DT = N_IN - SSD_H


def _inproj_kernel(x_ref, sc_ref, sh_ref, wt_ref, og_ref, or_ref, os_ref, w_ref):
    bB, bT, D = x_ref.shape

    @pl.when((pl.program_id(0) == 0) & (pl.program_id(1) == 0))
    def _():
        lane = lax.broadcasted_iota(jnp.int32, (1, LANE), 1)
        for j in range(N_GA // LANE):
            w_ref[:, j * LANE:(j + 1) * LANE] = wt_ref[j * LANE:(j + 1) * LANE, :].T.astype(BF)
        ga = wt_ref[N_GA:N_GA + LANE, :].T
        w_ref[:, N_GA:N_GA + LANE] = jnp.where(lane < GLA_GATE_RANK, ga, 0.0).astype(BF)
        src0, dst0 = N_GA + GLA_GATE_RANK, N_GA + LANE
        for j in range((N_DT - src0) // LANE):
            w_ref[:, dst0 + j * LANE:dst0 + (j + 1) * LANE] = \
                wt_ref[src0 + j * LANE:src0 + (j + 1) * LANE, :].T.astype(BF)
        dt = pltpu.roll(wt_ref[N_IN - LANE:N_IN, :].T, SSD_H, 1)
        w_ref[:, IN_W - LANE:IN_W] = jnp.where(lane < SSD_H, dt, 0.0).astype(BF)

    h = x_ref[...] * (1.0 + sc_ref[...]) + sh_ref[...]
    hb = h.reshape(bB * bT, D).astype(BF)
    cut = 2 * GLA_IN_W
    pa = _dot(hb, w_ref[:, 0:cut])
    pb = _dot(hb, w_ref[:, cut:IN_W])
    og_ref[...] = pa[:, 0:GLA_IN_W]
    or_ref[:, 0:cut - GLA_IN_W] = pa[:, GLA_IN_W:cut]
    or_ref[:, cut - GLA_IN_W:RET_IN_W] = pb[:, 0:GLA_IN_W + RET_IN_W - cut]
    os_ref[...] = pb[:, GLA_IN_W + RET_IN_W - cut:IN_W - cut]


def _tok_tiles(B, T):
    if T >= 512:
        return 1, 512
    return 512 // T, T


def _inproj_call(x3, sc, sh, wt, l):
    B, T, D = x3.shape
    bB, bT = _tok_tiles(B, T)
    nT = T // bT
    R = bB * bT
    N = B * T
    xmap = lambda i, j: (i, j, 0)
    mmap = lambda i, j: (i, 0, 0)
    omap = lambda i, j: (i * nT + j, 0)
    return pl.pallas_call(
        _inproj_kernel,
        grid=(B // bB, nT),
        in_specs=[pl.BlockSpec((bB, bT, D), xmap),
                  pl.BlockSpec((bB, 1, D), mmap),
                  pl.BlockSpec((bB, 1, D), mmap),
                  _resident_layer(wt.shape, l)],
        out_specs=[pl.BlockSpec((R, GLA_IN_W), omap),
                   pl.BlockSpec((R, RET_IN_W), omap),
                   pl.BlockSpec((R, SSD_IN_W), omap)],
        out_shape=[jax.ShapeDtypeStruct((N, GLA_IN_W), F32),
                   jax.ShapeDtypeStruct((N, RET_IN_W), F32),
                   jax.ShapeDtypeStruct((N, SSD_IN_W), F32)],
        scratch_shapes=[pltpu.VMEM((D, IN_W), BF)],
        compiler_params=_cp(2),
        name="in_proj",
    )(x3, sc, sh, wt)


def _head_block_mask(rows_per, cols_per, n):
    r = np.arange(rows_per * n)[:, None] // rows_per
    c = np.arange(cols_per * n)[None, :] // cols_per
    return (r == c).astype(np.float32)


def _block_tril(n, c):
    i = np.arange(n)[:, None]
    j = np.arange(n)[None, :]
    return ((i // c == j // c) & (j <= i)).astype(np.float32)


def _gla_front(x_ref, wg_ref, bg_ref, L_ref):
    q = x_ref[:, 0:128] * (GLA_DK ** -0.5)
    k = x_ref[:, 128:256]
    v = x_ref[:, 256:512]
    ga = x_ref[:, 512:640]
    r = x_ref[:, 640:896]
    gate = _dot(ga.astype(BF), wg_ref[...]) + bg_ref[...]
    la = _log_sigmoid(gate) * (LOG2E / GLA_GATE_TEMP)
    n = L_ref.shape[0]
    g = jnp.concatenate([_dot_3x(L_ref[...], la[i:i + n, :]) for i in range(0, la.shape[0], n)], axis=0)
    return q, k, v, r, g


def _gla_intra(q, g, kp_ref, gp_ref, vp_ref, E_ref, c):
    TT = q.shape[0]
    PAD = kp_ref.shape[0] - TT
    pos = lax.broadcasted_iota(jnp.int32, (TT, 1), 0) & (c - 1)
    o = jnp.zeros((TT, 2 * LANE), F32)
    for s in range(min(c, 8)):
        ks = kp_ref[pl.ds(PAD - s, TT), :]
        gs = gp_ref[pl.ds(PAD - s, TT), :]
        vs = vp_ref[pl.ds(PAD - s, TT), :]
        w = jnp.where(pos >= s, q * ks * jnp.exp2(g - gs), 0.0)
        o = o + _dot(w.astype(BF), E_ref[...]) * vs
    if c <= 8:
        return o
    assert c == 16
    nc = TT // c

    def upper(x):
        return x.reshape(nc, 2, 8, x.shape[-1])[:, 1].reshape(nc * 8, x.shape[-1])

    qu, gu = upper(q), upper(g)
    posu = lax.broadcasted_iota(jnp.int32, (nc * 8, 1), 0) & 7
    ou = jnp.zeros((nc * 8, 2 * LANE), F32)
    for s in range(8, c):
        ks = upper(kp_ref[pl.ds(PAD - s, TT), :])
        gs = upper(gp_ref[pl.ds(PAD - s, TT), :])
        vs = upper(vp_ref[pl.ds(PAD - s, TT), :])
        w = jnp.where(posu >= s - 8, qu * ks * jnp.exp2(gu - gs), 0.0)
        ou = ou + _dot(w.astype(BF), E_ref[...]) * vs
    ou = ou.reshape(nc, 1, 8, 2 * LANE)
    return o + jnp.concatenate([jnp.zeros_like(ou), ou], axis=1).reshape(TT, 2 * LANE)


def _gla_norm_gate(o, r, nw_ref, EA_ref):
    ms = _dot_x2(o * o, EA_ref[...])
    return o * lax.rsqrt(ms + EPS) * nw_ref[...] * _silu(r)


def _gla_prompt_kernel(x_ref, wg_ref, bg_ref, nw_ref, L_ref, E_ref, EA_ref, M_ref,
                       o_ref, sfin_ref, st_ref, kp_ref, gp_ref, vp_ref, oi_ref, u_ref, sb_ref, *, c):
    TT = x_ref.shape[0]
    nc = TT // c
    PAD = kp_ref.shape[0] - TT
    t = pl.program_id(1)

    @pl.when(t == 0)
    def _():
        st_ref[...] = jnp.zeros_like(st_ref)

    q, k, v, r, g = _gla_front(x_ref, wg_ref, bg_ref, L_ref)
    kp_ref[0:PAD, :] = jnp.zeros((PAD, LANE), F32)
    gp_ref[0:PAD, :] = jnp.zeros((PAD, LANE), F32)
    vp_ref[0:PAD, :] = jnp.zeros((PAD, 2 * LANE), F32)
    kp_ref[PAD:PAD + TT, :] = k
    gp_ref[PAD:PAD + TT, :] = g
    vp_ref[PAD:PAD + TT, :] = v
    o = _gla_intra(q, g, kp_ref, gp_ref, vp_ref, E_ref, c)

    M = M_ref[...]
    gl_all = gp_ref[pl.ds(PAD + c - 1, nc, stride=c), :]
    for n in range(nc):
        lo = n * c
        ke = (k[lo:lo + c, :] * jnp.exp2(gl_all[n:n + 1, :] - g[lo:lo + c, :])).astype(BF)
        u_ref[n] = _dot_tn(ke, v[lo:lo + c, :].astype(BF)) * M
    a_cols = jnp.concatenate([jnp.exp2(gl_all), jnp.zeros((LANE - nc, LANE), F32)], axis=0).T
    S = st_ref[...]
    for n in range(nc):
        sb_ref[n] = S.astype(BF)
        S = a_cols[:, n:n + 1] * S + u_ref[n]
    st_ref[...] = S
    qe = (q * jnp.exp2(g)).astype(BF)
    for n in range(nc):
        lo = n * c
        oi_ref[lo:lo + c, :] = _dot(qe[lo:lo + c, :], sb_ref[n])
    o = o + oi_ref[...]
    o_ref[...] = _gla_norm_gate(o, r, nw_ref, EA_ref).astype(o_ref.dtype)

    @pl.when(t == pl.num_programs(1) - 1)
    def _():
        for h in range(GLA_H):
            sfin_ref[0, h] = S[h * GLA_DK:(h + 1) * GLA_DK, h * GLA_DV:(h + 1) * GLA_DV]


def _gla_tables(TT, c):
    L = jnp.asarray(_block_tril(TT, c), BF)
    E = jnp.asarray(_head_block_mask(GLA_DK, GLA_DV, GLA_H), BF)
    EA = jnp.asarray(_head_block_mask(GLA_DV, GLA_DV, GLA_H) / GLA_DV, BF)
    M = jnp.asarray(_head_block_mask(GLA_DK, GLA_DV, GLA_H), F32)
    return L, E, EA, M


def _gla_params(w_gate, b_gate, norm_w):
    wg = jnp.zeros((LANE, GLA_H * GLA_DK), F32).at[:GLA_GATE_RANK].set(w_gate).astype(BF)
    return wg, b_gate.reshape(1, -1), norm_w.reshape(1, -1)


def _gla_prompt_call(gin, B, T, w_gate, b_gate, norm_w):
    TT, c = (1024 if T % 1024 == 0 else 512), GLA_CHUNK
    nT = T // TT
    L, E, EA, M = _gla_tables(min(TT, 256), c)
    wg, bg, nw = _gla_params(w_gate, b_gate, norm_w)
    PAD = 16
    return pl.pallas_call(
        functools.partial(_gla_prompt_kernel, c=c),
        grid=(B, nT),
        in_specs=[pl.BlockSpec((TT, GLA_IN_W), lambda b, t: (b * nT + t, 0)),
                  _const(wg.shape), _const(bg.shape), _const(nw.shape),
                  _const(L.shape), _const(E.shape), _const(EA.shape), _const(M.shape)],
        out_specs=[pl.BlockSpec((TT, GLA_WIDTH), lambda b, t: (b * nT + t, 0)),
                   pl.BlockSpec((1, GLA_H, GLA_DK, GLA_DV), lambda b, t: (b, 0, 0, 0))],
        out_shape=[jax.ShapeDtypeStruct((B * T, GLA_WIDTH), BF),
                   jax.ShapeDtypeStruct((B, GLA_H, GLA_DK, GLA_DV), F32)],
        scratch_shapes=[pltpu.VMEM((GLA_H * GLA_DK, GLA_H * GLA_DV), F32),
                        pltpu.VMEM((TT + PAD, LANE), F32),
                        pltpu.VMEM((TT + PAD, LANE), F32),
                        pltpu.VMEM((TT + PAD, 2 * LANE), F32),
                        pltpu.VMEM((TT, 2 * LANE), F32),
                        pltpu.VMEM((TT // c, GLA_H * GLA_DK, GLA_H * GLA_DV), F32),
                        pltpu.VMEM((TT // c, GLA_H * GLA_DK, GLA_H * GLA_DV), BF)],
        compiler_params=_cp(2),
        name="gla_prompt",
    )(gin, wg, bg, nw, L, E, EA, M)


def _rope(x, cos, sin_signed):
    lane = lax.broadcasted_iota(jnp.int32, (1, LANE), 1)
    first_half = (lane & (RET_DK - 1)) < RET_DK // 2
    out = []
    for p in range(2):
        xs = x[:, p * LANE:(p + 1) * LANE]
        up = pltpu.roll(xs, LANE - RET_DK // 2, 1)
        dn = pltpu.roll(xs, RET_DK // 2, 1)
        out.append(xs * cos + jnp.where(first_half, up, dn) * sin_signed)
    return jnp.concatenate(out, axis=1)


def _ret_front(x_ref, cos_ref, sin_ref, rows=slice(None)):
    cos, sin = cos_ref[rows, :], sin_ref[rows, :]
    q = _rope(x_ref[rows, 0:256], cos, sin)
    k = _rope(x_ref[rows, 256:512], cos, sin) * (RET_DK ** -0.5)
    v = x_ref[rows, 512:768]
    rg = x_ref[rows, 768:1024]
    return q, k, v, rg


def _ret_intra(q, k, v, D_ref):
    lane = lax.broadcasted_iota(jnp.int32, (1, RET_WIDTH), 1)
    qb, kb, vb = q.astype(BF), k.astype(BF), v.astype(BF)
    zero = jnp.zeros((), BF)
    o = jnp.zeros(q.shape, F32)
    for h in range(RET_H):
        hm = (lane // RET_DK) == h
        s = _dot_nt(jnp.where(hm, qb, zero), kb)
        p = (s * D_ref[h]).astype(BF)
        o = o + _dot(p, jnp.where(hm, vb, zero))
    return o


def _ret_norm_gate(o, rg, nw_ref, EA_ref):
    mu = _dot_x2(o, EA_ref[...])
    d = o - mu
    var = _dot_x2(d * d, EA_ref[...])
    return d * lax.rsqrt(var + EPS) * nw_ref[...] * _silu(rg)


def _ret_prompt_kernel(x_ref, cos_ref, sin_ref, D_ref, rd_ref, kd_ref, G_ref, M_ref, EA_ref, nw_ref,
                       o_ref, sfin_ref, st_ref):
    t = pl.program_id(1)

    @pl.when(t == 0)
    def _():
        st_ref[...] = jnp.zeros_like(st_ref)

    C = D_ref.shape[1]
    S = st_ref[...]
    for i in range(x_ref.shape[0] // C):
        rows = slice(i * C, (i + 1) * C)
        q, k, v, rg = _ret_front(x_ref, cos_ref, sin_ref, rows)
        o = _ret_intra(q, k, v, D_ref)
        o = o + _dot((q * rd_ref[...]).astype(BF), S.astype(BF))
        u = _dot_tn((k * kd_ref[...]).astype(BF), v.astype(BF))
        S = S * G_ref[...] + u * M_ref[...]
        o_ref[rows, :] = _ret_norm_gate(o, rg, nw_ref, EA_ref).astype(o_ref.dtype)
    st_ref[...] = S

    @pl.when(t == pl.num_programs(1) - 1)
    def _():
        for h in range(RET_H):
            sfin_ref[0, h] = S[h * RET_DK:(h + 1) * RET_DK, h * RET_DV:(h + 1) * RET_DV]


def _rope_tables(pos):
    half = RET_DK // 2
    inv = ROPE_BASE ** (-jnp.arange(half, dtype=F32) / half)
    ang = pos.astype(F32)[:, None] * inv[None, :]
    cos, sin = jnp.cos(ang), jnp.sin(ang)
    return jnp.tile(jnp.concatenate([cos, cos], 1), (1, 2)), jnp.tile(jnp.concatenate([-sin, sin], 1), (1, 2))


def _ret_log_gamma():
    return np.log(1.0 - 2.0 ** (-5.0 - np.arange(RET_H, dtype=np.float64)))


def _ret_prompt_call(rin, B, T, norm_w):
    C = 256
    TT = next(n * C for n in (8, 4, 1) if T % (n * C) == 0)
    nT = T // TT
    cos, sin = _rope_tables(jnp.arange(T, dtype=jnp.int32))
    lg = _ret_log_gamma()
    i = np.arange(C)
    dec = np.exp(lg[:, None, None] * (i[:, None] - i[None, :])[None]) * (i[:, None] >= i[None, :])[None]
    Dm = jnp.asarray(dec, F32)
    rd = jnp.asarray(np.repeat(np.exp(lg[None, :] * (i[:, None] + 1)), RET_DK, 1), F32)
    kd = jnp.asarray(np.repeat(np.exp(lg[None, :] * (C - 1 - i[:, None])), RET_DK, 1), F32)
    M = _head_block_mask(RET_DK, RET_DV, RET_H)
    G = jnp.asarray(M * np.repeat(np.exp(lg * C), RET_DK)[:, None], F32)
    M = jnp.asarray(M, F32)
    EA = jnp.asarray(_head_block_mask(RET_DV, RET_DV, RET_H) / RET_DV, BF)
    nw = norm_w.reshape(1, -1)
    return pl.pallas_call(
        _ret_prompt_kernel,
        grid=(B, nT),
        in_specs=[pl.BlockSpec((TT, RET_IN_W), lambda b, t: (b * nT + t, 0)),
                  pl.BlockSpec((TT, LANE), lambda b, t: (t, 0)),
                  pl.BlockSpec((TT, LANE), lambda b, t: (t, 0)),
                  _const(Dm.shape), _const(rd.shape), _const(kd.shape), _const(G.shape), _const(M.shape),
                  _const(EA.shape), _const(nw.shape)],
        out_specs=[pl.BlockSpec((TT, RET_WIDTH), lambda b, t: (b * nT + t, 0)),
                   pl.BlockSpec((1, RET_H, RET_DK, RET_DV), lambda b, t: (b, 0, 0, 0))],
        out_shape=[jax.ShapeDtypeStruct((B * T, RET_WIDTH), BF),
                   jax.ShapeDtypeStruct((B, RET_H, RET_DK, RET_DV), F32)],
        scratch_shapes=[pltpu.VMEM((RET_H * RET_DK, RET_H * RET_DV), F32)],
        compiler_params=_cp(2),
        name="ret_prompt",
    )(rin, cos, sin, Dm, rd, kd, G, M, EA, nw)


def _ssd_conv(xp_ref, cw_ref, cb_ref, TT):
    acc = cb_ref[...] + cw_ref[SSD_CONV_W - 1:SSD_CONV_W, :] * xp_ref[pl.ds(8, TT), :]
    for i in range(SSD_CONV_W - 1):
        acc = acc + cw_ref[i:i + 1, :] * xp_ref[pl.ds(8 - (SSD_CONV_W - 1) + i, TT), :]
    return acc


def _ssd_intra(xs, bm, cm, g, dt, Mk_ref):
    TT = xs.shape[0]
    g2 = g * LOG2E
    rT = ((g - jnp.log(dt)) * LOG2E).T
    lane = lax.broadcasted_iota(jnp.int32, (1, LANE), 1)
    lane2 = lax.broadcasted_iota(jnp.int32, (1, 2 * LANE), 1)
    zero = jnp.zeros((), BF)
    causal = Mk_ref[...].astype(BF) > zero
    bmb = bm.astype(BF)
    o_parts = []
    for grp in range(SSD_G):
        cb = _dot_nt(jnp.where((lane // SSD_N) == grp, cm, 0.0).astype(BF), bmb).astype(BF)
        xg = xs[:, grp * 2 * LANE:(grp + 1) * 2 * LANE].astype(BF)
        og = jnp.zeros((TT, 2 * LANE), F32)
        for h4 in range(SSD_H // SSD_G):
            h = grp * (SSD_H // SSD_G) + h4
            dec = jnp.exp2(g2[:, h:h + 1] - rT[h:h + 1, :]).astype(BF)
            p = jnp.where(causal, cb * dec, zero)
            og = og + _dot(p, jnp.where((lane2 // SSD_P) == h4, xg, zero))
        o_parts.append(og)
    return jnp.concatenate(o_parts, axis=1)


def _ssd_prompt_kernel(x_ref, cw_ref, cb_ref, dtb_ref, alog_ref, dexp_ref, nw_ref, L_ref, Mk_ref, Eexp_ref, M2_ref,
                       res_ref, g1_ref, og_ref, or_ref, wo_ref, lg_ref, lb_ref,
                       o_ref, sfin_ref, cfin_ref, st_ref, xp_ref):
    TT = x_ref.shape[0]
    t = pl.program_id(1)

    @pl.when(t == 0)
    def _():
        st_ref[...] = jnp.zeros_like(st_ref)
        xp_ref[0:8, :] = jnp.zeros((8, SSD_CONV_DIM), F32)

    z = x_ref[:, 0:SSD_WIDTH]
    xp_ref[8:8 + TT, :] = x_ref[:, SSD_WIDTH:SSD_WIDTH + SSD_CONV_DIM]
    sdt = x_ref[:, SSD_WIDTH + SSD_CONV_DIM:SSD_IN_W]
    xbc = _silu(_ssd_conv(xp_ref, cw_ref, cb_ref, TT))
    tail = xp_ref[TT:TT + 8, :]
    xp_ref[0:8, :] = tail
    dt_all = _softplus(sdt + dtb_ref[...])
    la_all = dt_all * (-jnp.exp(alog_ref[...]))
    Eexp = Eexp_ref[...]
    C = L_ref.shape[0]
    S = st_ref[...]
    for i in range(TT // C):
        rows = slice(i * C, (i + 1) * C)
        xs = xbc[rows, 0:SSD_WIDTH]
        bm = xbc[rows, SSD_WIDTH:SSD_WIDTH + LANE]
        cm = xbc[rows, SSD_WIDTH + LANE:SSD_CONV_DIM]
        dt = dt_all[rows, :]
        g = _dot_3x(L_ref[...], la_all[rows, :])
        gl = g[C - 1:C, :]
        eg_x = _dot_x2(jnp.exp(g), Eexp)
        cw_x = _dot_x2(dt * jnp.exp(gl - g), Eexp)
        egl_x = _dot_x2(jnp.exp(gl), Eexp)

        o = _ssd_intra(xs, bm, cm, g, dt, Mk_ref)
        o = o + eg_x * _dot(cm.astype(BF), S.astype(BF))
        u = _dot_tn(bm.astype(BF), (xs * cw_x).astype(BF))
        S = S * egl_x + u * M2_ref[...]

        y = (o + dexp_ref[...] * xs) * _silu(z[rows, :])
        ms = jnp.mean(y * y, axis=-1, keepdims=True)
        o_ssd = (y * lax.rsqrt(ms + EPS) * nw_ref[...]).astype(BF)
        merged = jnp.concatenate([og_ref[rows, :], or_ref[rows, :], o_ssd], axis=1)
        mix = _dot(merged, wo_ref[...])
        o_ref[0, rows, :] = _layer_norm(ALPHA * res_ref[0, rows, :] + g1_ref[0] * mix, lg_ref[0], lb_ref[0])
    st_ref[...] = S

    @pl.when(t == pl.num_programs(1) - 1)
    def _():
        for h in range(SSD_H):
            gi = h // (SSD_H // SSD_G)
            sfin_ref[0, h] = S[gi * SSD_N:(gi + 1) * SSD_N, h * SSD_P:(h + 1) * SSD_P]
        cfin_ref[0] = tail[8 - (SSD_CONV_W - 1):8, :]


def _pad_lanes(v, n=LANE):
    v = v.reshape(1, -1)
    return jnp.zeros((1, n), F32).at[:, :v.shape[1]].set(v)


def _ssd_tables(TT, c):
    L = jnp.asarray(_block_tril(TT, c), BF)
    Mk = jnp.asarray(_block_tril(TT, c), F32)
    e = np.zeros((LANE, SSD_WIDTH), np.float32)
    for h in range(SSD_H):
        e[h, h * SSD_P:(h + 1) * SSD_P] = 1.0
    M2 = np.zeros((SSD_G * SSD_N, SSD_WIDTH), np.float32)
    for h in range(SSD_H):
        gi = h // (SSD_H // SSD_G)
        M2[gi * SSD_N:(gi + 1) * SSD_N, h * SSD_P:(h + 1) * SSD_P] = 1.0
    return L, Mk, jnp.asarray(e, BF), jnp.asarray(M2, F32)


def _ssd_params(conv_w, conv_b, dt_bias, a_log, d, norm_w):
    return (conv_w, conv_b.reshape(1, -1), _pad_lanes(dt_bias), _pad_lanes(a_log),
            jnp.repeat(d, SSD_P).reshape(1, -1), norm_w.reshape(1, -1))


def _ssd_prompt_call(sin_, x3, g1, og, orr, w_out, ln_g, ln_b, conv_w, conv_b, dt_bias, a_log, d, norm_w):
    B, T, D = x3.shape
    C = 256
    TT = 4 * C if T % (4 * C) == 0 else (2 * C if T % (2 * C) == 0 else C)
    nT = T // TT
    L, Mk, Eexp, M2 = _ssd_tables(C, C)
    prm = _ssd_params(conv_w, conv_b, dt_bias, a_log, d, norm_w)
    rmap = lambda b, t: (b * nT + t, 0)
    return pl.pallas_call(
        _ssd_prompt_kernel,
        grid=(B, nT),
        in_specs=[pl.BlockSpec((TT, SSD_IN_W), rmap)]
                 + [_const(p.shape) for p in prm]
                 + [_const(L.shape), _const(Mk.shape), _const(Eexp.shape), _const(M2.shape)]
                 + [pl.BlockSpec((1, TT, D), lambda b, t: (b, t, 0)),
                    pl.BlockSpec((1, 1, D), lambda b, t: (b, 0, 0)),
                    pl.BlockSpec((TT, GLA_WIDTH), rmap), pl.BlockSpec((TT, RET_WIDTH), rmap),
                    _const((D, D)), _const((1, 1, D)), _const((1, 1, D))],
        out_specs=[pl.BlockSpec((1, TT, D), lambda b, t: (b, t, 0)),
                   pl.BlockSpec((1, SSD_H, SSD_N, SSD_P), lambda b, t: (b, 0, 0, 0)),
                   pl.BlockSpec((1, SSD_CONV_W - 1, SSD_CONV_DIM), lambda b, t: (b, 0, 0))],
        out_shape=[jax.ShapeDtypeStruct((B, T, D), F32),
                   jax.ShapeDtypeStruct((B, SSD_H, SSD_N, SSD_P), F32),
                   jax.ShapeDtypeStruct((B, SSD_CONV_W - 1, SSD_CONV_DIM), F32)],
        scratch_shapes=[pltpu.VMEM((SSD_G * SSD_N, SSD_WIDTH), F32),
                        pltpu.VMEM((TT + 8, SSD_CONV_DIM), F32)],
        compiler_params=_cp(2),
        name="ssd_outproj_ln",
    )(sin_, *prm, L, Mk, Eexp, M2, x3, g1, og, orr, w_out, ln_g.reshape(1, 1, D), ln_b.reshape(1, 1, D))


def _inproj_t_kernel(x_ref, sc_ref, sh_ref, wt_ref, og_ref, or_ref, os_ref, w_ref):
    nt, nb, D = x_ref.shape

    @pl.when(pl.program_id(0) == 0)
    def _():
        for src, dst, n in ((0, 0, N_GA + GLA_GATE_RANK), (N_GA + GLA_GATE_RANK, N_GA + LANE, N_DT - N_GA - GLA_GATE_RANK)):
            for r in range(0, n, 512):
                m = min(512, n - r)
                w_ref[dst + r:dst + r + m, :] = wt_ref[src + r:src + r + m, :].astype(BF)
        w_ref[N_GA + GLA_GATE_RANK:N_GA + LANE, :] = jnp.zeros((LANE - GLA_GATE_RANK, D), BF)
        tail = jnp.concatenate([wt_ref[N_DT:N_IN, :], jnp.zeros((LANE - SSD_H, D), F32)], axis=0)
        w_ref[IN_W - LANE:IN_W, :] = tail.astype(BF)

    h = x_ref[...] * (1.0 + sc_ref[...]) + sh_ref[...]
    for t in range(0, nt, 2):
        ht = h[t:t + 2].reshape(2 * nb, D).astype(BF)
        cols = slice(t * nb, (t + 2) * nb)
        og_ref[:, cols] = _dot_nt(w_ref[0:GLA_IN_W, :], ht)
        or_ref[:, cols] = _dot_nt(w_ref[GLA_IN_W:GLA_IN_W + RET_IN_W, :], ht)
        os_ref[:, cols] = _dot_nt(w_ref[GLA_IN_W + RET_IN_W:IN_W, :], ht)


def _inproj_t_call(xt, sc, sh, wt, l):
    T, B, D = xt.shape
    nt = 4
    cmap = lambda i: (0, i)
    return pl.pallas_call(
        _inproj_t_kernel,
        grid=(T // nt,),
        in_specs=[pl.BlockSpec((nt, B, D), lambda i: (i, 0, 0)),
                  pl.BlockSpec((1, B, D), lambda i: (0, 0, 0)),
                  pl.BlockSpec((1, B, D), lambda i: (0, 0, 0)),
                  _resident_layer(wt.shape, l)],
        out_specs=[pl.BlockSpec((GLA_IN_W, nt * B), cmap),
                   pl.BlockSpec((RET_IN_W, nt * B), cmap),
                   pl.BlockSpec((SSD_IN_W, nt * B), cmap)],
        out_shape=[jax.ShapeDtypeStruct((GLA_IN_W, T * B), F32),
                   jax.ShapeDtypeStruct((RET_IN_W, T * B), F32),
                   jax.ShapeDtypeStruct((SSD_IN_W, T * B), F32)],
        scratch_shapes=[pltpu.VMEM((IN_W, D), BF)],
        compiler_params=_cp(1),
        name="in_proj_t",
    )(xt, sc, sh, wt)


def _row_sum(x):
    return jnp.sum(x, axis=0, keepdims=True)


def _lane_state_readout(o, coef_ref, s0_ref, n_rows):
    nb = LANE
    half = len(o) // 2
    for part in range(2):
        def body(k8, accs, part=part):
            accs = list(accs)
            base = pl.multiple_of(k8 * 8, 8)
            grp = [coef_ref[pl.ds(base, 8), (part * half + i) * nb:(part * half + i + 1) * nb] for i in range(half)]
            for j in range(8):
                s0k = s0_ref[0, k8 * 8 + j]
                for i in range(half):
                    accs[i] = accs[i] + grp[i][j:j + 1, :] * s0k
            return tuple(accs)

        res = lax.fori_loop(0, n_rows // 8, body, tuple(o[part * half:(part + 1) * half]))
        o[part * half:(part + 1) * half] = list(res)
    return o


def _lane_state_update(sn_ref, s0_ref, decay_fn, coef_ref, val_fn, n_rows, T):
    nb = LANE

    def body(k8, carry):
        base = pl.multiple_of(k8 * 8, 8)
        grp = [coef_ref[pl.ds(base, 8), t * nb:(t + 1) * nb] for t in range(T)]
        dec = decay_fn(base)
        for j in range(8):
            dj = dec[j:j + 1, :] if dec.shape[0] == 8 else dec
            sk = dj * s0_ref[0, k8 * 8 + j]
            for t in range(T):
                sk = sk + grp[t][j:j + 1, :] * val_fn(t)
            sn_ref[0, 0, k8 * 8 + j] = sk
        return carry

    lax.fori_loop(0, n_rows // 8, body, 0)


def _state_specs(shape, l, first):
    assert l == 0 or not first
    tail = tuple(shape[2:])
    in_spec = pl.BlockSpec((None, 1) + tail, lambda h: (l, h, 0, 0, 0))
    out_spec = pl.BlockSpec(((shape[0] if first else 1), 1) + tail, lambda h: (l, h, 0, 0, 0))
    return in_spec, out_spec


def _zero_later_layers(ref):
    ref[1:] = jnp.zeros((ref.shape[0] - 1,) + tuple(ref.shape[1:]), ref.dtype)


def _finish_state_call(kern, n_in, first, prevs):
    if first:
        return functools.partial(kern, first=True), [], {}
    wrapped = lambda *a, **kw: kern(*a[:n_in], *a[n_in + len(prevs):], first=False, **kw)
    specs = [pl.BlockSpec(memory_space=pl.ANY)] * len(prevs)
    return wrapped, specs, {n_in + i: 1 + i for i in range(len(prevs))}


def _gla_t_kernel(x_ref, s0_ref, wg_ref, bg_ref, nw_ref, o_ref, sn_ref, qe_ref, ke_ref, a_ref, *, T, first):
    nb = LANE
    if first:
        _zero_later_layers(sn_ref)
    h = pl.program_id(0)
    r0 = pl.multiple_of(h * GLA_DK, GLA_DK)
    v0 = pl.multiple_of(h * GLA_DV, GLA_DV)
    q = x_ref[pl.ds(r0, GLA_DK), :] * (GLA_DK ** -0.5)
    k = x_ref[pl.ds(128 + r0, GLA_DK), :]
    gate = _dot(wg_ref[pl.ds(r0, GLA_DK), :], x_ref[512:640, :].astype(BF)) + bg_ref[pl.ds(r0, GLA_DK), :]
    la = _log_sigmoid(gate) * (1.0 / GLA_GATE_TEMP)
    gs = []
    acc = jnp.zeros((GLA_DK, nb), F32)
    for t in range(T):
        acc = acc + la[:, t * nb:(t + 1) * nb]
        gs.append(acc)
    gl = gs[T - 1]
    a_ref[...] = jnp.exp(gl)
    qs = [q[:, t * nb:(t + 1) * nb] for t in range(T)]
    ks = [k[:, t * nb:(t + 1) * nb] for t in range(T)]
    for t in range(T):
        qe_ref[:, t * nb:(t + 1) * nb] = qs[t] * jnp.exp(gs[t])
        ke_ref[:, t * nb:(t + 1) * nb] = ks[t] * jnp.exp(gl - gs[t])

    def vt(t):
        return x_ref[pl.ds(256 + v0, GLA_DV), t * nb:(t + 1) * nb]

    o = []
    for t in range(T):
        ot = jnp.zeros((GLA_DV, nb), F32)
        for u in range(t + 1):
            s = _row_sum(qs[t] * ks[u] * jnp.exp(gs[t] - gs[u]))
            ot = ot + s * vt(u)
        o.append(ot)

    o = _lane_state_readout(o, qe_ref, s0_ref, GLA_DK)
    _lane_state_update(sn_ref, s0_ref, lambda base: a_ref[pl.ds(base, 8), :], ke_ref, vt, GLA_DK, T)

    nw = nw_ref[pl.ds(v0, GLA_DV), :]
    for t in range(T):
        ms = jnp.mean(o[t] * o[t], axis=0, keepdims=True)
        r = x_ref[pl.ds(640 + v0, GLA_DV), t * nb:(t + 1) * nb]
        o_ref[:, t * nb:(t + 1) * nb] = (o[t] * lax.rsqrt(ms + EPS) * nw * _silu(r)).astype(o_ref.dtype)


def _gla_t_call(gT, s0, prev, l, T, w_gate, b_gate, norm_w):
    N = gT.shape[1]
    wg = jnp.zeros((GLA_H * GLA_DK, LANE), F32).at[:, :GLA_GATE_RANK].set(w_gate.T).astype(BF)
    bg = b_gate.reshape(-1, 1)
    nw = norm_w.reshape(-1, 1)
    first = prev is None
    prevs = [] if first else [prev]
    s_in, s_out = _state_specs(s0.shape, l, first)
    ins = [gT, s0, wg, bg, nw]
    specs = [_const(gT.shape), s_in, _const(wg.shape), _const(bg.shape), _const(nw.shape)]
    kern, pspecs, aliases = _finish_state_call(functools.partial(_gla_t_kernel, T=T), len(ins), first, prevs)
    ins, specs = ins + prevs, specs + pspecs
    return pl.pallas_call(
        kern,
        grid=(GLA_H,),
        in_specs=specs,
        out_specs=[pl.BlockSpec((GLA_DV, N), lambda h: (h, 0)), s_out],
        out_shape=[jax.ShapeDtypeStruct((GLA_WIDTH, N), BF), jax.ShapeDtypeStruct(s0.shape, F32)],
        scratch_shapes=[pltpu.VMEM((GLA_DK, N), F32), pltpu.VMEM((GLA_DK, N), F32), pltpu.VMEM((GLA_DK, LANE), F32)],
        input_output_aliases=aliases,
        compiler_params=_cp(1),
        name="gla_t",
    )(*ins)


def _ret_t_kernel(x_ref, s0_ref, cos_ref, sin_ref, pw_ref, nw_ref, o_ref, sn_ref, qd_ref, kd_ref, *, T, first):
    nb = LANE
    if first:
        _zero_later_layers(sn_ref)
    h = pl.program_id(0)
    r0 = pl.multiple_of(h * RET_DK, RET_DK)
    half_k = RET_DK // 2
    cos, sin = cos_ref[...], sin_ref[...]

    def rope_t(base):
        x1 = x_ref[pl.ds(base + r0, half_k), :]
        x2 = x_ref[pl.ds(base + r0 + half_k, half_k), :]
        return jnp.concatenate([x1 * cos - x2 * sin, x1 * sin + x2 * cos], axis=0)

    q = rope_t(0)
    k = rope_t(256) * (RET_DK ** -0.5)
    pw = pw_ref[h]
    qs = [q[:, t * nb:(t + 1) * nb] for t in range(T)]
    ks = [k[:, t * nb:(t + 1) * nb] for t in range(T)]
    for t in range(T):
        qd_ref[:, t * nb:(t + 1) * nb] = qs[t] * pw[t + 1:t + 2, :]
        kd_ref[:, t * nb:(t + 1) * nb] = ks[t] * pw[T - 1 - t:T - t, :]

    def vt(t):
        return x_ref[pl.ds(512 + r0, RET_DV), t * nb:(t + 1) * nb]

    o = []
    for t in range(T):
        ot = jnp.zeros((RET_DV, nb), F32)
        for u in range(t + 1):
            s = _row_sum(qs[t] * ks[u]) * pw[t - u:t - u + 1, :]
            ot = ot + s * vt(u)
        o.append(ot)

    o = _lane_state_readout(o, qd_ref, s0_ref, RET_DK)
    _lane_state_update(sn_ref, s0_ref, lambda base: pw[T:T + 1, :], kd_ref, vt, RET_DK, T)

    nw = nw_ref[pl.ds(r0, RET_DV), :]
    for t in range(T):
        mu = jnp.mean(o[t], axis=0, keepdims=True)
        d = o[t] - mu
        var = jnp.mean(d * d, axis=0, keepdims=True)
        rg = x_ref[pl.ds(768 + r0, RET_DV), t * nb:(t + 1) * nb]
        o_ref[:, t * nb:(t + 1) * nb] = (d * lax.rsqrt(var + EPS) * nw * _silu(rg)).astype(o_ref.dtype)


def _ret_t_call(rT, s0, prev, l, T, norm_w):
    N = rT.shape[1]
    B = N // T
    half = RET_DK // 2
    inv = ROPE_BASE ** (-jnp.arange(half, dtype=F32) / half)
    ang = inv[:, None] * (PAST_LEN + jnp.arange(T, dtype=jnp.int32)).astype(F32)[None, :]
    cos = jnp.repeat(jnp.cos(ang), B, axis=1)
    sin = jnp.repeat(jnp.sin(ang), B, axis=1)
    lg = _ret_log_gamma()
    pw = jnp.asarray(np.repeat(np.exp(lg[:, None] * np.arange(16)[None, :])[:, :, None], LANE, axis=2), F32)
    nw = norm_w.reshape(-1, 1)
    first = prev is None
    prevs = [] if first else [prev]
    s_in, s_out = _state_specs(s0.shape, l, first)
    ins = [rT, s0, cos, sin, pw, nw]
    specs = [_const(rT.shape), s_in, _const(cos.shape), _const(sin.shape), _const(pw.shape), _const(nw.shape)]
    kern, pspecs, aliases = _finish_state_call(functools.partial(_ret_t_kernel, T=T), len(ins), first, prevs)
    ins, specs = ins + prevs, specs + pspecs
    return pl.pallas_call(
        kern,
        grid=(RET_H,),
        in_specs=specs,
        out_specs=[pl.BlockSpec((RET_DV, N), lambda h: (h, 0)), s_out],
        out_shape=[jax.ShapeDtypeStruct((RET_WIDTH, N), BF), jax.ShapeDtypeStruct(s0.shape, F32)],
        scratch_shapes=[pltpu.VMEM((RET_DK, N), F32), pltpu.VMEM((RET_DK, N), F32)],
        input_output_aliases=aliases,
        compiler_params=_cp(1),
        name="ret_t",
    )(*ins)


def _ssd_t_kernel(x_ref, c0_ref, s0_ref, cw_ref, cb_ref, dtb_ref, alog_ref, d_ref, nw_ref,
                  o_ref, sn_ref, cn_ref, hist_ref, y_ref, ssq_ref, cm_ref, bw_ref, xw_ref, *, T, first):
    nb = LANE
    W1 = SSD_CONV_W - 1
    h = pl.program_id(0)
    XB = SSD_WIDTH
    if first:
        _zero_later_layers(sn_ref)

    @pl.when(h == 0)
    def _():
        ssq_ref[...] = jnp.zeros_like(ssq_ref)
        if first:
            _zero_later_layers(cn_ref)
        for i in range(W1):
            for j in range(SSD_CONV_DIM // LANE):
                hist_ref[j * LANE:(j + 1) * LANE, i * nb:(i + 1) * nb] = c0_ref[0, i][:, j * LANE:(j + 1) * LANE].T
                cn_ref[0, i, :, j * LANE:(j + 1) * LANE] = \
                    x_ref[XB + j * LANE:XB + (j + 1) * LANE, (T - W1 + i) * nb:(T - W1 + i + 1) * nb].T

    def conv_rows(ro):
        w = cw_ref[pl.ds(ro, 64), :]
        b = cb_ref[pl.ds(ro, 64), :]
        xx = [hist_ref[pl.ds(ro, 64), i * nb:(i + 1) * nb] for i in range(W1)]
        xx += [x_ref[pl.ds(XB + ro, 64), t * nb:(t + 1) * nb] for t in range(T)]
        out = []
        for t in range(T):
            acc = b + w[:, 0:1] * xx[t]
            for i in range(1, SSD_CONV_W):
                acc = acc + w[:, i:i + 1] * xx[t + i]
            out.append(_silu(acc))
        return out

    grp = h // (SSD_H // SSD_G)
    xs = conv_rows(pl.multiple_of(h * SSD_P, SSD_P))
    bm = conv_rows(pl.multiple_of(SSD_WIDTH + grp * SSD_N, SSD_N))
    cm = conv_rows(pl.multiple_of(SSD_WIDTH + SSD_G * SSD_N + grp * SSD_N, SSD_N))

    dt_all = _softplus(x_ref[pl.ds(XB + SSD_CONV_DIM + h, 1), :] + dtb_ref[pl.ds(h, 1), :])
    a = -jnp.exp(alog_ref[pl.ds(h, 1), :])
    dts = [dt_all[:, t * nb:(t + 1) * nb] for t in range(T)]
    gs = []
    acc = jnp.zeros((1, nb), F32)
    for t in range(T):
        acc = acc + dts[t] * a
        gs.append(acc)
    gl = gs[T - 1]

    o = []
    for t in range(T):
        ot = jnp.zeros((SSD_P, nb), F32)
        for u in range(t + 1):
            s = _row_sum(cm[t] * bm[u]) * (jnp.exp(gs[t] - gs[u]) * dts[u])
            ot = ot + s * xs[u]
        o.append(ot)

    for t in range(T):
        cm_ref[:, t * nb:(t + 1) * nb] = cm[t] * jnp.exp(gs[t])
        bw_ref[:, t * nb:(t + 1) * nb] = bm[t]
        xw_ref[:, t * nb:(t + 1) * nb] = xs[t] * (dts[t] * jnp.exp(gl - gs[t]))

    o = _lane_state_readout(o, cm_ref, s0_ref, SSD_N)
    egl = jnp.exp(gl)
    _lane_state_update(sn_ref, s0_ref, lambda base: egl, bw_ref, lambda t: xw_ref[:, t * nb:(t + 1) * nb], SSD_N, T)

    dd = d_ref[pl.ds(h, 1), :]
    p0 = pl.multiple_of(h * SSD_P, SSD_P)
    for t in range(T):
        z = x_ref[pl.ds(p0, SSD_P), t * nb:(t + 1) * nb]
        y = (o[t] + dd * xs[t]) * _silu(z)
        y_ref[pl.ds(p0, SSD_P), t * nb:(t + 1) * nb] = y
        ssq_ref[:, t * nb:(t + 1) * nb] += _row_sum(y * y)

    @pl.when(h == SSD_H - 1)
    def _():
        scale = lax.rsqrt(ssq_ref[...] * (1.0 / SSD_WIDTH) + EPS)
        o_ref[...] = (y_ref[...] * scale * nw_ref[...]).astype(o_ref.dtype)


def _ssd_t_call(sT, c0, s0, prev_s, prev_c, l, T, conv_w, conv_b, dt_bias, a_log, d, norm_w):
    N = sT.shape[1]
    col = lambda v: jnp.zeros((LANE, 1), F32).at[:SSD_H, 0].set(v)
    prm = (conv_w.T, conv_b.reshape(-1, 1), col(dt_bias), col(a_log), col(d), norm_w.reshape(-1, 1))
    first = prev_s is None
    prevs = [] if first else [prev_s, prev_c]
    s_in, s_out = _state_specs(s0.shape, l, first)
    c_in = pl.BlockSpec((1,) + tuple(c0.shape[1:]), lambda h: (l, 0, 0, 0))
    c_out = pl.BlockSpec(((c0.shape[0] if first else 1),) + tuple(c0.shape[1:]), lambda h: (l, 0, 0, 0))
    ins = [sT, c0, s0, *prm]
    specs = [_const(sT.shape), c_in, s_in] + [_const(p.shape) for p in prm]
    kern, pspecs, aliases = _finish_state_call(functools.partial(_ssd_t_kernel, T=T), len(ins), first, prevs)
    ins, specs = ins + prevs, specs + pspecs
    return pl.pallas_call(
        kern,
        grid=(SSD_H,),
        in_specs=specs,
        out_specs=[_const((SSD_WIDTH, N)), s_out, c_out],
        out_shape=[jax.ShapeDtypeStruct((SSD_WIDTH, N), BF), jax.ShapeDtypeStruct(s0.shape, F32),
                   jax.ShapeDtypeStruct(c0.shape, F32)],
        scratch_shapes=[pltpu.VMEM((SSD_CONV_DIM, (SSD_CONV_W - 1) * LANE), F32),
                        pltpu.VMEM((SSD_WIDTH, N), F32), pltpu.VMEM((1, N), F32),
                        pltpu.VMEM((SSD_N, N), F32), pltpu.VMEM((SSD_N, N), F32), pltpu.VMEM((SSD_P, N), F32)],
        input_output_aliases=aliases,
        compiler_params=_cp(1),
        name="ssd_t",
    )(*ins)


def _outproj_t_kernel(x_ref, g_ref, og_ref, or_ref, os_ref, w_ref, lg_ref, lb_ref, o_ref):
    nt, nb, D = x_ref.shape
    for t in range(nt):
        cols = slice(t * nb, (t + 1) * nb)
        mix = (_dot_tn(og_ref[:, cols], w_ref[0:GLA_WIDTH, :])
               + _dot_tn(or_ref[:, cols], w_ref[GLA_WIDTH:GLA_WIDTH + RET_WIDTH, :])
               + _dot_tn(os_ref[:, cols], w_ref[GLA_WIDTH + RET_WIDTH:D, :]))
        y = ALPHA * x_ref[t] + g_ref[0] * mix
        o_ref[t] = _layer_norm(y, lg_ref[0], lb_ref[0])


def _outproj_t_call(xt, g1, ogT, orT, osT, w_out, ln_g, ln_b):
    T, B, D = xt.shape
    nt = 4
    cmap = lambda i: (0, i)
    return pl.pallas_call(
        _outproj_t_kernel,
        grid=(T // nt,),
        in_specs=[pl.BlockSpec((nt, B, D), lambda i: (i, 0, 0)),
                  pl.BlockSpec((1, B, D), lambda i: (0, 0, 0)),
                  pl.BlockSpec((GLA_WIDTH, nt * B), cmap),
                  pl.BlockSpec((RET_WIDTH, nt * B), cmap),
                  pl.BlockSpec((SSD_WIDTH, nt * B), cmap),
                  _const((D, D)), _const((1, 1, D)), _const((1, 1, D))],
        out_specs=pl.BlockSpec((nt, B, D), lambda i: (i, 0, 0)),
        out_shape=jax.ShapeDtypeStruct((T, B, D), F32),
        compiler_params=_cp(1),
        name="out_proj_ln_t",
    )(xt, g1, ogT, orT, osT, w_out, ln_g.reshape(1, 1, D), ln_b.reshape(1, 1, D))


ROUTE_OFF = 8


def _moe_route_t(lt):
    neg = jnp.float32(-jnp.inf)
    row8 = lax.broadcasted_iota(jnp.int32, (8, 1), 0)
    lg = jnp.where(row8 < MOE_GROUPS, lt[0:8, :], neg)
    mg = jnp.max(lg, axis=0, keepdims=True)
    gsel = jnp.min(jnp.where(lg == mg, row8, 8), axis=0, keepdims=True)
    g_gate = 1.0 / jnp.sum(jnp.exp(lg - mg), axis=0, keepdims=True)
    rowe = lax.broadcasted_iota(jnp.int32, (MOE_EXPERTS, 1), 0)
    le = jnp.where((rowe // MOE_PER_GROUP) == gsel, lt[ROUTE_OFF:ROUTE_OFF + MOE_EXPERTS, :], neg)
    m1 = jnp.max(le, axis=0, keepdims=True)
    i1 = jnp.min(jnp.where(le == m1, rowe, MOE_EXPERTS), axis=0, keepdims=True)
    le2 = jnp.where(rowe == i1, neg, le)
    m2 = jnp.max(le2, axis=0, keepdims=True)
    i2 = jnp.min(jnp.where(le2 == m2, rowe, MOE_EXPERTS), axis=0, keepdims=True)
    e2 = jnp.exp(m2 - m1)
    w1 = g_gate / (1.0 + e2)
    w2 = g_gate * e2 / (1.0 + e2)
    comb = jnp.where(rowe == i1, w1, jnp.where(rowe == i2, w2, 0.0))
    cg = comb[0:4, :]
    for g in range(1, MOE_GROUPS):
        cg = cg + comb[g * MOE_PER_GROUP:(g + 1) * MOE_PER_GROUP, :]
    return gsel, cg, comb


MOE_SUB = 256
MOE_BLK = 16
MOE_ROWS = 256
MOE_NPS = MOE_SUB + MOE_GROUPS * MOE_BLK
assert MOE_SUB <= MOE_ROWS


def _moe_kernel(x_ref, sc_ref, sh_ref, g_ref, wr_ref, br_ref, us_ref, w1_ref, w3_ref, w2_ref, lg_ref, lb_ref,
                o_ref, hb_ref, cwb_ref, hp_ref, cwp_ref, yp_ref, pos_ref,
                cgrp_ref, fill_ref, cur_ref, na_ref, nb_ref, so_ref, nfa_ref, dsa_ref, dsb_ref, *, n_steps):
    bB, bT, D = x_ref.shape
    R = bB * bT
    n_q = R // MOE_SUB
    s = pl.program_id(1)
    x = x_ref[...]
    row8 = lax.broadcasted_iota(jnp.int32, (8, 1), 0)
    slot = lax.broadcasted_iota(jnp.int32, (MOE_NPS, 1), 0).astype(F32)

    @pl.when((pl.program_id(0) == 0) & (s == 0))
    def _():
        hb_ref[...] = jnp.zeros_like(hb_ref)
        cwb_ref[...] = jnp.zeros_like(cwb_ref)
        yp_ref[...] = jnp.zeros_like(yp_ref)

    @pl.when(s == 0)
    def _():
        na_ref[0] = 0
        for g in range(MOE_GROUPS):
            cur_ref[g] = -1
            fill_ref[g] = 0

    @pl.when(s < n_steps)
    def _():
        h = (x * (1.0 + sc_ref[...]) + sh_ref[...]).reshape(R, D)
        segs, offs = [], []
        for q in range(n_q):
            u = s * n_q + q
            hq = h[q * MOE_SUB:(q + 1) * MOE_SUB, :].astype(BF)
            gsel, cg, _ = _moe_route_t(_dot_nt(wr_ref[...], hq) + br_ref[...])
            onehot = jnp.where(row8 == gsel, 1.0, 0.0)
            rank = _dot(onehot.astype(BF), us_ref[...])
            cnt = jnp.sum(onehot, axis=1, keepdims=True)
            seg = jnp.ceil(cnt * (1.0 / MOE_BLK)) * MOE_BLK
            off = jnp.zeros((8, 1), F32)
            for g in range(1, MOE_GROUPS):
                off = off + jnp.where(row8 >= g, seg[g - 1:g, :], 0.0)
            pos = jnp.sum(onehot * (off + rank), axis=0, keepdims=True)
            pos_ref[u] = jnp.broadcast_to(pos, (8, MOE_SUB))
            perm = jnp.where(slot == pos, 1.0, 0.0).astype(BF)
            hp_ref[q] = _dot(perm, hq).astype(BF)
            cg8 = jnp.concatenate([cg, jnp.zeros((4, MOE_SUB), F32)], axis=0)
            cg_hi = cg8.astype(BF)
            cg_lo = (cg8 - cg_hi.astype(F32)).astype(BF)
            cwp_ref[q] = _dot_nt(perm, cg_hi) + _dot_nt(perm, cg_lo)
            segs.append(seg)
            offs.append(off)
        for q in range(n_q):
            u = s * n_q + q
            for g in range(MOE_GROUPS):
                so = offs[q][g, 0].astype(jnp.int32)
                nb = (segs[q][g, 0] * (1.0 / MOE_BLK)).astype(jnp.int32)
                f = fill_ref[g]
                c = cur_ref[g]
                na = na_ref[0]
                room = jnp.where(c < 0, 0, (MOE_ROWS - f) // MOE_BLK)
                n_a = jnp.minimum(nb, room)
                n_b = nb - n_a
                base_a = c * MOE_ROWS + f
                base_b = na * MOE_ROWS
                idx = u * MOE_GROUPS + g
                so_ref[idx] = so
                nb_ref[idx] = nb
                nfa_ref[idx] = n_a
                dsa_ref[idx] = base_a
                dsb_ref[idx] = base_b

                def put(k, carry, so=so, q=q, n_a=n_a, base_a=base_a, base_b=base_b):
                    dst = pl.multiple_of(jnp.where(k < n_a, base_a + k * MOE_BLK, base_b + (k - n_a) * MOE_BLK), MOE_BLK)
                    src = pl.multiple_of(so + k * MOE_BLK, MOE_BLK)
                    hb_ref[pl.ds(dst, MOE_BLK), :] = hp_ref[q, pl.ds(src, MOE_BLK), :]
                    cwb_ref[pl.ds(dst, MOE_BLK), :] = cwp_ref[q, pl.ds(src, MOE_BLK), :]
                    return carry

                lax.fori_loop(0, nb, put, 0)

                @pl.when(n_b > 0)
                def _(g=g, na=na, n_b=n_b):
                    cgrp_ref[na] = g
                    na_ref[0] = na + 1
                    cur_ref[g] = na
                    fill_ref[g] = n_b * MOE_BLK

                @pl.when(n_b == 0)
                def _(g=g, f=f, n_a=n_a):
                    fill_ref[g] = f + n_a * MOE_BLK

    @pl.when(s == n_steps - 1)
    def _():
        def chunk(c, carry):
            g = cgrp_ref[c]
            start = pl.multiple_of(c * MOE_ROWS, MOE_ROWS)
            hc = hb_ref[pl.ds(start, MOE_ROWS), :]
            cw = cwb_ref[pl.ds(start, MOE_ROWS), :]
            hids = []
            for j in range(MOE_PER_GROUP):
                e = g * MOE_PER_GROUP + j
                hid = _silu(_dot(hc, w1_ref[e])) * _dot(hc, w3_ref[e]) * cw[:, j:j + 1]
                hids.append(hid.astype(BF))
            w2g = w2_ref[pl.ds(g * MOE_PER_GROUP, MOE_PER_GROUP)].reshape(MOE_PER_GROUP * MOE_FF, D)
            hb_ref[pl.ds(start, MOE_ROWS), :] = _dot(jnp.concatenate(hids, axis=1), w2g).astype(BF)
            return carry

        lax.fori_loop(0, na_ref[0], chunk, 0)

    @pl.when(s >= n_steps)
    def _():
        for q in range(n_q):
            u = (s - n_steps) * n_q + q
            for g in range(MOE_GROUPS):
                idx = u * MOE_GROUPS + g
                so, n_a, base_a, base_b = so_ref[idx], nfa_ref[idx], dsa_ref[idx], dsb_ref[idx]

                def take(k, carry, so=so, q=q, n_a=n_a, base_a=base_a, base_b=base_b):
                    src = pl.multiple_of(jnp.where(k < n_a, base_a + k * MOE_BLK, base_b + (k - n_a) * MOE_BLK), MOE_BLK)
                    dst = pl.multiple_of(so + k * MOE_BLK, MOE_BLK)
                    yp_ref[q, pl.ds(dst, MOE_BLK), :] = hb_ref[pl.ds(src, MOE_BLK), :]
                    return carry

                lax.fori_loop(0, nb_ref[idx], take, 0)
        ys = []
        for q in range(n_q):
            u = (s - n_steps) * n_q + q
            perm = jnp.where(slot == pos_ref[u][0:1, :], 1.0, 0.0).astype(BF)
            ys.append(_dot_tn(perm, yp_ref[q]))
        y = jnp.concatenate(ys, axis=0)
        z = ALPHA * x + g_ref[...] * y.reshape(bB, bT, D)
        o_ref[...] = _layer_norm(z, lg_ref[...], lb_ref[...])


def _moe_call(x3, sc, sh, g2, wr, br, w1, w3, w2, l, ln_g, ln_b):
    B, T, D = x3.shape
    bB, bT = _tok_tiles(B, T)
    R = bB * bT
    if bB == 1:
        spp = 2 if B % 2 == 0 else 1
        nT = T // bT
        n_pools, n_steps = B // spp, spp * nT
        xmap = lambda p, s: (p * spp + (s % n_steps) // nT, (s % n_steps) % nT, 0)
        omap = lambda p, s: (p * spp + jnp.maximum(s - n_steps, 0) // nT, jnp.maximum(s - n_steps, 0) % nT, 0)
        mmap = lambda p, s: (p * spp + (s % n_steps) // nT, 0, 0)
        mshape = (1, 1, D)
    else:
        n_pools, n_steps = 1, B // bB
        xmap = lambda p, s: (s % n_steps, 0, 0)
        omap = lambda p, s: (jnp.maximum(s - n_steps, 0), 0, 0)
        mmap = lambda p, s: (0, 0, 0)
        mshape = (1, bT, D)
    n_sub = n_steps * (R // MOE_SUB)
    n_chunks = pl.cdiv(n_sub * (MOE_SUB + MOE_GROUPS * (MOE_BLK - 1)), MOE_ROWS) + MOE_GROUPS
    us = jnp.asarray(np.triu(np.ones((MOE_SUB, MOE_SUB), np.float32), 1), BF)
    smem = lambda n: pltpu.SMEM((n,), jnp.int32)
    return pl.pallas_call(
        functools.partial(_moe_kernel, n_steps=n_steps),
        grid=(n_pools, 2 * n_steps),
        in_specs=[pl.BlockSpec((bB, bT, D), xmap),
                  pl.BlockSpec(mshape, mmap), pl.BlockSpec(mshape, mmap), pl.BlockSpec(mshape, mmap),
                  _const(wr.shape), _const(br.shape), _const(us.shape),
                  _resident_layer(w1.shape, l), _resident_layer(w3.shape, l), _resident_layer(w2.shape, l),
                  _const((1, 1, D)), _const((1, 1, D))],
        out_specs=pl.BlockSpec((bB, bT, D), omap),
        out_shape=jax.ShapeDtypeStruct((B, T, D), F32),
        scratch_shapes=[pltpu.VMEM((n_chunks * MOE_ROWS, D), BF), pltpu.VMEM((n_chunks * MOE_ROWS, 8), F32),
                        pltpu.VMEM((R // MOE_SUB, MOE_NPS, D), BF), pltpu.VMEM((R // MOE_SUB, MOE_NPS, 8), F32),
                        pltpu.VMEM((R // MOE_SUB, MOE_NPS, D), BF),
                        pltpu.VMEM((n_sub, 8, MOE_SUB), F32),
                        smem(n_chunks), smem(MOE_GROUPS), smem(MOE_GROUPS), smem(1),
                        *[smem(n_sub * MOE_GROUPS) for _ in range(5)]],
        compiler_params=_cp(2),
        name="moe_ln",
    )(x3, sc, sh, g2, wr, br, us, w1, w3, w2, ln_g.reshape(1, 1, D), ln_b.reshape(1, 1, D))


def _router_params(w_group, b_group, w_expert, b_expert):
    wr = jnp.zeros((LANE, D_MODEL), F32).at[:MOE_GROUPS].set(w_group.T)
    wr = wr.at[ROUTE_OFF:ROUTE_OFF + MOE_EXPERTS].set(w_expert.T)
    br = jnp.zeros((LANE, 1), F32).at[:MOE_GROUPS, 0].set(b_group).at[ROUTE_OFF:ROUTE_OFF + MOE_EXPERTS, 0].set(b_expert)
    return wr.astype(BF), br


def kernel(x_prompt, x_sample, c_prompt, c_sample, state_gla, state_ret, state_ssd, state_conv, w_ada, b_ada, w_in, gla_w_gate, gla_b_gate, gla_norm, ret_norm, ssd_conv_w, ssd_conv_b, ssd_dt_bias, ssd_a_log, ssd_d, ssd_norm, w_out, ln1_g, ln1_b, moe_w_group, moe_b_group, moe_w_expert, moe_b_expert, moe_w1, moe_w3, moe_w2, ln2_g, ln2_b):
    Bp, Tp, D = x_prompt.shape
    Bs, Ts, _ = x_sample.shape
    w_in_t = jnp.swapaxes(w_in, 1, 2)
    w_out_b = w_out.astype(BF)
    w1_b, w3_b, w2_b = moe_w1.astype(BF), moe_w3.astype(BF), moe_w2.astype(BF)

    mod = _mod_call(jnp.concatenate([c_prompt, c_sample], axis=0), w_ada, b_ada)

    def moe(x, sc2, sh2, g2, l):
        wr, br = _router_params(moe_w_group[l], moe_b_group[l], moe_w_expert[l], moe_b_expert[l])
        return _moe_call(x, sc2, sh2, g2, wr, br, w1_b, w3_b, w2_b, l, ln2_g[l], ln2_b[l])

    x = x_prompt
    new = [[], [], [], []]
    for l in range(DEPTH):
        sh1, sc1, g1, sh2, sc2, g2 = (mod[l, :Bp, None, i * D:(i + 1) * D] for i in range(6))
        gin, rin, sin_ = _inproj_call(x, sc1, sh1, w_in_t, l)
        og, s_gla = _gla_prompt_call(gin, Bp, Tp, gla_w_gate[l], gla_b_gate[l], gla_norm[l])
        orr, s_ret = _ret_prompt_call(rin, Bp, Tp, ret_norm[l])
        x, s_ssd, s_conv = _ssd_prompt_call(sin_, x, g1, og, orr, w_out_b[l], ln1_g[l], ln1_b[l], ssd_conv_w[l],
                                            ssd_conv_b[l], ssd_dt_bias[l], ssd_a_log[l], ssd_d[l], ssd_norm[l])
        x = moe(x, sc2, sh2, g2, l)
        for acc, s in zip(new, (s_gla, s_ret, s_ssd, s_conv)):
            acc.append(s)
    y_p = x
    gla_p, ret_p, ssd_p, conv_p = (jnp.stack(a) for a in new)

    x = jnp.swapaxes(x_sample, 0, 1)
    sg = jnp.transpose(state_gla, (0, 2, 3, 4, 1))
    sr = jnp.transpose(state_ret, (0, 2, 3, 4, 1))
    ss = jnp.transpose(state_ssd, (0, 2, 3, 4, 1))
    cv = jnp.transpose(state_conv, (0, 2, 1, 3))
    gla_n = ret_n = ssd_n = conv_n = None
    for l in range(DEPTH):
        sh1, sc1, g1, sh2, sc2, g2 = (mod[l, None, Bp:, i * D:(i + 1) * D] for i in range(6))
        gT, rT, sT = _inproj_t_call(x, sc1, sh1, w_in_t, l)
        ogT, gla_n = _gla_t_call(gT, sg, gla_n, l, Ts, gla_w_gate[l], gla_b_gate[l], gla_norm[l])
        orT, ret_n = _ret_t_call(rT, sr, ret_n, l, Ts, ret_norm[l])
        osT, ssd_n, conv_n = _ssd_t_call(sT, cv, ss, ssd_n, conv_n, l, Ts, ssd_conv_w[l], ssd_conv_b[l],
                                         ssd_dt_bias[l], ssd_a_log[l], ssd_d[l], ssd_norm[l])
        x = _outproj_t_call(x, g1, ogT, orT, osT, w_out_b[l], ln1_g[l], ln1_b[l])
        x = moe(x, sc2, sh2, g2, l)
    y_s = jnp.swapaxes(x, 0, 1)
    gla_s = jnp.transpose(gla_n, (0, 4, 1, 2, 3))
    ret_s = jnp.transpose(ret_n, (0, 4, 1, 2, 3))
    ssd_s = jnp.transpose(ssd_n, (0, 4, 1, 2, 3))
    conv_s = jnp.transpose(conv_n, (0, 2, 1, 3))
    return (y_p, y_s, gla_p, ret_p, ssd_p, conv_p, gla_s, ret_s, ssd_s, conv_s)
```

```python
import functools

import numpy as np
import jax
import jax.numpy as jnp
from jax import lax
from jax.experimental import pallas as pl
from jax.experimental.pallas import tpu as pltpu

F32 = jnp.float32
BF = jnp.bfloat16

D_MODEL = 1024
DEPTH = 2
PAST_LEN = 16384
GLA_H, GLA_DK, GLA_DV = 4, 32, 64
GLA_WIDTH = GLA_H * GLA_DV
GLA_GATE_RANK = 16
GLA_GATE_TEMP = 16.0
GLA_CHUNK = 16
RET_H, RET_DK, RET_DV = 4, 64, 64
RET_WIDTH = RET_H * RET_DV
ROPE_BASE = 10000.0
SSD_H, SSD_P, SSD_G, SSD_N = 8, 64, 2, 64
SSD_WIDTH = SSD_H * SSD_P
SSD_CONV_W = 4
SSD_CONV_DIM = SSD_WIDTH + 2 * SSD_G * SSD_N
MOE_GROUPS, MOE_PER_GROUP = 4, 4
MOE_EXPERTS = MOE_GROUPS * MOE_PER_GROUP
MOE_FF = 256
ALPHA = (2 * DEPTH) ** 0.25
EPS = 1e-5
LOG2E = 1.4426950408889634

LANE = 128
GLA_IN_W = 128 + 128 + 256 + LANE + 256
RET_IN_W = 4 * 256
SSD_IN_W = 512 + SSD_CONV_DIM + LANE
IN_W = GLA_IN_W + RET_IN_W + SSD_IN_W
VMEM_LIMIT = 56 * 1024 * 1024


def _cp(n_axes, vmem=VMEM_LIMIT):
    return pltpu.CompilerParams(dimension_semantics=("arbitrary",) * n_axes, vmem_limit_bytes=vmem)


def _dot(a, b):
    return jnp.dot(a, b, preferred_element_type=F32)


def _dot_nt(a, b):
    return lax.dot_general(a, b, (((1,), (1,)), ((), ())), preferred_element_type=F32)


def _dot_tn(a, b):
    return lax.dot_general(a, b, (((0,), (0,)), ((), ())), preferred_element_type=F32)


def _split3(x):
    hi = x.astype(BF)
    r = x - hi.astype(F32)
    mid = r.astype(BF)
    lo = (r - mid.astype(F32)).astype(BF)
    return hi, mid, lo


def _dot_x2(x, e):
    hi = x.astype(BF)
    lo = (x - hi.astype(F32)).astype(BF)
    return _dot(hi, e) + _dot(lo, e)


def _dot_3x(e, x):
    hi, mid, lo = _split3(x)
    return _dot(e, hi) + (_dot(e, mid) + _dot(e, lo))


def _sigmoid(x):
    return 1.0 / (1.0 + jnp.exp2(x * (-LOG2E)))


def _silu(x):
    return x * _sigmoid(x)


def _log_sigmoid(x):
    return jnp.minimum(x, 0.0) - jnp.log(1.0 + jnp.exp(-jnp.abs(x)))


def _softplus(x):
    return jnp.maximum(x, 0.0) + jnp.log(1.0 + jnp.exp(-jnp.abs(x)))


def _layer_norm(x, g, b):
    mu = jnp.mean(x, axis=-1, keepdims=True)
    d = x - mu
    var = jnp.mean(d * d, axis=-1, keepdims=True)
    return d * lax.rsqrt(var + EPS) * g + b


def _const(shape):
    return pl.BlockSpec(shape, lambda *_: (0,) * len(shape))


def _resident_layer(shape, l):
    return pl.BlockSpec((None,) + tuple(shape[1:]), lambda *_: (l,) + (0,) * (len(shape) - 1),
                        pipeline_mode=pl.Buffered(1))


def _mod_kernel(c_ref, w_ref, b_ref, o_ref):
    s = _silu(c_ref[...]).astype(BF)
    o_ref[0] = _dot(s, w_ref[0].astype(BF)) + b_ref[0]


def _mod_call(c_all, w_ada, b_ada):
    R = c_all.shape[0]
    tn = 1536
    return pl.pallas_call(
        _mod_kernel,
        grid=(DEPTH, 6 * D_MODEL // tn),
        in_specs=[pl.BlockSpec((R, D_MODEL), lambda l, j: (0, 0)),
                  pl.BlockSpec((1, D_MODEL, tn), lambda l, j: (l, 0, j)),
                  pl.BlockSpec((1, 1, tn), lambda l, j: (l, 0, j))],
        out_specs=pl.BlockSpec((1, R, tn), lambda l, j: (l, 0, j)),
        out_shape=jax.ShapeDtypeStruct((DEPTH, R, 6 * D_MODEL), F32),
        compiler_params=_cp(2),
        name="ada_mod",
    )(c_all, w_ada, b_ada.reshape(DEPTH, 1, 6 * D_MODEL))


N_IN = 3096
N_GA = 128 + 128 + 256
N_DT = N_IN - SSD_H


def _inproj_kernel(x_ref, sc_ref, sh_ref, wt_ref, og_ref, or_ref, os_ref, w_ref):
    bB, bT, D = x_ref.shape

    @pl.when((pl.program_id(0) == 0) & (pl.program_id(1) == 0))
    def _():
        lane = lax.broadcasted_iota(jnp.int32, (1, LANE), 1)
        for j in range(N_GA // LANE):
            w_ref[:, j * LANE:(j + 1) * LANE] = wt_ref[j * LANE:(j + 1) * LANE, :].T.astype(BF)
        ga = wt_ref[N_GA:N_GA + LANE, :].T
        w_ref[:, N_GA:N_GA + LANE] = jnp.where(lane < GLA_GATE_RANK, ga, 0.0).astype(BF)
        src0, dst0 = N_GA + GLA_GATE_RANK, N_GA + LANE
        for j in range((N_DT - src0) // LANE):
            w_ref[:, dst0 + j * LANE:dst0 + (j + 1) * LANE] = \
                wt_ref[src0 + j * LANE:src0 + (j + 1) * LANE, :].T.astype(BF)
        dt = pltpu.roll(wt_ref[N_IN - LANE:N_IN, :].T, SSD_H, 1)
        w_ref[:, IN_W - LANE:IN_W] = jnp.where(lane < SSD_H, dt, 0.0).astype(BF)

    h = x_ref[...] * (1.0 + sc_ref[...]) + sh_ref[...]
    hb = h.reshape(bB * bT, D).astype(BF)
    cut = 2 * GLA_IN_W
    pa = _dot(hb, w_ref[:, 0:cut])
    pb = _dot(hb, w_ref[:, cut:IN_W])
    og_ref[...] = pa[:, 0:GLA_IN_W]
    or_ref[:, 0:cut - GLA_IN_W] = pa[:, GLA_IN_W:cut]
    or_ref[:, cut - GLA_IN_W:RET_IN_W] = pb[:, 0:GLA_IN_W + RET_IN_W - cut]
    os_ref[...] = pb[:, GLA_IN_W + RET_IN_W - cut:IN_W - cut]


def _tok_tiles(B, T):
    if T >= 512:
        return 1, 512
    return 512 // T, T


def _inproj_call(x3, sc, sh, wt, l):
    B, T, D = x3.shape
    bB, bT = _tok_tiles(B, T)
    nT = T // bT
    R = bB * bT
    N = B * T
    xmap = lambda i, j: (i, j, 0)
    mmap = lambda i, j: (i, 0, 0)
    omap = lambda i, j: (i * nT + j, 0)
    return pl.pallas_call(
        _inproj_kernel,
        grid=(B // bB, nT),
        in_specs=[pl.BlockSpec((bB, bT, D), xmap),
                  pl.BlockSpec((bB, 1, D), mmap),
                  pl.BlockSpec((bB, 1, D), mmap),
                  _resident_layer(wt.shape, l)],
        out_specs=[pl.BlockSpec((R, GLA_IN_W), omap),
                   pl.BlockSpec((R, RET_IN_W), omap),
                   pl.BlockSpec((R, SSD_IN_W), omap)],
        out_shape=[jax.ShapeDtypeStruct((N, GLA_IN_W), F32),
                   jax.ShapeDtypeStruct((N, RET_IN_W), F32),
                   jax.ShapeDtypeStruct((N, SSD_IN_W), F32)],
        scratch_shapes=[pltpu.VMEM((D, IN_W), BF)],
        compiler_params=_cp(2),
        name="in_proj",
    )(x3, sc, sh, wt)


def _head_block_mask(rows_per, cols_per, n):
    r = np.arange(rows_per * n)[:, None] // rows_per
    c = np.arange(cols_per * n)[None, :] // cols_per
    return (r == c).astype(np.float32)


def _block_tril(n, c):
    i = np.arange(n)[:, None]
    j = np.arange(n)[None, :]
    return ((i // c == j // c) & (j <= i)).astype(np.float32)


def _gla_front(x_ref, wg_ref, bg_ref, L_ref):
    q = x_ref[:, 0:128] * (GLA_DK ** -0.5)
    k = x_ref[:, 128:256]
    v = x_ref[:, 256:512]
    ga = x_ref[:, 512:640]
    r = x_ref[:, 640:896]
    gate = _dot(ga.astype(BF), wg_ref[...]) + bg_ref[...]
    la = _log_sigmoid(gate) * (LOG2E / GLA_GATE_TEMP)
    n = L_ref.shape[0]
    g = jnp.concatenate([_dot_3x(L_ref[...], la[i:i + n, :]) for i in range(0, la.shape[0], n)], axis=0)
    return q, k, v, r, g


def _gla_intra(q, g, kp_ref, gp_ref, vp_ref, E_ref, c):
    TT = q.shape[0]
    PAD = kp_ref.shape[0] - TT
    pos = lax.broadcasted_iota(jnp.int32, (TT, 1), 0) & (c - 1)
    o = jnp.zeros((TT, 2 * LANE), F32)
    for s in range(min(c, 8)):
        ks = kp_ref[pl.ds(PAD - s, TT), :]
        gs = gp_ref[pl.ds(PAD - s, TT), :]
        vs = vp_ref[pl.ds(PAD - s, TT), :]
        w = jnp.where(pos >= s, q * ks * jnp.exp2(g - gs), 0.0)
        o = o + _dot(w.astype(BF), E_ref[...]) * vs
    if c <= 8:
        return o
    assert c == 16
    nc = TT // c

    def upper(x):
        return x.reshape(nc, 2, 8, x.shape[-1])[:, 1].reshape(nc * 8, x.shape[-1])

    qu, gu = upper(q), upper(g)
    posu = lax.broadcasted_iota(jnp.int32, (nc * 8, 1), 0) & 7
    ou = jnp.zeros((nc * 8, 2 * LANE), F32)
    for s in range(8, c):
        ks = upper(kp_ref[pl.ds(PAD - s, TT), :])
        gs = upper(gp_ref[pl.ds(PAD - s, TT), :])
        vs = upper(vp_ref[pl.ds(PAD - s, TT), :])
        w = jnp.where(posu >= s - 8, qu * ks * jnp.exp2(gu - gs), 0.0)
        ou = ou + _dot(w.astype(BF), E_ref[...]) * vs
    ou = ou.reshape(nc, 1, 8, 2 * LANE)
    return o + jnp.concatenate([jnp.zeros_like(ou), ou], axis=1).reshape(TT, 2 * LANE)


def _gla_norm_gate(o, r, nw_ref, EA_ref):
    ms = _dot_x2(o * o, EA_ref[...])
    return o * lax.rsqrt(ms + EPS) * nw_ref[...] * _silu(r)


def _gla_prompt_kernel(x_ref, wg_ref, bg_ref, nw_ref, L_ref, E_ref, EA_ref, M_ref,
                       o_ref, sfin_ref, st_ref, kp_ref, gp_ref, vp_ref, oi_ref, u_ref, sb_ref, *, c):
    TT = x_ref.shape[0]
    nc = TT // c
    PAD = kp_ref.shape[0] - TT
    t = pl.program_id(1)

    @pl.when(t == 0)
    def _():
        st_ref[...] = jnp.zeros_like(st_ref)

    q, k, v, r, g = _gla_front(x_ref, wg_ref, bg_ref, L_ref)
    kp_ref[0:PAD, :] = jnp.zeros((PAD, LANE), F32)
    gp_ref[0:PAD, :] = jnp.zeros((PAD, LANE), F32)
    vp_ref[0:PAD, :] = jnp.zeros((PAD, 2 * LANE), F32)
    kp_ref[PAD:PAD + TT, :] = k
    gp_ref[PAD:PAD + TT, :] = g
    vp_ref[PAD:PAD + TT, :] = v
    o = _gla_intra(q, g, kp_ref, gp_ref, vp_ref, E_ref, c)

    M = M_ref[...]
    gl_all = gp_ref[pl.ds(PAD + c - 1, nc, stride=c), :]
    for n in range(nc):
        lo = n * c
        ke = (k[lo:lo + c, :] * jnp.exp2(gl_all[n:n + 1, :] - g[lo:lo + c, :])).astype(BF)
        u_ref[n] = _dot_tn(ke, v[lo:lo + c, :].astype(BF)) * M
    a_cols = jnp.concatenate([jnp.exp2(gl_all), jnp.zeros((LANE - nc, LANE), F32)], axis=0).T
    S = st_ref[...]
    for n in range(nc):
        sb_ref[n] = S.astype(BF)
        S = a_cols[:, n:n + 1] * S + u_ref[n]
    st_ref[...] = S
    qe = (q * jnp.exp2(g)).astype(BF)
    for n in range(nc):
        lo = n * c
        oi_ref[lo:lo + c, :] = _dot(qe[lo:lo + c, :], sb_ref[n])
    o = o + oi_ref[...]
    o_ref[...] = _gla_norm_gate(o, r, nw_ref, EA_ref).astype(o_ref.dtype)

    @pl.when(t == pl.num_programs(1) - 1)
    def _():
        for h in range(GLA_H):
            sfin_ref[0, h] = S[h * GLA_DK:(h + 1) * GLA_DK, h * GLA_DV:(h + 1) * GLA_DV]


def _gla_tables(TT, c):
    L = jnp.asarray(_block_tril(TT, c), BF)
    E = jnp.asarray(_head_block_mask(GLA_DK, GLA_DV, GLA_H), BF)
    EA = jnp.asarray(_head_block_mask(GLA_DV, GLA_DV, GLA_H) / GLA_DV, BF)
    M = jnp.asarray(_head_block_mask(GLA_DK, GLA_DV, GLA_H), F32)
    return L, E, EA, M


def _gla_params(w_gate, b_gate, norm_w):
    wg = jnp.zeros((LANE, GLA_H * GLA_DK), F32).at[:GLA_GATE_RANK].set(w_gate).astype(BF)
    return wg, b_gate.reshape(1, -1), norm_w.reshape(1, -1)


def _gla_prompt_call(gin, B, T, w_gate, b_gate, norm_w):
    TT, c = (1024 if T % 1024 == 0 else 512), GLA_CHUNK
    nT = T // TT
    L, E, EA, M = _gla_tables(min(TT, 256), c)
    wg, bg, nw = _gla_params(w_gate, b_gate, norm_w)
    PAD = 16
    return pl.pallas_call(
        functools.partial(_gla_prompt_kernel, c=c),
        grid=(B, nT),
        in_specs=[pl.BlockSpec((TT, GLA_IN_W), lambda b, t: (b * nT + t, 0)),
                  _const(wg.shape), _const(bg.shape), _const(nw.shape),
                  _const(L.shape), _const(E.shape), _const(EA.shape), _const(M.shape)],
        out_specs=[pl.BlockSpec((TT, GLA_WIDTH), lambda b, t: (b * nT + t, 0)),
                   pl.BlockSpec((1, GLA_H, GLA_DK, GLA_DV), lambda b, t: (b, 0, 0, 0))],
        out_shape=[jax.ShapeDtypeStruct((B * T, GLA_WIDTH), BF),
                   jax.ShapeDtypeStruct((B, GLA_H, GLA_DK, GLA_DV), F32)],
        scratch_shapes=[pltpu.VMEM((GLA_H * GLA_DK, GLA_H * GLA_DV), F32),
                        pltpu.VMEM((TT + PAD, LANE), F32),
                        pltpu.VMEM((TT + PAD, LANE), F32),
                        pltpu.VMEM((TT + PAD, 2 * LANE), F32),
                        pltpu.VMEM((TT, 2 * LANE), F32),
                        pltpu.VMEM((TT // c, GLA_H * GLA_DK, GLA_H * GLA_DV), F32),
                        pltpu.VMEM((TT // c, GLA_H * GLA_DK, GLA_H * GLA_DV), BF)],
        compiler_params=_cp(2),
        name="gla_prompt",
    )(gin, wg, bg, nw, L, E, EA, M)


def _rope(x, cos, sin_signed):
    lane = lax.broadcasted_iota(jnp.int32, (1, LANE), 1)
    first_half = (lane & (RET_DK - 1)) < RET_DK // 2
    out = []
    for p in range(2):
        xs = x[:, p * LANE:(p + 1) * LANE]
        up = pltpu.roll(xs, LANE - RET_DK // 2, 1)
        dn = pltpu.roll(xs, RET_DK // 2, 1)
        out.append(xs * cos + jnp.where(first_half, up, dn) * sin_signed)
    return jnp.concatenate(out, axis=1)


def _ret_front(x_ref, cos_ref, sin_ref, rows=slice(None)):
    cos, sin = cos_ref[rows, :], sin_ref[rows, :]
    q = _rope(x_ref[rows, 0:256], cos, sin)
    k = _rope(x_ref[rows, 256:512], cos, sin) * (RET_DK ** -0.5)
    v = x_ref[rows, 512:768]
    rg = x_ref[rows, 768:1024]
    return q, k, v, rg


def _ret_intra(q, k, v, D_ref):
    lane = lax.broadcasted_iota(jnp.int32, (1, RET_WIDTH), 1)
    qb, kb, vb = q.astype(BF), k.astype(BF), v.astype(BF)
    zero = jnp.zeros((), BF)
    o = jnp.zeros(q.shape, F32)
    for h in range(RET_H):
        hm = (lane // RET_DK) == h
        s = _dot_nt(jnp.where(hm, qb, zero), kb)
        p = (s * D_ref[h]).astype(BF)
        o = o + _dot(p, jnp.where(hm, vb, zero))
    return o


def _ret_norm_gate(o, rg, nw_ref, EA_ref):
    mu = _dot_x2(o, EA_ref[...])
    d = o - mu
    var = _dot_x2(d * d, EA_ref[...])
    return d * lax.rsqrt(var + EPS) * nw_ref[...] * _silu(rg)


def _ret_prompt_kernel(x_ref, cos_ref, sin_ref, D_ref, rd_ref, kd_ref, G_ref, M_ref, EA_ref, nw_ref,
                       o_ref, sfin_ref, st_ref):
    t = pl.program_id(1)

    @pl.when(t == 0)
    def _():
        st_ref[...] = jnp.zeros_like(st_ref)

    C = D_ref.shape[1]
    S = st_ref[...]
    for i in range(x_ref.shape[0] // C):
        rows = slice(i * C, (i + 1) * C)
        q, k, v, rg = _ret_front(x_ref, cos_ref, sin_ref, rows)
        o = _ret_intra(q, k, v, D_ref)
        o = o + _dot((q * rd_ref[...]).astype(BF), S.astype(BF))
        u = _dot_tn((k * kd_ref[...]).astype(BF), v.astype(BF))
        S = S * G_ref[...] + u * M_ref[...]
        o_ref[rows, :] = _ret_norm_gate(o, rg, nw_ref, EA_ref).astype(o_ref.dtype)
    st_ref[...] = S

    @pl.when(t == pl.num_programs(1) - 1)
    def _():
        for h in range(RET_H):
            sfin_ref[0, h] = S[h * RET_DK:(h + 1) * RET_DK, h * RET_DV:(h + 1) * RET_DV]


def _rope_tables(pos):
    half = RET_DK // 2
    inv = ROPE_BASE ** (-jnp.arange(half, dtype=F32) / half)
    ang = pos.astype(F32)[:, None] * inv[None, :]
    cos, sin = jnp.cos(ang), jnp.sin(ang)
    return jnp.tile(jnp.concatenate([cos, cos], 1), (1, 2)), jnp.tile(jnp.concatenate([-sin, sin], 1), (1, 2))


def _ret_log_gamma():
    return np.log(1.0 - 2.0 ** (-5.0 - np.arange(RET_H, dtype=np.float64)))


def _ret_prompt_call(rin, B, T, norm_w):
    C = 256
    TT = next(n * C for n in (8, 4, 1) if T % (n * C) == 0)
    nT = T // TT
    cos, sin = _rope_tables(jnp.arange(T, dtype=jnp.int32))
    lg = _ret_log_gamma()
    i = np.arange(C)
    dec = np.exp(lg[:, None, None] * (i[:, None] - i[None, :])[None]) * (i[:, None] >= i[None, :])[None]
    Dm = jnp.asarray(dec, F32)
    rd = jnp.asarray(np.repeat(np.exp(lg[None, :] * (i[:, None] + 1)), RET_DK, 1), F32)
    kd = jnp.asarray(np.repeat(np.exp(lg[None, :] * (C - 1 - i[:, None])), RET_DK, 1), F32)
    M = _head_block_mask(RET_DK, RET_DV, RET_H)
    G = jnp.asarray(M * np.repeat(np.exp(lg * C), RET_DK)[:, None], F32)
    M = jnp.asarray(M, F32)
    EA = jnp.asarray(_head_block_mask(RET_DV, RET_DV, RET_H) / RET_DV, BF)
    nw = norm_w.reshape(1, -1)
    return pl.pallas_call(
        _ret_prompt_kernel,
        grid=(B, nT),
        in_specs=[pl.BlockSpec((TT, RET_IN_W), lambda b, t: (b * nT + t, 0)),
                  pl.BlockSpec((TT, LANE), lambda b, t: (t, 0)),
                  pl.BlockSpec((TT, LANE), lambda b, t: (t, 0)),
                  _const(Dm.shape), _const(rd.shape), _const(kd.shape), _const(G.shape), _const(M.shape),
                  _const(EA.shape), _const(nw.shape)],
        out_specs=[pl.BlockSpec((TT, RET_WIDTH), lambda b, t: (b * nT + t, 0)),
                   pl.BlockSpec((1, RET_H, RET_DK, RET_DV), lambda b, t: (b, 0, 0, 0))],
        out_shape=[jax.ShapeDtypeStruct((B * T, RET_WIDTH), BF),
                   jax.ShapeDtypeStruct((B, RET_H, RET_DK, RET_DV), F32)],
        scratch_shapes=[pltpu.VMEM((RET_H * RET_DK, RET_H * RET_DV), F32)],
        compiler_params=_cp(2),
        name="ret_prompt",
    )(rin, cos, sin, Dm, rd, kd, G, M, EA, nw)


def _ssd_conv(xp_ref, cw_ref, cb_ref, TT):
    acc = cb_ref[...] + cw_ref[SSD_CONV_W - 1:SSD_CONV_W, :] * xp_ref[pl.ds(8, TT), :]
    for i in range(SSD_CONV_W - 1):
        acc = acc + cw_ref[i:i + 1, :] * xp_ref[pl.ds(8 - (SSD_CONV_W - 1) + i, TT), :]
    return acc


def _ssd_intra(xs, bm, cm, g, dt, Mk_ref):
    TT = xs.shape[0]
    g2 = g * LOG2E
    rT = ((g - jnp.log(dt)) * LOG2E).T
    lane = lax.broadcasted_iota(jnp.int32, (1, LANE), 1)
    lane2 = lax.broadcasted_iota(jnp.int32, (1, 2 * LANE), 1)
    zero = jnp.zeros((), BF)
    causal = Mk_ref[...].astype(BF) > zero
    bmb = bm.astype(BF)
    o_parts = []
    for grp in range(SSD_G):
        cb = _dot_nt(jnp.where((lane // SSD_N) == grp, cm, 0.0).astype(BF), bmb).astype(BF)
        xg = xs[:, grp * 2 * LANE:(grp + 1) * 2 * LANE].astype(BF)
        og = jnp.zeros((TT, 2 * LANE), F32)
        for h4 in range(SSD_H // SSD_G):
            h = grp * (SSD_H // SSD_G) + h4
            dec = jnp.exp2(g2[:, h:h + 1] - rT[h:h + 1, :]).astype(BF)
            p = jnp.where(causal, cb * dec, zero)
            og = og + _dot(p, jnp.where((lane2 // SSD_P) == h4, xg, zero))
        o_parts.append(og)
    return jnp.concatenate(o_parts, axis=1)


def _ssd_prompt_kernel(x_ref, cw_ref, cb_ref, dtb_ref, alog_ref, dexp_ref, nw_ref, L_ref, Mk_ref, Eexp_ref, M2_ref,
                       res_ref, g1_ref, og_ref, or_ref, wo_ref, lg_ref, lb_ref,
                       o_ref, sfin_ref, cfin_ref, st_ref, xp_ref):
    TT = x_ref.shape[0]
    t = pl.program_id(1)

    @pl.when(t == 0)
    def _():
        st_ref[...] = jnp.zeros_like(st_ref)
        xp_ref[0:8, :] = jnp.zeros((8, SSD_CONV_DIM), F32)

    z = x_ref[:, 0:SSD_WIDTH]
    xp_ref[8:8 + TT, :] = x_ref[:, SSD_WIDTH:SSD_WIDTH + SSD_CONV_DIM]
    sdt = x_ref[:, SSD_WIDTH + SSD_CONV_DIM:SSD_IN_W]
    xbc = _silu(_ssd_conv(xp_ref, cw_ref, cb_ref, TT))
    tail = xp_ref[TT:TT + 8, :]
    xp_ref[0:8, :] = tail
    dt_all = _softplus(sdt + dtb_ref[...])
    la_all = dt_all * (-jnp.exp(alog_ref[...]))
    Eexp = Eexp_ref[...]
    C = L_ref.shape[0]
    S = st_ref[...]
    for i in range(TT // C):
        rows = slice(i * C, (i + 1) * C)
        xs = xbc[rows, 0:SSD_WIDTH]
        bm = xbc[rows, SSD_WIDTH:SSD_WIDTH + LANE]
        cm = xbc[rows, SSD_WIDTH + LANE:SSD_CONV_DIM]
        dt = dt_all[rows, :]
        g = _dot_3x(L_ref[...], la_all[rows, :])
        gl = g[C - 1:C, :]
        eg_x = _dot_x2(jnp.exp(g), Eexp)
        cw_x = _dot_x2(dt * jnp.exp(gl - g), Eexp)
        egl_x = _dot_x2(jnp.exp(gl), Eexp)

        o = _ssd_intra(xs, bm, cm, g, dt, Mk_ref)
        o = o + eg_x * _dot(cm.astype(BF), S.astype(BF))
        u = _dot_tn(bm.astype(BF), (xs * cw_x).astype(BF))
        S = S * egl_x + u * M2_ref[...]

        y = (o + dexp_ref[...] * xs) * _silu(z[rows, :])
        ms = jnp.mean(y * y, axis=-1, keepdims=True)
        o_ssd = (y * lax.rsqrt(ms + EPS) * nw_ref[...]).astype(BF)
        merged = jnp.concatenate([og_ref[rows, :], or_ref[rows, :], o_ssd], axis=1)
        mix = _dot(merged, wo_ref[...])
        o_ref[0, rows, :] = _layer_norm(ALPHA * res_ref[0, rows, :] + g1_ref[0] * mix, lg_ref[0], lb_ref[0])
    st_ref[...] = S

    @pl.when(t == pl.num_programs(1) - 1)
    def _():
        for h in range(SSD_H):
            gi = h // (SSD_H // SSD_G)
            sfin_ref[0, h] = S[gi * SSD_N:(gi + 1) * SSD_N, h * SSD_P:(h + 1) * SSD_P]
        cfin_ref[0] = tail[8 - (SSD_CONV_W - 1):8, :]


def _pad_lanes(v, n=LANE):
    v = v.reshape(1, -1)
    return jnp.zeros((1, n), F32).at[:, :v.shape[1]].set(v)


def _ssd_tables(TT, c):
    L = jnp.asarray(_block_tril(TT, c), BF)
    Mk = jnp.asarray(_block_tril(TT, c), F32)
    e = np.zeros((LANE, SSD_WIDTH), np.float32)
    for h in range(SSD_H):
        e[h, h * SSD_P:(h + 1) * SSD_P] = 1.0
    M2 = np.zeros((SSD_G * SSD_N, SSD_WIDTH), np.float32)
    for h in range(SSD_H):
        gi = h // (SSD_H // SSD_G)
        M2[gi * SSD_N:(gi + 1) * SSD_N, h * SSD_P:(h + 1) * SSD_P] = 1.0
    return L, Mk, jnp.asarray(e, BF), jnp.asarray(M2, F32)


def _ssd_params(conv_w, conv_b, dt_bias, a_log, d, norm_w):
    return (conv_w, conv_b.reshape(1, -1), _pad_lanes(dt_bias), _pad_lanes(a_log),
            jnp.repeat(d, SSD_P).reshape(1, -1), norm_w.reshape(1, -1))


def _ssd_prompt_call(sin_, x3, g1, og, orr, w_out, ln_g, ln_b, conv_w, conv_b, dt_bias, a_log, d, norm_w):
    B, T, D = x3.shape
    C = 256
    TT = 4 * C if T % (4 * C) == 0 else (2 * C if T % (2 * C) == 0 else C)
    nT = T // TT
    L, Mk, Eexp, M2 = _ssd_tables(C, C)
    prm = _ssd_params(conv_w, conv_b, dt_bias, a_log, d, norm_w)
    rmap = lambda b, t: (b * nT + t, 0)
    return pl.pallas_call(
        _ssd_prompt_kernel,
        grid=(B, nT),
        in_specs=[pl.BlockSpec((TT, SSD_IN_W), rmap)]
                 + [_const(p.shape) for p in prm]
                 + [_const(L.shape), _const(Mk.shape), _const(Eexp.shape), _const(M2.shape)]
                 + [pl.BlockSpec((1, TT, D), lambda b, t: (b, t, 0)),
                    pl.BlockSpec((1, 1, D), lambda b, t: (b, 0, 0)),
                    pl.BlockSpec((TT, GLA_WIDTH), rmap), pl.BlockSpec((TT, RET_WIDTH), rmap),
                    _const((D, D)), _const((1, 1, D)), _const((1, 1, D))],
        out_specs=[pl.BlockSpec((1, TT, D), lambda b, t: (b, t, 0)),
                   pl.BlockSpec((1, SSD_H, SSD_N, SSD_P), lambda b, t: (b, 0, 0, 0)),
                   pl.BlockSpec((1, SSD_CONV_W - 1, SSD_CONV_DIM), lambda b, t: (b, 0, 0))],
        out_shape=[jax.ShapeDtypeStruct((B, T, D), F32),
                   jax.ShapeDtypeStruct((B, SSD_H, SSD_N, SSD_P), F32),
                   jax.ShapeDtypeStruct((B, SSD_CONV_W - 1, SSD_CONV_DIM), F32)],
        scratch_shapes=[pltpu.VMEM((SSD_G * SSD_N, SSD_WIDTH), F32),
                        pltpu.VMEM((TT + 8, SSD_CONV_DIM), F32)],
        compiler_params=_cp(2),
        name="ssd_outproj_ln",
    )(sin_, *prm, L, Mk, Eexp, M2, x3, g1, og, orr, w_out, ln_g.reshape(1, 1, D), ln_b.reshape(1, 1, D))


def _inproj_t_kernel(x_ref, sc_ref, sh_ref, wt_ref, og_ref, or_ref, os_ref, w_ref):
    nt, nb, D = x_ref.shape

    @pl.when(pl.program_id(0) == 0)
    def _():
        for src, dst, n in ((0, 0, N_GA + GLA_GATE_RANK), (N_GA + GLA_GATE_RANK, N_GA + LANE, N_DT - N_GA - GLA_GATE_RANK)):
            for r in range(0, n, 512):
                m = min(512, n - r)
                w_ref[dst + r:dst + r + m, :] = wt_ref[src + r:src + r + m, :].astype(BF)
        w_ref[N_GA + GLA_GATE_RANK:N_GA + LANE, :] = jnp.zeros((LANE - GLA_GATE_RANK, D), BF)
        tail = jnp.concatenate([wt_ref[N_DT:N_IN, :], jnp.zeros((LANE - SSD_H, D), F32)], axis=0)
        w_ref[IN_W - LANE:IN_W, :] = tail.astype(BF)

    h = x_ref[...] * (1.0 + sc_ref[...]) + sh_ref[...]
    for t in range(0, nt, 2):
        ht = h[t:t + 2].reshape(2 * nb, D).astype(BF)
        cols = slice(t * nb, (t + 2) * nb)
        og_ref[:, cols] = _dot_nt(w_ref[0:GLA_IN_W, :], ht)
        or_ref[:, cols] = _dot_nt(w_ref[GLA_IN_W:GLA_IN_W + RET_IN_W, :], ht)
        os_ref[:, cols] = _dot_nt(w_ref[GLA_IN_W + RET_IN_W:IN_W, :], ht)


def _inproj_t_call(xt, sc, sh, wt, l):
    T, B, D = xt.shape
    nt = 4
    cmap = lambda i: (0, i)
    return pl.pallas_call(
        _inproj_t_kernel,
        grid=(T // nt,),
        in_specs=[pl.BlockSpec((nt, B, D), lambda i: (i, 0, 0)),
                  pl.BlockSpec((1, B, D), lambda i: (0, 0, 0)),
                  pl.BlockSpec((1, B, D), lambda i: (0, 0, 0)),
                  _resident_layer(wt.shape, l)],
        out_specs=[pl.BlockSpec((GLA_IN_W, nt * B), cmap),
                   pl.BlockSpec((RET_IN_W, nt * B), cmap),
                   pl.BlockSpec((SSD_IN_W, nt * B), cmap)],
        out_shape=[jax.ShapeDtypeStruct((GLA_IN_W, T * B), F32),
                   jax.ShapeDtypeStruct((RET_IN_W, T * B), F32),
                   jax.ShapeDtypeStruct((SSD_IN_W, T * B), F32)],
        scratch_shapes=[pltpu.VMEM((IN_W, D), BF)],
        compiler_params=_cp(1),
        name="in_proj_t",
    )(xt, sc, sh, wt)


def _row_sum(x):
    return jnp.sum(x, axis=0, keepdims=True)


def _lane_state_readout(o, coef_ref, s0_ref, n_rows):
    nb = LANE
    half = len(o) // 2
    for part in range(2):
        def body(k8, accs, part=part):
            accs = list(accs)
            base = pl.multiple_of(k8 * 8, 8)
            grp = [coef_ref[pl.ds(base, 8), (part * half + i) * nb:(part * half + i + 1) * nb] for i in range(half)]
            for j in range(8):
                s0k = s0_ref[0, k8 * 8 + j]
                for i in range(half):
                    accs[i] = accs[i] + grp[i][j:j + 1, :] * s0k
            return tuple(accs)

        res = lax.fori_loop(0, n_rows // 8, body, tuple(o[part * half:(part + 1) * half]))
        o[part * half:(part + 1) * half] = list(res)
    return o


def _lane_state_update(sn_ref, s0_ref, decay_fn, coef_ref, val_fn, n_rows, T):
    nb = LANE

    def body(k8, carry):
        base = pl.multiple_of(k8 * 8, 8)
        grp = [coef_ref[pl.ds(base, 8), t * nb:(t + 1) * nb] for t in range(T)]
        dec = decay_fn(base)
        for j in range(8):
            dj = dec[j:j + 1, :] if dec.shape[0] == 8 else dec
            sk = dj * s0_ref[0, k8 * 8 + j]
            for t in range(T):
                sk = sk + grp[t][j:j + 1, :] * val_fn(t)
            sn_ref[0, 0, k8 * 8 + j] = sk
        return carry

    lax.fori_loop(0, n_rows // 8, body, 0)


def _state_specs(shape, l, first):
    assert l == 0 or not first
    tail = tuple(shape[2:])
    in_spec = pl.BlockSpec((None, 1) + tail, lambda h: (l, h, 0, 0, 0))
    out_spec = pl.BlockSpec(((shape[0] if first else 1), 1) + tail, lambda h: (l, h, 0, 0, 0))
    return in_spec, out_spec


def _zero_later_layers(ref):
    ref[1:] = jnp.zeros((ref.shape[0] - 1,) + tuple(ref.shape[1:]), ref.dtype)


def _finish_state_call(kern, n_in, first, prevs):
    if first:
        return functools.partial(kern, first=True), [], {}
    wrapped = lambda *a, **kw: kern(*a[:n_in], *a[n_in + len(prevs):], first=False, **kw)
    specs = [pl.BlockSpec(memory_space=pl.ANY)] * len(prevs)
    return wrapped, specs, {n_in + i: 1 + i for i in range(len(prevs))}


def _gla_t_kernel(x_ref, s0_ref, wg_ref, bg_ref, nw_ref, o_ref, sn_ref, qe_ref, ke_ref, a_ref, *, T, first):
    nb = LANE
    if first:
        _zero_later_layers(sn_ref)
    h = pl.program_id(0)
    r0 = pl.multiple_of(h * GLA_DK, GLA_DK)
    v0 = pl.multiple_of(h * GLA_DV, GLA_DV)
    q = x_ref[pl.ds(r0, GLA_DK), :] * (GLA_DK ** -0.5)
    k = x_ref[pl.ds(128 + r0, GLA_DK), :]
    gate = _dot(wg_ref[pl.ds(r0, GLA_DK), :], x_ref[512:640, :].astype(BF)) + bg_ref[pl.ds(r0, GLA_DK), :]
    la = _log_sigmoid(gate) * (1.0 / GLA_GATE_TEMP)
    gs = []
    acc = jnp.zeros((GLA_DK, nb), F32)
    for t in range(T):
        acc = acc + la[:, t * nb:(t + 1) * nb]
        gs.append(acc)
    gl = gs[T - 1]
    a_ref[...] = jnp.exp(gl)
    qs = [q[:, t * nb:(t + 1) * nb] for t in range(T)]
    ks = [k[:, t * nb:(t + 1) * nb] for t in range(T)]
    for t in range(T):
        qe_ref[:, t * nb:(t + 1) * nb] = qs[t] * jnp.exp(gs[t])
        ke_ref[:, t * nb:(t + 1) * nb] = ks[t] * jnp.exp(gl - gs[t])

    def vt(t):
        return x_ref[pl.ds(256 + v0, GLA_DV), t * nb:(t + 1) * nb]

    o = []
    for t in range(T):
        ot = jnp.zeros((GLA_DV, nb), F32)
        for u in range(t + 1):
            s = _row_sum(qs[t] * ks[u] * jnp.exp(gs[t] - gs[u]))
            ot = ot + s * vt(u)
        o.append(ot)

    o = _lane_state_readout(o, qe_ref, s0_ref, GLA_DK)
    _lane_state_update(sn_ref, s0_ref, lambda base: a_ref[pl.ds(base, 8), :], ke_ref, vt, GLA_DK, T)

    nw = nw_ref[pl.ds(v0, GLA_DV), :]
    for t in range(T):
        ms = jnp.mean(o[t] * o[t], axis=0, keepdims=True)
        r = x_ref[pl.ds(640 + v0, GLA_DV), t * nb:(t + 1) * nb]
        o_ref[:, t * nb:(t + 1) * nb] = (o[t] * lax.rsqrt(ms + EPS) * nw * _silu(r)).astype(o_ref.dtype)


def _gla_t_call(gT, s0, prev, l, T, w_gate, b_gate, norm_w):
    N = gT.shape[1]
    wg = jnp.zeros((GLA_H * GLA_DK, LANE), F32).at[:, :GLA_GATE_RANK].set(w_gate.T).astype(BF)
    bg = b_gate.reshape(-1, 1)
    nw = norm_w.reshape(-1, 1)
    first = prev is None
    prevs = [] if first else [prev]
    s_in, s_out = _state_specs(s0.shape, l, first)
    ins = [gT, s0, wg, bg, nw]
    specs = [_const(gT.shape), s_in, _const(wg.shape), _const(bg.shape), _const(nw.shape)]
    kern, pspecs, aliases = _finish_state_call(functools.partial(_gla_t_kernel, T=T), len(ins), first, prevs)
    ins, specs = ins + prevs, specs + pspecs
    return pl.pallas_call(
        kern,
        grid=(GLA_H,),
        in_specs=specs,
        out_specs=[pl.BlockSpec((GLA_DV, N), lambda h: (h, 0)), s_out],
        out_shape=[jax.ShapeDtypeStruct((GLA_WIDTH, N), BF), jax.ShapeDtypeStruct(s0.shape, F32)],
        scratch_shapes=[pltpu.VMEM((GLA_DK, N), F32), pltpu.VMEM((GLA_DK, N), F32), pltpu.VMEM((GLA_DK, LANE), F32)],
        input_output_aliases=aliases,
        compiler_params=_cp(1),
        name="gla_t",
    )(*ins)


def _ret_t_kernel(x_ref, s0_ref, cos_ref, sin_ref, pw_ref, nw_ref, o_ref, sn_ref, qd_ref, kd_ref, *, T, first):
    nb = LANE
    if first:
        _zero_later_layers(sn_ref)
    h = pl.program_id(0)
    r0 = pl.multiple_of(h * RET_DK, RET_DK)
    half_k = RET_DK // 2
    cos, sin = cos_ref[...], sin_ref[...]

    def rope_t(base):
        x1 = x_ref[pl.ds(base + r0, half_k), :]
        x2 = x_ref[pl.ds(base + r0 + half_k, half_k), :]
        return jnp.concatenate([x1 * cos - x2 * sin, x1 * sin + x2 * cos], axis=0)

    q = rope_t(0)
    k = rope_t(256) * (RET_DK ** -0.5)
    pw = pw_ref[h]
    qs = [q[:, t * nb:(t + 1) * nb] for t in range(T)]
    ks = [k[:, t * nb:(t + 1) * nb] for t in range(T)]
    for t in range(T):
        qd_ref[:, t * nb:(t + 1) * nb] = qs[t] * pw[t + 1:t + 2, :]
        kd_ref[:, t * nb:(t + 1) * nb] = ks[t] * pw[T - 1 - t:T - t, :]

    def vt(t):
        return x_ref[pl.ds(512 + r0, RET_DV), t * nb:(t + 1) * nb]

    o = []
    for t in range(T):
        ot = jnp.zeros((RET_DV, nb), F32)
        for u in range(t + 1):
            s = _row_sum(qs[t] * ks[u]) * pw[t - u:t - u + 1, :]
            ot = ot + s * vt(u)
        o.append(ot)

    o = _lane_state_readout(o, qd_ref, s0_ref, RET_DK)
    _lane_state_update(sn_ref, s0_ref, lambda base: pw[T:T + 1, :], kd_ref, vt, RET_DK, T)

    nw = nw_ref[pl.ds(r0, RET_DV), :]
    for t in range(T):
        mu = jnp.mean(o[t], axis=0, keepdims=True)
        d = o[t] - mu
        var = jnp.mean(d * d, axis=0, keepdims=True)
        rg = x_ref[pl.ds(768 + r0, RET_DV), t * nb:(t + 1) * nb]
        o_ref[:, t * nb:(t + 1) * nb] = (d * lax.rsqrt(var + EPS) * nw * _silu(rg)).astype(o_ref.dtype)


def _ret_t_call(rT, s0, prev, l, T, norm_w):
    N = rT.shape[1]
    B = N // T
    half = RET_DK // 2
    inv = ROPE_BASE ** (-jnp.arange(half, dtype=F32) / half)
    ang = inv[:, None] * (PAST_LEN + jnp.arange(T, dtype=jnp.int32)).astype(F32)[None, :]
    cos = jnp.repeat(jnp.cos(ang), B, axis=1)
    sin = jnp.repeat(jnp.sin(ang), B, axis=1)
    lg = _ret_log_gamma()
    pw = jnp.asarray(np.repeat(np.exp(lg[:, None] * np.arange(16)[None, :])[:, :, None], LANE, axis=2), F32)
    nw = norm_w.reshape(-1, 1)
    first = prev is None
    prevs = [] if first else [prev]
    s_in, s_out = _state_specs(s0.shape, l, first)
    ins = [rT, s0, cos, sin, pw, nw]
    specs = [_const(rT.shape), s_in, _const(cos.shape), _const(sin.shape), _const(pw.shape), _const(nw.shape)]
    kern, pspecs, aliases = _finish_state_call(functools.partial(_ret_t_kernel, T=T), len(ins), first, prevs)
    ins, specs = ins + prevs, specs + pspecs
    return pl.pallas_call(
        kern,
        grid=(RET_H,),
        in_specs=specs,
        out_specs=[pl.BlockSpec((RET_DV, N), lambda h: (h, 0)), s_out],
        out_shape=[jax.ShapeDtypeStruct((RET_WIDTH, N), BF), jax.ShapeDtypeStruct(s0.shape, F32)],
        scratch_shapes=[pltpu.VMEM((RET_DK, N), F32), pltpu.VMEM((RET_DK, N), F32)],
        input_output_aliases=aliases,
        compiler_params=_cp(1),
        name="ret_t",
    )(*ins)


def _ssd_t_kernel(x_ref, c0_ref, s0_ref, cw_ref, cb_ref, dtb_ref, alog_ref, d_ref, nw_ref,
                  o_ref, sn_ref, cn_ref, hist_ref, y_ref, ssq_ref, cm_ref, bw_ref, xw_ref, *, T, first):
    nb = LANE
    W1 = SSD_CONV_W - 1
    h = pl.program_id(0)
    XB = SSD_WIDTH
    if first:
        _zero_later_layers(sn_ref)

    @pl.when(h == 0)
    def _():
        ssq_ref[...] = jnp.zeros_like(ssq_ref)
        if first:
            _zero_later_layers(cn_ref)
        for i in range(W1):
            for j in range(SSD_CONV_DIM // LANE):
                hist_ref[j * LANE:(j + 1) * LANE, i * nb:(i + 1) * nb] = c0_ref[0, i][:, j * LANE:(j + 1) * LANE].T
                cn_ref[0, i, :, j * LANE:(j + 1) * LANE] = \
                    x_ref[XB + j * LANE:XB + (j + 1) * LANE, (T - W1 + i) * nb:(T - W1 + i + 1) * nb].T

    def conv_rows(ro):
        w = cw_ref[pl.ds(ro, 64), :]
        b = cb_ref[pl.ds(ro, 64), :]
        xx = [hist_ref[pl.ds(ro, 64), i * nb:(i + 1) * nb] for i in range(W1)]
        xx += [x_ref[pl.ds(XB + ro, 64), t * nb:(t + 1) * nb] for t in range(T)]
        out = []
        for t in range(T):
            acc = b + w[:, 0:1] * xx[t]
            for i in range(1, SSD_CONV_W):
                acc = acc + w[:, i:i + 1] * xx[t + i]
            out.append(_silu(acc))
        return out

    grp = h // (SSD_H // SSD_G)
    xs = conv_rows(pl.multiple_of(h * SSD_P, SSD_P))
    bm = conv_rows(pl.multiple_of(SSD_WIDTH + grp * SSD_N, SSD_N))
    cm = conv_rows(pl.multiple_of(SSD_WIDTH + SSD_G * SSD_N + grp * SSD_N, SSD_N))

    dt_all = _softplus(x_ref[pl.ds(XB + SSD_CONV_DIM + h, 1), :] + dtb_ref[pl.ds(h, 1), :])
    a = -jnp.exp(alog_ref[pl.ds(h, 1), :])
    dts = [dt_all[:, t * nb:(t + 1) * nb] for t in range(T)]
    gs = []
    acc = jnp.zeros((1, nb), F32)
    for t in range(T):
        acc = acc + dts[t] * a
        gs.append(acc)
    gl = gs[T - 1]

    o = []
    for t in range(T):
        ot = jnp.zeros((SSD_P, nb), F32)
        for u in range(t + 1):
            s = _row_sum(cm[t] * bm[u]) * (jnp.exp(gs[t] - gs[u]) * dts[u])
            ot = ot + s * xs[u]
        o.append(ot)

    for t in range(T):
        cm_ref[:, t * nb:(t + 1) * nb] = cm[t] * jnp.exp(gs[t])
        bw_ref[:, t * nb:(t + 1) * nb] = bm[t]
        xw_ref[:, t * nb:(t + 1) * nb] = xs[t] * (dts[t] * jnp.exp(gl - gs[t]))

    o = _lane_state_readout(o, cm_ref, s0_ref, SSD_N)
    egl = jnp.exp(gl)
    _lane_state_update(sn_ref, s0_ref, lambda base: egl, bw_ref, lambda t: xw_ref[:, t * nb:(t + 1) * nb], SSD_N, T)

    dd = d_ref[pl.ds(h, 1), :]
    p0 = pl.multiple_of(h * SSD_P, SSD_P)
    for t in range(T):
        z = x_ref[pl.ds(p0, SSD_P), t * nb:(t + 1) * nb]
        y = (o[t] + dd * xs[t]) * _silu(z)
        y_ref[pl.ds(p0, SSD_P), t * nb:(t + 1) * nb] = y
        ssq_ref[:, t * nb:(t + 1) * nb] += _row_sum(y * y)

    @pl.when(h == SSD_H - 1)
    def _():
        scale = lax.rsqrt(ssq_ref[...] * (1.0 / SSD_WIDTH) + EPS)
        o_ref[...] = (y_ref[...] * scale * nw_ref[...]).astype(o_ref.dtype)


def _ssd_t_call(sT, c0, s0, prev_s, prev_c, l, T, conv_w, conv_b, dt_bias, a_log, d, norm_w):
    N = sT.shape[1]
    col = lambda v: jnp.zeros((LANE, 1), F32).at[:SSD_H, 0].set(v)
    prm = (conv_w.T, conv_b.reshape(-1, 1), col(dt_bias), col(a_log), col(d), norm_w.reshape(-1, 1))
    first = prev_s is None
    prevs = [] if first else [prev_s, prev_c]
    s_in, s_out = _state_specs(s0.shape, l, first)
    c_in = pl.BlockSpec((1,) + tuple(c0.shape[1:]), lambda h: (l, 0, 0, 0))
    c_out = pl.BlockSpec(((c0.shape[0] if first else 1),) + tuple(c0.shape[1:]), lambda h: (l, 0, 0, 0))
    ins = [sT, c0, s0, *prm]
    specs = [_const(sT.shape), c_in, s_in] + [_const(p.shape) for p in prm]
    kern, pspecs, aliases = _finish_state_call(functools.partial(_ssd_t_kernel, T=T), len(ins), first, prevs)
    ins, specs = ins + prevs, specs + pspecs
    return pl.pallas_call(
        kern,
        grid=(SSD_H,),
        in_specs=specs,
        out_specs=[_const((SSD_WIDTH, N)), s_out, c_out],
        out_shape=[jax.ShapeDtypeStruct((SSD_WIDTH, N), BF), jax.ShapeDtypeStruct(s0.shape, F32),
                   jax.ShapeDtypeStruct(c0.shape, F32)],
        scratch_shapes=[pltpu.VMEM((SSD_CONV_DIM, (SSD_CONV_W - 1) * LANE), F32),
                        pltpu.VMEM((SSD_WIDTH, N), F32), pltpu.VMEM((1, N), F32),
                        pltpu.VMEM((SSD_N, N), F32), pltpu.VMEM((SSD_N, N), F32), pltpu.VMEM((SSD_P, N), F32)],
        input_output_aliases=aliases,
        compiler_params=_cp(1),
        name="ssd_t",
    )(*ins)


def _outproj_t_kernel(x_ref, g_ref, og_ref, or_ref, os_ref, w_ref, lg_ref, lb_ref, o_ref):
    nt, nb, D = x_ref.shape
    for t in range(nt):
        cols = slice(t * nb, (t + 1) * nb)
        mix = (_dot_tn(og_ref[:, cols], w_ref[0:GLA_WIDTH, :])
               + _dot_tn(or_ref[:, cols], w_ref[GLA_WIDTH:GLA_WIDTH + RET_WIDTH, :])
               + _dot_tn(os_ref[:, cols], w_ref[GLA_WIDTH + RET_WIDTH:D, :]))
        y = ALPHA * x_ref[t] + g_ref[0] * mix
        o_ref[t] = _layer_norm(y, lg_ref[0], lb_ref[0])


def _outproj_t_call(xt, g1, ogT, orT, osT, w_out, ln_g, ln_b):
    T, B, D = xt.shape
    nt = 4
    cmap = lambda i: (0, i)
    return pl.pallas_call(
        _outproj_t_kernel,
        grid=(T // nt,),
        in_specs=[pl.BlockSpec((nt, B, D), lambda i: (i, 0, 0)),
                  pl.BlockSpec((1, B, D), lambda i: (0, 0, 0)),
                  pl.BlockSpec((GLA_WIDTH, nt * B), cmap),
                  pl.BlockSpec((RET_WIDTH, nt * B), cmap),
                  pl.BlockSpec((SSD_WIDTH, nt * B), cmap),
                  _const((D, D)), _const((1, 1, D)), _const((1, 1, D))],
        out_specs=pl.BlockSpec((nt, B, D), lambda i: (i, 0, 0)),
        out_shape=jax.ShapeDtypeStruct((T, B, D), F32),
        compiler_params=_cp(1),
        name="out_proj_ln_t",
    )(xt, g1, ogT, orT, osT, w_out, ln_g.reshape(1, 1, D), ln_b.reshape(1, 1, D))


ROUTE_OFF = 8


def _moe_route_t(lt):
    neg = jnp.float32(-jnp.inf)
    row8 = lax.broadcasted_iota(jnp.int32, (8, 1), 0)
    lg = jnp.where(row8 < MOE_GROUPS, lt[0:8, :], neg)
    mg = jnp.max(lg, axis=0, keepdims=True)
    gsel = jnp.min(jnp.where(lg == mg, row8, 8), axis=0, keepdims=True)
    g_gate = 1.0 / jnp.sum(jnp.exp(lg - mg), axis=0, keepdims=True)
    rowe = lax.broadcasted_iota(jnp.int32, (MOE_EXPERTS, 1), 0)
    le = jnp.where((rowe // MOE_PER_GROUP) == gsel, lt[ROUTE_OFF:ROUTE_OFF + MOE_EXPERTS, :], neg)
    m1 = jnp.max(le, axis=0, keepdims=True)
    i1 = jnp.min(jnp.where(le == m1, rowe, MOE_EXPERTS), axis=0, keepdims=True)
    le2 = jnp.where(rowe == i1, neg, le)
    m2 = jnp.max(le2, axis=0, keepdims=True)
    i2 = jnp.min(jnp.where(le2 == m2, rowe, MOE_EXPERTS), axis=0, keepdims=True)
    e2 = jnp.exp(m2 - m1)
    w1 = g_gate / (1.0 + e2)
    w2 = g_gate * e2 / (1.0 + e2)
    comb = jnp.where(rowe == i1, w1, jnp.where(rowe == i2, w2, 0.0))
    cg = comb[0:4, :]
    for g in range(1, MOE_GROUPS):
        cg = cg + comb[g * MOE_PER_GROUP:(g + 1) * MOE_PER_GROUP, :]
    return gsel, cg, comb


MOE_SUB = 256
MOE_BLK = 16
MOE_ROWS = 256
MOE_NPS = MOE_SUB + MOE_GROUPS * MOE_BLK
assert MOE_SUB <= MOE_ROWS


def _moe_kernel(x_ref, sc_ref, sh_ref, g_ref, wr_ref, br_ref, us_ref, w1_ref, w3_ref, w2_ref, lg_ref, lb_ref,
                o_ref, hb_ref, cwb_ref, hp_ref, cwp_ref, yp_ref, pos_ref,
                cgrp_ref, fill_ref, cur_ref, na_ref, nb_ref, so_ref, nfa_ref, dsa_ref, dsb_ref, *, n_steps):
    bB, bT, D = x_ref.shape
    R = bB * bT
    n_q = R // MOE_SUB
    s = pl.program_id(1)
    x = x_ref[...]
    row8 = lax.broadcasted_iota(jnp.int32, (8, 1), 0)
    slot = lax.broadcasted_iota(jnp.int32, (MOE_NPS, 1), 0).astype(F32)

    @pl.when((pl.program_id(0) == 0) & (s == 0))
    def _():
        hb_ref[...] = jnp.zeros_like(hb_ref)
        cwb_ref[...] = jnp.zeros_like(cwb_ref)
        yp_ref[...] = jnp.zeros_like(yp_ref)

    @pl.when(s == 0)
    def _():
        na_ref[0] = 0
        for g in range(MOE_GROUPS):
            cur_ref[g] = -1
            fill_ref[g] = 0

    @pl.when(s < n_steps)
    def _():
        h = (x * (1.0 + sc_ref[...]) + sh_ref[...]).reshape(R, D)
        segs, offs = [], []
        for q in range(n_q):
            u = s * n_q + q
            hq = h[q * MOE_SUB:(q + 1) * MOE_SUB, :].astype(BF)
            gsel, cg, _ = _moe_route_t(_dot_nt(wr_ref[...], hq) + br_ref[...])
            onehot = jnp.where(row8 == gsel, 1.0, 0.0)
            rank = _dot(onehot.astype(BF), us_ref[...])
            cnt = jnp.sum(onehot, axis=1, keepdims=True)
            seg = jnp.ceil(cnt * (1.0 / MOE_BLK)) * MOE_BLK
            off = jnp.zeros((8, 1), F32)
            for g in range(1, MOE_GROUPS):
                off = off + jnp.where(row8 >= g, seg[g - 1:g, :], 0.0)
            pos = jnp.sum(onehot * (off + rank), axis=0, keepdims=True)
            pos_ref[u] = jnp.broadcast_to(pos, (8, MOE_SUB))
            perm = jnp.where(slot == pos, 1.0, 0.0).astype(BF)
            hp_ref[q] = _dot(perm, hq).astype(BF)
            cg8 = jnp.concatenate([cg, jnp.zeros((4, MOE_SUB), F32)], axis=0)
            cg_hi = cg8.astype(BF)
            cg_lo = (cg8 - cg_hi.astype(F32)).astype(BF)
            cwp_ref[q] = _dot_nt(perm, cg_hi) + _dot_nt(perm, cg_lo)
            segs.append(seg)
            offs.append(off)
        for q in range(n_q):
            u = s * n_q + q
            for g in range(MOE_GROUPS):
                so = offs[q][g, 0].astype(jnp.int32)
                nb = (segs[q][g, 0] * (1.0 / MOE_BLK)).astype(jnp.int32)
                f = fill_ref[g]
                c = cur_ref[g]
                na = na_ref[0]
                room = jnp.where(c < 0, 0, (MOE_ROWS - f) // MOE_BLK)
                n_a = jnp.minimum(nb, room)
                n_b = nb - n_a
                base_a = c * MOE_ROWS + f
                base_b = na * MOE_ROWS
                idx = u * MOE_GROUPS + g
                so_ref[idx] = so
                nb_ref[idx] = nb
                nfa_ref[idx] = n_a
                dsa_ref[idx] = base_a
                dsb_ref[idx] = base_b

                def put(k, carry, so=so, q=q, n_a=n_a, base_a=base_a, base_b=base_b):
                    dst = pl.multiple_of(jnp.where(k < n_a, base_a + k * MOE_BLK, base_b + (k - n_a) * MOE_BLK), MOE_BLK)
                    src = pl.multiple_of(so + k * MOE_BLK, MOE_BLK)
                    hb_ref[pl.ds(dst, MOE_BLK), :] = hp_ref[q, pl.ds(src, MOE_BLK), :]
                    cwb_ref[pl.ds(dst, MOE_BLK), :] = cwp_ref[q, pl.ds(src, MOE_BLK), :]
                    return carry

                lax.fori_loop(0, nb, put, 0)

                @pl.when(n_b > 0)
                def _(g=g, na=na, n_b=n_b):
                    cgrp_ref[na] = g
                    na_ref[0] = na + 1
                    cur_ref[g] = na
                    fill_ref[g] = n_b * MOE_BLK

                @pl.when(n_b == 0)
                def _(g=g, f=f, n_a=n_a):
                    fill_ref[g] = f + n_a * MOE_BLK

    @pl.when(s == n_steps - 1)
    def _():
        def chunk(c, carry):
            g = cgrp_ref[c]
            start = pl.multiple_of(c * MOE_ROWS, MOE_ROWS)
            hc = hb_ref[pl.ds(start, MOE_ROWS), :]
            cw = cwb_ref[pl.ds(start, MOE_ROWS), :]
            hids = []
            for j in range(MOE_PER_GROUP):
                e = g * MOE_PER_GROUP + j
                hid = _silu(_dot(hc, w1_ref[e])) * _dot(hc, w3_ref[e]) * cw[:, j:j + 1]
                hids.append(hid.astype(BF))
            w2g = w2_ref[pl.ds(g * MOE_PER_GROUP, MOE_PER_GROUP)].reshape(MOE_PER_GROUP * MOE_FF, D)
            hb_ref[pl.ds(start, MOE_ROWS), :] = _dot(jnp.concatenate(hids, axis=1), w2g).astype(BF)
            return carry

        lax.fori_loop(0, na_ref[0], chunk, 0)

    @pl.when(s >= n_steps)
    def _():
        for q in range(n_q):
            u = (s - n_steps) * n_q + q
            for g in range(MOE_GROUPS):
                idx = u * MOE_GROUPS + g
                so, n_a, base_a, base_b = so_ref[idx], nfa_ref[idx], dsa_ref[idx], dsb_ref[idx]

                def take(k, carry, so=so, q=q, n_a=n_a, base_a=base_a, base_b=base_b):
                    src = pl.multiple_of(jnp.where(k < n_a, base_a + k * MOE_BLK, base_b + (k - n_a) * MOE_BLK), MOE_BLK)
                    dst = pl.multiple_of(so + k * MOE_BLK, MOE_BLK)
                    yp_ref[q, pl.ds(dst, MOE_BLK), :] = hb_ref[pl.ds(src, MOE_BLK), :]
                    return carry

                lax.fori_loop(0, nb_ref[idx], take, 0)
        ys = []
        for q in range(n_q):
            u = (s - n_steps) * n_q + q
            perm = jnp.where(slot == pos_ref[u][0:1, :], 1.0, 0.0).astype(BF)
            ys.append(_dot_tn(perm, yp_ref[q]))
        y = jnp.concatenate(ys, axis=0)
        z = ALPHA * x + g_ref[...] * y.reshape(bB, bT, D)
        o_ref[...] = _layer_norm(z, lg_ref[...], lb_ref[...])


def _moe_call(x3, sc, sh, g2, wr, br, w1, w3, w2, l, ln_g, ln_b):
    B, T, D = x3.shape
    bB, bT = _tok_tiles(B, T)
    R = bB * bT
    if bB == 1:
        spp = 2 if B % 2 == 0 else 1
        nT = T // bT
        n_pools, n_steps = B // spp, spp * nT
        xmap = lambda p, s: (p * spp + (s % n_steps) // nT, (s % n_steps) % nT, 0)
        omap = lambda p, s: (p * spp + jnp.maximum(s - n_steps, 0) // nT, jnp.maximum(s - n_steps, 0) % nT, 0)
        mmap = lambda p, s: (p * spp + (s % n_steps) // nT, 0, 0)
        mshape = (1, 1, D)
    else:
        n_pools, n_steps = 1, B // bB
        xmap = lambda p, s: (s % n_steps, 0, 0)
        omap = lambda p, s: (jnp.maximum(s - n_steps, 0), 0, 0)
        mmap = lambda p, s: (0, 0, 0)
        mshape = (1, bT, D)
    n_sub = n_steps * (R // MOE_SUB)
    n_chunks = pl.cdiv(n_sub * (MOE_SUB + MOE_GROUPS * (MOE_BLK - 1)), MOE_ROWS) + MOE_GROUPS
    us = jnp.asarray(np.triu(np.ones((MOE_SUB, MOE_SUB), np.float32), 1), BF)
    smem = lambda n: pltpu.SMEM((n,), jnp.int32)
    return pl.pallas_call(
        functools.partial(_moe_kernel, n_steps=n_steps),
        grid=(n_pools, 2 * n_steps),
        in_specs=[pl.BlockSpec((bB, bT, D), xmap),
                  pl.BlockSpec(mshape, mmap), pl.BlockSpec(mshape, mmap), pl.BlockSpec(mshape, mmap),
                  _const(wr.shape), _const(br.shape), _const(us.shape),
                  _resident_layer(w1.shape, l), _resident_layer(w3.shape, l), _resident_layer(w2.shape, l),
                  _const((1, 1, D)), _const((1, 1, D))],
        out_specs=pl.BlockSpec((bB, bT, D), omap),
        out_shape=jax.ShapeDtypeStruct((B, T, D), F32),
        scratch_shapes=[pltpu.VMEM((n_chunks * MOE_ROWS, D), BF), pltpu.VMEM((n_chunks * MOE_ROWS, 8), F32),
                        pltpu.VMEM((R // MOE_SUB, MOE_NPS, D), BF), pltpu.VMEM((R // MOE_SUB, MOE_NPS, 8), F32),
                        pltpu.VMEM((R // MOE_SUB, MOE_NPS, D), BF),
                        pltpu.VMEM((n_sub, 8, MOE_SUB), F32),
                        smem(n_chunks), smem(MOE_GROUPS), smem(MOE_GROUPS), smem(1),
                        *[smem(n_sub * MOE_GROUPS) for _ in range(5)]],
        compiler_params=_cp(2),
        name="moe_ln",
    )(x3, sc, sh, g2, wr, br, us, w1, w3, w2, ln_g.reshape(1, 1, D), ln_b.reshape(1, 1, D))


def _router_params(w_group, b_group, w_expert, b_expert):
    wr = jnp.zeros((LANE, D_MODEL), F32).at[:MOE_GROUPS].set(w_group.T)
    wr = wr.at[ROUTE_OFF:ROUTE_OFF + MOE_EXPERTS].set(w_expert.T)
    br = jnp.zeros((LANE, 1), F32).at[:MOE_GROUPS, 0].set(b_group).at[ROUTE_OFF:ROUTE_OFF + MOE_EXPERTS, 0].set(b_expert)
    return wr.astype(BF), br


def kernel(x_prompt, x_sample, c_prompt, c_sample, state_gla, state_ret, state_ssd, state_conv, w_ada, b_ada, w_in, gla_w_gate, gla_b_gate, gla_norm, ret_norm, ssd_conv_w, ssd_conv_b, ssd_dt_bias, ssd_a_log, ssd_d, ssd_norm, w_out, ln1_g, ln1_b, moe_w_group, moe_b_group, moe_w_expert, moe_b_expert, moe_w1, moe_w3, moe_w2, ln2_g, ln2_b):
    Bp, Tp, D = x_prompt.shape
    Bs, Ts, _ = x_sample.shape
    w_in_t = jnp.swapaxes(w_in, 1, 2)
    w_out_b = w_out.astype(BF)
    w1_b, w3_b, w2_b = moe_w1.astype(BF), moe_w3.astype(BF), moe_w2.astype(BF)

    mod = _mod_call(jnp.concatenate([c_prompt, c_sample], axis=0), w_ada, b_ada)

    def moe(x, sc2, sh2, g2, l):
        wr, br = _router_params(moe_w_group[l], moe_b_group[l], moe_w_expert[l], moe_b_expert[l])
        return _moe_call(x, sc2, sh2, g2, wr, br, w1_b, w3_b, w2_b, l, ln2_g[l], ln2_b[l])

    x = x_prompt
    new = [[], [], [], []]
    for l in range(DEPTH):
        sh1, sc1, g1, sh2, sc2, g2 = (mod[l, :Bp, None, i * D:(i + 1) * D] for i in range(6))
        gin, rin, sin_ = _inproj_call(x, sc1, sh1, w_in_t, l)
        og, s_gla = _gla_prompt_call(gin, Bp, Tp, gla_w_gate[l], gla_b_gate[l], gla_norm[l])
        orr, s_ret = _ret_prompt_call(rin, Bp, Tp, ret_norm[l])
        x, s_ssd, s_conv = _ssd_prompt_call(sin_, x, g1, og, orr, w_out_b[l], ln1_g[l], ln1_b[l], ssd_conv_w[l],
                                            ssd_conv_b[l], ssd_dt_bias[l], ssd_a_log[l], ssd_d[l], ssd_norm[l])
        x = moe(x, sc2, sh2, g2, l)
        for acc, s in zip(new, (s_gla, s_ret, s_ssd, s_conv)):
            acc.append(s)
    y_p = x
    gla_p, ret_p, ssd_p, conv_p = (jnp.stack(a) for a in new)

    x = jnp.swapaxes(x_sample, 0, 1)
    sg = jnp.transpose(state_gla, (0, 2, 3, 4, 1))
    sr = jnp.transpose(state_ret, (0, 2, 3, 4, 1))
    ss = jnp.transpose(state_ssd, (0, 2, 3, 4, 1))
    cv = jnp.transpose(state_conv, (0, 2, 1, 3))
    gla_n = ret_n = ssd_n = conv_n = None
    for l in range(DEPTH):
        sh1, sc1, g1, sh2, sc2, g2 = (mod[l, None, Bp:, i * D:(i + 1) * D] for i in range(6))
        gT, rT, sT = _inproj_t_call(x, sc1, sh1, w_in_t, l)
        ogT, gla_n = _gla_t_call(gT, sg, gla_n, l, Ts, gla_w_gate[l], gla_b_gate[l], gla_norm[l])
        orT, ret_n = _ret_t_call(rT, sr, ret_n, l, Ts, ret_norm[l])
        osT, ssd_n, conv_n = _ssd_t_call(sT, cv, ss, ssd_n, conv_n, l, Ts, ssd_conv_w[l], ssd_conv_b[l],
                                         ssd_dt_bias[l], ssd_a_log[l], ssd_d[l], ssd_norm[l])
        x = _outproj_t_call(x, g1, ogT, orT, osT, w_out_b[l], ln1_g[l], ln1_b[l])
        x = moe(x, sc2, sh2, g2, l)
    y_s = jnp.swapaxes(x, 0, 1)
    gla_s = jnp.transpose(gla_n, (0, 4, 1, 2, 3))
    ret_s = jnp.transpose(ret_n, (0, 4, 1, 2, 3))
    ssd_s = jnp.transpose(ssd_n, (0, 4, 1, 2, 3))
    conv_s = jnp.transpose(conv_n, (0, 2, 1, 3))
    return (y_p, y_s, gla_p, ret_p, ssd_p, conv_p, gla_s, ret_s, ssd_s, conv_s)
```

```python
import functools

import numpy as np
import jax
import jax.numpy as jnp
from jax import lax
from jax.experimental import pallas as pl
from jax.experimental.pallas import tpu as pltpu

F32 = jnp.float32
BF = jnp.bfloat16

D_MODEL = 1024
DEPTH = 2
PAST_LEN = 16384
GLA_H, GLA_DK, GLA_DV = 4, 32, 64
GLA_WIDTH = GLA_H * GLA_DV
GLA_GATE_RANK = 16
GLA_GATE_TEMP = 16.0
GLA_CHUNK = 16
RET_H, RET_DK, RET_DV = 4, 64, 64
RET_WIDTH = RET_H * RET_DV
ROPE_BASE = 10000.0
SSD_H, SSD_P, SSD_G, SSD_N = 8, 64, 2, 64
SSD_WIDTH = SSD_H * SSD_P
SSD_CONV_W = 4
SSD_CONV_DIM = SSD_WIDTH + 2 * SSD_G * SSD_N
MOE_GROUPS, MOE_PER_GROUP = 4, 4
MOE_EXPERTS = MOE_GROUPS * MOE_PER_GROUP
MOE_FF = 256
ALPHA = (2 * DEPTH) ** 0.25
EPS = 1e-5
LOG2E = 1.4426950408889634

LANE = 128
GLA_IN_W = 128 + 128 + 256 + LANE + 256
RET_IN_W = 4 * 256
SSD_IN_W = 512 + SSD_CONV_DIM + LANE
IN_W = GLA_IN_W + RET_IN_W + SSD_IN_W
VMEM_LIMIT = 56 * 1024 * 1024


def _cp(n_axes, vmem=VMEM_LIMIT):
    return pltpu.CompilerParams(dimension_semantics=("arbitrary",) * n_axes, vmem_limit_bytes=vmem)


def _dot(a, b):
    return jnp.dot(a, b, preferred_element_type=F32)


def _dot_nt(a, b):
    return lax.dot_general(a, b, (((1,), (1,)), ((), ())), preferred_element_type=F32)


def _dot_tn(a, b):
    return lax.dot_general(a, b, (((0,), (0,)), ((), ())), preferred_element_type=F32)


def _split3(x):
    hi = x.astype(BF)
    r = x - hi.astype(F32)
    mid = r.astype(BF)
    lo = (r - mid.astype(F32)).astype(BF)
    return hi, mid, lo


def _dot_x2(x, e):
    hi = x.astype(BF)
    lo = (x - hi.astype(F32)).astype(BF)
    return _dot(hi, e) + _dot(lo, e)


def _dot_3x(e, x):
    hi, mid, lo = _split3(x)
    return _dot(e, hi) + (_dot(e, mid) + _dot(e, lo))


def _sigmoid(x):
    return 1.0 / (1.0 + jnp.exp2(x * (-LOG2E)))


def _silu(x):
    return x * _sigmoid(x)


def _log_sigmoid(x):
    return jnp.minimum(x, 0.0) - jnp.log(1.0 + jnp.exp(-jnp.abs(x)))


def _softplus(x):
    return jnp.maximum(x, 0.0) + jnp.log(1.0 + jnp.exp(-jnp.abs(x)))


def _layer_norm(x, g, b):
    mu = jnp.mean(x, axis=-1, keepdims=True)
    d = x - mu
    var = jnp.mean(d * d, axis=-1, keepdims=True)
    return d * lax.rsqrt(var + EPS) * g + b


def _const(shape):
    return pl.BlockSpec(shape, lambda *_: (0,) * len(shape))


def _resident_layer(shape, l):
    return pl.BlockSpec((None,) + tuple(shape[1:]), lambda *_: (l,) + (0,) * (len(shape) - 1),
                        pipeline_mode=pl.Buffered(1))


def _mod_kernel(c_ref, w_ref, b_ref, o_ref):
    s = _silu(c_ref[...]).astype(BF)
    o_ref[0] = _dot(s, w_ref[0].astype(BF)) + b_ref[0]


def _mod_call(c_all, w_ada, b_ada):
    R = c_all.shape[0]
    tn = 1536
    return pl.pallas_call(
        _mod_kernel,
        grid=(DEPTH, 6 * D_MODEL // tn),
        in_specs=[pl.BlockSpec((R, D_MODEL), lambda l, j: (0, 0)),
                  pl.BlockSpec((1, D_MODEL, tn), lambda l, j: (l, 0, j)),
                  pl.BlockSpec((1, 1, tn), lambda l, j: (l, 0, j))],
        out_specs=pl.BlockSpec((1, R, tn), lambda l, j: (l, 0, j)),
        out_shape=jax.ShapeDtypeStruct((DEPTH, R, 6 * D_MODEL), F32),
        compiler_params=_cp(2),
        name="ada_mod",
    )(c_all, w_ada, b_ada.reshape(DEPTH, 1, 6 * D_MODEL))


N_IN = 3096
N_GA = 128 + 128 + 256
N_DT = N_IN - SSD_H


def _inproj_kernel(x_ref, sc_ref, sh_ref, wt_ref, og_ref, or_ref, os_ref, w_ref):
    bB, bT, D = x_ref.shape

    @pl.when((pl.program_id(0) == 0) & (pl.program_id(1) == 0))
    def _():
        lane = lax.broadcasted_iota(jnp.int32, (1, LANE), 1)
        for j in range(N_GA // LANE):
            w_ref[:, j * LANE:(j + 1) * LANE] = wt_ref[j * LANE:(j + 1) * LANE, :].T.astype(BF)
        ga = wt_ref[N_GA:N_GA + LANE, :].T
        w_ref[:, N_GA:N_GA + LANE] = jnp.where(lane < GLA_GATE_RANK, ga, 0.0).astype(BF)
        src0, dst0 = N_GA + GLA_GATE_RANK, N_GA + LANE
        for j in range((N_DT - src0) // LANE):
            w_ref[:, dst0 + j * LANE:dst0 + (j + 1) * LANE] = \
                wt_ref[src0 + j * LANE:src0 + (j + 1) * LANE, :].T.astype(BF)
        dt = pltpu.roll(wt_ref[N_IN - LANE:N_IN, :].T, SSD_H, 1)
        w_ref[:, IN_W - LANE:IN_W] = jnp.where(lane < SSD_H, dt, 0.0).astype(BF)

    h = x_ref[...] * (1.0 + sc_ref[...]) + sh_ref[...]
    hb = h.reshape(bB * bT, D).astype(BF)
    cut = 2 * GLA_IN_W
    pa = _dot(hb, w_ref[:, 0:cut])
    pb = _dot(hb, w_ref[:, cut:IN_W])
    og_ref[...] = pa[:, 0:GLA_IN_W]
    or_ref[:, 0:cut - GLA_IN_W] = pa[:, GLA_IN_W:cut]
    or_ref[:, cut - GLA_IN_W:RET_IN_W] = pb[:, 0:GLA_IN_W + RET_IN_W - cut]
    os_ref[...] = pb[:, GLA_IN_W + RET_IN_W - cut:IN_W - cut]


def _tok_tiles(B, T):
    if T >= 512:
        return 1, 512
    return 512 // T, T


def _inproj_call(x3, sc, sh, wt, l):
    B, T, D = x3.shape
    bB, bT = _tok_tiles(B, T)
    nT = T // bT
    R = bB * bT
    N = B * T
    xmap = lambda i, j: (i, j, 0)
    mmap = lambda i, j: (i, 0, 0)
    omap = lambda i, j: (i * nT + j, 0)
    return pl.pallas_call(
        _inproj_kernel,
        grid=(B // bB, nT),
        in_specs=[pl.BlockSpec((bB, bT, D), xmap),
                  pl.BlockSpec((bB, 1, D), mmap),
                  pl.BlockSpec((bB, 1, D), mmap),
                  _resident_layer(wt.shape, l)],
        out_specs=[pl.BlockSpec((R, GLA_IN_W), omap),
                   pl.BlockSpec((R, RET_IN_W), omap),
                   pl.BlockSpec((R, SSD_IN_W), omap)],
        out_shape=[jax.ShapeDtypeStruct((N, GLA_IN_W), F32),
                   jax.ShapeDtypeStruct((N, RET_IN_W), F32),
                   jax.ShapeDtypeStruct((N, SSD_IN_W), F32)],
        scratch_shapes=[pltpu.VMEM((D, IN_W), BF)],
        compiler_params=_cp(2),
        name="in_proj",
    )(x3, sc, sh, wt)


def _head_block_mask(rows_per, cols_per, n):
    r = np.arange(rows_per * n)[:, None] // rows_per
    c = np.arange(cols_per * n)[None, :] // cols_per
    return (r == c).astype(np.float32)


def _block_tril(n, c):
    i = np.arange(n)[:, None]
    j = np.arange(n)[None, :]
    return ((i // c == j // c) & (j <= i)).astype(np.float32)


def _gla_front(x_ref, wg_ref, bg_ref, L_ref):
    q = x_ref[:, 0:128] * (GLA_DK ** -0.5)
    k = x_ref[:, 128:256]
    v = x_ref[:, 256:512]
    ga = x_ref[:, 512:640]
    r = x_ref[:, 640:896]
    gate = _dot(ga.astype(BF), wg_ref[...]) + bg_ref[...]
    la = _log_sigmoid(gate) * (LOG2E / GLA_GATE_TEMP)
    n = L_ref.shape[0]
    g = jnp.concatenate([_dot_3x(L_ref[...], la[i:i + n, :]) for i in range(0, la.shape[0], n)], axis=0)
    return q, k, v, r, g


def _gla_intra(q, g, kp_ref, gp_ref, vp_ref, E_ref, c):
    TT = q.shape[0]
    PAD = kp_ref.shape[0] - TT
    pos = lax.broadcasted_iota(jnp.int32, (TT, 1), 0) & (c - 1)
    o = jnp.zeros((TT, 2 * LANE), F32)
    for s in range(min(c, 8)):
        ks = kp_ref[pl.ds(PAD - s, TT), :]
        gs = gp_ref[pl.ds(PAD - s, TT), :]
        vs = vp_ref[pl.ds(PAD - s, TT), :]
        w = jnp.where(pos >= s, q * ks * jnp.exp2(g - gs), 0.0)
        o = o + _dot(w.astype(BF), E_ref[...]) * vs
    if c <= 8:
        return o
    assert c == 16
    nc = TT // c

    def upper(x):
        return x.reshape(nc, 2, 8, x.shape[-1])[:, 1].reshape(nc * 8, x.shape[-1])

    qu, gu = upper(q), upper(g)
    posu = lax.broadcasted_iota(jnp.int32, (nc * 8, 1), 0) & 7
    ou = jnp.zeros((nc * 8, 2 * LANE), F32)
    for s in range(8, c):
        ks = upper(kp_ref[pl.ds(PAD - s, TT), :])
        gs = upper(gp_ref[pl.ds(PAD - s, TT), :])
        vs = upper(vp_ref[pl.ds(PAD - s, TT), :])
        w = jnp.where(posu >= s - 8, qu * ks * jnp.exp2(gu - gs), 0.0)
        ou = ou + _dot(w.astype(BF), E_ref[...]) * vs
    ou = ou.reshape(nc, 1, 8, 2 * LANE)
    return o + jnp.concatenate([jnp.zeros_like(ou), ou], axis=1).reshape(TT, 2 * LANE)


def _gla_norm_gate(o, r, nw_ref, EA_ref):
    ms = _dot_x2(o * o, EA_ref[...])
    return o * lax.rsqrt(ms + EPS) * nw_ref[...] * _silu(r)


def _gla_prompt_kernel(x_ref, wg_ref, bg_ref, nw_ref, L_ref, E_ref, EA_ref, M_ref,
                       o_ref, sfin_ref, st_ref, kp_ref, gp_ref, vp_ref, oi_ref, u_ref, sb_ref, *, c):
    TT = x_ref.shape[0]
    nc = TT // c
    PAD = kp_ref.shape[0] - TT
    t = pl.program_id(1)

    @pl.when(t == 0)
    def _():
        st_ref[...] = jnp.zeros_like(st_ref)

    q, k, v, r, g = _gla_front(x_ref, wg_ref, bg_ref, L_ref)
    kp_ref[0:PAD, :] = jnp.zeros((PAD, LANE), F32)
    gp_ref[0:PAD, :] = jnp.zeros((PAD, LANE), F32)
    vp_ref[0:PAD, :] = jnp.zeros((PAD, 2 * LANE), F32)
    kp_ref[PAD:PAD + TT, :] = k
    gp_ref[PAD:PAD + TT, :] = g
    vp_ref[PAD:PAD + TT, :] = v
    o = _gla_intra(q, g, kp_ref, gp_ref, vp_ref, E_ref, c)

    M = M_ref[...]
    gl_all = gp_ref[pl.ds(PAD + c - 1, nc, stride=c), :]
    for n in range(nc):
        lo = n * c
        ke = (k[lo:lo + c, :] * jnp.exp2(gl_all[n:n + 1, :] - g[lo:lo + c, :])).astype(BF)
        u_ref[n] = _dot_tn(ke, v[lo:lo + c, :].astype(BF)) * M
    assert nc <= LANE
    a_rows = jnp.exp2(gl_all)
    if nc < LANE:
        a_rows = jnp.concatenate([a_rows, jnp.zeros((LANE - nc, LANE), F32)], axis=0)
    a_cols = a_rows.T
    S = st_ref[...]
    for n in range(nc):
        sb_ref[n] = S.astype(BF)
        S = a_cols[:, n:n + 1] * S + u_ref[n]
    st_ref[...] = S
    qe = (q * jnp.exp2(g)).astype(BF)
    for n in range(nc):
        lo = n * c
        oi_ref[lo:lo + c, :] = _dot(qe[lo:lo + c, :], sb_ref[n])
    o = o + oi_ref[...]
    o_ref[...] = _gla_norm_gate(o, r, nw_ref, EA_ref).astype(o_ref.dtype)

    @pl.when(t == pl.num_programs(1) - 1)
    def _():
        for h in range(GLA_H):
            sfin_ref[0, h] = S[h * GLA_DK:(h + 1) * GLA_DK, h * GLA_DV:(h + 1) * GLA_DV]


def _gla_tables(TT, c):
    L = jnp.asarray(_block_tril(TT, c), BF)
    E = jnp.asarray(_head_block_mask(GLA_DK, GLA_DV, GLA_H), BF)
    EA = jnp.asarray(_head_block_mask(GLA_DV, GLA_DV, GLA_H) / GLA_DV, BF)
    M = jnp.asarray(_head_block_mask(GLA_DK, GLA_DV, GLA_H), F32)
    return L, E, EA, M


def _gla_params(w_gate, b_gate, norm_w):
    wg = jnp.zeros((LANE, GLA_H * GLA_DK), F32).at[:GLA_GATE_RANK].set(w_gate).astype(BF)
    return wg, b_gate.reshape(1, -1), norm_w.reshape(1, -1)


def _gla_prompt_call(gin, B, T, w_gate, b_gate, norm_w):
    TT, c = next(tt for tt in (2048, 1024, 512) if T % tt == 0), GLA_CHUNK
    nT = T // TT
    L, E, EA, M = _gla_tables(min(TT, 256), c)
    wg, bg, nw = _gla_params(w_gate, b_gate, norm_w)
    PAD = 16
    return pl.pallas_call(
        functools.partial(_gla_prompt_kernel, c=c),
        grid=(B, nT),
        in_specs=[pl.BlockSpec((TT, GLA_IN_W), lambda b, t: (b * nT + t, 0)),
                  _const(wg.shape), _const(bg.shape), _const(nw.shape),
                  _const(L.shape), _const(E.shape), _const(EA.shape), _const(M.shape)],
        out_specs=[pl.BlockSpec((TT, GLA_WIDTH), lambda b, t: (b * nT + t, 0)),
                   pl.BlockSpec((1, GLA_H, GLA_DK, GLA_DV), lambda b, t: (b, 0, 0, 0))],
        out_shape=[jax.ShapeDtypeStruct((B * T, GLA_WIDTH), BF),
                   jax.ShapeDtypeStruct((B, GLA_H, GLA_DK, GLA_DV), F32)],
        scratch_shapes=[pltpu.VMEM((GLA_H * GLA_DK, GLA_H * GLA_DV), F32),
                        pltpu.VMEM((TT + PAD, LANE), F32),
                        pltpu.VMEM((TT + PAD, LANE), F32),
                        pltpu.VMEM((TT + PAD, 2 * LANE), F32),
                        pltpu.VMEM((TT, 2 * LANE), F32),
                        pltpu.VMEM((TT // c, GLA_H * GLA_DK, GLA_H * GLA_DV), F32),
                        pltpu.VMEM((TT // c, GLA_H * GLA_DK, GLA_H * GLA_DV), BF)],
        compiler_params=_cp(2),
        name="gla_prompt",
    )(gin, wg, bg, nw, L, E, EA, M)


def _rope(x, cos, sin_signed):
    lane = lax.broadcasted_iota(jnp.int32, (1, LANE), 1)
    first_half = (lane & (RET_DK - 1)) < RET_DK // 2
    out = []
    for p in range(2):
        xs = x[:, p * LANE:(p + 1) * LANE]
        up = pltpu.roll(xs, LANE - RET_DK // 2, 1)
        dn = pltpu.roll(xs, RET_DK // 2, 1)
        out.append(xs * cos + jnp.where(first_half, up, dn) * sin_signed)
    return jnp.concatenate(out, axis=1)


def _ret_front(x_ref, cos_ref, sin_ref, rows=slice(None)):
    cos, sin = cos_ref[rows, :], sin_ref[rows, :]
    q = _rope(x_ref[rows, 0:256], cos, sin)
    k = _rope(x_ref[rows, 256:512], cos, sin) * (RET_DK ** -0.5)
    v = x_ref[rows, 512:768]
    rg = x_ref[rows, 768:1024]
    return q, k, v, rg


def _ret_intra(q, k, v, D_ref):
    lane = lax.broadcasted_iota(jnp.int32, (1, RET_WIDTH), 1)
    qb, kb, vb = q.astype(BF), k.astype(BF), v.astype(BF)
    zero = jnp.zeros((), BF)
    o = jnp.zeros(q.shape, F32)
    for h in range(RET_H):
        hm = (lane // RET_DK) == h
        s = _dot_nt(jnp.where(hm, qb, zero), kb)
        p = (s * D_ref[h]).astype(BF)
        o = o + _dot(p, jnp.where(hm, vb, zero))
    return o


def _ret_norm_gate(o, rg, nw_ref, EA_ref):
    mu = _dot_x2(o, EA_ref[...])
    d = o - mu
    var = _dot_x2(d * d, EA_ref[...])
    return d * lax.rsqrt(var + EPS) * nw_ref[...] * _silu(rg)


def _ret_prompt_kernel(x_ref, cos_ref, sin_ref, D_ref, rd_ref, kd_ref, G_ref, M_ref, EA_ref, nw_ref,
                       o_ref, sfin_ref, st_ref):
    t = pl.program_id(1)

    @pl.when(t == 0)
    def _():
        st_ref[...] = jnp.zeros_like(st_ref)

    C = D_ref.shape[1]
    S = st_ref[...]
    for i in range(x_ref.shape[0] // C):
        rows = slice(i * C, (i + 1) * C)
        q, k, v, rg = _ret_front(x_ref, cos_ref, sin_ref, rows)
        o = _ret_intra(q, k, v, D_ref)
        o = o + _dot((q * rd_ref[...]).astype(BF), S.astype(BF))
        u = _dot_tn((k * kd_ref[...]).astype(BF), v.astype(BF))
        S = S * G_ref[...] + u * M_ref[...]
        o_ref[rows, :] = _ret_norm_gate(o, rg, nw_ref, EA_ref).astype(o_ref.dtype)
    st_ref[...] = S

    @pl.when(t == pl.num_programs(1) - 1)
    def _():
        for h in range(RET_H):
            sfin_ref[0, h] = S[h * RET_DK:(h + 1) * RET_DK, h * RET_DV:(h + 1) * RET_DV]


def _rope_tables(pos):
    half = RET_DK // 2
    inv = ROPE_BASE ** (-jnp.arange(half, dtype=F32) / half)
    ang = pos.astype(F32)[:, None] * inv[None, :]
    cos, sin = jnp.cos(ang), jnp.sin(ang)
    return jnp.tile(jnp.concatenate([cos, cos], 1), (1, 2)), jnp.tile(jnp.concatenate([-sin, sin], 1), (1, 2))


def _ret_log_gamma():
    return np.log(1.0 - 2.0 ** (-5.0 - np.arange(RET_H, dtype=np.float64)))


def _ret_prompt_call(rin, B, T, norm_w):
    C = 256
    TT = next(n * C for n in (8, 4, 1) if T % (n * C) == 0)
    nT = T // TT
    cos, sin = _rope_tables(jnp.arange(T, dtype=jnp.int32))
    lg = _ret_log_gamma()
    i = np.arange(C)
    dec = np.exp(lg[:, None, None] * (i[:, None] - i[None, :])[None]) * (i[:, None] >= i[None, :])[None]
    Dm = jnp.asarray(dec, F32)
    rd = jnp.asarray(np.repeat(np.exp(lg[None, :] * (i[:, None] + 1)), RET_DK, 1), F32)
    kd = jnp.asarray(np.repeat(np.exp(lg[None, :] * (C - 1 - i[:, None])), RET_DK, 1), F32)
    M = _head_block_mask(RET_DK, RET_DV, RET_H)
    G = jnp.asarray(M * np.repeat(np.exp(lg * C), RET_DK)[:, None], F32)
    M = jnp.asarray(M, F32)
    EA = jnp.asarray(_head_block_mask(RET_DV, RET_DV, RET_H) / RET_DV, BF)
    nw = norm_w.reshape(1, -1)
    return pl.pallas_call(
        _ret_prompt_kernel,
        grid=(B, nT),
        in_specs=[pl.BlockSpec((TT, RET_IN_W), lambda b, t: (b * nT + t, 0)),
                  pl.BlockSpec((TT, LANE), lambda b, t: (t, 0)),
                  pl.BlockSpec((TT, LANE), lambda b, t: (t, 0)),
                  _const(Dm.shape), _const(rd.shape), _const(kd.shape), _const(G.shape), _const(M.shape),
                  _const(EA.shape), _const(nw.shape)],
        out_specs=[pl.BlockSpec((TT, RET_WIDTH), lambda b, t: (b * nT + t, 0)),
                   pl.BlockSpec((1, RET_H, RET_DK, RET_DV), lambda b, t: (b, 0, 0, 0))],
        out_shape=[jax.ShapeDtypeStruct((B * T, RET_WIDTH), BF),
                   jax.ShapeDtypeStruct((B, RET_H, RET_DK, RET_DV), F32)],
        scratch_shapes=[pltpu.VMEM((RET_H * RET_DK, RET_H * RET_DV), F32)],
        compiler_params=_cp(2),
        name="ret_prompt",
    )(rin, cos, sin, Dm, rd, kd, G, M, EA, nw)


def _ssd_conv(xp_ref, cw_ref, cb_ref, TT):
    acc = cb_ref[...] + cw_ref[SSD_CONV_W - 1:SSD_CONV_W, :] * xp_ref[pl.ds(8, TT), :]
    for i in range(SSD_CONV_W - 1):
        acc = acc + cw_ref[i:i + 1, :] * xp_ref[pl.ds(8 - (SSD_CONV_W - 1) + i, TT), :]
    return acc


def _ssd_intra(xs, bm, cm, g, dt, Mk_ref):
    TT = xs.shape[0]
    g2 = g * LOG2E
    rT = ((g - jnp.log(dt)) * LOG2E).T
    lane = lax.broadcasted_iota(jnp.int32, (1, LANE), 1)
    lane2 = lax.broadcasted_iota(jnp.int32, (1, 2 * LANE), 1)
    zero = jnp.zeros((), BF)
    causal = Mk_ref[...].astype(BF) > zero
    bmb = bm.astype(BF)
    o_parts = []
    for grp in range(SSD_G):
        cb = _dot_nt(jnp.where((lane // SSD_N) == grp, cm, 0.0).astype(BF), bmb).astype(BF)
        xg = xs[:, grp * 2 * LANE:(grp + 1) * 2 * LANE].astype(BF)
        og = jnp.zeros((TT, 2 * LANE), F32)
        for h4 in range(SSD_H // SSD_G):
            h = grp * (SSD_H // SSD_G) + h4
            dec = jnp.exp2(g2[:, h:h + 1] - rT[h:h + 1, :]).astype(BF)
            p = jnp.where(causal, cb * dec, zero)
            og = og + _dot(p, jnp.where((lane2 // SSD_P) == h4, xg, zero))
        o_parts.append(og)
    return jnp.concatenate(o_parts, axis=1)


def _ssd_prompt_kernel(x_ref, cw_ref, cb_ref, dtb_ref, alog_ref, dexp_ref, nw_ref, L_ref, Mk_ref, Eexp_ref, M2_ref,
                       res_ref, g1_ref, og_ref, or_ref, wo_ref, lg_ref, lb_ref,
                       o_ref, sfin_ref, cfin_ref, st_ref, xp_ref):
    TT = x_ref.shape[0]
    t = pl.program_id(1)

    @pl.when(t == 0)
    def _():
        st_ref[...] = jnp.zeros_like(st_ref)
        xp_ref[0:8, :] = jnp.zeros((8, SSD_CONV_DIM), F32)

    z = x_ref[:, 0:SSD_WIDTH]
    xp_ref[8:8 + TT, :] = x_ref[:, SSD_WIDTH:SSD_WIDTH + SSD_CONV_DIM]
    sdt = x_ref[:, SSD_WIDTH + SSD_CONV_DIM:SSD_IN_W]
    xbc = _silu(_ssd_conv(xp_ref, cw_ref, cb_ref, TT))
    tail = xp_ref[TT:TT + 8, :]
    xp_ref[0:8, :] = tail
    dt_all = _softplus(sdt + dtb_ref[...])
    la_all = dt_all * (-jnp.exp(alog_ref[...]))
    Eexp = Eexp_ref[...]
    C = L_ref.shape[0]
    S = st_ref[...]
    for i in range(TT // C):
        rows = slice(i * C, (i + 1) * C)
        xs = xbc[rows, 0:SSD_WIDTH]
        bm = xbc[rows, SSD_WIDTH:SSD_WIDTH + LANE]
        cm = xbc[rows, SSD_WIDTH + LANE:SSD_CONV_DIM]
        dt = dt_all[rows, :]
        g = _dot_3x(L_ref[...], la_all[rows, :])
        gl = g[C - 1:C, :]
        eg_x = _dot_x2(jnp.exp(g), Eexp)
        cw_x = _dot_x2(dt * jnp.exp(gl - g), Eexp)
        egl_x = _dot_x2(jnp.exp(gl), Eexp)

        o = _ssd_intra(xs, bm, cm, g, dt, Mk_ref)
        o = o + eg_x * _dot(cm.astype(BF), S.astype(BF))
        u = _dot_tn(bm.astype(BF), (xs * cw_x).astype(BF))
        S = S * egl_x + u * M2_ref[...]

        y = (o + dexp_ref[...] * xs) * _silu(z[rows, :])
        ms = jnp.mean(y * y, axis=-1, keepdims=True)
        o_ssd = (y * lax.rsqrt(ms + EPS) * nw_ref[...]).astype(BF)
        merged = jnp.concatenate([og_ref[rows, :], or_ref[rows, :], o_ssd], axis=1)
        mix = _dot(merged, wo_ref[...])
        o_ref[0, rows, :] = _layer_norm(ALPHA * res_ref[0, rows, :] + g1_ref[0] * mix, lg_ref[0], lb_ref[0])
    st_ref[...] = S

    @pl.when(t == pl.num_programs(1) - 1)
    def _():
        for h in range(SSD_H):
            gi = h // (SSD_H // SSD_G)
            sfin_ref[0, h] = S[gi * SSD_N:(gi + 1) * SSD_N, h * SSD_P:(h + 1) * SSD_P]
        cfin_ref[0] = tail[8 - (SSD_CONV_W - 1):8, :]


def _pad_lanes(v, n=LANE):
    v = v.reshape(1, -1)
    return jnp.zeros((1, n), F32).at[:, :v.shape[1]].set(v)


def _ssd_tables(TT, c):
    L = jnp.asarray(_block_tril(TT, c), BF)
    Mk = jnp.asarray(_block_tril(TT, c), F32)
    e = np.zeros((LANE, SSD_WIDTH), np.float32)
    for h in range(SSD_H):
        e[h, h * SSD_P:(h + 1) * SSD_P] = 1.0
    M2 = np.zeros((SSD_G * SSD_N, SSD_WIDTH), np.float32)
    for h in range(SSD_H):
        gi = h // (SSD_H // SSD_G)
        M2[gi * SSD_N:(gi + 1) * SSD_N, h * SSD_P:(h + 1) * SSD_P] = 1.0
    return L, Mk, jnp.asarray(e, BF), jnp.asarray(M2, F32)


def _ssd_params(conv_w, conv_b, dt_bias, a_log, d, norm_w):
    return (conv_w, conv_b.reshape(1, -1), _pad_lanes(dt_bias), _pad_lanes(a_log),
            jnp.repeat(d, SSD_P).reshape(1, -1), norm_w.reshape(1, -1))


def _ssd_prompt_call(sin_, x3, g1, og, orr, w_out, ln_g, ln_b, conv_w, conv_b, dt_bias, a_log, d, norm_w):
    B, T, D = x3.shape
    C = 256
    TT = 4 * C if T % (4 * C) == 0 else (2 * C if T % (2 * C) == 0 else C)
    nT = T // TT
    L, Mk, Eexp, M2 = _ssd_tables(C, C)
    prm = _ssd_params(conv_w, conv_b, dt_bias, a_log, d, norm_w)
    rmap = lambda b, t: (b * nT + t, 0)
    return pl.pallas_call(
        _ssd_prompt_kernel,
        grid=(B, nT),
        in_specs=[pl.BlockSpec((TT, SSD_IN_W), rmap)]
                 + [_const(p.shape) for p in prm]
                 + [_const(L.shape), _const(Mk.shape), _const(Eexp.shape), _const(M2.shape)]
                 + [pl.BlockSpec((1, TT, D), lambda b, t: (b, t, 0)),
                    pl.BlockSpec((1, 1, D), lambda b, t: (b, 0, 0)),
                    pl.BlockSpec((TT, GLA_WIDTH), rmap), pl.BlockSpec((TT, RET_WIDTH), rmap),
                    _const((D, D)), _const((1, 1, D)), _const((1, 1, D))],
        out_specs=[pl.BlockSpec((1, TT, D), lambda b, t: (b, t, 0)),
                   pl.BlockSpec((1, SSD_H, SSD_N, SSD_P), lambda b, t: (b, 0, 0, 0)),
                   pl.BlockSpec((1, SSD_CONV_W - 1, SSD_CONV_DIM), lambda b, t: (b, 0, 0))],
        out_shape=[jax.ShapeDtypeStruct((B, T, D), F32),
                   jax.ShapeDtypeStruct((B, SSD_H, SSD_N, SSD_P), F32),
                   jax.ShapeDtypeStruct((B, SSD_CONV_W - 1, SSD_CONV_DIM), F32)],
        scratch_shapes=[pltpu.VMEM((SSD_G * SSD_N, SSD_WIDTH), F32),
                        pltpu.VMEM((TT + 8, SSD_CONV_DIM), F32)],
        compiler_params=_cp(2),
        name="ssd_outproj_ln",
    )(sin_, *prm, L, Mk, Eexp, M2, x3, g1, og, orr, w_out, ln_g.reshape(1, 1, D), ln_b.reshape(1, 1, D))


def _inproj_t_kernel(x_ref, sc_ref, sh_ref, wt_ref, og_ref, or_ref, os_ref, w_ref):
    nt, nb, D = x_ref.shape

    @pl.when(pl.program_id(0) == 0)
    def _():
        for src, dst, n in ((0, 0, N_GA + GLA_GATE_RANK), (N_GA + GLA_GATE_RANK, N_GA + LANE, N_DT - N_GA - GLA_GATE_RANK)):
            for r in range(0, n, 512):
                m = min(512, n - r)
                w_ref[dst + r:dst + r + m, :] = wt_ref[src + r:src + r + m, :].astype(BF)
        w_ref[N_GA + GLA_GATE_RANK:N_GA + LANE, :] = jnp.zeros((LANE - GLA_GATE_RANK, D), BF)
        tail = jnp.concatenate([wt_ref[N_DT:N_IN, :], jnp.zeros((LANE - SSD_H, D), F32)], axis=0)
        w_ref[IN_W - LANE:IN_W, :] = tail.astype(BF)

    h = x_ref[...] * (1.0 + sc_ref[...]) + sh_ref[...]
    for t in range(0, nt, 2):
        ht = h[t:t + 2].reshape(2 * nb, D).astype(BF)
        cols = slice(t * nb, (t + 2) * nb)
        og_ref[:, cols] = _dot_nt(w_ref[0:GLA_IN_W, :], ht)
        or_ref[:, cols] = _dot_nt(w_ref[GLA_IN_W:GLA_IN_W + RET_IN_W, :], ht)
        os_ref[:, cols] = _dot_nt(w_ref[GLA_IN_W + RET_IN_W:IN_W, :], ht)


def _inproj_t_call(xt, sc, sh, wt, l):
    T, B, D = xt.shape
    nt = 4
    cmap = lambda i: (0, i)
    return pl.pallas_call(
        _inproj_t_kernel,
        grid=(T // nt,),
        in_specs=[pl.BlockSpec((nt, B, D), lambda i: (i, 0, 0)),
                  pl.BlockSpec((1, B, D), lambda i: (0, 0, 0)),
                  pl.BlockSpec((1, B, D), lambda i: (0, 0, 0)),
                  _resident_layer(wt.shape, l)],
        out_specs=[pl.BlockSpec((GLA_IN_W, nt * B), cmap),
                   pl.BlockSpec((RET_IN_W, nt * B), cmap),
                   pl.BlockSpec((SSD_IN_W, nt * B), cmap)],
        out_shape=[jax.ShapeDtypeStruct((GLA_IN_W, T * B), F32),
                   jax.ShapeDtypeStruct((RET_IN_W, T * B), F32),
                   jax.ShapeDtypeStruct((SSD_IN_W, T * B), F32)],
        scratch_shapes=[pltpu.VMEM((IN_W, D), BF)],
        compiler_params=_cp(1),
        name="in_proj_t",
    )(xt, sc, sh, wt)


def _row_sum(x):
    return jnp.sum(x, axis=0, keepdims=True)


def _lane_state_readout(o, coef_ref, s0_ref, n_rows):
    nb = LANE
    half = len(o) // 2
    for part in range(2):
        def body(k8, accs, part=part):
            accs = list(accs)
            base = pl.multiple_of(k8 * 8, 8)
            grp = [coef_ref[pl.ds(base, 8), (part * half + i) * nb:(part * half + i + 1) * nb] for i in range(half)]
            for j in range(8):
                s0k = s0_ref[0, k8 * 8 + j]
                for i in range(half):
                    accs[i] = accs[i] + grp[i][j:j + 1, :] * s0k
            return tuple(accs)

        res = lax.fori_loop(0, n_rows // 8, body, tuple(o[part * half:(part + 1) * half]))
        o[part * half:(part + 1) * half] = list(res)
    return o


def _lane_state_update(sn_ref, s0_ref, decay_fn, coef_ref, val_fn, n_rows, T):
    nb = LANE

    def body(k8, carry):
        base = pl.multiple_of(k8 * 8, 8)
        grp = [coef_ref[pl.ds(base, 8), t * nb:(t + 1) * nb] for t in range(T)]
        dec = decay_fn(base)
        for j in range(8):
            dj = dec[j:j + 1, :] if dec.shape[0] == 8 else dec
            sk = dj * s0_ref[0, k8 * 8 + j]
            for t in range(T):
                sk = sk + grp[t][j:j + 1, :] * val_fn(t)
            sn_ref[0, 0, k8 * 8 + j] = sk
        return carry

    lax.fori_loop(0, n_rows // 8, body, 0)


def _state_specs(shape, l, first):
    assert l == 0 or not first
    tail = tuple(shape[2:])
    in_spec = pl.BlockSpec((None, 1) + tail, lambda h: (l, h, 0, 0, 0))
    out_spec = pl.BlockSpec(((shape[0] if first else 1), 1) + tail, lambda h: (l, h, 0, 0, 0))
    return in_spec, out_spec


def _zero_later_layers(ref):
    ref[1:] = jnp.zeros((ref.shape[0] - 1,) + tuple(ref.shape[1:]), ref.dtype)


def _finish_state_call(kern, n_in, first, prevs):
    if first:
        return functools.partial(kern, first=True), [], {}
    wrapped = lambda *a, **kw: kern(*a[:n_in], *a[n_in + len(prevs):], first=False, **kw)
    specs = [pl.BlockSpec(memory_space=pl.ANY)] * len(prevs)
    return wrapped, specs, {n_in + i: 1 + i for i in range(len(prevs))}


def _gla_t_kernel(x_ref, s0_ref, wg_ref, bg_ref, nw_ref, o_ref, sn_ref, qe_ref, ke_ref, a_ref, *, T, first):
    nb = LANE
    if first:
        _zero_later_layers(sn_ref)
    h = pl.program_id(0)
    r0 = pl.multiple_of(h * GLA_DK, GLA_DK)
    v0 = pl.multiple_of(h * GLA_DV, GLA_DV)
    q = x_ref[pl.ds(r0, GLA_DK), :] * (GLA_DK ** -0.5)
    k = x_ref[pl.ds(128 + r0, GLA_DK), :]
    gate = _dot(wg_ref[pl.ds(r0, GLA_DK), :], x_ref[512:640, :].astype(BF)) + bg_ref[pl.ds(r0, GLA_DK), :]
    la = _log_sigmoid(gate) * (1.0 / GLA_GATE_TEMP)
    gs = []
    acc = jnp.zeros((GLA_DK, nb), F32)
    for t in range(T):
        acc = acc + la[:, t * nb:(t + 1) * nb]
        gs.append(acc)
    gl = gs[T - 1]
    a_ref[...] = jnp.exp(gl)
    qs = [q[:, t * nb:(t + 1) * nb] for t in range(T)]
    ks = [k[:, t * nb:(t + 1) * nb] for t in range(T)]
    for t in range(T):
        qe_ref[:, t * nb:(t + 1) * nb] = qs[t] * jnp.exp(gs[t])
        ke_ref[:, t * nb:(t + 1) * nb] = ks[t] * jnp.exp(gl - gs[t])

    def vt(t):
        return x_ref[pl.ds(256 + v0, GLA_DV), t * nb:(t + 1) * nb]

    o = []
    for t in range(T):
        ot = jnp.zeros((GLA_DV, nb), F32)
        for u in range(t + 1):
            s = _row_sum(qs[t] * ks[u] * jnp.exp(gs[t] - gs[u]))
            ot = ot + s * vt(u)
        o.append(ot)

    o = _lane_state_readout(o, qe_ref, s0_ref, GLA_DK)
    _lane_state_update(sn_ref, s0_ref, lambda base: a_ref[pl.ds(base, 8), :], ke_ref, vt, GLA_DK, T)

    nw = nw_ref[pl.ds(v0, GLA_DV), :]
    for t in range(T):
        ms = jnp.mean(o[t] * o[t], axis=0, keepdims=True)
        r = x_ref[pl.ds(640 + v0, GLA_DV), t * nb:(t + 1) * nb]
        o_ref[:, t * nb:(t + 1) * nb] = (o[t] * lax.rsqrt(ms + EPS) * nw * _silu(r)).astype(o_ref.dtype)


def _gla_t_call(gT, s0, prev, l, T, w_gate, b_gate, norm_w):
    N = gT.shape[1]
    wg = jnp.zeros((GLA_H * GLA_DK, LANE), F32).at[:, :GLA_GATE_RANK].set(w_gate.T).astype(BF)
    bg = b_gate.reshape(-1, 1)
    nw = norm_w.reshape(-1, 1)
    first = prev is None
    prevs = [] if first else [prev]
    s_in, s_out = _state_specs(s0.shape, l, first)
    ins = [gT, s0, wg, bg, nw]
    specs = [_const(gT.shape), s_in, _const(wg.shape), _const(bg.shape), _const(nw.shape)]
    kern, pspecs, aliases = _finish_state_call(functools.partial(_gla_t_kernel, T=T), len(ins), first, prevs)
    ins, specs = ins + prevs, specs + pspecs
    return pl.pallas_call(
        kern,
        grid=(GLA_H,),
        in_specs=specs,
        out_specs=[pl.BlockSpec((GLA_DV, N), lambda h: (h, 0)), s_out],
        out_shape=[jax.ShapeDtypeStruct((GLA_WIDTH, N), BF), jax.ShapeDtypeStruct(s0.shape, F32)],
        scratch_shapes=[pltpu.VMEM((GLA_DK, N), F32), pltpu.VMEM((GLA_DK, N), F32), pltpu.VMEM((GLA_DK, LANE), F32)],
        input_output_aliases=aliases,
        compiler_params=_cp(1),
        name="gla_t",
    )(*ins)


def _ret_t_kernel(x_ref, s0_ref, cos_ref, sin_ref, pw_ref, nw_ref, o_ref, sn_ref, qd_ref, kd_ref, *, T, first):
    nb = LANE
    if first:
        _zero_later_layers(sn_ref)
    h = pl.program_id(0)
    r0 = pl.multiple_of(h * RET_DK, RET_DK)
    half_k = RET_DK // 2
    cos, sin = cos_ref[...], sin_ref[...]

    def rope_t(base):
        x1 = x_ref[pl.ds(base + r0, half_k), :]
        x2 = x_ref[pl.ds(base + r0 + half_k, half_k), :]
        return jnp.concatenate([x1 * cos - x2 * sin, x1 * sin + x2 * cos], axis=0)

    q = rope_t(0)
    k = rope_t(256) * (RET_DK ** -0.5)
    pw = pw_ref[h]
    qs = [q[:, t * nb:(t + 1) * nb] for t in range(T)]
    ks = [k[:, t * nb:(t + 1) * nb] for t in range(T)]
    for t in range(T):
        qd_ref[:, t * nb:(t + 1) * nb] = qs[t] * pw[t + 1:t + 2, :]
        kd_ref[:, t * nb:(t + 1) * nb] = ks[t] * pw[T - 1 - t:T - t, :]

    def vt(t):
        return x_ref[pl.ds(512 + r0, RET_DV), t * nb:(t + 1) * nb]

    o = []
    for t in range(T):
        ot = jnp.zeros((RET_DV, nb), F32)
        for u in range(t + 1):
            s = _row_sum(qs[t] * ks[u]) * pw[t - u:t - u + 1, :]
            ot = ot + s * vt(u)
        o.append(ot)

    o = _lane_state_readout(o, qd_ref, s0_ref, RET_DK)
    _lane_state_update(sn_ref, s0_ref, lambda base: pw[T:T + 1, :], kd_ref, vt, RET_DK, T)

    nw = nw_ref[pl.ds(r0, RET_DV), :]
    for t in range(T):
        mu = jnp.mean(o[t], axis=0, keepdims=True)
        d = o[t] - mu
        var = jnp.mean(d * d, axis=0, keepdims=True)
        rg = x_ref[pl.ds(768 + r0, RET_DV), t * nb:(t + 1) * nb]
        o_ref[:, t * nb:(t + 1) * nb] = (d * lax.rsqrt(var + EPS) * nw * _silu(rg)).astype(o_ref.dtype)


def _ret_t_call(rT, s0, prev, l, T, norm_w):
    N = rT.shape[1]
    B = N // T
    half = RET_DK // 2
    inv = ROPE_BASE ** (-jnp.arange(half, dtype=F32) / half)
    ang = inv[:, None] * (PAST_LEN + jnp.arange(T, dtype=jnp.int32)).astype(F32)[None, :]
    cos = jnp.repeat(jnp.cos(ang), B, axis=1)
    sin = jnp.repeat(jnp.sin(ang), B, axis=1)
    lg = _ret_log_gamma()
    pw = jnp.asarray(np.repeat(np.exp(lg[:, None] * np.arange(16)[None, :])[:, :, None], LANE, axis=2), F32)
    nw = norm_w.reshape(-1, 1)
    first = prev is None
    prevs = [] if first else [prev]
    s_in, s_out = _state_specs(s0.shape, l, first)
    ins = [rT, s0, cos, sin, pw, nw]
    specs = [_const(rT.shape), s_in, _const(cos.shape), _const(sin.shape), _const(pw.shape), _const(nw.shape)]
    kern, pspecs, aliases = _finish_state_call(functools.partial(_ret_t_kernel, T=T), len(ins), first, prevs)
    ins, specs = ins + prevs, specs + pspecs
    return pl.pallas_call(
        kern,
        grid=(RET_H,),
        in_specs=specs,
        out_specs=[pl.BlockSpec((RET_DV, N), lambda h: (h, 0)), s_out],
        out_shape=[jax.ShapeDtypeStruct((RET_WIDTH, N), BF), jax.ShapeDtypeStruct(s0.shape, F32)],
        scratch_shapes=[pltpu.VMEM((RET_DK, N), F32), pltpu.VMEM((RET_DK, N), F32)],
        input_output_aliases=aliases,
        compiler_params=_cp(1),
        name="ret_t",
    )(*ins)


def _ssd_t_kernel(x_ref, c0_ref, s0_ref, cw_ref, cb_ref, dtb_ref, alog_ref, d_ref, nw_ref,
                  o_ref, sn_ref, cn_ref, hist_ref, y_ref, ssq_ref, cm_ref, bw_ref, xw_ref, *, T, first):
    nb = LANE
    W1 = SSD_CONV_W - 1
    h = pl.program_id(0)
    XB = SSD_WIDTH
    if first:
        _zero_later_layers(sn_ref)

    @pl.when(h == 0)
    def _():
        ssq_ref[...] = jnp.zeros_like(ssq_ref)
        if first:
            _zero_later_layers(cn_ref)
        for i in range(W1):
            for j in range(SSD_CONV_DIM // LANE):
                hist_ref[j * LANE:(j + 1) * LANE, i * nb:(i + 1) * nb] = c0_ref[0, i][:, j * LANE:(j + 1) * LANE].T
                cn_ref[0, i, :, j * LANE:(j + 1) * LANE] = \
                    x_ref[XB + j * LANE:XB + (j + 1) * LANE, (T - W1 + i) * nb:(T - W1 + i + 1) * nb].T

    def conv_rows(ro):
        w = cw_ref[pl.ds(ro, 64), :]
        b = cb_ref[pl.ds(ro, 64), :]
        xx = [hist_ref[pl.ds(ro, 64), i * nb:(i + 1) * nb] for i in range(W1)]
        xx += [x_ref[pl.ds(XB + ro, 64), t * nb:(t + 1) * nb] for t in range(T)]
        out = []
        for t in range(T):
            acc = b + w[:, 0:1] * xx[t]
            for i in range(1, SSD_CONV_W):
                acc = acc + w[:, i:i + 1] * xx[t + i]
            out.append(_silu(acc))
        return out

    grp = h // (SSD_H // SSD_G)
    xs = conv_rows(pl.multiple_of(h * SSD_P, SSD_P))
    bm = conv_rows(pl.multiple_of(SSD_WIDTH + grp * SSD_N, SSD_N))
    cm = conv_rows(pl.multiple_of(SSD_WIDTH + SSD_G * SSD_N + grp * SSD_N, SSD_N))

    dt_all = _softplus(x_ref[pl.ds(XB + SSD_CONV_DIM + h, 1), :] + dtb_ref[pl.ds(h, 1), :])
    a = -jnp.exp(alog_ref[pl.ds(h, 1), :])
    dts = [dt_all[:, t * nb:(t + 1) * nb] for t in range(T)]
    gs = []
    acc = jnp.zeros((1, nb), F32)
    for t in range(T):
        acc = acc + dts[t] * a
        gs.append(acc)
    gl = gs[T - 1]

    o = []
    for t in range(T):
        ot = jnp.zeros((SSD_P, nb), F32)
        for u in range(t + 1):
            s = _row_sum(cm[t] * bm[u]) * (jnp.exp(gs[t] - gs[u]) * dts[u])
            ot = ot + s * xs[u]
        o.append(ot)

    for t in range(T):
        cm_ref[:, t * nb:(t + 1) * nb] = cm[t] * jnp.exp(gs[t])
        bw_ref[:, t * nb:(t + 1) * nb] = bm[t]
        xw_ref[:, t * nb:(t + 1) * nb] = xs[t] * (dts[t] * jnp.exp(gl - gs[t]))

    o = _lane_state_readout(o, cm_ref, s0_ref, SSD_N)
    egl = jnp.exp(gl)
    _lane_state_update(sn_ref, s0_ref, lambda base: egl, bw_ref, lambda t: xw_ref[:, t * nb:(t + 1) * nb], SSD_N, T)

    dd = d_ref[pl.ds(h, 1), :]
    p0 = pl.multiple_of(h * SSD_P, SSD_P)
    for t in range(T):
        z = x_ref[pl.ds(p0, SSD_P), t * nb:(t + 1) * nb]
        y = (o[t] + dd * xs[t]) * _silu(z)
        y_ref[pl.ds(p0, SSD_P), t * nb:(t + 1) * nb] = y
        ssq_ref[:, t * nb:(t + 1) * nb] += _row_sum(y * y)

    @pl.when(h == SSD_H - 1)
    def _():
        scale = lax.rsqrt(ssq_ref[...] * (1.0 / SSD_WIDTH) + EPS)
        o_ref[...] = (y_ref[...] * scale * nw_ref[...]).astype(o_ref.dtype)


def _ssd_t_call(sT, c0, s0, prev_s, prev_c, l, T, conv_w, conv_b, dt_bias, a_log, d, norm_w):
    N = sT.shape[1]
    col = lambda v: jnp.zeros((LANE, 1), F32).at[:SSD_H, 0].set(v)
    prm = (conv_w.T, conv_b.reshape(-1, 1), col(dt_bias), col(a_log), col(d), norm_w.reshape(-1, 1))
    first = prev_s is None
    prevs = [] if first else [prev_s, prev_c]
    s_in, s_out = _state_specs(s0.shape, l, first)
    c_in = pl.BlockSpec((1,) + tuple(c0.shape[1:]), lambda h: (l, 0, 0, 0))
    c_out = pl.BlockSpec(((c0.shape[0] if first else 1),) + tuple(c0.shape[1:]), lambda h: (l, 0, 0, 0))
    ins = [sT, c0, s0, *prm]
    specs = [_const(sT.shape), c_in, s_in] + [_const(p.shape) for p in prm]
    kern, pspecs, aliases = _finish_state_call(functools.partial(_ssd_t_kernel, T=T), len(ins), first, prevs)
    ins, specs = ins + prevs, specs + pspecs
    return pl.pallas_call(
        kern,
        grid=(SSD_H,),
        in_specs=specs,
        out_specs=[_const((SSD_WIDTH, N)), s_out, c_out],
        out_shape=[jax.ShapeDtypeStruct((SSD_WIDTH, N), BF), jax.ShapeDtypeStruct(s0.shape, F32),
                   jax.ShapeDtypeStruct(c0.shape, F32)],
        scratch_shapes=[pltpu.VMEM((SSD_CONV_DIM, (SSD_CONV_W - 1) * LANE), F32),
                        pltpu.VMEM((SSD_WIDTH, N), F32), pltpu.VMEM((1, N), F32),
                        pltpu.VMEM((SSD_N, N), F32), pltpu.VMEM((SSD_N, N), F32), pltpu.VMEM((SSD_P, N), F32)],
        input_output_aliases=aliases,
        compiler_params=_cp(1),
        name="ssd_t",
    )(*ins)


def _outproj_t_kernel(x_ref, g_ref, og_ref, or_ref, os_ref, w_ref, lg_ref, lb_ref, o_ref):
    nt, nb, D = x_ref.shape
    for t in range(nt):
        cols = slice(t * nb, (t + 1) * nb)
        mix = (_dot_tn(og_ref[:, cols], w_ref[0:GLA_WIDTH, :])
               + _dot_tn(or_ref[:, cols], w_ref[GLA_WIDTH:GLA_WIDTH + RET_WIDTH, :])
               + _dot_tn(os_ref[:, cols], w_ref[GLA_WIDTH + RET_WIDTH:D, :]))
        y = ALPHA * x_ref[t] + g_ref[0] * mix
        o_ref[t] = _layer_norm(y, lg_ref[0], lb_ref[0])


def _outproj_t_call(xt, g1, ogT, orT, osT, w_out, ln_g, ln_b):
    T, B, D = xt.shape
    nt = 4
    cmap = lambda i: (0, i)
    return pl.pallas_call(
        _outproj_t_kernel,
        grid=(T // nt,),
        in_specs=[pl.BlockSpec((nt, B, D), lambda i: (i, 0, 0)),
                  pl.BlockSpec((1, B, D), lambda i: (0, 0, 0)),
                  pl.BlockSpec((GLA_WIDTH, nt * B), cmap),
                  pl.BlockSpec((RET_WIDTH, nt * B), cmap),
                  pl.BlockSpec((SSD_WIDTH, nt * B), cmap),
                  _const((D, D)), _const((1, 1, D)), _const((1, 1, D))],
        out_specs=pl.BlockSpec((nt, B, D), lambda i: (i, 0, 0)),
        out_shape=jax.ShapeDtypeStruct((T, B, D), F32),
        compiler_params=_cp(1),
        name="out_proj_ln_t",
    )(xt, g1, ogT, orT, osT, w_out, ln_g.reshape(1, 1, D), ln_b.reshape(1, 1, D))


ROUTE_OFF = 8


def _moe_route_t(lt):
    neg = jnp.float32(-jnp.inf)
    row8 = lax.broadcasted_iota(jnp.int32, (8, 1), 0)
    lg = jnp.where(row8 < MOE_GROUPS, lt[0:8, :], neg)
    mg = jnp.max(lg, axis=0, keepdims=True)
    gsel = jnp.min(jnp.where(lg == mg, row8, 8), axis=0, keepdims=True)
    g_gate = 1.0 / jnp.sum(jnp.exp(lg - mg), axis=0, keepdims=True)
    rowe = lax.broadcasted_iota(jnp.int32, (MOE_EXPERTS, 1), 0)
    le = jnp.where((rowe // MOE_PER_GROUP) == gsel, lt[ROUTE_OFF:ROUTE_OFF + MOE_EXPERTS, :], neg)
    m1 = jnp.max(le, axis=0, keepdims=True)
    i1 = jnp.min(jnp.where(le == m1, rowe, MOE_EXPERTS), axis=0, keepdims=True)
    le2 = jnp.where(rowe == i1, neg, le)
    m2 = jnp.max(le2, axis=0, keepdims=True)
    i2 = jnp.min(jnp.where(le2 == m2, rowe, MOE_EXPERTS), axis=0, keepdims=True)
    e2 = jnp.exp(m2 - m1)
    w1 = g_gate / (1.0 + e2)
    w2 = g_gate * e2 / (1.0 + e2)
    comb = jnp.where(rowe == i1, w1, jnp.where(rowe == i2, w2, 0.0))
    cg = comb[0:4, :]
    for g in range(1, MOE_GROUPS):
        cg = cg + comb[g * MOE_PER_GROUP:(g + 1) * MOE_PER_GROUP, :]
    return gsel, cg, comb


MOE_SUB = 256
MOE_BLK = 16
MOE_ROWS = 256
MOE_NPS = MOE_SUB + MOE_GROUPS * MOE_BLK
assert MOE_SUB <= MOE_ROWS


def _moe_kernel(x_ref, sc_ref, sh_ref, g_ref, wr_ref, br_ref, us_ref, w1_ref, w3_ref, w2_ref, lg_ref, lb_ref,
                o_ref, hb_ref, cwb_ref, hp_ref, cwp_ref, yp_ref, pos_ref,
                cgrp_ref, fill_ref, cur_ref, na_ref, nb_ref, so_ref, nfa_ref, dsa_ref, dsb_ref, *, n_steps):
    bB, bT, D = x_ref.shape
    R = bB * bT
    n_q = R // MOE_SUB
    s = pl.program_id(1)
    x = x_ref[...]
    row8 = lax.broadcasted_iota(jnp.int32, (8, 1), 0)
    slot = lax.broadcasted_iota(jnp.int32, (MOE_NPS, 1), 0).astype(F32)

    @pl.when((pl.program_id(0) == 0) & (s == 0))
    def _():
        hb_ref[...] = jnp.zeros_like(hb_ref)
        cwb_ref[...] = jnp.zeros_like(cwb_ref)
        yp_ref[...] = jnp.zeros_like(yp_ref)

    @pl.when(s == 0)
    def _():
        na_ref[0] = 0
        for g in range(MOE_GROUPS):
            cur_ref[g] = -1
            fill_ref[g] = 0

    @pl.when(s < n_steps)
    def _():
        h = (x * (1.0 + sc_ref[...]) + sh_ref[...]).reshape(R, D)
        segs, offs = [], []
        for q in range(n_q):
            u = s * n_q + q
            hq = h[q * MOE_SUB:(q + 1) * MOE_SUB, :].astype(BF)
            gsel, cg, _ = _moe_route_t(_dot_nt(wr_ref[...], hq) + br_ref[...])
            onehot = jnp.where(row8 == gsel, 1.0, 0.0)
            rank = _dot(onehot.astype(BF), us_ref[...])
            cnt = jnp.sum(onehot, axis=1, keepdims=True)
            seg = jnp.ceil(cnt * (1.0 / MOE_BLK)) * MOE_BLK
            off = jnp.zeros((8, 1), F32)
            for g in range(1, MOE_GROUPS):
                off = off + jnp.where(row8 >= g, seg[g - 1:g, :], 0.0)
            pos = jnp.sum(onehot * (off + rank), axis=0, keepdims=True)
            pos_ref[u] = jnp.broadcast_to(pos, (8, MOE_SUB))
            perm = jnp.where(slot == pos, 1.0, 0.0).astype(BF)
            hp_ref[q] = _dot(perm, hq).astype(BF)
            cg8 = jnp.concatenate([cg, jnp.zeros((4, MOE_SUB), F32)], axis=0)
            cg_hi = cg8.astype(BF)
            cg_lo = (cg8 - cg_hi.astype(F32)).astype(BF)
            cwp_ref[q] = _dot_nt(perm, cg_hi) + _dot_nt(perm, cg_lo)
            segs.append(seg)
            offs.append(off)
        for q in range(n_q):
            u = s * n_q + q
            for g in range(MOE_GROUPS):
                so = offs[q][g, 0].astype(jnp.int32)
                nb = (segs[q][g, 0] * (1.0 / MOE_BLK)).astype(jnp.int32)
                f = fill_ref[g]
                c = cur_ref[g]
                na = na_ref[0]
                room = jnp.where(c < 0, 0, (MOE_ROWS - f) // MOE_BLK)
                n_a = jnp.minimum(nb, room)
                n_b = nb - n_a
                base_a = c * MOE_ROWS + f
                base_b = na * MOE_ROWS
                idx = u * MOE_GROUPS + g
                so_ref[idx] = so
                nb_ref[idx] = nb
                nfa_ref[idx] = n_a
                dsa_ref[idx] = base_a
                dsb_ref[idx] = base_b

                def put(k, carry, so=so, q=q, n_a=n_a, base_a=base_a, base_b=base_b):
                    dst = pl.multiple_of(jnp.where(k < n_a, base_a + k * MOE_BLK, base_b + (k - n_a) * MOE_BLK), MOE_BLK)
                    src = pl.multiple_of(so + k * MOE_BLK, MOE_BLK)
                    hb_ref[pl.ds(dst, MOE_BLK), :] = hp_ref[q, pl.ds(src, MOE_BLK), :]
                    cwb_ref[pl.ds(dst, MOE_BLK), :] = cwp_ref[q, pl.ds(src, MOE_BLK), :]
                    return carry

                lax.fori_loop(0, nb, put, 0)

                @pl.when(n_b > 0)
                def _(g=g, na=na, n_b=n_b):
                    cgrp_ref[na] = g
                    na_ref[0] = na + 1
                    cur_ref[g] = na
                    fill_ref[g] = n_b * MOE_BLK

                @pl.when(n_b == 0)
                def _(g=g, f=f, n_a=n_a):
                    fill_ref[g] = f + n_a * MOE_BLK

    @pl.when(s == n_steps - 1)
    def _():
        def chunk(c, carry):
            g = cgrp_ref[c]
            start = pl.multiple_of(c * MOE_ROWS, MOE_ROWS)
            hc = hb_ref[pl.ds(start, MOE_ROWS), :]
            cw = cwb_ref[pl.ds(start, MOE_ROWS), :]
            hids = []
            for j in range(MOE_PER_GROUP):
                e = g * MOE_PER_GROUP + j
                hid = _silu(_dot(hc, w1_ref[e])) * _dot(hc, w3_ref[e]) * cw[:, j:j + 1]
                hids.append(hid.astype(BF))
            w2g = w2_ref[pl.ds(g * MOE_PER_GROUP, MOE_PER_GROUP)].reshape(MOE_PER_GROUP * MOE_FF, D)
            hb_ref[pl.ds(start, MOE_ROWS), :] = _dot(jnp.concatenate(hids, axis=1), w2g).astype(BF)
            return carry

        lax.fori_loop(0, na_ref[0], chunk, 0)

    @pl.when(s >= n_steps)
    def _():
        for q in range(n_q):
            u = (s - n_steps) * n_q + q
            for g in range(MOE_GROUPS):
                idx = u * MOE_GROUPS + g
                so, n_a, base_a, base_b = so_ref[idx], nfa_ref[idx], dsa_ref[idx], dsb_ref[idx]

                def take(k, carry, so=so, q=q, n_a=n_a, base_a=base_a, base_b=base_b):
                    src = pl.multiple_of(jnp.where(k < n_a, base_a + k * MOE_BLK, base_b + (k - n_a) * MOE_BLK), MOE_BLK)
                    dst = pl.multiple_of(so + k * MOE_BLK, MOE_BLK)
                    yp_ref[q, pl.ds(dst, MOE_BLK), :] = hb_ref[pl.ds(src, MOE_BLK), :]
                    return carry

                lax.fori_loop(0, nb_ref[idx], take, 0)
        ys = []
        for q in range(n_q):
            u = (s - n_steps) * n_q + q
            perm = jnp.where(slot == pos_ref[u][0:1, :], 1.0, 0.0).astype(BF)
            ys.append(_dot_tn(perm, yp_ref[q]))
        y = jnp.concatenate(ys, axis=0)
        z = ALPHA * x + g_ref[...] * y.reshape(bB, bT, D)
        o_ref[...] = _layer_norm(z, lg_ref[...], lb_ref[...])


def _moe_call(x3, sc, sh, g2, wr, br, w1, w3, w2, l, ln_g, ln_b):
    B, T, D = x3.shape
    bB, bT = _tok_tiles(B, T)
    R = bB * bT
    if bB == 1:
        spp = 2 if B % 2 == 0 else 1
        nT = T // bT
        n_pools, n_steps = B // spp, spp * nT
        xmap = lambda p, s: (p * spp + (s % n_steps) // nT, (s % n_steps) % nT, 0)
        omap = lambda p, s: (p * spp + jnp.maximum(s - n_steps, 0) // nT, jnp.maximum(s - n_steps, 0) % nT, 0)
        mmap = lambda p, s: (p * spp + (s % n_steps) // nT, 0, 0)
        mshape = (1, 1, D)
    else:
        n_pools, n_steps = 1, B // bB
        xmap = lambda p, s: (s % n_steps, 0, 0)
        omap = lambda p, s: (jnp.maximum(s - n_steps, 0), 0, 0)
        mmap = lambda p, s: (0, 0, 0)
        mshape = (1, bT, D)
    n_sub = n_steps * (R // MOE_SUB)
    n_chunks = pl.cdiv(n_sub * (MOE_SUB + MOE_GROUPS * (MOE_BLK - 1)), MOE_ROWS) + MOE_GROUPS
    us = jnp.asarray(np.triu(np.ones((MOE_SUB, MOE_SUB), np.float32), 1), BF)
    smem = lambda n: pltpu.SMEM((n,), jnp.int32)
    return pl.pallas_call(
        functools.partial(_moe_kernel, n_steps=n_steps),
        grid=(n_pools, 2 * n_steps),
        in_specs=[pl.BlockSpec((bB, bT, D), xmap),
                  pl.BlockSpec(mshape, mmap), pl.BlockSpec(mshape, mmap), pl.BlockSpec(mshape, mmap),
                  _const(wr.shape), _const(br.shape), _const(us.shape),
                  _resident_layer(w1.shape, l), _resident_layer(w3.shape, l), _resident_layer(w2.shape, l),
                  _const((1, 1, D)), _const((1, 1, D))],
        out_specs=pl.BlockSpec((bB, bT, D), omap),
        out_shape=jax.ShapeDtypeStruct((B, T, D), F32),
        scratch_shapes=[pltpu.VMEM((n_chunks * MOE_ROWS, D), BF), pltpu.VMEM((n_chunks * MOE_ROWS, 8), F32),
                        pltpu.VMEM((R // MOE_SUB, MOE_NPS, D), BF), pltpu.VMEM((R // MOE_SUB, MOE_NPS, 8), F32),
                        pltpu.VMEM((R // MOE_SUB, MOE_NPS, D), BF),
                        pltpu.VMEM((n_sub, 8, MOE_SUB), F32),
                        smem(n_chunks), smem(MOE_GROUPS), smem(MOE_GROUPS), smem(1),
                        *[smem(n_sub * MOE_GROUPS) for _ in range(5)]],
        compiler_params=_cp(2),
        name="moe_ln",
    )(x3, sc, sh, g2, wr, br, us, w1, w3, w2, ln_g.reshape(1, 1, D), ln_b.reshape(1, 1, D))


def _router_params(w_group, b_group, w_expert, b_expert):
    wr = jnp.zeros((LANE, D_MODEL), F32).at[:MOE_GROUPS].set(w_group.T)
    wr = wr.at[ROUTE_OFF:ROUTE_OFF + MOE_EXPERTS].set(w_expert.T)
    br = jnp.zeros((LANE, 1), F32).at[:MOE_GROUPS, 0].set(b_group).at[ROUTE_OFF:ROUTE_OFF + MOE_EXPERTS, 0].set(b_expert)
    return wr.astype(BF), br


def kernel(x_prompt, x_sample, c_prompt, c_sample, state_gla, state_ret, state_ssd, state_conv, w_ada, b_ada, w_in, gla_w_gate, gla_b_gate, gla_norm, ret_norm, ssd_conv_w, ssd_conv_b, ssd_dt_bias, ssd_a_log, ssd_d, ssd_norm, w_out, ln1_g, ln1_b, moe_w_group, moe_b_group, moe_w_expert, moe_b_expert, moe_w1, moe_w3, moe_w2, ln2_g, ln2_b):
    Bp, Tp, D = x_prompt.shape
    Bs, Ts, _ = x_sample.shape
    w_in_t = jnp.swapaxes(w_in, 1, 2)
    w_out_b = w_out.astype(BF)
    w1_b, w3_b, w2_b = moe_w1.astype(BF), moe_w3.astype(BF), moe_w2.astype(BF)

    mod = _mod_call(jnp.concatenate([c_prompt, c_sample], axis=0), w_ada, b_ada)

    def moe(x, sc2, sh2, g2, l):
        wr, br = _router_params(moe_w_group[l], moe_b_group[l], moe_w_expert[l], moe_b_expert[l])
        return _moe_call(x, sc2, sh2, g2, wr, br, w1_b, w3_b, w2_b, l, ln2_g[l], ln2_b[l])

    x = x_prompt
    new = [[], [], [], []]
    for l in range(DEPTH):
        sh1, sc1, g1, sh2, sc2, g2 = (mod[l, :Bp, None, i * D:(i + 1) * D] for i in range(6))
        gin, rin, sin_ = _inproj_call(x, sc1, sh1, w_in_t, l)
        og, s_gla = _gla_prompt_call(gin, Bp, Tp, gla_w_gate[l], gla_b_gate[l], gla_norm[l])
        orr, s_ret = _ret_prompt_call(rin, Bp, Tp, ret_norm[l])
        x, s_ssd, s_conv = _ssd_prompt_call(sin_, x, g1, og, orr, w_out_b[l], ln1_g[l], ln1_b[l], ssd_conv_w[l],
                                            ssd_conv_b[l], ssd_dt_bias[l], ssd_a_log[l], ssd_d[l], ssd_norm[l])
        x = moe(x, sc2, sh2, g2, l)
        for acc, s in zip(new, (s_gla, s_ret, s_ssd, s_conv)):
            acc.append(s)
    y_p = x
    gla_p, ret_p, ssd_p, conv_p = (jnp.stack(a) for a in new)

    x = jnp.swapaxes(x_sample, 0, 1)
    sg = jnp.transpose(state_gla, (0, 2, 3, 4, 1))
    sr = jnp.transpose(state_ret, (0, 2, 3, 4, 1))
    ss = jnp.transpose(state_ssd, (0, 2, 3, 4, 1))
    cv = jnp.transpose(state_conv, (0, 2, 1, 3))
    gla_n = ret_n = ssd_n = conv_n = None
    for l in range(DEPTH):
        sh1, sc1, g1, sh2, sc2, g2 = (mod[l, None, Bp:, i * D:(i + 1) * D] for i in range(6))
        gT, rT, sT = _inproj_t_call(x, sc1, sh1, w_in_t, l)
        ogT, gla_n = _gla_t_call(gT, sg, gla_n, l, Ts, gla_w_gate[l], gla_b_gate[l], gla_norm[l])
        orT, ret_n = _ret_t_call(rT, sr, ret_n, l, Ts, ret_norm[l])
        osT, ssd_n, conv_n = _ssd_t_call(sT, cv, ss, ssd_n, conv_n, l, Ts, ssd_conv_w[l], ssd_conv_b[l],
                                         ssd_dt_bias[l], ssd_a_log[l], ssd_d[l], ssd_norm[l])
        x = _outproj_t_call(x, g1, ogT, orT, osT, w_out_b[l], ln1_g[l], ln1_b[l])
        x = moe(x, sc2, sh2, g2, l)
    y_s = jnp.swapaxes(x, 0, 1)
    gla_s = jnp.transpose(gla_n, (0, 4, 1, 2, 3))
    ret_s = jnp.transpose(ret_n, (0, 4, 1, 2, 3))
    ssd_s = jnp.transpose(ssd_n, (0, 4, 1, 2, 3))
    conv_s = jnp.transpose(conv_n, (0, 2, 1, 3))
    return (y_p, y_s, gla_p, ret_p, ssd_p, conv_p, gla_s, ret_s, ssd_s, conv_s)
```
